```python
import math
import jax
import jax.numpy as jnp
from jax import lax
import numpy as np

D_MODEL = 1024
BATCH = 8
SEQ = 2048
DEPTH = 4
DEC_BATCH = 128
DEC_SEQ = 4
PAST_LEN = 8192
PAGE_SIZE = 128

N_EVEN = (DEPTH + 1) // 2
N_ODD = DEPTH // 2
WINDOW = 128
ATTN_HEADS = 8
KV_HEADS = 2
HEAD_DIM = 64
ATTN_WIDTH = ATTN_HEADS * HEAD_DIM
KV_WIDTH = KV_HEADS * HEAD_DIM
ROT_DIM = HEAD_DIM // 4
ROPE_THETA = 500000.0
S5_GROUP = 16
S5_WIDTH = D_MODEL // 2
S5_GROUPS = S5_WIDTH // S5_GROUP
S5_STATE = 64
EVEN_IN = ATTN_WIDTH + 2 * KV_WIDTH + S5_WIDTH
EVEN_MIX = ATTN_WIDTH + S5_WIDTH
ML_HEADS = 8
ML_DV = D_MODEL // ML_HEADS
ML_DK = ML_DV // 2
ML_CHUNK = 64
ML_QK = ML_HEADS * ML_DK
ML_WIDTH = ML_HEADS * ML_DV
ODD_IN = 2 * ML_QK + 2 * ML_WIDTH + 2 * ML_HEADS
D_FF = -(-8 * D_MODEL // (3 * 256)) * 256
EPS = 1e-6

kernel_name = 'hybrid_swa_s5_mlstm_step'


def rmsnorm(x, g):
    xf = x.astype(jnp.float32)
    y = xf * lax.rsqrt(jnp.mean(xf * xf, axis=-1, keepdims=True) + EPS)
    return (y * g.astype(jnp.float32)).astype(x.dtype)


def rope_partial(x, pos):
    half = ROT_DIM // 2
    inv = jnp.power(jnp.float32(ROPE_THETA), -jnp.arange(half, dtype=jnp.float32) / half)
    ang = pos.astype(jnp.float32)[:, None] * inv[None, :]
    cos = jnp.cos(ang)[:, None, :]
    sin = jnp.sin(ang)[:, None, :]
    xr = x[..., :ROT_DIM].astype(jnp.float32)
    x1, x2 = xr[..., :half], xr[..., half:]
    rot = jnp.concatenate([x1 * cos - x2 * sin, x2 * cos + x1 * sin], axis=-1)
    return jnp.concatenate([rot.astype(x.dtype), x[..., ROT_DIM:]], axis=-1)


def sink_attention(q, k, v, mask, sinks):
    *lead, lq, n_heads, hd = q.shape
    grp = n_heads // KV_HEADS
    qg = q.reshape(*lead, lq, KV_HEADS, grp, hd)
    s = jnp.einsum('...qhgd,...khd->...hgqk', qg, k).astype(jnp.float32) * (hd ** -0.5)
    s = jnp.where(mask, s, -jnp.inf)
    sink = sinks.astype(jnp.float32).reshape(KV_HEADS, grp, 1)
    m = jnp.maximum(jnp.max(s, axis=-1), sink)
    p = jnp.exp(s - m[..., None])
    p = p / (jnp.sum(p, axis=-1) + jnp.exp(sink - m))[..., None]
    o = jnp.einsum('...hgqk,...khd->...qhgd', p.astype(v.dtype), v)
    return o.reshape(*lead, lq, n_heads * hd)


def banded_attention(q, k, v, sinks):
    bsz, L = q.shape[:2]
    nb = L // WINDOW
    qb = q.reshape(bsz, nb, WINDOW, ATTN_HEADS, HEAD_DIM)

    def band(t):
        tb = t.reshape(bsz, nb, WINDOW, KV_HEADS, HEAD_DIM)
        prev = jnp.concatenate([jnp.zeros_like(tb[:, :1]), tb[:, :-1]], axis=1)
        return jnp.concatenate([prev, tb], axis=2)

    kb, vb = band(k), band(v)
    rel = (jnp.arange(WINDOW)[:, None] + WINDOW) - jnp.arange(2 * WINDOW)[None, :]
    in_band = (rel >= 0) & (rel <= WINDOW)
    valid = (jnp.arange(nb)[:, None] > 0) | (jnp.arange(2 * WINDOW)[None, :] >= WINDOW)
    mask = (in_band[None] & valid[:, None, :])[:, None, None]
    o = sink_attention(qb, kb, vb, mask, sinks)
    return o.reshape(bsz, L, ATTN_WIDTH)


def s5_mixer(u, a_re, a_im, log_dt, b_re, b_im, c_re, c_im, d_skip, w_glu, b_glu, h0):
    bsz, L, _ = u.shape
    uf = u.astype(jnp.float32).reshape(bsz, L, S5_GROUPS, S5_GROUP)
    dt = jnp.exp(log_dt.astype(jnp.float32))
    ar, ai = a_re.astype(jnp.float32), a_im.astype(jnp.float32)
    mag = jnp.exp(ar * dt)
    lr, li = mag * jnp.cos(ai * dt), mag * jnp.sin(ai * dt)
    den = ar * ar + ai * ai
    cr = ((lr - 1.0) * ar + li * ai) / den
    ci = (li * ar - (lr - 1.0) * ai) / den
    br, bi = b_re.astype(jnp.float32), b_im.astype(jnp.float32)
    bbr = cr[..., None] * br - ci[..., None] * bi
    bbi = cr[..., None] * bi + ci[..., None] * br
    xr = jnp.einsum('blgh,gph->blgp', uf, bbr)
    xi = jnp.einsum('blgh,gph->blgp', uf, bbi)
    if h0 is not None:
        h0r, h0i = h0[0].astype(jnp.float32), h0[1].astype(jnp.float32)
        xr = xr.at[:, 0].add(lr * h0r - li * h0i)
        xi = xi.at[:, 0].add(lr * h0i + li * h0r)

    def combine(e1, e2):
        a1r, a1i, b1r, b1i = e1
        a2r, a2i, b2r, b2i = e2
        return (a1r * a2r - a1i * a2i, a1r * a2i + a1i * a2r,
                a2r * b1r - a2i * b1i + b2r, a2r * b1i + a2i * b1r + b2i)

    _, _, sr, si = lax.associative_scan(
        combine, (jnp.broadcast_to(lr, xr.shape), jnp.broadcast_to(li, xr.shape), xr, xi), axis=1)
    y = (jnp.einsum('blgp,ghp->blgh', sr, c_re.astype(jnp.float32))
         - jnp.einsum('blgp,ghp->blgh', si, c_im.astype(jnp.float32))
         + d_skip.astype(jnp.float32) * uf).reshape(bsz, L, S5_WIDTH)
    g = jax.nn.gelu(y, approximate=False)
    out = g * jax.nn.sigmoid(g @ w_glu.astype(jnp.float32) + b_glu.astype(jnp.float32))
    return out.astype(u.dtype), sr[:, -1], si[:, -1]


def even_mixer(h, pos, kv_window, ssm0, w_in, g_q, g_k, sinks, a_re, a_im, log_dt, b_re, b_im,
               c_re, c_im, d_skip, w_glu, b_glu, w_out):
    bsz, L, _ = h.shape
    proj = h @ w_in
    q, k, v, u = jnp.split(proj, [ATTN_WIDTH, ATTN_WIDTH + KV_WIDTH, ATTN_WIDTH + 2 * KV_WIDTH], axis=-1)
    q = rope_partial(rmsnorm(q.reshape(bsz, L, ATTN_HEADS, HEAD_DIM), g_q), pos)
    k = rope_partial(rmsnorm(k.reshape(bsz, L, KV_HEADS, HEAD_DIM), g_k), pos)
    v = v.reshape(bsz, L, KV_HEADS, HEAD_DIM)
    if kv_window is None:
        attn = banded_attention(q, k, v, sinks)
        k_all, v_all = k, v
    else:
        k_all = jnp.concatenate([kv_window[0].astype(k.dtype), k], axis=1)
        v_all = jnp.concatenate([kv_window[1].astype(v.dtype), v], axis=1)
        k_pos = pos[0] - WINDOW + jnp.arange(WINDOW + L)
        rel = pos[:, None] - k_pos[None, :]
        mask = (rel >= 0) & (rel <= WINDOW)
        attn = sink_attention(q, k_all, v_all, mask, sinks)
    ssm_out, sr, si = s5_mixer(u, a_re, a_im, log_dt, b_re, b_im, c_re, c_im, d_skip, w_glu, b_glu, ssm0)
    mixed = jnp.concatenate([attn, ssm_out.astype(attn.dtype)], axis=-1)
    return mixed @ w_out, k_all[:, -WINDOW:], v_all[:, -WINDOW:], sr, si


def mlstm_chunk(carry, inp):
    c, n, m = carry
    q, k, v, ig, lf = inp
    T = q.shape[-2]
    b = lax.cumsum(lf, axis=2)
    causal = jnp.tril(jnp.ones((T, T), dtype=bool))
    dmat = jnp.where(causal, b[..., :, None] - b[..., None, :] + ig[..., None, :], -jnp.inf)
    inter = b + m[..., None]
    m_row = jnp.maximum(inter, jnp.max(dmat, axis=-1))
    w_intra = jnp.exp(dmat - m_row[..., None])
    w_inter = jnp.exp(inter - m_row)
    qk = jnp.einsum('bhtd,bhsd->bhts', q, k) * w_intra
    num = w_inter[..., None] * jnp.einsum('bhtd,bhdv->bhtv', q, c) + jnp.einsum('bhts,bhsv->bhtv', qk, v)
    den = w_inter * jnp.einsum('bhtd,bhd->bht', q, n) + jnp.sum(qk, axis=-1)
    h = num / jnp.maximum(jnp.abs(den), jnp.exp(-m_row))[..., None]
    m_new = m_row[..., -1]
    wk = w_intra[..., -1, :]
    decay = w_inter[..., -1]
    c_new = decay[..., None, None] * c + jnp.einsum('bhs,bhsd,bhsv->bhdv', wk, k, v)
    n_new = decay[..., None] * n + jnp.einsum('bhs,bhsd->bhd', wk, k)
    return (c_new, n_new, m_new), h


def mlstm(q, k, v, ig, lf, c0, n0, m0):
    bsz, L = q.shape[:2]
    lc = ML_CHUNK if L % ML_CHUNK == 0 else L
    nc = L // lc

    def to_chunks(t):
        t = t.reshape(bsz, nc, lc, *t.shape[2:])
        return jnp.swapaxes(jnp.moveaxis(t, 1, 0), 2, 3)

    (c, n, m), h = lax.scan(mlstm_chunk, (c0, n0, m0),
                            (to_chunks(q), to_chunks(k), to_chunks(v), to_chunks(ig), to_chunks(lf)))
    h = jnp.moveaxis(jnp.swapaxes(h, 2, 3), 0, 1).reshape(bsz, L, ML_HEADS, ML_DV)
    return h, c, n, m


def odd_mixer(h, state0, w_in, b_i, b_f, g_out, w_out):
    bsz, L, _ = h.shape
    proj = (h @ w_in).astype(jnp.float32)
    q, k, v, o, ig, fg = jnp.split(
        proj, [ML_QK, 2 * ML_QK, 2 * ML_QK + ML_WIDTH, 2 * ML_QK + 2 * ML_WIDTH,
               2 * ML_QK + 2 * ML_WIDTH + ML_HEADS], axis=-1)
    q = q.reshape(bsz, L, ML_HEADS, ML_DK)
    k = k.reshape(bsz, L, ML_HEADS, ML_DK) * (ML_DK ** -0.5)
    v = v.reshape(bsz, L, ML_HEADS, ML_DV)
    ig = ig + b_i.astype(jnp.float32)
    lf = jax.nn.log_sigmoid(fg + b_f.astype(jnp.float32))
    if state0 is None:
        c0 = jnp.zeros((bsz, ML_HEADS, ML_DK, ML_DV), jnp.float32)
        n0 = jnp.zeros((bsz, ML_HEADS, ML_DK), jnp.float32)
        m0 = jnp.zeros((bsz, ML_HEADS), jnp.float32)
    else:
        c0, n0, m0 = (s.astype(jnp.float32) for s in state0)
    hh, c, n, m = mlstm(q, k, v, ig, lf, c0, n0, m0)
    hh = rmsnorm(hh, g_out.reshape(ML_HEADS, ML_DV))
    out = (hh.reshape(bsz, L, ML_WIDTH) * jax.nn.sigmoid(o)).astype(h.dtype)
    return out @ w_out, c, n, m


def swiglu(h, wg, wu, wd):
    return (jax.nn.silu(h @ wg) * (h @ wu)) @ wd


def setup_inputs(seed: int = 0) -> dict:
    key = jax.random.key(seed)
    ks = iter(jax.random.split(key, 48))

    def nrm(shape, scale=1.0):
        return jax.random.normal(next(ks), shape, jnp.float32) * scale

    a_im_base = jnp.pi * jnp.arange(S5_STATE, dtype=jnp.float32)
    return {
        'x_prompt': nrm((BATCH, SEQ, D_MODEL)),
        'x_sample': nrm((DEC_BATCH, DEC_SEQ, D_MODEL)),
        'cache_k': nrm((N_EVEN, DEC_BATCH, WINDOW, KV_HEADS, HEAD_DIM)),
        'cache_v': nrm((N_EVEN, DEC_BATCH, WINDOW, KV_HEADS, HEAD_DIM)),
        'state_ssm_re': nrm((N_EVEN, DEC_BATCH, S5_GROUPS, S5_STATE), 0.1),
        'state_ssm_im': nrm((N_EVEN, DEC_BATCH, S5_GROUPS, S5_STATE), 0.1),
        'state_mlstm_c': nrm((N_ODD, DEC_BATCH, ML_HEADS, ML_DK, ML_DV), 0.1),
        'state_mlstm_n': nrm((N_ODD, DEC_BATCH, ML_HEADS, ML_DK), 0.1),
        'state_mlstm_m': nrm((N_ODD, DEC_BATCH, ML_HEADS)),
        'norm_mix': 1.0 + nrm((DEPTH, D_MODEL), 0.01),
        'norm_ffn': 1.0 + nrm((DEPTH, D_MODEL), 0.01),
        'w_in_even': nrm((N_EVEN, D_MODEL, EVEN_IN), D_MODEL ** -0.5),
        'q_norm': 1.0 + nrm((N_EVEN, HEAD_DIM), 0.01),
        'k_norm': 1.0 + nrm((N_EVEN, HEAD_DIM), 0.01),
        'attn_sinks': nrm((N_EVEN, ATTN_HEADS), 0.5),
        's5_a_re': -0.5 + nrm((N_EVEN, S5_GROUPS, S5_STATE), 0.01),
        's5_a_im': a_im_base + nrm((N_EVEN, S5_GROUPS, S5_STATE), 0.01),
        's5_log_dt': jax.random.uniform(next(ks), (N_EVEN, S5_GROUPS, S5_STATE), jnp.float32,
                                        math.log(0.001), math.log(0.1)),
        's5_b_re': nrm((N_EVEN, S5_GROUPS, S5_STATE, S5_GROUP), (2 * S5_GROUP) ** -0.5),
        's5_b_im': nrm((N_EVEN, S5_GROUPS, S5_STATE, S5_GROUP), (2 * S5_GROUP) ** -0.5),
        's5_c_re': nrm((N_EVEN, S5_GROUPS, S5_GROUP, S5_STATE), S5_STATE ** -0.5),
        's5_c_im': nrm((N_EVEN, S5_GROUPS, S5_GROUP, S5_STATE), S5_STATE ** -0.5),
        's5_d': nrm((N_EVEN, S5_GROUPS, S5_GROUP), 0.5),
        's5_w_glu': nrm((N_EVEN, S5_WIDTH, S5_WIDTH), S5_WIDTH ** -0.5),
        's5_b_glu': nrm((N_EVEN, S5_WIDTH), 0.01),
        'w_out_even': nrm((N_EVEN, EVEN_MIX, D_MODEL), EVEN_MIX ** -0.5),
        'w_in_odd': nrm((N_ODD, D_MODEL, ODD_IN), D_MODEL ** -0.5),
        'ml_b_i': nrm((N_ODD, ML_HEADS), 0.1),
        'ml_b_f': jnp.linspace(3.0, 6.0, ML_HEADS, dtype=jnp.float32)[None] + nrm((N_ODD, ML_HEADS), 0.1),
        'ml_out_norm': 1.0 + nrm((N_ODD, ML_WIDTH), 0.01),
        'w_out_odd': nrm((N_ODD, ML_WIDTH, D_MODEL), ML_WIDTH ** -0.5),
        'w_gate': nrm((DEPTH, D_MODEL, D_FF), D_MODEL ** -0.5),
        'w_up': nrm((DEPTH, D_MODEL, D_FF), D_MODEL ** -0.5),
        'w_down': nrm((DEPTH, D_FF, D_MODEL), D_FF ** -0.5),
    }


def reference(x_prompt, x_sample, cache_k, cache_v, state_ssm_re, state_ssm_im, state_mlstm_c,
              state_mlstm_n, state_mlstm_m, norm_mix, norm_ffn, w_in_even, q_norm, k_norm, attn_sinks,
              s5_a_re, s5_a_im, s5_log_dt, s5_b_re, s5_b_im, s5_c_re, s5_c_im, s5_d, s5_w_glu, s5_b_glu,
              w_out_even, w_in_odd, ml_b_i, ml_b_f, ml_out_norm, w_out_odd, w_gate, w_up, w_down):
    pos_p = jnp.arange(x_prompt.shape[1])
    pos_s = PAST_LEN + jnp.arange(x_sample.shape[1])
    yp, ys = x_prompt, x_sample
    pk, pv, psr, psi, pc, pn, pm = [], [], [], [], [], [], []
    sk, sv, ssr, ssi, sc, sn, sm = [], [], [], [], [], [], []
    for layer in range(DEPTH):
        if layer % 2 == 0:
            e = layer // 2
            w = (w_in_even[e], q_norm[e], k_norm[e], attn_sinks[e], s5_a_re[e], s5_a_im[e], s5_log_dt[e],
                 s5_b_re[e], s5_b_im[e], s5_c_re[e], s5_c_im[e], s5_d[e], s5_w_glu[e], s5_b_glu[e],
                 w_out_even[e])
            out, k_new, v_new, sr, si = even_mixer(rmsnorm(yp, norm_mix[layer]), pos_p, None, None, *w)
            yp = yp + out
            pk.append(k_new); pv.append(v_new); psr.append(sr); psi.append(si)
            out, k_new, v_new, sr, si = even_mixer(rmsnorm(ys, norm_mix[layer]), pos_s,
                                                   (cache_k[e], cache_v[e]),
                                                   (state_ssm_re[e], state_ssm_im[e]), *w)
            ys = ys + out
            sk.append(k_new); sv.append(v_new); ssr.append(sr); ssi.append(si)
        else:
            o = layer // 2
            w = (w_in_odd[o], ml_b_i[o], ml_b_f[o], ml_out_norm[o], w_out_odd[o])
            out, c, n, m = odd_mixer(rmsnorm(yp, norm_mix[layer]), None, *w)
            yp = yp + out
            pc.append(c); pn.append(n); pm.append(m)
            out, c, n, m = odd_mixer(rmsnorm(ys, norm_mix[layer]),
                                     (state_mlstm_c[o], state_mlstm_n[o], state_mlstm_m[o]), *w)
            ys = ys + out
            sc.append(c); sn.append(n); sm.append(m)
        yp = yp + swiglu(rmsnorm(yp, norm_ffn[layer]), w_gate[layer], w_up[layer], w_down[layer])
        ys = ys + swiglu(rmsnorm(ys, norm_ffn[layer]), w_gate[layer], w_up[layer], w_down[layer])
    return (yp, ys,
            jnp.stack(pk), jnp.stack(pv), jnp.stack(psr), jnp.stack(psi),
            jnp.stack(pc), jnp.stack(pn), jnp.stack(pm),
            jnp.stack(sk), jnp.stack(sv), jnp.stack(ssr), jnp.stack(ssi),
            jnp.stack(sc), jnp.stack(sn), jnp.stack(sm))
```

```python
import functools
import math

import jax
import jax.numpy as jnp
from jax import lax
from jax.experimental import pallas as pl
from jax.experimental.pallas import tpu as pltpu

F32 = jnp.float32
BF16 = jnp.bfloat16

D_MODEL = 1024
DEPTH = 4
PAST_LEN = 8192
WINDOW = 128
ATTN_HEADS = 8
KV_HEADS = 2
HEAD_DIM = 64
ATTN_WIDTH = ATTN_HEADS * HEAD_DIM
KV_WIDTH = KV_HEADS * HEAD_DIM
ROT_DIM = HEAD_DIM // 4
ROPE_THETA = 500000.0
S5_GROUP = 16
S5_WIDTH = D_MODEL // 2
S5_GROUPS = S5_WIDTH // S5_GROUP
S5_STATE = 64
S5_FLAT = S5_GROUPS * S5_STATE
ML_HEADS = 8
ML_DV = D_MODEL // ML_HEADS
ML_DK = ML_DV // 2
ML_QK = ML_HEADS * ML_DK
ML_WIDTH = ML_HEADS * ML_DV
ODD_IN = 2 * ML_QK + 2 * ML_WIDTH + 2 * ML_HEADS
ODD_IN_PAD = 2 * ML_QK + 2 * ML_WIDTH + 128
D_FF = 2816
EPS = 1e-6

LANES = 128
SUBLANES = 8
ROW_TILE = 512
FF_TILE = 256
S5_CHUNK = 64
ML_CHUNK = 128
SAMPLE_BLOCK = 8
VMEM_LIMIT = 56 * 1024 * 1024

NEG_INF = float("-inf")


def _cparams(*sem):
    return pltpu.CompilerParams(dimension_semantics=sem, vmem_limit_bytes=VMEM_LIMIT)


def _const_spec(shape):
    zeros = (0,) * len(shape)
    return pl.BlockSpec(shape, lambda *_: zeros, pipeline_mode=pl.Buffered(1))


def _rms(x, g):
    ms = jnp.mean(x * x, axis=-1, keepdims=True)
    return x * lax.rsqrt(ms + EPS) * g


def _split3(a):
    a1 = a.astype(BF16)
    r1 = a - a1.astype(F32)
    a2 = r1.astype(BF16)
    a3 = (r1 - a2.astype(F32)).astype(BF16)
    return a1, a2, a3


def _log_sigmoid(x):
    return jnp.minimum(x, 0.0) - jnp.log1p(jnp.exp(-jnp.abs(x)))


def _sigmoid(x):
    return 1.0 / (1.0 + jnp.exp(-x))


def _norm_matmul_kernel(x_ref, g_ref, w_ref, o_ref):
    h = _rms(x_ref[...], g_ref[...]).astype(BF16)
    o_ref[...] = jnp.dot(h, w_ref[...], preferred_element_type=F32)


def _norm_matmul(x, g, w):
    n, d = x.shape
    m = w.shape[1]
    tm = min(ROW_TILE, n)
    return pl.pallas_call(
        _norm_matmul_kernel,
        grid=(n // tm,),
        in_specs=[pl.BlockSpec((tm, d), lambda i: (i, 0)), _const_spec((1, d)), _const_spec((d, m))],
        out_specs=pl.BlockSpec((tm, m), lambda i: (i, 0)),
        out_shape=jax.ShapeDtypeStruct((n, m), F32),
        compiler_params=_cparams("parallel"),
        name="norm_matmul",
    )(x, g, w)


def _mix_ffn_kernel(*refs, n_mix):
    x_ref = refs[0]
    a_refs = refs[1:1 + n_mix]
    wo_ref, g_ref, wg_ref, wu_ref, wd_ref, o_ref, act_ref = refs[1 + n_mix:]
    y = x_ref[...]
    off = 0
    for a_ref in a_refs:
        ka = a_ref.shape[1]
        y = y + jnp.dot(a_ref[...], wo_ref[off:off + ka, :], preferred_element_type=F32)
        off += ka
    h = _rms(y, g_ref[...]).astype(BF16)
    for f in range(D_FF // FF_TILE):
        cols = slice(f * FF_TILE, (f + 1) * FF_TILE)
        gate = jnp.dot(h, wg_ref[:, cols], preferred_element_type=F32)
        up = jnp.dot(h, wu_ref[:, cols], preferred_element_type=F32)
        act_ref[:, cols] = (gate * _sigmoid(gate) * up).astype(BF16)
    o_ref[...] = y + jnp.dot(act_ref[...], wd_ref[...], preferred_element_type=F32)


def _mix_ffn(x, mixes, w_out, g_ffn, wg, wu, wd):
    n, d = x.shape
    tm = min(ROW_TILE, n)
    row = lambda i: (i, 0)
    in_specs = [pl.BlockSpec((tm, d), row)]
    in_specs += [pl.BlockSpec((tm, a.shape[1]), row) for a in mixes]
    in_specs += [_const_spec(w_out.shape), _const_spec((1, d)), _const_spec(wg.shape), _const_spec(wu.shape),
                 _const_spec(wd.shape)]
    return pl.pallas_call(
        functools.partial(_mix_ffn_kernel, n_mix=len(mixes)),
        grid=(n // tm,),
        in_specs=in_specs,
        out_specs=pl.BlockSpec((tm, d), row),
        out_shape=jax.ShapeDtypeStruct((n, d), F32),
        scratch_shapes=[pltpu.VMEM((tm, D_FF), BF16)],
        compiler_params=_cparams("parallel"),
        name="mix_ffn",
    )(x, *mixes, w_out, g_ffn, wg, wu, wd)


def _head_ones():
    r = lax.broadcasted_iota(jnp.int32, (LANES, LANES), 0) // HEAD_DIM
    c = lax.broadcasted_iota(jnp.int32, (LANES, LANES), 1) // HEAD_DIM
    return jnp.where(r == c, 1.0, 0.0).astype(BF16)


def _qk_prep(x, g, ones, ct, sa, sb):
    x2 = x * x
    hi = x2.astype(BF16)
    lo = (x2 - hi.astype(F32)).astype(BF16)
    ss = jnp.dot(hi, ones, preferred_element_type=F32) + jnp.dot(lo, ones, preferred_element_type=F32)
    xn = x * lax.rsqrt(ss * (1.0 / HEAD_DIM) + EPS) * g
    return xn * ct + pltpu.roll(xn, LANES - ROT_DIM // 2, 1) * sa + pltpu.roll(xn, ROT_DIM // 2, 1) * sb


def _rope_tables(pos):
    half = ROT_DIM // 2
    inv = jnp.power(jnp.float32(ROPE_THETA), -jnp.arange(half, dtype=F32) / half)
    ang = pos.astype(F32)[:, None] * inv[None, :]
    cos, sin = jnp.cos(ang), jnp.sin(ang)
    n = pos.shape[0]
    one = jnp.ones((n, HEAD_DIM - ROT_DIM), F32)
    zero = jnp.zeros((n, HEAD_DIM - ROT_DIM), F32)
    z8 = jnp.zeros((n, half), F32)
    ct = jnp.concatenate([cos, cos, one], axis=1)
    sa = jnp.concatenate([-sin, z8, zero], axis=1)
    sb = jnp.concatenate([z8, sin, zero], axis=1)
    tile = lambda t: jnp.concatenate([t, t], axis=1)
    return tile(ct), tile(sa), tile(sb)


def _attn_prompt_kernel(q_ref, kv_ref, ct_ref, sa_ref, sb_ref, gq_ref, gk_ref, sink_ref,
                        o_ref, pk_ref, pv_ref, kprev, vprev, *, nb):
    i = pl.program_id(1)

    @pl.when(i == 0)
    def _():
        kprev[...] = jnp.zeros_like(kprev)
        vprev[...] = jnp.zeros_like(vprev)

    ones = _head_ones()
    ct, sa, sb = ct_ref[...], sa_ref[...], sb_ref[...]
    kv = kv_ref[0]
    kn = _qk_prep(kv[:, :KV_WIDTH], gk_ref[...], ones, ct, sa, sb)
    v = kv[:, KV_WIDTH:]
    qn = [_qk_prep(q_ref[0, :, c * LANES:(c + 1) * LANES], gq_ref[...], ones, ct, sa, sb)
          for c in range(ATTN_WIDTH // LANES)]
    kcat = jnp.concatenate([kprev[...], kn], axis=0).astype(BF16)
    vcat = jnp.concatenate([vprev[...], v], axis=0).astype(BF16)
    r = lax.broadcasted_iota(jnp.int32, (WINDOW, 2 * WINDOW), 0)
    c = lax.broadcasted_iota(jnp.int32, (WINDOW, 2 * WINDOW), 1)
    rel = r + WINDOW - c
    mask = (rel >= 0) & (rel <= WINDOW) & ((c >= WINDOW) | (i > 0))
    grp = ATTN_HEADS // KV_HEADS
    outs = []
    for h in range(KV_HEADS):
        kh = kcat[:, h * HEAD_DIM:(h + 1) * HEAD_DIM]
        vh = vcat[:, h * HEAD_DIM:(h + 1) * HEAD_DIM]
        qs = []
        for g in range(grp):
            head = h * grp + g
            lane0 = (head % 2) * HEAD_DIM
            qs.append(qn[head // 2][:, lane0:lane0 + HEAD_DIM])
        qs = (jnp.concatenate(qs, axis=0) * (HEAD_DIM ** -0.5)).astype(BF16)
        s = lax.dot_general(qs, kh, (((1,), (1,)), ((), ())), preferred_element_type=F32)
        ps = []
        for g in range(grp):
            sg = jnp.where(mask, s[g * WINDOW:(g + 1) * WINDOW], NEG_INF)
            sink = sink_ref[h * grp + g]
            m = jnp.maximum(jnp.max(sg, axis=-1, keepdims=True), sink)
            p = jnp.exp(sg - m)
            den = jnp.sum(p, axis=-1, keepdims=True) + jnp.exp(sink - m)
            ps.append((p / den).astype(BF16))
        o = jnp.dot(jnp.concatenate(ps, axis=0), vh, preferred_element_type=F32)
        outs += [o[g * WINDOW:(g + 1) * WINDOW] for g in range(grp)]
    o_ref[0] = jnp.concatenate(outs, axis=1).astype(BF16)
    kprev[...] = kn
    vprev[...] = v

    @pl.when(i == nb - 1)
    def _():
        pk_ref[0] = kn
        pv_ref[0] = v


def _attn_prompt(proj, tables, gq, gk, sinks):
    bsz, seq, _ = proj.shape
    nb = seq // WINDOW
    tab = pl.BlockSpec((WINDOW, LANES), lambda b, i: (i, 0))
    return pl.pallas_call(
        functools.partial(_attn_prompt_kernel, nb=nb),
        grid=(bsz, nb),
        in_specs=[pl.BlockSpec((1, WINDOW, ATTN_WIDTH), lambda b, i: (b, i, 0)),
                  pl.BlockSpec((1, WINDOW, 2 * KV_WIDTH), lambda b, i: (b, i, EVEN_KV_BLOCK)),
                  tab, tab, tab, _const_spec((1, LANES)), _const_spec((1, LANES)),
                  pl.BlockSpec(memory_space=pltpu.SMEM)],
        out_specs=[pl.BlockSpec((1, WINDOW, ATTN_WIDTH), lambda b, i: (b, i, 0)),
                   pl.BlockSpec((1, WINDOW, KV_WIDTH), lambda b, i: (b, 0, 0)),
                   pl.BlockSpec((1, WINDOW, KV_WIDTH), lambda b, i: (b, 0, 0))],
        out_shape=[jax.ShapeDtypeStruct((bsz, seq, ATTN_WIDTH), BF16),
                   jax.ShapeDtypeStruct((bsz, WINDOW, KV_WIDTH), F32),
                   jax.ShapeDtypeStruct((bsz, WINDOW, KV_WIDTH), F32)],
        scratch_shapes=[pltpu.VMEM((WINDOW, KV_WIDTH), F32), pltpu.VMEM((WINDOW, KV_WIDTH), F32)],
        compiler_params=_cparams("parallel", "arbitrary"),
        name="attn_prompt",
    )(proj, proj, *tables, gq, gk, sinks)


EVEN_U_BLOCK = ATTN_WIDTH // S5_WIDTH
EVEN_KV_BLOCK = (ATTN_WIDTH + S5_WIDTH) // (2 * KV_WIDTH)
KALL_ROWS = WINDOW + SUBLANES


def _attn_sample_kernel(q_ref, kv_ref, ck_ref, cv_ref, ct_ref, sa_ref, sb_ref, gq_ref, gk_ref, sink_ref,
                        o_ref, nk_ref, nv_ref, kall, vall, *, bs, t_new):
    ones = _head_ones()
    ct, sa, sb = ct_ref[...], sa_ref[...], sb_ref[...]
    kv = kv_ref[...]
    kn = _qk_prep(kv[:, :KV_WIDTH], gk_ref[...], ones, ct, sa, sb)
    v = kv[:, KV_WIDTH:]
    qn = jnp.concatenate(
        [_qk_prep(q_ref[:, c * LANES:(c + 1) * LANES], gq_ref[...], ones, ct, sa, sb)
         for c in range(ATTN_WIDTH // LANES)], axis=1) * (HEAD_DIM ** -0.5)
    grp = ATTN_HEADS // KV_HEADS
    rows = grp * t_new
    r = lax.broadcasted_iota(jnp.int32, (rows, KALL_ROWS), 0)
    c = lax.broadcasted_iota(jnp.int32, (rows, KALL_ROWS), 1)
    t = r % t_new
    mask = (c >= t) & (c <= t + WINDOW)
    rg = lax.broadcasted_iota(jnp.int32, (rows, 1), 0) // t_new
    pad = jnp.zeros((KALL_ROWS - WINDOW - t_new, KV_WIDTH), F32)
    for b in range(bs):
        rb = slice(b * t_new, (b + 1) * t_new)
        ck, cv = ck_ref[b], cv_ref[b]
        kall[0:WINDOW, :] = ck
        vall[0:WINDOW, :] = cv
        kall[WINDOW:KALL_ROWS, :] = jnp.concatenate([kn[rb], pad], axis=0)
        vall[WINDOW:KALL_ROWS, :] = jnp.concatenate([v[rb], pad], axis=0)
        nk_ref[b] = pltpu.roll(ck, WINDOW - t_new, 0)
        nv_ref[b] = pltpu.roll(cv, WINDOW - t_new, 0)
        nk_ref[b, WINDOW - t_new:WINDOW, :] = kn[rb]
        nv_ref[b, WINDOW - t_new:WINDOW, :] = v[rb]
        kb = kall[...].astype(BF16)
        vb = vall[...].astype(BF16)
        outs = []
        for h in range(KV_HEADS):
            kh = kb[:, h * HEAD_DIM:(h + 1) * HEAD_DIM]
            vh = vb[:, h * HEAD_DIM:(h + 1) * HEAD_DIM]
            qs = jnp.concatenate(
                [qn[rb, (h * grp + g) * HEAD_DIM:(h * grp + g + 1) * HEAD_DIM] for g in range(grp)],
                axis=0).astype(BF16)
            s = lax.dot_general(qs, kh, (((1,), (1,)), ((), ())), preferred_element_type=F32)
            s = jnp.where(mask, s, NEG_INF)
            sink = jnp.zeros((rows, 1), F32)
            for g in range(grp):
                sink = jnp.where(rg == g, sink_ref[h * grp + g], sink)
            m = jnp.maximum(jnp.max(s, axis=-1, keepdims=True), sink)
            p = jnp.exp(s - m)
            den = jnp.sum(p, axis=-1, keepdims=True) + jnp.exp(sink - m)
            o = jnp.dot((p / den).astype(BF16), vh, preferred_element_type=F32)
            outs += [o[g * t_new:(g + 1) * t_new] for g in range(grp)]
        o_ref[rb, :] = jnp.concatenate(outs, axis=1).astype(BF16)


def _attn_sample(proj, cache_k, cache_v, tables, gq, gk, sinks, t_new):
    n = proj.shape[0]
    bsz = n // t_new
    bs = SAMPLE_BLOCK
    rows = bs * t_new
    row = lambda i: (i, 0)
    cache = pl.BlockSpec((bs, WINDOW, KV_WIDTH), lambda i: (i, 0, 0))
    return pl.pallas_call(
        functools.partial(_attn_sample_kernel, bs=bs, t_new=t_new),
        grid=(bsz // bs,),
        in_specs=[pl.BlockSpec((rows, ATTN_WIDTH), row),
                  pl.BlockSpec((rows, 2 * KV_WIDTH), lambda i: (i, EVEN_KV_BLOCK)),
                  cache, cache,
                  _const_spec((rows, LANES)), _const_spec((rows, LANES)), _const_spec((rows, LANES)),
                  _const_spec((1, LANES)), _const_spec((1, LANES)),
                  pl.BlockSpec(memory_space=pltpu.SMEM)],
        out_specs=[pl.BlockSpec((rows, ATTN_WIDTH), row), cache, cache],
        out_shape=[jax.ShapeDtypeStruct((n, ATTN_WIDTH), BF16),
                   jax.ShapeDtypeStruct((bsz, WINDOW, KV_WIDTH), F32),
                   jax.ShapeDtypeStruct((bsz, WINDOW, KV_WIDTH), F32)],
        scratch_shapes=[pltpu.VMEM((KALL_ROWS, KV_WIDTH), F32), pltpu.VMEM((KALL_ROWS, KV_WIDTH), F32)],
        compiler_params=_cparams("parallel"),
        name="attn_sample",
    )(proj, proj, cache_k, cache_v, *tables, gq, gk, sinks)


S5_UCHUNKS = S5_WIDTH // LANES
S5_SUB = S5_FLAT // S5_UCHUNKS
S5_SCHUNKS = S5_FLAT // LANES


def _s5_tail(y, wglu_ref, bglu_ref):
    g = 0.5 * y * (1.0 + lax.erf(y * (2.0 ** -0.5)))
    z = jnp.dot(g.astype(BF16), wglu_ref[...], preferred_element_type=F32) + bglu_ref[...]
    return (g * _sigmoid(z)).astype(BF16)


def _s5_prompt_kernel(u_ref, wb_ref, wc_ref, lam_ref, d_ref, wglu_ref, bglu_ref,
                      o_ref, sr_ref, si_ref, xs, hst, *, nbatch, tc):
    rows = nbatch * tc

    @pl.when(pl.program_id(1) == 0)
    def _():
        hst[...] = jnp.zeros_like(hst)

    u = u_ref[...].reshape(rows, S5_WIDTH)
    ub = u.astype(BF16)
    for cc in range(S5_UCHUNKS):
        res = jnp.dot(ub[:, cc * LANES:(cc + 1) * LANES], wb_ref[cc], preferred_element_type=F32)
        for j in range(S5_SUB // LANES):
            xs[cc * 4 + j] = res[:, j * LANES:(j + 1) * LANES]
            xs[S5_SCHUNKS + cc * 4 + j] = res[:, S5_SUB + j * LANES:S5_SUB + (j + 1) * LANES]

    def step(t, carry):
        new_r, new_i = [], []
        for k in range(S5_SCHUNKS):
            hr, hi = carry[k], carry[S5_SCHUNKS + k]
            lr, li = lam_ref[k], lam_ref[S5_SCHUNKS + k]
            idx = pl.ds(t, nbatch, stride=tc)
            nr = lr * hr - li * hi + xs[k, idx, :]
            ni = lr * hi + li * hr + xs[S5_SCHUNKS + k, idx, :]
            xs[k, idx, :] = nr
            xs[S5_SCHUNKS + k, idx, :] = ni
            new_r.append(nr)
            new_i.append(ni)
        return tuple(new_r + new_i)

    fin = lax.fori_loop(0, tc, step, tuple(hst[k] for k in range(2 * S5_SCHUNKS)))
    for k in range(2 * S5_SCHUNKS):
        hst[k] = fin[k]
    sr_ref[...] = jnp.concatenate(fin[:S5_SCHUNKS], axis=1)
    si_ref[...] = jnp.concatenate(fin[S5_SCHUNKS:], axis=1)

    ys = []
    for cc in range(S5_UCHUNKS):
        s = jnp.concatenate([xs[cc * 4 + j] for j in range(4)]
                            + [xs[S5_SCHUNKS + cc * 4 + j] for j in range(4)], axis=1).astype(BF16)
        cols = slice(cc * LANES, (cc + 1) * LANES)
        ys.append(jnp.dot(s, wc_ref[cc], preferred_element_type=F32) + d_ref[:, cols] * u[:, cols])
    out = _s5_tail(jnp.concatenate(ys, axis=1), wglu_ref, bglu_ref)
    o_ref[...] = out.reshape(nbatch, tc, S5_WIDTH)


def _s5_prompt(proj, prm):
    bsz, seq, _ = proj.shape
    nbatch, tc = SUBLANES, S5_CHUNK
    st = pl.BlockSpec((nbatch, S5_FLAT), lambda b, c: (b, 0))
    return pl.pallas_call(
        functools.partial(_s5_prompt_kernel, nbatch=nbatch, tc=tc),
        grid=(bsz // nbatch, seq // tc),
        in_specs=[pl.BlockSpec((nbatch, tc, S5_WIDTH), lambda b, c: (b, c, EVEN_U_BLOCK)),
                  _const_spec(prm["wb"].shape), _const_spec(prm["wc"].shape), _const_spec(prm["lam8"].shape),
                  _const_spec((1, S5_WIDTH)), _const_spec((S5_WIDTH, S5_WIDTH)), _const_spec((1, S5_WIDTH))],
        out_specs=[pl.BlockSpec((nbatch, tc, S5_WIDTH), lambda b, c: (b, c, 0)), st, st],
        out_shape=[jax.ShapeDtypeStruct((bsz, seq, S5_WIDTH), BF16),
                   jax.ShapeDtypeStruct((bsz, S5_FLAT), F32), jax.ShapeDtypeStruct((bsz, S5_FLAT), F32)],
        scratch_shapes=[pltpu.VMEM((2 * S5_SCHUNKS, nbatch * tc, LANES), F32),
                        pltpu.VMEM((2 * S5_SCHUNKS, nbatch, LANES), F32)],
        compiler_params=_cparams("parallel", "arbitrary"),
        name="s5_prompt",
    )(proj, prm["wb"], prm["wc"], prm["lam8"], prm["d"], prm["wglu"], prm["bglu"])


def _s5_sample_kernel(u_ref, wb_ref, wc_ref, lr_ref, li_ref, d_ref, wglu_ref, bglu_ref, h0r_ref, h0i_ref,
                      o_ref, sr_ref, si_ref, xr, xi, *, nseq, t_new):
    u = u_ref[...]
    ub = u.astype(BF16)
    for cc in range(S5_UCHUNKS):
        res = jnp.dot(ub[:, cc * LANES:(cc + 1) * LANES], wb_ref[cc], preferred_element_type=F32)
        xr[:, cc * S5_SUB:(cc + 1) * S5_SUB] = res[:, :S5_SUB]
        xi[:, cc * S5_SUB:(cc + 1) * S5_SUB] = res[:, S5_SUB:]
    lr, li = lr_ref[...], li_ref[...]
    hr, hi = h0r_ref[...], h0i_ref[...]
    for t in range(t_new):
        rows = slice(t * nseq, (t + 1) * nseq)
        nr = lr * hr - li * hi + xr[rows, :]
        ni = lr * hi + li * hr + xi[rows, :]
        xr[rows, :] = nr
        xi[rows, :] = ni
        hr, hi = nr, ni
    sr_ref[...] = hr
    si_ref[...] = hi
    ys = []
    for cc in range(S5_UCHUNKS):
        sc = slice(cc * S5_SUB, (cc + 1) * S5_SUB)
        s = jnp.concatenate([xr[:, sc], xi[:, sc]], axis=1).astype(BF16)
        cols = slice(cc * LANES, (cc + 1) * LANES)
        ys.append(jnp.dot(s, wc_ref[cc], preferred_element_type=F32) + d_ref[:, cols] * u[:, cols])
    o_ref[...] = _s5_tail(jnp.concatenate(ys, axis=1), wglu_ref, bglu_ref)


def _s5_sample(u_tm, h0r, h0i, prm, t_new):
    n = u_tm.shape[0]
    nseq = n // t_new
    full = lambda a: _const_spec(a.shape)
    args = (u_tm, prm["wb"], prm["wc"], prm["lr"], prm["li"], prm["d"], prm["wglu"], prm["bglu"], h0r, h0i)
    return pl.pallas_call(
        functools.partial(_s5_sample_kernel, nseq=nseq, t_new=t_new),
        grid=(1,),
        in_specs=[full(a) for a in args],
        out_specs=[pl.BlockSpec((n, S5_WIDTH), lambda i: (0, 0)), pl.BlockSpec((nseq, S5_FLAT), lambda i: (0, 0)),
                   pl.BlockSpec((nseq, S5_FLAT), lambda i: (0, 0))],
        out_shape=[jax.ShapeDtypeStruct((n, S5_WIDTH), BF16),
                   jax.ShapeDtypeStruct((nseq, S5_FLAT), F32), jax.ShapeDtypeStruct((nseq, S5_FLAT), F32)],
        scratch_shapes=[pltpu.VMEM((n, S5_FLAT), F32), pltpu.VMEM((n, S5_FLAT), F32)],
        compiler_params=_cparams("arbitrary"),
        name="s5_sample",
    )(*args)


def _s5_params(a_re, a_im, log_dt, b_re, b_im, c_re, c_im, d_skip, w_glu, b_glu):
    dt = jnp.exp(log_dt)
    mag = jnp.exp(a_re * dt)
    lr, li = mag * jnp.cos(a_im * dt), mag * jnp.sin(a_im * dt)
    den = a_re * a_re + a_im * a_im
    cr = ((lr - 1.0) * a_re + li * a_im) / den
    ci = (li * a_re - (lr - 1.0) * a_im) / den
    bbr = cr[..., None] * b_re - ci[..., None] * b_im
    bbi = cr[..., None] * b_im + ci[..., None] * b_re
    gpc = LANES // S5_GROUP
    eye = jnp.eye(gpc, dtype=F32)

    def in_blocks(bb):
        bb = bb.reshape(S5_UCHUNKS, gpc, S5_STATE, S5_GROUP)
        return jnp.einsum("cgph,gk->cghkp", bb, eye).reshape(S5_UCHUNKS, LANES, S5_SUB)

    def out_blocks(cm):
        cm = cm.reshape(S5_UCHUNKS, gpc, S5_GROUP, S5_STATE)
        return jnp.einsum("cghp,gk->cgpkh", cm, eye).reshape(S5_UCHUNKS, S5_SUB, LANES)

    wb = jnp.concatenate([in_blocks(bbr), in_blocks(bbi)], axis=2).astype(BF16)
    wc = jnp.concatenate([out_blocks(c_re), -out_blocks(c_im)], axis=1).astype(BF16)
    lr_f, li_f = lr.reshape(1, S5_FLAT), li.reshape(1, S5_FLAT)
    lam = jnp.concatenate([lr_f.reshape(S5_SCHUNKS, 1, LANES), li_f.reshape(S5_SCHUNKS, 1, LANES)], axis=0)
    lam8 = jnp.broadcast_to(lam, (2 * S5_SCHUNKS, SUBLANES, LANES))
    return dict(wb=wb, wc=wc, lam8=lam8, lr=lr_f, li=li_f, d=d_skip.reshape(1, S5_WIDTH),
                wglu=w_glu.astype(BF16), bglu=b_glu.reshape(1, S5_WIDTH))


ODD_K_BLOCK = 1
ODD_V_BLOCK = (2 * ML_QK) // ML_WIDTH
ODD_O_BLOCK = ODD_V_BLOCK + 1
ODD_G_BLOCK = (2 * ML_QK + 2 * ML_WIDTH) // LANES
ML_AUG = 2 * ML_DV


def _head_out(h, o, gout):
    hn = h * lax.rsqrt(jnp.mean(h * h, axis=-1, keepdims=True) + EPS) * gout
    return (hn * _sigmoid(o)).astype(BF16)


def _mlstm_prompt_kernel(q_ref, k_ref, v_ref, o_ref, g_ref, bias_ref, gout_ref,
                         h_ref, c_ref, n_ref, m_ref, caug, mst, *, tc, nchunks):
    ci = pl.program_id(1)

    @pl.when(ci == 0)
    def _():
        caug[...] = jnp.zeros_like(caug)
        mst[...] = jnp.zeros_like(mst)

    g = g_ref[0] + bias_ref[...]
    lf = pltpu.roll(_log_sigmoid(g), LANES - ML_HEADS, 1)
    rt = lax.broadcasted_iota(jnp.int32, (tc, tc), 0)
    cs = lax.broadcasted_iota(jnp.int32, (tc, tc), 1)
    causal = cs <= rt
    tril = jnp.where(causal, 1.0, 0.0).astype(BF16)
    triu = jnp.where(rt <= cs, 1.0, 0.0).astype(BF16)
    bcol = sum(jnp.dot(tril, p, preferred_element_type=F32) for p in _split3(lf))
    brow = sum(jnp.dot(p, triu, preferred_element_type=F32) for p in _split3(lf.T))
    crow = g.T - brow
    ones_col = jnp.where(lax.broadcasted_iota(jnp.int32, (tc, ML_DV), 1) == 0, 1.0, 0.0)
    for hd in range(ML_HEADS):
        bc = bcol[:, hd:hd + 1]
        dmat = jnp.where(causal, bc + crow[hd:hd + 1, :], NEG_INF)
        m_prev = mst[hd:hd + 1, 0:1]
        inter = bc + m_prev
        m_row = jnp.maximum(inter, jnp.max(dmat, axis=-1, keepdims=True))
        w_intra = jnp.exp(dmat - m_row)
        w_inter = jnp.exp(inter - m_row)
        qh = q_ref[0, :, hd * ML_DK:(hd + 1) * ML_DK].astype(BF16)
        kh = k_ref[0, :, hd * ML_DK:(hd + 1) * ML_DK] * (ML_DK ** -0.5)
        qk = lax.dot_general(qh, kh.astype(BF16), (((1,), (1,)), ((), ())), preferred_element_type=F32) * w_intra
        vaug = jnp.concatenate([v_ref[0, :, hd * ML_DV:(hd + 1) * ML_DV], ones_col], axis=1).astype(BF16)
        cm = caug[hd]
        both = (w_inter * jnp.dot(qh, cm.astype(BF16), preferred_element_type=F32)
                + jnp.dot(qk.astype(BF16), vaug, preferred_element_type=F32))
        den = both[:, ML_DV:ML_DV + 1]
        h = both[:, :ML_DV] / jnp.maximum(jnp.abs(den), jnp.exp(-m_row))
        cols = slice(hd * ML_DV, (hd + 1) * ML_DV)
        h_ref[0, :, cols] = _head_out(h, o_ref[0, :, cols], gout_ref[:, cols])
        m_new = m_row[tc - 1:tc, :]
        wk = jnp.exp(bc[tc - 1:tc, :] - bc + g[:, hd:hd + 1] - m_new)
        upd = lax.dot_general((kh * wk).astype(BF16), vaug, (((0,), (0,)), ((), ())),
                              preferred_element_type=F32)
        caug[hd] = w_inter[tc - 1:tc, :] * cm + upd
        mst[hd:hd + 1, :] = jnp.broadcast_to(m_new, (1, LANES))

    @pl.when(ci == nchunks - 1)
    def _():
        c_ref[0] = caug[:, :, :ML_DV]
        n_ref[0] = caug[:, :, ML_DV:]
        m_ref[0] = mst[...]


def _mlstm_prompt(proj, bias, gout):
    bsz, seq, _ = proj.shape
    tc = ML_CHUNK
    nchunks = seq // tc
    blk = lambda w, j: pl.BlockSpec((1, tc, w), lambda b, c: (b, c, j))
    st = lambda shape: pl.BlockSpec((1,) + shape, lambda b, c: (b,) + (0,) * len(shape))
    return pl.pallas_call(
        functools.partial(_mlstm_prompt_kernel, tc=tc, nchunks=nchunks),
        grid=(bsz, nchunks),
        in_specs=[blk(ML_QK, 0), blk(ML_QK, ODD_K_BLOCK), blk(ML_WIDTH, ODD_V_BLOCK), blk(ML_WIDTH, ODD_O_BLOCK),
                  blk(LANES, ODD_G_BLOCK), _const_spec((1, LANES)), _const_spec((1, ML_WIDTH))],
        out_specs=[blk(ML_WIDTH, 0), st((ML_HEADS, ML_DK, ML_DV)), st((ML_HEADS, ML_DK, ML_DV)),
                   st((ML_HEADS, LANES))],
        out_shape=[jax.ShapeDtypeStruct((bsz, seq, ML_WIDTH), BF16),
                   jax.ShapeDtypeStruct((bsz, ML_HEADS, ML_DK, ML_DV), F32),
                   jax.ShapeDtypeStruct((bsz, ML_HEADS, ML_DK, ML_DV), F32),
                   jax.ShapeDtypeStruct((bsz, ML_HEADS, LANES), F32)],
        scratch_shapes=[pltpu.VMEM((ML_HEADS, ML_DK, ML_AUG), F32), pltpu.VMEM((ML_HEADS, LANES), F32)],
        compiler_params=_cparams("parallel", "arbitrary"),
        name="mlstm_prompt",
    )(proj, proj, proj, proj, proj, bias, gout)


def _mlstm_sample_kernel(q_ref, k_ref, v_ref, o_ref, g_ref, bias_ref, gout_ref, c0_ref, n0_ref, m0_ref,
                         h_ref, c_ref, n_ref, m_ref, *, bs, t_new):
    tio = lax.broadcasted_iota(jnp.int32, (t_new, LANES), 0)

    def one_sequence(b, carry):
        g = g_ref[b] + bias_ref[...]
        lf = pltpu.roll(_log_sigmoid(g), LANES - ML_HEADS, 1)
        cum, acc = [], None
        for t in range(t_new):
            acc = lf[t:t + 1] if acc is None else acc + lf[t:t + 1]
            cum.append(acc)
        bcol = jnp.concatenate(cum, axis=0)
        m_prev = m0_ref[pl.ds(b, 1), :]
        inter = bcol + m_prev
        dcols = [jnp.where(tio >= s, bcol - cum[s] + g[s:s + 1], NEG_INF) for s in range(t_new)]
        m_row = inter
        for dc in dcols:
            m_row = jnp.maximum(m_row, dc)
        w_inter = jnp.exp(inter - m_row)
        w_intra = [jnp.exp(dc - m_row) for dc in dcols]
        floor = jnp.exp(-m_row)
        m_ref[pl.ds(b, 1), :] = m_row[t_new - 1:t_new]
        for hd in range(ML_HEADS):
            hl = slice(hd, hd + 1)
            qh = q_ref[b, :, hd * ML_DK:(hd + 1) * ML_DK]
            kh = k_ref[b, :, hd * ML_DK:(hd + 1) * ML_DK] * (ML_DK ** -0.5)
            vh = v_ref[b, :, hd * ML_DV:(hd + 1) * ML_DV]
            cm = c0_ref[b, hd]
            nrow = n0_ref[b, hl, :]
            qhb, khb = qh.astype(BF16), kh.astype(BF16)
            qk = lax.dot_general(qhb, khb, (((1,), (1,)), ((), ())), preferred_element_type=F32)
            wi = w_inter[:, hl]
            num = wi * jnp.dot(qhb, cm.astype(BF16), preferred_element_type=F32)
            den = wi * jnp.sum(qh * nrow, axis=-1, keepdims=True)
            kw_rows = []
            for s in range(t_new):
                coef = qk[:, s:s + 1] * w_intra[s][:, hl]
                num = num + coef * vh[s:s + 1, :]
                den = den + coef
                kw_rows.append(kh[s:s + 1, :] * w_intra[s][t_new - 1:t_new, hl])
            h = num / jnp.maximum(jnp.abs(den), floor[:, hl])
            cols = slice(hd * ML_DV, (hd + 1) * ML_DV)
            h_ref[b, :, cols] = _head_out(h, o_ref[b, :, cols], gout_ref[:, cols])
            kw = jnp.concatenate(kw_rows, axis=0)
            upd = lax.dot_general(kw.astype(BF16), vh.astype(BF16), (((0,), (0,)), ((), ())),
                                  preferred_element_type=F32)
            decay = w_inter[t_new - 1:t_new, hl]
            c_ref[b, hd] = decay * cm + upd
            n_ref[b, hl, :] = decay * nrow + jnp.sum(kw, axis=0, keepdims=True)
        return carry

    lax.fori_loop(0, bs, one_sequence, 0)


def _mlstm_sample(proj, bias, gout, c0, n0, m0):
    bsz, t_new, _ = proj.shape
    bs = SAMPLE_BLOCK
    blk = lambda w, j: pl.BlockSpec((bs, t_new, w), lambda i: (i, 0, j))
    cst = pl.BlockSpec((bs, ML_HEADS, ML_DK, ML_DV), lambda i: (i, 0, 0, 0))
    nst = pl.BlockSpec((bs, ML_HEADS, ML_DK), lambda i: (i, 0, 0))
    mst = pl.BlockSpec((bs, LANES), lambda i: (i, 0))
    return pl.pallas_call(
        functools.partial(_mlstm_sample_kernel, bs=bs, t_new=t_new),
        grid=(bsz // bs,),
        in_specs=[blk(ML_QK, 0), blk(ML_QK, ODD_K_BLOCK), blk(ML_WIDTH, ODD_V_BLOCK), blk(ML_WIDTH, ODD_O_BLOCK),
                  blk(LANES, ODD_G_BLOCK), _const_spec((1, LANES)), _const_spec((1, ML_WIDTH)), cst, nst, mst],
        out_specs=[blk(ML_WIDTH, 0), cst, nst, mst],
        out_shape=[jax.ShapeDtypeStruct((bsz, t_new, ML_WIDTH), BF16),
                   jax.ShapeDtypeStruct(c0.shape, F32), jax.ShapeDtypeStruct(n0.shape, F32),
                   jax.ShapeDtypeStruct(m0.shape, F32)],
        compiler_params=_cparams("parallel"),
        name="mlstm_sample",
    )(proj, proj, proj, proj, proj, bias, gout, c0, n0, m0)


def _pad_lanes(x):
    return jnp.pad(x, [(0, 0)] * (x.ndim - 1) + [(0, LANES - x.shape[-1])])


def kernel(x_prompt, x_sample, cache_k, cache_v, state_ssm_re, state_ssm_im, state_mlstm_c, state_mlstm_n, state_mlstm_m, norm_mix, norm_ffn, w_in_even, q_norm, k_norm, attn_sinks, s5_a_re, s5_a_im, s5_log_dt, s5_b_re, s5_b_im, s5_c_re, s5_c_im, s5_d, s5_w_glu, s5_b_glu, w_out_even, w_in_odd, ml_b_i, ml_b_f, ml_out_norm, w_out_odd, w_gate, w_up, w_down):
    bp, lp, _ = x_prompt.shape
    bsm, ls, _ = x_sample.shape
    yp = x_prompt.reshape(bp * lp, D_MODEL)
    ys = x_sample.reshape(bsm * ls, D_MODEL)
    tab_p = _rope_tables(jnp.arange(lp))
    tab_s = tuple(jnp.tile(t, (SAMPLE_BLOCK, 1)) for t in _rope_tables(PAST_LEN + jnp.arange(ls)))
    outs = {name: [] for name in ("pk", "pv", "psr", "psi", "pc", "pn", "pm", "sk", "sv", "ssr", "ssi", "sc", "sn", "sm")}
    for layer in range(DEPTH):
        g_mix = norm_mix[layer].reshape(1, D_MODEL)
        g_ffn = norm_ffn[layer].reshape(1, D_MODEL)
        ffn_w = (w_gate[layer].astype(BF16), w_up[layer].astype(BF16), w_down[layer].astype(BF16))
        if layer % 2 == 0:
            e = layer // 2
            w = w_in_even[e]
            w_in = jnp.concatenate([w[:, :ATTN_WIDTH], w[:, ATTN_WIDTH + 2 * KV_WIDTH:],
                                    w[:, ATTN_WIDTH:ATTN_WIDTH + 2 * KV_WIDTH]], axis=1).astype(BF16)
            gq = jnp.tile(q_norm[e], LANES // HEAD_DIM).reshape(1, LANES)
            gk = jnp.tile(k_norm[e], LANES // HEAD_DIM).reshape(1, LANES)
            prm = _s5_params(s5_a_re[e], s5_a_im[e], s5_log_dt[e], s5_b_re[e], s5_b_im[e], s5_c_re[e], s5_c_im[e],
                             s5_d[e], s5_w_glu[e], s5_b_glu[e])
            w_out = w_out_even[e].astype(BF16)
            proj = _norm_matmul(yp, g_mix, w_in)
            proj3 = proj.reshape(bp, lp, -1)
            attn, k_new, v_new = _attn_prompt(proj3, tab_p, gq, gk, attn_sinks[e])
            ssm, sr, si = _s5_prompt(proj3, prm)
            yp = _mix_ffn(yp, [attn.reshape(bp * lp, -1), ssm.reshape(bp * lp, -1)], w_out, g_ffn, *ffn_w)
            outs["pk"].append(k_new.reshape(bp, WINDOW, KV_HEADS, HEAD_DIM))
            outs["pv"].append(v_new.reshape(bp, WINDOW, KV_HEADS, HEAD_DIM))
            outs["psr"].append(sr.reshape(bp, S5_GROUPS, S5_STATE))
            outs["psi"].append(si.reshape(bp, S5_GROUPS, S5_STATE))
            proj = _norm_matmul(ys, g_mix, w_in)
            attn, k_new, v_new = _attn_sample(proj, cache_k[e].reshape(bsm, WINDOW, KV_WIDTH),
                                              cache_v[e].reshape(bsm, WINDOW, KV_WIDTH), tab_s, gq, gk,
                                              attn_sinks[e], ls)
            u = proj[:, ATTN_WIDTH:ATTN_WIDTH + S5_WIDTH]
            u_tm = u.reshape(bsm, ls, S5_WIDTH).transpose(1, 0, 2).reshape(ls * bsm, S5_WIDTH)
            ssm_tm, sr, si = _s5_sample(u_tm, state_ssm_re[e].reshape(bsm, S5_FLAT),
                                        state_ssm_im[e].reshape(bsm, S5_FLAT), prm, ls)
            ssm = ssm_tm.reshape(ls, bsm, S5_WIDTH).transpose(1, 0, 2).reshape(bsm * ls, S5_WIDTH)
            ys = _mix_ffn(ys, [attn, ssm], w_out, g_ffn, *ffn_w)
            outs["sk"].append(k_new.reshape(bsm, WINDOW, KV_HEADS, HEAD_DIM))
            outs["sv"].append(v_new.reshape(bsm, WINDOW, KV_HEADS, HEAD_DIM))
            outs["ssr"].append(sr.reshape(bsm, S5_GROUPS, S5_STATE))
            outs["ssi"].append(si.reshape(bsm, S5_GROUPS, S5_STATE))
        else:
            o = layer // 2
            w_in = jnp.pad(w_in_odd[o], ((0, 0), (0, ODD_IN_PAD - ODD_IN))).astype(BF16)
            bias = _pad_lanes(jnp.concatenate([ml_b_i[o], ml_b_f[o]]).reshape(1, 2 * ML_HEADS))
            gout = ml_out_norm[o].reshape(1, ML_WIDTH)
            w_out = w_out_odd[o].astype(BF16)
            proj = _norm_matmul(yp, g_mix, w_in)
            hh, c, n, m = _mlstm_prompt(proj.reshape(bp, lp, -1), bias, gout)
            yp = _mix_ffn(yp, [hh.reshape(bp * lp, -1)], w_out, g_ffn, *ffn_w)
            outs["pc"].append(c)
            outs["pn"].append(n[..., 0])
            outs["pm"].append(m[..., 0])
            proj = _norm_matmul(ys, g_mix, w_in)
            hh, c, n, m = _mlstm_sample(proj.reshape(bsm, ls, -1), bias, gout, state_mlstm_c[o], state_mlstm_n[o],
                                        _pad_lanes(state_mlstm_m[o]))
            ys = _mix_ffn(ys, [hh.reshape(bsm * ls, -1)], w_out, g_ffn, *ffn_w)
            outs["sc"].append(c)
            outs["sn"].append(n)
            outs["sm"].append(m[:, :ML_HEADS])
    st = lambda name: jnp.stack(outs[name])
    return (yp.reshape(bp, lp, D_MODEL), ys.reshape(bsm, ls, D_MODEL),
            st("pk"), st("pv"), st("psr"), st("psi"), st("pc"), st("pn"), st("pm"),
            st("sk"), st("sv"), st("ssr"), st("ssi"), st("sc"), st("sn"), st("sm"))
```

```python
import functools
import math

import jax
import jax.numpy as jnp
from jax import lax
from jax.experimental import pallas as pl
from jax.experimental.pallas import tpu as pltpu

F32 = jnp.float32
BF16 = jnp.bfloat16

D_MODEL = 1024
DEPTH = 4
PAST_LEN = 8192
WINDOW = 128
ATTN_HEADS = 8
KV_HEADS = 2
HEAD_DIM = 64
ATTN_WIDTH = ATTN_HEADS * HEAD_DIM
KV_WIDTH = KV_HEADS * HEAD_DIM
ROT_DIM = HEAD_DIM // 4
ROPE_THETA = 500000.0
S5_GROUP = 16
S5_WIDTH = D_MODEL // 2
S5_GROUPS = S5_WIDTH // S5_GROUP
S5_STATE = 64
S5_FLAT = S5_GROUPS * S5_STATE
ML_HEADS = 8
ML_DV = D_MODEL // ML_HEADS
ML_DK = ML_DV // 2
ML_QK = ML_HEADS * ML_DK
ML_WIDTH = ML_HEADS * ML_DV
ODD_IN = 2 * ML_QK + 2 * ML_WIDTH + 2 * ML_HEADS
ODD_IN_PAD = 2 * ML_QK + 2 * ML_WIDTH + 128
D_FF = 2816
EPS = 1e-6

LANES = 128
SUBLANES = 8
ROW_TILE = 512
FF_TILE = 256
S5_CHUNK = 64
ML_CHUNK = 128
SAMPLE_BLOCK = 8
VMEM_LIMIT = 56 * 1024 * 1024

NEG_INF = float("-inf")


def _cparams(*sem):
    return pltpu.CompilerParams(dimension_semantics=sem, vmem_limit_bytes=VMEM_LIMIT)


def _const_spec(shape):
    zeros = (0,) * len(shape)
    return pl.BlockSpec(shape, lambda *_: zeros, pipeline_mode=pl.Buffered(1))


def _layer_spec(shape, layer):
    zeros = (0,) * len(shape)
    return pl.BlockSpec((None,) + tuple(shape), lambda *_: (layer,) + zeros, pipeline_mode=pl.Buffered(1))


def _skip_refs(body, n_skip):
    if n_skip == 0:
        return body

    def wrapped(*refs):
        return body(*refs[n_skip:])

    return wrapped


def _alias_inputs(prev, first_state_out):
    prev = () if prev is None else tuple(prev)
    specs = [pl.BlockSpec(memory_space=pl.ANY) for _ in prev]
    aliases = {i: first_state_out + i for i in range(len(prev))}
    return prev, specs, aliases


def _rms(x, g):
    ms = jnp.mean(x * x, axis=-1, keepdims=True)
    return x * lax.rsqrt(ms + EPS) * g


def _split3(a):
    a1 = a.astype(BF16)
    r1 = a - a1.astype(F32)
    a2 = r1.astype(BF16)
    a3 = (r1 - a2.astype(F32)).astype(BF16)
    return a1, a2, a3


def _log_sigmoid(x):
    return jnp.minimum(x, 0.0) - jnp.log1p(jnp.exp(-jnp.abs(x)))


def _sigmoid(x):
    return 1.0 / (1.0 + jnp.exp(-x))


def _norm_matmul_kernel(x_ref, g_ref, w_ref, o_ref):
    h = _rms(x_ref[...], g_ref[...]).astype(BF16)
    o_ref[...] = jnp.dot(h, w_ref[...], preferred_element_type=F32)


def _norm_matmul(x, g, layer, w, widx):
    n, d = x.shape
    m = w.shape[2]
    tm = min(ROW_TILE, n)
    return pl.pallas_call(
        _norm_matmul_kernel,
        grid=(n // tm,),
        in_specs=[pl.BlockSpec((tm, d), lambda i: (i, 0)), _layer_spec((1, d), layer), _layer_spec((d, m), widx)],
        out_specs=pl.BlockSpec((tm, m), lambda i: (i, 0)),
        out_shape=jax.ShapeDtypeStruct((n, m), F32),
        compiler_params=_cparams("parallel"),
        name="norm_matmul",
    )(x, g, w)


def _mix_ffn_kernel(*refs, n_mix):
    x_ref = refs[0]
    a_refs = refs[1:1 + n_mix]
    wo_ref, g_ref, wg_ref, wu_ref, wd_ref, o_ref, act_ref = refs[1 + n_mix:]
    y = x_ref[...]
    off = 0
    for a_ref in a_refs:
        ka = a_ref.shape[1]
        y = y + jnp.dot(a_ref[...], wo_ref[off:off + ka, :], preferred_element_type=F32)
        off += ka
    h = _rms(y, g_ref[...]).astype(BF16)
    for f in range(D_FF // FF_TILE):
        cols = slice(f * FF_TILE, (f + 1) * FF_TILE)
        gate = jnp.dot(h, wg_ref[:, cols], preferred_element_type=F32)
        up = jnp.dot(h, wu_ref[:, cols], preferred_element_type=F32)
        act_ref[:, cols] = (gate * _sigmoid(gate) * up).astype(BF16)
    o_ref[...] = y + jnp.dot(act_ref[...], wd_ref[...], preferred_element_type=F32)


def _mix_ffn(x, mixes, w_out, oidx, layer, g_ffn, wg, wu, wd):
    n, d = x.shape
    tm = min(ROW_TILE, n)
    row = lambda i: (i, 0)
    in_specs = [pl.BlockSpec((tm, d), row)]
    in_specs += [pl.BlockSpec((tm, a.shape[1]), row) for a in mixes]
    in_specs += [_layer_spec(w_out.shape[1:], oidx), _layer_spec((1, d), layer), _layer_spec(wg.shape[1:], layer),
                 _layer_spec(wu.shape[1:], layer), _layer_spec(wd.shape[1:], layer)]
    return pl.pallas_call(
        functools.partial(_mix_ffn_kernel, n_mix=len(mixes)),
        grid=(n // tm,),
        in_specs=in_specs,
        out_specs=pl.BlockSpec((tm, d), row),
        out_shape=jax.ShapeDtypeStruct((n, d), F32),
        scratch_shapes=[pltpu.VMEM((tm, D_FF), BF16)],
        compiler_params=_cparams("parallel"),
        name="mix_ffn",
    )(x, *mixes, w_out, g_ffn, wg, wu, wd)


def _head_ones():
    r = lax.broadcasted_iota(jnp.int32, (LANES, LANES), 0) // HEAD_DIM
    c = lax.broadcasted_iota(jnp.int32, (LANES, LANES), 1) // HEAD_DIM
    return jnp.where(r == c, 1.0, 0.0).astype(BF16)


def _qk_prep(x, g, ones, ct, sa, sb):
    x2 = x * x
    hi = x2.astype(BF16)
    lo = (x2 - hi.astype(F32)).astype(BF16)
    ss = jnp.dot(hi, ones, preferred_element_type=F32) + jnp.dot(lo, ones, preferred_element_type=F32)
    xn = x * lax.rsqrt(ss * (1.0 / HEAD_DIM) + EPS) * g
    return xn * ct + pltpu.roll(xn, LANES - ROT_DIM // 2, 1) * sa + pltpu.roll(xn, ROT_DIM // 2, 1) * sb


def _rope_tables(pos):
    half = ROT_DIM // 2
    inv = jnp.power(jnp.float32(ROPE_THETA), -jnp.arange(half, dtype=F32) / half)
    ang = pos.astype(F32)[:, None] * inv[None, :]
    cos, sin = jnp.cos(ang), jnp.sin(ang)
    n = pos.shape[0]
    one = jnp.ones((n, HEAD_DIM - ROT_DIM), F32)
    zero = jnp.zeros((n, HEAD_DIM - ROT_DIM), F32)
    z8 = jnp.zeros((n, half), F32)
    ct = jnp.concatenate([cos, cos, one], axis=1)
    sa = jnp.concatenate([-sin, z8, zero], axis=1)
    sb = jnp.concatenate([z8, sin, zero], axis=1)
    tile = lambda t: jnp.concatenate([t, t], axis=1)
    return tile(ct), tile(sa), tile(sb)


def _attn_prompt_kernel(q_ref, kv_ref, ct_ref, sa_ref, sb_ref, gq_ref, gk_ref, sink_ref,
                        o_ref, pk_ref, pv_ref, kprev, vprev, *, nb, layer):
    i = pl.program_id(1)

    @pl.when(i == 0)
    def _():
        kprev[...] = jnp.zeros_like(kprev)
        vprev[...] = jnp.zeros_like(vprev)

    ones = _head_ones()
    ct, sa, sb = ct_ref[...], sa_ref[...], sb_ref[...]
    kv = kv_ref[0]
    kn = _qk_prep(kv[:, :KV_WIDTH], gk_ref[...], ones, ct, sa, sb)
    v = kv[:, KV_WIDTH:]
    qn = [_qk_prep(q_ref[0, :, c * LANES:(c + 1) * LANES], gq_ref[...], ones, ct, sa, sb)
          for c in range(ATTN_WIDTH // LANES)]
    kcat = jnp.concatenate([kprev[...], kn], axis=0).astype(BF16)
    vcat = jnp.concatenate([vprev[...], v], axis=0).astype(BF16)
    r = lax.broadcasted_iota(jnp.int32, (WINDOW, 2 * WINDOW), 0)
    c = lax.broadcasted_iota(jnp.int32, (WINDOW, 2 * WINDOW), 1)
    rel = r + WINDOW - c
    mask = (rel >= 0) & (rel <= WINDOW) & ((c >= WINDOW) | (i > 0))
    grp = ATTN_HEADS // KV_HEADS
    outs = []
    for h in range(KV_HEADS):
        kh = kcat[:, h * HEAD_DIM:(h + 1) * HEAD_DIM]
        vh = vcat[:, h * HEAD_DIM:(h + 1) * HEAD_DIM]
        qs = []
        for g in range(grp):
            head = h * grp + g
            lane0 = (head % 2) * HEAD_DIM
            qs.append(qn[head // 2][:, lane0:lane0 + HEAD_DIM])
        qs = (jnp.concatenate(qs, axis=0) * (HEAD_DIM ** -0.5)).astype(BF16)
        s = lax.dot_general(qs, kh, (((1,), (1,)), ((), ())), preferred_element_type=F32)
        ps = []
        for g in range(grp):
            sg = jnp.where(mask, s[g * WINDOW:(g + 1) * WINDOW], NEG_INF)
            sink = sink_ref[layer, h * grp + g]
            m = jnp.maximum(jnp.max(sg, axis=-1, keepdims=True), sink)
            p = jnp.exp(sg - m)
            den = jnp.sum(p, axis=-1, keepdims=True) + jnp.exp(sink - m)
            ps.append((p / den).astype(BF16))
        o = jnp.dot(jnp.concatenate(ps, axis=0), vh, preferred_element_type=F32)
        outs += [o[g * WINDOW:(g + 1) * WINDOW] for g in range(grp)]
    o_ref[0] = jnp.concatenate(outs, axis=1).astype(BF16)
    kprev[...] = kn
    vprev[...] = v

    @pl.when(i == nb - 1)
    def _():
        pk_ref[0] = kn
        pv_ref[0] = v


def _attn_prompt(proj, tables, gq, gk, sinks, layer, prev):
    bsz, seq, _ = proj.shape
    nb = seq // WINDOW
    n_layers = gq.shape[0]
    tab = pl.BlockSpec((WINDOW, LANES), lambda b, i: (i, 0))
    prev, prev_specs, aliases = _alias_inputs(prev, 1)
    win = pl.BlockSpec((None, 1, WINDOW, KV_WIDTH), lambda b, i: (layer, b, 0, 0))
    win_shape = jax.ShapeDtypeStruct((n_layers, bsz, WINDOW, KV_WIDTH), F32)
    return pl.pallas_call(
        _skip_refs(functools.partial(_attn_prompt_kernel, nb=nb, layer=layer), len(prev)),
        grid=(bsz, nb),
        in_specs=prev_specs + [
            pl.BlockSpec((1, WINDOW, ATTN_WIDTH), lambda b, i: (b, i, 0)),
            pl.BlockSpec((1, WINDOW, 2 * KV_WIDTH), lambda b, i: (b, i, EVEN_KV_BLOCK)),
            tab, tab, tab, _layer_spec((1, LANES), layer), _layer_spec((1, LANES), layer),
            pl.BlockSpec(memory_space=pltpu.SMEM)],
        out_specs=[pl.BlockSpec((1, WINDOW, ATTN_WIDTH), lambda b, i: (b, i, 0)), win, win],
        out_shape=[jax.ShapeDtypeStruct((bsz, seq, ATTN_WIDTH), BF16), win_shape, win_shape],
        input_output_aliases=aliases,
        scratch_shapes=[pltpu.VMEM((WINDOW, KV_WIDTH), F32), pltpu.VMEM((WINDOW, KV_WIDTH), F32)],
        compiler_params=_cparams("parallel", "arbitrary"),
        name="attn_prompt",
    )(*prev, proj, proj, *tables, gq, gk, sinks)


EVEN_U_BLOCK = ATTN_WIDTH // S5_WIDTH
EVEN_KV_BLOCK = (ATTN_WIDTH + S5_WIDTH) // (2 * KV_WIDTH)
KALL_ROWS = WINDOW + SUBLANES


def _attn_sample_kernel(q_ref, kv_ref, ck_ref, cv_ref, ct_ref, sa_ref, sb_ref, gq_ref, gk_ref, sink_ref,
                        o_ref, nk_ref, nv_ref, kall, vall, *, bs, t_new, layer):
    ones = _head_ones()
    ct, sa, sb = ct_ref[...], sa_ref[...], sb_ref[...]
    kv = kv_ref[...]
    kn = _qk_prep(kv[:, :KV_WIDTH], gk_ref[...], ones, ct, sa, sb)
    v = kv[:, KV_WIDTH:]
    qn = jnp.concatenate(
        [_qk_prep(q_ref[:, c * LANES:(c + 1) * LANES], gq_ref[...], ones, ct, sa, sb)
         for c in range(ATTN_WIDTH // LANES)], axis=1) * (HEAD_DIM ** -0.5)
    grp = ATTN_HEADS // KV_HEADS
    rows = grp * t_new
    r = lax.broadcasted_iota(jnp.int32, (rows, KALL_ROWS), 0)
    c = lax.broadcasted_iota(jnp.int32, (rows, KALL_ROWS), 1)
    t = r % t_new
    mask = (c >= t) & (c <= t + WINDOW)
    rg = lax.broadcasted_iota(jnp.int32, (rows, 1), 0) // t_new
    pad = jnp.zeros((KALL_ROWS - WINDOW - t_new, KV_WIDTH), F32)
    for b in range(bs):
        rb = slice(b * t_new, (b + 1) * t_new)
        ck, cv = ck_ref[b], cv_ref[b]
        kall[0:WINDOW, :] = ck
        vall[0:WINDOW, :] = cv
        kall[WINDOW:KALL_ROWS, :] = jnp.concatenate([kn[rb], pad], axis=0)
        vall[WINDOW:KALL_ROWS, :] = jnp.concatenate([v[rb], pad], axis=0)
        nk_ref[b] = pltpu.roll(ck, WINDOW - t_new, 0)
        nv_ref[b] = pltpu.roll(cv, WINDOW - t_new, 0)
        nk_ref[b, WINDOW - t_new:WINDOW, :] = kn[rb]
        nv_ref[b, WINDOW - t_new:WINDOW, :] = v[rb]
        kb = kall[...].astype(BF16)
        vb = vall[...].astype(BF16)
        outs = []
        for h in range(KV_HEADS):
            kh = kb[:, h * HEAD_DIM:(h + 1) * HEAD_DIM]
            vh = vb[:, h * HEAD_DIM:(h + 1) * HEAD_DIM]
            qs = jnp.concatenate(
                [qn[rb, (h * grp + g) * HEAD_DIM:(h * grp + g + 1) * HEAD_DIM] for g in range(grp)],
                axis=0).astype(BF16)
            s = lax.dot_general(qs, kh, (((1,), (1,)), ((), ())), preferred_element_type=F32)
            s = jnp.where(mask, s, NEG_INF)
            sink = jnp.zeros((rows, 1), F32)
            for g in range(grp):
                sink = jnp.where(rg == g, sink_ref[layer, h * grp + g], sink)
            m = jnp.maximum(jnp.max(s, axis=-1, keepdims=True), sink)
            p = jnp.exp(s - m)
            den = jnp.sum(p, axis=-1, keepdims=True) + jnp.exp(sink - m)
            o = jnp.dot((p / den).astype(BF16), vh, preferred_element_type=F32)
            outs += [o[g * t_new:(g + 1) * t_new] for g in range(grp)]
        o_ref[rb, :] = jnp.concatenate(outs, axis=1).astype(BF16)


def _attn_sample(proj, cache_k, cache_v, tables, gq, gk, sinks, t_new, layer, prev):
    n = proj.shape[0]
    bsz = n // t_new
    bs = SAMPLE_BLOCK
    rows = bs * t_new
    row = lambda i: (i, 0)
    cache = pl.BlockSpec((None, bs, WINDOW, KV_WIDTH), lambda i: (layer, i, 0, 0))
    prev, prev_specs, aliases = _alias_inputs(prev, 1)
    return pl.pallas_call(
        _skip_refs(functools.partial(_attn_sample_kernel, bs=bs, t_new=t_new, layer=layer), len(prev)),
        grid=(bsz // bs,),
        in_specs=prev_specs + [
            pl.BlockSpec((rows, ATTN_WIDTH), row),
            pl.BlockSpec((rows, 2 * KV_WIDTH), lambda i: (i, EVEN_KV_BLOCK)),
            cache, cache,
            _const_spec((rows, LANES)), _const_spec((rows, LANES)), _const_spec((rows, LANES)),
            _layer_spec((1, LANES), layer), _layer_spec((1, LANES), layer),
            pl.BlockSpec(memory_space=pltpu.SMEM)],
        out_specs=[pl.BlockSpec((rows, ATTN_WIDTH), row), cache, cache],
        out_shape=[jax.ShapeDtypeStruct((n, ATTN_WIDTH), BF16),
                   jax.ShapeDtypeStruct(cache_k.shape, F32), jax.ShapeDtypeStruct(cache_v.shape, F32)],
        input_output_aliases=aliases,
        scratch_shapes=[pltpu.VMEM((KALL_ROWS, KV_WIDTH), F32), pltpu.VMEM((KALL_ROWS, KV_WIDTH), F32)],
        compiler_params=_cparams("parallel"),
        name="attn_sample",
    )(*prev, proj, proj, cache_k, cache_v, *tables, gq, gk, sinks)


S5_UCHUNKS = S5_WIDTH // LANES
S5_SUB = S5_FLAT // S5_UCHUNKS
S5_SCHUNKS = S5_FLAT // LANES


def _s5_tail(y, wglu_ref, bglu_ref):
    g = 0.5 * y * (1.0 + lax.erf(y * (2.0 ** -0.5)))
    z = jnp.dot(g.astype(BF16), wglu_ref[...], preferred_element_type=F32) + bglu_ref[...]
    return g * _sigmoid(z)


def _s5_prompt_kernel(u_ref, wb_ref, wc_ref, lam_ref, d_ref, wglu_ref, bglu_ref,
                      o_ref, sr_ref, si_ref, xs, hst, *, nbatch, tc):
    rows = nbatch * tc

    @pl.when(pl.program_id(1) == 0)
    def _():
        hst[...] = jnp.zeros_like(hst)

    u = jnp.swapaxes(u_ref[...], 0, 1).reshape(rows, S5_WIDTH)
    ub = u.astype(BF16)
    for cc in range(S5_UCHUNKS):
        res = jnp.dot(ub[:, cc * LANES:(cc + 1) * LANES], wb_ref[cc], preferred_element_type=F32)
        for j in range(S5_SUB // LANES):
            xs[cc * 4 + j] = res[:, j * LANES:(j + 1) * LANES]
            xs[S5_SCHUNKS + cc * 4 + j] = res[:, S5_SUB + j * LANES:S5_SUB + (j + 1) * LANES]

    def step(t, carry):
        new_r, new_i = [], []
        for k in range(S5_SCHUNKS):
            hr, hi = carry[k], carry[S5_SCHUNKS + k]
            lr, li = lam_ref[k], lam_ref[S5_SCHUNKS + k]
            idx = pl.ds(pl.multiple_of(t * nbatch, nbatch), nbatch)
            nr = lr * hr - li * hi + xs[k, idx, :]
            ni = lr * hi + li * hr + xs[S5_SCHUNKS + k, idx, :]
            xs[k, idx, :] = nr
            xs[S5_SCHUNKS + k, idx, :] = ni
            new_r.append(nr)
            new_i.append(ni)
        return tuple(new_r + new_i)

    fin = lax.fori_loop(0, tc, step, tuple(hst[k] for k in range(2 * S5_SCHUNKS)))
    for k in range(2 * S5_SCHUNKS):
        hst[k] = fin[k]
    sr_ref[...] = jnp.concatenate(fin[:S5_SCHUNKS], axis=1)
    si_ref[...] = jnp.concatenate(fin[S5_SCHUNKS:], axis=1)

    ys = []
    for cc in range(S5_UCHUNKS):
        s = jnp.concatenate([xs[cc * 4 + j] for j in range(4)]
                            + [xs[S5_SCHUNKS + cc * 4 + j] for j in range(4)], axis=1).astype(BF16)
        cols = slice(cc * LANES, (cc + 1) * LANES)
        ys.append(jnp.dot(s, wc_ref[cc], preferred_element_type=F32) + d_ref[:, cols] * u[:, cols])
    out = _s5_tail(jnp.concatenate(ys, axis=1), wglu_ref, bglu_ref)
    o_ref[...] = jnp.swapaxes(out.reshape(tc, nbatch, S5_WIDTH), 0, 1).astype(BF16)


def _s5_prompt(proj, prm, layer, prev):
    bsz, seq, _ = proj.shape
    nbatch, tc = SUBLANES, S5_CHUNK
    n_layers = prm["wb"].shape[0]
    st = pl.BlockSpec((None, nbatch, S5_FLAT), lambda b, c: (layer, b, 0))
    st_shape = jax.ShapeDtypeStruct((n_layers, bsz, S5_FLAT), F32)
    prev, prev_specs, aliases = _alias_inputs(prev, 1)
    names = ("wb", "wc", "lam8", "d", "wglu", "bglu")
    return pl.pallas_call(
        _skip_refs(functools.partial(_s5_prompt_kernel, nbatch=nbatch, tc=tc), len(prev)),
        grid=(bsz // nbatch, seq // tc),
        in_specs=prev_specs + [pl.BlockSpec((nbatch, tc, S5_WIDTH), lambda b, c: (b, c, EVEN_U_BLOCK))]
        + [_layer_spec(prm[k].shape[1:], layer) for k in names],
        out_specs=[pl.BlockSpec((nbatch, tc, S5_WIDTH), lambda b, c: (b, c, 0)), st, st],
        out_shape=[jax.ShapeDtypeStruct((bsz, seq, S5_WIDTH), BF16), st_shape, st_shape],
        input_output_aliases=aliases,
        scratch_shapes=[pltpu.VMEM((2 * S5_SCHUNKS, nbatch * tc, LANES), F32),
                        pltpu.VMEM((2 * S5_SCHUNKS, nbatch, LANES), F32)],
        compiler_params=_cparams("parallel", "arbitrary"),
        name="s5_prompt",
    )(*prev, proj, *[prm[k] for k in names])


def _s5_sample_kernel(u_ref, wb_ref, wc_ref, lr_ref, li_ref, d_ref, wglu_ref, bglu_ref, h0r_ref, h0i_ref,
                      o_ref, sr_ref, si_ref, xr, xi, *, nseq, t_new):
    u = u_ref[...]
    ub = u.astype(BF16)
    for cc in range(S5_UCHUNKS):
        res = jnp.dot(ub[:, cc * LANES:(cc + 1) * LANES], wb_ref[cc], preferred_element_type=F32)
        xr[:, cc * S5_SUB:(cc + 1) * S5_SUB] = res[:, :S5_SUB]
        xi[:, cc * S5_SUB:(cc + 1) * S5_SUB] = res[:, S5_SUB:]
    lr, li = lr_ref[...], li_ref[...]
    hr, hi = h0r_ref[...], h0i_ref[...]
    for t in range(t_new):
        rows = slice(t * nseq, (t + 1) * nseq)
        nr = lr * hr - li * hi + xr[rows, :]
        ni = lr * hi + li * hr + xi[rows, :]
        xr[rows, :] = nr
        xi[rows, :] = ni
        hr, hi = nr, ni
    sr_ref[...] = hr
    si_ref[...] = hi
    ys = []
    for cc in range(S5_UCHUNKS):
        sc = slice(cc * S5_SUB, (cc + 1) * S5_SUB)
        s = jnp.concatenate([xr[:, sc], xi[:, sc]], axis=1).astype(BF16)
        cols = slice(cc * LANES, (cc + 1) * LANES)
        ys.append(jnp.dot(s, wc_ref[cc], preferred_element_type=F32) + d_ref[:, cols] * u[:, cols])
    o_ref[...] = _s5_tail(jnp.concatenate(ys, axis=1), wglu_ref, bglu_ref).astype(BF16)


def _s5_sample(u_tm, h0r, h0i, prm, t_new, layer, prev):
    n = u_tm.shape[0]
    nseq = n // t_new
    names = ("wb", "wc", "lr", "li", "d", "wglu", "bglu")
    st = pl.BlockSpec((None, nseq, S5_FLAT), lambda i: (layer, 0, 0))
    prev, prev_specs, aliases = _alias_inputs(prev, 1)
    return pl.pallas_call(
        _skip_refs(functools.partial(_s5_sample_kernel, nseq=nseq, t_new=t_new), len(prev)),
        grid=(1,),
        in_specs=prev_specs + [_const_spec(u_tm.shape)] + [_layer_spec(prm[k].shape[1:], layer) for k in names]
        + [_layer_spec((nseq, S5_FLAT), layer), _layer_spec((nseq, S5_FLAT), layer)],
        out_specs=[pl.BlockSpec((n, S5_WIDTH), lambda i: (0, 0)), st, st],
        out_shape=[jax.ShapeDtypeStruct((n, S5_WIDTH), BF16),
                   jax.ShapeDtypeStruct(h0r.shape, F32), jax.ShapeDtypeStruct(h0i.shape, F32)],
        input_output_aliases=aliases,
        scratch_shapes=[pltpu.VMEM((n, S5_FLAT), F32), pltpu.VMEM((n, S5_FLAT), F32)],
        compiler_params=_cparams("arbitrary"),
        name="s5_sample",
    )(*prev, u_tm, *[prm[k] for k in names], h0r, h0i)


def _s5_params(a_re, a_im, log_dt, b_re, b_im, c_re, c_im, d_skip, w_glu, b_glu):
    nl = a_re.shape[0]
    dt = jnp.exp(log_dt)
    mag = jnp.exp(a_re * dt)
    lr, li = mag * jnp.cos(a_im * dt), mag * jnp.sin(a_im * dt)
    den = a_re * a_re + a_im * a_im
    cr = ((lr - 1.0) * a_re + li * a_im) / den
    ci = (li * a_re - (lr - 1.0) * a_im) / den
    bbr = cr[..., None] * b_re - ci[..., None] * b_im
    bbi = cr[..., None] * b_im + ci[..., None] * b_re
    gpc = LANES // S5_GROUP
    eye = jnp.eye(gpc, dtype=F32)

    def in_blocks(bb):
        bb = bb.reshape(nl, S5_UCHUNKS, gpc, S5_STATE, S5_GROUP)
        return jnp.einsum("lcgph,gk->lcghkp", bb, eye).reshape(nl, S5_UCHUNKS, LANES, S5_SUB)

    def out_blocks(cm):
        cm = cm.reshape(nl, S5_UCHUNKS, gpc, S5_GROUP, S5_STATE)
        return jnp.einsum("lcghp,gk->lcgpkh", cm, eye).reshape(nl, S5_UCHUNKS, S5_SUB, LANES)

    wb = jnp.concatenate([in_blocks(bbr), in_blocks(bbi)], axis=3).astype(BF16)
    wc = jnp.concatenate([out_blocks(c_re), -out_blocks(c_im)], axis=2).astype(BF16)
    lr_f, li_f = lr.reshape(nl, 1, S5_FLAT), li.reshape(nl, 1, S5_FLAT)
    lam = jnp.concatenate([lr_f.reshape(nl, S5_SCHUNKS, 1, LANES), li_f.reshape(nl, S5_SCHUNKS, 1, LANES)], axis=1)
    lam8 = jnp.broadcast_to(lam, (nl, 2 * S5_SCHUNKS, SUBLANES, LANES))
    return dict(wb=wb, wc=wc, lam8=lam8, lr=lr_f, li=li_f, d=d_skip.reshape(nl, 1, S5_WIDTH),
                wglu=w_glu.astype(BF16), bglu=b_glu.reshape(nl, 1, S5_WIDTH))


ODD_K_BLOCK = 1
ODD_V_BLOCK = (2 * ML_QK) // ML_WIDTH
ODD_O_BLOCK = ODD_V_BLOCK + 1
ODD_G_BLOCK = (2 * ML_QK + 2 * ML_WIDTH) // LANES
ML_AUG = 2 * ML_DV


def _head_out(h, o, gout):
    hn = h * lax.rsqrt(jnp.mean(h * h, axis=-1, keepdims=True) + EPS) * gout
    return (hn * _sigmoid(o)).astype(BF16)


def _mlstm_prompt_kernel(q_ref, k_ref, v_ref, o_ref, g_ref, bias_ref, gout_ref,
                         h_ref, c_ref, n_ref, m_ref, caug, mst, *, tc, nchunks):
    ci = pl.program_id(1)

    @pl.when(ci == 0)
    def _():
        caug[...] = jnp.zeros_like(caug)
        mst[...] = jnp.zeros_like(mst)

    g = g_ref[0] + bias_ref[...]
    lf = pltpu.roll(_log_sigmoid(g), LANES - ML_HEADS, 1)
    rt = lax.broadcasted_iota(jnp.int32, (tc, tc), 0)
    cs = lax.broadcasted_iota(jnp.int32, (tc, tc), 1)
    causal = cs <= rt
    tril = jnp.where(causal, 1.0, 0.0).astype(BF16)
    triu = jnp.where(rt <= cs, 1.0, 0.0).astype(BF16)
    bcol = sum(jnp.dot(tril, p, preferred_element_type=F32) for p in _split3(lf))
    brow = sum(jnp.dot(p, triu, preferred_element_type=F32) for p in _split3(lf.T))
    crow = g.T - brow
    ones_col = jnp.where(lax.broadcasted_iota(jnp.int32, (tc, ML_DV), 1) == 0, 1.0, 0.0)
    for hd in range(ML_HEADS):
        bc = bcol[:, hd:hd + 1]
        dmat = jnp.where(causal, bc + crow[hd:hd + 1, :], NEG_INF)
        m_prev = mst[hd:hd + 1, 0:1]
        inter = bc + m_prev
        m_row = jnp.maximum(inter, jnp.max(dmat, axis=-1, keepdims=True))
        w_intra = jnp.exp(dmat - m_row)
        w_inter = jnp.exp(inter - m_row)
        qh = q_ref[0, :, hd * ML_DK:(hd + 1) * ML_DK].astype(BF16)
        kh = k_ref[0, :, hd * ML_DK:(hd + 1) * ML_DK] * (ML_DK ** -0.5)
        qk = lax.dot_general(qh, kh.astype(BF16), (((1,), (1,)), ((), ())), preferred_element_type=F32) * w_intra
        vaug = jnp.concatenate([v_ref[0, :, hd * ML_DV:(hd + 1) * ML_DV], ones_col], axis=1).astype(BF16)
        cm = caug[hd]
        both = (w_inter * jnp.dot(qh, cm.astype(BF16), preferred_element_type=F32)
                + jnp.dot(qk.astype(BF16), vaug, preferred_element_type=F32))
        den = both[:, ML_DV:ML_DV + 1]
        h = both[:, :ML_DV] / jnp.maximum(jnp.abs(den), jnp.exp(-m_row))
        cols = slice(hd * ML_DV, (hd + 1) * ML_DV)
        h_ref[0, :, cols] = _head_out(h, o_ref[0, :, cols], gout_ref[:, cols])
        m_new = m_row[tc - 1:tc, :]
        wk = jnp.exp(bc[tc - 1:tc, :] - bc + g[:, hd:hd + 1] - m_new)
        upd = lax.dot_general((kh * wk).astype(BF16), vaug, (((0,), (0,)), ((), ())),
                              preferred_element_type=F32)
        caug[hd] = w_inter[tc - 1:tc, :] * cm + upd
        mst[hd:hd + 1, :] = jnp.broadcast_to(m_new, (1, LANES))

    @pl.when(ci == nchunks - 1)
    def _():
        c_ref[0] = caug[:, :, :ML_DV]
        n_ref[0] = caug[:, :, ML_DV:]
        m_ref[0] = mst[...]


def _mlstm_prompt(proj, bias, gout, layer, prev):
    bsz, seq, _ = proj.shape
    tc = ML_CHUNK
    nchunks = seq // tc
    n_layers = bias.shape[0]
    blk = lambda w, j: pl.BlockSpec((1, tc, w), lambda b, c: (b, c, j))
    st = lambda shape: pl.BlockSpec((None, 1) + shape, lambda b, c: (layer, b) + (0,) * len(shape))
    st_shape = lambda shape: jax.ShapeDtypeStruct((n_layers, bsz) + shape, F32)
    prev, prev_specs, aliases = _alias_inputs(prev, 1)
    return pl.pallas_call(
        _skip_refs(functools.partial(_mlstm_prompt_kernel, tc=tc, nchunks=nchunks), len(prev)),
        grid=(bsz, nchunks),
        in_specs=prev_specs + [
            blk(ML_QK, 0), blk(ML_QK, ODD_K_BLOCK), blk(ML_WIDTH, ODD_V_BLOCK), blk(ML_WIDTH, ODD_O_BLOCK),
            blk(LANES, ODD_G_BLOCK), _layer_spec((1, LANES), layer), _layer_spec((1, ML_WIDTH), layer)],
        out_specs=[blk(ML_WIDTH, 0), st((ML_HEADS, ML_DK, ML_DV)), st((ML_HEADS, ML_DK, ML_DV)),
                   st((ML_HEADS, LANES))],
        out_shape=[jax.ShapeDtypeStruct((bsz, seq, ML_WIDTH), BF16),
                   st_shape((ML_HEADS, ML_DK, ML_DV)), st_shape((ML_HEADS, ML_DK, ML_DV)),
                   st_shape((ML_HEADS, LANES))],
        input_output_aliases=aliases,
        scratch_shapes=[pltpu.VMEM((ML_HEADS, ML_DK, ML_AUG), F32), pltpu.VMEM((ML_HEADS, LANES), F32)],
        compiler_params=_cparams("parallel", "arbitrary"),
        name="mlstm_prompt",
    )(*prev, proj, proj, proj, proj, proj, bias, gout)


def _mlstm_sample_kernel(q_ref, k_ref, v_ref, o_ref, g_ref, bias_ref, gout_ref, c0_ref, n0_ref, m0_ref,
                         h_ref, c_ref, n_ref, m_ref, *, bs, t_new):
    tio = lax.broadcasted_iota(jnp.int32, (t_new, LANES), 0)

    def one_sequence(b, carry):
        g = g_ref[b] + bias_ref[...]
        lf = pltpu.roll(_log_sigmoid(g), LANES - ML_HEADS, 1)
        cum, acc = [], None
        for t in range(t_new):
            acc = lf[t:t + 1] if acc is None else acc + lf[t:t + 1]
            cum.append(acc)
        bcol = jnp.concatenate(cum, axis=0)
        m_prev = m0_ref[pl.ds(b, 1), :]
        inter = bcol + m_prev
        dcols = [jnp.where(tio >= s, bcol - cum[s] + g[s:s + 1], NEG_INF) for s in range(t_new)]
        m_row = inter
        for dc in dcols:
            m_row = jnp.maximum(m_row, dc)
        w_inter = jnp.exp(inter - m_row)
        w_intra = [jnp.exp(dc - m_row) for dc in dcols]
        floor = jnp.exp(-m_row)
        m_ref[pl.ds(b, 1), :] = m_row[t_new - 1:t_new]
        for hd in range(ML_HEADS):
            hl = slice(hd, hd + 1)
            qh = q_ref[b, :, hd * ML_DK:(hd + 1) * ML_DK]
            kh = k_ref[b, :, hd * ML_DK:(hd + 1) * ML_DK] * (ML_DK ** -0.5)
            vh = v_ref[b, :, hd * ML_DV:(hd + 1) * ML_DV]
            cm = c0_ref[b, hd]
            nrow = n0_ref[b, hl, :]
            qhb, khb = qh.astype(BF16), kh.astype(BF16)
            qk = lax.dot_general(qhb, khb, (((1,), (1,)), ((), ())), preferred_element_type=F32)
            wi = w_inter[:, hl]
            num = wi * jnp.dot(qhb, cm.astype(BF16), preferred_element_type=F32)
            den = wi * jnp.sum(qh * nrow, axis=-1, keepdims=True)
            kw_rows = []
            for s in range(t_new):
                coef = qk[:, s:s + 1] * w_intra[s][:, hl]
                num = num + coef * vh[s:s + 1, :]
                den = den + coef
                kw_rows.append(kh[s:s + 1, :] * w_intra[s][t_new - 1:t_new, hl])
            h = num / jnp.maximum(jnp.abs(den), floor[:, hl])
            cols = slice(hd * ML_DV, (hd + 1) * ML_DV)
            h_ref[b, :, cols] = _head_out(h, o_ref[b, :, cols], gout_ref[:, cols])
            kw = jnp.concatenate(kw_rows, axis=0)
            upd = lax.dot_general(kw.astype(BF16), vh.astype(BF16), (((0,), (0,)), ((), ())),
                                  preferred_element_type=F32)
            decay = w_inter[t_new - 1:t_new, hl]
            c_ref[b, hd] = decay * cm + upd
            n_ref[b, hl, :] = decay * nrow + jnp.sum(kw, axis=0, keepdims=True)
        return carry

    lax.fori_loop(0, bs, one_sequence, 0)


def _mlstm_sample(proj, bias, gout, c0, n0, m0, layer, prev):
    bsz, t_new, _ = proj.shape
    bs = SAMPLE_BLOCK
    blk = lambda w, j: pl.BlockSpec((bs, t_new, w), lambda i: (i, 0, j))
    cst = pl.BlockSpec((None, bs, ML_HEADS, ML_DK, ML_DV), lambda i: (layer, i, 0, 0, 0))
    nst = pl.BlockSpec((None, bs, ML_HEADS, ML_DK), lambda i: (layer, i, 0, 0))
    mst = pl.BlockSpec((None, bs, LANES), lambda i: (layer, i, 0))
    prev, prev_specs, aliases = _alias_inputs(prev, 1)
    return pl.pallas_call(
        _skip_refs(functools.partial(_mlstm_sample_kernel, bs=bs, t_new=t_new), len(prev)),
        grid=(bsz // bs,),
        in_specs=prev_specs + [
            blk(ML_QK, 0), blk(ML_QK, ODD_K_BLOCK), blk(ML_WIDTH, ODD_V_BLOCK), blk(ML_WIDTH, ODD_O_BLOCK),
            blk(LANES, ODD_G_BLOCK), _layer_spec((1, LANES), layer), _layer_spec((1, ML_WIDTH), layer),
            cst, nst, mst],
        out_specs=[blk(ML_WIDTH, 0), cst, nst, mst],
        out_shape=[jax.ShapeDtypeStruct((bsz, t_new, ML_WIDTH), BF16),
                   jax.ShapeDtypeStruct(c0.shape, F32), jax.ShapeDtypeStruct(n0.shape, F32),
                   jax.ShapeDtypeStruct(m0.shape, F32)],
        input_output_aliases=aliases,
        compiler_params=_cparams("parallel"),
        name="mlstm_sample",
    )(*prev, proj, proj, proj, proj, proj, bias, gout, c0, n0, m0)


def _pad_lanes(x):
    return jnp.pad(x, [(0, 0)] * (x.ndim - 1) + [(0, LANES - x.shape[-1])])


def kernel(x_prompt, x_sample, cache_k, cache_v, state_ssm_re, state_ssm_im, state_mlstm_c, state_mlstm_n, state_mlstm_m, norm_mix, norm_ffn, w_in_even, q_norm, k_norm, attn_sinks, s5_a_re, s5_a_im, s5_log_dt, s5_b_re, s5_b_im, s5_c_re, s5_c_im, s5_d, s5_w_glu, s5_b_glu, w_out_even, w_in_odd, ml_b_i, ml_b_f, ml_out_norm, w_out_odd, w_gate, w_up, w_down):
    bp, lp, _ = x_prompt.shape
    bsm, ls, _ = x_sample.shape
    yp = x_prompt.reshape(bp * lp, D_MODEL)
    ys = x_sample.reshape(bsm * ls, D_MODEL)
    tab_p = _rope_tables(jnp.arange(lp))
    tab_s = tuple(jnp.tile(t, (SAMPLE_BLOCK, 1)) for t in _rope_tables(PAST_LEN + jnp.arange(ls)))
    n_even, n_odd = w_in_even.shape[0], w_in_odd.shape[0]

    g_mix = norm_mix.reshape(DEPTH, 1, D_MODEL)
    g_ffn = norm_ffn.reshape(DEPTH, 1, D_MODEL)
    wg, wu, wd = w_gate.astype(BF16), w_up.astype(BF16), w_down.astype(BF16)
    kv0, u0 = ATTN_WIDTH, ATTN_WIDTH + 2 * KV_WIDTH
    w_in_e = jnp.concatenate([w_in_even[..., :kv0], w_in_even[..., u0:], w_in_even[..., kv0:u0]], axis=-1).astype(BF16)
    w_out_e = w_out_even.astype(BF16)
    gq = jnp.tile(q_norm, (1, LANES // HEAD_DIM)).reshape(n_even, 1, LANES)
    gk = jnp.tile(k_norm, (1, LANES // HEAD_DIM)).reshape(n_even, 1, LANES)
    prm = _s5_params(s5_a_re, s5_a_im, s5_log_dt, s5_b_re, s5_b_im, s5_c_re, s5_c_im, s5_d, s5_w_glu, s5_b_glu)
    w_in_o = jnp.pad(w_in_odd, ((0, 0), (0, 0), (0, ODD_IN_PAD - ODD_IN))).astype(BF16)
    w_out_o = w_out_odd.astype(BF16)
    ml_bias = _pad_lanes(jnp.concatenate([ml_b_i, ml_b_f], axis=-1)).reshape(n_odd, 1, LANES)
    ml_gout = ml_out_norm.reshape(n_odd, 1, ML_WIDTH)
    ck = cache_k.reshape(n_even, bsm, WINDOW, KV_WIDTH)
    cv = cache_v.reshape(n_even, bsm, WINDOW, KV_WIDTH)
    h0r = state_ssm_re.reshape(n_even, bsm, S5_FLAT)
    h0i = state_ssm_im.reshape(n_even, bsm, S5_FLAT)
    m0 = _pad_lanes(state_mlstm_m)

    p_attn = p_ssm = p_ml = s_attn = s_ssm = s_ml = None
    for layer in range(DEPTH):
        ffn = (layer, g_ffn, wg, wu, wd)
        if layer % 2 == 0:
            e = layer // 2
            proj = _norm_matmul(yp, g_mix, layer, w_in_e, e)
            proj3 = proj.reshape(bp, lp, -1)
            attn, *p_attn = _attn_prompt(proj3, tab_p, gq, gk, attn_sinks, e, p_attn)
            ssm, *p_ssm = _s5_prompt(proj3, prm, e, p_ssm)
            yp = _mix_ffn(yp, [attn.reshape(bp * lp, -1), ssm.reshape(bp * lp, -1)], w_out_e, e, *ffn)
            proj = _norm_matmul(ys, g_mix, layer, w_in_e, e)
            attn, *s_attn = _attn_sample(proj, ck, cv, tab_s, gq, gk, attn_sinks, ls, e, s_attn)
            u = proj[:, ATTN_WIDTH:ATTN_WIDTH + S5_WIDTH]
            u_tm = u.reshape(bsm, ls, S5_WIDTH).transpose(1, 0, 2).reshape(ls * bsm, S5_WIDTH)
            ssm_tm, *s_ssm = _s5_sample(u_tm, h0r, h0i, prm, ls, e, s_ssm)
            ssm = ssm_tm.reshape(ls, bsm, S5_WIDTH).transpose(1, 0, 2).reshape(bsm * ls, S5_WIDTH)
            ys = _mix_ffn(ys, [attn, ssm], w_out_e, e, *ffn)
        else:
            o = layer // 2
            proj = _norm_matmul(yp, g_mix, layer, w_in_o, o)
            hh, *p_ml = _mlstm_prompt(proj.reshape(bp, lp, -1), ml_bias, ml_gout, o, p_ml)
            yp = _mix_ffn(yp, [hh.reshape(bp * lp, -1)], w_out_o, o, *ffn)
            proj = _norm_matmul(ys, g_mix, layer, w_in_o, o)
            hh, *s_ml = _mlstm_sample(proj.reshape(bsm, ls, -1), ml_bias, ml_gout, state_mlstm_c, state_mlstm_n, m0,
                                      o, s_ml)
            ys = _mix_ffn(ys, [hh.reshape(bsm * ls, -1)], w_out_o, o, *ffn)
    heads = lambda a: a.reshape(a.shape[:3] + (KV_HEADS, HEAD_DIM))
    groups = lambda a: a.reshape(a.shape[:2] + (S5_GROUPS, S5_STATE))
    return (yp.reshape(bp, lp, D_MODEL), ys.reshape(bsm, ls, D_MODEL),
            heads(p_attn[0]), heads(p_attn[1]), groups(p_ssm[0]), groups(p_ssm[1]),
            p_ml[0], p_ml[1][..., 0], p_ml[2][..., 0],
            heads(s_attn[0]), heads(s_attn[1]), groups(s_ssm[0]), groups(s_ssm[1]),
            s_ml[0], s_ml[1], s_ml[2][..., :ML_HEADS])
```

```python
import functools

import numpy as np

import jax
import jax.numpy as jnp
from jax import lax
from jax.experimental import pallas as pl
from jax.experimental.pallas import tpu as pltpu

F32 = jnp.float32
BF16 = jnp.bfloat16

D_MODEL = 1024
DEPTH = 4
PAST_LEN = 8192
WINDOW = 128
ATTN_HEADS = 8
KV_HEADS = 2
HEAD_DIM = 64
ATTN_WIDTH = ATTN_HEADS * HEAD_DIM
KV_WIDTH = KV_HEADS * HEAD_DIM
ROT_DIM = HEAD_DIM // 4
ROPE_THETA = 500000.0
S5_GROUP = 16
S5_WIDTH = D_MODEL // 2
S5_GROUPS = S5_WIDTH // S5_GROUP
S5_STATE = 64
S5_FLAT = S5_GROUPS * S5_STATE
ML_HEADS = 8
ML_DV = D_MODEL // ML_HEADS
ML_DK = ML_DV // 2
ML_QK = ML_HEADS * ML_DK
ML_WIDTH = ML_HEADS * ML_DV
ODD_IN = 2 * ML_QK + 2 * ML_WIDTH + 2 * ML_HEADS
ODD_IN_PAD = 2 * ML_QK + 2 * ML_WIDTH + 128
D_FF = 2816
EPS = 1e-6

LANES = 128
SUBLANES = 8
ROW_TILE = 512
FF_TILE = 256
S5_CHUNK = 64
ML_CHUNK = 128
SAMPLE_BLOCK = 8
VMEM_LIMIT = 56 * 1024 * 1024

NEG_INF = float("-inf")


def _cparams(*sem):
    return pltpu.CompilerParams(dimension_semantics=sem, vmem_limit_bytes=VMEM_LIMIT)


def _const_spec(shape):
    zeros = (0,) * len(shape)
    return pl.BlockSpec(shape, lambda *_: zeros, pipeline_mode=pl.Buffered(1))


def _layer_spec(shape, layer):
    zeros = (0,) * len(shape)
    return pl.BlockSpec((None,) + tuple(shape), lambda *_: (layer,) + zeros, pipeline_mode=pl.Buffered(1))


def _skip_refs(body, n_skip):
    if n_skip == 0:
        return body

    def wrapped(*refs):
        return body(*refs[n_skip:])

    return wrapped


def _alias_inputs(prev, first_state_out):
    prev = () if prev is None else tuple(prev)
    specs = [pl.BlockSpec(memory_space=pl.ANY) for _ in prev]
    aliases = {i: first_state_out + i for i in range(len(prev))}
    return prev, specs, aliases


def _rms(x, g):
    ms = jnp.mean(x * x, axis=-1, keepdims=True)
    return x * lax.rsqrt(ms + EPS) * g


def _split3(a):
    a1 = a.astype(BF16)
    r1 = a - a1.astype(F32)
    a2 = r1.astype(BF16)
    a3 = (r1 - a2.astype(F32)).astype(BF16)
    return a1, a2, a3


def _log_sigmoid(x):
    return jnp.minimum(x, 0.0) - jnp.log1p(jnp.exp(-jnp.abs(x)))


def _sigmoid(x):
    return 1.0 / (1.0 + jnp.exp(-x))


def _norm_matmul_kernel(x_ref, g_ref, w_ref, o_ref):
    h = _rms(x_ref[...], g_ref[...]).astype(BF16)
    o_ref[...] = jnp.dot(h, w_ref[...], preferred_element_type=F32)


def _norm_matmul(x, g, layer, w, widx):
    n, d = x.shape
    m = w.shape[2]
    tm = min(ROW_TILE, n)
    return pl.pallas_call(
        _norm_matmul_kernel,
        grid=(n // tm,),
        in_specs=[pl.BlockSpec((tm, d), lambda i: (i, 0)), _layer_spec((1, d), layer), _layer_spec((d, m), widx)],
        out_specs=pl.BlockSpec((tm, m), lambda i: (i, 0)),
        out_shape=jax.ShapeDtypeStruct((n, m), F32),
        compiler_params=_cparams("parallel"),
        name="norm_matmul",
    )(x, g, w)


def _mix_ffn_kernel(*refs, n_mix):
    x_ref = refs[0]
    a_refs = refs[1:1 + n_mix]
    wo_ref, g_ref, wg_ref, wu_ref, wd_ref, o_ref, act_ref = refs[1 + n_mix:]
    y = x_ref[...]
    off = 0
    for a_ref in a_refs:
        ka = a_ref.shape[1]
        y = y + jnp.dot(a_ref[...], wo_ref[off:off + ka, :], preferred_element_type=F32)
        off += ka
    h = _rms(y, g_ref[...]).astype(BF16)
    for f in range(D_FF // FF_TILE):
        cols = slice(f * FF_TILE, (f + 1) * FF_TILE)
        gate = jnp.dot(h, wg_ref[:, cols], preferred_element_type=F32)
        up = jnp.dot(h, wu_ref[:, cols], preferred_element_type=F32)
        act_ref[:, cols] = (gate * _sigmoid(gate) * up).astype(BF16)
    o_ref[...] = y + jnp.dot(act_ref[...], wd_ref[...], preferred_element_type=F32)


def _mix_ffn(x, mixes, w_out, oidx, layer, g_ffn, wg, wu, wd):
    n, d = x.shape
    tm = min(ROW_TILE, n)
    row = lambda i: (i, 0)
    in_specs = [pl.BlockSpec((tm, d), row)]
    in_specs += [pl.BlockSpec((tm, a.shape[1]), row) for a in mixes]
    in_specs += [_layer_spec(w_out.shape[1:], oidx), _layer_spec((1, d), layer), _layer_spec(wg.shape[1:], layer),
                 _layer_spec(wu.shape[1:], layer), _layer_spec(wd.shape[1:], layer)]
    return pl.pallas_call(
        functools.partial(_mix_ffn_kernel, n_mix=len(mixes)),
        grid=(n // tm,),
        in_specs=in_specs,
        out_specs=pl.BlockSpec((tm, d), row),
        out_shape=jax.ShapeDtypeStruct((n, d), F32),
        scratch_shapes=[pltpu.VMEM((tm, D_FF), BF16)],
        compiler_params=_cparams("parallel"),
        name="mix_ffn",
    )(x, *mixes, w_out, g_ffn, wg, wu, wd)


def _head_ones():
    r = lax.broadcasted_iota(jnp.int32, (LANES, LANES), 0) // HEAD_DIM
    c = lax.broadcasted_iota(jnp.int32, (LANES, LANES), 1) // HEAD_DIM
    return jnp.where(r == c, 1.0, 0.0).astype(BF16)


def _qk_prep(x, g, ones, ct, sa, sb):
    x2 = x * x
    hi = x2.astype(BF16)
    lo = (x2 - hi.astype(F32)).astype(BF16)
    ss = jnp.dot(hi, ones, preferred_element_type=F32) + jnp.dot(lo, ones, preferred_element_type=F32)
    xn = x * lax.rsqrt(ss * (1.0 / HEAD_DIM) + EPS) * g
    return xn * ct + pltpu.roll(xn, LANES - ROT_DIM // 2, 1) * sa + pltpu.roll(xn, ROT_DIM // 2, 1) * sb


def _rope_tables(pos):
    half = ROT_DIM // 2
    inv = jnp.power(jnp.float32(ROPE_THETA), -jnp.arange(half, dtype=F32) / half)
    ang = pos.astype(F32)[:, None] * inv[None, :]
    cos, sin = jnp.cos(ang), jnp.sin(ang)
    n = pos.shape[0]
    one = jnp.ones((n, HEAD_DIM - ROT_DIM), F32)
    zero = jnp.zeros((n, HEAD_DIM - ROT_DIM), F32)
    z8 = jnp.zeros((n, half), F32)
    ct = jnp.concatenate([cos, cos, one], axis=1)
    sa = jnp.concatenate([-sin, z8, zero], axis=1)
    sb = jnp.concatenate([z8, sin, zero], axis=1)
    tile = lambda t: jnp.concatenate([t, t], axis=1)
    return tile(ct), tile(sa), tile(sb)


ATTN_SEQS = 2
ATTN_QCHUNKS = ATTN_WIDTH // LANES
ATTN_HEAD_ORDER = tuple(h * ATTN_QCHUNKS + j for j in range(ATTN_QCHUNKS) for h in range(KV_HEADS))


def _attn_prompt_kernel(q_ref, kv_ref, ct_ref, sa_ref, sb_ref, gq_ref, gk_ref, sink_ref,
                        o_ref, pk_ref, pv_ref, kprev, vprev, *, nb, layer, nseq):
    i = pl.program_id(1)

    @pl.when(i == 0)
    def _():
        kprev[...] = jnp.zeros_like(kprev)
        vprev[...] = jnp.zeros_like(vprev)

    ones = _head_ones()
    ct, sa, sb = ct_ref[...], sa_ref[...], sb_ref[...]
    r = lax.broadcasted_iota(jnp.int32, (WINDOW, 2 * WINDOW), 0)
    c = lax.broadcasted_iota(jnp.int32, (WINDOW, 2 * WINDOW), 1)
    rel = r + WINDOW - c
    mask = (rel >= 0) & (rel <= WINDOW) & ((c >= WINDOW) | (i > 0))
    lane = lax.broadcasted_iota(jnp.int32, (WINDOW, LANES), 1)
    group0 = lane < HEAD_DIM
    v_ones = jnp.ones((2 * WINDOW, LANES), BF16)
    nq = ATTN_QCHUNKS
    st = [dict() for _ in range(nseq)]

    def prep(sq):
        d = st[sq]
        kv = kv_ref[sq]
        d["kn"] = _qk_prep(kv[:, :KV_WIDTH], gk_ref[...], ones, ct, sa, sb)
        d["v"] = kv[:, KV_WIDTH:]
        d["qn"] = [_qk_prep(q_ref[sq, :, j * LANES:(j + 1) * LANES], gq_ref[...], ones, ct, sa, sb)
                   * (HEAD_DIM ** -0.5) for j in range(nq)]
        d["kcat"] = jnp.concatenate([kprev[sq], d["kn"]], axis=0).astype(BF16)
        d["vaug"] = jnp.concatenate([jnp.concatenate([vprev[sq], d["v"]], axis=0).astype(BF16), v_ones], axis=1)
        kprev[sq] = d["kn"]
        vprev[sq] = d["v"]

    def scores(sq, h):
        d = st[sq]
        keep = group0 if h == 0 else jnp.logical_not(group0)
        qs = jnp.concatenate([jnp.where(keep, qj, 0.0) for qj in d["qn"]], axis=0).astype(BF16)
        d["s", h] = lax.dot_general(qs, d["kcat"], (((1,), (1,)), ((), ())), preferred_element_type=F32)

    def softmax_pv(sq, h):
        d = st[sq]
        s = d.pop(("s", h))
        ps, corr = [], []
        for j in range(nq):
            sg = jnp.where(mask, s[j * WINDOW:(j + 1) * WINDOW], NEG_INF)
            sink = sink_ref[layer, h * nq + j]
            m = jnp.maximum(jnp.max(sg, axis=-1, keepdims=True), sink)
            ps.append(jnp.exp(sg - m).astype(BF16))
            corr.append(jnp.exp(sink - m))
        o = jnp.dot(jnp.concatenate(ps, axis=0), d["vaug"], preferred_element_type=F32)
        d["o", h] = [o[j * WINDOW:(j + 1) * WINDOW, :LANES] / (o[j * WINDOW:(j + 1) * WINDOW, LANES:] + corr[j])
                     for j in range(nq)]

    def finish(sq):
        d = st[sq]
        o_ref[sq] = jnp.concatenate([jnp.where(group0, d["o", 0][j], d["o", 1][j]) for j in range(nq)],
                                    axis=1).astype(BF16)

    for sq in range(nseq):
        prep(sq)
    for sq in range(nseq):
        scores(sq, 0)
        scores(sq, 1)
    for sq in range(nseq):
        softmax_pv(sq, 0)
        softmax_pv(sq, 1)
        finish(sq)

    @pl.when(i == nb - 1)
    def _():
        for sq in range(nseq):
            pk_ref[sq] = st[sq]["kn"]
            pv_ref[sq] = st[sq]["v"]


def _attn_prompt(proj, tables, gq, gk, sinks, layer, prev):
    bsz, seq, _ = proj.shape
    nb = seq // WINDOW
    nseq = ATTN_SEQS
    n_layers = gq.shape[0]
    tab = pl.BlockSpec((WINDOW, LANES), lambda b, i: (i, 0))
    prev, prev_specs, aliases = _alias_inputs(prev, 1)
    win = pl.BlockSpec((None, nseq, WINDOW, KV_WIDTH), lambda b, i: (layer, b, 0, 0))
    win_shape = jax.ShapeDtypeStruct((n_layers, bsz, WINDOW, KV_WIDTH), F32)
    return pl.pallas_call(
        _skip_refs(functools.partial(_attn_prompt_kernel, nb=nb, layer=layer, nseq=nseq), len(prev)),
        grid=(bsz // nseq, nb),
        in_specs=prev_specs + [
            pl.BlockSpec((nseq, WINDOW, ATTN_WIDTH), lambda b, i: (b, i, 0)),
            pl.BlockSpec((nseq, WINDOW, 2 * KV_WIDTH), lambda b, i: (b, i, EVEN_KV_BLOCK)),
            tab, tab, tab, _layer_spec((1, LANES), layer), _layer_spec((1, LANES), layer),
            pl.BlockSpec(memory_space=pltpu.SMEM)],
        out_specs=[pl.BlockSpec((nseq, WINDOW, ATTN_WIDTH), lambda b, i: (b, i, 0)), win, win],
        out_shape=[jax.ShapeDtypeStruct((bsz, seq, ATTN_WIDTH), BF16), win_shape, win_shape],
        input_output_aliases=aliases,
        scratch_shapes=[pltpu.VMEM((nseq, WINDOW, KV_WIDTH), F32), pltpu.VMEM((nseq, WINDOW, KV_WIDTH), F32)],
        compiler_params=_cparams("parallel", "arbitrary"),
        name="attn_prompt",
    )(*prev, proj, proj, *tables, gq, gk, sinks)


EVEN_U_BLOCK = ATTN_WIDTH // S5_WIDTH
EVEN_KV_BLOCK = (ATTN_WIDTH + S5_WIDTH) // (2 * KV_WIDTH)
KALL_ROWS = WINDOW + SUBLANES


def _attn_sample_kernel(q_ref, kv_ref, ck_ref, cv_ref, ct_ref, sa_ref, sb_ref, gq_ref, gk_ref, sink_ref,
                        o_ref, nk_ref, nv_ref, kall, vall, *, bs, t_new, layer):
    ones = _head_ones()
    ct, sa, sb = ct_ref[...], sa_ref[...], sb_ref[...]
    kv = kv_ref[...]
    kn = _qk_prep(kv[:, :KV_WIDTH], gk_ref[...], ones, ct, sa, sb)
    v = kv[:, KV_WIDTH:]
    qn = jnp.concatenate(
        [_qk_prep(q_ref[:, c * LANES:(c + 1) * LANES], gq_ref[...], ones, ct, sa, sb)
         for c in range(ATTN_WIDTH // LANES)], axis=1) * (HEAD_DIM ** -0.5)
    grp = ATTN_HEADS // KV_HEADS
    rows = grp * t_new
    r = lax.broadcasted_iota(jnp.int32, (rows, KALL_ROWS), 0)
    c = lax.broadcasted_iota(jnp.int32, (rows, KALL_ROWS), 1)
    t = r % t_new
    mask = (c >= t) & (c <= t + WINDOW)
    rg = lax.broadcasted_iota(jnp.int32, (rows, 1), 0) // t_new
    pad = jnp.zeros((KALL_ROWS - WINDOW - t_new, KV_WIDTH), F32)
    for b in range(bs):
        rb = slice(b * t_new, (b + 1) * t_new)
        ck, cv = ck_ref[b], cv_ref[b]
        kall[0:WINDOW, :] = ck
        vall[0:WINDOW, :] = cv
        kall[WINDOW:KALL_ROWS, :] = jnp.concatenate([kn[rb], pad], axis=0)
        vall[WINDOW:KALL_ROWS, :] = jnp.concatenate([v[rb], pad], axis=0)
        nk_ref[b] = pltpu.roll(ck, WINDOW - t_new, 0)
        nv_ref[b] = pltpu.roll(cv, WINDOW - t_new, 0)
        nk_ref[b, WINDOW - t_new:WINDOW, :] = kn[rb]
        nv_ref[b, WINDOW - t_new:WINDOW, :] = v[rb]
        kb = kall[...].astype(BF16)
        vb = vall[...].astype(BF16)
        outs = [None] * ATTN_HEADS
        for h in range(KV_HEADS):
            kh = kb[:, h * HEAD_DIM:(h + 1) * HEAD_DIM]
            vh = vb[:, h * HEAD_DIM:(h + 1) * HEAD_DIM]
            qs = jnp.concatenate(
                [qn[rb, (g * KV_HEADS + h) * HEAD_DIM:(g * KV_HEADS + h + 1) * HEAD_DIM] for g in range(grp)],
                axis=0).astype(BF16)
            s = lax.dot_general(qs, kh, (((1,), (1,)), ((), ())), preferred_element_type=F32)
            s = jnp.where(mask, s, NEG_INF)
            sink = jnp.zeros((rows, 1), F32)
            for g in range(grp):
                sink = jnp.where(rg == g, sink_ref[layer, h * grp + g], sink)
            m = jnp.maximum(jnp.max(s, axis=-1, keepdims=True), sink)
            p = jnp.exp(s - m)
            den = jnp.sum(p, axis=-1, keepdims=True) + jnp.exp(sink - m)
            o = jnp.dot((p / den).astype(BF16), vh, preferred_element_type=F32)
            for g in range(grp):
                outs[g * KV_HEADS + h] = o[g * t_new:(g + 1) * t_new]
        o_ref[rb, :] = jnp.concatenate(outs, axis=1).astype(BF16)


def _attn_sample(proj, cache_k, cache_v, tables, gq, gk, sinks, t_new, layer, prev):
    n = proj.shape[0]
    bsz = n // t_new
    bs = SAMPLE_BLOCK
    rows = bs * t_new
    row = lambda i: (i, 0)
    cache = pl.BlockSpec((None, bs, WINDOW, KV_WIDTH), lambda i: (layer, i, 0, 0))
    prev, prev_specs, aliases = _alias_inputs(prev, 1)
    return pl.pallas_call(
        _skip_refs(functools.partial(_attn_sample_kernel, bs=bs, t_new=t_new, layer=layer), len(prev)),
        grid=(bsz // bs,),
        in_specs=prev_specs + [
            pl.BlockSpec((rows, ATTN_WIDTH), row),
            pl.BlockSpec((rows, 2 * KV_WIDTH), lambda i: (i, EVEN_KV_BLOCK)),
            cache, cache,
            _const_spec((rows, LANES)), _const_spec((rows, LANES)), _const_spec((rows, LANES)),
            _layer_spec((1, LANES), layer), _layer_spec((1, LANES), layer),
            pl.BlockSpec(memory_space=pltpu.SMEM)],
        out_specs=[pl.BlockSpec((rows, ATTN_WIDTH), row), cache, cache],
        out_shape=[jax.ShapeDtypeStruct((n, ATTN_WIDTH), BF16),
                   jax.ShapeDtypeStruct(cache_k.shape, F32), jax.ShapeDtypeStruct(cache_v.shape, F32)],
        input_output_aliases=aliases,
        scratch_shapes=[pltpu.VMEM((KALL_ROWS, KV_WIDTH), F32), pltpu.VMEM((KALL_ROWS, KV_WIDTH), F32)],
        compiler_params=_cparams("parallel"),
        name="attn_sample",
    )(*prev, proj, proj, cache_k, cache_v, *tables, gq, gk, sinks)


S5_UCHUNKS = S5_WIDTH // LANES
S5_SUB = S5_FLAT // S5_UCHUNKS
S5_SCHUNKS = S5_FLAT // LANES


def _s5_tail(y, wglu_ref, bglu_ref):
    g = 0.5 * y * (1.0 + lax.erf(y * (2.0 ** -0.5)))
    z = jnp.dot(g.astype(BF16), wglu_ref[...], preferred_element_type=F32) + bglu_ref[...]
    return g * _sigmoid(z)


def _s5_prompt_kernel(u_ref, wb_ref, wc_ref, lam_ref, d_ref, wglu_ref, bglu_ref,
                      o_ref, sr_ref, si_ref, xs, hst, *, nbatch, tc):
    rows = nbatch * tc

    @pl.when(pl.program_id(1) == 0)
    def _():
        hst[...] = jnp.zeros_like(hst)

    u = jnp.swapaxes(u_ref[...], 0, 1).reshape(rows, S5_WIDTH)
    ub = u.astype(BF16)
    for cc in range(S5_UCHUNKS):
        res = jnp.dot(ub[:, cc * LANES:(cc + 1) * LANES], wb_ref[cc], preferred_element_type=F32)
        for j in range(S5_SUB // LANES):
            xs[cc * 4 + j] = res[:, j * LANES:(j + 1) * LANES]
            xs[S5_SCHUNKS + cc * 4 + j] = res[:, S5_SUB + j * LANES:S5_SUB + (j + 1) * LANES]

    def step(t, carry):
        new_r, new_i = [], []
        for k in range(S5_SCHUNKS):
            hr, hi = carry[k], carry[S5_SCHUNKS + k]
            lr, li = lam_ref[k], lam_ref[S5_SCHUNKS + k]
            idx = pl.ds(pl.multiple_of(t * nbatch, nbatch), nbatch)
            nr = lr * hr - li * hi + xs[k, idx, :]
            ni = lr * hi + li * hr + xs[S5_SCHUNKS + k, idx, :]
            xs[k, idx, :] = nr
            xs[S5_SCHUNKS + k, idx, :] = ni
            new_r.append(nr)
            new_i.append(ni)
        return tuple(new_r + new_i)

    fin = lax.fori_loop(0, tc, step, tuple(hst[k] for k in range(2 * S5_SCHUNKS)))
    for k in range(2 * S5_SCHUNKS):
        hst[k] = fin[k]
    sr_ref[...] = jnp.concatenate(fin[:S5_SCHUNKS], axis=1)
    si_ref[...] = jnp.concatenate(fin[S5_SCHUNKS:], axis=1)

    ys = []
    for cc in range(S5_UCHUNKS):
        s = jnp.concatenate([xs[cc * 4 + j] for j in range(4)]
                            + [xs[S5_SCHUNKS + cc * 4 + j] for j in range(4)], axis=1).astype(BF16)
        cols = slice(cc * LANES, (cc + 1) * LANES)
        ys.append(jnp.dot(s, wc_ref[cc], preferred_element_type=F32) + d_ref[:, cols] * u[:, cols])
    out = _s5_tail(jnp.concatenate(ys, axis=1), wglu_ref, bglu_ref)
    o_ref[...] = jnp.swapaxes(out.reshape(tc, nbatch, S5_WIDTH), 0, 1).astype(BF16)


def _s5_prompt(proj, prm, layer, prev):
    bsz, seq, _ = proj.shape
    nbatch, tc = SUBLANES, S5_CHUNK
    n_layers = prm["wb"].shape[0]
    st = pl.BlockSpec((None, nbatch, S5_FLAT), lambda b, c: (layer, b, 0))
    st_shape = jax.ShapeDtypeStruct((n_layers, bsz, S5_FLAT), F32)
    prev, prev_specs, aliases = _alias_inputs(prev, 1)
    names = ("wb", "wc", "lam8", "d", "wglu", "bglu")
    return pl.pallas_call(
        _skip_refs(functools.partial(_s5_prompt_kernel, nbatch=nbatch, tc=tc), len(prev)),
        grid=(bsz // nbatch, seq // tc),
        in_specs=prev_specs + [pl.BlockSpec((nbatch, tc, S5_WIDTH), lambda b, c: (b, c, EVEN_U_BLOCK))]
        + [_layer_spec(prm[k].shape[1:], layer) for k in names],
        out_specs=[pl.BlockSpec((nbatch, tc, S5_WIDTH), lambda b, c: (b, c, 0)), st, st],
        out_shape=[jax.ShapeDtypeStruct((bsz, seq, S5_WIDTH), BF16), st_shape, st_shape],
        input_output_aliases=aliases,
        scratch_shapes=[pltpu.VMEM((2 * S5_SCHUNKS, nbatch * tc, LANES), F32),
                        pltpu.VMEM((2 * S5_SCHUNKS, nbatch, LANES), F32)],
        compiler_params=_cparams("parallel", "arbitrary"),
        name="s5_prompt",
    )(*prev, proj, *[prm[k] for k in names])


def _s5_sample_kernel(u_ref, wb_ref, wc_ref, lr_ref, li_ref, d_ref, wglu_ref, bglu_ref, h0r_ref, h0i_ref,
                      o_ref, sr_ref, si_ref, xr, xi, *, nseq, t_new):
    u = u_ref[...]
    ub = u.astype(BF16)
    for cc in range(S5_UCHUNKS):
        res = jnp.dot(ub[:, cc * LANES:(cc + 1) * LANES], wb_ref[cc], preferred_element_type=F32)
        xr[:, cc * S5_SUB:(cc + 1) * S5_SUB] = res[:, :S5_SUB]
        xi[:, cc * S5_SUB:(cc + 1) * S5_SUB] = res[:, S5_SUB:]
    lr, li = lr_ref[...], li_ref[...]
    hr, hi = h0r_ref[...], h0i_ref[...]
    for t in range(t_new):
        rows = slice(t * nseq, (t + 1) * nseq)
        nr = lr * hr - li * hi + xr[rows, :]
        ni = lr * hi + li * hr + xi[rows, :]
        xr[rows, :] = nr
        xi[rows, :] = ni
        hr, hi = nr, ni
    sr_ref[...] = hr
    si_ref[...] = hi
    ys = []
    for cc in range(S5_UCHUNKS):
        sc = slice(cc * S5_SUB, (cc + 1) * S5_SUB)
        s = jnp.concatenate([xr[:, sc], xi[:, sc]], axis=1).astype(BF16)
        cols = slice(cc * LANES, (cc + 1) * LANES)
        ys.append(jnp.dot(s, wc_ref[cc], preferred_element_type=F32) + d_ref[:, cols] * u[:, cols])
    o_ref[...] = _s5_tail(jnp.concatenate(ys, axis=1), wglu_ref, bglu_ref).astype(BF16)


def _s5_sample(u_tm, h0r, h0i, prm, t_new, layer, prev):
    n = u_tm.shape[0]
    nseq = n // t_new
    names = ("wb", "wc", "lr", "li", "d", "wglu", "bglu")
    st = pl.BlockSpec((None, nseq, S5_FLAT), lambda i: (layer, 0, 0))
    prev, prev_specs, aliases = _alias_inputs(prev, 1)
    return pl.pallas_call(
        _skip_refs(functools.partial(_s5_sample_kernel, nseq=nseq, t_new=t_new), len(prev)),
        grid=(1,),
        in_specs=prev_specs + [_const_spec(u_tm.shape)] + [_layer_spec(prm[k].shape[1:], layer) for k in names]
        + [_layer_spec((nseq, S5_FLAT), layer), _layer_spec((nseq, S5_FLAT), layer)],
        out_specs=[pl.BlockSpec((n, S5_WIDTH), lambda i: (0, 0)), st, st],
        out_shape=[jax.ShapeDtypeStruct((n, S5_WIDTH), BF16),
                   jax.ShapeDtypeStruct(h0r.shape, F32), jax.ShapeDtypeStruct(h0i.shape, F32)],
        input_output_aliases=aliases,
        scratch_shapes=[pltpu.VMEM((n, S5_FLAT), F32), pltpu.VMEM((n, S5_FLAT), F32)],
        compiler_params=_cparams("arbitrary"),
        name="s5_sample",
    )(*prev, u_tm, *[prm[k] for k in names], h0r, h0i)


def _s5_params(a_re, a_im, log_dt, b_re, b_im, c_re, c_im, d_skip, w_glu, b_glu):
    nl = a_re.shape[0]
    dt = jnp.exp(log_dt)
    mag = jnp.exp(a_re * dt)
    lr, li = mag * jnp.cos(a_im * dt), mag * jnp.sin(a_im * dt)
    den = a_re * a_re + a_im * a_im
    cr = ((lr - 1.0) * a_re + li * a_im) / den
    ci = (li * a_re - (lr - 1.0) * a_im) / den
    bbr = cr[..., None] * b_re - ci[..., None] * b_im
    bbi = cr[..., None] * b_im + ci[..., None] * b_re
    gpc = LANES // S5_GROUP
    eye = jnp.eye(gpc, dtype=F32)

    def in_blocks(bb):
        bb = bb.reshape(nl, S5_UCHUNKS, gpc, S5_STATE, S5_GROUP)
        return jnp.einsum("lcgph,gk->lcghkp", bb, eye).reshape(nl, S5_UCHUNKS, LANES, S5_SUB)

    def out_blocks(cm):
        cm = cm.reshape(nl, S5_UCHUNKS, gpc, S5_GROUP, S5_STATE)
        return jnp.einsum("lcghp,gk->lcgpkh", cm, eye).reshape(nl, S5_UCHUNKS, S5_SUB, LANES)

    wb = jnp.concatenate([in_blocks(bbr), in_blocks(bbi)], axis=3).astype(BF16)
    wc = jnp.concatenate([out_blocks(c_re), -out_blocks(c_im)], axis=2).astype(BF16)
    lr_f, li_f = lr.reshape(nl, 1, S5_FLAT), li.reshape(nl, 1, S5_FLAT)
    lam = jnp.concatenate([lr_f.reshape(nl, S5_SCHUNKS, 1, LANES), li_f.reshape(nl, S5_SCHUNKS, 1, LANES)], axis=1)
    lam8 = jnp.broadcast_to(lam, (nl, 2 * S5_SCHUNKS, SUBLANES, LANES))
    return dict(wb=wb, wc=wc, lam8=lam8, lr=lr_f, li=li_f, d=d_skip.reshape(nl, 1, S5_WIDTH),
                wglu=w_glu.astype(BF16), bglu=b_glu.reshape(nl, 1, S5_WIDTH))


ODD_K_BLOCK = 1
ODD_V_BLOCK = (2 * ML_QK) // ML_WIDTH
ODD_O_BLOCK = ODD_V_BLOCK + 1
ODD_G_BLOCK = (2 * ML_QK + 2 * ML_WIDTH) // LANES
ML_AUG = 2 * ML_DV


def _head_out(h, o, gout):
    hn = h * lax.rsqrt(jnp.mean(h * h, axis=-1, keepdims=True) + EPS) * gout
    return (hn * _sigmoid(o)).astype(BF16)


ODDP_V_BLOCK = 0
ODDP_O_BLOCK = 1
ODDP_Q_BLOCK = (2 * ML_WIDTH) // ML_QK
ODDP_G_BLOCK = (2 * ML_WIDTH + ML_QK) // LANES
ML_SPLIT = 3
ML_PIECE_LANES = 2 * ML_HEADS
ML_SEQS = 2


def _norm_matmul_kt_kernel(x_ref, g_ref, w_ref, wkt_ref, o_ref, kt_ref):
    h = _rms(x_ref[...], g_ref[...]).astype(BF16)
    o_ref[...] = jnp.dot(h, w_ref[...], preferred_element_type=F32)
    kt = lax.dot_general(wkt_ref[...], h, (((1,), (1,)), ((), ())), preferred_element_type=F32)
    kt_ref[...] = kt * (ML_DK ** -0.5)


def _norm_matmul_kt(x, g, layer, w, wkt, widx):
    n, d = x.shape
    m = w.shape[2]
    mk = wkt.shape[1]
    tm = min(ROW_TILE, n)
    return pl.pallas_call(
        _norm_matmul_kt_kernel,
        grid=(n // tm,),
        in_specs=[pl.BlockSpec((tm, d), lambda i: (i, 0)), _layer_spec((1, d), layer), _layer_spec((d, m), widx),
                  _layer_spec((mk, d), widx)],
        out_specs=[pl.BlockSpec((tm, m), lambda i: (i, 0)), pl.BlockSpec((mk, tm), lambda i: (0, i))],
        out_shape=[jax.ShapeDtypeStruct((n, m), F32), jax.ShapeDtypeStruct((mk, n), F32)],
        compiler_params=_cparams("parallel"),
        name="norm_matmul_kt",
    )(x, g, w, wkt)


def _cummax_rows(x):
    n = x.shape[0]
    row = lax.broadcasted_iota(jnp.int32, x.shape, 0)
    shift = 1
    while shift < n:
        x = jnp.maximum(x, jnp.where(row >= shift, pltpu.roll(x, shift, 0), NEG_INF))
        shift *= 2
    return x


def _pieces(x):
    lane = lax.broadcasted_iota(jnp.int32, x.shape, 1)
    xx = x + pltpu.roll(x, ML_PIECE_LANES, 1) + pltpu.roll(x, 2 * ML_PIECE_LANES, 1)
    a1, a2, a3 = _split3(xx)
    return jnp.where(lane < ML_PIECE_LANES, a1, jnp.where(lane < 2 * ML_PIECE_LANES, a2, a3))


def _ml_select_constants():
    mask = np.zeros((ML_HEADS, LANES), np.float32)
    sel = np.zeros((ML_HEADS, LANES, 2 * ML_DV), np.float32)
    for h in range(ML_HEADS):
        for k in range(ML_SPLIT):
            lo, hi = k * ML_PIECE_LANES + h, k * ML_PIECE_LANES + ML_HEADS + h
            mask[h, lo] = mask[h, hi] = 1.0
            sel[h, lo, :ML_DV] = 1.0
            sel[h, hi, ML_DV:] = 1.0
    return jnp.asarray(mask), jnp.asarray(sel, dtype=BF16)


def _mlstm_prompt_kernel(*refs, tc, nchunks, nseq):
    v_ref, o_ref, q_ref, g_ref = refs[:4]
    kt_refs = refs[4:4 + nseq]
    bias_ref, gout_ref, mask_ref, sel_ref, h_ref, c_ref, n_ref, m_ref, caug, mst = refs[4 + nseq:]
    ci = pl.program_id(1)

    @pl.when(ci == 0)
    def _():
        caug[...] = jnp.zeros_like(caug)
        mst[...] = jnp.zeros_like(mst)

    nh = ML_HEADS
    lane = lax.broadcasted_iota(jnp.int32, (tc, LANES), 1)
    lo, hi = lane < nh, (lane >= nh) & (lane < 2 * nh)
    rt = lax.broadcasted_iota(jnp.int32, (tc, tc), 0)
    cs = lax.broadcasted_iota(jnp.int32, (tc, tc), 1)
    causal = cs <= rt
    tril = jnp.where(causal, 1.0, 0.0).astype(BF16)
    ones = jnp.ones((tc, ML_DV), F32)

    def gates(sq):
        g = g_ref[sq] + bias_ref[...]
        lf = jnp.where(hi, _log_sigmoid(g), 0.0)
        b = sum(jnp.dot(tril, p, preferred_element_type=F32) for p in _split3(lf))
        c = jnp.where(hi, pltpu.roll(g, nh, 1) - b, 0.0)
        m_prev = mst[sq]
        mx = jnp.maximum(_cummax_rows(c), m_prev)
        m_row = b + mx
        mx_lo = pltpu.roll(mx, LANES - nh, 1)
        w_inter = jnp.exp(pltpu.roll(m_prev, LANES - nh, 1) - mx_lo)
        mst[sq] = m_row[tc - 1:tc, :]
        return dict(xc=_pieces(jnp.where(lo, w_inter, jnp.where(hi, jnp.exp(-m_row), 0.0))),
                    lc=_pieces(jnp.where(lo, -mx_lo, jnp.where(hi, 1.0, 0.0))),
                    rc=_pieces(jnp.where(lo, 1.0, jnp.where(hi, c, 0.0))))

    gt = [gates(sq) for sq in range(nseq)]
    units = [(sq, hd) for hd in range(nh) for sq in range(nseq)]
    st = [dict() for _ in units]

    def stage1(u):
        sq, hd = units[u]
        d = st[u]
        rh = gt[sq]["rc"] * mask_ref[hd:hd + 1, :].astype(BF16)
        d["dmat"] = lax.dot_general(gt[sq]["lc"], rh, (((1,), (1,)), ((), ())), preferred_element_type=F32)
        d["wb"] = jnp.dot(gt[sq]["xc"], sel_ref[hd], preferred_element_type=F32)
        d["qh"] = q_ref[sq, :, hd * ML_DK:(hd + 1) * ML_DK]
        d["kt"] = kt_refs[sq][hd * ML_DK:(hd + 1) * ML_DK, :]
        d["qk"] = jnp.dot(d["qh"].astype(BF16), d["kt"].astype(BF16), preferred_element_type=F32)

    def stage2(u):
        sq, hd = units[u]
        d = st[u]
        cols = slice(hd * ML_DV, (hd + 1) * ML_DV)
        d["w"] = jnp.exp(jnp.where(causal, d["dmat"], NEG_INF))
        d["vaug"] = jnp.concatenate([v_ref[sq, :, cols], ones], axis=1).astype(BF16)
        d["cm"] = caug[sq, hd]
        lhs = jnp.concatenate([(d["qk"] * d["w"]).astype(BF16), (d["wb"][:, :ML_DK] * d["qh"]).astype(BF16)], axis=1)
        rhs = jnp.concatenate([d["vaug"], d["cm"].astype(BF16)], axis=0)
        d["both"] = jnp.dot(lhs, rhs, preferred_element_type=F32)
        kw = (d["kt"] * d["w"][tc - 1:tc, :]).astype(BF16)
        d["upd"] = jnp.dot(kw, d["vaug"], preferred_element_type=F32)

    def stage3(u):
        sq, hd = units[u]
        d = st[u]
        cols = slice(hd * ML_DV, (hd + 1) * ML_DV)
        both, wb = d["both"], d["wb"]
        h = both[:, :ML_DV] / jnp.maximum(jnp.abs(both[:, ML_DV:]), wb[:, ML_DV:])
        h_ref[sq, :, cols] = _head_out(h, o_ref[sq, :, cols], gout_ref[:, cols])
        decay = wb[tc - 1:tc, :ML_DV]
        caug[sq, hd] = jnp.concatenate([decay, decay], axis=1) * d["cm"] + d["upd"]
        d.clear()

    for step in range(len(units) + 2):
        if step < len(units):
            stage1(step)
        if 0 <= step - 1 < len(units):
            stage2(step - 1)
        if 0 <= step - 2 < len(units):
            stage3(step - 2)

    @pl.when(ci == nchunks - 1)
    def _():
        c_ref[...] = caug[:, :, :, :ML_DV]
        n_ref[...] = caug[:, :, :, ML_DV:]
        m_ref[...] = mst[...]


def _kt_index(b, c, *, sq, nseq, nchunks):
    return 0, (b * nseq + sq) * nchunks + c


def _mlstm_prompt(proj, kt, bias, gout, consts, layer, prev):
    bsz, seq, _ = proj.shape
    tc, nseq = ML_CHUNK, ML_SEQS
    nchunks = seq // tc
    n_layers = bias.shape[0]
    mask, sel = consts
    blk = lambda w, j: pl.BlockSpec((nseq, tc, w), lambda b, c: (b, c, j))
    st = lambda shape: pl.BlockSpec((None, nseq) + shape, lambda b, c: (layer, b) + (0,) * len(shape))
    st_shape = lambda shape: jax.ShapeDtypeStruct((n_layers, bsz) + shape, F32)
    prev, prev_specs, aliases = _alias_inputs(prev, 1)
    return pl.pallas_call(
        _skip_refs(functools.partial(_mlstm_prompt_kernel, tc=tc, nchunks=nchunks, nseq=nseq), len(prev)),
        grid=(bsz // nseq, nchunks),
        in_specs=prev_specs + [
            blk(ML_WIDTH, ODDP_V_BLOCK), blk(ML_WIDTH, ODDP_O_BLOCK), blk(ML_QK, ODDP_Q_BLOCK),
            blk(LANES, ODDP_G_BLOCK)]
        + [pl.BlockSpec((ML_QK, tc), functools.partial(_kt_index, sq=sq, nseq=nseq, nchunks=nchunks))
           for sq in range(nseq)] + [
            _layer_spec((1, LANES), layer), _layer_spec((1, ML_WIDTH), layer),
            _const_spec(mask.shape), _const_spec(sel.shape)],
        out_specs=[blk(ML_WIDTH, 0), st((ML_HEADS, ML_DK, ML_DV)), st((ML_HEADS, ML_DK, ML_DV)), st((1, LANES))],
        out_shape=[jax.ShapeDtypeStruct((bsz, seq, ML_WIDTH), BF16),
                   st_shape((ML_HEADS, ML_DK, ML_DV)), st_shape((ML_HEADS, ML_DK, ML_DV)), st_shape((1, LANES))],
        input_output_aliases=aliases,
        scratch_shapes=[pltpu.VMEM((nseq, ML_HEADS, ML_DK, ML_AUG), F32), pltpu.VMEM((nseq, 1, LANES), F32)],
        compiler_params=_cparams("parallel", "arbitrary"),
        name="mlstm_prompt",
    )(*prev, proj, proj, proj, proj, *([kt] * nseq), bias, gout, mask, sel)


def _mlstm_sample_kernel(q_ref, k_ref, v_ref, o_ref, g_ref, bias_ref, gout_ref, c0_ref, n0_ref, m0_ref,
                         h_ref, c_ref, n_ref, m_ref, *, bs, t_new):
    tio = lax.broadcasted_iota(jnp.int32, (t_new, LANES), 0)

    def one_sequence(b, carry):
        g = g_ref[b] + bias_ref[...]
        lf = pltpu.roll(_log_sigmoid(g), LANES - ML_HEADS, 1)
        cum, acc = [], None
        for t in range(t_new):
            acc = lf[t:t + 1] if acc is None else acc + lf[t:t + 1]
            cum.append(acc)
        bcol = jnp.concatenate(cum, axis=0)
        m_prev = m0_ref[pl.ds(b, 1), :]
        inter = bcol + m_prev
        dcols = [jnp.where(tio >= s, bcol - cum[s] + g[s:s + 1], NEG_INF) for s in range(t_new)]
        m_row = inter
        for dc in dcols:
            m_row = jnp.maximum(m_row, dc)
        w_inter = jnp.exp(inter - m_row)
        w_intra = [jnp.exp(dc - m_row) for dc in dcols]
        floor = jnp.exp(-m_row)
        m_ref[pl.ds(b, 1), :] = m_row[t_new - 1:t_new]
        for hd in range(ML_HEADS):
            hl = slice(hd, hd + 1)
            qh = q_ref[b, :, hd * ML_DK:(hd + 1) * ML_DK]
            kh = k_ref[b, :, hd * ML_DK:(hd + 1) * ML_DK] * (ML_DK ** -0.5)
            vh = v_ref[b, :, hd * ML_DV:(hd + 1) * ML_DV]
            cm = c0_ref[b, hd]
            nrow = n0_ref[b, hl, :]
            qhb, khb = qh.astype(BF16), kh.astype(BF16)
            qk = lax.dot_general(qhb, khb, (((1,), (1,)), ((), ())), preferred_element_type=F32)
            wi = w_inter[:, hl]
            num = wi * jnp.dot(qhb, cm.astype(BF16), preferred_element_type=F32)
            den = wi * jnp.sum(qh * nrow, axis=-1, keepdims=True)
            kw_rows = []
            for s in range(t_new):
                coef = qk[:, s:s + 1] * w_intra[s][:, hl]
                num = num + coef * vh[s:s + 1, :]
                den = den + coef
                kw_rows.append(kh[s:s + 1, :] * w_intra[s][t_new - 1:t_new, hl])
            h = num / jnp.maximum(jnp.abs(den), floor[:, hl])
            cols = slice(hd * ML_DV, (hd + 1) * ML_DV)
            h_ref[b, :, cols] = _head_out(h, o_ref[b, :, cols], gout_ref[:, cols])
            kw = jnp.concatenate(kw_rows, axis=0)
            upd = lax.dot_general(kw.astype(BF16), vh.astype(BF16), (((0,), (0,)), ((), ())),
                                  preferred_element_type=F32)
            decay = w_inter[t_new - 1:t_new, hl]
            c_ref[b, hd] = decay * cm + upd
            n_ref[b, hl, :] = decay * nrow + jnp.sum(kw, axis=0, keepdims=True)
        return carry

    lax.fori_loop(0, bs, one_sequence, 0)


def _mlstm_sample(proj, bias, gout, c0, n0, m0, layer, prev):
    bsz, t_new, _ = proj.shape
    bs = SAMPLE_BLOCK
    blk = lambda w, j: pl.BlockSpec((bs, t_new, w), lambda i: (i, 0, j))
    cst = pl.BlockSpec((None, bs, ML_HEADS, ML_DK, ML_DV), lambda i: (layer, i, 0, 0, 0))
    nst = pl.BlockSpec((None, bs, ML_HEADS, ML_DK), lambda i: (layer, i, 0, 0))
    mst = pl.BlockSpec((None, bs, LANES), lambda i: (layer, i, 0))
    prev, prev_specs, aliases = _alias_inputs(prev, 1)
    return pl.pallas_call(
        _skip_refs(functools.partial(_mlstm_sample_kernel, bs=bs, t_new=t_new), len(prev)),
        grid=(bsz // bs,),
        in_specs=prev_specs + [
            blk(ML_QK, 0), blk(ML_QK, ODD_K_BLOCK), blk(ML_WIDTH, ODD_V_BLOCK), blk(ML_WIDTH, ODD_O_BLOCK),
            blk(LANES, ODD_G_BLOCK), _layer_spec((1, LANES), layer), _layer_spec((1, ML_WIDTH), layer),
            cst, nst, mst],
        out_specs=[blk(ML_WIDTH, 0), cst, nst, mst],
        out_shape=[jax.ShapeDtypeStruct((bsz, t_new, ML_WIDTH), BF16),
                   jax.ShapeDtypeStruct(c0.shape, F32), jax.ShapeDtypeStruct(n0.shape, F32),
                   jax.ShapeDtypeStruct(m0.shape, F32)],
        input_output_aliases=aliases,
        compiler_params=_cparams("parallel"),
        name="mlstm_sample",
    )(*prev, proj, proj, proj, proj, proj, bias, gout, c0, n0, m0)


def _pad_lanes(x):
    return jnp.pad(x, [(0, 0)] * (x.ndim - 1) + [(0, LANES - x.shape[-1])])


def kernel(x_prompt, x_sample, cache_k, cache_v, state_ssm_re, state_ssm_im, state_mlstm_c, state_mlstm_n, state_mlstm_m, norm_mix, norm_ffn, w_in_even, q_norm, k_norm, attn_sinks, s5_a_re, s5_a_im, s5_log_dt, s5_b_re, s5_b_im, s5_c_re, s5_c_im, s5_d, s5_w_glu, s5_b_glu, w_out_even, w_in_odd, ml_b_i, ml_b_f, ml_out_norm, w_out_odd, w_gate, w_up, w_down):
    bp, lp, _ = x_prompt.shape
    bsm, ls, _ = x_sample.shape
    yp = x_prompt.reshape(bp * lp, D_MODEL)
    ys = x_sample.reshape(bsm * ls, D_MODEL)
    tab_p = _rope_tables(jnp.arange(lp))
    tab_s = tuple(jnp.tile(t, (SAMPLE_BLOCK, 1)) for t in _rope_tables(PAST_LEN + jnp.arange(ls)))
    n_even, n_odd = w_in_even.shape[0], w_in_odd.shape[0]

    g_mix = norm_mix.reshape(DEPTH, 1, D_MODEL)
    g_ffn = norm_ffn.reshape(DEPTH, 1, D_MODEL)
    wg, wu, wd = w_gate.astype(BF16), w_up.astype(BF16), w_down.astype(BF16)
    kv0, u0 = ATTN_WIDTH, ATTN_WIDTH + 2 * KV_WIDTH
    order = jnp.asarray(ATTN_HEAD_ORDER)
    wq = w_in_even[..., :kv0].reshape(n_even, D_MODEL, ATTN_HEADS, HEAD_DIM)[:, :, order].reshape(n_even, D_MODEL, kv0)
    w_in_e = jnp.concatenate([wq, w_in_even[..., u0:], w_in_even[..., kv0:u0]], axis=-1).astype(BF16)
    wo_attn = w_out_even[:, :kv0].reshape(n_even, ATTN_HEADS, HEAD_DIM, D_MODEL)[:, order].reshape(n_even, kv0, D_MODEL)
    w_out_e = jnp.concatenate([wo_attn, w_out_even[:, kv0:]], axis=1).astype(BF16)
    gq = jnp.tile(q_norm, (1, LANES // HEAD_DIM)).reshape(n_even, 1, LANES)
    gk = jnp.tile(k_norm, (1, LANES // HEAD_DIM)).reshape(n_even, 1, LANES)
    prm = _s5_params(s5_a_re, s5_a_im, s5_log_dt, s5_b_re, s5_b_im, s5_c_re, s5_c_im, s5_d, s5_w_glu, s5_b_glu)
    w_in_o = jnp.pad(w_in_odd, ((0, 0), (0, 0), (0, ODD_IN_PAD - ODD_IN))).astype(BF16)
    k0, v0, g0 = ML_QK, 2 * ML_QK, 2 * ML_QK + 2 * ML_WIDTH
    w_in_op = jnp.concatenate([w_in_o[..., v0:g0], w_in_o[..., :k0], w_in_o[..., g0:]], axis=-1)
    w_kt = jnp.swapaxes(w_in_o[..., k0:v0], 1, 2)
    ml_consts = _ml_select_constants()
    w_out_o = w_out_odd.astype(BF16)
    ml_bias = _pad_lanes(jnp.concatenate([ml_b_i, ml_b_f], axis=-1)).reshape(n_odd, 1, LANES)
    ml_gout = ml_out_norm.reshape(n_odd, 1, ML_WIDTH)
    ck = cache_k.reshape(n_even, bsm, WINDOW, KV_WIDTH)
    cv = cache_v.reshape(n_even, bsm, WINDOW, KV_WIDTH)
    h0r = state_ssm_re.reshape(n_even, bsm, S5_FLAT)
    h0i = state_ssm_im.reshape(n_even, bsm, S5_FLAT)
    m0 = _pad_lanes(state_mlstm_m)

    p_attn = p_ssm = p_ml = s_attn = s_ssm = s_ml = None
    for layer in range(DEPTH):
        ffn = (layer, g_ffn, wg, wu, wd)
        if layer % 2 == 0:
            e = layer // 2
            proj = _norm_matmul(yp, g_mix, layer, w_in_e, e)
            proj3 = proj.reshape(bp, lp, -1)
            attn, *p_attn = _attn_prompt(proj3, tab_p, gq, gk, attn_sinks, e, p_attn)
            ssm, *p_ssm = _s5_prompt(proj3, prm, e, p_ssm)
            yp = _mix_ffn(yp, [attn.reshape(bp * lp, -1), ssm.reshape(bp * lp, -1)], w_out_e, e, *ffn)
            proj = _norm_matmul(ys, g_mix, layer, w_in_e, e)
            attn, *s_attn = _attn_sample(proj, ck, cv, tab_s, gq, gk, attn_sinks, ls, e, s_attn)
            u = proj[:, ATTN_WIDTH:ATTN_WIDTH + S5_WIDTH]
            u_tm = u.reshape(bsm, ls, S5_WIDTH).transpose(1, 0, 2).reshape(ls * bsm, S5_WIDTH)
            ssm_tm, *s_ssm = _s5_sample(u_tm, h0r, h0i, prm, ls, e, s_ssm)
            ssm = ssm_tm.reshape(ls, bsm, S5_WIDTH).transpose(1, 0, 2).reshape(bsm * ls, S5_WIDTH)
            ys = _mix_ffn(ys, [attn, ssm], w_out_e, e, *ffn)
        else:
            o = layer // 2
            proj, kt = _norm_matmul_kt(yp, g_mix, layer, w_in_op, w_kt, o)
            hh, *p_ml = _mlstm_prompt(proj.reshape(bp, lp, -1), kt, ml_bias, ml_gout, ml_consts, o, p_ml)
            yp = _mix_ffn(yp, [hh.reshape(bp * lp, -1)], w_out_o, o, *ffn)
            proj = _norm_matmul(ys, g_mix, layer, w_in_o, o)
            hh, *s_ml = _mlstm_sample(proj.reshape(bsm, ls, -1), ml_bias, ml_gout, state_mlstm_c, state_mlstm_n, m0,
                                      o, s_ml)
            ys = _mix_ffn(ys, [hh.reshape(bsm * ls, -1)], w_out_o, o, *ffn)
    heads = lambda a: a.reshape(a.shape[:3] + (KV_HEADS, HEAD_DIM))
    groups = lambda a: a.reshape(a.shape[:2] + (S5_GROUPS, S5_STATE))
    return (yp.reshape(bp, lp, D_MODEL), ys.reshape(bsm, ls, D_MODEL),
            heads(p_attn[0]), heads(p_attn[1]), groups(p_ssm[0]), groups(p_ssm[1]),
            p_ml[0], p_ml[1][..., 0], p_ml[2][:, :, 0, ML_HEADS:2 * ML_HEADS],
            heads(s_attn[0]), heads(s_attn[1]), groups(s_ssm[0]), groups(s_ssm[1]),
            s_ml[0], s_ml[1], s_ml[2][..., :ML_HEADS])
```

```python
import functools

import numpy as np

import jax
import jax.numpy as jnp
from jax import lax
from jax.experimental import pallas as pl
from jax.experimental.pallas import tpu as pltpu

F32 = jnp.float32
BF16 = jnp.bfloat16

D_MODEL = 1024
DEPTH = 4
PAST_LEN = 8192
WINDOW = 128
ATTN_HEADS = 8
KV_HEADS = 2
HEAD_DIM = 64
ATTN_WIDTH = ATTN_HEADS * HEAD_DIM
KV_WIDTH = KV_HEADS * HEAD_DIM
ROT_DIM = HEAD_DIM // 4
ROPE_THETA = 500000.0
S5_GROUP = 16
S5_WIDTH = D_MODEL // 2
S5_GROUPS = S5_WIDTH // S5_GROUP
S5_STATE = 64
S5_FLAT = S5_GROUPS * S5_STATE
ML_HEADS = 8
ML_DV = D_MODEL // ML_HEADS
ML_DK = ML_DV // 2
ML_QK = ML_HEADS * ML_DK
ML_WIDTH = ML_HEADS * ML_DV
ODD_IN = 2 * ML_QK + 2 * ML_WIDTH + 2 * ML_HEADS
ODD_IN_PAD = 2 * ML_QK + 2 * ML_WIDTH + 128
D_FF = 2816
EPS = 1e-6

LANES = 128
SUBLANES = 8
ROW_TILE = 512
FF_TILE = 256
S5_CHUNK = 64
ML_CHUNK = 128
SAMPLE_BLOCK = 8
VMEM_LIMIT = 56 * 1024 * 1024

NEG_INF = float("-inf")


def _cparams(*sem):
    return pltpu.CompilerParams(dimension_semantics=sem, vmem_limit_bytes=VMEM_LIMIT)


def _const_spec(shape):
    zeros = (0,) * len(shape)
    return pl.BlockSpec(shape, lambda *_: zeros, pipeline_mode=pl.Buffered(1))


def _layer_spec(shape, layer):
    zeros = (0,) * len(shape)
    return pl.BlockSpec((None,) + tuple(shape), lambda *_: (layer,) + zeros, pipeline_mode=pl.Buffered(1))


def _skip_refs(body, n_skip):
    if n_skip == 0:
        return body

    def wrapped(*refs):
        return body(*refs[n_skip:])

    return wrapped


def _alias_inputs(prev, first_state_out):
    prev = () if prev is None else tuple(prev)
    specs = [pl.BlockSpec(memory_space=pl.ANY) for _ in prev]
    aliases = {i: first_state_out + i for i in range(len(prev))}
    return prev, specs, aliases


def _rms(x, g):
    ms = jnp.mean(x * x, axis=-1, keepdims=True)
    return x * lax.rsqrt(ms + EPS) * g


def _split3(a):
    a1 = a.astype(BF16)
    r1 = a - a1.astype(F32)
    a2 = r1.astype(BF16)
    a3 = (r1 - a2.astype(F32)).astype(BF16)
    return a1, a2, a3


def _log_sigmoid(x):
    return jnp.minimum(x, 0.0) - jnp.log1p(jnp.exp(-jnp.abs(x)))


def _sigmoid(x):
    return 1.0 / (1.0 + jnp.exp(-x))


def _norm_matmul_kernel(x_ref, g_ref, w_ref, o_ref):
    h = _rms(x_ref[...], g_ref[...]).astype(BF16)
    o_ref[...] = jnp.dot(h, w_ref[...], preferred_element_type=F32)


def _norm_matmul(x, g, layer, w, widx):
    n, d = x.shape
    m = w.shape[2]
    tm = min(ROW_TILE, n)
    return pl.pallas_call(
        _norm_matmul_kernel,
        grid=(n // tm,),
        in_specs=[pl.BlockSpec((tm, d), lambda i: (i, 0)), _layer_spec((1, d), layer), _layer_spec((d, m), widx)],
        out_specs=pl.BlockSpec((tm, m), lambda i: (i, 0)),
        out_shape=jax.ShapeDtypeStruct((n, m), F32),
        compiler_params=_cparams("parallel"),
        name="norm_matmul",
    )(x, g, w)


def _mix_ffn_kernel(*refs, n_mix):
    x_ref = refs[0]
    a_refs = refs[1:1 + n_mix]
    wo_ref, g_ref, wg_ref, wu_ref, wd_ref, o_ref, act_ref = refs[1 + n_mix:]
    y = x_ref[...]
    off = 0
    for a_ref in a_refs:
        ka = a_ref.shape[1]
        y = y + jnp.dot(a_ref[...], wo_ref[off:off + ka, :], preferred_element_type=F32)
        off += ka
    h = _rms(y, g_ref[...]).astype(BF16)
    for f in range(D_FF // FF_TILE):
        cols = slice(f * FF_TILE, (f + 1) * FF_TILE)
        gate = jnp.dot(h, wg_ref[:, cols], preferred_element_type=F32)
        up = jnp.dot(h, wu_ref[:, cols], preferred_element_type=F32)
        act_ref[:, cols] = (gate * _sigmoid(gate) * up).astype(BF16)
    o_ref[...] = y + jnp.dot(act_ref[...], wd_ref[...], preferred_element_type=F32)


def _mix_ffn(x, mixes, w_out, oidx, layer, g_ffn, wg, wu, wd):
    n, d = x.shape
    tm = min(ROW_TILE, n)
    row = lambda i: (i, 0)
    in_specs = [pl.BlockSpec((tm, d), row)]
    in_specs += [pl.BlockSpec((tm, a.shape[1]), row) for a in mixes]
    in_specs += [_layer_spec(w_out.shape[1:], oidx), _layer_spec((1, d), layer), _layer_spec(wg.shape[1:], layer),
                 _layer_spec(wu.shape[1:], layer), _layer_spec(wd.shape[1:], layer)]
    return pl.pallas_call(
        functools.partial(_mix_ffn_kernel, n_mix=len(mixes)),
        grid=(n // tm,),
        in_specs=in_specs,
        out_specs=pl.BlockSpec((tm, d), row),
        out_shape=jax.ShapeDtypeStruct((n, d), F32),
        scratch_shapes=[pltpu.VMEM((tm, D_FF), BF16)],
        compiler_params=_cparams("parallel"),
        name="mix_ffn",
    )(x, *mixes, w_out, g_ffn, wg, wu, wd)


def _head_ones():
    r = lax.broadcasted_iota(jnp.int32, (LANES, LANES), 0) // HEAD_DIM
    c = lax.broadcasted_iota(jnp.int32, (LANES, LANES), 1) // HEAD_DIM
    return jnp.where(r == c, 1.0, 0.0).astype(BF16)


def _qk_prep(x, g, ones, ct, sa, sb):
    x2 = x * x
    hi = x2.astype(BF16)
    lo = (x2 - hi.astype(F32)).astype(BF16)
    ss = jnp.dot(hi, ones, preferred_element_type=F32) + jnp.dot(lo, ones, preferred_element_type=F32)
    xn = x * lax.rsqrt(ss * (1.0 / HEAD_DIM) + EPS) * g
    return xn * ct + pltpu.roll(xn, LANES - ROT_DIM // 2, 1) * sa + pltpu.roll(xn, ROT_DIM // 2, 1) * sb


def _rope_tables(pos):
    half = ROT_DIM // 2
    inv = jnp.power(jnp.float32(ROPE_THETA), -jnp.arange(half, dtype=F32) / half)
    ang = pos.astype(F32)[:, None] * inv[None, :]
    cos, sin = jnp.cos(ang), jnp.sin(ang)
    n = pos.shape[0]
    one = jnp.ones((n, HEAD_DIM - ROT_DIM), F32)
    zero = jnp.zeros((n, HEAD_DIM - ROT_DIM), F32)
    z8 = jnp.zeros((n, half), F32)
    ct = jnp.concatenate([cos, cos, one], axis=1)
    sa = jnp.concatenate([-sin, z8, zero], axis=1)
    sb = jnp.concatenate([z8, sin, zero], axis=1)
    tile = lambda t: jnp.concatenate([t, t], axis=1)
    return tile(ct), tile(sa), tile(sb)


ATTN_SEQS = 2
ATTN_QCHUNKS = ATTN_WIDTH // LANES
ATTN_HEAD_ORDER = tuple(h * ATTN_QCHUNKS + j for j in range(ATTN_QCHUNKS) for h in range(KV_HEADS))


def _attn_prompt_kernel(q_ref, kv_ref, ct_ref, sa_ref, sb_ref, gq_ref, gk_ref, sink_ref,
                        o_ref, pk_ref, pv_ref, kprev, vprev, *, nb, layer, nseq):
    i = pl.program_id(1)

    @pl.when(i == 0)
    def _():
        kprev[...] = jnp.zeros_like(kprev)
        vprev[...] = jnp.zeros_like(vprev)

    ones = _head_ones()
    ct, sa, sb = ct_ref[...], sa_ref[...], sb_ref[...]
    r = lax.broadcasted_iota(jnp.int32, (WINDOW, 2 * WINDOW), 0)
    c = lax.broadcasted_iota(jnp.int32, (WINDOW, 2 * WINDOW), 1)
    rel = r + WINDOW - c
    mask = (rel >= 0) & (rel <= WINDOW) & ((c >= WINDOW) | (i > 0))
    lane = lax.broadcasted_iota(jnp.int32, (WINDOW, LANES), 1)
    group0 = lane < HEAD_DIM
    v_ones = jnp.ones((2 * WINDOW, LANES), BF16)
    nq = ATTN_QCHUNKS
    st = [dict() for _ in range(nseq)]

    def prep(sq):
        d = st[sq]
        kv = kv_ref[sq]
        d["kn"] = _qk_prep(kv[:, :KV_WIDTH], gk_ref[...], ones, ct, sa, sb)
        d["v"] = kv[:, KV_WIDTH:]
        d["qn"] = [_qk_prep(q_ref[sq, :, j * LANES:(j + 1) * LANES], gq_ref[...], ones, ct, sa, sb)
                   * (HEAD_DIM ** -0.5) for j in range(nq)]
        d["kcat"] = jnp.concatenate([kprev[sq], d["kn"]], axis=0).astype(BF16)
        d["vaug"] = jnp.concatenate([jnp.concatenate([vprev[sq], d["v"]], axis=0).astype(BF16), v_ones], axis=1)
        kprev[sq] = d["kn"]
        vprev[sq] = d["v"]

    def scores(sq, h):
        d = st[sq]
        keep = group0 if h == 0 else jnp.logical_not(group0)
        qs = jnp.concatenate([jnp.where(keep, qj, 0.0) for qj in d["qn"]], axis=0).astype(BF16)
        d["s", h] = lax.dot_general(qs, d["kcat"], (((1,), (1,)), ((), ())), preferred_element_type=F32)

    def softmax_pv(sq, h):
        d = st[sq]
        s = d.pop(("s", h))
        ps, corr = [], []
        for j in range(nq):
            sg = jnp.where(mask, s[j * WINDOW:(j + 1) * WINDOW], NEG_INF)
            sink = sink_ref[layer, h * nq + j]
            m = jnp.maximum(jnp.max(sg, axis=-1, keepdims=True), sink)
            ps.append(jnp.exp(sg - m).astype(BF16))
            corr.append(jnp.exp(sink - m))
        o = jnp.dot(jnp.concatenate(ps, axis=0), d["vaug"], preferred_element_type=F32)
        d["o", h] = [o[j * WINDOW:(j + 1) * WINDOW, :LANES] / (o[j * WINDOW:(j + 1) * WINDOW, LANES:] + corr[j])
                     for j in range(nq)]

    def finish(sq):
        d = st[sq]
        o_ref[sq] = jnp.concatenate([jnp.where(group0, d["o", 0][j], d["o", 1][j]) for j in range(nq)],
                                    axis=1).astype(BF16)

    for sq in range(nseq):
        prep(sq)
    for sq in range(nseq):
        scores(sq, 0)
        scores(sq, 1)
    for sq in range(nseq):
        softmax_pv(sq, 0)
        softmax_pv(sq, 1)
        finish(sq)

    @pl.when(i == nb - 1)
    def _():
        for sq in range(nseq):
            pk_ref[sq] = st[sq]["kn"]
            pv_ref[sq] = st[sq]["v"]


def _attn_prompt(proj, tables, gq, gk, sinks, layer, prev):
    bsz, seq, _ = proj.shape
    nb = seq // WINDOW
    nseq = ATTN_SEQS
    n_layers = gq.shape[0]
    tab = pl.BlockSpec((WINDOW, LANES), lambda b, i: (i, 0))
    prev, prev_specs, aliases = _alias_inputs(prev, 1)
    win = pl.BlockSpec((None, nseq, WINDOW, KV_WIDTH), lambda b, i: (layer, b, 0, 0))
    win_shape = jax.ShapeDtypeStruct((n_layers, bsz, WINDOW, KV_WIDTH), F32)
    return pl.pallas_call(
        _skip_refs(functools.partial(_attn_prompt_kernel, nb=nb, layer=layer, nseq=nseq), len(prev)),
        grid=(bsz // nseq, nb),
        in_specs=prev_specs + [
            pl.BlockSpec((nseq, WINDOW, ATTN_WIDTH), lambda b, i: (b, i, 0)),
            pl.BlockSpec((nseq, WINDOW, 2 * KV_WIDTH), lambda b, i: (b, i, EVEN_KV_BLOCK)),
            tab, tab, tab, _layer_spec((1, LANES), layer), _layer_spec((1, LANES), layer),
            pl.BlockSpec(memory_space=pltpu.SMEM)],
        out_specs=[pl.BlockSpec((nseq, WINDOW, ATTN_WIDTH), lambda b, i: (b, i, 0)), win, win],
        out_shape=[jax.ShapeDtypeStruct((bsz, seq, ATTN_WIDTH), BF16), win_shape, win_shape],
        input_output_aliases=aliases,
        scratch_shapes=[pltpu.VMEM((nseq, WINDOW, KV_WIDTH), F32), pltpu.VMEM((nseq, WINDOW, KV_WIDTH), F32)],
        compiler_params=_cparams("parallel", "arbitrary"),
        name="attn_prompt",
    )(*prev, proj, proj, *tables, gq, gk, sinks)


EVEN_U_BLOCK = ATTN_WIDTH // S5_WIDTH
EVEN_KV_BLOCK = (ATTN_WIDTH + S5_WIDTH) // (2 * KV_WIDTH)
KALL_ROWS = WINDOW + SUBLANES


def _attn_sample_kernel(q_ref, kv_ref, ck_ref, cv_ref, ct_ref, sa_ref, sb_ref, gq_ref, gk_ref, sink_ref,
                        o_ref, nk_ref, nv_ref, kall, vall, o_seq, *, bs, t_new, layer):
    ones = _head_ones()
    ct, sa, sb = ct_ref[...], sa_ref[...], sb_ref[...]
    kv = kv_ref[...]
    kn = _qk_prep(kv[:, :KV_WIDTH], gk_ref[...], ones, ct, sa, sb)
    v = kv[:, KV_WIDTH:]
    qn = jnp.concatenate(
        [_qk_prep(q_ref[:, c * LANES:(c + 1) * LANES], gq_ref[...], ones, ct, sa, sb)
         for c in range(ATTN_WIDTH // LANES)], axis=1) * (HEAD_DIM ** -0.5)
    grp = ATTN_HEADS // KV_HEADS
    rows = grp * t_new
    r = lax.broadcasted_iota(jnp.int32, (rows, KALL_ROWS), 0)
    c = lax.broadcasted_iota(jnp.int32, (rows, KALL_ROWS), 1)
    t = r % t_new
    mask = (c >= t) & (c <= t + WINDOW)
    rg = lax.broadcasted_iota(jnp.int32, (rows, 1), 0) // t_new
    pad = jnp.zeros((KALL_ROWS - WINDOW - t_new, KV_WIDTH), F32)
    def seq_rows(a, b):
        return jnp.concatenate([a[t * bs + b:t * bs + b + 1] for t in range(t_new)], axis=0)

    for b in range(bs):
        ck, cv = ck_ref[b], cv_ref[b]
        kn_b, v_b, qn_b = seq_rows(kn, b), seq_rows(v, b), seq_rows(qn, b)
        kall[0:WINDOW, :] = ck
        vall[0:WINDOW, :] = cv
        kall[WINDOW:KALL_ROWS, :] = jnp.concatenate([kn_b, pad], axis=0)
        vall[WINDOW:KALL_ROWS, :] = jnp.concatenate([v_b, pad], axis=0)
        nk_ref[b] = pltpu.roll(ck, WINDOW - t_new, 0)
        nv_ref[b] = pltpu.roll(cv, WINDOW - t_new, 0)
        nk_ref[b, WINDOW - t_new:WINDOW, :] = kn_b
        nv_ref[b, WINDOW - t_new:WINDOW, :] = v_b
        kb = kall[...].astype(BF16)
        vb = vall[...].astype(BF16)
        outs = [None] * ATTN_HEADS
        for h in range(KV_HEADS):
            kh = kb[:, h * HEAD_DIM:(h + 1) * HEAD_DIM]
            vh = vb[:, h * HEAD_DIM:(h + 1) * HEAD_DIM]
            qs = jnp.concatenate(
                [qn_b[:, (g * KV_HEADS + h) * HEAD_DIM:(g * KV_HEADS + h + 1) * HEAD_DIM] for g in range(grp)],
                axis=0).astype(BF16)
            s = lax.dot_general(qs, kh, (((1,), (1,)), ((), ())), preferred_element_type=F32)
            s = jnp.where(mask, s, NEG_INF)
            sink = jnp.zeros((rows, 1), F32)
            for g in range(grp):
                sink = jnp.where(rg == g, sink_ref[layer, h * grp + g], sink)
            m = jnp.maximum(jnp.max(s, axis=-1, keepdims=True), sink)
            p = jnp.exp(s - m)
            den = jnp.sum(p, axis=-1, keepdims=True) + jnp.exp(sink - m)
            o = jnp.dot((p / den).astype(BF16), vh, preferred_element_type=F32)
            for g in range(grp):
                outs[g * KV_HEADS + h] = o[g * t_new:(g + 1) * t_new]
        o_b = jnp.concatenate(outs, axis=1)
        for t in range(t_new):
            o_seq[t * bs + b:t * bs + b + 1, :] = o_b[t:t + 1]
    o_ref[...] = o_seq[...].astype(BF16)


def _attn_sample(proj, cache_k, cache_v, tables, gq, gk, sinks, t_new, layer, prev):
    n = proj.shape[0]
    bsz = n // t_new
    bs = SAMPLE_BLOCK
    rows = bs * t_new
    row = lambda i: (i, 0)
    cache = pl.BlockSpec((None, bs, WINDOW, KV_WIDTH), lambda i: (layer, i, 0, 0))
    prev, prev_specs, aliases = _alias_inputs(prev, 1)
    return pl.pallas_call(
        _skip_refs(functools.partial(_attn_sample_kernel, bs=bs, t_new=t_new, layer=layer), len(prev)),
        grid=(bsz // bs,),
        in_specs=prev_specs + [
            pl.BlockSpec((rows, ATTN_WIDTH), row),
            pl.BlockSpec((rows, 2 * KV_WIDTH), lambda i: (i, EVEN_KV_BLOCK)),
            cache, cache,
            _const_spec((rows, LANES)), _const_spec((rows, LANES)), _const_spec((rows, LANES)),
            _layer_spec((1, LANES), layer), _layer_spec((1, LANES), layer),
            pl.BlockSpec(memory_space=pltpu.SMEM)],
        out_specs=[pl.BlockSpec((rows, ATTN_WIDTH), row), cache, cache],
        out_shape=[jax.ShapeDtypeStruct((n, ATTN_WIDTH), BF16),
                   jax.ShapeDtypeStruct(cache_k.shape, F32), jax.ShapeDtypeStruct(cache_v.shape, F32)],
        input_output_aliases=aliases,
        scratch_shapes=[pltpu.VMEM((KALL_ROWS, KV_WIDTH), F32), pltpu.VMEM((KALL_ROWS, KV_WIDTH), F32),
                        pltpu.VMEM((rows, ATTN_WIDTH), F32)],
        compiler_params=_cparams("parallel"),
        name="attn_sample",
    )(*prev, proj, proj, cache_k, cache_v, *tables, gq, gk, sinks)


S5_UCHUNKS = S5_WIDTH // LANES
S5_SUB = S5_FLAT // S5_UCHUNKS
S5_SCHUNKS = S5_FLAT // LANES


def _s5_tail(y, wglu_ref, bglu_ref):
    g = 0.5 * y * (1.0 + lax.erf(y * (2.0 ** -0.5)))
    z = jnp.dot(g.astype(BF16), wglu_ref[...], preferred_element_type=F32) + bglu_ref[...]
    return g * _sigmoid(z)


def _s5_prompt_kernel(u_ref, wb_ref, wc_ref, lam_ref, d_ref, wglu_ref, bglu_ref,
                      o_ref, sr_ref, si_ref, xs, hst, *, nbatch, tc):
    rows = nbatch * tc

    @pl.when(pl.program_id(1) == 0)
    def _():
        hst[...] = jnp.zeros_like(hst)

    u = jnp.swapaxes(u_ref[...], 0, 1).reshape(rows, S5_WIDTH)
    ub = u.astype(BF16)
    for cc in range(S5_UCHUNKS):
        res = jnp.dot(ub[:, cc * LANES:(cc + 1) * LANES], wb_ref[cc], preferred_element_type=F32)
        for j in range(S5_SUB // LANES):
            xs[cc * 4 + j] = res[:, j * LANES:(j + 1) * LANES]
            xs[S5_SCHUNKS + cc * 4 + j] = res[:, S5_SUB + j * LANES:S5_SUB + (j + 1) * LANES]

    def step(t, carry):
        new_r, new_i = [], []
        for k in range(S5_SCHUNKS):
            hr, hi = carry[k], carry[S5_SCHUNKS + k]
            lr, li = lam_ref[k], lam_ref[S5_SCHUNKS + k]
            idx = pl.ds(pl.multiple_of(t * nbatch, nbatch), nbatch)
            nr = lr * hr - li * hi + xs[k, idx, :]
            ni = lr * hi + li * hr + xs[S5_SCHUNKS + k, idx, :]
            xs[k, idx, :] = nr
            xs[S5_SCHUNKS + k, idx, :] = ni
            new_r.append(nr)
            new_i.append(ni)
        return tuple(new_r + new_i)

    fin = lax.fori_loop(0, tc, step, tuple(hst[k] for k in range(2 * S5_SCHUNKS)))
    for k in range(2 * S5_SCHUNKS):
        hst[k] = fin[k]
    sr_ref[...] = jnp.concatenate(fin[:S5_SCHUNKS], axis=1)
    si_ref[...] = jnp.concatenate(fin[S5_SCHUNKS:], axis=1)

    ys = []
    for cc in range(S5_UCHUNKS):
        s = jnp.concatenate([xs[cc * 4 + j] for j in range(4)]
                            + [xs[S5_SCHUNKS + cc * 4 + j] for j in range(4)], axis=1).astype(BF16)
        cols = slice(cc * LANES, (cc + 1) * LANES)
        ys.append(jnp.dot(s, wc_ref[cc], preferred_element_type=F32) + d_ref[:, cols] * u[:, cols])
    out = _s5_tail(jnp.concatenate(ys, axis=1), wglu_ref, bglu_ref)
    o_ref[...] = jnp.swapaxes(out.reshape(tc, nbatch, S5_WIDTH), 0, 1).astype(BF16)


def _s5_prompt(proj, prm, layer, prev):
    bsz, seq, _ = proj.shape
    nbatch, tc = SUBLANES, S5_CHUNK
    n_layers = prm["wb"].shape[0]
    st = pl.BlockSpec((None, nbatch, S5_FLAT), lambda b, c: (layer, b, 0))
    st_shape = jax.ShapeDtypeStruct((n_layers, bsz, S5_FLAT), F32)
    prev, prev_specs, aliases = _alias_inputs(prev, 1)
    names = ("wb", "wc", "lam8", "d", "wglu", "bglu")
    return pl.pallas_call(
        _skip_refs(functools.partial(_s5_prompt_kernel, nbatch=nbatch, tc=tc), len(prev)),
        grid=(bsz // nbatch, seq // tc),
        in_specs=prev_specs + [pl.BlockSpec((nbatch, tc, S5_WIDTH), lambda b, c: (b, c, EVEN_U_BLOCK))]
        + [_layer_spec(prm[k].shape[1:], layer) for k in names],
        out_specs=[pl.BlockSpec((nbatch, tc, S5_WIDTH), lambda b, c: (b, c, 0)), st, st],
        out_shape=[jax.ShapeDtypeStruct((bsz, seq, S5_WIDTH), BF16), st_shape, st_shape],
        input_output_aliases=aliases,
        scratch_shapes=[pltpu.VMEM((2 * S5_SCHUNKS, nbatch * tc, LANES), F32),
                        pltpu.VMEM((2 * S5_SCHUNKS, nbatch, LANES), F32)],
        compiler_params=_cparams("parallel", "arbitrary"),
        name="s5_prompt",
    )(*prev, proj, *[prm[k] for k in names])


def _s5_sample_kernel(u_ref, wb_ref, wc_ref, lr_ref, li_ref, d_ref, wglu_ref, bglu_ref, h0r_ref, h0i_ref,
                      o_ref, sr_ref, si_ref, xr, xi, *, nseq, t_new):
    nt, st = nseq // SAMPLE_TILE, SAMPLE_TILE
    n = nseq * t_new
    u = u_ref[...]
    ub = u.astype(BF16)
    for cc in range(S5_UCHUNKS):
        res = jnp.dot(ub[:, cc * LANES:(cc + 1) * LANES], wb_ref[cc], preferred_element_type=F32)
        sc = slice(cc * S5_SUB, (cc + 1) * S5_SUB)
        xr[:, :, :, sc] = res[:, :S5_SUB].reshape(nt, t_new, st, S5_SUB)
        xi[:, :, :, sc] = res[:, S5_SUB:].reshape(nt, t_new, st, S5_SUB)
    lr, li = lr_ref[...], li_ref[...]
    hr, hi = h0r_ref[...], h0i_ref[...]
    for t in range(t_new):
        nr = lr * hr - li * hi + xr[:, t].reshape(nseq, S5_FLAT)
        ni = lr * hi + li * hr + xi[:, t].reshape(nseq, S5_FLAT)
        xr[:, t] = nr.reshape(nt, st, S5_FLAT)
        xi[:, t] = ni.reshape(nt, st, S5_FLAT)
        hr, hi = nr, ni
    sr_ref[...] = hr
    si_ref[...] = hi
    ys = []
    for cc in range(S5_UCHUNKS):
        sc = slice(cc * S5_SUB, (cc + 1) * S5_SUB)
        s = jnp.concatenate([xr[:, :, :, sc].reshape(n, S5_SUB), xi[:, :, :, sc].reshape(n, S5_SUB)],
                            axis=1).astype(BF16)
        cols = slice(cc * LANES, (cc + 1) * LANES)
        ys.append(jnp.dot(s, wc_ref[cc], preferred_element_type=F32) + d_ref[:, cols] * u[:, cols])
    o_ref[...] = _s5_tail(jnp.concatenate(ys, axis=1), wglu_ref, bglu_ref).astype(BF16)


def _s5_sample(proj, h0r, h0i, prm, t_new, layer, prev):
    n = proj.shape[0]
    nseq = n // t_new
    names = ("wb", "wc", "lr", "li", "d", "wglu", "bglu")
    st = pl.BlockSpec((None, nseq, S5_FLAT), lambda i: (layer, 0, 0))
    prev, prev_specs, aliases = _alias_inputs(prev, 1)
    scratch = pltpu.VMEM((nseq // SAMPLE_TILE, t_new, SAMPLE_TILE, S5_FLAT), F32)
    return pl.pallas_call(
        _skip_refs(functools.partial(_s5_sample_kernel, nseq=nseq, t_new=t_new), len(prev)),
        grid=(1,),
        in_specs=prev_specs + [pl.BlockSpec((n, S5_WIDTH), lambda i: (0, EVEN_U_BLOCK))]
        + [_layer_spec(prm[k].shape[1:], layer) for k in names]
        + [_layer_spec((nseq, S5_FLAT), layer), _layer_spec((nseq, S5_FLAT), layer)],
        out_specs=[pl.BlockSpec((n, S5_WIDTH), lambda i: (0, 0)), st, st],
        out_shape=[jax.ShapeDtypeStruct((n, S5_WIDTH), BF16),
                   jax.ShapeDtypeStruct(h0r.shape, F32), jax.ShapeDtypeStruct(h0i.shape, F32)],
        input_output_aliases=aliases,
        scratch_shapes=[scratch, scratch],
        compiler_params=_cparams("arbitrary"),
        name="s5_sample",
    )(*prev, proj, *[prm[k] for k in names], h0r, h0i)


def _s5_params(a_re, a_im, log_dt, b_re, b_im, c_re, c_im, d_skip, w_glu, b_glu):
    nl = a_re.shape[0]
    dt = jnp.exp(log_dt)
    mag = jnp.exp(a_re * dt)
    lr, li = mag * jnp.cos(a_im * dt), mag * jnp.sin(a_im * dt)
    den = a_re * a_re + a_im * a_im
    cr = ((lr - 1.0) * a_re + li * a_im) / den
    ci = (li * a_re - (lr - 1.0) * a_im) / den
    bbr = cr[..., None] * b_re - ci[..., None] * b_im
    bbi = cr[..., None] * b_im + ci[..., None] * b_re
    gpc = LANES // S5_GROUP
    eye = jnp.eye(gpc, dtype=F32)

    def in_blocks(bb):
        bb = bb.reshape(nl, S5_UCHUNKS, gpc, S5_STATE, S5_GROUP)
        return jnp.einsum("lcgph,gk->lcghkp", bb, eye).reshape(nl, S5_UCHUNKS, LANES, S5_SUB)

    def out_blocks(cm):
        cm = cm.reshape(nl, S5_UCHUNKS, gpc, S5_GROUP, S5_STATE)
        return jnp.einsum("lcghp,gk->lcgpkh", cm, eye).reshape(nl, S5_UCHUNKS, S5_SUB, LANES)

    wb = jnp.concatenate([in_blocks(bbr), in_blocks(bbi)], axis=3).astype(BF16)
    wc = jnp.concatenate([out_blocks(c_re), -out_blocks(c_im)], axis=2).astype(BF16)
    lr_f, li_f = lr.reshape(nl, 1, S5_FLAT), li.reshape(nl, 1, S5_FLAT)
    lam = jnp.concatenate([lr_f.reshape(nl, S5_SCHUNKS, 1, LANES), li_f.reshape(nl, S5_SCHUNKS, 1, LANES)], axis=1)
    lam8 = jnp.broadcast_to(lam, (nl, 2 * S5_SCHUNKS, SUBLANES, LANES))
    return dict(wb=wb, wc=wc, lam8=lam8, lr=lr_f, li=li_f, d=d_skip.reshape(nl, 1, S5_WIDTH),
                wglu=w_glu.astype(BF16), bglu=b_glu.reshape(nl, 1, S5_WIDTH))


ODD_K_BLOCK = 1
ODD_V_BLOCK = (2 * ML_QK) // ML_WIDTH
ODD_O_BLOCK = ODD_V_BLOCK + 1
ODD_G_BLOCK = (2 * ML_QK + 2 * ML_WIDTH) // LANES
ML_AUG = 2 * ML_DV


def _head_out(h, o, gout):
    hn = h * lax.rsqrt(jnp.mean(h * h, axis=-1, keepdims=True) + EPS) * gout
    return (hn * _sigmoid(o)).astype(BF16)


ODDP_V_BLOCK = 0
ODDP_O_BLOCK = 1
ODDP_Q_BLOCK = (2 * ML_WIDTH) // ML_QK
ODDP_G_BLOCK = (2 * ML_WIDTH + ML_QK) // LANES
ML_SPLIT = 3
ML_PIECE_LANES = 2 * ML_HEADS
ML_SEQS = 2


def _norm_matmul_kt_kernel(x_ref, g_ref, w_ref, wkt_ref, o_ref, kt_ref):
    h = _rms(x_ref[...], g_ref[...]).astype(BF16)
    o_ref[...] = jnp.dot(h, w_ref[...], preferred_element_type=F32)
    kt = lax.dot_general(wkt_ref[...], h, (((1,), (1,)), ((), ())), preferred_element_type=F32)
    kt_ref[...] = kt * (ML_DK ** -0.5)


def _norm_matmul_kt(x, g, layer, w, wkt, widx):
    n, d = x.shape
    m = w.shape[2]
    mk = wkt.shape[1]
    tm = min(ROW_TILE, n)
    return pl.pallas_call(
        _norm_matmul_kt_kernel,
        grid=(n // tm,),
        in_specs=[pl.BlockSpec((tm, d), lambda i: (i, 0)), _layer_spec((1, d), layer), _layer_spec((d, m), widx),
                  _layer_spec((mk, d), widx)],
        out_specs=[pl.BlockSpec((tm, m), lambda i: (i, 0)), pl.BlockSpec((mk, tm), lambda i: (0, i))],
        out_shape=[jax.ShapeDtypeStruct((n, m), F32), jax.ShapeDtypeStruct((mk, n), F32)],
        compiler_params=_cparams("parallel"),
        name="norm_matmul_kt",
    )(x, g, w, wkt)


def _cummax_rows(x):
    n = x.shape[0]
    row = lax.broadcasted_iota(jnp.int32, x.shape, 0)
    shift = 1
    while shift < n:
        x = jnp.maximum(x, jnp.where(row >= shift, pltpu.roll(x, shift, 0), NEG_INF))
        shift *= 2
    return x


def _pieces(x):
    lane = lax.broadcasted_iota(jnp.int32, x.shape, 1)
    xx = x + pltpu.roll(x, ML_PIECE_LANES, 1) + pltpu.roll(x, 2 * ML_PIECE_LANES, 1)
    a1, a2, a3 = _split3(xx)
    return jnp.where(lane < ML_PIECE_LANES, a1, jnp.where(lane < 2 * ML_PIECE_LANES, a2, a3))


def _ml_select_constants():
    mask = np.zeros((ML_HEADS, LANES), np.float32)
    sel = np.zeros((ML_HEADS, LANES, 2 * ML_DV), np.float32)
    for h in range(ML_HEADS):
        for k in range(ML_SPLIT):
            lo, hi = k * ML_PIECE_LANES + h, k * ML_PIECE_LANES + ML_HEADS + h
            mask[h, lo] = mask[h, hi] = 1.0
            sel[h, lo, :ML_DV] = 1.0
            sel[h, hi, ML_DV:] = 1.0
    return jnp.asarray(mask), jnp.asarray(sel, dtype=BF16)


def _mlstm_prompt_kernel(*refs, tc, nchunks, nseq):
    v_ref, o_ref, q_ref, g_ref = refs[:4]
    kt_refs = refs[4:4 + nseq]
    bias_ref, gout_ref, mask_ref, sel_ref, h_ref, c_ref, n_ref, m_ref, caug, mst = refs[4 + nseq:]
    ci = pl.program_id(1)

    @pl.when(ci == 0)
    def _():
        caug[...] = jnp.zeros_like(caug)
        mst[...] = jnp.zeros_like(mst)

    nh = ML_HEADS
    lane = lax.broadcasted_iota(jnp.int32, (tc, LANES), 1)
    lo, hi = lane < nh, (lane >= nh) & (lane < 2 * nh)
    rt = lax.broadcasted_iota(jnp.int32, (tc, tc), 0)
    cs = lax.broadcasted_iota(jnp.int32, (tc, tc), 1)
    causal = cs <= rt
    tril = jnp.where(causal, 1.0, 0.0).astype(BF16)
    ones = jnp.ones((tc, ML_DV), F32)

    def gates(sq):
        g = g_ref[sq] + bias_ref[...]
        lf = jnp.where(hi, _log_sigmoid(g), 0.0)
        b = sum(jnp.dot(tril, p, preferred_element_type=F32) for p in _split3(lf))
        c = jnp.where(hi, pltpu.roll(g, nh, 1) - b, 0.0)
        m_prev = mst[sq]
        mx = jnp.maximum(_cummax_rows(c), m_prev)
        m_row = b + mx
        mx_lo = pltpu.roll(mx, LANES - nh, 1)
        w_inter = jnp.exp(pltpu.roll(m_prev, LANES - nh, 1) - mx_lo)
        mst[sq] = m_row[tc - 1:tc, :]
        return dict(xc=_pieces(jnp.where(lo, w_inter, jnp.where(hi, jnp.exp(-m_row), 0.0))),
                    lc=_pieces(jnp.where(lo, -mx_lo, jnp.where(hi, 1.0, 0.0))),
                    rc=_pieces(jnp.where(lo, 1.0, jnp.where(hi, c, 0.0))))

    gt = [gates(sq) for sq in range(nseq)]
    units = [(sq, hd) for hd in range(nh) for sq in range(nseq)]
    st = [dict() for _ in units]

    def stage1(u):
        sq, hd = units[u]
        d = st[u]
        rh = gt[sq]["rc"] * mask_ref[hd:hd + 1, :].astype(BF16)
        d["dmat"] = lax.dot_general(gt[sq]["lc"], rh, (((1,), (1,)), ((), ())), preferred_element_type=F32)
        d["wb"] = jnp.dot(gt[sq]["xc"], sel_ref[hd], preferred_element_type=F32)
        d["qh"] = q_ref[sq, :, hd * ML_DK:(hd + 1) * ML_DK]
        d["kt"] = kt_refs[sq][hd * ML_DK:(hd + 1) * ML_DK, :]
        d["qk"] = jnp.dot(d["qh"].astype(BF16), d["kt"].astype(BF16), preferred_element_type=F32)

    def stage2(u):
        sq, hd = units[u]
        d = st[u]
        cols = slice(hd * ML_DV, (hd + 1) * ML_DV)
        d["w"] = jnp.exp(jnp.where(causal, d["dmat"], NEG_INF))
        d["vaug"] = jnp.concatenate([v_ref[sq, :, cols], ones], axis=1).astype(BF16)
        d["cm"] = caug[sq, hd]
        lhs = jnp.concatenate([(d["qk"] * d["w"]).astype(BF16), (d["wb"][:, :ML_DK] * d["qh"]).astype(BF16)], axis=1)
        rhs = jnp.concatenate([d["vaug"], d["cm"].astype(BF16)], axis=0)
        d["both"] = jnp.dot(lhs, rhs, preferred_element_type=F32)
        kw = (d["kt"] * d["w"][tc - 1:tc, :]).astype(BF16)
        d["upd"] = jnp.dot(kw, d["vaug"], preferred_element_type=F32)

    def stage3(u):
        sq, hd = units[u]
        d = st[u]
        cols = slice(hd * ML_DV, (hd + 1) * ML_DV)
        both, wb = d["both"], d["wb"]
        h = both[:, :ML_DV] / jnp.maximum(jnp.abs(both[:, ML_DV:]), wb[:, ML_DV:])
        h_ref[sq, :, cols] = _head_out(h, o_ref[sq, :, cols], gout_ref[:, cols])
        decay = wb[tc - 1:tc, :ML_DV]
        caug[sq, hd] = jnp.concatenate([decay, decay], axis=1) * d["cm"] + d["upd"]
        d.clear()

    for step in range(len(units) + 2):
        if step < len(units):
            stage1(step)
        if 0 <= step - 1 < len(units):
            stage2(step - 1)
        if 0 <= step - 2 < len(units):
            stage3(step - 2)

    @pl.when(ci == nchunks - 1)
    def _():
        c_ref[...] = caug[:, :, :, :ML_DV]
        n_ref[...] = caug[:, :, :, ML_DV:]
        m_ref[...] = mst[...]


def _kt_index(b, c, *, sq, nseq, nchunks):
    return 0, (b * nseq + sq) * nchunks + c


def _mlstm_prompt(proj, kt, bias, gout, consts, layer, prev):
    bsz, seq, _ = proj.shape
    tc, nseq = ML_CHUNK, ML_SEQS
    nchunks = seq // tc
    n_layers = bias.shape[0]
    mask, sel = consts
    blk = lambda w, j: pl.BlockSpec((nseq, tc, w), lambda b, c: (b, c, j))
    st = lambda shape: pl.BlockSpec((None, nseq) + shape, lambda b, c: (layer, b) + (0,) * len(shape))
    st_shape = lambda shape: jax.ShapeDtypeStruct((n_layers, bsz) + shape, F32)
    prev, prev_specs, aliases = _alias_inputs(prev, 1)
    return pl.pallas_call(
        _skip_refs(functools.partial(_mlstm_prompt_kernel, tc=tc, nchunks=nchunks, nseq=nseq), len(prev)),
        grid=(bsz // nseq, nchunks),
        in_specs=prev_specs + [
            blk(ML_WIDTH, ODDP_V_BLOCK), blk(ML_WIDTH, ODDP_O_BLOCK), blk(ML_QK, ODDP_Q_BLOCK),
            blk(LANES, ODDP_G_BLOCK)]
        + [pl.BlockSpec((ML_QK, tc), functools.partial(_kt_index, sq=sq, nseq=nseq, nchunks=nchunks))
           for sq in range(nseq)] + [
            _layer_spec((1, LANES), layer), _layer_spec((1, ML_WIDTH), layer),
            _const_spec(mask.shape), _const_spec(sel.shape)],
        out_specs=[blk(ML_WIDTH, 0), st((ML_HEADS, ML_DK, ML_DV)), st((ML_HEADS, ML_DK, ML_DV)), st((1, LANES))],
        out_shape=[jax.ShapeDtypeStruct((bsz, seq, ML_WIDTH), BF16),
                   st_shape((ML_HEADS, ML_DK, ML_DV)), st_shape((ML_HEADS, ML_DK, ML_DV)), st_shape((1, LANES))],
        input_output_aliases=aliases,
        scratch_shapes=[pltpu.VMEM((nseq, ML_HEADS, ML_DK, ML_AUG), F32), pltpu.VMEM((nseq, 1, LANES), F32)],
        compiler_params=_cparams("parallel", "arbitrary"),
        name="mlstm_prompt",
    )(*prev, proj, proj, proj, proj, *([kt] * nseq), bias, gout, mask, sel)


SAMPLE_TILE = SUBLANES
MLS_SEQS = 32


def _mlstm_sample_kernel(v_ref, o_ref, q_ref, g_ref, kt_ref, bias_ref, gout_ref, mask_ref, sel_ref,
                         c0_ref, n0_ref, m0_ref, h_ref, c_ref, n_ref, m_ref, *, nseq, t_new):
    nh, nt, st = ML_HEADS, nseq // SAMPLE_TILE, SAMPLE_TILE
    rows = nseq * t_new
    lane = lax.broadcasted_iota(jnp.int32, (rows, LANES), 1)
    lo, hi = lane < nh, (lane >= nh) & (lane < 2 * nh)
    tiles = lambda a: a.reshape(nt, t_new, st, a.shape[-1])
    flat = lambda a: a.reshape(rows, a.shape[-1])
    per_seq = lambda a: a.reshape(nseq, a.shape[-1])

    g = g_ref[...] + bias_ref[...]
    lf = tiles(jnp.where(hi, _log_sigmoid(g), 0.0))
    ig = tiles(jnp.where(hi, pltpu.roll(g, nh, 1), 0.0))
    m_prev = m0_ref[...].reshape(nt, st, LANES)
    bs, cs, ms = [], [], []
    b_run, m_run = None, m_prev
    for t in range(t_new):
        b_run = lf[:, t] if b_run is None else b_run + lf[:, t]
        c_t = ig[:, t] - b_run
        m_run = jnp.maximum(m_run, c_t)
        bs.append(b_run)
        cs.append(c_t)
        ms.append(m_run)
    stack = lambda xs: flat(jnp.stack(xs, axis=1))
    b, c, mx = stack(bs), stack(cs), stack(ms)
    m_prev_rows = stack([m_prev] * t_new)
    m_row = b + mx
    m_ref[...] = per_seq(bs[-1] + ms[-1])
    mx_lo = pltpu.roll(mx, LANES - nh, 1)
    w_inter = jnp.exp(pltpu.roll(m_prev_rows, LANES - nh, 1) - mx_lo)
    xc = _pieces(jnp.where(lo, w_inter, jnp.where(hi, jnp.exp(-m_row), 0.0)))
    lc = _pieces(jnp.where(lo, -mx_lo, jnp.where(hi, 1.0, 0.0)))
    rc = _pieces(jnp.where(lo, 1.0, jnp.where(hi, c, 0.0)))

    def seq_of(idx):
        return (idx // (t_new * st)) * st + idx % st, (idx % (t_new * st)) // st

    rt = lax.broadcasted_iota(jnp.int32, (rows, rows), 0)
    ct = lax.broadcasted_iota(jnp.int32, (rows, rows), 1)
    (rs, rtok), (cseq, ctok) = seq_of(rt), seq_of(ct)
    valid = (rs == cseq) & (ctok <= rtok)
    rq = lax.broadcasted_iota(jnp.int32, (rows, nseq * ML_DK), 0)
    cq = lax.broadcasted_iota(jnp.int32, (rows, nseq * ML_DK), 1)
    own_q = seq_of(rq)[0] == cq // ML_DK
    rk = lax.broadcasted_iota(jnp.int32, (nseq * ML_DK, rows), 0)
    ck = lax.broadcasted_iota(jnp.int32, (nseq * ML_DK, rows), 1)
    own_k = rk // ML_DK == seq_of(ck)[0]
    ones = jnp.ones((rows, ML_DV), F32)
    last = lambda a: per_seq(tiles(a)[:, t_new - 1])

    for hd in range(nh):
        cols = slice(hd * ML_DV, (hd + 1) * ML_DV)
        rh = rc * mask_ref[hd:hd + 1, :].astype(BF16)
        dmat = lax.dot_general(lc, rh, (((1,), (1,)), ((), ())), preferred_element_type=F32)
        wb = jnp.dot(xc, sel_ref[hd], preferred_element_type=F32)
        qh = q_ref[:, hd * ML_DK:(hd + 1) * ML_DK]
        kt = kt_ref[hd * ML_DK:(hd + 1) * ML_DK, :]
        ktb = kt.astype(BF16)
        w = jnp.exp(jnp.where(valid, dmat, NEG_INF))
        qk = jnp.dot(qh.astype(BF16), ktb, preferred_element_type=F32) * w
        vaug = jnp.concatenate([v_ref[:, cols], ones], axis=1).astype(BF16)
        po = jnp.dot(qk.astype(BF16), vaug, preferred_element_type=F32)
        wq = wb[:, :ML_DK] * qh
        wq2 = jnp.concatenate([wq, wq], axis=1)
        wq_bd = jnp.where(own_q, jnp.concatenate([wq2] * (nseq * ML_DK // LANES), axis=1), 0.0).astype(BF16)
        cstack = c0_ref[:, hd].reshape(nseq * ML_DK, ML_DV)
        num = po[:, :ML_DV] + jnp.dot(wq_bd, cstack.astype(BF16), preferred_element_type=F32)
        n0 = n0_ref[hd]
        n_rows = stack([n0.reshape(nt, st, ML_DK)] * t_new)
        den = po[:, ML_DV:] + jnp.sum(wq * n_rows, axis=-1, keepdims=True)
        h = num / jnp.maximum(jnp.abs(den), wb[:, ML_DV:])
        h_ref[:, cols] = _head_out(h, o_ref[:, cols], gout_ref[:, cols])
        w_last = last(w)
        decay = last(wb[:, :ML_DV])
        n_upd = lax.dot_general(w_last.astype(BF16), ktb, (((1,), (1,)), ((), ())), preferred_element_type=F32)
        n_ref[hd] = decay[:, :ML_DK] * n0 + n_upd
        wk = jnp.sum(w_last, axis=0, keepdims=True)
        kw_bd = jnp.where(own_k, jnp.concatenate([kt * wk] * nseq, axis=0), 0.0).astype(BF16)
        upd = jnp.dot(kw_bd, v_ref[:, cols].astype(BF16), preferred_element_type=F32)
        decay_rows = jnp.broadcast_to(decay[:, None, :], (nseq, ML_DK, ML_DV)).reshape(nseq * ML_DK, ML_DV)
        c_ref[:, hd] = (decay_rows * cstack + upd).reshape(nseq, ML_DK, ML_DV)


def _mlstm_sample(proj, kt, bias, gout, consts, c0, n0h, m0, t_new, layer, prev):
    n = proj.shape[0]
    bsz = n // t_new
    nseq = MLS_SEQS
    rows = nseq * t_new
    mask, sel = consts
    blk = lambda w, j: pl.BlockSpec((rows, w), lambda i: (i, j))
    cst = pl.BlockSpec((None, nseq, ML_HEADS, ML_DK, ML_DV), lambda i: (layer, i, 0, 0, 0))
    nst = pl.BlockSpec((None, ML_HEADS, nseq, ML_DK), lambda i: (layer, 0, i, 0))
    mst = pl.BlockSpec((None, nseq, LANES), lambda i: (layer, i, 0))
    prev, prev_specs, aliases = _alias_inputs(prev, 1)
    return pl.pallas_call(
        _skip_refs(functools.partial(_mlstm_sample_kernel, nseq=nseq, t_new=t_new), len(prev)),
        grid=(bsz // nseq,),
        in_specs=prev_specs + [
            blk(ML_WIDTH, ODDP_V_BLOCK), blk(ML_WIDTH, ODDP_O_BLOCK), blk(ML_QK, ODDP_Q_BLOCK),
            blk(LANES, ODDP_G_BLOCK), pl.BlockSpec((ML_QK, rows), lambda i: (0, i)),
            _layer_spec((1, LANES), layer), _layer_spec((1, ML_WIDTH), layer),
            _const_spec(mask.shape), _const_spec(sel.shape), cst, nst, mst],
        out_specs=[blk(ML_WIDTH, 0), cst, nst, mst],
        out_shape=[jax.ShapeDtypeStruct((n, ML_WIDTH), BF16),
                   jax.ShapeDtypeStruct(c0.shape, F32), jax.ShapeDtypeStruct(n0h.shape, F32),
                   jax.ShapeDtypeStruct(m0.shape, F32)],
        input_output_aliases=aliases,
        compiler_params=_cparams("parallel"),
        name="mlstm_sample",
    )(*prev, proj, proj, proj, proj, kt, bias, gout, mask, sel, c0, n0h, m0)


def _pad_lanes(x):
    return jnp.pad(x, [(0, 0)] * (x.ndim - 1) + [(0, LANES - x.shape[-1])])


def kernel(x_prompt, x_sample, cache_k, cache_v, state_ssm_re, state_ssm_im, state_mlstm_c, state_mlstm_n, state_mlstm_m, norm_mix, norm_ffn, w_in_even, q_norm, k_norm, attn_sinks, s5_a_re, s5_a_im, s5_log_dt, s5_b_re, s5_b_im, s5_c_re, s5_c_im, s5_d, s5_w_glu, s5_b_glu, w_out_even, w_in_odd, ml_b_i, ml_b_f, ml_out_norm, w_out_odd, w_gate, w_up, w_down):
    bp, lp, _ = x_prompt.shape
    bsm, ls, _ = x_sample.shape
    yp = x_prompt.reshape(bp * lp, D_MODEL)
    ys = x_sample.reshape(bsm // SAMPLE_TILE, SAMPLE_TILE, ls, D_MODEL).transpose(0, 2, 1, 3).reshape(bsm * ls, D_MODEL)
    tab_p = _rope_tables(jnp.arange(lp))
    tab_s = tuple(jnp.repeat(t, SAMPLE_TILE, axis=0) for t in _rope_tables(PAST_LEN + jnp.arange(ls)))
    n_even, n_odd = w_in_even.shape[0], w_in_odd.shape[0]

    g_mix = norm_mix.reshape(DEPTH, 1, D_MODEL)
    g_ffn = norm_ffn.reshape(DEPTH, 1, D_MODEL)
    wg, wu, wd = w_gate.astype(BF16), w_up.astype(BF16), w_down.astype(BF16)
    kv0, u0 = ATTN_WIDTH, ATTN_WIDTH + 2 * KV_WIDTH
    order = jnp.asarray(ATTN_HEAD_ORDER)
    wq = w_in_even[..., :kv0].reshape(n_even, D_MODEL, ATTN_HEADS, HEAD_DIM)[:, :, order].reshape(n_even, D_MODEL, kv0)
    w_in_e = jnp.concatenate([wq, w_in_even[..., u0:], w_in_even[..., kv0:u0]], axis=-1).astype(BF16)
    wo_attn = w_out_even[:, :kv0].reshape(n_even, ATTN_HEADS, HEAD_DIM, D_MODEL)[:, order].reshape(n_even, kv0, D_MODEL)
    w_out_e = jnp.concatenate([wo_attn, w_out_even[:, kv0:]], axis=1).astype(BF16)
    gq = jnp.tile(q_norm, (1, LANES // HEAD_DIM)).reshape(n_even, 1, LANES)
    gk = jnp.tile(k_norm, (1, LANES // HEAD_DIM)).reshape(n_even, 1, LANES)
    prm = _s5_params(s5_a_re, s5_a_im, s5_log_dt, s5_b_re, s5_b_im, s5_c_re, s5_c_im, s5_d, s5_w_glu, s5_b_glu)
    k0, v0, g0 = ML_QK, 2 * ML_QK, 2 * ML_QK + 2 * ML_WIDTH
    w_in_o = jnp.concatenate([w_in_odd[..., v0:g0], w_in_odd[..., :k0], _pad_lanes(w_in_odd[..., g0:])],
                             axis=-1).astype(BF16)
    w_kt = jnp.swapaxes(w_in_odd[..., k0:v0], 1, 2).astype(BF16)
    ml_consts = _ml_select_constants()
    w_out_o = w_out_odd.astype(BF16)
    ml_bias = _pad_lanes(jnp.concatenate([ml_b_i, ml_b_f], axis=-1)).reshape(n_odd, 1, LANES)
    ml_gout = ml_out_norm.reshape(n_odd, 1, ML_WIDTH)
    ck = cache_k.reshape(n_even, bsm, WINDOW, KV_WIDTH)
    cv = cache_v.reshape(n_even, bsm, WINDOW, KV_WIDTH)
    h0r = state_ssm_re.reshape(n_even, bsm, S5_FLAT)
    h0i = state_ssm_im.reshape(n_even, bsm, S5_FLAT)
    n0h = jnp.swapaxes(state_mlstm_n, 1, 2)
    m0 = jnp.pad(state_mlstm_m, ((0, 0), (0, 0), (ML_HEADS, LANES - 2 * ML_HEADS)))

    p_attn = p_ssm = p_ml = s_attn = s_ssm = s_ml = None
    for layer in range(DEPTH):
        ffn = (layer, g_ffn, wg, wu, wd)
        if layer % 2 == 0:
            e = layer // 2
            proj = _norm_matmul(yp, g_mix, layer, w_in_e, e)
            proj3 = proj.reshape(bp, lp, -1)
            attn, *p_attn = _attn_prompt(proj3, tab_p, gq, gk, attn_sinks, e, p_attn)
            ssm, *p_ssm = _s5_prompt(proj3, prm, e, p_ssm)
            yp = _mix_ffn(yp, [attn.reshape(bp * lp, -1), ssm.reshape(bp * lp, -1)], w_out_e, e, *ffn)
            proj = _norm_matmul(ys, g_mix, layer, w_in_e, e)
            attn, *s_attn = _attn_sample(proj, ck, cv, tab_s, gq, gk, attn_sinks, ls, e, s_attn)
            ssm, *s_ssm = _s5_sample(proj, h0r, h0i, prm, ls, e, s_ssm)
            ys = _mix_ffn(ys, [attn, ssm], w_out_e, e, *ffn)
        else:
            o = layer // 2
            proj, kt = _norm_matmul_kt(yp, g_mix, layer, w_in_o, w_kt, o)
            hh, *p_ml = _mlstm_prompt(proj.reshape(bp, lp, -1), kt, ml_bias, ml_gout, ml_consts, o, p_ml)
            yp = _mix_ffn(yp, [hh.reshape(bp * lp, -1)], w_out_o, o, *ffn)
            proj, kt = _norm_matmul_kt(ys, g_mix, layer, w_in_o, w_kt, o)
            hh, *s_ml = _mlstm_sample(proj, kt, ml_bias, ml_gout, ml_consts, state_mlstm_c, n0h, m0, ls, o, s_ml)
            ys = _mix_ffn(ys, [hh], w_out_o, o, *ffn)
    heads = lambda a: a.reshape(a.shape[:3] + (KV_HEADS, HEAD_DIM))
    groups = lambda a: a.reshape(a.shape[:2] + (S5_GROUPS, S5_STATE))
    ys = ys.reshape(bsm // SAMPLE_TILE, ls, SAMPLE_TILE, D_MODEL).transpose(0, 2, 1, 3).reshape(bsm, ls, D_MODEL)
    return (yp.reshape(bp, lp, D_MODEL), ys,
            heads(p_attn[0]), heads(p_attn[1]), groups(p_ssm[0]), groups(p_ssm[1]),
            p_ml[0], p_ml[1][..., 0], p_ml[2][:, :, 0, ML_HEADS:2 * ML_HEADS],
            heads(s_attn[0]), heads(s_attn[1]), groups(s_ssm[0]), groups(s_ssm[1]),
            s_ml[0], jnp.swapaxes(s_ml[1], 1, 2), s_ml[2][..., ML_HEADS:2 * ML_HEADS])
```

```python
import functools

import numpy as np

import jax
import jax.numpy as jnp
from jax import lax
from jax.experimental import pallas as pl
from jax.experimental.pallas import tpu as pltpu

F32 = jnp.float32
BF16 = jnp.bfloat16

D_MODEL = 1024
DEPTH = 4
PAST_LEN = 8192
WINDOW = 128
ATTN_HEADS = 8
KV_HEADS = 2
HEAD_DIM = 64
ATTN_WIDTH = ATTN_HEADS * HEAD_DIM
KV_WIDTH = KV_HEADS * HEAD_DIM
ROT_DIM = HEAD_DIM // 4
ROPE_THETA = 500000.0
S5_GROUP = 16
S5_WIDTH = D_MODEL // 2
S5_GROUPS = S5_WIDTH // S5_GROUP
S5_STATE = 64
S5_FLAT = S5_GROUPS * S5_STATE
ML_HEADS = 8
ML_DV = D_MODEL // ML_HEADS
ML_DK = ML_DV // 2
ML_QK = ML_HEADS * ML_DK
ML_WIDTH = ML_HEADS * ML_DV
ODD_IN = 2 * ML_QK + 2 * ML_WIDTH + 2 * ML_HEADS
ODD_IN_PAD = 2 * ML_QK + 2 * ML_WIDTH + 128
D_FF = 2816
EPS = 1e-6

LANES = 128
SUBLANES = 8
ROW_TILE = 512
FF_TILE = 256
S5_CHUNK = 64
ML_CHUNK = 128
SAMPLE_BLOCK = 8
VMEM_LIMIT = 56 * 1024 * 1024

NEG_INF = float("-inf")


def _cparams(*sem):
    return pltpu.CompilerParams(dimension_semantics=sem, vmem_limit_bytes=VMEM_LIMIT)


def _const_spec(shape):
    zeros = (0,) * len(shape)
    return pl.BlockSpec(shape, lambda *_: zeros, pipeline_mode=pl.Buffered(1))


def _layer_spec(shape, layer):
    zeros = (0,) * len(shape)
    return pl.BlockSpec((None,) + tuple(shape), lambda *_: (layer,) + zeros, pipeline_mode=pl.Buffered(1))


def _skip_refs(body, n_skip):
    if n_skip == 0:
        return body

    def wrapped(*refs):
        return body(*refs[n_skip:])

    return wrapped


def _alias_inputs(prev, first_state_out):
    prev = () if prev is None else tuple(prev)
    specs = [pl.BlockSpec(memory_space=pl.ANY) for _ in prev]
    aliases = {i: first_state_out + i for i in range(len(prev))}
    return prev, specs, aliases


def _rms(x, g):
    ms = jnp.mean(x * x, axis=-1, keepdims=True)
    return x * lax.rsqrt(ms + EPS) * g


def _split3(a):
    a1 = a.astype(BF16)
    r1 = a - a1.astype(F32)
    a2 = r1.astype(BF16)
    a3 = (r1 - a2.astype(F32)).astype(BF16)
    return a1, a2, a3


def _log_sigmoid(x):
    return jnp.minimum(x, 0.0) - jnp.log1p(jnp.exp(-jnp.abs(x)))


def _sigmoid(x):
    return 1.0 / (1.0 + jnp.exp(-x))


def _norm_matmul_kernel(x_ref, g_ref, w_ref, o_ref):
    h = _rms(x_ref[...], g_ref[...]).astype(BF16)
    o_ref[...] = jnp.dot(h, w_ref[...], preferred_element_type=F32)


def _norm_matmul(x, g, layer, w, widx):
    n, d = x.shape
    m = w.shape[2]
    tm = min(ROW_TILE, n)
    return pl.pallas_call(
        _norm_matmul_kernel,
        grid=(n // tm,),
        in_specs=[pl.BlockSpec((tm, d), lambda i: (i, 0)), _layer_spec((1, d), layer), _layer_spec((d, m), widx)],
        out_specs=pl.BlockSpec((tm, m), lambda i: (i, 0)),
        out_shape=jax.ShapeDtypeStruct((n, m), F32),
        compiler_params=_cparams("parallel"),
        name="norm_matmul",
    )(x, g, w)


def _mix_ffn_kernel(*refs, n_mix):
    x_ref = refs[0]
    a_refs = refs[1:1 + n_mix]
    wo_ref, g_ref, wg_ref, wu_ref, wd_ref, o_ref, act_ref = refs[1 + n_mix:]
    y = x_ref[...]
    off = 0
    for a_ref in a_refs:
        ka = a_ref.shape[1]
        y = y + jnp.dot(a_ref[...], wo_ref[off:off + ka, :], preferred_element_type=F32)
        off += ka
    h = _rms(y, g_ref[...]).astype(BF16)
    for f in range(D_FF // FF_TILE):
        cols = slice(f * FF_TILE, (f + 1) * FF_TILE)
        gate = jnp.dot(h, wg_ref[:, cols], preferred_element_type=F32)
        up = jnp.dot(h, wu_ref[:, cols], preferred_element_type=F32)
        act_ref[:, cols] = (gate * _sigmoid(gate) * up).astype(BF16)
    o_ref[...] = y + jnp.dot(act_ref[...], wd_ref[...], preferred_element_type=F32)


def _mix_ffn(x, mixes, w_out, oidx, layer, g_ffn, wg, wu, wd):
    n, d = x.shape
    tm = min(ROW_TILE, n)
    row = lambda i: (i, 0)
    in_specs = [pl.BlockSpec((tm, d), row)]
    in_specs += [pl.BlockSpec((tm, a.shape[1]), row) for a in mixes]
    in_specs += [_layer_spec(w_out.shape[1:], oidx), _layer_spec((1, d), layer), _layer_spec(wg.shape[1:], layer),
                 _layer_spec(wu.shape[1:], layer), _layer_spec(wd.shape[1:], layer)]
    return pl.pallas_call(
        functools.partial(_mix_ffn_kernel, n_mix=len(mixes)),
        grid=(n // tm,),
        in_specs=in_specs,
        out_specs=pl.BlockSpec((tm, d), row),
        out_shape=jax.ShapeDtypeStruct((n, d), F32),
        scratch_shapes=[pltpu.VMEM((tm, D_FF), BF16)],
        compiler_params=_cparams("parallel"),
        name="mix_ffn",
    )(x, *mixes, w_out, g_ffn, wg, wu, wd)


def _head_ones():
    r = lax.broadcasted_iota(jnp.int32, (LANES, LANES), 0) // HEAD_DIM
    c = lax.broadcasted_iota(jnp.int32, (LANES, LANES), 1) // HEAD_DIM
    return jnp.where(r == c, 1.0, 0.0).astype(BF16)


def _qk_prep(x, g, ones, ct, sa, sb):
    x2 = x * x
    hi = x2.astype(BF16)
    lo = (x2 - hi.astype(F32)).astype(BF16)
    ss = jnp.dot(hi, ones, preferred_element_type=F32) + jnp.dot(lo, ones, preferred_element_type=F32)
    xn = x * lax.rsqrt(ss * (1.0 / HEAD_DIM) + EPS) * g
    return xn * ct + pltpu.roll(xn, LANES - ROT_DIM // 2, 1) * sa + pltpu.roll(xn, ROT_DIM // 2, 1) * sb


def _rope_tables(pos):
    half = ROT_DIM // 2
    inv = jnp.power(jnp.float32(ROPE_THETA), -jnp.arange(half, dtype=F32) / half)
    ang = pos.astype(F32)[:, None] * inv[None, :]
    cos, sin = jnp.cos(ang), jnp.sin(ang)
    n = pos.shape[0]
    one = jnp.ones((n, HEAD_DIM - ROT_DIM), F32)
    zero = jnp.zeros((n, HEAD_DIM - ROT_DIM), F32)
    z8 = jnp.zeros((n, half), F32)
    ct = jnp.concatenate([cos, cos, one], axis=1)
    sa = jnp.concatenate([-sin, z8, zero], axis=1)
    sb = jnp.concatenate([z8, sin, zero], axis=1)
    tile = lambda t: jnp.concatenate([t, t], axis=1)
    return tile(ct), tile(sa), tile(sb)


ATTN_SEQS = 4
ATTN_QCHUNKS = ATTN_WIDTH // LANES
ATTN_HEAD_ORDER = tuple(h * ATTN_QCHUNKS + j for j in range(ATTN_QCHUNKS) for h in range(KV_HEADS))


def _attn_prompt_kernel(q_ref, kv_ref, ct_ref, sa_ref, sb_ref, gq_ref, gk_ref, sink_ref,
                        o_ref, pk_ref, pv_ref, kprev, vprev, *, nb, layer, nseq):
    i = pl.program_id(1)

    @pl.when(i == 0)
    def _():
        kprev[...] = jnp.zeros_like(kprev)
        vprev[...] = jnp.zeros_like(vprev)

    ones = _head_ones()
    ct, sa, sb = ct_ref[...], sa_ref[...], sb_ref[...]
    r = lax.broadcasted_iota(jnp.int32, (WINDOW, 2 * WINDOW), 0)
    c = lax.broadcasted_iota(jnp.int32, (WINDOW, 2 * WINDOW), 1)
    rel = r + WINDOW - c
    mask = (rel >= 0) & (rel <= WINDOW) & ((c >= WINDOW) | (i > 0))
    lane = lax.broadcasted_iota(jnp.int32, (WINDOW, LANES), 1)
    group0 = lane < HEAD_DIM
    v_ones = jnp.ones((2 * WINDOW, LANES), BF16)
    nq = ATTN_QCHUNKS
    st = [dict() for _ in range(nseq)]

    def prep(sq):
        d = st[sq]
        kv = kv_ref[sq]
        d["kn"] = _qk_prep(kv[:, :KV_WIDTH], gk_ref[...], ones, ct, sa, sb)
        d["v"] = kv[:, KV_WIDTH:]
        d["qn"] = [_qk_prep(q_ref[sq, :, j * LANES:(j + 1) * LANES], gq_ref[...], ones, ct, sa, sb)
                   * (HEAD_DIM ** -0.5) for j in range(nq)]
        d["kcat"] = jnp.concatenate([kprev[sq], d["kn"]], axis=0).astype(BF16)
        d["vaug"] = jnp.concatenate([jnp.concatenate([vprev[sq], d["v"]], axis=0).astype(BF16), v_ones], axis=1)
        kprev[sq] = d["kn"]
        vprev[sq] = d["v"]

    def scores(sq, h):
        d = st[sq]
        keep = group0 if h == 0 else jnp.logical_not(group0)
        qs = jnp.concatenate([jnp.where(keep, qj, 0.0) for qj in d["qn"]], axis=0).astype(BF16)
        d["s", h] = lax.dot_general(qs, d["kcat"], (((1,), (1,)), ((), ())), preferred_element_type=F32)

    def softmax_pv(sq, h):
        d = st[sq]
        s = d.pop(("s", h))
        ps, corr = [], []
        for j in range(nq):
            sg = jnp.where(mask, s[j * WINDOW:(j + 1) * WINDOW], NEG_INF)
            sink = sink_ref[layer, h * nq + j]
            m = jnp.maximum(jnp.max(sg, axis=-1, keepdims=True), sink)
            ps.append(jnp.exp(sg - m).astype(BF16))
            corr.append(jnp.exp(sink - m))
        o = jnp.dot(jnp.concatenate(ps, axis=0), d["vaug"], preferred_element_type=F32)
        d["o", h] = [o[j * WINDOW:(j + 1) * WINDOW, :LANES] / (o[j * WINDOW:(j + 1) * WINDOW, LANES:] + corr[j])
                     for j in range(nq)]

    def finish(sq):
        d = st[sq]
        o_ref[sq] = jnp.concatenate([jnp.where(group0, d["o", 0][j], d["o", 1][j]) for j in range(nq)],
                                    axis=1).astype(BF16)

    for sq in range(nseq):
        prep(sq)
    for sq in range(nseq):
        scores(sq, 0)
        scores(sq, 1)
    for sq in range(nseq):
        softmax_pv(sq, 0)
        softmax_pv(sq, 1)
        finish(sq)

    @pl.when(i == nb - 1)
    def _():
        for sq in range(nseq):
            pk_ref[sq] = st[sq]["kn"]
            pv_ref[sq] = st[sq]["v"]


def _attn_prompt(proj, tables, gq, gk, sinks, layer, prev):
    bsz, seq, _ = proj.shape
    nb = seq // WINDOW
    nseq = ATTN_SEQS
    n_layers = gq.shape[0]
    tab = pl.BlockSpec((WINDOW, LANES), lambda b, i: (i, 0))
    prev, prev_specs, aliases = _alias_inputs(prev, 1)
    win = pl.BlockSpec((None, nseq, WINDOW, KV_WIDTH), lambda b, i: (layer, b, 0, 0))
    win_shape = jax.ShapeDtypeStruct((n_layers, bsz, WINDOW, KV_WIDTH), F32)
    return pl.pallas_call(
        _skip_refs(functools.partial(_attn_prompt_kernel, nb=nb, layer=layer, nseq=nseq), len(prev)),
        grid=(bsz // nseq, nb),
        in_specs=prev_specs + [
            pl.BlockSpec((nseq, WINDOW, ATTN_WIDTH), lambda b, i: (b, i, 0)),
            pl.BlockSpec((nseq, WINDOW, 2 * KV_WIDTH), lambda b, i: (b, i, EVEN_KV_BLOCK)),
            tab, tab, tab, _layer_spec((1, LANES), layer), _layer_spec((1, LANES), layer),
            pl.BlockSpec(memory_space=pltpu.SMEM)],
        out_specs=[pl.BlockSpec((nseq, WINDOW, ATTN_WIDTH), lambda b, i: (b, i, 0)), win, win],
        out_shape=[jax.ShapeDtypeStruct((bsz, seq, ATTN_WIDTH), BF16), win_shape, win_shape],
        input_output_aliases=aliases,
        scratch_shapes=[pltpu.VMEM((nseq, WINDOW, KV_WIDTH), F32), pltpu.VMEM((nseq, WINDOW, KV_WIDTH), F32)],
        compiler_params=_cparams("parallel", "arbitrary"),
        name="attn_prompt",
    )(*prev, proj, proj, *tables, gq, gk, sinks)


EVEN_U_BLOCK = ATTN_WIDTH // S5_WIDTH
EVEN_KV_BLOCK = (ATTN_WIDTH + S5_WIDTH) // (2 * KV_WIDTH)
KALL_ROWS = WINDOW + SUBLANES


def _attn_sample_kernel(q_ref, kv_ref, ck_ref, cv_ref, ct_ref, sa_ref, sb_ref, gq_ref, gk_ref, sink_ref,
                        o_ref, nk_ref, nv_ref, o_seq, *, bs, t_new, layer):
    ones = _head_ones()
    ct, sa, sb = ct_ref[...], sa_ref[...], sb_ref[...]
    kv = kv_ref[...]
    kn = _qk_prep(kv[:, :KV_WIDTH], gk_ref[...], ones, ct, sa, sb)
    v = kv[:, KV_WIDTH:]
    nq = ATTN_QCHUNKS
    qn = [_qk_prep(q_ref[:, j * LANES:(j + 1) * LANES], gq_ref[...], ones, ct, sa, sb) * (HEAD_DIM ** -0.5)
          for j in range(nq)]
    rows = nq * t_new
    r = lax.broadcasted_iota(jnp.int32, (rows, KALL_ROWS), 0)
    c = lax.broadcasted_iota(jnp.int32, (rows, KALL_ROWS), 1)
    t = r % t_new
    mask = (c >= t) & (c <= t + WINDOW)
    rj = lax.broadcasted_iota(jnp.int32, (rows, 1), 0) // t_new
    lane = lax.broadcasted_iota(jnp.int32, (t_new, LANES), 1)
    group0 = lane < HEAD_DIM
    pad = jnp.zeros((KALL_ROWS - WINDOW - t_new, KV_WIDTH), F32)
    v_ones = jnp.ones((KALL_ROWS, LANES), BF16)

    def seq_rows(a, b):
        return jnp.concatenate([a[tt * bs + b:tt * bs + b + 1] for tt in range(t_new)], axis=0)

    sinks = []
    for h in range(KV_HEADS):
        sk = jnp.zeros((rows, 1), F32)
        for j in range(nq):
            sk = jnp.where(rj == j, sink_ref[layer, h * nq + j], sk)
        sinks.append(sk)

    st = [dict() for _ in range(bs)]
    for b in range(bs):
        d = st[b]
        ck, cv = ck_ref[b], cv_ref[b]
        kn_b, v_b = seq_rows(kn, b), seq_rows(v, b)
        d["kall"] = jnp.concatenate([ck, kn_b, pad], axis=0).astype(BF16)
        d["vaug"] = jnp.concatenate([jnp.concatenate([cv, v_b, pad], axis=0).astype(BF16), v_ones], axis=1)
        nk_ref[b] = pltpu.roll(ck, WINDOW - t_new, 0)
        nv_ref[b] = pltpu.roll(cv, WINDOW - t_new, 0)
        nk_ref[b, WINDOW - t_new:WINDOW, :] = kn_b
        nv_ref[b, WINDOW - t_new:WINDOW, :] = v_b
        qb = [seq_rows(qj, b) for qj in qn]
        for h in range(KV_HEADS):
            keep = group0 if h == 0 else jnp.logical_not(group0)
            qs = jnp.concatenate([jnp.where(keep, q, 0.0) for q in qb], axis=0).astype(BF16)
            d["s", h] = lax.dot_general(qs, d["kall"], (((1,), (1,)), ((), ())), preferred_element_type=F32)
    for b in range(bs):
        d = st[b]
        for h in range(KV_HEADS):
            s = jnp.where(mask, d.pop(("s", h)), NEG_INF)
            m = jnp.maximum(jnp.max(s, axis=-1, keepdims=True), sinks[h])
            o = jnp.dot(jnp.exp(s - m).astype(BF16), d["vaug"], preferred_element_type=F32)
            d["o", h] = o[:, :LANES] / (o[:, LANES:] + jnp.exp(sinks[h] - m))
    for b in range(bs):
        d = st[b]
        o_b = jnp.concatenate([jnp.where(group0, d["o", 0][j * t_new:(j + 1) * t_new],
                                         d["o", 1][j * t_new:(j + 1) * t_new]) for j in range(nq)], axis=1)
        for tt in range(t_new):
            o_seq[tt * bs + b:tt * bs + b + 1, :] = o_b[tt:tt + 1]
    o_ref[...] = o_seq[...].astype(BF16)


def _attn_sample(proj, cache_k, cache_v, tables, gq, gk, sinks, t_new, layer, prev):
    n = proj.shape[0]
    bsz = n // t_new
    bs = SAMPLE_BLOCK
    rows = bs * t_new
    row = lambda i: (i, 0)
    cache = pl.BlockSpec((None, bs, WINDOW, KV_WIDTH), lambda i: (layer, i, 0, 0))
    prev, prev_specs, aliases = _alias_inputs(prev, 1)
    return pl.pallas_call(
        _skip_refs(functools.partial(_attn_sample_kernel, bs=bs, t_new=t_new, layer=layer), len(prev)),
        grid=(bsz // bs,),
        in_specs=prev_specs + [
            pl.BlockSpec((rows, ATTN_WIDTH), row),
            pl.BlockSpec((rows, 2 * KV_WIDTH), lambda i: (i, EVEN_KV_BLOCK)),
            cache, cache,
            _const_spec((rows, LANES)), _const_spec((rows, LANES)), _const_spec((rows, LANES)),
            _layer_spec((1, LANES), layer), _layer_spec((1, LANES), layer),
            pl.BlockSpec(memory_space=pltpu.SMEM)],
        out_specs=[pl.BlockSpec((rows, ATTN_WIDTH), row), cache, cache],
        out_shape=[jax.ShapeDtypeStruct((n, ATTN_WIDTH), BF16),
                   jax.ShapeDtypeStruct(cache_k.shape, F32), jax.ShapeDtypeStruct(cache_v.shape, F32)],
        input_output_aliases=aliases,
        scratch_shapes=[pltpu.VMEM((rows, ATTN_WIDTH), F32)],
        compiler_params=_cparams("parallel"),
        name="attn_sample",
    )(*prev, proj, proj, cache_k, cache_v, *tables, gq, gk, sinks)


S5_UCHUNKS = S5_WIDTH // LANES
S5_SUB = S5_FLAT // S5_UCHUNKS
S5_SCHUNKS = S5_FLAT // LANES


def _s5_tail(y, wglu_ref, bglu_ref):
    g = 0.5 * y * (1.0 + lax.erf(y * (2.0 ** -0.5)))
    z = jnp.dot(g.astype(BF16), wglu_ref[...], preferred_element_type=F32) + bglu_ref[...]
    return g * _sigmoid(z)


S5_PARTS = 2


def _s5_prompt_kernel(u_ref, wb_ref, wc_ref, lam_ref, d_ref, wglu_ref, bglu_ref,
                      o_ref, sr_ref, si_ref, xs, hst, *, nbatch, tc):
    rows = nbatch * tc
    prow, ptok = rows // S5_PARTS, tc // S5_PARTS

    @pl.when(pl.program_id(1) == 0)
    def _():
        hst[...] = jnp.zeros_like(hst)

    u = jnp.swapaxes(u_ref[...], 0, 1).reshape(rows, S5_WIDTH)
    ub = u.astype(BF16)

    def in_proj(p, cc):
        rs = slice(p * prow, (p + 1) * prow)
        res = jnp.dot(ub[rs, cc * LANES:(cc + 1) * LANES], wb_ref[cc], preferred_element_type=F32)
        for j in range(S5_SUB // LANES):
            xs[cc * 4 + j, rs, :] = res[:, j * LANES:(j + 1) * LANES]
            xs[S5_SCHUNKS + cc * 4 + j, rs, :] = res[:, S5_SUB + j * LANES:S5_SUB + (j + 1) * LANES]

    ys = {}

    def out_proj(p, cc):
        rs = slice(p * prow, (p + 1) * prow)
        s = jnp.concatenate([xs[cc * 4 + j, rs, :] for j in range(4)]
                            + [xs[S5_SCHUNKS + cc * 4 + j, rs, :] for j in range(4)], axis=1).astype(BF16)
        cols = slice(cc * LANES, (cc + 1) * LANES)
        ys[p, cc] = jnp.dot(s, wc_ref[cc], preferred_element_type=F32) + d_ref[:, cols] * u[rs, cols]

    def tail(p):
        out = _s5_tail(jnp.concatenate([ys.pop((p, cc)) for cc in range(S5_UCHUNKS)], axis=1), wglu_ref, bglu_ref)
        o_ref[:, p * ptok:(p + 1) * ptok, :] = jnp.swapaxes(out.reshape(ptok, nbatch, S5_WIDTH), 0, 1).astype(BF16)

    def scan_step(t, h):
        idx = slice(t * nbatch, (t + 1) * nbatch)
        new = list(h)
        for k in range(S5_SCHUNKS):
            hr, hi = h[k], h[S5_SCHUNKS + k]
            lr, li = lam_ref[k], lam_ref[S5_SCHUNKS + k]
            nr = lr * hr - li * hi + xs[k, idx, :]
            ni = lr * hi + li * hr + xs[S5_SCHUNKS + k, idx, :]
            xs[k, idx, :] = nr
            xs[S5_SCHUNKS + k, idx, :] = ni
            new[k], new[S5_SCHUNKS + k] = nr, ni
        return new

    for cc in range(S5_UCHUNKS):
        in_proj(0, cc)
    h = [hst[k] for k in range(2 * S5_SCHUNKS)]
    for p in range(S5_PARTS):
        work = []
        if p + 1 < S5_PARTS:
            work += [functools.partial(in_proj, p + 1, cc) for cc in range(S5_UCHUNKS)]
        if p >= 1:
            work += [functools.partial(out_proj, p - 1, cc) for cc in range(S5_UCHUNKS)]
            work.append(functools.partial(tail, p - 1))
        every = max(1, ptok // max(1, len(work)))
        for i in range(ptok):
            h = scan_step(p * ptok + i, h)
            if work and (i + 1) % every == 0:
                work.pop(0)()
        for w in work:
            w()
    for cc in range(S5_UCHUNKS):
        out_proj(S5_PARTS - 1, cc)
    tail(S5_PARTS - 1)
    for k in range(2 * S5_SCHUNKS):
        hst[k] = h[k]
    sr_ref[...] = jnp.concatenate(h[:S5_SCHUNKS], axis=1)
    si_ref[...] = jnp.concatenate(h[S5_SCHUNKS:], axis=1)


def _s5_prompt(proj, prm, layer, prev):
    bsz, seq, _ = proj.shape
    nbatch, tc = SUBLANES, S5_CHUNK
    n_layers = prm["wb"].shape[0]
    st = pl.BlockSpec((None, nbatch, S5_FLAT), lambda b, c: (layer, b, 0))
    st_shape = jax.ShapeDtypeStruct((n_layers, bsz, S5_FLAT), F32)
    prev, prev_specs, aliases = _alias_inputs(prev, 1)
    names = ("wb", "wc", "lam8", "d", "wglu", "bglu")
    return pl.pallas_call(
        _skip_refs(functools.partial(_s5_prompt_kernel, nbatch=nbatch, tc=tc), len(prev)),
        grid=(bsz // nbatch, seq // tc),
        in_specs=prev_specs + [pl.BlockSpec((nbatch, tc, S5_WIDTH), lambda b, c: (b, c, EVEN_U_BLOCK))]
        + [_layer_spec(prm[k].shape[1:], layer) for k in names],
        out_specs=[pl.BlockSpec((nbatch, tc, S5_WIDTH), lambda b, c: (b, c, 0)), st, st],
        out_shape=[jax.ShapeDtypeStruct((bsz, seq, S5_WIDTH), BF16), st_shape, st_shape],
        input_output_aliases=aliases,
        scratch_shapes=[pltpu.VMEM((2 * S5_SCHUNKS, nbatch * tc, LANES), F32),
                        pltpu.VMEM((2 * S5_SCHUNKS, nbatch, LANES), F32)],
        compiler_params=_cparams("parallel", "arbitrary"),
        name="s5_prompt",
    )(*prev, proj, *[prm[k] for k in names])


def _s5_sample_kernel(u_ref, wb_ref, wc_ref, lr_ref, li_ref, d_ref, wglu_ref, bglu_ref, h0r_ref, h0i_ref,
                      o_ref, sr_ref, si_ref, xr, xi, *, nseq, t_new):
    nt, st = nseq // SAMPLE_TILE, SAMPLE_TILE
    n = nseq * t_new
    u = u_ref[...]
    ub = u.astype(BF16)
    for cc in range(S5_UCHUNKS):
        res = jnp.dot(ub[:, cc * LANES:(cc + 1) * LANES], wb_ref[cc], preferred_element_type=F32)
        sc = slice(cc * S5_SUB, (cc + 1) * S5_SUB)
        xr[:, :, :, sc] = res[:, :S5_SUB].reshape(nt, t_new, st, S5_SUB)
        xi[:, :, :, sc] = res[:, S5_SUB:].reshape(nt, t_new, st, S5_SUB)
    lr, li = lr_ref[...], li_ref[...]
    hr, hi = h0r_ref[...], h0i_ref[...]
    for t in range(t_new):
        nr = lr * hr - li * hi + xr[:, t].reshape(nseq, S5_FLAT)
        ni = lr * hi + li * hr + xi[:, t].reshape(nseq, S5_FLAT)
        xr[:, t] = nr.reshape(nt, st, S5_FLAT)
        xi[:, t] = ni.reshape(nt, st, S5_FLAT)
        hr, hi = nr, ni
    sr_ref[...] = hr
    si_ref[...] = hi
    ys = []
    for cc in range(S5_UCHUNKS):
        sc = slice(cc * S5_SUB, (cc + 1) * S5_SUB)
        s = jnp.concatenate([xr[:, :, :, sc].reshape(n, S5_SUB), xi[:, :, :, sc].reshape(n, S5_SUB)],
                            axis=1).astype(BF16)
        cols = slice(cc * LANES, (cc + 1) * LANES)
        ys.append(jnp.dot(s, wc_ref[cc], preferred_element_type=F32) + d_ref[:, cols] * u[:, cols])
    o_ref[...] = _s5_tail(jnp.concatenate(ys, axis=1), wglu_ref, bglu_ref).astype(BF16)


def _s5_sample(proj, h0r, h0i, prm, t_new, layer, prev):
    n = proj.shape[0]
    nseq = n // t_new
    names = ("wb", "wc", "lr", "li", "d", "wglu", "bglu")
    st = pl.BlockSpec((None, nseq, S5_FLAT), lambda i: (layer, 0, 0))
    prev, prev_specs, aliases = _alias_inputs(prev, 1)
    scratch = pltpu.VMEM((nseq // SAMPLE_TILE, t_new, SAMPLE_TILE, S5_FLAT), F32)
    return pl.pallas_call(
        _skip_refs(functools.partial(_s5_sample_kernel, nseq=nseq, t_new=t_new), len(prev)),
        grid=(1,),
        in_specs=prev_specs + [pl.BlockSpec((n, S5_WIDTH), lambda i: (0, EVEN_U_BLOCK))]
        + [_layer_spec(prm[k].shape[1:], layer) for k in names]
        + [_layer_spec((nseq, S5_FLAT), layer), _layer_spec((nseq, S5_FLAT), layer)],
        out_specs=[pl.BlockSpec((n, S5_WIDTH), lambda i: (0, 0)), st, st],
        out_shape=[jax.ShapeDtypeStruct((n, S5_WIDTH), BF16),
                   jax.ShapeDtypeStruct(h0r.shape, F32), jax.ShapeDtypeStruct(h0i.shape, F32)],
        input_output_aliases=aliases,
        scratch_shapes=[scratch, scratch],
        compiler_params=_cparams("arbitrary"),
        name="s5_sample",
    )(*prev, proj, *[prm[k] for k in names], h0r, h0i)


def _s5_params(a_re, a_im, log_dt, b_re, b_im, c_re, c_im, d_skip, w_glu, b_glu):
    nl = a_re.shape[0]
    dt = jnp.exp(log_dt)
    mag = jnp.exp(a_re * dt)
    lr, li = mag * jnp.cos(a_im * dt), mag * jnp.sin(a_im * dt)
    den = a_re * a_re + a_im * a_im
    cr = ((lr - 1.0) * a_re + li * a_im) / den
    ci = (li * a_re - (lr - 1.0) * a_im) / den
    bbr = cr[..., None] * b_re - ci[..., None] * b_im
    bbi = cr[..., None] * b_im + ci[..., None] * b_re
    gpc = LANES // S5_GROUP
    eye = jnp.eye(gpc, dtype=F32)

    def in_blocks(bb):
        bb = bb.reshape(nl, S5_UCHUNKS, gpc, S5_STATE, S5_GROUP)
        return jnp.einsum("lcgph,gk->lcghkp", bb, eye).reshape(nl, S5_UCHUNKS, LANES, S5_SUB)

    def out_blocks(cm):
        cm = cm.reshape(nl, S5_UCHUNKS, gpc, S5_GROUP, S5_STATE)
        return jnp.einsum("lcghp,gk->lcgpkh", cm, eye).reshape(nl, S5_UCHUNKS, S5_SUB, LANES)

    wb = jnp.concatenate([in_blocks(bbr), in_blocks(bbi)], axis=3).astype(BF16)
    wc = jnp.concatenate([out_blocks(c_re), -out_blocks(c_im)], axis=2).astype(BF16)
    lr_f, li_f = lr.reshape(nl, 1, S5_FLAT), li.reshape(nl, 1, S5_FLAT)
    lam = jnp.concatenate([lr_f.reshape(nl, S5_SCHUNKS, 1, LANES), li_f.reshape(nl, S5_SCHUNKS, 1, LANES)], axis=1)
    lam8 = jnp.broadcast_to(lam, (nl, 2 * S5_SCHUNKS, SUBLANES, LANES))
    return dict(wb=wb, wc=wc, lam8=lam8, lr=lr_f, li=li_f, d=d_skip.reshape(nl, 1, S5_WIDTH),
                wglu=w_glu.astype(BF16), bglu=b_glu.reshape(nl, 1, S5_WIDTH))


ODD_K_BLOCK = 1
ODD_V_BLOCK = (2 * ML_QK) // ML_WIDTH
ODD_O_BLOCK = ODD_V_BLOCK + 1
ODD_G_BLOCK = (2 * ML_QK + 2 * ML_WIDTH) // LANES
ML_AUG = 2 * ML_DV


def _head_out(h, o, gout):
    hn = h * lax.rsqrt(jnp.mean(h * h, axis=-1, keepdims=True) + EPS) * gout
    return (hn * _sigmoid(o)).astype(BF16)


ODDP_V_BLOCK = 0
ODDP_O_BLOCK = 1
ODDP_Q_BLOCK = (2 * ML_WIDTH) // ML_QK
ODDP_G_BLOCK = (2 * ML_WIDTH + ML_QK) // LANES
ML_SPLIT = 3
ML_PIECE_LANES = 2 * ML_HEADS
ML_SEQS = 4


def _norm_matmul_kt_kernel(x_ref, g_ref, w_ref, wkt_ref, o_ref, kt_ref):
    h = _rms(x_ref[...], g_ref[...]).astype(BF16)
    o_ref[...] = jnp.dot(h, w_ref[...], preferred_element_type=F32)
    kt = lax.dot_general(wkt_ref[...], h, (((1,), (1,)), ((), ())), preferred_element_type=F32)
    kt_ref[...] = kt * (ML_DK ** -0.5)


def _norm_matmul_kt(x, g, layer, w, wkt, widx):
    n, d = x.shape
    m = w.shape[2]
    mk = wkt.shape[1]
    tm = min(ROW_TILE, n)
    return pl.pallas_call(
        _norm_matmul_kt_kernel,
        grid=(n // tm,),
        in_specs=[pl.BlockSpec((tm, d), lambda i: (i, 0)), _layer_spec((1, d), layer), _layer_spec((d, m), widx),
                  _layer_spec((mk, d), widx)],
        out_specs=[pl.BlockSpec((tm, m), lambda i: (i, 0)), pl.BlockSpec((mk, tm), lambda i: (0, i))],
        out_shape=[jax.ShapeDtypeStruct((n, m), F32), jax.ShapeDtypeStruct((mk, n), F32)],
        compiler_params=_cparams("parallel"),
        name="norm_matmul_kt",
    )(x, g, w, wkt)


def _cummax_rows(x):
    n = x.shape[0]
    row = lax.broadcasted_iota(jnp.int32, x.shape, 0)
    shift = 1
    while shift < n:
        x = jnp.maximum(x, jnp.where(row >= shift, pltpu.roll(x, shift, 0), NEG_INF))
        shift *= 2
    return x


def _pieces(x):
    lane = lax.broadcasted_iota(jnp.int32, x.shape, 1)
    xx = x + pltpu.roll(x, ML_PIECE_LANES, 1) + pltpu.roll(x, 2 * ML_PIECE_LANES, 1)
    a1, a2, a3 = _split3(xx)
    return jnp.where(lane < ML_PIECE_LANES, a1, jnp.where(lane < 2 * ML_PIECE_LANES, a2, a3))


def _ml_select_constants():
    mask = np.zeros((ML_HEADS, LANES), np.float32)
    sel = np.zeros((ML_HEADS, LANES, 2 * ML_DV), np.float32)
    for h in range(ML_HEADS):
        for k in range(ML_SPLIT):
            lo, hi = k * ML_PIECE_LANES + h, k * ML_PIECE_LANES + ML_HEADS + h
            mask[h, lo] = mask[h, hi] = 1.0
            sel[h, lo, :ML_DV] = 1.0
            sel[h, hi, ML_DV:] = 1.0
    return jnp.asarray(mask), jnp.asarray(sel, dtype=BF16)


def _mlstm_prompt_kernel(*refs, tc, nchunks, nseq):
    v_ref, o_ref, q_ref, g_ref = refs[:4]
    kt_refs = refs[4:4 + nseq]
    bias_ref, gout_ref, mask_ref, sel_ref, h_ref, c_ref, n_ref, m_ref, caug, mst = refs[4 + nseq:]
    ci = pl.program_id(1)

    @pl.when(ci == 0)
    def _():
        caug[...] = jnp.zeros_like(caug)
        mst[...] = jnp.zeros_like(mst)

    nh = ML_HEADS
    lane = lax.broadcasted_iota(jnp.int32, (tc, LANES), 1)
    lo, hi = lane < nh, (lane >= nh) & (lane < 2 * nh)
    rt = lax.broadcasted_iota(jnp.int32, (tc, tc), 0)
    cs = lax.broadcasted_iota(jnp.int32, (tc, tc), 1)
    causal = cs <= rt
    tril = jnp.where(causal, 1.0, 0.0).astype(BF16)
    ones = jnp.ones((tc, ML_DV), F32)

    def gates(sq):
        g = g_ref[sq] + bias_ref[...]
        lf = jnp.where(hi, _log_sigmoid(g), 0.0)
        b = sum(jnp.dot(tril, p, preferred_element_type=F32) for p in _split3(lf))
        c = jnp.where(hi, pltpu.roll(g, nh, 1) - b, 0.0)
        m_prev = mst[sq]
        mx = jnp.maximum(_cummax_rows(c), m_prev)
        m_row = b + mx
        mx_lo = pltpu.roll(mx, LANES - nh, 1)
        w_inter = jnp.exp(pltpu.roll(m_prev, LANES - nh, 1) - mx_lo)
        mst[sq] = m_row[tc - 1:tc, :]
        return dict(xc=_pieces(jnp.where(lo, w_inter, jnp.where(hi, jnp.exp(-m_row), 0.0))),
                    lc=_pieces(jnp.where(lo, -mx_lo, jnp.where(hi, 1.0, 0.0))),
                    rc=_pieces(jnp.where(lo, 1.0, jnp.where(hi, c, 0.0))))

    gt = [gates(sq) for sq in range(nseq)]
    units = [(sq, hd) for hd in range(nh) for sq in range(nseq)]
    st = [dict() for _ in units]

    def stage1(u):
        sq, hd = units[u]
        d = st[u]
        rh = gt[sq]["rc"] * mask_ref[hd:hd + 1, :].astype(BF16)
        d["dmat"] = lax.dot_general(gt[sq]["lc"], rh, (((1,), (1,)), ((), ())), preferred_element_type=F32)
        d["wb"] = jnp.dot(gt[sq]["xc"], sel_ref[hd], preferred_element_type=F32)
        d["qh"] = q_ref[sq, :, hd * ML_DK:(hd + 1) * ML_DK]
        d["kt"] = kt_refs[sq][hd * ML_DK:(hd + 1) * ML_DK, :]
        d["qk"] = jnp.dot(d["qh"].astype(BF16), d["kt"].astype(BF16), preferred_element_type=F32)

    def stage2(u):
        sq, hd = units[u]
        d = st[u]
        cols = slice(hd * ML_DV, (hd + 1) * ML_DV)
        d["w"] = jnp.exp(jnp.where(causal, d["dmat"], NEG_INF))
        d["vaug"] = jnp.concatenate([v_ref[sq, :, cols], ones], axis=1).astype(BF16)
        d["cm"] = caug[sq, hd]
        lhs = jnp.concatenate([(d["qk"] * d["w"]).astype(BF16), (d["wb"][:, :ML_DK] * d["qh"]).astype(BF16)], axis=1)
        rhs = jnp.concatenate([d["vaug"], d["cm"].astype(BF16)], axis=0)
        d["both"] = jnp.dot(lhs, rhs, preferred_element_type=F32)
        kw = (d["kt"] * d["w"][tc - 1:tc, :]).astype(BF16)
        d["upd"] = jnp.dot(kw, d["vaug"], preferred_element_type=F32)

    def stage3(u):
        sq, hd = units[u]
        d = st[u]
        cols = slice(hd * ML_DV, (hd + 1) * ML_DV)
        both, wb = d["both"], d["wb"]
        h = both[:, :ML_DV] / jnp.maximum(jnp.abs(both[:, ML_DV:]), wb[:, ML_DV:])
        h_ref[sq, :, cols] = _head_out(h, o_ref[sq, :, cols], gout_ref[:, cols])
        decay = wb[tc - 1:tc, :ML_DV]
        caug[sq, hd] = jnp.concatenate([decay, decay], axis=1) * d["cm"] + d["upd"]
        d.clear()

    for step in range(len(units) + 2):
        if step < len(units):
            stage1(step)
        if 0 <= step - 1 < len(units):
            stage2(step - 1)
        if 0 <= step - 2 < len(units):
            stage3(step - 2)

    @pl.when(ci == nchunks - 1)
    def _():
        c_ref[...] = caug[:, :, :, :ML_DV]
        n_ref[...] = caug[:, :, :, ML_DV:]
        m_ref[...] = mst[...]


def _kt_index(b, c, *, sq, nseq, nchunks):
    return 0, (b * nseq + sq) * nchunks + c


def _mlstm_prompt(proj, kt, bias, gout, consts, layer, prev):
    bsz, seq, _ = proj.shape
    tc, nseq = ML_CHUNK, ML_SEQS
    nchunks = seq // tc
    n_layers = bias.shape[0]
    mask, sel = consts
    blk = lambda w, j: pl.BlockSpec((nseq, tc, w), lambda b, c: (b, c, j))
    st = lambda shape: pl.BlockSpec((None, nseq) + shape, lambda b, c: (layer, b) + (0,) * len(shape))
    st_shape = lambda shape: jax.ShapeDtypeStruct((n_layers, bsz) + shape, F32)
    prev, prev_specs, aliases = _alias_inputs(prev, 1)
    return pl.pallas_call(
        _skip_refs(functools.partial(_mlstm_prompt_kernel, tc=tc, nchunks=nchunks, nseq=nseq), len(prev)),
        grid=(bsz // nseq, nchunks),
        in_specs=prev_specs + [
            blk(ML_WIDTH, ODDP_V_BLOCK), blk(ML_WIDTH, ODDP_O_BLOCK), blk(ML_QK, ODDP_Q_BLOCK),
            blk(LANES, ODDP_G_BLOCK)]
        + [pl.BlockSpec((ML_QK, tc), functools.partial(_kt_index, sq=sq, nseq=nseq, nchunks=nchunks))
           for sq in range(nseq)] + [
            _layer_spec((1, LANES), layer), _layer_spec((1, ML_WIDTH), layer),
            _const_spec(mask.shape), _const_spec(sel.shape)],
        out_specs=[blk(ML_WIDTH, 0), st((ML_HEADS, ML_DK, ML_DV)), st((ML_HEADS, ML_DK, ML_DV)), st((1, LANES))],
        out_shape=[jax.ShapeDtypeStruct((bsz, seq, ML_WIDTH), BF16),
                   st_shape((ML_HEADS, ML_DK, ML_DV)), st_shape((ML_HEADS, ML_DK, ML_DV)), st_shape((1, LANES))],
        input_output_aliases=aliases,
        scratch_shapes=[pltpu.VMEM((nseq, ML_HEADS, ML_DK, ML_AUG), F32), pltpu.VMEM((nseq, 1, LANES), F32)],
        compiler_params=_cparams("parallel", "arbitrary"),
        name="mlstm_prompt",
    )(*prev, proj, proj, proj, proj, *([kt] * nseq), bias, gout, mask, sel)


SAMPLE_TILE = SUBLANES
MLS_SEQS = 32


def _mlstm_sample_kernel(v_ref, o_ref, q_ref, g_ref, kt_ref, bias_ref, gout_ref, mask_ref, sel_ref,
                         c0_ref, n0_ref, m0_ref, h_ref, c_ref, n_ref, m_ref, *, nseq, t_new):
    nh, nt, st = ML_HEADS, nseq // SAMPLE_TILE, SAMPLE_TILE
    rows = nseq * t_new
    lane = lax.broadcasted_iota(jnp.int32, (rows, LANES), 1)
    lo, hi = lane < nh, (lane >= nh) & (lane < 2 * nh)
    tiles = lambda a: a.reshape(nt, t_new, st, a.shape[-1])
    flat = lambda a: a.reshape(rows, a.shape[-1])
    per_seq = lambda a: a.reshape(nseq, a.shape[-1])

    g = g_ref[...] + bias_ref[...]
    lf = tiles(jnp.where(hi, _log_sigmoid(g), 0.0))
    ig = tiles(jnp.where(hi, pltpu.roll(g, nh, 1), 0.0))
    m_prev = m0_ref[...].reshape(nt, st, LANES)
    bs, cs, ms = [], [], []
    b_run, m_run = None, m_prev
    for t in range(t_new):
        b_run = lf[:, t] if b_run is None else b_run + lf[:, t]
        c_t = ig[:, t] - b_run
        m_run = jnp.maximum(m_run, c_t)
        bs.append(b_run)
        cs.append(c_t)
        ms.append(m_run)
    stack = lambda xs: flat(jnp.stack(xs, axis=1))
    b, c, mx = stack(bs), stack(cs), stack(ms)
    m_prev_rows = stack([m_prev] * t_new)
    m_row = b + mx
    m_ref[...] = per_seq(bs[-1] + ms[-1])
    mx_lo = pltpu.roll(mx, LANES - nh, 1)
    w_inter = jnp.exp(pltpu.roll(m_prev_rows, LANES - nh, 1) - mx_lo)
    xc = _pieces(jnp.where(lo, w_inter, jnp.where(hi, jnp.exp(-m_row), 0.0)))
    lc = _pieces(jnp.where(lo, -mx_lo, jnp.where(hi, 1.0, 0.0)))
    rc = _pieces(jnp.where(lo, 1.0, jnp.where(hi, c, 0.0)))

    def seq_of(idx):
        return (idx // (t_new * st)) * st + idx % st, (idx % (t_new * st)) // st

    rt = lax.broadcasted_iota(jnp.int32, (rows, rows), 0)
    ct = lax.broadcasted_iota(jnp.int32, (rows, rows), 1)
    (rs, rtok), (cseq, ctok) = seq_of(rt), seq_of(ct)
    valid = (rs == cseq) & (ctok <= rtok)
    rq = lax.broadcasted_iota(jnp.int32, (rows, nseq * ML_DK), 0)
    cq = lax.broadcasted_iota(jnp.int32, (rows, nseq * ML_DK), 1)
    own_q = seq_of(rq)[0] == cq // ML_DK
    rk = lax.broadcasted_iota(jnp.int32, (nseq * ML_DK, rows), 0)
    ck = lax.broadcasted_iota(jnp.int32, (nseq * ML_DK, rows), 1)
    own_k = rk // ML_DK == seq_of(ck)[0]
    ones = jnp.ones((rows, ML_DV), F32)
    last = lambda a: per_seq(tiles(a)[:, t_new - 1])

    for hd in range(nh):
        cols = slice(hd * ML_DV, (hd + 1) * ML_DV)
        rh = rc * mask_ref[hd:hd + 1, :].astype(BF16)
        dmat = lax.dot_general(lc, rh, (((1,), (1,)), ((), ())), preferred_element_type=F32)
        wb = jnp.dot(xc, sel_ref[hd], preferred_element_type=F32)
        qh = q_ref[:, hd * ML_DK:(hd + 1) * ML_DK]
        kt = kt_ref[hd * ML_DK:(hd + 1) * ML_DK, :]
        ktb = kt.astype(BF16)
        w = jnp.exp(jnp.where(valid, dmat, NEG_INF))
        qk = jnp.dot(qh.astype(BF16), ktb, preferred_element_type=F32) * w
        vaug = jnp.concatenate([v_ref[:, cols], ones], axis=1).astype(BF16)
        po = jnp.dot(qk.astype(BF16), vaug, preferred_element_type=F32)
        wq = wb[:, :ML_DK] * qh
        wq2 = jnp.concatenate([wq, wq], axis=1)
        wq_bd = jnp.where(own_q, jnp.concatenate([wq2] * (nseq * ML_DK // LANES), axis=1), 0.0).astype(BF16)
        cstack = c0_ref[:, hd].reshape(nseq * ML_DK, ML_DV)
        num = po[:, :ML_DV] + jnp.dot(wq_bd, cstack.astype(BF16), preferred_element_type=F32)
        n0 = n0_ref[hd]
        n_rows = stack([n0.reshape(nt, st, ML_DK)] * t_new)
        den = po[:, ML_DV:] + jnp.sum(wq * n_rows, axis=-1, keepdims=True)
        h = num / jnp.maximum(jnp.abs(den), wb[:, ML_DV:])
        h_ref[:, cols] = _head_out(h, o_ref[:, cols], gout_ref[:, cols])
        w_last = last(w)
        decay = last(wb[:, :ML_DV])
        n_upd = lax.dot_general(w_last.astype(BF16), ktb, (((1,), (1,)), ((), ())), preferred_element_type=F32)
        n_ref[hd] = decay[:, :ML_DK] * n0 + n_upd
        wk = jnp.sum(w_last, axis=0, keepdims=True)
        kw_bd = jnp.where(own_k, jnp.concatenate([kt * wk] * nseq, axis=0), 0.0).astype(BF16)
        upd = jnp.dot(kw_bd, v_ref[:, cols].astype(BF16), preferred_element_type=F32)
        decay_rows = jnp.broadcast_to(decay[:, None, :], (nseq, ML_DK, ML_DV)).reshape(nseq * ML_DK, ML_DV)
        c_ref[:, hd] = (decay_rows * cstack + upd).reshape(nseq, ML_DK, ML_DV)


def _mlstm_sample(proj, kt, bias, gout, consts, c0, n0h, m0, t_new, layer, prev):
    n = proj.shape[0]
    bsz = n // t_new
    nseq = MLS_SEQS
    rows = nseq * t_new
    mask, sel = consts
    blk = lambda w, j: pl.BlockSpec((rows, w), lambda i: (i, j))
    cst = pl.BlockSpec((None, nseq, ML_HEADS, ML_DK, ML_DV), lambda i: (layer, i, 0, 0, 0))
    nst = pl.BlockSpec((None, ML_HEADS, nseq, ML_DK), lambda i: (layer, 0, i, 0))
    mst = pl.BlockSpec((None, nseq, LANES), lambda i: (layer, i, 0))
    prev, prev_specs, aliases = _alias_inputs(prev, 1)
    return pl.pallas_call(
        _skip_refs(functools.partial(_mlstm_sample_kernel, nseq=nseq, t_new=t_new), len(prev)),
        grid=(bsz // nseq,),
        in_specs=prev_specs + [
            blk(ML_WIDTH, ODDP_V_BLOCK), blk(ML_WIDTH, ODDP_O_BLOCK), blk(ML_QK, ODDP_Q_BLOCK),
            blk(LANES, ODDP_G_BLOCK), pl.BlockSpec((ML_QK, rows), lambda i: (0, i)),
            _layer_spec((1, LANES), layer), _layer_spec((1, ML_WIDTH), layer),
            _const_spec(mask.shape), _const_spec(sel.shape), cst, nst, mst],
        out_specs=[blk(ML_WIDTH, 0), cst, nst, mst],
        out_shape=[jax.ShapeDtypeStruct((n, ML_WIDTH), BF16),
                   jax.ShapeDtypeStruct(c0.shape, F32), jax.ShapeDtypeStruct(n0h.shape, F32),
                   jax.ShapeDtypeStruct(m0.shape, F32)],
        input_output_aliases=aliases,
        compiler_params=_cparams("parallel"),
        name="mlstm_sample",
    )(*prev, proj, proj, proj, proj, kt, bias, gout, mask, sel, c0, n0h, m0)


def _pad_lanes(x):
    return jnp.pad(x, [(0, 0)] * (x.ndim - 1) + [(0, LANES - x.shape[-1])])


def kernel(x_prompt, x_sample, cache_k, cache_v, state_ssm_re, state_ssm_im, state_mlstm_c, state_mlstm_n, state_mlstm_m, norm_mix, norm_ffn, w_in_even, q_norm, k_norm, attn_sinks, s5_a_re, s5_a_im, s5_log_dt, s5_b_re, s5_b_im, s5_c_re, s5_c_im, s5_d, s5_w_glu, s5_b_glu, w_out_even, w_in_odd, ml_b_i, ml_b_f, ml_out_norm, w_out_odd, w_gate, w_up, w_down):
    bp, lp, _ = x_prompt.shape
    bsm, ls, _ = x_sample.shape
    yp = x_prompt.reshape(bp * lp, D_MODEL)
    ys = x_sample.reshape(bsm // SAMPLE_TILE, SAMPLE_TILE, ls, D_MODEL).transpose(0, 2, 1, 3).reshape(bsm * ls, D_MODEL)
    tab_p = _rope_tables(jnp.arange(lp))
    tab_s = tuple(jnp.repeat(t, SAMPLE_TILE, axis=0) for t in _rope_tables(PAST_LEN + jnp.arange(ls)))
    n_even, n_odd = w_in_even.shape[0], w_in_odd.shape[0]

    g_mix = norm_mix.reshape(DEPTH, 1, D_MODEL)
    g_ffn = norm_ffn.reshape(DEPTH, 1, D_MODEL)
    wg, wu, wd = w_gate.astype(BF16), w_up.astype(BF16), w_down.astype(BF16)
    kv0, u0 = ATTN_WIDTH, ATTN_WIDTH + 2 * KV_WIDTH
    order = jnp.asarray(ATTN_HEAD_ORDER)
    wq = w_in_even[..., :kv0].reshape(n_even, D_MODEL, ATTN_HEADS, HEAD_DIM)[:, :, order].reshape(n_even, D_MODEL, kv0)
    w_in_e = jnp.concatenate([wq, w_in_even[..., u0:], w_in_even[..., kv0:u0]], axis=-1).astype(BF16)
    wo_attn = w_out_even[:, :kv0].reshape(n_even, ATTN_HEADS, HEAD_DIM, D_MODEL)[:, order].reshape(n_even, kv0, D_MODEL)
    w_out_e = jnp.concatenate([wo_attn, w_out_even[:, kv0:]], axis=1).astype(BF16)
    gq = jnp.tile(q_norm, (1, LANES // HEAD_DIM)).reshape(n_even, 1, LANES)
    gk = jnp.tile(k_norm, (1, LANES // HEAD_DIM)).reshape(n_even, 1, LANES)
    prm = _s5_params(s5_a_re, s5_a_im, s5_log_dt, s5_b_re, s5_b_im, s5_c_re, s5_c_im, s5_d, s5_w_glu, s5_b_glu)
    k0, v0, g0 = ML_QK, 2 * ML_QK, 2 * ML_QK + 2 * ML_WIDTH
    w_in_o = jnp.concatenate([w_in_odd[..., v0:g0], w_in_odd[..., :k0], _pad_lanes(w_in_odd[..., g0:])],
                             axis=-1).astype(BF16)
    w_kt = jnp.swapaxes(w_in_odd[..., k0:v0], 1, 2).astype(BF16)
    ml_consts = _ml_select_constants()
    w_out_o = w_out_odd.astype(BF16)
    ml_bias = _pad_lanes(jnp.concatenate([ml_b_i, ml_b_f], axis=-1)).reshape(n_odd, 1, LANES)
    ml_gout = ml_out_norm.reshape(n_odd, 1, ML_WIDTH)
    ck = cache_k.reshape(n_even, bsm, WINDOW, KV_WIDTH)
    cv = cache_v.reshape(n_even, bsm, WINDOW, KV_WIDTH)
    h0r = state_ssm_re.reshape(n_even, bsm, S5_FLAT)
    h0i = state_ssm_im.reshape(n_even, bsm, S5_FLAT)
    n0h = jnp.swapaxes(state_mlstm_n, 1, 2)
    m0 = jnp.pad(state_mlstm_m, ((0, 0), (0, 0), (ML_HEADS, LANES - 2 * ML_HEADS)))

    p_attn = p_ssm = p_ml = s_attn = s_ssm = s_ml = None
    for layer in range(DEPTH):
        ffn = (layer, g_ffn, wg, wu, wd)
        if layer % 2 == 0:
            e = layer // 2
            proj = _norm_matmul(yp, g_mix, layer, w_in_e, e)
            proj3 = proj.reshape(bp, lp, -1)
            attn, *p_attn = _attn_prompt(proj3, tab_p, gq, gk, attn_sinks, e, p_attn)
            ssm, *p_ssm = _s5_prompt(proj3, prm, e, p_ssm)
            yp = _mix_ffn(yp, [attn.reshape(bp * lp, -1), ssm.reshape(bp * lp, -1)], w_out_e, e, *ffn)
            proj = _norm_matmul(ys, g_mix, layer, w_in_e, e)
            attn, *s_attn = _attn_sample(proj, ck, cv, tab_s, gq, gk, attn_sinks, ls, e, s_attn)
            ssm, *s_ssm = _s5_sample(proj, h0r, h0i, prm, ls, e, s_ssm)
            ys = _mix_ffn(ys, [attn, ssm], w_out_e, e, *ffn)
        else:
            o = layer // 2
            proj, kt = _norm_matmul_kt(yp, g_mix, layer, w_in_o, w_kt, o)
            hh, *p_ml = _mlstm_prompt(proj.reshape(bp, lp, -1), kt, ml_bias, ml_gout, ml_consts, o, p_ml)
            yp = _mix_ffn(yp, [hh.reshape(bp * lp, -1)], w_out_o, o, *ffn)
            proj, kt = _norm_matmul_kt(ys, g_mix, layer, w_in_o, w_kt, o)
            hh, *s_ml = _mlstm_sample(proj, kt, ml_bias, ml_gout, ml_consts, state_mlstm_c, n0h, m0, ls, o, s_ml)
            ys = _mix_ffn(ys, [hh], w_out_o, o, *ffn)
    heads = lambda a: a.reshape(a.shape[:3] + (KV_HEADS, HEAD_DIM))
    groups = lambda a: a.reshape(a.shape[:2] + (S5_GROUPS, S5_STATE))
    ys = ys.reshape(bsm // SAMPLE_TILE, ls, SAMPLE_TILE, D_MODEL).transpose(0, 2, 1, 3).reshape(bsm, ls, D_MODEL)
    return (yp.reshape(bp, lp, D_MODEL), ys,
            heads(p_attn[0]), heads(p_attn[1]), groups(p_ssm[0]), groups(p_ssm[1]),
            p_ml[0], p_ml[1][..., 0], p_ml[2][:, :, 0, ML_HEADS:2 * ML_HEADS],
            heads(s_attn[0]), heads(s_attn[1]), groups(s_ssm[0]), groups(s_ssm[1]),
            s_ml[0], jnp.swapaxes(s_ml[1], 1, 2), s_ml[2][..., ML_HEADS:2 * ML_HEADS])
```

```python
import functools

import numpy as np

import jax
import jax.numpy as jnp
from jax import lax
from jax.experimental import pallas as pl
from jax.experimental.pallas import tpu as pltpu

F32 = jnp.float32
BF16 = jnp.bfloat16

D_MODEL = 1024
DEPTH = 4
PAST_LEN = 8192
WINDOW = 128
ATTN_HEADS = 8
KV_HEADS = 2
HEAD_DIM = 64
ATTN_WIDTH = ATTN_HEADS * HEAD_DIM
KV_WIDTH = KV_HEADS * HEAD_DIM
ROT_DIM = HEAD_DIM // 4
ROPE_THETA = 500000.0
S5_GROUP = 16
S5_WIDTH = D_MODEL // 2
S5_GROUPS = S5_WIDTH // S5_GROUP
S5_STATE = 64
S5_FLAT = S5_GROUPS * S5_STATE
ML_HEADS = 8
ML_DV = D_MODEL // ML_HEADS
ML_DK = ML_DV // 2
ML_QK = ML_HEADS * ML_DK
ML_WIDTH = ML_HEADS * ML_DV
ODD_IN = 2 * ML_QK + 2 * ML_WIDTH + 2 * ML_HEADS
ODD_IN_PAD = 2 * ML_QK + 2 * ML_WIDTH + 128
D_FF = 2816
EPS = 1e-6

LANES = 128
SUBLANES = 8
ROW_TILE = 512
FF_TILE = 256
S5_CHUNK = 64
ML_CHUNK = 128
SAMPLE_BLOCK = 8
VMEM_LIMIT = 56 * 1024 * 1024

NEG_INF = float("-inf")


def _cparams(*sem):
    return pltpu.CompilerParams(dimension_semantics=sem, vmem_limit_bytes=VMEM_LIMIT)


def _const_spec(shape):
    zeros = (0,) * len(shape)
    return pl.BlockSpec(shape, lambda *_: zeros, pipeline_mode=pl.Buffered(1))


def _layer_spec(shape, layer):
    zeros = (0,) * len(shape)
    return pl.BlockSpec((None,) + tuple(shape), lambda *_: (layer,) + zeros, pipeline_mode=pl.Buffered(1))


def _skip_refs(body, n_skip):
    if n_skip == 0:
        return body

    def wrapped(*refs):
        return body(*refs[n_skip:])

    return wrapped


def _alias_inputs(prev, first_state_out):
    prev = () if prev is None else tuple(prev)
    specs = [pl.BlockSpec(memory_space=pl.ANY) for _ in prev]
    aliases = {i: first_state_out + i for i in range(len(prev))}
    return prev, specs, aliases


def _rms(x, g):
    ms = jnp.mean(x * x, axis=-1, keepdims=True)
    return x * lax.rsqrt(ms + EPS) * g


def _split3(a):
    a1 = a.astype(BF16)
    r1 = a - a1.astype(F32)
    a2 = r1.astype(BF16)
    a3 = (r1 - a2.astype(F32)).astype(BF16)
    return a1, a2, a3


def _log_sigmoid(x):
    return jnp.minimum(x, 0.0) - jnp.log(1.0 + jnp.exp(-jnp.abs(x)))


def _sigmoid(x):
    return 1.0 / (1.0 + jnp.exp(-x))


def _norm_matmul_kernel(x_ref, g_ref, w_ref, o_ref):
    h = _rms(x_ref[...], g_ref[...]).astype(BF16)
    o_ref[...] = jnp.dot(h, w_ref[...], preferred_element_type=F32)


def _row_groups_call(body, groups, consts, const_specs, out_defs, scratch_shapes, name):
    steps, tiles = [], []
    for arrays in groups:
        n = arrays[0].shape[0]
        tm = min(ROW_TILE, n)
        tiles.append(tm)
        steps.append(n // tm)
    offs = [sum(steps[:k]) for k in range(len(groups))]

    def local(k):
        return lambda i: jnp.clip(i - offs[k], 0, steps[k] - 1)

    in_specs, out_specs, out_shape, args = [], [], [], []
    for k, arrays in enumerate(groups):
        for a in arrays:
            in_specs.append(pl.BlockSpec((tiles[k], a.shape[1]), lambda i, f=local(k): (f(i), 0)))
            args.append(a)
    for k, arrays in enumerate(groups):
        n = arrays[0].shape[0]
        for width, dtype, by_rows in out_defs:
            if by_rows:
                out_specs.append(pl.BlockSpec((tiles[k], width), lambda i, f=local(k): (f(i), 0)))
                out_shape.append(jax.ShapeDtypeStruct((n, width), dtype))
            else:
                out_specs.append(pl.BlockSpec((width, tiles[k]), lambda i, f=local(k): (0, f(i))))
                out_shape.append(jax.ShapeDtypeStruct((width, n), dtype))
    n_in = [len(arrays) for arrays in groups]
    n_out = len(out_defs)

    def kern(*refs):
        i = pl.program_id(0)
        pos = 0
        ins = []
        for cnt in n_in:
            ins.append(refs[pos:pos + cnt])
            pos += cnt
        crefs = refs[pos:pos + len(consts)]
        pos += len(consts)
        outs = [refs[pos + k * n_out:pos + (k + 1) * n_out] for k in range(len(groups))]
        scratch = refs[pos + len(groups) * n_out:]
        for k in range(len(groups)):
            @pl.when((i >= offs[k]) & (i < offs[k] + steps[k]))
            def _(k=k):
                body(*ins[k], *crefs, *outs[k], *scratch)

    res = pl.pallas_call(
        kern,
        grid=(sum(steps),),
        in_specs=in_specs + list(const_specs),
        out_specs=out_specs,
        out_shape=out_shape,
        scratch_shapes=scratch_shapes,
        compiler_params=_cparams("arbitrary"),
        name=name,
    )(*args, *consts)
    return [res[k * n_out:(k + 1) * n_out] for k in range(len(groups))]


def _norm_matmul(xs, g, layer, w, widx):
    d, m = w.shape[1], w.shape[2]
    res = _row_groups_call(_norm_matmul_kernel, [[x] for x in xs], [g, w],
                           [_layer_spec((1, d), layer), _layer_spec((d, m), widx)],
                           [(m, F32, True)], [], "norm_matmul")
    return [r[0] for r in res]


def _mix_ffn_kernel(*refs, n_mix):
    x_ref = refs[0]
    a_refs = refs[1:1 + n_mix]
    wo_ref, g_ref, wg_ref, wu_ref, wd_ref, o_ref, act_ref = refs[1 + n_mix:]
    y = x_ref[...]
    off = 0
    for a_ref in a_refs:
        ka = a_ref.shape[1]
        y = y + jnp.dot(a_ref[...], wo_ref[off:off + ka, :], preferred_element_type=F32)
        off += ka
    h = _rms(y, g_ref[...]).astype(BF16)
    for f in range(D_FF // FF_TILE):
        cols = slice(f * FF_TILE, (f + 1) * FF_TILE)
        gate = jnp.dot(h, wg_ref[:, cols], preferred_element_type=F32)
        up = jnp.dot(h, wu_ref[:, cols], preferred_element_type=F32)
        act_ref[:, cols] = (gate * _sigmoid(gate) * up).astype(BF16)
    o_ref[...] = y + jnp.dot(act_ref[...], wd_ref[...], preferred_element_type=F32)


def _mix_ffn(groups, w_out, oidx, layer, g_ffn, wg, wu, wd):
    d = w_out.shape[2]
    n_mix = len(groups[0]) - 1
    tm = min(ROW_TILE, max(g[0].shape[0] for g in groups))
    res = _row_groups_call(
        functools.partial(_mix_ffn_kernel, n_mix=n_mix), groups, [w_out, g_ffn, wg, wu, wd],
        [_layer_spec(w_out.shape[1:], oidx), _layer_spec((1, d), layer), _layer_spec(wg.shape[1:], layer),
         _layer_spec(wu.shape[1:], layer), _layer_spec(wd.shape[1:], layer)],
        [(d, F32, True)], [pltpu.VMEM((tm, D_FF), BF16)], "mix_ffn")
    return [r[0] for r in res]


def _head_ones():
    r = lax.broadcasted_iota(jnp.int32, (LANES, LANES), 0) // HEAD_DIM
    c = lax.broadcasted_iota(jnp.int32, (LANES, LANES), 1) // HEAD_DIM
    return jnp.where(r == c, 1.0, 0.0).astype(BF16)


def _qk_prep(x, g, ones, ct, sa, sb):
    x2 = x * x
    hi = x2.astype(BF16)
    lo = (x2 - hi.astype(F32)).astype(BF16)
    ss = jnp.dot(hi, ones, preferred_element_type=F32) + jnp.dot(lo, ones, preferred_element_type=F32)
    xn = x * lax.rsqrt(ss * (1.0 / HEAD_DIM) + EPS) * g
    return xn * ct + pltpu.roll(xn, LANES - ROT_DIM // 2, 1) * sa + pltpu.roll(xn, ROT_DIM // 2, 1) * sb


def _rope_tables(pos):
    half = ROT_DIM // 2
    inv = jnp.power(jnp.float32(ROPE_THETA), -jnp.arange(half, dtype=F32) / half)
    ang = pos.astype(F32)[:, None] * inv[None, :]
    cos, sin = jnp.cos(ang), jnp.sin(ang)
    n = pos.shape[0]
    one = jnp.ones((n, HEAD_DIM - ROT_DIM), F32)
    zero = jnp.zeros((n, HEAD_DIM - ROT_DIM), F32)
    z8 = jnp.zeros((n, half), F32)
    ct = jnp.concatenate([cos, cos, one], axis=1)
    sa = jnp.concatenate([-sin, z8, zero], axis=1)
    sb = jnp.concatenate([z8, sin, zero], axis=1)
    tile = lambda t: jnp.concatenate([t, t], axis=1)
    return tile(ct), tile(sa), tile(sb)


ATTN_SEQS = 4
ATTN_QCHUNKS = ATTN_WIDTH // LANES
ATTN_HEAD_ORDER = tuple(h * ATTN_QCHUNKS + j for j in range(ATTN_QCHUNKS) for h in range(KV_HEADS))


def _attn_prompt_kernel(q_ref, kv_ref, ct_ref, sa_ref, sb_ref, gq_ref, gk_ref, sink_ref,
                        o_ref, pk_ref, pv_ref, kprev, vprev, *, nb, layer, nseq):
    i = pl.program_id(1)

    @pl.when(i == 0)
    def _():
        kprev[...] = jnp.zeros_like(kprev)
        vprev[...] = jnp.zeros_like(vprev)

    ones = _head_ones()
    ct, sa, sb = ct_ref[...], sa_ref[...], sb_ref[...]
    r = lax.broadcasted_iota(jnp.int32, (WINDOW, 2 * WINDOW), 0)
    c = lax.broadcasted_iota(jnp.int32, (WINDOW, 2 * WINDOW), 1)
    rel = r + WINDOW - c
    mask = (rel >= 0) & (rel <= WINDOW) & ((c >= WINDOW) | (i > 0))
    lane = lax.broadcasted_iota(jnp.int32, (WINDOW, LANES), 1)
    group0 = lane < HEAD_DIM
    v_ones = jnp.ones((2 * WINDOW, LANES), BF16)
    nq = ATTN_QCHUNKS
    st = [dict() for _ in range(nseq)]

    def prep(sq):
        d = st[sq]
        kv = kv_ref[sq]
        d["kn"] = _qk_prep(kv[:, :KV_WIDTH], gk_ref[...], ones, ct, sa, sb)
        d["v"] = kv[:, KV_WIDTH:]
        d["qn"] = [_qk_prep(q_ref[sq, :, j * LANES:(j + 1) * LANES], gq_ref[...], ones, ct, sa, sb)
                   * (HEAD_DIM ** -0.5) for j in range(nq)]
        d["kcat"] = jnp.concatenate([kprev[sq], d["kn"]], axis=0).astype(BF16)
        d["vaug"] = jnp.concatenate([jnp.concatenate([vprev[sq], d["v"]], axis=0).astype(BF16), v_ones], axis=1)
        kprev[sq] = d["kn"]
        vprev[sq] = d["v"]

    def scores(sq, h):
        d = st[sq]
        keep = group0 if h == 0 else jnp.logical_not(group0)
        qs = jnp.concatenate([jnp.where(keep, qj, 0.0) for qj in d["qn"]], axis=0).astype(BF16)
        d["s", h] = lax.dot_general(qs, d["kcat"], (((1,), (1,)), ((), ())), preferred_element_type=F32)

    def softmax_pv(sq, h):
        d = st[sq]
        s = d.pop(("s", h))
        ps, corr = [], []
        for j in range(nq):
            sg = jnp.where(mask, s[j * WINDOW:(j + 1) * WINDOW], NEG_INF)
            sink = sink_ref[layer, h * nq + j]
            m = jnp.maximum(jnp.max(sg, axis=-1, keepdims=True), sink)
            ps.append(jnp.exp(sg - m).astype(BF16))
            corr.append(jnp.exp(sink - m))
        o = jnp.dot(jnp.concatenate(ps, axis=0), d["vaug"], preferred_element_type=F32)
        d["o", h] = [o[j * WINDOW:(j + 1) * WINDOW, :LANES] / (o[j * WINDOW:(j + 1) * WINDOW, LANES:] + corr[j])
                     for j in range(nq)]

    def finish(sq):
        d = st[sq]
        o_ref[sq] = jnp.concatenate([jnp.where(group0, d["o", 0][j], d["o", 1][j]) for j in range(nq)],
                                    axis=1).astype(BF16)

    for sq in range(nseq):
        prep(sq)
    for sq in range(nseq):
        scores(sq, 0)
        scores(sq, 1)
    for sq in range(nseq):
        softmax_pv(sq, 0)
        softmax_pv(sq, 1)
        finish(sq)

    @pl.when(i == nb - 1)
    def _():
        for sq in range(nseq):
            pk_ref[sq] = st[sq]["kn"]
            pv_ref[sq] = st[sq]["v"]


def _attn_prompt(proj, tables, gq, gk, sinks, layer, prev):
    bsz, seq, _ = proj.shape
    nb = seq // WINDOW
    nseq = ATTN_SEQS
    n_layers = gq.shape[0]
    tab = pl.BlockSpec((WINDOW, LANES), lambda b, i: (i, 0))
    prev, prev_specs, aliases = _alias_inputs(prev, 1)
    win = pl.BlockSpec((None, nseq, WINDOW, KV_WIDTH), lambda b, i: (layer, b, 0, 0))
    win_shape = jax.ShapeDtypeStruct((n_layers, bsz, WINDOW, KV_WIDTH), F32)
    return pl.pallas_call(
        _skip_refs(functools.partial(_attn_prompt_kernel, nb=nb, layer=layer, nseq=nseq), len(prev)),
        grid=(bsz // nseq, nb),
        in_specs=prev_specs + [
            pl.BlockSpec((nseq, WINDOW, ATTN_WIDTH), lambda b, i: (b, i, 0)),
            pl.BlockSpec((nseq, WINDOW, 2 * KV_WIDTH), lambda b, i: (b, i, EVEN_KV_BLOCK)),
            tab, tab, tab, _layer_spec((1, LANES), layer), _layer_spec((1, LANES), layer),
            pl.BlockSpec(memory_space=pltpu.SMEM)],
        out_specs=[pl.BlockSpec((nseq, WINDOW, ATTN_WIDTH), lambda b, i: (b, i, 0)), win, win],
        out_shape=[jax.ShapeDtypeStruct((bsz, seq, ATTN_WIDTH), BF16), win_shape, win_shape],
        input_output_aliases=aliases,
        scratch_shapes=[pltpu.VMEM((nseq, WINDOW, KV_WIDTH), F32), pltpu.VMEM((nseq, WINDOW, KV_WIDTH), F32)],
        compiler_params=_cparams("parallel", "arbitrary"),
        name="attn_prompt",
    )(*prev, proj, proj, *tables, gq, gk, sinks)


EVEN_U_BLOCK = ATTN_WIDTH // S5_WIDTH
EVEN_KV_BLOCK = (ATTN_WIDTH + S5_WIDTH) // (2 * KV_WIDTH)
KALL_ROWS = WINDOW + SUBLANES


def _attn_sample_kernel(q_ref, kv_ref, ck_ref, cv_ref, ct_ref, sa_ref, sb_ref, gq_ref, gk_ref, sink_ref,
                        o_ref, nk_ref, nv_ref, o_seq, *, bs, t_new, layer):
    ones = _head_ones()
    ct, sa, sb = ct_ref[...], sa_ref[...], sb_ref[...]
    kv = kv_ref[...]
    kn = _qk_prep(kv[:, :KV_WIDTH], gk_ref[...], ones, ct, sa, sb)
    v = kv[:, KV_WIDTH:]
    nq = ATTN_QCHUNKS
    qn = [_qk_prep(q_ref[:, j * LANES:(j + 1) * LANES], gq_ref[...], ones, ct, sa, sb) * (HEAD_DIM ** -0.5)
          for j in range(nq)]
    rows = nq * t_new
    r = lax.broadcasted_iota(jnp.int32, (rows, KALL_ROWS), 0)
    c = lax.broadcasted_iota(jnp.int32, (rows, KALL_ROWS), 1)
    t = r % t_new
    mask = (c >= t) & (c <= t + WINDOW)
    rj = lax.broadcasted_iota(jnp.int32, (rows, 1), 0) // t_new
    lane = lax.broadcasted_iota(jnp.int32, (t_new, LANES), 1)
    group0 = lane < HEAD_DIM
    pad = jnp.zeros((KALL_ROWS - WINDOW - t_new, KV_WIDTH), F32)
    v_ones = jnp.ones((KALL_ROWS, LANES), BF16)

    def seq_rows(a, b):
        return jnp.concatenate([a[tt * bs + b:tt * bs + b + 1] for tt in range(t_new)], axis=0)

    sinks = []
    for h in range(KV_HEADS):
        sk = jnp.zeros((rows, 1), F32)
        for j in range(nq):
            sk = jnp.where(rj == j, sink_ref[layer, h * nq + j], sk)
        sinks.append(sk)

    st = [dict() for _ in range(bs)]
    for b in range(bs):
        d = st[b]
        ck, cv = ck_ref[b], cv_ref[b]
        kn_b, v_b = seq_rows(kn, b), seq_rows(v, b)
        d["kall"] = jnp.concatenate([ck, kn_b, pad], axis=0).astype(BF16)
        d["vaug"] = jnp.concatenate([jnp.concatenate([cv, v_b, pad], axis=0).astype(BF16), v_ones], axis=1)
        nk_ref[b] = pltpu.roll(ck, WINDOW - t_new, 0)
        nv_ref[b] = pltpu.roll(cv, WINDOW - t_new, 0)
        nk_ref[b, WINDOW - t_new:WINDOW, :] = kn_b
        nv_ref[b, WINDOW - t_new:WINDOW, :] = v_b
        qb = [seq_rows(qj, b) for qj in qn]
        for h in range(KV_HEADS):
            keep = group0 if h == 0 else jnp.logical_not(group0)
            qs = jnp.concatenate([jnp.where(keep, q, 0.0) for q in qb], axis=0).astype(BF16)
            d["s", h] = lax.dot_general(qs, d["kall"], (((1,), (1,)), ((), ())), preferred_element_type=F32)
    for b in range(bs):
        d = st[b]
        for h in range(KV_HEADS):
            s = jnp.where(mask, d.pop(("s", h)), NEG_INF)
            m = jnp.maximum(jnp.max(s, axis=-1, keepdims=True), sinks[h])
            o = jnp.dot(jnp.exp(s - m).astype(BF16), d["vaug"], preferred_element_type=F32)
            d["o", h] = o[:, :LANES] / (o[:, LANES:] + jnp.exp(sinks[h] - m))
    for b in range(bs):
        d = st[b]
        o_b = jnp.concatenate([jnp.where(group0, d["o", 0][j * t_new:(j + 1) * t_new],
                                         d["o", 1][j * t_new:(j + 1) * t_new]) for j in range(nq)], axis=1)
        for tt in range(t_new):
            o_seq[tt * bs + b:tt * bs + b + 1, :] = o_b[tt:tt + 1]
    o_ref[...] = o_seq[...].astype(BF16)


def _attn_sample(proj, cache_k, cache_v, tables, gq, gk, sinks, t_new, layer, prev):
    n = proj.shape[0]
    bsz = n // t_new
    bs = SAMPLE_BLOCK
    rows = bs * t_new
    row = lambda i: (i, 0)
    cache = pl.BlockSpec((None, bs, WINDOW, KV_WIDTH), lambda i: (layer, i, 0, 0))
    prev, prev_specs, aliases = _alias_inputs(prev, 1)
    return pl.pallas_call(
        _skip_refs(functools.partial(_attn_sample_kernel, bs=bs, t_new=t_new, layer=layer), len(prev)),
        grid=(bsz // bs,),
        in_specs=prev_specs + [
            pl.BlockSpec((rows, ATTN_WIDTH), row),
            pl.BlockSpec((rows, 2 * KV_WIDTH), lambda i: (i, EVEN_KV_BLOCK)),
            cache, cache,
            _const_spec((rows, LANES)), _const_spec((rows, LANES)), _const_spec((rows, LANES)),
            _layer_spec((1, LANES), layer), _layer_spec((1, LANES), layer),
            pl.BlockSpec(memory_space=pltpu.SMEM)],
        out_specs=[pl.BlockSpec((rows, ATTN_WIDTH), row), cache, cache],
        out_shape=[jax.ShapeDtypeStruct((n, ATTN_WIDTH), BF16),
                   jax.ShapeDtypeStruct(cache_k.shape, F32), jax.ShapeDtypeStruct(cache_v.shape, F32)],
        input_output_aliases=aliases,
        scratch_shapes=[pltpu.VMEM((rows, ATTN_WIDTH), F32)],
        compiler_params=_cparams("parallel"),
        name="attn_sample",
    )(*prev, proj, proj, cache_k, cache_v, *tables, gq, gk, sinks)


S5_UCHUNKS = S5_WIDTH // LANES
S5_SUB = S5_FLAT // S5_UCHUNKS
S5_SCHUNKS = S5_FLAT // LANES


def _s5_tail(y, wglu_ref, bglu_ref):
    g = 0.5 * y * (1.0 + lax.erf(y * (2.0 ** -0.5)))
    z = jnp.dot(g.astype(BF16), wglu_ref[...], preferred_element_type=F32) + bglu_ref[...]
    return g * _sigmoid(z)


S5_PARTS = 2


def _s5_prompt_kernel(u_ref, wb_ref, wc_ref, lam_ref, d_ref, wglu_ref, bglu_ref,
                      o_ref, sr_ref, si_ref, xs, hst, *, nbatch, tc):
    rows = nbatch * tc
    prow, ptok = rows // S5_PARTS, tc // S5_PARTS

    @pl.when(pl.program_id(1) == 0)
    def _():
        hst[...] = jnp.zeros_like(hst)

    u = jnp.swapaxes(u_ref[...], 0, 1).reshape(rows, S5_WIDTH)
    ub = u.astype(BF16)

    def in_proj(p, cc):
        rs = slice(p * prow, (p + 1) * prow)
        res = jnp.dot(ub[rs, cc * LANES:(cc + 1) * LANES], wb_ref[cc], preferred_element_type=F32)
        for j in range(S5_SUB // LANES):
            xs[cc * 4 + j, rs, :] = res[:, j * LANES:(j + 1) * LANES]
            xs[S5_SCHUNKS + cc * 4 + j, rs, :] = res[:, S5_SUB + j * LANES:S5_SUB + (j + 1) * LANES]

    ys = {}

    def out_proj(p, cc):
        rs = slice(p * prow, (p + 1) * prow)
        s = jnp.concatenate([xs[cc * 4 + j, rs, :] for j in range(4)]
                            + [xs[S5_SCHUNKS + cc * 4 + j, rs, :] for j in range(4)], axis=1).astype(BF16)
        cols = slice(cc * LANES, (cc + 1) * LANES)
        ys[p, cc] = jnp.dot(s, wc_ref[cc], preferred_element_type=F32) + d_ref[:, cols] * u[rs, cols]

    def tail(p):
        out = _s5_tail(jnp.concatenate([ys.pop((p, cc)) for cc in range(S5_UCHUNKS)], axis=1), wglu_ref, bglu_ref)
        o_ref[:, p * ptok:(p + 1) * ptok, :] = jnp.swapaxes(out.reshape(ptok, nbatch, S5_WIDTH), 0, 1).astype(BF16)

    def scan_step(t, h):
        idx = slice(t * nbatch, (t + 1) * nbatch)
        new = list(h)
        for k in range(S5_SCHUNKS):
            hr, hi = h[k], h[S5_SCHUNKS + k]
            lr, li = lam_ref[k], lam_ref[S5_SCHUNKS + k]
            nr = lr * hr - li * hi + xs[k, idx, :]
            ni = lr * hi + li * hr + xs[S5_SCHUNKS + k, idx, :]
            xs[k, idx, :] = nr
            xs[S5_SCHUNKS + k, idx, :] = ni
            new[k], new[S5_SCHUNKS + k] = nr, ni
        return new

    for cc in range(S5_UCHUNKS):
        in_proj(0, cc)
    h = [hst[k] for k in range(2 * S5_SCHUNKS)]
    for p in range(S5_PARTS):
        work = []
        if p + 1 < S5_PARTS:
            work += [functools.partial(in_proj, p + 1, cc) for cc in range(S5_UCHUNKS)]
        if p >= 1:
            work += [functools.partial(out_proj, p - 1, cc) for cc in range(S5_UCHUNKS)]
            work.append(functools.partial(tail, p - 1))
        every = max(1, ptok // max(1, len(work)))
        for i in range(ptok):
            h = scan_step(p * ptok + i, h)
            if work and (i + 1) % every == 0:
                work.pop(0)()
        for w in work:
            w()
    for cc in range(S5_UCHUNKS):
        out_proj(S5_PARTS - 1, cc)
    tail(S5_PARTS - 1)
    for k in range(2 * S5_SCHUNKS):
        hst[k] = h[k]
    sr_ref[...] = jnp.concatenate(h[:S5_SCHUNKS], axis=1)
    si_ref[...] = jnp.concatenate(h[S5_SCHUNKS:], axis=1)


def _s5_prompt(proj, prm, layer, prev):
    bsz, seq, _ = proj.shape
    nbatch, tc = SUBLANES, S5_CHUNK
    n_layers = prm["wb"].shape[0]
    st = pl.BlockSpec((None, nbatch, S5_FLAT), lambda b, c: (layer, b, 0))
    st_shape = jax.ShapeDtypeStruct((n_layers, bsz, S5_FLAT), F32)
    prev, prev_specs, aliases = _alias_inputs(prev, 1)
    names = ("wb", "wc", "lam8", "d", "wglu", "bglu")
    return pl.pallas_call(
        _skip_refs(functools.partial(_s5_prompt_kernel, nbatch=nbatch, tc=tc), len(prev)),
        grid=(bsz // nbatch, seq // tc),
        in_specs=prev_specs + [pl.BlockSpec((nbatch, tc, S5_WIDTH), lambda b, c: (b, c, EVEN_U_BLOCK))]
        + [_layer_spec(prm[k].shape[1:], layer) for k in names],
        out_specs=[pl.BlockSpec((nbatch, tc, S5_WIDTH), lambda b, c: (b, c, 0)), st, st],
        out_shape=[jax.ShapeDtypeStruct((bsz, seq, S5_WIDTH), BF16), st_shape, st_shape],
        input_output_aliases=aliases,
        scratch_shapes=[pltpu.VMEM((2 * S5_SCHUNKS, nbatch * tc, LANES), F32),
                        pltpu.VMEM((2 * S5_SCHUNKS, nbatch, LANES), F32)],
        compiler_params=_cparams("parallel", "arbitrary"),
        name="s5_prompt",
    )(*prev, proj, *[prm[k] for k in names])


def _s5_sample_kernel(u_ref, wb_ref, wc_ref, lr_ref, li_ref, d_ref, wglu_ref, bglu_ref, h0r_ref, h0i_ref,
                      o_ref, sr_ref, si_ref, xr, xi, *, nseq, t_new):
    nt, st = nseq // SAMPLE_TILE, SAMPLE_TILE
    n = nseq * t_new
    u = u_ref[...]
    ub = u.astype(BF16)
    for cc in range(S5_UCHUNKS):
        res = jnp.dot(ub[:, cc * LANES:(cc + 1) * LANES], wb_ref[cc], preferred_element_type=F32)
        sc = slice(cc * S5_SUB, (cc + 1) * S5_SUB)
        xr[:, :, :, sc] = res[:, :S5_SUB].reshape(nt, t_new, st, S5_SUB)
        xi[:, :, :, sc] = res[:, S5_SUB:].reshape(nt, t_new, st, S5_SUB)
    lr, li = lr_ref[...], li_ref[...]
    hr, hi = h0r_ref[...], h0i_ref[...]
    for t in range(t_new):
        nr = lr * hr - li * hi + xr[:, t].reshape(nseq, S5_FLAT)
        ni = lr * hi + li * hr + xi[:, t].reshape(nseq, S5_FLAT)
        xr[:, t] = nr.reshape(nt, st, S5_FLAT)
        xi[:, t] = ni.reshape(nt, st, S5_FLAT)
        hr, hi = nr, ni
    sr_ref[...] = hr
    si_ref[...] = hi
    ys = []
    for cc in range(S5_UCHUNKS):
        sc = slice(cc * S5_SUB, (cc + 1) * S5_SUB)
        s = jnp.concatenate([xr[:, :, :, sc].reshape(n, S5_SUB), xi[:, :, :, sc].reshape(n, S5_SUB)],
                            axis=1).astype(BF16)
        cols = slice(cc * LANES, (cc + 1) * LANES)
        ys.append(jnp.dot(s, wc_ref[cc], preferred_element_type=F32) + d_ref[:, cols] * u[:, cols])
    o_ref[...] = _s5_tail(jnp.concatenate(ys, axis=1), wglu_ref, bglu_ref).astype(BF16)


def _s5_sample(proj, h0r, h0i, prm, t_new, layer, prev):
    n = proj.shape[0]
    nseq = n // t_new
    names = ("wb", "wc", "lr", "li", "d", "wglu", "bglu")
    st = pl.BlockSpec((None, nseq, S5_FLAT), lambda i: (layer, 0, 0))
    prev, prev_specs, aliases = _alias_inputs(prev, 1)
    scratch = pltpu.VMEM((nseq // SAMPLE_TILE, t_new, SAMPLE_TILE, S5_FLAT), F32)
    return pl.pallas_call(
        _skip_refs(functools.partial(_s5_sample_kernel, nseq=nseq, t_new=t_new), len(prev)),
        grid=(1,),
        in_specs=prev_specs + [pl.BlockSpec((n, S5_WIDTH), lambda i: (0, EVEN_U_BLOCK))]
        + [_layer_spec(prm[k].shape[1:], layer) for k in names]
        + [_layer_spec((nseq, S5_FLAT), layer), _layer_spec((nseq, S5_FLAT), layer)],
        out_specs=[pl.BlockSpec((n, S5_WIDTH), lambda i: (0, 0)), st, st],
        out_shape=[jax.ShapeDtypeStruct((n, S5_WIDTH), BF16),
                   jax.ShapeDtypeStruct(h0r.shape, F32), jax.ShapeDtypeStruct(h0i.shape, F32)],
        input_output_aliases=aliases,
        scratch_shapes=[scratch, scratch],
        compiler_params=_cparams("arbitrary"),
        name="s5_sample",
    )(*prev, proj, *[prm[k] for k in names], h0r, h0i)


def _s5_params(a_re, a_im, log_dt, b_re, b_im, c_re, c_im, d_skip, w_glu, b_glu):
    nl = a_re.shape[0]
    dt = jnp.exp(log_dt)
    mag = jnp.exp(a_re * dt)
    lr, li = mag * jnp.cos(a_im * dt), mag * jnp.sin(a_im * dt)
    den = a_re * a_re + a_im * a_im
    cr = ((lr - 1.0) * a_re + li * a_im) / den
    ci = (li * a_re - (lr - 1.0) * a_im) / den
    bbr = cr[..., None] * b_re - ci[..., None] * b_im
    bbi = cr[..., None] * b_im + ci[..., None] * b_re
    gpc = LANES // S5_GROUP
    eye = jnp.eye(gpc, dtype=F32)

    def in_blocks(bb):
        bb = bb.reshape(nl, S5_UCHUNKS, gpc, S5_STATE, S5_GROUP)
        return jnp.einsum("lcgph,gk->lcghkp", bb, eye).reshape(nl, S5_UCHUNKS, LANES, S5_SUB)

    def out_blocks(cm):
        cm = cm.reshape(nl, S5_UCHUNKS, gpc, S5_GROUP, S5_STATE)
        return jnp.einsum("lcghp,gk->lcgpkh", cm, eye).reshape(nl, S5_UCHUNKS, S5_SUB, LANES)

    wb = jnp.concatenate([in_blocks(bbr), in_blocks(bbi)], axis=3).astype(BF16)
    wc = jnp.concatenate([out_blocks(c_re), -out_blocks(c_im)], axis=2).astype(BF16)
    lr_f, li_f = lr.reshape(nl, 1, S5_FLAT), li.reshape(nl, 1, S5_FLAT)
    lam = jnp.concatenate([lr_f.reshape(nl, S5_SCHUNKS, 1, LANES), li_f.reshape(nl, S5_SCHUNKS, 1, LANES)], axis=1)
    lam8 = jnp.broadcast_to(lam, (nl, 2 * S5_SCHUNKS, SUBLANES, LANES))
    return dict(wb=wb, wc=wc, lam8=lam8, lr=lr_f, li=li_f, d=d_skip.reshape(nl, 1, S5_WIDTH),
                wglu=w_glu.astype(BF16), bglu=b_glu.reshape(nl, 1, S5_WIDTH))


ODD_K_BLOCK = 1
ODD_V_BLOCK = (2 * ML_QK) // ML_WIDTH
ODD_O_BLOCK = ODD_V_BLOCK + 1
ODD_G_BLOCK = (2 * ML_QK + 2 * ML_WIDTH) // LANES
ML_AUG = 2 * ML_DV


def _head_out(h, o, gout):
    hn = h * lax.rsqrt(jnp.mean(h * h, axis=-1, keepdims=True) + EPS) * gout
    return (hn * _sigmoid(o)).astype(BF16)


ODDP_V_BLOCK = 0
ODDP_O_BLOCK = 1
ODDP_Q_BLOCK = (2 * ML_WIDTH) // ML_QK
ODDP_G_BLOCK = (2 * ML_WIDTH + ML_QK) // LANES
ML_SPLIT = 3
ML_PIECE_LANES = 2 * ML_HEADS
ML_SEQS = 4


def _norm_matmul_kt_kernel(x_ref, g_ref, *refs, n_w):
    w_refs, wkt_ref, o_ref, kt_ref = refs[:n_w], refs[n_w], refs[n_w + 1], refs[n_w + 2]
    h = _rms(x_ref[...], g_ref[...]).astype(BF16)
    off = 0
    for w_ref in w_refs:
        m = w_ref.shape[1]
        o_ref[:, off:off + m] = jnp.dot(h, w_ref[...], preferred_element_type=F32)
        off += m
    kt = lax.dot_general(wkt_ref[...], h, (((1,), (1,)), ((), ())), preferred_element_type=F32)
    kt_ref[...] = kt * (ML_DK ** -0.5)


def _norm_matmul_kt(xs, g, layer, ws, wkt, widx):
    d = ws[0].shape[1]
    m = sum(w.shape[2] for w in ws)
    mk = wkt.shape[1]
    return _row_groups_call(
        functools.partial(_norm_matmul_kt_kernel, n_w=len(ws)), [[x] for x in xs], [g, *ws, wkt],
        [_layer_spec((1, d), layer)] + [_layer_spec(w.shape[1:], widx) for w in ws] + [_layer_spec((mk, d), widx)],
        [(m, F32, True), (mk, F32, False)], [], "norm_matmul_kt")


def _cummax_rows(x):
    n = x.shape[0]
    row = lax.broadcasted_iota(jnp.int32, x.shape, 0)
    shift = 1
    while shift < n:
        x = jnp.maximum(x, jnp.where(row >= shift, pltpu.roll(x, shift, 0), NEG_INF))
        shift *= 2
    return x


def _pieces(x):
    lane = lax.broadcasted_iota(jnp.int32, x.shape, 1)
    xx = x + pltpu.roll(x, ML_PIECE_LANES, 1) + pltpu.roll(x, 2 * ML_PIECE_LANES, 1)
    a1, a2, a3 = _split3(xx)
    return jnp.where(lane < ML_PIECE_LANES, a1, jnp.where(lane < 2 * ML_PIECE_LANES, a2, a3))


def _ml_select_constants():
    mask = np.zeros((ML_HEADS, LANES), np.float32)
    sel = np.zeros((ML_HEADS, LANES, 2 * ML_DV), np.float32)
    for h in range(ML_HEADS):
        for k in range(ML_SPLIT):
            lo, hi = k * ML_PIECE_LANES + h, k * ML_PIECE_LANES + ML_HEADS + h
            mask[h, lo] = mask[h, hi] = 1.0
            sel[h, lo, :ML_DV] = 1.0
            sel[h, hi, ML_DV:] = 1.0
    return jnp.asarray(mask), jnp.asarray(sel, dtype=BF16)


def _mlstm_prompt_kernel(*refs, tc, nchunks, nseq):
    v_ref, o_ref, q_ref, g_ref = refs[:4]
    kt_refs = refs[4:4 + nseq]
    bias_ref, gout_ref, mask_ref, sel_ref, h_ref, c_ref, n_ref, m_ref, caug, mst = refs[4 + nseq:]
    ci = pl.program_id(1)

    @pl.when(ci == 0)
    def _():
        caug[...] = jnp.zeros_like(caug)
        mst[...] = jnp.zeros_like(mst)

    nh = ML_HEADS
    lane = lax.broadcasted_iota(jnp.int32, (tc, LANES), 1)
    lo, hi = lane < nh, (lane >= nh) & (lane < 2 * nh)
    rt = lax.broadcasted_iota(jnp.int32, (tc, tc), 0)
    cs = lax.broadcasted_iota(jnp.int32, (tc, tc), 1)
    causal = cs <= rt
    tril = jnp.where(causal, 1.0, 0.0).astype(BF16)
    ones = jnp.ones((tc, ML_DV), F32)

    def gates(sq):
        g = g_ref[sq] + bias_ref[...]
        lf = jnp.where(hi, _log_sigmoid(g), 0.0)
        b = sum(jnp.dot(tril, p, preferred_element_type=F32) for p in _split3(lf))
        c = jnp.where(hi, pltpu.roll(g, nh, 1) - b, 0.0)
        m_prev = mst[sq]
        mx = jnp.maximum(_cummax_rows(c), m_prev)
        m_row = b + mx
        mx_lo = pltpu.roll(mx, LANES - nh, 1)
        w_inter = jnp.exp(pltpu.roll(m_prev, LANES - nh, 1) - mx_lo)
        mst[sq] = m_row[tc - 1:tc, :]
        return dict(xc=_pieces(jnp.where(lo, w_inter, jnp.where(hi, jnp.exp(-m_row), 0.0))),
                    lc=_pieces(jnp.where(lo, -mx_lo, jnp.where(hi, 1.0, 0.0))),
                    rc=_pieces(jnp.where(lo, 1.0, jnp.where(hi, c, 0.0))))

    gt = [gates(sq) for sq in range(nseq)]
    units = [(sq, hd) for hd in range(nh) for sq in range(nseq)]
    st = [dict() for _ in units]

    def stage1(u):
        sq, hd = units[u]
        d = st[u]
        rh = gt[sq]["rc"] * mask_ref[hd:hd + 1, :].astype(BF16)
        d["dmat"] = lax.dot_general(gt[sq]["lc"], rh, (((1,), (1,)), ((), ())), preferred_element_type=F32)
        d["wb"] = jnp.dot(gt[sq]["xc"], sel_ref[hd], preferred_element_type=F32)
        d["qh"] = q_ref[sq, :, hd * ML_DK:(hd + 1) * ML_DK]
        d["kt"] = kt_refs[sq][hd * ML_DK:(hd + 1) * ML_DK, :]
        d["qk"] = jnp.dot(d["qh"].astype(BF16), d["kt"].astype(BF16), preferred_element_type=F32)

    def stage2(u):
        sq, hd = units[u]
        d = st[u]
        cols = slice(hd * ML_DV, (hd + 1) * ML_DV)
        d["w"] = jnp.exp(jnp.where(causal, d["dmat"], NEG_INF))
        d["vaug"] = jnp.concatenate([v_ref[sq, :, cols], ones], axis=1).astype(BF16)
        d["cm"] = caug[sq, hd]
        lhs = jnp.concatenate([(d["qk"] * d["w"]).astype(BF16), (d["wb"][:, :ML_DK] * d["qh"]).astype(BF16)], axis=1)
        rhs = jnp.concatenate([d["vaug"], d["cm"].astype(BF16)], axis=0)
        d["both"] = jnp.dot(lhs, rhs, preferred_element_type=F32)
        kw = (d["kt"] * d["w"][tc - 1:tc, :]).astype(BF16)
        d["upd"] = jnp.dot(kw, d["vaug"], preferred_element_type=F32)

    def stage3(u):
        sq, hd = units[u]
        d = st[u]
        cols = slice(hd * ML_DV, (hd + 1) * ML_DV)
        both, wb = d["both"], d["wb"]
        h = both[:, :ML_DV] / jnp.maximum(jnp.abs(both[:, ML_DV:]), wb[:, ML_DV:])
        h_ref[sq, :, cols] = _head_out(h, o_ref[sq, :, cols], gout_ref[:, cols])
        decay = wb[tc - 1:tc, :ML_DV]
        caug[sq, hd] = jnp.concatenate([decay, decay], axis=1) * d["cm"] + d["upd"]
        d.clear()

    for step in range(len(units) + 2):
        if step < len(units):
            stage1(step)
        if 0 <= step - 1 < len(units):
            stage2(step - 1)
        if 0 <= step - 2 < len(units):
            stage3(step - 2)

    @pl.when(ci == nchunks - 1)
    def _():
        c_ref[...] = caug[:, :, :, :ML_DV]
        n_ref[...] = caug[:, :, :, ML_DV:]
        m_ref[...] = mst[...]


def _kt_index(b, c, *, sq, nseq, nchunks):
    return 0, (b * nseq + sq) * nchunks + c


def _mlstm_prompt(proj, kt, bias, gout, consts, layer, prev):
    bsz, seq, _ = proj.shape
    tc, nseq = ML_CHUNK, ML_SEQS
    nchunks = seq // tc
    n_layers = bias.shape[0]
    mask, sel = consts
    blk = lambda w, j: pl.BlockSpec((nseq, tc, w), lambda b, c: (b, c, j))
    st = lambda shape: pl.BlockSpec((None, nseq) + shape, lambda b, c: (layer, b) + (0,) * len(shape))
    st_shape = lambda shape: jax.ShapeDtypeStruct((n_layers, bsz) + shape, F32)
    prev, prev_specs, aliases = _alias_inputs(prev, 1)
    return pl.pallas_call(
        _skip_refs(functools.partial(_mlstm_prompt_kernel, tc=tc, nchunks=nchunks, nseq=nseq), len(prev)),
        grid=(bsz // nseq, nchunks),
        in_specs=prev_specs + [
            blk(ML_WIDTH, ODDP_V_BLOCK), blk(ML_WIDTH, ODDP_O_BLOCK), blk(ML_QK, ODDP_Q_BLOCK),
            blk(LANES, ODDP_G_BLOCK)]
        + [pl.BlockSpec((ML_QK, tc), functools.partial(_kt_index, sq=sq, nseq=nseq, nchunks=nchunks))
           for sq in range(nseq)] + [
            _layer_spec((1, LANES), layer), _layer_spec((1, ML_WIDTH), layer),
            _const_spec(mask.shape), _const_spec(sel.shape)],
        out_specs=[blk(ML_WIDTH, 0), st((ML_HEADS, ML_DK, ML_DV)), st((ML_HEADS, ML_DK, ML_DV)), st((1, LANES))],
        out_shape=[jax.ShapeDtypeStruct((bsz, seq, ML_WIDTH), BF16),
                   st_shape((ML_HEADS, ML_DK, ML_DV)), st_shape((ML_HEADS, ML_DK, ML_DV)), st_shape((1, LANES))],
        input_output_aliases=aliases,
        scratch_shapes=[pltpu.VMEM((nseq, ML_HEADS, ML_DK, ML_AUG), F32), pltpu.VMEM((nseq, 1, LANES), F32)],
        compiler_params=_cparams("parallel", "arbitrary"),
        name="mlstm_prompt",
    )(*prev, proj, proj, proj, proj, *([kt] * nseq), bias, gout, mask, sel)


SAMPLE_TILE = SUBLANES
MLS_SEQS = 32


def _mlstm_sample_kernel(v_ref, o_ref, q_ref, g_ref, kt_ref, bias_ref, gout_ref, mask_ref, sel_ref,
                         c0_ref, n0_ref, m0_ref, h_ref, c_ref, n_ref, m_ref, *, nseq, t_new):
    nh, nt, st = ML_HEADS, nseq // SAMPLE_TILE, SAMPLE_TILE
    rows = nseq * t_new
    lane = lax.broadcasted_iota(jnp.int32, (rows, LANES), 1)
    lo, hi = lane < nh, (lane >= nh) & (lane < 2 * nh)
    tiles = lambda a: a.reshape(nt, t_new, st, a.shape[-1])
    flat = lambda a: a.reshape(rows, a.shape[-1])
    per_seq = lambda a: a.reshape(nseq, a.shape[-1])

    g = g_ref[...] + bias_ref[...]
    lf = tiles(jnp.where(hi, _log_sigmoid(g), 0.0))
    ig = tiles(jnp.where(hi, pltpu.roll(g, nh, 1), 0.0))
    m_prev = m0_ref[...].reshape(nt, st, LANES)
    bs, cs, ms = [], [], []
    b_run, m_run = None, m_prev
    for t in range(t_new):
        b_run = lf[:, t] if b_run is None else b_run + lf[:, t]
        c_t = ig[:, t] - b_run
        m_run = jnp.maximum(m_run, c_t)
        bs.append(b_run)
        cs.append(c_t)
        ms.append(m_run)
    stack = lambda xs: flat(jnp.stack(xs, axis=1))
    b, c, mx = stack(bs), stack(cs), stack(ms)
    m_prev_rows = stack([m_prev] * t_new)
    m_row = b + mx
    m_ref[...] = per_seq(bs[-1] + ms[-1])
    mx_lo = pltpu.roll(mx, LANES - nh, 1)
    w_inter = jnp.exp(pltpu.roll(m_prev_rows, LANES - nh, 1) - mx_lo)
    xc = _pieces(jnp.where(lo, w_inter, jnp.where(hi, jnp.exp(-m_row), 0.0)))
    lc = _pieces(jnp.where(lo, -mx_lo, jnp.where(hi, 1.0, 0.0)))
    rc = _pieces(jnp.where(lo, 1.0, jnp.where(hi, c, 0.0)))

    def seq_of(idx):
        return (idx // (t_new * st)) * st + idx % st, (idx % (t_new * st)) // st

    rt = lax.broadcasted_iota(jnp.int32, (rows, rows), 0)
    ct = lax.broadcasted_iota(jnp.int32, (rows, rows), 1)
    (rs, rtok), (cseq, ctok) = seq_of(rt), seq_of(ct)
    valid = (rs == cseq) & (ctok <= rtok)
    rq = lax.broadcasted_iota(jnp.int32, (rows, nseq * ML_DK), 0)
    cq = lax.broadcasted_iota(jnp.int32, (rows, nseq * ML_DK), 1)
    own_q = seq_of(rq)[0] == cq // ML_DK
    rk = lax.broadcasted_iota(jnp.int32, (nseq * ML_DK, rows), 0)
    ck = lax.broadcasted_iota(jnp.int32, (nseq * ML_DK, rows), 1)
    own_k = rk // ML_DK == seq_of(ck)[0]
    ones = jnp.ones((rows, ML_DV), F32)
    last = lambda a: per_seq(tiles(a)[:, t_new - 1])

    for hd in range(nh):
        cols = slice(hd * ML_DV, (hd + 1) * ML_DV)
        rh = rc * mask_ref[hd:hd + 1, :].astype(BF16)
        dmat = lax.dot_general(lc, rh, (((1,), (1,)), ((), ())), preferred_element_type=F32)
        wb = jnp.dot(xc, sel_ref[hd], preferred_element_type=F32)
        qh = q_ref[:, hd * ML_DK:(hd + 1) * ML_DK]
        kt = kt_ref[hd * ML_DK:(hd + 1) * ML_DK, :]
        ktb = kt.astype(BF16)
        w = jnp.exp(jnp.where(valid, dmat, NEG_INF))
        qk = jnp.dot(qh.astype(BF16), ktb, preferred_element_type=F32) * w
        vaug = jnp.concatenate([v_ref[:, cols], ones], axis=1).astype(BF16)
        po = jnp.dot(qk.astype(BF16), vaug, preferred_element_type=F32)
        wq = wb[:, :ML_DK] * qh
        wq2 = jnp.concatenate([wq, wq], axis=1)
        wq_bd = jnp.where(own_q, jnp.concatenate([wq2] * (nseq * ML_DK // LANES), axis=1), 0.0).astype(BF16)
        cstack = c0_ref[:, hd].reshape(nseq * ML_DK, ML_DV)
        num = po[:, :ML_DV] + jnp.dot(wq_bd, cstack.astype(BF16), preferred_element_type=F32)
        n0 = n0_ref[hd]
        n_rows = stack([n0.reshape(nt, st, ML_DK)] * t_new)
        den = po[:, ML_DV:] + jnp.sum(wq * n_rows, axis=-1, keepdims=True)
        h = num / jnp.maximum(jnp.abs(den), wb[:, ML_DV:])
        h_ref[:, cols] = _head_out(h, o_ref[:, cols], gout_ref[:, cols])
        w_last = last(w)
        decay = last(wb[:, :ML_DV])
        n_upd = lax.dot_general(w_last.astype(BF16), ktb, (((1,), (1,)), ((), ())), preferred_element_type=F32)
        n_ref[hd] = decay[:, :ML_DK] * n0 + n_upd
        wk = jnp.sum(w_last, axis=0, keepdims=True)
        kw_bd = jnp.where(own_k, jnp.concatenate([kt * wk] * nseq, axis=0), 0.0).astype(BF16)
        upd = jnp.dot(kw_bd, v_ref[:, cols].astype(BF16), preferred_element_type=F32)
        decay_rows = jnp.broadcast_to(decay[:, None, :], (nseq, ML_DK, ML_DV)).reshape(nseq * ML_DK, ML_DV)
        c_ref[:, hd] = (decay_rows * cstack + upd).reshape(nseq, ML_DK, ML_DV)


def _mlstm_sample(proj, kt, bias, gout, consts, c0, n0h, m0, t_new, layer, prev):
    n = proj.shape[0]
    bsz = n // t_new
    nseq = MLS_SEQS
    rows = nseq * t_new
    mask, sel = consts
    blk = lambda w, j: pl.BlockSpec((rows, w), lambda i: (i, j))
    cst = pl.BlockSpec((None, nseq, ML_HEADS, ML_DK, ML_DV), lambda i: (layer, i, 0, 0, 0))
    nst = pl.BlockSpec((None, ML_HEADS, nseq, ML_DK), lambda i: (layer, 0, i, 0))
    mst = pl.BlockSpec((None, nseq, LANES), lambda i: (layer, i, 0))
    prev, prev_specs, aliases = _alias_inputs(prev, 1)
    return pl.pallas_call(
        _skip_refs(functools.partial(_mlstm_sample_kernel, nseq=nseq, t_new=t_new), len(prev)),
        grid=(bsz // nseq,),
        in_specs=prev_specs + [
            blk(ML_WIDTH, ODDP_V_BLOCK), blk(ML_WIDTH, ODDP_O_BLOCK), blk(ML_QK, ODDP_Q_BLOCK),
            blk(LANES, ODDP_G_BLOCK), pl.BlockSpec((ML_QK, rows), lambda i: (0, i)),
            _layer_spec((1, LANES), layer), _layer_spec((1, ML_WIDTH), layer),
            _const_spec(mask.shape), _const_spec(sel.shape), cst, nst, mst],
        out_specs=[blk(ML_WIDTH, 0), cst, nst, mst],
        out_shape=[jax.ShapeDtypeStruct((n, ML_WIDTH), BF16),
                   jax.ShapeDtypeStruct(c0.shape, F32), jax.ShapeDtypeStruct(n0h.shape, F32),
                   jax.ShapeDtypeStruct(m0.shape, F32)],
        input_output_aliases=aliases,
        compiler_params=_cparams("parallel"),
        name="mlstm_sample",
    )(*prev, proj, proj, proj, proj, kt, bias, gout, mask, sel, c0, n0h, m0)


def _pad_lanes(x):
    return jnp.pad(x, [(0, 0)] * (x.ndim - 1) + [(0, LANES - x.shape[-1])])


def kernel(x_prompt, x_sample, cache_k, cache_v, state_ssm_re, state_ssm_im, state_mlstm_c, state_mlstm_n, state_mlstm_m, norm_mix, norm_ffn, w_in_even, q_norm, k_norm, attn_sinks, s5_a_re, s5_a_im, s5_log_dt, s5_b_re, s5_b_im, s5_c_re, s5_c_im, s5_d, s5_w_glu, s5_b_glu, w_out_even, w_in_odd, ml_b_i, ml_b_f, ml_out_norm, w_out_odd, w_gate, w_up, w_down):
    bp, lp, _ = x_prompt.shape
    bsm, ls, _ = x_sample.shape
    yp = x_prompt.reshape(bp * lp, D_MODEL)
    ys = x_sample.reshape(bsm // SAMPLE_TILE, SAMPLE_TILE, ls, D_MODEL).transpose(0, 2, 1, 3).reshape(bsm * ls, D_MODEL)
    tab_p = _rope_tables(jnp.arange(lp))
    tab_s = tuple(jnp.repeat(t, SAMPLE_TILE, axis=0) for t in _rope_tables(PAST_LEN + jnp.arange(ls)))
    n_even, n_odd = w_in_even.shape[0], w_in_odd.shape[0]

    g_mix = norm_mix.reshape(DEPTH, 1, D_MODEL)
    g_ffn = norm_ffn.reshape(DEPTH, 1, D_MODEL)
    wg, wu, wd = w_gate.astype(BF16), w_up.astype(BF16), w_down.astype(BF16)
    kv0, u0 = ATTN_WIDTH, ATTN_WIDTH + 2 * KV_WIDTH
    order = jnp.asarray(ATTN_HEAD_ORDER)
    wq = w_in_even[..., :kv0].reshape(n_even, D_MODEL, ATTN_HEADS, HEAD_DIM)[:, :, order].reshape(n_even, D_MODEL, kv0)
    w_in_e = jnp.concatenate([wq, w_in_even[..., u0:], w_in_even[..., kv0:u0]], axis=-1).astype(BF16)
    wo_attn = w_out_even[:, :kv0].reshape(n_even, ATTN_HEADS, HEAD_DIM, D_MODEL)[:, order].reshape(n_even, kv0, D_MODEL)
    w_out_e = jnp.concatenate([wo_attn, w_out_even[:, kv0:]], axis=1).astype(BF16)
    gq = jnp.tile(q_norm, (1, LANES // HEAD_DIM)).reshape(n_even, 1, LANES)
    gk = jnp.tile(k_norm, (1, LANES // HEAD_DIM)).reshape(n_even, 1, LANES)
    prm = _s5_params(s5_a_re, s5_a_im, s5_log_dt, s5_b_re, s5_b_im, s5_c_re, s5_c_im, s5_d, s5_w_glu, s5_b_glu)
    k0, v0, g0 = ML_QK, 2 * ML_QK, 2 * ML_QK + 2 * ML_WIDTH
    w_in_o = (w_in_odd[..., v0:g0].astype(BF16), w_in_odd[..., :k0].astype(BF16),
              _pad_lanes(w_in_odd[..., g0:]).astype(BF16))
    w_kt = jnp.swapaxes(w_in_odd[..., k0:v0], 1, 2).astype(BF16)
    ml_consts = _ml_select_constants()
    w_out_o = w_out_odd.astype(BF16)
    ml_bias = _pad_lanes(jnp.concatenate([ml_b_i, ml_b_f], axis=-1)).reshape(n_odd, 1, LANES)
    ml_gout = ml_out_norm.reshape(n_odd, 1, ML_WIDTH)
    ck = cache_k.reshape(n_even, bsm, WINDOW, KV_WIDTH)
    cv = cache_v.reshape(n_even, bsm, WINDOW, KV_WIDTH)
    h0r = state_ssm_re.reshape(n_even, bsm, S5_FLAT)
    h0i = state_ssm_im.reshape(n_even, bsm, S5_FLAT)
    n0h = jnp.swapaxes(state_mlstm_n, 1, 2)
    m0 = jnp.pad(state_mlstm_m, ((0, 0), (0, 0), (ML_HEADS, LANES - 2 * ML_HEADS)))

    p_attn = p_ssm = p_ml = s_attn = s_ssm = s_ml = None
    for layer in range(DEPTH):
        ffn = (layer, g_ffn, wg, wu, wd)
        if layer % 2 == 0:
            e = layer // 2
            proj_p, proj_s = _norm_matmul([yp, ys], g_mix, layer, w_in_e, e)
            proj3 = proj_p.reshape(bp, lp, -1)
            attn_p, *p_attn = _attn_prompt(proj3, tab_p, gq, gk, attn_sinks, e, p_attn)
            ssm_p, *p_ssm = _s5_prompt(proj3, prm, e, p_ssm)
            attn_s, *s_attn = _attn_sample(proj_s, ck, cv, tab_s, gq, gk, attn_sinks, ls, e, s_attn)
            ssm_s, *s_ssm = _s5_sample(proj_s, h0r, h0i, prm, ls, e, s_ssm)
            yp, ys = _mix_ffn([[yp, attn_p.reshape(bp * lp, -1), ssm_p.reshape(bp * lp, -1)], [ys, attn_s, ssm_s]],
                              w_out_e, e, *ffn)
        else:
            o = layer // 2
            (proj_p, kt_p), (proj_s, kt_s) = _norm_matmul_kt([yp, ys], g_mix, layer, w_in_o, w_kt, o)
            hh_p, *p_ml = _mlstm_prompt(proj_p.reshape(bp, lp, -1), kt_p, ml_bias, ml_gout, ml_consts, o, p_ml)
            hh_s, *s_ml = _mlstm_sample(proj_s, kt_s, ml_bias, ml_gout, ml_consts, state_mlstm_c, n0h, m0, ls, o,
                                        s_ml)
            yp, ys = _mix_ffn([[yp, hh_p.reshape(bp * lp, -1)], [ys, hh_s]], w_out_o, o, *ffn)
    heads = lambda a: a.reshape(a.shape[:3] + (KV_HEADS, HEAD_DIM))
    groups = lambda a: a.reshape(a.shape[:2] + (S5_GROUPS, S5_STATE))
    ys = ys.reshape(bsm // SAMPLE_TILE, ls, SAMPLE_TILE, D_MODEL).transpose(0, 2, 1, 3).reshape(bsm, ls, D_MODEL)
    return (yp.reshape(bp, lp, D_MODEL), ys,
            heads(p_attn[0]), heads(p_attn[1]), groups(p_ssm[0]), groups(p_ssm[1]),
            p_ml[0], p_ml[1][..., 0], p_ml[2][:, :, 0, ML_HEADS:2 * ML_HEADS],
            heads(s_attn[0]), heads(s_attn[1]), groups(s_ssm[0]), groups(s_ssm[1]),
            s_ml[0], jnp.swapaxes(s_ml[1], 1, 2), s_ml[2][..., ML_HEADS:2 * ML_HEADS])
```

```python
import functools

import numpy as np

import jax
import jax.numpy as jnp
from jax import lax
from jax.experimental import pallas as pl
from jax.experimental.pallas import tpu as pltpu

F32 = jnp.float32
BF16 = jnp.bfloat16

D_MODEL = 1024
DEPTH = 4
PAST_LEN = 8192
WINDOW = 128
ATTN_HEADS = 8
KV_HEADS = 2
HEAD_DIM = 64
ATTN_WIDTH = ATTN_HEADS * HEAD_DIM
KV_WIDTH = KV_HEADS * HEAD_DIM
ROT_DIM = HEAD_DIM // 4
ROPE_THETA = 500000.0
S5_GROUP = 16
S5_WIDTH = D_MODEL // 2
S5_GROUPS = S5_WIDTH // S5_GROUP
S5_STATE = 64
S5_FLAT = S5_GROUPS * S5_STATE
ML_HEADS = 8
ML_DV = D_MODEL // ML_HEADS
ML_DK = ML_DV // 2
ML_QK = ML_HEADS * ML_DK
ML_WIDTH = ML_HEADS * ML_DV
ODD_IN = 2 * ML_QK + 2 * ML_WIDTH + 2 * ML_HEADS
ODD_IN_PAD = 2 * ML_QK + 2 * ML_WIDTH + 128
D_FF = 2816
EPS = 1e-6

LANES = 128
SUBLANES = 8
ROW_TILE = 512
FF_TILE = 256
S5_CHUNK = 64
ML_CHUNK = 128
SAMPLE_TILE = SUBLANES
VMEM_LIMIT = 56 * 1024 * 1024

NEG_INF = float("-inf")


def _cparams(*sem):
    return pltpu.CompilerParams(dimension_semantics=sem, vmem_limit_bytes=VMEM_LIMIT)


def _const_spec(shape):
    zeros = (0,) * len(shape)
    return pl.BlockSpec(shape, lambda *_: zeros, pipeline_mode=pl.Buffered(1))


def _layer_spec(shape, layer):
    zeros = (0,) * len(shape)
    return pl.BlockSpec((None,) + tuple(shape), lambda *_: (layer,) + zeros, pipeline_mode=pl.Buffered(1))


def _skip_refs(body, n_skip):
    if n_skip == 0:
        return body

    def wrapped(*refs):
        return body(*refs[n_skip:])

    return wrapped


def _alias_inputs(prev, first_state_out):
    prev = () if prev is None else tuple(prev)
    specs = [pl.BlockSpec(memory_space=pl.ANY) for _ in prev]
    aliases = {i: first_state_out + i for i in range(len(prev))}
    return prev, specs, aliases


def _rms(x, g):
    ms = jnp.mean(x * x, axis=-1, keepdims=True)
    return x * lax.rsqrt(ms + EPS) * g


def _split3(a):
    a1 = a.astype(BF16)
    r1 = a - a1.astype(F32)
    a2 = r1.astype(BF16)
    a3 = (r1 - a2.astype(F32)).astype(BF16)
    return a1, a2, a3


def _log_sigmoid(x):
    return jnp.minimum(x, 0.0) - jnp.log(1.0 + jnp.exp(-jnp.abs(x)))


def _sigmoid(x):
    return 1.0 / (1.0 + jnp.exp(-x))


def _norm_matmul_kernel(x_ref, g_ref, w_ref, o_ref):
    h = _rms(x_ref[...], g_ref[...]).astype(BF16)
    o_ref[...] = jnp.dot(h, w_ref[...], preferred_element_type=F32)


def _row_groups_call(body, groups, consts, const_specs, out_defs, scratch_shapes, name):
    steps, tiles = [], []
    for arrays in groups:
        n = arrays[0].shape[0]
        tm = min(ROW_TILE, n)
        tiles.append(tm)
        steps.append(n // tm)
    offs = [sum(steps[:k]) for k in range(len(groups))]

    def local(k):
        return lambda i: jnp.clip(i - offs[k], 0, steps[k] - 1)

    in_specs, out_specs, out_shape, args = [], [], [], []
    for k, arrays in enumerate(groups):
        for a in arrays:
            in_specs.append(pl.BlockSpec((tiles[k], a.shape[1]), lambda i, f=local(k): (f(i), 0)))
            args.append(a)
    for k, arrays in enumerate(groups):
        n = arrays[0].shape[0]
        for width, dtype, by_rows in out_defs:
            if by_rows:
                out_specs.append(pl.BlockSpec((tiles[k], width), lambda i, f=local(k): (f(i), 0)))
                out_shape.append(jax.ShapeDtypeStruct((n, width), dtype))
            else:
                out_specs.append(pl.BlockSpec((width, tiles[k]), lambda i, f=local(k): (0, f(i))))
                out_shape.append(jax.ShapeDtypeStruct((width, n), dtype))
    n_in = [len(arrays) for arrays in groups]
    n_out = len(out_defs)

    def kern(*refs):
        i = pl.program_id(0)
        pos = 0
        ins = []
        for cnt in n_in:
            ins.append(refs[pos:pos + cnt])
            pos += cnt
        crefs = refs[pos:pos + len(consts)]
        pos += len(consts)
        outs = [refs[pos + k * n_out:pos + (k + 1) * n_out] for k in range(len(groups))]
        scratch = refs[pos + len(groups) * n_out:]
        for k in range(len(groups)):
            @pl.when((i >= offs[k]) & (i < offs[k] + steps[k]))
            def _(k=k):
                body(*ins[k], *crefs, *outs[k], *scratch)

    res = pl.pallas_call(
        kern,
        grid=(sum(steps),),
        in_specs=in_specs + list(const_specs),
        out_specs=out_specs,
        out_shape=out_shape,
        scratch_shapes=scratch_shapes,
        compiler_params=_cparams("arbitrary"),
        name=name,
    )(*args, *consts)
    return [res[k * n_out:(k + 1) * n_out] for k in range(len(groups))]


def _norm_matmul(xs, g, layer, w, widx):
    d, m = w.shape[1], w.shape[2]
    res = _row_groups_call(_norm_matmul_kernel, [[x] for x in xs], [g, w],
                           [_layer_spec((1, d), layer), _layer_spec((d, m), widx)],
                           [(m, F32, True)], [], "norm_matmul")
    return [r[0] for r in res]


def _mix_ffn_kernel(*refs, n_mix):
    x_ref = refs[0]
    a_refs = refs[1:1 + n_mix]
    wo_ref, g_ref, wg_ref, wu_ref, wd_ref, o_ref, act_ref = refs[1 + n_mix:]
    y = x_ref[...]
    off = 0
    for a_ref in a_refs:
        ka = a_ref.shape[1]
        y = y + jnp.dot(a_ref[...], wo_ref[off:off + ka, :], preferred_element_type=F32)
        off += ka
    h = _rms(y, g_ref[...]).astype(BF16)
    for f in range(D_FF // FF_TILE):
        cols = slice(f * FF_TILE, (f + 1) * FF_TILE)
        gate = jnp.dot(h, wg_ref[:, cols], preferred_element_type=F32)
        up = jnp.dot(h, wu_ref[:, cols], preferred_element_type=F32)
        act_ref[:, cols] = (gate * _sigmoid(gate) * up).astype(BF16)
    o_ref[...] = y + jnp.dot(act_ref[...], wd_ref[...], preferred_element_type=F32)


def _mix_ffn(groups, w_out, oidx, layer, g_ffn, wg, wu, wd):
    d = w_out.shape[2]
    n_mix = len(groups[0]) - 1
    tm = min(ROW_TILE, max(g[0].shape[0] for g in groups))
    res = _row_groups_call(
        functools.partial(_mix_ffn_kernel, n_mix=n_mix), groups, [w_out, g_ffn, wg, wu, wd],
        [_layer_spec(w_out.shape[1:], oidx), _layer_spec((1, d), layer), _layer_spec(wg.shape[1:], layer),
         _layer_spec(wu.shape[1:], layer), _layer_spec(wd.shape[1:], layer)],
        [(d, F32, True)], [pltpu.VMEM((tm, D_FF), BF16)], "mix_ffn")
    return [r[0] for r in res]


def _head_ones():
    r = lax.broadcasted_iota(jnp.int32, (LANES, LANES), 0) // HEAD_DIM
    c = lax.broadcasted_iota(jnp.int32, (LANES, LANES), 1) // HEAD_DIM
    return jnp.where(r == c, 1.0, 0.0).astype(BF16)


def _qk_prep(x, g, ones, ct, sa, sb):
    x2 = x * x
    hi = x2.astype(BF16)
    lo = (x2 - hi.astype(F32)).astype(BF16)
    ss = jnp.dot(hi, ones, preferred_element_type=F32) + jnp.dot(lo, ones, preferred_element_type=F32)
    xn = x * lax.rsqrt(ss * (1.0 / HEAD_DIM) + EPS) * g
    return xn * ct + pltpu.roll(xn, LANES - ROT_DIM // 2, 1) * sa + pltpu.roll(xn, ROT_DIM // 2, 1) * sb


def _rope_tables(pos):
    half = ROT_DIM // 2
    inv = jnp.power(jnp.float32(ROPE_THETA), -jnp.arange(half, dtype=F32) / half)
    ang = pos.astype(F32)[:, None] * inv[None, :]
    cos, sin = jnp.cos(ang), jnp.sin(ang)
    n = pos.shape[0]
    one = jnp.ones((n, HEAD_DIM - ROT_DIM), F32)
    zero = jnp.zeros((n, HEAD_DIM - ROT_DIM), F32)
    z8 = jnp.zeros((n, half), F32)
    ct = jnp.concatenate([cos, cos, one], axis=1)
    sa = jnp.concatenate([-sin, z8, zero], axis=1)
    sb = jnp.concatenate([z8, sin, zero], axis=1)
    tile = lambda t: jnp.concatenate([t, t], axis=1)
    return tile(ct), tile(sa), tile(sb)


ATTN_SEQS = 4
ATTN_QCHUNKS = ATTN_WIDTH // LANES
ATTN_HEAD_ORDER = tuple(h * ATTN_QCHUNKS + j for j in range(ATTN_QCHUNKS) for h in range(KV_HEADS))


def _attn_prompt_kernel(q_ref, kv_ref, ct_ref, sa_ref, sb_ref, gq_ref, gk_ref, sink_ref,
                        o_ref, pk_ref, pv_ref, kprev, vprev, *, nb, layer, nseq):
    i = pl.program_id(1)

    @pl.when(i == 0)
    def _():
        kprev[...] = jnp.zeros_like(kprev)
        vprev[...] = jnp.zeros_like(vprev)

    ones = _head_ones()
    ct, sa, sb = ct_ref[...], sa_ref[...], sb_ref[...]
    r = lax.broadcasted_iota(jnp.int32, (WINDOW, 2 * WINDOW), 0)
    c = lax.broadcasted_iota(jnp.int32, (WINDOW, 2 * WINDOW), 1)
    rel = r + WINDOW - c
    mask = (rel >= 0) & (rel <= WINDOW) & ((c >= WINDOW) | (i > 0))
    lane = lax.broadcasted_iota(jnp.int32, (WINDOW, LANES), 1)
    group0 = lane < HEAD_DIM
    v_ones = jnp.ones((2 * WINDOW, LANES), BF16)
    nq = ATTN_QCHUNKS
    st = [dict() for _ in range(nseq)]

    def prep(sq):
        d = st[sq]
        kv = kv_ref[sq]
        d["kn"] = _qk_prep(kv[:, :KV_WIDTH], gk_ref[...], ones, ct, sa, sb)
        d["v"] = kv[:, KV_WIDTH:]
        d["qn"] = [_qk_prep(q_ref[sq, :, j * LANES:(j + 1) * LANES], gq_ref[...], ones, ct, sa, sb)
                   * (HEAD_DIM ** -0.5) for j in range(nq)]
        d["kcat"] = jnp.concatenate([kprev[sq], d["kn"]], axis=0).astype(BF16)
        d["vaug"] = jnp.concatenate([jnp.concatenate([vprev[sq], d["v"]], axis=0).astype(BF16), v_ones], axis=1)
        kprev[sq] = d["kn"]
        vprev[sq] = d["v"]

    def scores(sq, h):
        d = st[sq]
        keep = group0 if h == 0 else jnp.logical_not(group0)
        qs = jnp.concatenate([jnp.where(keep, qj, 0.0) for qj in d["qn"]], axis=0).astype(BF16)
        d["s", h] = lax.dot_general(qs, d["kcat"], (((1,), (1,)), ((), ())), preferred_element_type=F32)

    def softmax_pv(sq, h):
        d = st[sq]
        s = d.pop(("s", h))
        ps, corr = [], []
        for j in range(nq):
            sg = jnp.where(mask, s[j * WINDOW:(j + 1) * WINDOW], NEG_INF)
            sink = sink_ref[layer, h * nq + j]
            m = jnp.maximum(jnp.max(sg, axis=-1, keepdims=True), sink)
            ps.append(jnp.exp(sg - m).astype(BF16))
            corr.append(jnp.exp(sink - m))
        o = jnp.dot(jnp.concatenate(ps, axis=0), d["vaug"], preferred_element_type=F32)
        d["o", h] = [o[j * WINDOW:(j + 1) * WINDOW, :LANES] / (o[j * WINDOW:(j + 1) * WINDOW, LANES:] + corr[j])
                     for j in range(nq)]

    def finish(sq):
        d = st[sq]
        o_ref[sq] = jnp.concatenate([jnp.where(group0, d["o", 0][j], d["o", 1][j]) for j in range(nq)],
                                    axis=1).astype(BF16)

    for sq in range(nseq):
        prep(sq)
    for sq in range(nseq):
        scores(sq, 0)
        scores(sq, 1)
    for sq in range(nseq):
        softmax_pv(sq, 0)
        softmax_pv(sq, 1)
        finish(sq)

    @pl.when(i == nb - 1)
    def _():
        for sq in range(nseq):
            pk_ref[sq] = st[sq]["kn"]
            pv_ref[sq] = st[sq]["v"]


def _attn_prompt(proj, tables, gq, gk, sinks, layer, prev):
    bsz, seq, _ = proj.shape
    nb = seq // WINDOW
    nseq = ATTN_SEQS
    n_layers = gq.shape[0]
    tab = pl.BlockSpec((WINDOW, LANES), lambda b, i: (i, 0))
    prev, prev_specs, aliases = _alias_inputs(prev, 1)
    win = pl.BlockSpec((None, nseq, WINDOW, KV_WIDTH), lambda b, i: (layer, b, 0, 0))
    win_shape = jax.ShapeDtypeStruct((n_layers, bsz, WINDOW, KV_WIDTH), F32)
    return pl.pallas_call(
        _skip_refs(functools.partial(_attn_prompt_kernel, nb=nb, layer=layer, nseq=nseq), len(prev)),
        grid=(bsz // nseq, nb),
        in_specs=prev_specs + [
            pl.BlockSpec((nseq, WINDOW, ATTN_WIDTH), lambda b, i: (b, i, 0)),
            pl.BlockSpec((nseq, WINDOW, 2 * KV_WIDTH), lambda b, i: (b, i, EVEN_KV_BLOCK)),
            tab, tab, tab, _layer_spec((1, LANES), layer), _layer_spec((1, LANES), layer),
            pl.BlockSpec(memory_space=pltpu.SMEM)],
        out_specs=[pl.BlockSpec((nseq, WINDOW, ATTN_WIDTH), lambda b, i: (b, i, 0)), win, win],
        out_shape=[jax.ShapeDtypeStruct((bsz, seq, ATTN_WIDTH), BF16), win_shape, win_shape],
        input_output_aliases=aliases,
        scratch_shapes=[pltpu.VMEM((nseq, WINDOW, KV_WIDTH), F32), pltpu.VMEM((nseq, WINDOW, KV_WIDTH), F32)],
        compiler_params=_cparams("parallel", "arbitrary"),
        name="attn_prompt",
    )(*prev, proj, proj, *tables, gq, gk, sinks)


EVEN_U_BLOCK = ATTN_WIDTH // S5_WIDTH
EVEN_KV_BLOCK = (ATTN_WIDTH + S5_WIDTH) // (2 * KV_WIDTH)
KALL_ROWS = WINDOW + SUBLANES


def _attn_sample_kernel(q_ref, kv_ref, ck_ref, cv_ref, ct_ref, sa_ref, sb_ref, gq_ref, gk_ref, sink_ref,
                        o_ref, nk_ref, nv_ref, o_seq, *, bs, t_new, layer):
    ones = _head_ones()
    ct, sa, sb = ct_ref[...], sa_ref[...], sb_ref[...]
    kv = kv_ref[...]
    kn = _qk_prep(kv[:, :KV_WIDTH], gk_ref[...], ones, ct, sa, sb)
    v = kv[:, KV_WIDTH:]
    nq = ATTN_QCHUNKS
    qn = [_qk_prep(q_ref[:, j * LANES:(j + 1) * LANES], gq_ref[...], ones, ct, sa, sb) * (HEAD_DIM ** -0.5)
          for j in range(nq)]
    rows = nq * t_new
    r = lax.broadcasted_iota(jnp.int32, (rows, KALL_ROWS), 0)
    c = lax.broadcasted_iota(jnp.int32, (rows, KALL_ROWS), 1)
    t = r % t_new
    mask = (c >= t) & (c <= t + WINDOW)
    rj = lax.broadcasted_iota(jnp.int32, (rows, 1), 0) // t_new
    lane = lax.broadcasted_iota(jnp.int32, (t_new, LANES), 1)
    group0 = lane < HEAD_DIM
    pad = jnp.zeros((KALL_ROWS - WINDOW - t_new, KV_WIDTH), F32)
    v_ones = jnp.ones((KALL_ROWS, LANES), BF16)

    def seq_rows(a, b):
        return jnp.concatenate([a[tt * bs + b:tt * bs + b + 1] for tt in range(t_new)], axis=0)

    sinks = []
    for h in range(KV_HEADS):
        sk = jnp.zeros((rows, 1), F32)
        for j in range(nq):
            sk = jnp.where(rj == j, sink_ref[layer, h * nq + j], sk)
        sinks.append(sk)

    st = [dict() for _ in range(bs)]
    for b in range(bs):
        d = st[b]
        ck, cv = ck_ref[b], cv_ref[b]
        kn_b, v_b = seq_rows(kn, b), seq_rows(v, b)
        d["kall"] = jnp.concatenate([ck, kn_b, pad], axis=0).astype(BF16)
        d["vaug"] = jnp.concatenate([jnp.concatenate([cv, v_b, pad], axis=0).astype(BF16), v_ones], axis=1)
        nk_ref[b] = pltpu.roll(ck, WINDOW - t_new, 0)
        nv_ref[b] = pltpu.roll(cv, WINDOW - t_new, 0)
        nk_ref[b, WINDOW - t_new:WINDOW, :] = kn_b
        nv_ref[b, WINDOW - t_new:WINDOW, :] = v_b
        qb = [seq_rows(qj, b) for qj in qn]
        for h in range(KV_HEADS):
            keep = group0 if h == 0 else jnp.logical_not(group0)
            qs = jnp.concatenate([jnp.where(keep, q, 0.0) for q in qb], axis=0).astype(BF16)
            d["s", h] = lax.dot_general(qs, d["kall"], (((1,), (1,)), ((), ())), preferred_element_type=F32)
    for b in range(bs):
        d = st[b]
        for h in range(KV_HEADS):
            s = jnp.where(mask, d.pop(("s", h)), NEG_INF)
            m = jnp.maximum(jnp.max(s, axis=-1, keepdims=True), sinks[h])
            o = jnp.dot(jnp.exp(s - m).astype(BF16), d["vaug"], preferred_element_type=F32)
            d["o", h] = o[:, :LANES] / (o[:, LANES:] + jnp.exp(sinks[h] - m))
    for b in range(bs):
        d = st[b]
        o_b = jnp.concatenate([jnp.where(group0, d["o", 0][j * t_new:(j + 1) * t_new],
                                         d["o", 1][j * t_new:(j + 1) * t_new]) for j in range(nq)], axis=1)
        for tt in range(t_new):
            o_seq[tt * bs + b:tt * bs + b + 1, :] = o_b[tt:tt + 1]
    o_ref[...] = o_seq[...].astype(BF16)


def _attn_sample(proj, cache_k, cache_v, tables, gq, gk, sinks, t_new, layer, prev):
    n = proj.shape[0]
    bsz = n // t_new
    bs = SAMPLE_TILE
    rows = bs * t_new
    row = lambda i: (i, 0)
    cache = pl.BlockSpec((None, bs, WINDOW, KV_WIDTH), lambda i: (layer, i, 0, 0))
    prev, prev_specs, aliases = _alias_inputs(prev, 1)
    return pl.pallas_call(
        _skip_refs(functools.partial(_attn_sample_kernel, bs=bs, t_new=t_new, layer=layer), len(prev)),
        grid=(bsz // bs,),
        in_specs=prev_specs + [
            pl.BlockSpec((rows, ATTN_WIDTH), row),
            pl.BlockSpec((rows, 2 * KV_WIDTH), lambda i: (i, EVEN_KV_BLOCK)),
            cache, cache,
            _const_spec((rows, LANES)), _const_spec((rows, LANES)), _const_spec((rows, LANES)),
            _layer_spec((1, LANES), layer), _layer_spec((1, LANES), layer),
            pl.BlockSpec(memory_space=pltpu.SMEM)],
        out_specs=[pl.BlockSpec((rows, ATTN_WIDTH), row), cache, cache],
        out_shape=[jax.ShapeDtypeStruct((n, ATTN_WIDTH), BF16),
                   jax.ShapeDtypeStruct(cache_k.shape, F32), jax.ShapeDtypeStruct(cache_v.shape, F32)],
        input_output_aliases=aliases,
        scratch_shapes=[pltpu.VMEM((rows, ATTN_WIDTH), F32)],
        compiler_params=_cparams("parallel"),
        name="attn_sample",
    )(*prev, proj, proj, cache_k, cache_v, *tables, gq, gk, sinks)


S5_UCHUNKS = S5_WIDTH // LANES
S5_SUB = S5_FLAT // S5_UCHUNKS
S5_SCHUNKS = S5_FLAT // LANES


def _s5_tail(y, wglu_ref, bglu_ref):
    g = 0.5 * y * (1.0 + lax.erf(y * (2.0 ** -0.5)))
    z = jnp.dot(g.astype(BF16), wglu_ref[...], preferred_element_type=F32) + bglu_ref[...]
    return g * _sigmoid(z)


S5_PARTS = 2


def _s5_prompt_kernel(u_ref, wb_ref, wc_ref, lam_ref, d_ref, wglu_ref, bglu_ref,
                      o_ref, sr_ref, si_ref, xs, hst, *, nbatch, tc):
    rows = nbatch * tc
    prow, ptok = rows // S5_PARTS, tc // S5_PARTS

    @pl.when(pl.program_id(1) == 0)
    def _():
        hst[...] = jnp.zeros_like(hst)

    u = jnp.swapaxes(u_ref[...], 0, 1).reshape(rows, S5_WIDTH)
    ub = u.astype(BF16)

    def in_proj(p, cc):
        rs = slice(p * prow, (p + 1) * prow)
        res = jnp.dot(ub[rs, cc * LANES:(cc + 1) * LANES], wb_ref[cc], preferred_element_type=F32)
        for j in range(S5_SUB // LANES):
            xs[cc * 4 + j, rs, :] = res[:, j * LANES:(j + 1) * LANES]
            xs[S5_SCHUNKS + cc * 4 + j, rs, :] = res[:, S5_SUB + j * LANES:S5_SUB + (j + 1) * LANES]

    ys = {}

    def out_proj(p, cc):
        rs = slice(p * prow, (p + 1) * prow)
        s = jnp.concatenate([xs[cc * 4 + j, rs, :] for j in range(4)]
                            + [xs[S5_SCHUNKS + cc * 4 + j, rs, :] for j in range(4)], axis=1).astype(BF16)
        cols = slice(cc * LANES, (cc + 1) * LANES)
        ys[p, cc] = jnp.dot(s, wc_ref[cc], preferred_element_type=F32) + d_ref[:, cols] * u[rs, cols]

    def tail(p):
        out = _s5_tail(jnp.concatenate([ys.pop((p, cc)) for cc in range(S5_UCHUNKS)], axis=1), wglu_ref, bglu_ref)
        o_ref[:, p * ptok:(p + 1) * ptok, :] = jnp.swapaxes(out.reshape(ptok, nbatch, S5_WIDTH), 0, 1).astype(BF16)

    def scan_step(t, h):
        idx = slice(t * nbatch, (t + 1) * nbatch)
        new = list(h)
        for k in range(S5_SCHUNKS):
            hr, hi = h[k], h[S5_SCHUNKS + k]
            lr, li = lam_ref[k], lam_ref[S5_SCHUNKS + k]
            nr = lr * hr - li * hi + xs[k, idx, :]
            ni = lr * hi + li * hr + xs[S5_SCHUNKS + k, idx, :]
            xs[k, idx, :] = nr
            xs[S5_SCHUNKS + k, idx, :] = ni
            new[k], new[S5_SCHUNKS + k] = nr, ni
        return new

    for cc in range(S5_UCHUNKS):
        in_proj(0, cc)
    h = [hst[k] for k in range(2 * S5_SCHUNKS)]
    for p in range(S5_PARTS):
        work = []
        if p + 1 < S5_PARTS:
            work += [functools.partial(in_proj, p + 1, cc) for cc in range(S5_UCHUNKS)]
        if p >= 1:
            work += [functools.partial(out_proj, p - 1, cc) for cc in range(S5_UCHUNKS)]
            work.append(functools.partial(tail, p - 1))
        every = max(1, ptok // max(1, len(work)))
        for i in range(ptok):
            h = scan_step(p * ptok + i, h)
            if work and (i + 1) % every == 0:
                work.pop(0)()
        for w in work:
            w()
    for cc in range(S5_UCHUNKS):
        out_proj(S5_PARTS - 1, cc)
    tail(S5_PARTS - 1)
    for k in range(2 * S5_SCHUNKS):
        hst[k] = h[k]
    sr_ref[...] = jnp.concatenate(h[:S5_SCHUNKS], axis=1)
    si_ref[...] = jnp.concatenate(h[S5_SCHUNKS:], axis=1)


def _s5_prompt(proj, prm, layer, prev):
    bsz, seq, _ = proj.shape
    nbatch, tc = SUBLANES, S5_CHUNK
    n_layers = prm["wb"].shape[0]
    st = pl.BlockSpec((None, nbatch, S5_FLAT), lambda b, c: (layer, b, 0))
    st_shape = jax.ShapeDtypeStruct((n_layers, bsz, S5_FLAT), F32)
    prev, prev_specs, aliases = _alias_inputs(prev, 1)
    names = ("wb", "wc", "lam8", "d", "wglu", "bglu")
    return pl.pallas_call(
        _skip_refs(functools.partial(_s5_prompt_kernel, nbatch=nbatch, tc=tc), len(prev)),
        grid=(bsz // nbatch, seq // tc),
        in_specs=prev_specs + [pl.BlockSpec((nbatch, tc, S5_WIDTH), lambda b, c: (b, c, EVEN_U_BLOCK))]
        + [_layer_spec(prm[k].shape[1:], layer) for k in names],
        out_specs=[pl.BlockSpec((nbatch, tc, S5_WIDTH), lambda b, c: (b, c, 0)), st, st],
        out_shape=[jax.ShapeDtypeStruct((bsz, seq, S5_WIDTH), BF16), st_shape, st_shape],
        input_output_aliases=aliases,
        scratch_shapes=[pltpu.VMEM((2 * S5_SCHUNKS, nbatch * tc, LANES), F32),
                        pltpu.VMEM((2 * S5_SCHUNKS, nbatch, LANES), F32)],
        compiler_params=_cparams("parallel", "arbitrary"),
        name="s5_prompt",
    )(*prev, proj, *[prm[k] for k in names])


def _s5_sample_kernel(u_ref, wb_ref, wc_ref, lr_ref, li_ref, d_ref, wglu_ref, bglu_ref, h0r_ref, h0i_ref,
                      o_ref, sr_ref, si_ref, xr, xi, *, nseq, t_new):
    nt, st = nseq // SAMPLE_TILE, SAMPLE_TILE
    n = nseq * t_new
    u = u_ref[...]
    ub = u.astype(BF16)
    for cc in range(S5_UCHUNKS):
        res = jnp.dot(ub[:, cc * LANES:(cc + 1) * LANES], wb_ref[cc], preferred_element_type=F32)
        sc = slice(cc * S5_SUB, (cc + 1) * S5_SUB)
        xr[:, :, :, sc] = res[:, :S5_SUB].reshape(nt, t_new, st, S5_SUB)
        xi[:, :, :, sc] = res[:, S5_SUB:].reshape(nt, t_new, st, S5_SUB)
    lr, li = lr_ref[...], li_ref[...]
    hr, hi = h0r_ref[...], h0i_ref[...]
    for t in range(t_new):
        nr = lr * hr - li * hi + xr[:, t].reshape(nseq, S5_FLAT)
        ni = lr * hi + li * hr + xi[:, t].reshape(nseq, S5_FLAT)
        xr[:, t] = nr.reshape(nt, st, S5_FLAT)
        xi[:, t] = ni.reshape(nt, st, S5_FLAT)
        hr, hi = nr, ni
    sr_ref[...] = hr
    si_ref[...] = hi
    ys = []
    for cc in range(S5_UCHUNKS):
        sc = slice(cc * S5_SUB, (cc + 1) * S5_SUB)
        s = jnp.concatenate([xr[:, :, :, sc].reshape(n, S5_SUB), xi[:, :, :, sc].reshape(n, S5_SUB)],
                            axis=1).astype(BF16)
        cols = slice(cc * LANES, (cc + 1) * LANES)
        ys.append(jnp.dot(s, wc_ref[cc], preferred_element_type=F32) + d_ref[:, cols] * u[:, cols])
    o_ref[...] = _s5_tail(jnp.concatenate(ys, axis=1), wglu_ref, bglu_ref).astype(BF16)


def _s5_sample(proj, h0r, h0i, prm, t_new, layer, prev):
    n = proj.shape[0]
    nseq = n // t_new
    names = ("wb", "wc", "lr", "li", "d", "wglu", "bglu")
    st = pl.BlockSpec((None, nseq, S5_FLAT), lambda i: (layer, 0, 0))
    prev, prev_specs, aliases = _alias_inputs(prev, 1)
    scratch = pltpu.VMEM((nseq // SAMPLE_TILE, t_new, SAMPLE_TILE, S5_FLAT), F32)
    return pl.pallas_call(
        _skip_refs(functools.partial(_s5_sample_kernel, nseq=nseq, t_new=t_new), len(prev)),
        grid=(1,),
        in_specs=prev_specs + [pl.BlockSpec((n, S5_WIDTH), lambda i: (0, EVEN_U_BLOCK))]
        + [_layer_spec(prm[k].shape[1:], layer) for k in names]
        + [_layer_spec((nseq, S5_FLAT), layer), _layer_spec((nseq, S5_FLAT), layer)],
        out_specs=[pl.BlockSpec((n, S5_WIDTH), lambda i: (0, 0)), st, st],
        out_shape=[jax.ShapeDtypeStruct((n, S5_WIDTH), BF16),
                   jax.ShapeDtypeStruct(h0r.shape, F32), jax.ShapeDtypeStruct(h0i.shape, F32)],
        input_output_aliases=aliases,
        scratch_shapes=[scratch, scratch],
        compiler_params=_cparams("arbitrary"),
        name="s5_sample",
    )(*prev, proj, *[prm[k] for k in names], h0r, h0i)


def _s5_params(a_re, a_im, log_dt, b_re, b_im, c_re, c_im, d_skip, w_glu, b_glu):
    nl = a_re.shape[0]
    dt = jnp.exp(log_dt)
    mag = jnp.exp(a_re * dt)
    lr, li = mag * jnp.cos(a_im * dt), mag * jnp.sin(a_im * dt)
    den = a_re * a_re + a_im * a_im
    cr = ((lr - 1.0) * a_re + li * a_im) / den
    ci = (li * a_re - (lr - 1.0) * a_im) / den
    bbr = cr[..., None] * b_re - ci[..., None] * b_im
    bbi = cr[..., None] * b_im + ci[..., None] * b_re
    gpc = LANES // S5_GROUP
    eye = jnp.eye(gpc, dtype=F32)

    def in_blocks(bb):
        bb = bb.reshape(nl, S5_UCHUNKS, gpc, S5_STATE, S5_GROUP)
        return jnp.einsum("lcgph,gk->lcghkp", bb, eye).reshape(nl, S5_UCHUNKS, LANES, S5_SUB)

    def out_blocks(cm):
        cm = cm.reshape(nl, S5_UCHUNKS, gpc, S5_GROUP, S5_STATE)
        return jnp.einsum("lcghp,gk->lcgpkh", cm, eye).reshape(nl, S5_UCHUNKS, S5_SUB, LANES)

    wb = jnp.concatenate([in_blocks(bbr), in_blocks(bbi)], axis=3).astype(BF16)
    wc = jnp.concatenate([out_blocks(c_re), -out_blocks(c_im)], axis=2).astype(BF16)
    lr_f, li_f = lr.reshape(nl, 1, S5_FLAT), li.reshape(nl, 1, S5_FLAT)
    lam = jnp.concatenate([lr_f.reshape(nl, S5_SCHUNKS, 1, LANES), li_f.reshape(nl, S5_SCHUNKS, 1, LANES)], axis=1)
    lam8 = jnp.broadcast_to(lam, (nl, 2 * S5_SCHUNKS, SUBLANES, LANES))
    return dict(wb=wb, wc=wc, lam8=lam8, lr=lr_f, li=li_f, d=d_skip.reshape(nl, 1, S5_WIDTH),
                wglu=w_glu.astype(BF16), bglu=b_glu.reshape(nl, 1, S5_WIDTH))


ML_AUG = 2 * ML_DV


def _head_out(h, o, gout):
    hn = h * lax.rsqrt(jnp.mean(h * h, axis=-1, keepdims=True) + EPS) * gout
    return (hn * _sigmoid(o)).astype(BF16)


ODDP_V_BLOCK = 0
ODDP_O_BLOCK = 1
ODDP_Q_BLOCK = (2 * ML_WIDTH) // ML_QK
ODDP_G_BLOCK = (2 * ML_WIDTH + ML_QK) // LANES
ML_SPLIT = 3
ML_PIECE_LANES = 2 * ML_HEADS
ML_SEQS = 4


def _norm_matmul_kt_kernel(x_ref, g_ref, w_ref, o_ref, kt_ref):
    h = _rms(x_ref[...], g_ref[...]).astype(BF16)
    k0, v0, g0 = ML_QK, 2 * ML_QK, 2 * ML_QK + 2 * ML_WIDTH
    o_ref[:, :g0 - v0] = jnp.dot(h, w_ref[:, v0:g0], preferred_element_type=F32)
    o_ref[:, g0 - v0:g0 - v0 + k0] = jnp.dot(h, w_ref[:, :k0], preferred_element_type=F32)
    o_ref[:, g0 - v0 + k0:] = jnp.dot(h, w_ref[:, g0:], preferred_element_type=F32)
    kt = lax.dot_general(w_ref[:, k0:v0], h, (((0,), (1,)), ((), ())), preferred_element_type=F32)
    kt_ref[...] = kt * (ML_DK ** -0.5)


def _norm_matmul_kt(xs, g, layer, w, widx):
    d, m_in = w.shape[1], w.shape[2]
    return _row_groups_call(
        _norm_matmul_kt_kernel, [[x] for x in xs], [g, w],
        [_layer_spec((1, d), layer), _layer_spec((d, m_in), widx)],
        [(m_in - ML_QK, F32, True), (ML_QK, F32, False)], [], "norm_matmul_kt")


def _cummax_rows(x):
    n = x.shape[0]
    row = lax.broadcasted_iota(jnp.int32, x.shape, 0)
    shift = 1
    while shift < n:
        x = jnp.maximum(x, jnp.where(row >= shift, pltpu.roll(x, shift, 0), NEG_INF))
        shift *= 2
    return x


def _pieces(x):
    lane = lax.broadcasted_iota(jnp.int32, x.shape, 1)
    xx = x + pltpu.roll(x, ML_PIECE_LANES, 1) + pltpu.roll(x, 2 * ML_PIECE_LANES, 1)
    a1, a2, a3 = _split3(xx)
    return jnp.where(lane < ML_PIECE_LANES, a1, jnp.where(lane < 2 * ML_PIECE_LANES, a2, a3))


def _ml_select_constants():
    mask = np.zeros((ML_HEADS, LANES), np.float32)
    sel = np.zeros((ML_HEADS, LANES, 2 * ML_DV), np.float32)
    for h in range(ML_HEADS):
        for k in range(ML_SPLIT):
            lo, hi = k * ML_PIECE_LANES + h, k * ML_PIECE_LANES + ML_HEADS + h
            mask[h, lo] = mask[h, hi] = 1.0
            sel[h, lo, :ML_DV] = 1.0
            sel[h, hi, ML_DV:] = 1.0
    return jnp.asarray(mask), jnp.asarray(sel, dtype=BF16)


def _mlstm_prompt_kernel(*refs, tc, nchunks, nseq):
    v_ref, o_ref, q_ref, g_ref = refs[:4]
    kt_refs = refs[4:4 + nseq]
    bias_ref, gout_ref, mask_ref, sel_ref, h_ref, c_ref, n_ref, m_ref, caug, mst = refs[4 + nseq:]
    ci = pl.program_id(1)

    @pl.when(ci == 0)
    def _():
        caug[...] = jnp.zeros_like(caug)
        mst[...] = jnp.zeros_like(mst)

    nh = ML_HEADS
    lane = lax.broadcasted_iota(jnp.int32, (tc, LANES), 1)
    lo, hi = lane < nh, (lane >= nh) & (lane < 2 * nh)
    rt = lax.broadcasted_iota(jnp.int32, (tc, tc), 0)
    cs = lax.broadcasted_iota(jnp.int32, (tc, tc), 1)
    causal = cs <= rt
    tril = jnp.where(causal, 1.0, 0.0).astype(BF16)
    ones = jnp.ones((tc, ML_DV), F32)

    def gates(sq):
        g = g_ref[sq] + bias_ref[...]
        lf = jnp.where(hi, _log_sigmoid(g), 0.0)
        b = sum(jnp.dot(tril, p, preferred_element_type=F32) for p in _split3(lf))
        c = jnp.where(hi, pltpu.roll(g, nh, 1) - b, 0.0)
        m_prev = mst[sq]
        mx = jnp.maximum(_cummax_rows(c), m_prev)
        m_row = b + mx
        mx_lo = pltpu.roll(mx, LANES - nh, 1)
        w_inter = jnp.exp(pltpu.roll(m_prev, LANES - nh, 1) - mx_lo)
        mst[sq] = m_row[tc - 1:tc, :]
        return dict(xc=_pieces(jnp.where(lo, w_inter, jnp.where(hi, jnp.exp(-m_row), 0.0))),
                    lc=_pieces(jnp.where(lo, -mx_lo, jnp.where(hi, 1.0, 0.0))),
                    rc=_pieces(jnp.where(lo, 1.0, jnp.where(hi, c, 0.0))))

    gt = [gates(sq) for sq in range(nseq)]
    units = [(sq, hd) for hd in range(nh) for sq in range(nseq)]
    st = [dict() for _ in units]

    def stage1(u):
        sq, hd = units[u]
        d = st[u]
        rh = gt[sq]["rc"] * mask_ref[hd:hd + 1, :].astype(BF16)
        d["dmat"] = lax.dot_general(gt[sq]["lc"], rh, (((1,), (1,)), ((), ())), preferred_element_type=F32)
        d["wb"] = jnp.dot(gt[sq]["xc"], sel_ref[hd], preferred_element_type=F32)
        d["qh"] = q_ref[sq, :, hd * ML_DK:(hd + 1) * ML_DK]
        d["kt"] = kt_refs[sq][hd * ML_DK:(hd + 1) * ML_DK, :]
        d["qk"] = jnp.dot(d["qh"].astype(BF16), d["kt"].astype(BF16), preferred_element_type=F32)

    def stage2(u):
        sq, hd = units[u]
        d = st[u]
        cols = slice(hd * ML_DV, (hd + 1) * ML_DV)
        d["w"] = jnp.exp(jnp.where(causal, d["dmat"], NEG_INF))
        d["vaug"] = jnp.concatenate([v_ref[sq, :, cols], ones], axis=1).astype(BF16)
        d["cm"] = caug[sq, hd]
        lhs = jnp.concatenate([(d["qk"] * d["w"]).astype(BF16), (d["wb"][:, :ML_DK] * d["qh"]).astype(BF16)], axis=1)
        rhs = jnp.concatenate([d["vaug"], d["cm"].astype(BF16)], axis=0)
        d["both"] = jnp.dot(lhs, rhs, preferred_element_type=F32)
        kw = (d["kt"] * d["w"][tc - 1:tc, :]).astype(BF16)
        d["upd"] = jnp.dot(kw, d["vaug"], preferred_element_type=F32)

    def stage3(u):
        sq, hd = units[u]
        d = st[u]
        cols = slice(hd * ML_DV, (hd + 1) * ML_DV)
        both, wb = d["both"], d["wb"]
        h = both[:, :ML_DV] / jnp.maximum(jnp.abs(both[:, ML_DV:]), wb[:, ML_DV:])
        h_ref[sq, :, cols] = _head_out(h, o_ref[sq, :, cols], gout_ref[:, cols])
        decay = wb[tc - 1:tc, :ML_DV]
        caug[sq, hd] = jnp.concatenate([decay, decay], axis=1) * d["cm"] + d["upd"]
        d.clear()

    for step in range(len(units) + 2):
        if step < len(units):
            stage1(step)
        if 0 <= step - 1 < len(units):
            stage2(step - 1)
        if 0 <= step - 2 < len(units):
            stage3(step - 2)

    @pl.when(ci == nchunks - 1)
    def _():
        c_ref[...] = caug[:, :, :, :ML_DV]
        n_ref[...] = caug[:, :, :, ML_DV:]
        m_ref[...] = mst[...]


def _kt_index(b, c, *, sq, nseq, nchunks):
    return 0, (b * nseq + sq) * nchunks + c


def _mlstm_prompt(proj, kt, bias, gout, consts, layer, prev):
    bsz, seq, _ = proj.shape
    tc, nseq = ML_CHUNK, ML_SEQS
    nchunks = seq // tc
    n_layers = bias.shape[0]
    mask, sel = consts
    blk = lambda w, j: pl.BlockSpec((nseq, tc, w), lambda b, c: (b, c, j))
    st = lambda shape: pl.BlockSpec((None, nseq) + shape, lambda b, c: (layer, b) + (0,) * len(shape))
    st_shape = lambda shape: jax.ShapeDtypeStruct((n_layers, bsz) + shape, F32)
    prev, prev_specs, aliases = _alias_inputs(prev, 1)
    return pl.pallas_call(
        _skip_refs(functools.partial(_mlstm_prompt_kernel, tc=tc, nchunks=nchunks, nseq=nseq), len(prev)),
        grid=(bsz // nseq, nchunks),
        in_specs=prev_specs + [
            blk(ML_WIDTH, ODDP_V_BLOCK), blk(ML_WIDTH, ODDP_O_BLOCK), blk(ML_QK, ODDP_Q_BLOCK),
            blk(LANES, ODDP_G_BLOCK)]
        + [pl.BlockSpec((ML_QK, tc), functools.partial(_kt_index, sq=sq, nseq=nseq, nchunks=nchunks))
           for sq in range(nseq)] + [
            _layer_spec((1, LANES), layer), _layer_spec((1, ML_WIDTH), layer),
            _const_spec(mask.shape), _const_spec(sel.shape)],
        out_specs=[blk(ML_WIDTH, 0), st((ML_HEADS, ML_DK, ML_DV)), st((ML_HEADS, ML_DK, ML_DV)), st((1, LANES))],
        out_shape=[jax.ShapeDtypeStruct((bsz, seq, ML_WIDTH), BF16),
                   st_shape((ML_HEADS, ML_DK, ML_DV)), st_shape((ML_HEADS, ML_DK, ML_DV)), st_shape((1, LANES))],
        input_output_aliases=aliases,
        scratch_shapes=[pltpu.VMEM((nseq, ML_HEADS, ML_DK, ML_AUG), F32), pltpu.VMEM((nseq, 1, LANES), F32)],
        compiler_params=_cparams("parallel", "arbitrary"),
        name="mlstm_prompt",
    )(*prev, proj, proj, proj, proj, *([kt] * nseq), bias, gout, mask, sel)


MLS_SEQS = 32


def _mlstm_sample_kernel(v_ref, o_ref, q_ref, g_ref, kt_ref, bias_ref, gout_ref, mask_ref, sel_ref,
                         c0_ref, n0_ref, m0_ref, h_ref, c_ref, n_ref, m_ref, *, nseq, t_new):
    nh, nt, st = ML_HEADS, nseq // SAMPLE_TILE, SAMPLE_TILE
    rows = nseq * t_new
    lane = lax.broadcasted_iota(jnp.int32, (rows, LANES), 1)
    lo, hi = lane < nh, (lane >= nh) & (lane < 2 * nh)
    tiles = lambda a: a.reshape(nt, t_new, st, a.shape[-1])
    flat = lambda a: a.reshape(rows, a.shape[-1])
    per_seq = lambda a: a.reshape(nseq, a.shape[-1])

    g = g_ref[...] + bias_ref[...]
    lf = tiles(jnp.where(hi, _log_sigmoid(g), 0.0))
    ig = tiles(jnp.where(hi, pltpu.roll(g, nh, 1), 0.0))
    m_prev = m0_ref[...].reshape(nt, st, LANES)
    bs, cs, ms = [], [], []
    b_run, m_run = None, m_prev
    for t in range(t_new):
        b_run = lf[:, t] if b_run is None else b_run + lf[:, t]
        c_t = ig[:, t] - b_run
        m_run = jnp.maximum(m_run, c_t)
        bs.append(b_run)
        cs.append(c_t)
        ms.append(m_run)
    stack = lambda xs: flat(jnp.stack(xs, axis=1))
    b, c, mx = stack(bs), stack(cs), stack(ms)
    m_prev_rows = stack([m_prev] * t_new)
    m_row = b + mx
    m_ref[...] = per_seq(bs[-1] + ms[-1])
    mx_lo = pltpu.roll(mx, LANES - nh, 1)
    w_inter = jnp.exp(pltpu.roll(m_prev_rows, LANES - nh, 1) - mx_lo)
    xc = _pieces(jnp.where(lo, w_inter, jnp.where(hi, jnp.exp(-m_row), 0.0)))
    lc = _pieces(jnp.where(lo, -mx_lo, jnp.where(hi, 1.0, 0.0)))
    rc = _pieces(jnp.where(lo, 1.0, jnp.where(hi, c, 0.0)))

    def seq_of(idx):
        return (idx // (t_new * st)) * st + idx % st, (idx % (t_new * st)) // st

    rt = lax.broadcasted_iota(jnp.int32, (rows, rows), 0)
    ct = lax.broadcasted_iota(jnp.int32, (rows, rows), 1)
    (rs, rtok), (cseq, ctok) = seq_of(rt), seq_of(ct)
    valid = (rs == cseq) & (ctok <= rtok)
    rq = lax.broadcasted_iota(jnp.int32, (rows, nseq * ML_DK), 0)
    cq = lax.broadcasted_iota(jnp.int32, (rows, nseq * ML_DK), 1)
    own_q = seq_of(rq)[0] == cq // ML_DK
    rk = lax.broadcasted_iota(jnp.int32, (nseq * ML_DK, rows), 0)
    ck = lax.broadcasted_iota(jnp.int32, (nseq * ML_DK, rows), 1)
    own_k = rk // ML_DK == seq_of(ck)[0]
    ones = jnp.ones((rows, ML_DV), F32)
    last = lambda a: per_seq(tiles(a)[:, t_new - 1])

    for hd in range(nh):
        cols = slice(hd * ML_DV, (hd + 1) * ML_DV)
        rh = rc * mask_ref[hd:hd + 1, :].astype(BF16)
        dmat = lax.dot_general(lc, rh, (((1,), (1,)), ((), ())), preferred_element_type=F32)
        wb = jnp.dot(xc, sel_ref[hd], preferred_element_type=F32)
        qh = q_ref[:, hd * ML_DK:(hd + 1) * ML_DK]
        kt = kt_ref[hd * ML_DK:(hd + 1) * ML_DK, :]
        ktb = kt.astype(BF16)
        w = jnp.exp(jnp.where(valid, dmat, NEG_INF))
        qk = jnp.dot(qh.astype(BF16), ktb, preferred_element_type=F32) * w
        vaug = jnp.concatenate([v_ref[:, cols], ones], axis=1).astype(BF16)
        po = jnp.dot(qk.astype(BF16), vaug, preferred_element_type=F32)
        wq = wb[:, :ML_DK] * qh
        wq2 = jnp.concatenate([wq, wq], axis=1)
        wq_bd = jnp.where(own_q, jnp.concatenate([wq2] * (nseq * ML_DK // LANES), axis=1), 0.0).astype(BF16)
        cstack = c0_ref[:, hd].reshape(nseq * ML_DK, ML_DV)
        num = po[:, :ML_DV] + jnp.dot(wq_bd, cstack.astype(BF16), preferred_element_type=F32)
        n0 = n0_ref[hd]
        n_rows = stack([n0.reshape(nt, st, ML_DK)] * t_new)
        den = po[:, ML_DV:] + jnp.sum(wq * n_rows, axis=-1, keepdims=True)
        h = num / jnp.maximum(jnp.abs(den), wb[:, ML_DV:])
        h_ref[:, cols] = _head_out(h, o_ref[:, cols], gout_ref[:, cols])
        w_last = last(w)
        decay = last(wb[:, :ML_DV])
        n_upd = lax.dot_general(w_last.astype(BF16), ktb, (((1,), (1,)), ((), ())), preferred_element_type=F32)
        n_ref[hd] = decay[:, :ML_DK] * n0 + n_upd
        wk = jnp.sum(w_last, axis=0, keepdims=True)
        kw_bd = jnp.where(own_k, jnp.concatenate([kt * wk] * nseq, axis=0), 0.0).astype(BF16)
        upd = jnp.dot(kw_bd, v_ref[:, cols].astype(BF16), preferred_element_type=F32)
        decay_rows = jnp.broadcast_to(decay[:, None, :], (nseq, ML_DK, ML_DV)).reshape(nseq * ML_DK, ML_DV)
        c_ref[:, hd] = (decay_rows * cstack + upd).reshape(nseq, ML_DK, ML_DV)


def _mlstm_sample(proj, kt, bias, gout, consts, c0, n0h, m0, t_new, layer, prev):
    n = proj.shape[0]
    bsz = n // t_new
    nseq = MLS_SEQS
    rows = nseq * t_new
    mask, sel = consts
    blk = lambda w, j: pl.BlockSpec((rows, w), lambda i: (i, j))
    cst = pl.BlockSpec((None, nseq, ML_HEADS, ML_DK, ML_DV), lambda i: (layer, i, 0, 0, 0))
    nst = pl.BlockSpec((None, ML_HEADS, nseq, ML_DK), lambda i: (layer, 0, i, 0))
    mst = pl.BlockSpec((None, nseq, LANES), lambda i: (layer, i, 0))
    prev, prev_specs, aliases = _alias_inputs(prev, 1)
    return pl.pallas_call(
        _skip_refs(functools.partial(_mlstm_sample_kernel, nseq=nseq, t_new=t_new), len(prev)),
        grid=(bsz // nseq,),
        in_specs=prev_specs + [
            blk(ML_WIDTH, ODDP_V_BLOCK), blk(ML_WIDTH, ODDP_O_BLOCK), blk(ML_QK, ODDP_Q_BLOCK),
            blk(LANES, ODDP_G_BLOCK), pl.BlockSpec((ML_QK, rows), lambda i: (0, i)),
            _layer_spec((1, LANES), layer), _layer_spec((1, ML_WIDTH), layer),
            _const_spec(mask.shape), _const_spec(sel.shape), cst, nst, mst],
        out_specs=[blk(ML_WIDTH, 0), cst, nst, mst],
        out_shape=[jax.ShapeDtypeStruct((n, ML_WIDTH), BF16),
                   jax.ShapeDtypeStruct(c0.shape, F32), jax.ShapeDtypeStruct(n0h.shape, F32),
                   jax.ShapeDtypeStruct(m0.shape, F32)],
        input_output_aliases=aliases,
        compiler_params=_cparams("parallel"),
        name="mlstm_sample",
    )(*prev, proj, proj, proj, proj, kt, bias, gout, mask, sel, c0, n0h, m0)


def _pad_lanes(x):
    return jnp.pad(x, [(0, 0)] * (x.ndim - 1) + [(0, LANES - x.shape[-1])])


def kernel(x_prompt, x_sample, cache_k, cache_v, state_ssm_re, state_ssm_im, state_mlstm_c, state_mlstm_n, state_mlstm_m, norm_mix, norm_ffn, w_in_even, q_norm, k_norm, attn_sinks, s5_a_re, s5_a_im, s5_log_dt, s5_b_re, s5_b_im, s5_c_re, s5_c_im, s5_d, s5_w_glu, s5_b_glu, w_out_even, w_in_odd, ml_b_i, ml_b_f, ml_out_norm, w_out_odd, w_gate, w_up, w_down):
    bp, lp, _ = x_prompt.shape
    bsm, ls, _ = x_sample.shape
    yp = x_prompt.reshape(bp * lp, D_MODEL)
    ys = x_sample.reshape(bsm // SAMPLE_TILE, SAMPLE_TILE, ls, D_MODEL).transpose(0, 2, 1, 3).reshape(bsm * ls, D_MODEL)
    tab_p = _rope_tables(jnp.arange(lp))
    tab_s = tuple(jnp.repeat(t, SAMPLE_TILE, axis=0) for t in _rope_tables(PAST_LEN + jnp.arange(ls)))
    n_even, n_odd = w_in_even.shape[0], w_in_odd.shape[0]

    g_mix = norm_mix.reshape(DEPTH, 1, D_MODEL)
    g_ffn = norm_ffn.reshape(DEPTH, 1, D_MODEL)
    wg, wu, wd = w_gate.astype(BF16), w_up.astype(BF16), w_down.astype(BF16)
    kv0, u0 = ATTN_WIDTH, ATTN_WIDTH + 2 * KV_WIDTH
    order = jnp.asarray(ATTN_HEAD_ORDER)
    wq = w_in_even[..., :kv0].reshape(n_even, D_MODEL, ATTN_HEADS, HEAD_DIM)[:, :, order].reshape(n_even, D_MODEL, kv0)
    w_in_e = jnp.concatenate([wq, w_in_even[..., u0:], w_in_even[..., kv0:u0]], axis=-1).astype(BF16)
    wo_attn = w_out_even[:, :kv0].reshape(n_even, ATTN_HEADS, HEAD_DIM, D_MODEL)[:, order].reshape(n_even, kv0, D_MODEL)
    w_out_e = jnp.concatenate([wo_attn, w_out_even[:, kv0:]], axis=1).astype(BF16)
    gq = jnp.tile(q_norm, (1, LANES // HEAD_DIM)).reshape(n_even, 1, LANES)
    gk = jnp.tile(k_norm, (1, LANES // HEAD_DIM)).reshape(n_even, 1, LANES)
    prm = _s5_params(s5_a_re, s5_a_im, s5_log_dt, s5_b_re, s5_b_im, s5_c_re, s5_c_im, s5_d, s5_w_glu, s5_b_glu)
    w_in_o = jnp.pad(w_in_odd, ((0, 0), (0, 0), (0, ODD_IN_PAD - ODD_IN))).astype(BF16)
    ml_consts = _ml_select_constants()
    w_out_o = w_out_odd.astype(BF16)
    ml_bias = _pad_lanes(jnp.concatenate([ml_b_i, ml_b_f], axis=-1)).reshape(n_odd, 1, LANES)
    ml_gout = ml_out_norm.reshape(n_odd, 1, ML_WIDTH)
    ck = cache_k.reshape(n_even, bsm, WINDOW, KV_WIDTH)
    cv = cache_v.reshape(n_even, bsm, WINDOW, KV_WIDTH)
    h0r = state_ssm_re.reshape(n_even, bsm, S5_FLAT)
    h0i = state_ssm_im.reshape(n_even, bsm, S5_FLAT)
    n0h = jnp.swapaxes(state_mlstm_n, 1, 2)
    m0 = jnp.pad(state_mlstm_m, ((0, 0), (0, 0), (ML_HEADS, LANES - 2 * ML_HEADS)))

    p_attn = p_ssm = p_ml = s_attn = s_ssm = s_ml = None
    for layer in range(DEPTH):
        ffn = (layer, g_ffn, wg, wu, wd)
        if layer % 2 == 0:
            e = layer // 2
            proj_p, proj_s = _norm_matmul([yp, ys], g_mix, layer, w_in_e, e)
            proj3 = proj_p.reshape(bp, lp, -1)
            attn_p, *p_attn = _attn_prompt(proj3, tab_p, gq, gk, attn_sinks, e, p_attn)
            ssm_p, *p_ssm = _s5_prompt(proj3, prm, e, p_ssm)
            attn_s, *s_attn = _attn_sample(proj_s, ck, cv, tab_s, gq, gk, attn_sinks, ls, e, s_attn)
            ssm_s, *s_ssm = _s5_sample(proj_s, h0r, h0i, prm, ls, e, s_ssm)
            yp, ys = _mix_ffn([[yp, attn_p.reshape(bp * lp, -1), ssm_p.reshape(bp * lp, -1)], [ys, attn_s, ssm_s]],
                              w_out_e, e, *ffn)
        else:
            o = layer // 2
            (proj_p, kt_p), (proj_s, kt_s) = _norm_matmul_kt([yp, ys], g_mix, layer, w_in_o, o)
            hh_p, *p_ml = _mlstm_prompt(proj_p.reshape(bp, lp, -1), kt_p, ml_bias, ml_gout, ml_consts, o, p_ml)
            hh_s, *s_ml = _mlstm_sample(proj_s, kt_s, ml_bias, ml_gout, ml_consts, state_mlstm_c, n0h, m0, ls, o,
                                        s_ml)
            yp, ys = _mix_ffn([[yp, hh_p.reshape(bp * lp, -1)], [ys, hh_s]], w_out_o, o, *ffn)
    heads = lambda a: a.reshape(a.shape[:3] + (KV_HEADS, HEAD_DIM))
    groups = lambda a: a.reshape(a.shape[:2] + (S5_GROUPS, S5_STATE))
    ys = ys.reshape(bsm // SAMPLE_TILE, ls, SAMPLE_TILE, D_MODEL).transpose(0, 2, 1, 3).reshape(bsm, ls, D_MODEL)
    return (yp.reshape(bp, lp, D_MODEL), ys,
            heads(p_attn[0]), heads(p_attn[1]), groups(p_ssm[0]), groups(p_ssm[1]),
            p_ml[0], p_ml[1][..., 0], p_ml[2][:, :, 0, ML_HEADS:2 * ML_HEADS],
            heads(s_attn[0]), heads(s_attn[1]), groups(s_ssm[0]), groups(s_ssm[1]),
            s_ml[0], jnp.swapaxes(s_ml[1], 1, 2), s_ml[2][..., ML_HEADS:2 * ML_HEADS])
```

```python
import functools

import numpy as np

import jax
import jax.numpy as jnp
from jax import lax
from jax.experimental import pallas as pl
from jax.experimental.pallas import tpu as pltpu

F32 = jnp.float32
BF16 = jnp.bfloat16

D_MODEL = 1024
DEPTH = 4
PAST_LEN = 8192
WINDOW = 128
ATTN_HEADS = 8
KV_HEADS = 2
HEAD_DIM = 64
ATTN_WIDTH = ATTN_HEADS * HEAD_DIM
KV_WIDTH = KV_HEADS * HEAD_DIM
ROT_DIM = HEAD_DIM // 4
ROPE_THETA = 500000.0
S5_GROUP = 16
S5_WIDTH = D_MODEL // 2
S5_GROUPS = S5_WIDTH // S5_GROUP
S5_STATE = 64
S5_FLAT = S5_GROUPS * S5_STATE
ML_HEADS = 8
ML_DV = D_MODEL // ML_HEADS
ML_DK = ML_DV // 2
ML_QK = ML_HEADS * ML_DK
ML_WIDTH = ML_HEADS * ML_DV
ODD_IN = 2 * ML_QK + 2 * ML_WIDTH + 2 * ML_HEADS
ODD_IN_PAD = 2 * ML_QK + 2 * ML_WIDTH + 128
D_FF = 2816
EPS = 1e-6

LANES = 128
SUBLANES = 8
ROW_TILE = 512
FF_TILE = 256
S5_CHUNK = 64
ML_CHUNK = 128
SAMPLE_TILE = SUBLANES
VMEM_LIMIT = 56 * 1024 * 1024

NEG_INF = float("-inf")


def _cparams(*sem):
    return pltpu.CompilerParams(dimension_semantics=sem, vmem_limit_bytes=VMEM_LIMIT)


def _const_spec(shape):
    zeros = (0,) * len(shape)
    return pl.BlockSpec(shape, lambda *_: zeros, pipeline_mode=pl.Buffered(1))


def _layer_spec(shape, layer):
    zeros = (0,) * len(shape)
    return pl.BlockSpec((None,) + tuple(shape), lambda *_: (layer,) + zeros, pipeline_mode=pl.Buffered(1))


def _skip_refs(body, n_skip):
    if n_skip == 0:
        return body

    def wrapped(*refs):
        return body(*refs[n_skip:])

    return wrapped


def _alias_inputs(prev, first_state_out):
    prev = () if prev is None else tuple(prev)
    specs = [pl.BlockSpec(memory_space=pl.ANY) for _ in prev]
    aliases = {i: first_state_out + i for i in range(len(prev))}
    return prev, specs, aliases


def _rms(x, g):
    ms = jnp.mean(x * x, axis=-1, keepdims=True)
    return x * lax.rsqrt(ms + EPS) * g


def _split3(a):
    a1 = a.astype(BF16)
    r1 = a - a1.astype(F32)
    a2 = r1.astype(BF16)
    a3 = (r1 - a2.astype(F32)).astype(BF16)
    return a1, a2, a3


def _log_sigmoid(x):
    return jnp.minimum(x, 0.0) - jnp.log(1.0 + jnp.exp(-jnp.abs(x)))


def _sigmoid(x):
    return 1.0 / (1.0 + jnp.exp(-x))


def _norm_matmul_kernel(x_ref, g_ref, w_ref, o_ref):
    h = _rms(x_ref[...], g_ref[...]).astype(BF16)
    o_ref[...] = jnp.dot(h, w_ref[...], preferred_element_type=F32)


def _row_groups_call(body, groups, consts, const_specs, out_defs, scratch_shapes, name):
    steps, tiles = [], []
    for arrays in groups:
        n = arrays[0].shape[0]
        tm = min(ROW_TILE, n)
        tiles.append(tm)
        steps.append(n // tm)
    offs = [sum(steps[:k]) for k in range(len(groups))]

    def local(k):
        return lambda i: jnp.clip(i - offs[k], 0, steps[k] - 1)

    in_specs, out_specs, out_shape, args = [], [], [], []
    for k, arrays in enumerate(groups):
        for a in arrays:
            in_specs.append(pl.BlockSpec((tiles[k], a.shape[1]), lambda i, f=local(k): (f(i), 0)))
            args.append(a)
    for k, arrays in enumerate(groups):
        n = arrays[0].shape[0]
        for width, dtype, by_rows in out_defs:
            if by_rows:
                out_specs.append(pl.BlockSpec((tiles[k], width), lambda i, f=local(k): (f(i), 0)))
                out_shape.append(jax.ShapeDtypeStruct((n, width), dtype))
            else:
                out_specs.append(pl.BlockSpec((width, tiles[k]), lambda i, f=local(k): (0, f(i))))
                out_shape.append(jax.ShapeDtypeStruct((width, n), dtype))
    n_in = [len(arrays) for arrays in groups]
    n_out = len(out_defs)

    def kern(*refs):
        i = pl.program_id(0)
        pos = 0
        ins = []
        for cnt in n_in:
            ins.append(refs[pos:pos + cnt])
            pos += cnt
        crefs = refs[pos:pos + len(consts)]
        pos += len(consts)
        outs = [refs[pos + k * n_out:pos + (k + 1) * n_out] for k in range(len(groups))]
        scratch = refs[pos + len(groups) * n_out:]
        for k in range(len(groups)):
            @pl.when((i >= offs[k]) & (i < offs[k] + steps[k]))
            def _(k=k):
                body(*ins[k], *crefs, *outs[k], *scratch)

    res = pl.pallas_call(
        kern,
        grid=(sum(steps),),
        in_specs=in_specs + list(const_specs),
        out_specs=out_specs,
        out_shape=out_shape,
        scratch_shapes=scratch_shapes,
        compiler_params=_cparams("arbitrary"),
        name=name,
    )(*args, *consts)
    return [res[k * n_out:(k + 1) * n_out] for k in range(len(groups))]


def _norm_matmul(xs, g, layer, w, widx):
    d, m = w.shape[1], w.shape[2]
    res = _row_groups_call(_norm_matmul_kernel, [[x] for x in xs], [g, w],
                           [_layer_spec((1, d), layer), _layer_spec((d, m), widx)],
                           [(m, F32, True)], [], "norm_matmul")
    return [r[0] for r in res]


def _mix_ffn_kernel(*refs, n_mix):
    x_ref = refs[0]
    a_refs = refs[1:1 + n_mix]
    wo_ref, g_ref, wg_ref, wu_ref, wd_ref, o_ref, act_ref = refs[1 + n_mix:]
    y = x_ref[...]
    off = 0
    for a_ref in a_refs:
        ka = a_ref.shape[1]
        y = y + jnp.dot(a_ref[...], wo_ref[off:off + ka, :], preferred_element_type=F32)
        off += ka
    h = _rms(y, g_ref[...]).astype(BF16)
    for f in range(D_FF // FF_TILE):
        cols = slice(f * FF_TILE, (f + 1) * FF_TILE)
        gate = jnp.dot(h, wg_ref[:, cols], preferred_element_type=F32)
        up = jnp.dot(h, wu_ref[:, cols], preferred_element_type=F32)
        act_ref[:, cols] = (gate * _sigmoid(gate) * up).astype(BF16)
    o_ref[...] = y + jnp.dot(act_ref[...], wd_ref[...], preferred_element_type=F32)


def _mix_ffn(groups, w_out, oidx, layer, g_ffn, wg, wu, wd):
    d = w_out.shape[2]
    n_mix = len(groups[0]) - 1
    tm = min(ROW_TILE, max(g[0].shape[0] for g in groups))
    res = _row_groups_call(
        functools.partial(_mix_ffn_kernel, n_mix=n_mix), groups, [w_out, g_ffn, wg, wu, wd],
        [_layer_spec(w_out.shape[1:], oidx), _layer_spec((1, d), layer), _layer_spec(wg.shape[1:], layer),
         _layer_spec(wu.shape[1:], layer), _layer_spec(wd.shape[1:], layer)],
        [(d, F32, True)], [pltpu.VMEM((tm, D_FF), BF16)], "mix_ffn")
    return [r[0] for r in res]


def _head_ones():
    r = lax.broadcasted_iota(jnp.int32, (LANES, LANES), 0) // HEAD_DIM
    c = lax.broadcasted_iota(jnp.int32, (LANES, LANES), 1) // HEAD_DIM
    return jnp.where(r == c, 1.0, 0.0).astype(BF16)


def _qk_prep(x, g, ones, ct, sa, sb):
    x2 = x * x
    hi = x2.astype(BF16)
    lo = (x2 - hi.astype(F32)).astype(BF16)
    ss = jnp.dot(hi, ones, preferred_element_type=F32) + jnp.dot(lo, ones, preferred_element_type=F32)
    xn = x * lax.rsqrt(ss * (1.0 / HEAD_DIM) + EPS) * g
    return xn * ct + pltpu.roll(xn, LANES - ROT_DIM // 2, 1) * sa + pltpu.roll(xn, ROT_DIM // 2, 1) * sb


def _rope_tables(pos):
    half = ROT_DIM // 2
    inv = jnp.power(jnp.float32(ROPE_THETA), -jnp.arange(half, dtype=F32) / half)
    ang = pos.astype(F32)[:, None] * inv[None, :]
    cos, sin = jnp.cos(ang), jnp.sin(ang)
    n = pos.shape[0]
    one = jnp.ones((n, HEAD_DIM - ROT_DIM), F32)
    zero = jnp.zeros((n, HEAD_DIM - ROT_DIM), F32)
    z8 = jnp.zeros((n, half), F32)
    ct = jnp.concatenate([cos, cos, one], axis=1)
    sa = jnp.concatenate([-sin, z8, zero], axis=1)
    sb = jnp.concatenate([z8, sin, zero], axis=1)
    tile = lambda t: jnp.concatenate([t, t], axis=1)
    return tile(ct), tile(sa), tile(sb)


ATTN_SEQS = 4
ATTN_QCHUNKS = ATTN_WIDTH // LANES
ATTN_HEAD_ORDER = tuple(h * ATTN_QCHUNKS + j for j in range(ATTN_QCHUNKS) for h in range(KV_HEADS))


def _attn_prompt_kernel(q_ref, kv_ref, ct_ref, sa_ref, sb_ref, gq_ref, gk_ref, sink_ref,
                        o_ref, pk_ref, pv_ref, kprev, vprev, *, nb, layer, nseq):
    i = pl.program_id(1)

    @pl.when(i == 0)
    def _():
        kprev[...] = jnp.zeros_like(kprev)
        vprev[...] = jnp.zeros_like(vprev)

    ones = _head_ones()
    ct, sa, sb = ct_ref[...], sa_ref[...], sb_ref[...]
    r = lax.broadcasted_iota(jnp.int32, (WINDOW, 2 * WINDOW), 0)
    c = lax.broadcasted_iota(jnp.int32, (WINDOW, 2 * WINDOW), 1)
    rel = r + WINDOW - c
    mask = (rel >= 0) & (rel <= WINDOW) & ((c >= WINDOW) | (i > 0))
    lane = lax.broadcasted_iota(jnp.int32, (WINDOW, LANES), 1)
    group0 = lane < HEAD_DIM
    v_ones = jnp.ones((2 * WINDOW, LANES), BF16)
    nq = ATTN_QCHUNKS
    st = [dict() for _ in range(nseq)]

    def prep(sq):
        d = st[sq]
        kv = kv_ref[sq]
        d["kn"] = _qk_prep(kv[:, :KV_WIDTH], gk_ref[...], ones, ct, sa, sb)
        d["v"] = kv[:, KV_WIDTH:]
        d["qn"] = [_qk_prep(q_ref[sq, :, j * LANES:(j + 1) * LANES], gq_ref[...], ones, ct, sa, sb)
                   * (HEAD_DIM ** -0.5) for j in range(nq)]
        d["kcat"] = jnp.concatenate([kprev[sq], d["kn"]], axis=0).astype(BF16)
        d["vaug"] = jnp.concatenate([jnp.concatenate([vprev[sq], d["v"]], axis=0).astype(BF16), v_ones], axis=1)
        kprev[sq] = d["kn"]
        vprev[sq] = d["v"]

    def scores(sq, h):
        d = st[sq]
        keep = group0 if h == 0 else jnp.logical_not(group0)
        qs = jnp.concatenate([jnp.where(keep, qj, 0.0) for qj in d["qn"]], axis=0).astype(BF16)
        d["s", h] = lax.dot_general(qs, d["kcat"], (((1,), (1,)), ((), ())), preferred_element_type=F32)

    def softmax_pv(sq, h):
        d = st[sq]
        s = d.pop(("s", h))
        ps, corr = [], []
        for j in range(nq):
            sg = jnp.where(mask, s[j * WINDOW:(j + 1) * WINDOW], NEG_INF)
            sink = sink_ref[layer, h * nq + j]
            m = jnp.maximum(jnp.max(sg, axis=-1, keepdims=True), sink)
            ps.append(jnp.exp(sg - m).astype(BF16))
            corr.append(jnp.exp(sink - m))
        o = jnp.dot(jnp.concatenate(ps, axis=0), d["vaug"], preferred_element_type=F32)
        d["o", h] = [o[j * WINDOW:(j + 1) * WINDOW, :LANES] / (o[j * WINDOW:(j + 1) * WINDOW, LANES:] + corr[j])
                     for j in range(nq)]

    def finish(sq):
        d = st[sq]
        o_ref[sq] = jnp.concatenate([jnp.where(group0, d["o", 0][j], d["o", 1][j]) for j in range(nq)],
                                    axis=1).astype(BF16)

    for sq in range(nseq):
        prep(sq)
    for sq in range(nseq):
        scores(sq, 0)
        scores(sq, 1)
    for sq in range(nseq):
        softmax_pv(sq, 0)
        softmax_pv(sq, 1)
        finish(sq)

    @pl.when(i == nb - 1)
    def _():
        for sq in range(nseq):
            pk_ref[sq] = st[sq]["kn"]
            pv_ref[sq] = st[sq]["v"]


def _attn_prompt(proj, tables, gq, gk, sinks, layer, prev):
    bsz, seq, _ = proj.shape
    nb = seq // WINDOW
    nseq = ATTN_SEQS
    n_layers = gq.shape[0]
    tab = pl.BlockSpec((WINDOW, LANES), lambda b, i: (i, 0))
    prev, prev_specs, aliases = _alias_inputs(prev, 1)
    win = pl.BlockSpec((None, nseq, WINDOW, KV_WIDTH), lambda b, i: (layer, b, 0, 0))
    win_shape = jax.ShapeDtypeStruct((n_layers, bsz, WINDOW, KV_WIDTH), F32)
    return pl.pallas_call(
        _skip_refs(functools.partial(_attn_prompt_kernel, nb=nb, layer=layer, nseq=nseq), len(prev)),
        grid=(bsz // nseq, nb),
        in_specs=prev_specs + [
            pl.BlockSpec((nseq, WINDOW, ATTN_WIDTH), lambda b, i: (b, i, 0)),
            pl.BlockSpec((nseq, WINDOW, 2 * KV_WIDTH), lambda b, i: (b, i, EVEN_KV_BLOCK)),
            tab, tab, tab, _layer_spec((1, LANES), layer), _layer_spec((1, LANES), layer),
            pl.BlockSpec(memory_space=pltpu.SMEM)],
        out_specs=[pl.BlockSpec((nseq, WINDOW, ATTN_WIDTH), lambda b, i: (b, i, 0)), win, win],
        out_shape=[jax.ShapeDtypeStruct((bsz, seq, ATTN_WIDTH), BF16), win_shape, win_shape],
        input_output_aliases=aliases,
        scratch_shapes=[pltpu.VMEM((nseq, WINDOW, KV_WIDTH), F32), pltpu.VMEM((nseq, WINDOW, KV_WIDTH), F32)],
        compiler_params=_cparams("parallel", "arbitrary"),
        name="attn_prompt",
    )(*prev, proj, proj, *tables, gq, gk, sinks)


EVEN_U_BLOCK = ATTN_WIDTH // S5_WIDTH
EVEN_KV_BLOCK = (ATTN_WIDTH + S5_WIDTH) // (2 * KV_WIDTH)
KALL_ROWS = WINDOW + SUBLANES


def _attn_sample_kernel(q_ref, kv_ref, ck_ref, cv_ref, ct_ref, sa_ref, sb_ref, gq_ref, gk_ref, sink_ref,
                        o_ref, nk_ref, nv_ref, o_seq, *, bs, t_new, layer):
    ones = _head_ones()
    ct, sa, sb = ct_ref[...], sa_ref[...], sb_ref[...]
    kv = kv_ref[...]
    kn = _qk_prep(kv[:, :KV_WIDTH], gk_ref[...], ones, ct, sa, sb)
    v = kv[:, KV_WIDTH:]
    nq = ATTN_QCHUNKS
    qn = [_qk_prep(q_ref[:, j * LANES:(j + 1) * LANES], gq_ref[...], ones, ct, sa, sb) * (HEAD_DIM ** -0.5)
          for j in range(nq)]
    rows = nq * t_new
    r = lax.broadcasted_iota(jnp.int32, (rows, KALL_ROWS), 0)
    c = lax.broadcasted_iota(jnp.int32, (rows, KALL_ROWS), 1)
    t = r % t_new
    mask = (c >= t) & (c <= t + WINDOW)
    rj = lax.broadcasted_iota(jnp.int32, (rows, 1), 0) // t_new
    lane = lax.broadcasted_iota(jnp.int32, (t_new, LANES), 1)
    group0 = lane < HEAD_DIM
    pad = jnp.zeros((KALL_ROWS - WINDOW - t_new, KV_WIDTH), F32)
    v_ones = jnp.ones((KALL_ROWS, LANES), BF16)

    def seq_rows(a, b):
        return jnp.concatenate([a[tt * bs + b:tt * bs + b + 1] for tt in range(t_new)], axis=0)

    sinks = []
    for h in range(KV_HEADS):
        sk = jnp.zeros((rows, 1), F32)
        for j in range(nq):
            sk = jnp.where(rj == j, sink_ref[layer, h * nq + j], sk)
        sinks.append(sk)

    st = [dict() for _ in range(bs)]
    for b in range(bs):
        d = st[b]
        ck, cv = ck_ref[b], cv_ref[b]
        kn_b, v_b = seq_rows(kn, b), seq_rows(v, b)
        d["kall"] = jnp.concatenate([ck, kn_b, pad], axis=0).astype(BF16)
        d["vaug"] = jnp.concatenate([jnp.concatenate([cv, v_b, pad], axis=0).astype(BF16), v_ones], axis=1)
        nk_ref[b] = pltpu.roll(ck, WINDOW - t_new, 0)
        nv_ref[b] = pltpu.roll(cv, WINDOW - t_new, 0)
        nk_ref[b, WINDOW - t_new:WINDOW, :] = kn_b
        nv_ref[b, WINDOW - t_new:WINDOW, :] = v_b
        qb = [seq_rows(qj, b) for qj in qn]
        for h in range(KV_HEADS):
            keep = group0 if h == 0 else jnp.logical_not(group0)
            qs = jnp.concatenate([jnp.where(keep, q, 0.0) for q in qb], axis=0).astype(BF16)
            d["s", h] = lax.dot_general(qs, d["kall"], (((1,), (1,)), ((), ())), preferred_element_type=F32)
    for b in range(bs):
        d = st[b]
        for h in range(KV_HEADS):
            s = jnp.where(mask, d.pop(("s", h)), NEG_INF)
            m = jnp.maximum(jnp.max(s, axis=-1, keepdims=True), sinks[h])
            o = jnp.dot(jnp.exp(s - m).astype(BF16), d["vaug"], preferred_element_type=F32)
            d["o", h] = o[:, :LANES] / (o[:, LANES:] + jnp.exp(sinks[h] - m))
    for b in range(bs):
        d = st[b]
        o_b = jnp.concatenate([jnp.where(group0, d["o", 0][j * t_new:(j + 1) * t_new],
                                         d["o", 1][j * t_new:(j + 1) * t_new]) for j in range(nq)], axis=1)
        for tt in range(t_new):
            o_seq[tt * bs + b:tt * bs + b + 1, :] = o_b[tt:tt + 1]
    o_ref[...] = o_seq[...].astype(BF16)


def _attn_sample(proj, cache_k, cache_v, tables, gq, gk, sinks, t_new, layer, prev):
    n = proj.shape[0]
    bsz = n // t_new
    bs = SAMPLE_TILE
    rows = bs * t_new
    row = lambda i: (i, 0)
    cache = pl.BlockSpec((None, bs, WINDOW, KV_WIDTH), lambda i: (layer, i, 0, 0))
    prev, prev_specs, aliases = _alias_inputs(prev, 1)
    return pl.pallas_call(
        _skip_refs(functools.partial(_attn_sample_kernel, bs=bs, t_new=t_new, layer=layer), len(prev)),
        grid=(bsz // bs,),
        in_specs=prev_specs + [
            pl.BlockSpec((rows, ATTN_WIDTH), row),
            pl.BlockSpec((rows, 2 * KV_WIDTH), lambda i: (i, EVEN_KV_BLOCK)),
            cache, cache,
            _const_spec((rows, LANES)), _const_spec((rows, LANES)), _const_spec((rows, LANES)),
            _layer_spec((1, LANES), layer), _layer_spec((1, LANES), layer),
            pl.BlockSpec(memory_space=pltpu.SMEM)],
        out_specs=[pl.BlockSpec((rows, ATTN_WIDTH), row), cache, cache],
        out_shape=[jax.ShapeDtypeStruct((n, ATTN_WIDTH), BF16),
                   jax.ShapeDtypeStruct(cache_k.shape, F32), jax.ShapeDtypeStruct(cache_v.shape, F32)],
        input_output_aliases=aliases,
        scratch_shapes=[pltpu.VMEM((rows, ATTN_WIDTH), F32)],
        compiler_params=_cparams("parallel"),
        name="attn_sample",
    )(*prev, proj, proj, cache_k, cache_v, *tables, gq, gk, sinks)


S5_UCHUNKS = S5_WIDTH // LANES
S5_SUB = S5_FLAT // S5_UCHUNKS
S5_SCHUNKS = S5_FLAT // LANES


def _s5_tail(y, wglu_ref, bglu_ref):
    g = 0.5 * y * (1.0 + lax.erf(y * (2.0 ** -0.5)))
    z = jnp.dot(g.astype(BF16), wglu_ref[...], preferred_element_type=F32) + bglu_ref[...]
    return g * _sigmoid(z)


S5_PARTS = 2


def _s5_prompt_kernel(u_ref, wb_ref, wc_ref, lam_ref, d_ref, wglu_ref, bglu_ref,
                      o_ref, sr_ref, si_ref, xs, hst, *, nbatch, tc):
    rows = nbatch * tc
    prow, ptok = rows // S5_PARTS, tc // S5_PARTS

    @pl.when(pl.program_id(1) == 0)
    def _():
        hst[...] = jnp.zeros_like(hst)

    u = jnp.swapaxes(u_ref[...], 0, 1).reshape(rows, S5_WIDTH)
    ub = u.astype(BF16)

    def in_proj(p, cc):
        rs = slice(p * prow, (p + 1) * prow)
        res = jnp.dot(ub[rs, cc * LANES:(cc + 1) * LANES], wb_ref[cc], preferred_element_type=F32)
        for j in range(S5_SUB // LANES):
            xs[cc * 4 + j, rs, :] = res[:, j * LANES:(j + 1) * LANES]
            xs[S5_SCHUNKS + cc * 4 + j, rs, :] = res[:, S5_SUB + j * LANES:S5_SUB + (j + 1) * LANES]

    ys = {}

    def out_proj(p, cc):
        rs = slice(p * prow, (p + 1) * prow)
        s = jnp.concatenate([xs[cc * 4 + j, rs, :] for j in range(4)]
                            + [xs[S5_SCHUNKS + cc * 4 + j, rs, :] for j in range(4)], axis=1).astype(BF16)
        cols = slice(cc * LANES, (cc + 1) * LANES)
        ys[p, cc] = jnp.dot(s, wc_ref[cc], preferred_element_type=F32) + d_ref[:, cols] * u[rs, cols]

    def tail(p):
        out = _s5_tail(jnp.concatenate([ys.pop((p, cc)) for cc in range(S5_UCHUNKS)], axis=1), wglu_ref, bglu_ref)
        o_ref[:, p * ptok:(p + 1) * ptok, :] = jnp.swapaxes(out.reshape(ptok, nbatch, S5_WIDTH), 0, 1).astype(BF16)

    def scan_step(t, h):
        idx = slice(t * nbatch, (t + 1) * nbatch)
        new = list(h)
        for k in range(S5_SCHUNKS):
            hr, hi = h[k], h[S5_SCHUNKS + k]
            lr, li = lam_ref[k], lam_ref[S5_SCHUNKS + k]
            nr = lr * hr - li * hi + xs[k, idx, :]
            ni = lr * hi + li * hr + xs[S5_SCHUNKS + k, idx, :]
            xs[k, idx, :] = nr
            xs[S5_SCHUNKS + k, idx, :] = ni
            new[k], new[S5_SCHUNKS + k] = nr, ni
        return new

    for cc in range(S5_UCHUNKS):
        in_proj(0, cc)
    h = [hst[k] for k in range(2 * S5_SCHUNKS)]
    for p in range(S5_PARTS):
        work = []
        if p + 1 < S5_PARTS:
            work += [functools.partial(in_proj, p + 1, cc) for cc in range(S5_UCHUNKS)]
        if p >= 1:
            work += [functools.partial(out_proj, p - 1, cc) for cc in range(S5_UCHUNKS)]
            work.append(functools.partial(tail, p - 1))
        every = max(1, ptok // max(1, len(work)))
        for i in range(ptok):
            h = scan_step(p * ptok + i, h)
            if work and (i + 1) % every == 0:
                work.pop(0)()
        for w in work:
            w()
    for cc in range(S5_UCHUNKS):
        out_proj(S5_PARTS - 1, cc)
    tail(S5_PARTS - 1)
    for k in range(2 * S5_SCHUNKS):
        hst[k] = h[k]
    sr_ref[...] = jnp.concatenate(h[:S5_SCHUNKS], axis=1)
    si_ref[...] = jnp.concatenate(h[S5_SCHUNKS:], axis=1)


def _s5_prompt(proj, prm, layer, prev):
    bsz, seq, _ = proj.shape
    nbatch, tc = SUBLANES, S5_CHUNK
    n_layers = prm["wb"].shape[0]
    st = pl.BlockSpec((None, nbatch, S5_FLAT), lambda b, c: (layer, b, 0))
    st_shape = jax.ShapeDtypeStruct((n_layers, bsz, S5_FLAT), F32)
    prev, prev_specs, aliases = _alias_inputs(prev, 1)
    names = ("wb", "wc", "lam8", "d", "wglu", "bglu")
    return pl.pallas_call(
        _skip_refs(functools.partial(_s5_prompt_kernel, nbatch=nbatch, tc=tc), len(prev)),
        grid=(bsz // nbatch, seq // tc),
        in_specs=prev_specs + [pl.BlockSpec((nbatch, tc, S5_WIDTH), lambda b, c: (b, c, EVEN_U_BLOCK))]
        + [_layer_spec(prm[k].shape[1:], layer) for k in names],
        out_specs=[pl.BlockSpec((nbatch, tc, S5_WIDTH), lambda b, c: (b, c, 0)), st, st],
        out_shape=[jax.ShapeDtypeStruct((bsz, seq, S5_WIDTH), BF16), st_shape, st_shape],
        input_output_aliases=aliases,
        scratch_shapes=[pltpu.VMEM((2 * S5_SCHUNKS, nbatch * tc, LANES), F32),
                        pltpu.VMEM((2 * S5_SCHUNKS, nbatch, LANES), F32)],
        compiler_params=_cparams("parallel", "arbitrary"),
        name="s5_prompt",
    )(*prev, proj, *[prm[k] for k in names])


def _s5_sample_kernel(u_ref, wb_ref, wc_ref, lr_ref, li_ref, d_ref, wglu_ref, bglu_ref, h0r_ref, h0i_ref,
                      o_ref, sr_ref, si_ref, xr, xi, *, nseq, t_new):
    nt, st = nseq // SAMPLE_TILE, SAMPLE_TILE
    n = nseq * t_new
    u = u_ref[...]
    ub = u.astype(BF16)
    for cc in range(S5_UCHUNKS):
        res = jnp.dot(ub[:, cc * LANES:(cc + 1) * LANES], wb_ref[cc], preferred_element_type=F32)
        sc = slice(cc * S5_SUB, (cc + 1) * S5_SUB)
        xr[:, :, :, sc] = res[:, :S5_SUB].reshape(nt, t_new, st, S5_SUB)
        xi[:, :, :, sc] = res[:, S5_SUB:].reshape(nt, t_new, st, S5_SUB)
    lr, li = lr_ref[...], li_ref[...]
    hr, hi = h0r_ref[...], h0i_ref[...]
    for t in range(t_new):
        nr = lr * hr - li * hi + xr[:, t].reshape(nseq, S5_FLAT)
        ni = lr * hi + li * hr + xi[:, t].reshape(nseq, S5_FLAT)
        xr[:, t] = nr.reshape(nt, st, S5_FLAT)
        xi[:, t] = ni.reshape(nt, st, S5_FLAT)
        hr, hi = nr, ni
    sr_ref[...] = hr
    si_ref[...] = hi
    ys = []
    for cc in range(S5_UCHUNKS):
        sc = slice(cc * S5_SUB, (cc + 1) * S5_SUB)
        s = jnp.concatenate([xr[:, :, :, sc].reshape(n, S5_SUB), xi[:, :, :, sc].reshape(n, S5_SUB)],
                            axis=1).astype(BF16)
        cols = slice(cc * LANES, (cc + 1) * LANES)
        ys.append(jnp.dot(s, wc_ref[cc], preferred_element_type=F32) + d_ref[:, cols] * u[:, cols])
    o_ref[...] = _s5_tail(jnp.concatenate(ys, axis=1), wglu_ref, bglu_ref).astype(BF16)


def _s5_sample(proj, h0r, h0i, prm, t_new, layer, prev):
    n = proj.shape[0]
    nseq = n // t_new
    names = ("wb", "wc", "lr", "li", "d", "wglu", "bglu")
    st = pl.BlockSpec((None, nseq, S5_FLAT), lambda i: (layer, 0, 0))
    prev, prev_specs, aliases = _alias_inputs(prev, 1)
    scratch = pltpu.VMEM((nseq // SAMPLE_TILE, t_new, SAMPLE_TILE, S5_FLAT), F32)
    return pl.pallas_call(
        _skip_refs(functools.partial(_s5_sample_kernel, nseq=nseq, t_new=t_new), len(prev)),
        grid=(1,),
        in_specs=prev_specs + [pl.BlockSpec((n, S5_WIDTH), lambda i: (0, EVEN_U_BLOCK))]
        + [_layer_spec(prm[k].shape[1:], layer) for k in names]
        + [_layer_spec((nseq, S5_FLAT), layer), _layer_spec((nseq, S5_FLAT), layer)],
        out_specs=[pl.BlockSpec((n, S5_WIDTH), lambda i: (0, 0)), st, st],
        out_shape=[jax.ShapeDtypeStruct((n, S5_WIDTH), BF16),
                   jax.ShapeDtypeStruct(h0r.shape, F32), jax.ShapeDtypeStruct(h0i.shape, F32)],
        input_output_aliases=aliases,
        scratch_shapes=[scratch, scratch],
        compiler_params=_cparams("arbitrary"),
        name="s5_sample",
    )(*prev, proj, *[prm[k] for k in names], h0r, h0i)


def _s5_params(a_re, a_im, log_dt, b_re, b_im, c_re, c_im, d_skip, w_glu, b_glu):
    nl = a_re.shape[0]
    dt = jnp.exp(log_dt)
    mag = jnp.exp(a_re * dt)
    lr, li = mag * jnp.cos(a_im * dt), mag * jnp.sin(a_im * dt)
    den = a_re * a_re + a_im * a_im
    cr = ((lr - 1.0) * a_re + li * a_im) / den
    ci = (li * a_re - (lr - 1.0) * a_im) / den
    bbr = cr[..., None] * b_re - ci[..., None] * b_im
    bbi = cr[..., None] * b_im + ci[..., None] * b_re
    gpc = LANES // S5_GROUP
    eye = jnp.eye(gpc, dtype=F32)

    def in_blocks(bb):
        bb = bb.reshape(nl, S5_UCHUNKS, gpc, S5_STATE, S5_GROUP)
        return jnp.einsum("lcgph,gk->lcghkp", bb, eye).reshape(nl, S5_UCHUNKS, LANES, S5_SUB)

    def out_blocks(cm):
        cm = cm.reshape(nl, S5_UCHUNKS, gpc, S5_GROUP, S5_STATE)
        return jnp.einsum("lcghp,gk->lcgpkh", cm, eye).reshape(nl, S5_UCHUNKS, S5_SUB, LANES)

    wb = jnp.concatenate([in_blocks(bbr), in_blocks(bbi)], axis=3).astype(BF16)
    wc = jnp.concatenate([out_blocks(c_re), -out_blocks(c_im)], axis=2).astype(BF16)
    lr_f, li_f = lr.reshape(nl, 1, S5_FLAT), li.reshape(nl, 1, S5_FLAT)
    lam = jnp.concatenate([lr_f.reshape(nl, S5_SCHUNKS, 1, LANES), li_f.reshape(nl, S5_SCHUNKS, 1, LANES)], axis=1)
    lam8 = jnp.broadcast_to(lam, (nl, 2 * S5_SCHUNKS, SUBLANES, LANES))
    return dict(wb=wb, wc=wc, lam8=lam8, lr=lr_f, li=li_f, d=d_skip.reshape(nl, 1, S5_WIDTH),
                wglu=w_glu.astype(BF16), bglu=b_glu.reshape(nl, 1, S5_WIDTH))


ML_AUG = 2 * ML_DV


EXP_CLAMP = 88.0


def _den_floor(m_row):
    return jnp.exp(jnp.minimum(-m_row, EXP_CLAMP))


def _head_out(h, o, gout):
    hn = h * lax.rsqrt(jnp.mean(h * h, axis=-1, keepdims=True) + EPS) * gout
    return (hn * _sigmoid(o)).astype(BF16)


ODDP_V_BLOCK = 0
ODDP_O_BLOCK = 1
ODDP_Q_BLOCK = (2 * ML_WIDTH) // ML_QK
ODDP_G_BLOCK = (2 * ML_WIDTH + ML_QK) // LANES
ML_SPLIT = 3
ML_PIECE_LANES = 2 * ML_HEADS
ML_SEQS = 4


def _norm_matmul_kt_kernel(x_ref, g_ref, w_ref, o_ref, kt_ref):
    h = _rms(x_ref[...], g_ref[...]).astype(BF16)
    k0, v0, g0 = ML_QK, 2 * ML_QK, 2 * ML_QK + 2 * ML_WIDTH
    o_ref[:, :g0 - v0] = jnp.dot(h, w_ref[:, v0:g0], preferred_element_type=F32)
    o_ref[:, g0 - v0:g0 - v0 + k0] = jnp.dot(h, w_ref[:, :k0], preferred_element_type=F32)
    o_ref[:, g0 - v0 + k0:] = jnp.dot(h, w_ref[:, g0:], preferred_element_type=F32)
    kt = lax.dot_general(w_ref[:, k0:v0], h, (((0,), (1,)), ((), ())), preferred_element_type=F32)
    kt_ref[...] = kt * (ML_DK ** -0.5)


def _norm_matmul_kt(xs, g, layer, w, widx):
    d, m_in = w.shape[1], w.shape[2]
    return _row_groups_call(
        _norm_matmul_kt_kernel, [[x] for x in xs], [g, w],
        [_layer_spec((1, d), layer), _layer_spec((d, m_in), widx)],
        [(m_in - ML_QK, F32, True), (ML_QK, F32, False)], [], "norm_matmul_kt")


def _cummax_rows(x):
    n = x.shape[0]
    row = lax.broadcasted_iota(jnp.int32, x.shape, 0)
    shift = 1
    while shift < n:
        x = jnp.maximum(x, jnp.where(row >= shift, pltpu.roll(x, shift, 0), NEG_INF))
        shift *= 2
    return x


def _pieces(x):
    lane = lax.broadcasted_iota(jnp.int32, x.shape, 1)
    xx = x + pltpu.roll(x, ML_PIECE_LANES, 1) + pltpu.roll(x, 2 * ML_PIECE_LANES, 1)
    a1, a2, a3 = _split3(xx)
    return jnp.where(lane < ML_PIECE_LANES, a1, jnp.where(lane < 2 * ML_PIECE_LANES, a2, a3))


def _ml_select_constants():
    mask = np.zeros((ML_HEADS, LANES), np.float32)
    sel = np.zeros((ML_HEADS, LANES, 2 * ML_DV), np.float32)
    for h in range(ML_HEADS):
        for k in range(ML_SPLIT):
            lo, hi = k * ML_PIECE_LANES + h, k * ML_PIECE_LANES + ML_HEADS + h
            mask[h, lo] = mask[h, hi] = 1.0
            sel[h, lo, :ML_DV] = 1.0
            sel[h, hi, ML_DV:] = 1.0
    return jnp.asarray(mask), jnp.asarray(sel, dtype=BF16)


def _mlstm_prompt_kernel(*refs, tc, nchunks, nseq):
    v_ref, o_ref, q_ref, g_ref = refs[:4]
    kt_refs = refs[4:4 + nseq]
    bias_ref, gout_ref, mask_ref, sel_ref, h_ref, c_ref, n_ref, m_ref, caug, mst = refs[4 + nseq:]
    ci = pl.program_id(1)

    @pl.when(ci == 0)
    def _():
        caug[...] = jnp.zeros_like(caug)
        mst[...] = jnp.zeros_like(mst)

    nh = ML_HEADS
    lane = lax.broadcasted_iota(jnp.int32, (tc, LANES), 1)
    lo, hi = lane < nh, (lane >= nh) & (lane < 2 * nh)
    rt = lax.broadcasted_iota(jnp.int32, (tc, tc), 0)
    cs = lax.broadcasted_iota(jnp.int32, (tc, tc), 1)
    causal = cs <= rt
    tril = jnp.where(causal, 1.0, 0.0).astype(BF16)
    ones = jnp.ones((tc, ML_DV), F32)

    def gates(sq):
        g = g_ref[sq] + bias_ref[...]
        lf = jnp.where(hi, _log_sigmoid(g), 0.0)
        b = sum(jnp.dot(tril, p, preferred_element_type=F32) for p in _split3(lf))
        c = jnp.where(hi, pltpu.roll(g, nh, 1) - b, 0.0)
        m_prev = mst[sq]
        mx = jnp.maximum(_cummax_rows(c), m_prev)
        m_row = b + mx
        mx_lo = pltpu.roll(mx, LANES - nh, 1)
        w_inter = jnp.exp(pltpu.roll(m_prev, LANES - nh, 1) - mx_lo)
        mst[sq] = m_row[tc - 1:tc, :]
        return dict(xc=_pieces(jnp.where(lo, w_inter, jnp.where(hi, _den_floor(m_row), 0.0))),
                    lc=_pieces(jnp.where(lo, -mx_lo, jnp.where(hi, 1.0, 0.0))),
                    rc=_pieces(jnp.where(lo, 1.0, jnp.where(hi, c, 0.0))))

    gt = [gates(sq) for sq in range(nseq)]
    units = [(sq, hd) for hd in range(nh) for sq in range(nseq)]
    st = [dict() for _ in units]

    def stage1(u):
        sq, hd = units[u]
        d = st[u]
        rh = gt[sq]["rc"] * mask_ref[hd:hd + 1, :].astype(BF16)
        d["dmat"] = lax.dot_general(gt[sq]["lc"], rh, (((1,), (1,)), ((), ())), preferred_element_type=F32)
        d["wb"] = jnp.dot(gt[sq]["xc"], sel_ref[hd], preferred_element_type=F32)
        d["qh"] = q_ref[sq, :, hd * ML_DK:(hd + 1) * ML_DK]
        d["kt"] = kt_refs[sq][hd * ML_DK:(hd + 1) * ML_DK, :]
        d["qk"] = jnp.dot(d["qh"].astype(BF16), d["kt"].astype(BF16), preferred_element_type=F32)

    def stage2(u):
        sq, hd = units[u]
        d = st[u]
        cols = slice(hd * ML_DV, (hd + 1) * ML_DV)
        d["w"] = jnp.exp(jnp.where(causal, d["dmat"], NEG_INF))
        d["vaug"] = jnp.concatenate([v_ref[sq, :, cols], ones], axis=1).astype(BF16)
        d["cm"] = caug[sq, hd]
        lhs = jnp.concatenate([(d["qk"] * d["w"]).astype(BF16), (d["wb"][:, :ML_DK] * d["qh"]).astype(BF16)], axis=1)
        rhs = jnp.concatenate([d["vaug"], d["cm"].astype(BF16)], axis=0)
        d["both"] = jnp.dot(lhs, rhs, preferred_element_type=F32)
        kw = (d["kt"] * d["w"][tc - 1:tc, :]).astype(BF16)
        d["upd"] = jnp.dot(kw, d["vaug"], preferred_element_type=F32)

    def stage3(u):
        sq, hd = units[u]
        d = st[u]
        cols = slice(hd * ML_DV, (hd + 1) * ML_DV)
        both, wb = d["both"], d["wb"]
        h = both[:, :ML_DV] / jnp.maximum(jnp.abs(both[:, ML_DV:]), wb[:, ML_DV:])
        h_ref[sq, :, cols] = _head_out(h, o_ref[sq, :, cols], gout_ref[:, cols])
        decay = wb[tc - 1:tc, :ML_DV]
        caug[sq, hd] = jnp.concatenate([decay, decay], axis=1) * d["cm"] + d["upd"]
        d.clear()

    for step in range(len(units) + 2):
        if step < len(units):
            stage1(step)
        if 0 <= step - 1 < len(units):
            stage2(step - 1)
        if 0 <= step - 2 < len(units):
            stage3(step - 2)

    @pl.when(ci == nchunks - 1)
    def _():
        c_ref[...] = caug[:, :, :, :ML_DV]
        n_ref[...] = caug[:, :, :, ML_DV:]
        m_ref[...] = mst[...]


def _kt_index(b, c, *, sq, nseq, nchunks):
    return 0, (b * nseq + sq) * nchunks + c


def _mlstm_prompt(proj, kt, bias, gout, consts, layer, prev):
    bsz, seq, _ = proj.shape
    tc, nseq = ML_CHUNK, ML_SEQS
    nchunks = seq // tc
    n_layers = bias.shape[0]
    mask, sel = consts
    blk = lambda w, j: pl.BlockSpec((nseq, tc, w), lambda b, c: (b, c, j))
    st = lambda shape: pl.BlockSpec((None, nseq) + shape, lambda b, c: (layer, b) + (0,) * len(shape))
    st_shape = lambda shape: jax.ShapeDtypeStruct((n_layers, bsz) + shape, F32)
    prev, prev_specs, aliases = _alias_inputs(prev, 1)
    return pl.pallas_call(
        _skip_refs(functools.partial(_mlstm_prompt_kernel, tc=tc, nchunks=nchunks, nseq=nseq), len(prev)),
        grid=(bsz // nseq, nchunks),
        in_specs=prev_specs + [
            blk(ML_WIDTH, ODDP_V_BLOCK), blk(ML_WIDTH, ODDP_O_BLOCK), blk(ML_QK, ODDP_Q_BLOCK),
            blk(LANES, ODDP_G_BLOCK)]
        + [pl.BlockSpec((ML_QK, tc), functools.partial(_kt_index, sq=sq, nseq=nseq, nchunks=nchunks))
           for sq in range(nseq)] + [
            _layer_spec((1, LANES), layer), _layer_spec((1, ML_WIDTH), layer),
            _const_spec(mask.shape), _const_spec(sel.shape)],
        out_specs=[blk(ML_WIDTH, 0), st((ML_HEADS, ML_DK, ML_DV)), st((ML_HEADS, ML_DK, ML_DV)), st((1, LANES))],
        out_shape=[jax.ShapeDtypeStruct((bsz, seq, ML_WIDTH), BF16),
                   st_shape((ML_HEADS, ML_DK, ML_DV)), st_shape((ML_HEADS, ML_DK, ML_DV)), st_shape((1, LANES))],
        input_output_aliases=aliases,
        scratch_shapes=[pltpu.VMEM((nseq, ML_HEADS, ML_DK, ML_AUG), F32), pltpu.VMEM((nseq, 1, LANES), F32)],
        compiler_params=_cparams("parallel", "arbitrary"),
        name="mlstm_prompt",
    )(*prev, proj, proj, proj, proj, *([kt] * nseq), bias, gout, mask, sel)


MLS_SEQS = 32


def _mlstm_sample_kernel(v_ref, o_ref, q_ref, g_ref, kt_ref, bias_ref, gout_ref, mask_ref, sel_ref,
                         c0_ref, n0_ref, m0_ref, h_ref, c_ref, n_ref, m_ref, *, nseq, t_new):
    nh, nt, st = ML_HEADS, nseq // SAMPLE_TILE, SAMPLE_TILE
    rows = nseq * t_new
    lane = lax.broadcasted_iota(jnp.int32, (rows, LANES), 1)
    lo, hi = lane < nh, (lane >= nh) & (lane < 2 * nh)
    tiles = lambda a: a.reshape(nt, t_new, st, a.shape[-1])
    flat = lambda a: a.reshape(rows, a.shape[-1])
    per_seq = lambda a: a.reshape(nseq, a.shape[-1])

    g = g_ref[...] + bias_ref[...]
    lf = tiles(jnp.where(hi, _log_sigmoid(g), 0.0))
    ig = tiles(jnp.where(hi, pltpu.roll(g, nh, 1), 0.0))
    m_prev = m0_ref[...].reshape(nt, st, LANES)
    bs, cs, ms = [], [], []
    b_run, m_run = None, m_prev
    for t in range(t_new):
        b_run = lf[:, t] if b_run is None else b_run + lf[:, t]
        c_t = ig[:, t] - b_run
        m_run = jnp.maximum(m_run, c_t)
        bs.append(b_run)
        cs.append(c_t)
        ms.append(m_run)
    stack = lambda xs: flat(jnp.stack(xs, axis=1))
    b, c, mx = stack(bs), stack(cs), stack(ms)
    m_prev_rows = stack([m_prev] * t_new)
    m_row = b + mx
    m_ref[...] = per_seq(bs[-1] + ms[-1])
    mx_lo = pltpu.roll(mx, LANES - nh, 1)
    w_inter = jnp.exp(pltpu.roll(m_prev_rows, LANES - nh, 1) - mx_lo)
    xc = _pieces(jnp.where(lo, w_inter, jnp.where(hi, _den_floor(m_row), 0.0)))
    lc = _pieces(jnp.where(lo, -mx_lo, jnp.where(hi, 1.0, 0.0)))
    rc = _pieces(jnp.where(lo, 1.0, jnp.where(hi, c, 0.0)))

    def seq_of(idx):
        return (idx // (t_new * st)) * st + idx % st, (idx % (t_new * st)) // st

    rt = lax.broadcasted_iota(jnp.int32, (rows, rows), 0)
    ct = lax.broadcasted_iota(jnp.int32, (rows, rows), 1)
    (rs, rtok), (cseq, ctok) = seq_of(rt), seq_of(ct)
    valid = (rs == cseq) & (ctok <= rtok)
    rq = lax.broadcasted_iota(jnp.int32, (rows, nseq * ML_DK), 0)
    cq = lax.broadcasted_iota(jnp.int32, (rows, nseq * ML_DK), 1)
    own_q = seq_of(rq)[0] == cq // ML_DK
    rk = lax.broadcasted_iota(jnp.int32, (nseq * ML_DK, rows), 0)
    ck = lax.broadcasted_iota(jnp.int32, (nseq * ML_DK, rows), 1)
    own_k = rk // ML_DK == seq_of(ck)[0]
    ones = jnp.ones((rows, ML_DV), F32)
    last = lambda a: per_seq(tiles(a)[:, t_new - 1])

    for hd in range(nh):
        cols = slice(hd * ML_DV, (hd + 1) * ML_DV)
        rh = rc * mask_ref[hd:hd + 1, :].astype(BF16)
        dmat = lax.dot_general(lc, rh, (((1,), (1,)), ((), ())), preferred_element_type=F32)
        wb = jnp.dot(xc, sel_ref[hd], preferred_element_type=F32)
        qh = q_ref[:, hd * ML_DK:(hd + 1) * ML_DK]
        kt = kt_ref[hd * ML_DK:(hd + 1) * ML_DK, :]
        ktb = kt.astype(BF16)
        w = jnp.exp(jnp.where(valid, dmat, NEG_INF))
        qk = jnp.dot(qh.astype(BF16), ktb, preferred_element_type=F32) * w
        vaug = jnp.concatenate([v_ref[:, cols], ones], axis=1).astype(BF16)
        po = jnp.dot(qk.astype(BF16), vaug, preferred_element_type=F32)
        wq = wb[:, :ML_DK] * qh
        wq2 = jnp.concatenate([wq, wq], axis=1)
        wq_bd = jnp.where(own_q, jnp.concatenate([wq2] * (nseq * ML_DK // LANES), axis=1), 0.0).astype(BF16)
        cstack = c0_ref[:, hd].reshape(nseq * ML_DK, ML_DV)
        num = po[:, :ML_DV] + jnp.dot(wq_bd, cstack.astype(BF16), preferred_element_type=F32)
        n0 = n0_ref[hd]
        n_rows = stack([n0.reshape(nt, st, ML_DK)] * t_new)
        den = po[:, ML_DV:] + jnp.sum(wq * n_rows, axis=-1, keepdims=True)
        h = num / jnp.maximum(jnp.abs(den), wb[:, ML_DV:])
        h_ref[:, cols] = _head_out(h, o_ref[:, cols], gout_ref[:, cols])
        w_last = last(w)
        decay = last(wb[:, :ML_DV])
        n_upd = lax.dot_general(w_last.astype(BF16), ktb, (((1,), (1,)), ((), ())), preferred_element_type=F32)
        n_ref[hd] = decay[:, :ML_DK] * n0 + n_upd
        wk = jnp.sum(w_last, axis=0, keepdims=True)
        kw_bd = jnp.where(own_k, jnp.concatenate([kt * wk] * nseq, axis=0), 0.0).astype(BF16)
        upd = jnp.dot(kw_bd, v_ref[:, cols].astype(BF16), preferred_element_type=F32)
        decay_rows = jnp.broadcast_to(decay[:, None, :], (nseq, ML_DK, ML_DV)).reshape(nseq * ML_DK, ML_DV)
        c_ref[:, hd] = (decay_rows * cstack + upd).reshape(nseq, ML_DK, ML_DV)


def _mlstm_sample(proj, kt, bias, gout, consts, c0, n0h, m0, t_new, layer, prev):
    n = proj.shape[0]
    bsz = n // t_new
    nseq = MLS_SEQS
    rows = nseq * t_new
    mask, sel = consts
    blk = lambda w, j: pl.BlockSpec((rows, w), lambda i: (i, j))
    cst = pl.BlockSpec((None, nseq, ML_HEADS, ML_DK, ML_DV), lambda i: (layer, i, 0, 0, 0))
    nst = pl.BlockSpec((None, ML_HEADS, nseq, ML_DK), lambda i: (layer, 0, i, 0))
    mst = pl.BlockSpec((None, nseq, LANES), lambda i: (layer, i, 0))
    prev, prev_specs, aliases = _alias_inputs(prev, 1)
    return pl.pallas_call(
        _skip_refs(functools.partial(_mlstm_sample_kernel, nseq=nseq, t_new=t_new), len(prev)),
        grid=(bsz // nseq,),
        in_specs=prev_specs + [
            blk(ML_WIDTH, ODDP_V_BLOCK), blk(ML_WIDTH, ODDP_O_BLOCK), blk(ML_QK, ODDP_Q_BLOCK),
            blk(LANES, ODDP_G_BLOCK), pl.BlockSpec((ML_QK, rows), lambda i: (0, i)),
            _layer_spec((1, LANES), layer), _layer_spec((1, ML_WIDTH), layer),
            _const_spec(mask.shape), _const_spec(sel.shape), cst, nst, mst],
        out_specs=[blk(ML_WIDTH, 0), cst, nst, mst],
        out_shape=[jax.ShapeDtypeStruct((n, ML_WIDTH), BF16),
                   jax.ShapeDtypeStruct(c0.shape, F32), jax.ShapeDtypeStruct(n0h.shape, F32),
                   jax.ShapeDtypeStruct(m0.shape, F32)],
        input_output_aliases=aliases,
        compiler_params=_cparams("parallel"),
        name="mlstm_sample",
    )(*prev, proj, proj, proj, proj, kt, bias, gout, mask, sel, c0, n0h, m0)


def _pad_lanes(x):
    return jnp.pad(x, [(0, 0)] * (x.ndim - 1) + [(0, LANES - x.shape[-1])])


def kernel(x_prompt, x_sample, cache_k, cache_v, state_ssm_re, state_ssm_im, state_mlstm_c, state_mlstm_n, state_mlstm_m, norm_mix, norm_ffn, w_in_even, q_norm, k_norm, attn_sinks, s5_a_re, s5_a_im, s5_log_dt, s5_b_re, s5_b_im, s5_c_re, s5_c_im, s5_d, s5_w_glu, s5_b_glu, w_out_even, w_in_odd, ml_b_i, ml_b_f, ml_out_norm, w_out_odd, w_gate, w_up, w_down):
    bp, lp, _ = x_prompt.shape
    bsm, ls, _ = x_sample.shape
    yp = x_prompt.reshape(bp * lp, D_MODEL)
    ys = x_sample.reshape(bsm // SAMPLE_TILE, SAMPLE_TILE, ls, D_MODEL).transpose(0, 2, 1, 3).reshape(bsm * ls, D_MODEL)
    tab_p = _rope_tables(jnp.arange(lp))
    tab_s = tuple(jnp.repeat(t, SAMPLE_TILE, axis=0) for t in _rope_tables(PAST_LEN + jnp.arange(ls)))
    n_even, n_odd = w_in_even.shape[0], w_in_odd.shape[0]

    g_mix = norm_mix.reshape(DEPTH, 1, D_MODEL)
    g_ffn = norm_ffn.reshape(DEPTH, 1, D_MODEL)
    wg, wu, wd = w_gate.astype(BF16), w_up.astype(BF16), w_down.astype(BF16)
    kv0, u0 = ATTN_WIDTH, ATTN_WIDTH + 2 * KV_WIDTH
    order = jnp.asarray(ATTN_HEAD_ORDER)
    wq = w_in_even[..., :kv0].reshape(n_even, D_MODEL, ATTN_HEADS, HEAD_DIM)[:, :, order].reshape(n_even, D_MODEL, kv0)
    w_in_e = jnp.concatenate([wq, w_in_even[..., u0:], w_in_even[..., kv0:u0]], axis=-1).astype(BF16)
    wo_attn = w_out_even[:, :kv0].reshape(n_even, ATTN_HEADS, HEAD_DIM, D_MODEL)[:, order].reshape(n_even, kv0, D_MODEL)
    w_out_e = jnp.concatenate([wo_attn, w_out_even[:, kv0:]], axis=1).astype(BF16)
    gq = jnp.tile(q_norm, (1, LANES // HEAD_DIM)).reshape(n_even, 1, LANES)
    gk = jnp.tile(k_norm, (1, LANES // HEAD_DIM)).reshape(n_even, 1, LANES)
    prm = _s5_params(s5_a_re, s5_a_im, s5_log_dt, s5_b_re, s5_b_im, s5_c_re, s5_c_im, s5_d, s5_w_glu, s5_b_glu)
    w_in_o = jnp.pad(w_in_odd, ((0, 0), (0, 0), (0, ODD_IN_PAD - ODD_IN))).astype(BF16)
    ml_consts = _ml_select_constants()
    w_out_o = w_out_odd.astype(BF16)
    ml_bias = _pad_lanes(jnp.concatenate([ml_b_i, ml_b_f], axis=-1)).reshape(n_odd, 1, LANES)
    ml_gout = ml_out_norm.reshape(n_odd, 1, ML_WIDTH)
    ck = cache_k.reshape(n_even, bsm, WINDOW, KV_WIDTH)
    cv = cache_v.reshape(n_even, bsm, WINDOW, KV_WIDTH)
    h0r = state_ssm_re.reshape(n_even, bsm, S5_FLAT)
    h0i = state_ssm_im.reshape(n_even, bsm, S5_FLAT)
    n0h = jnp.swapaxes(state_mlstm_n, 1, 2)
    m0 = jnp.pad(state_mlstm_m, ((0, 0), (0, 0), (ML_HEADS, LANES - 2 * ML_HEADS)))

    p_attn = p_ssm = p_ml = s_attn = s_ssm = s_ml = None
    for layer in range(DEPTH):
        ffn = (layer, g_ffn, wg, wu, wd)
        if layer % 2 == 0:
            e = layer // 2
            proj_p, proj_s = _norm_matmul([yp, ys], g_mix, layer, w_in_e, e)
            proj3 = proj_p.reshape(bp, lp, -1)
            attn_p, *p_attn = _attn_prompt(proj3, tab_p, gq, gk, attn_sinks, e, p_attn)
            ssm_p, *p_ssm = _s5_prompt(proj3, prm, e, p_ssm)
            attn_s, *s_attn = _attn_sample(proj_s, ck, cv, tab_s, gq, gk, attn_sinks, ls, e, s_attn)
            ssm_s, *s_ssm = _s5_sample(proj_s, h0r, h0i, prm, ls, e, s_ssm)
            yp, ys = _mix_ffn([[yp, attn_p.reshape(bp * lp, -1), ssm_p.reshape(bp * lp, -1)], [ys, attn_s, ssm_s]],
                              w_out_e, e, *ffn)
        else:
            o = layer // 2
            (proj_p, kt_p), (proj_s, kt_s) = _norm_matmul_kt([yp, ys], g_mix, layer, w_in_o, o)
            hh_p, *p_ml = _mlstm_prompt(proj_p.reshape(bp, lp, -1), kt_p, ml_bias, ml_gout, ml_consts, o, p_ml)
            hh_s, *s_ml = _mlstm_sample(proj_s, kt_s, ml_bias, ml_gout, ml_consts, state_mlstm_c, n0h, m0, ls, o,
                                        s_ml)
            yp, ys = _mix_ffn([[yp, hh_p.reshape(bp * lp, -1)], [ys, hh_s]], w_out_o, o, *ffn)
    heads = lambda a: a.reshape(a.shape[:3] + (KV_HEADS, HEAD_DIM))
    groups = lambda a: a.reshape(a.shape[:2] + (S5_GROUPS, S5_STATE))
    ys = ys.reshape(bsm // SAMPLE_TILE, ls, SAMPLE_TILE, D_MODEL).transpose(0, 2, 1, 3).reshape(bsm, ls, D_MODEL)
    return (yp.reshape(bp, lp, D_MODEL), ys,
            heads(p_attn[0]), heads(p_attn[1]), groups(p_ssm[0]), groups(p_ssm[1]),
            p_ml[0], p_ml[1][..., 0], p_ml[2][:, :, 0, ML_HEADS:2 * ML_HEADS],
            heads(s_attn[0]), heads(s_attn[1]), groups(s_ssm[0]), groups(s_ssm[1]),
            s_ml[0], jnp.swapaxes(s_ml[1], 1, 2), s_ml[2][..., ML_HEADS:2 * ML_HEADS])
```

```python
import functools

import numpy as np

import jax
import jax.numpy as jnp
from jax import lax
from jax.experimental import pallas as pl
from jax.experimental.pallas import tpu as pltpu

F32 = jnp.float32
BF16 = jnp.bfloat16

D_MODEL = 1024
DEPTH = 4
PAST_LEN = 8192
WINDOW = 128
ATTN_HEADS = 8
KV_HEADS = 2
HEAD_DIM = 64
ATTN_WIDTH = ATTN_HEADS * HEAD_DIM
KV_WIDTH = KV_HEADS * HEAD_DIM
ROT_DIM = HEAD_DIM // 4
ROPE_THETA = 500000.0
S5_GROUP = 16
S5_WIDTH = D_MODEL // 2
S5_GROUPS = S5_WIDTH // S5_GROUP
S5_STATE = 64
S5_FLAT = S5_GROUPS * S5_STATE
ML_HEADS = 8
ML_DV = D_MODEL // ML_HEADS
ML_DK = ML_DV // 2
ML_QK = ML_HEADS * ML_DK
ML_WIDTH = ML_HEADS * ML_DV
D_FF = 2816
EPS = 1e-6

LANES = 128
SUBLANES = 8
ROW_TILE = 512
FF_TILE = 256
S5_CHUNK = 64
ML_CHUNK = 128
SAMPLE_TILE = SUBLANES
VMEM_LIMIT = 56 * 1024 * 1024

NEG_INF = float("-inf")


def _cparams(*sem):
    return pltpu.CompilerParams(dimension_semantics=sem, vmem_limit_bytes=VMEM_LIMIT)


def _const_spec(shape):
    zeros = (0,) * len(shape)
    return pl.BlockSpec(shape, lambda *_: zeros, pipeline_mode=pl.Buffered(1))


def _layer_spec(shape, layer):
    zeros = (0,) * len(shape)
    return pl.BlockSpec((None,) + tuple(shape), lambda *_: (layer,) + zeros, pipeline_mode=pl.Buffered(1))


def _skip_refs(body, n_skip):
    if n_skip == 0:
        return body

    def wrapped(*refs):
        return body(*refs[n_skip:])

    return wrapped


def _alias_inputs(prev, first_state_out):
    prev = () if prev is None else tuple(prev)
    specs = [pl.BlockSpec(memory_space=pl.ANY) for _ in prev]
    aliases = {i: first_state_out + i for i in range(len(prev))}
    return prev, specs, aliases


def _rms(x, g):
    ms = jnp.mean(x * x, axis=-1, keepdims=True)
    return x * lax.rsqrt(ms + EPS) * g


def _split3(a):
    a1 = a.astype(BF16)
    r1 = a - a1.astype(F32)
    a2 = r1.astype(BF16)
    a3 = (r1 - a2.astype(F32)).astype(BF16)
    return a1, a2, a3


def _log_sigmoid(x):
    return jnp.minimum(x, 0.0) - jnp.log(1.0 + jnp.exp(-jnp.abs(x)))


def _sigmoid(x):
    return 1.0 / (1.0 + jnp.exp(-x))


def _norm_matmul_kernel(x_ref, g_ref, w_ref, o_ref):
    h = _rms(x_ref[...], g_ref[...]).astype(BF16)
    o_ref[...] = jnp.dot(h, w_ref[...], preferred_element_type=F32)


def _row_groups_call(body, groups, consts, const_specs, out_defs, scratch_shapes, name):
    steps, tiles = [], []
    for arrays in groups:
        n = arrays[0].shape[0]
        tm = min(ROW_TILE, n)
        tiles.append(tm)
        steps.append(n // tm)
    offs = [sum(steps[:k]) for k in range(len(groups))]

    def local(k):
        return lambda i: jnp.clip(i - offs[k], 0, steps[k] - 1)

    in_specs, out_specs, out_shape, args = [], [], [], []
    for k, arrays in enumerate(groups):
        for a in arrays:
            in_specs.append(pl.BlockSpec((tiles[k], a.shape[1]), lambda i, f=local(k): (f(i), 0)))
            args.append(a)
    for k, arrays in enumerate(groups):
        n = arrays[0].shape[0]
        for width, dtype, by_rows in out_defs:
            if by_rows:
                out_specs.append(pl.BlockSpec((tiles[k], width), lambda i, f=local(k): (f(i), 0)))
                out_shape.append(jax.ShapeDtypeStruct((n, width), dtype))
            else:
                out_specs.append(pl.BlockSpec((width, tiles[k]), lambda i, f=local(k): (0, f(i))))
                out_shape.append(jax.ShapeDtypeStruct((width, n), dtype))
    n_in = [len(arrays) for arrays in groups]
    n_out = len(out_defs)

    def kern(*refs):
        i = pl.program_id(0)
        pos = 0
        ins = []
        for cnt in n_in:
            ins.append(refs[pos:pos + cnt])
            pos += cnt
        crefs = refs[pos:pos + len(consts)]
        pos += len(consts)
        outs = [refs[pos + k * n_out:pos + (k + 1) * n_out] for k in range(len(groups))]
        scratch = refs[pos + len(groups) * n_out:]
        for k in range(len(groups)):
            @pl.when((i >= offs[k]) & (i < offs[k] + steps[k]))
            def _(k=k):
                body(*ins[k], *crefs, *outs[k], *scratch)

    res = pl.pallas_call(
        kern,
        grid=(sum(steps),),
        in_specs=in_specs + list(const_specs),
        out_specs=out_specs,
        out_shape=out_shape,
        scratch_shapes=scratch_shapes,
        compiler_params=_cparams("arbitrary"),
        name=name,
    )(*args, *consts)
    return [res[k * n_out:(k + 1) * n_out] for k in range(len(groups))]


def _norm_matmul(xs, g, layer, w, widx):
    d, m = w.shape[1], w.shape[2]
    res = _row_groups_call(_norm_matmul_kernel, [[x] for x in xs], [g, w],
                           [_layer_spec((1, d), layer), _layer_spec((d, m), widx)],
                           [(m, F32, True)], [], "norm_matmul")
    return [r[0] for r in res]


def _mix_ffn_kernel(*refs, n_mix):
    x_ref = refs[0]
    a_refs = refs[1:1 + n_mix]
    wo_ref, g_ref, wg_ref, wu_ref, wd_ref, o_ref, act_ref = refs[1 + n_mix:]
    y = x_ref[...]
    off = 0
    for a_ref in a_refs:
        ka = a_ref.shape[1]
        y = y + jnp.dot(a_ref[...], wo_ref[off:off + ka, :], preferred_element_type=F32)
        off += ka
    h = _rms(y, g_ref[...]).astype(BF16)
    for f in range(D_FF // FF_TILE):
        cols = slice(f * FF_TILE, (f + 1) * FF_TILE)
        gate = jnp.dot(h, wg_ref[:, cols], preferred_element_type=F32)
        up = jnp.dot(h, wu_ref[:, cols], preferred_element_type=F32)
        act_ref[:, cols] = (gate * _sigmoid(gate) * up).astype(BF16)
    o_ref[...] = y + jnp.dot(act_ref[...], wd_ref[...], preferred_element_type=F32)


def _mix_ffn(groups, w_out, oidx, layer, g_ffn, wg, wu, wd):
    d = w_out.shape[2]
    n_mix = len(groups[0]) - 1
    tm = min(ROW_TILE, max(g[0].shape[0] for g in groups))
    res = _row_groups_call(
        functools.partial(_mix_ffn_kernel, n_mix=n_mix), groups, [w_out, g_ffn, wg, wu, wd],
        [_layer_spec(w_out.shape[1:], oidx), _layer_spec((1, d), layer), _layer_spec(wg.shape[1:], layer),
         _layer_spec(wu.shape[1:], layer), _layer_spec(wd.shape[1:], layer)],
        [(d, F32, True)], [pltpu.VMEM((tm, D_FF), BF16)], "mix_ffn")
    return [r[0] for r in res]


def _head_ones():
    r = lax.broadcasted_iota(jnp.int32, (LANES, LANES), 0) // HEAD_DIM
    c = lax.broadcasted_iota(jnp.int32, (LANES, LANES), 1) // HEAD_DIM
    return jnp.where(r == c, 1.0, 0.0).astype(BF16)


def _qk_prep(x, g, ones, ct, sa, sb):
    x2 = x * x
    hi = x2.astype(BF16)
    lo = (x2 - hi.astype(F32)).astype(BF16)
    ss = jnp.dot(hi, ones, preferred_element_type=F32) + jnp.dot(lo, ones, preferred_element_type=F32)
    xn = x * lax.rsqrt(ss * (1.0 / HEAD_DIM) + EPS) * g
    return xn * ct + pltpu.roll(xn, LANES - ROT_DIM // 2, 1) * sa + pltpu.roll(xn, ROT_DIM // 2, 1) * sb


def _rope_tables(pos):
    half = ROT_DIM // 2
    inv = jnp.power(jnp.float32(ROPE_THETA), -jnp.arange(half, dtype=F32) / half)
    ang = pos.astype(F32)[:, None] * inv[None, :]
    cos, sin = jnp.cos(ang), jnp.sin(ang)
    n = pos.shape[0]
    one = jnp.ones((n, HEAD_DIM - ROT_DIM), F32)
    zero = jnp.zeros((n, HEAD_DIM - ROT_DIM), F32)
    z8 = jnp.zeros((n, half), F32)
    ct = jnp.concatenate([cos, cos, one], axis=1)
    sa = jnp.concatenate([-sin, z8, zero], axis=1)
    sb = jnp.concatenate([z8, sin, zero], axis=1)
    tile = lambda t: jnp.concatenate([t, t], axis=1)
    return tile(ct), tile(sa), tile(sb)


ATTN_SEQS = 4
ATTN_QCHUNKS = ATTN_WIDTH // LANES
ATTN_HEAD_ORDER = tuple(h * ATTN_QCHUNKS + j for j in range(ATTN_QCHUNKS) for h in range(KV_HEADS))


def _attn_prompt_kernel(q_ref, kv_ref, ct_ref, sa_ref, sb_ref, gq_ref, gk_ref, sink_ref,
                        o_ref, pk_ref, pv_ref, kprev, vprev, *, nb, layer, nseq):
    i = pl.program_id(1)

    @pl.when(i == 0)
    def _():
        kprev[...] = jnp.zeros_like(kprev)
        vprev[...] = jnp.zeros_like(vprev)

    ones = _head_ones()
    ct, sa, sb = ct_ref[...], sa_ref[...], sb_ref[...]
    r = lax.broadcasted_iota(jnp.int32, (WINDOW, 2 * WINDOW), 0)
    c = lax.broadcasted_iota(jnp.int32, (WINDOW, 2 * WINDOW), 1)
    rel = r + WINDOW - c
    mask = (rel >= 0) & (rel <= WINDOW) & ((c >= WINDOW) | (i > 0))
    lane = lax.broadcasted_iota(jnp.int32, (WINDOW, LANES), 1)
    group0 = lane < HEAD_DIM
    v_ones = jnp.ones((2 * WINDOW, LANES), BF16)
    nq = ATTN_QCHUNKS
    st = [dict() for _ in range(nseq)]

    def prep(sq):
        d = st[sq]
        kv = kv_ref[sq]
        d["kn"] = _qk_prep(kv[:, :KV_WIDTH], gk_ref[...], ones, ct, sa, sb)
        d["v"] = kv[:, KV_WIDTH:]
        d["qn"] = [_qk_prep(q_ref[sq, :, j * LANES:(j + 1) * LANES], gq_ref[...], ones, ct, sa, sb)
                   * (HEAD_DIM ** -0.5) for j in range(nq)]
        d["kcat"] = jnp.concatenate([kprev[sq], d["kn"]], axis=0).astype(BF16)
        d["vaug"] = jnp.concatenate([jnp.concatenate([vprev[sq], d["v"]], axis=0).astype(BF16), v_ones], axis=1)
        kprev[sq] = d["kn"]
        vprev[sq] = d["v"]

    def scores(sq, h):
        d = st[sq]
        keep = group0 if h == 0 else jnp.logical_not(group0)
        qs = jnp.concatenate([jnp.where(keep, qj, 0.0) for qj in d["qn"]], axis=0).astype(BF16)
        d["s", h] = lax.dot_general(qs, d["kcat"], (((1,), (1,)), ((), ())), preferred_element_type=F32)

    def softmax_pv(sq, h):
        d = st[sq]
        s = d.pop(("s", h))
        ps, corr = [], []
        for j in range(nq):
            sg = jnp.where(mask, s[j * WINDOW:(j + 1) * WINDOW], NEG_INF)
            sink = sink_ref[layer, h * nq + j]
            m = jnp.maximum(jnp.max(sg, axis=-1, keepdims=True), sink)
            ps.append(jnp.exp(sg - m).astype(BF16))
            corr.append(jnp.exp(sink - m))
        o = jnp.dot(jnp.concatenate(ps, axis=0), d["vaug"], preferred_element_type=F32)
        d["o", h] = [o[j * WINDOW:(j + 1) * WINDOW, :LANES] / (o[j * WINDOW:(j + 1) * WINDOW, LANES:] + corr[j])
                     for j in range(nq)]

    def finish(sq):
        d = st[sq]
        o_ref[sq] = jnp.concatenate([jnp.where(group0, d["o", 0][j], d["o", 1][j]) for j in range(nq)],
                                    axis=1).astype(BF16)

    for sq in range(nseq):
        prep(sq)
    for sq in range(nseq):
        scores(sq, 0)
        scores(sq, 1)
    for sq in range(nseq):
        softmax_pv(sq, 0)
        softmax_pv(sq, 1)
        finish(sq)

    @pl.when(i == nb - 1)
    def _():
        for sq in range(nseq):
            pk_ref[sq] = st[sq]["kn"]
            pv_ref[sq] = st[sq]["v"]


def _attn_prompt(proj, tables, gq, gk, sinks, layer, prev):
    bsz, seq, _ = proj.shape
    nb = seq // WINDOW
    nseq = ATTN_SEQS
    n_layers = gq.shape[0]
    tab = pl.BlockSpec((WINDOW, LANES), lambda b, i: (i, 0))
    prev, prev_specs, aliases = _alias_inputs(prev, 1)
    win = pl.BlockSpec((None, nseq, WINDOW, KV_WIDTH), lambda b, i: (layer, b, 0, 0))
    win_shape = jax.ShapeDtypeStruct((n_layers, bsz, WINDOW, KV_WIDTH), F32)
    return pl.pallas_call(
        _skip_refs(functools.partial(_attn_prompt_kernel, nb=nb, layer=layer, nseq=nseq), len(prev)),
        grid=(bsz // nseq, nb),
        in_specs=prev_specs + [
            pl.BlockSpec((nseq, WINDOW, ATTN_WIDTH), lambda b, i: (b, i, 0)),
            pl.BlockSpec((nseq, WINDOW, 2 * KV_WIDTH), lambda b, i: (b, i, EVEN_KV_BLOCK)),
            tab, tab, tab, _layer_spec((1, LANES), layer), _layer_spec((1, LANES), layer),
            pl.BlockSpec(memory_space=pltpu.SMEM)],
        out_specs=[pl.BlockSpec((nseq, WINDOW, ATTN_WIDTH), lambda b, i: (b, i, 0)), win, win],
        out_shape=[jax.ShapeDtypeStruct((bsz, seq, ATTN_WIDTH), BF16), win_shape, win_shape],
        input_output_aliases=aliases,
        scratch_shapes=[pltpu.VMEM((nseq, WINDOW, KV_WIDTH), F32), pltpu.VMEM((nseq, WINDOW, KV_WIDTH), F32)],
        compiler_params=_cparams("parallel", "arbitrary"),
        name="attn_prompt",
    )(*prev, proj, proj, *tables, gq, gk, sinks)


EVEN_U_BLOCK = ATTN_WIDTH // S5_WIDTH
EVEN_KV_BLOCK = (ATTN_WIDTH + S5_WIDTH) // (2 * KV_WIDTH)
KALL_ROWS = WINDOW + SUBLANES


def _attn_sample_kernel(q_ref, kv_ref, ck_ref, cv_ref, ct_ref, sa_ref, sb_ref, gq_ref, gk_ref, sink_ref,
                        o_ref, nk_ref, nv_ref, o_seq, *, bs, t_new, layer):
    ones = _head_ones()
    ct, sa, sb = ct_ref[...], sa_ref[...], sb_ref[...]
    kv = kv_ref[...]
    kn = _qk_prep(kv[:, :KV_WIDTH], gk_ref[...], ones, ct, sa, sb)
    v = kv[:, KV_WIDTH:]
    nq = ATTN_QCHUNKS
    qn = [_qk_prep(q_ref[:, j * LANES:(j + 1) * LANES], gq_ref[...], ones, ct, sa, sb) * (HEAD_DIM ** -0.5)
          for j in range(nq)]
    rows = nq * t_new
    r = lax.broadcasted_iota(jnp.int32, (rows, KALL_ROWS), 0)
    c = lax.broadcasted_iota(jnp.int32, (rows, KALL_ROWS), 1)
    t = r % t_new
    mask = (c >= t) & (c <= t + WINDOW)
    rj = lax.broadcasted_iota(jnp.int32, (rows, 1), 0) // t_new
    lane = lax.broadcasted_iota(jnp.int32, (t_new, LANES), 1)
    group0 = lane < HEAD_DIM
    pad = jnp.zeros((KALL_ROWS - WINDOW - t_new, KV_WIDTH), F32)
    v_ones = jnp.ones((KALL_ROWS, LANES), BF16)

    def seq_rows(a, b):
        return jnp.concatenate([a[tt * bs + b:tt * bs + b + 1] for tt in range(t_new)], axis=0)

    sinks = []
    for h in range(KV_HEADS):
        sk = jnp.zeros((rows, 1), F32)
        for j in range(nq):
            sk = jnp.where(rj == j, sink_ref[layer, h * nq + j], sk)
        sinks.append(sk)

    st = [dict() for _ in range(bs)]
    for b in range(bs):
        d = st[b]
        ck, cv = ck_ref[b], cv_ref[b]
        kn_b, v_b = seq_rows(kn, b), seq_rows(v, b)
        d["kall"] = jnp.concatenate([ck, kn_b, pad], axis=0).astype(BF16)
        d["vaug"] = jnp.concatenate([jnp.concatenate([cv, v_b, pad], axis=0).astype(BF16), v_ones], axis=1)
        nk_ref[b] = pltpu.roll(ck, WINDOW - t_new, 0)
        nv_ref[b] = pltpu.roll(cv, WINDOW - t_new, 0)
        nk_ref[b, WINDOW - t_new:WINDOW, :] = kn_b
        nv_ref[b, WINDOW - t_new:WINDOW, :] = v_b
        qb = [seq_rows(qj, b) for qj in qn]
        for h in range(KV_HEADS):
            keep = group0 if h == 0 else jnp.logical_not(group0)
            qs = jnp.concatenate([jnp.where(keep, q, 0.0) for q in qb], axis=0).astype(BF16)
            d["s", h] = lax.dot_general(qs, d["kall"], (((1,), (1,)), ((), ())), preferred_element_type=F32)
    for b in range(bs):
        d = st[b]
        for h in range(KV_HEADS):
            s = jnp.where(mask, d.pop(("s", h)), NEG_INF)
            m = jnp.maximum(jnp.max(s, axis=-1, keepdims=True), sinks[h])
            o = jnp.dot(jnp.exp(s - m).astype(BF16), d["vaug"], preferred_element_type=F32)
            d["o", h] = o[:, :LANES] / (o[:, LANES:] + jnp.exp(sinks[h] - m))
    for b in range(bs):
        d = st[b]
        o_b = jnp.concatenate([jnp.where(group0, d["o", 0][j * t_new:(j + 1) * t_new],
                                         d["o", 1][j * t_new:(j + 1) * t_new]) for j in range(nq)], axis=1)
        for tt in range(t_new):
            o_seq[tt * bs + b:tt * bs + b + 1, :] = o_b[tt:tt + 1]
    o_ref[...] = o_seq[...].astype(BF16)


def _attn_sample(proj, cache_k, cache_v, tables, gq, gk, sinks, t_new, layer, prev):
    n = proj.shape[0]
    bsz = n // t_new
    bs = SAMPLE_TILE
    rows = bs * t_new
    row = lambda i: (i, 0)
    cache = pl.BlockSpec((None, bs, WINDOW, KV_WIDTH), lambda i: (layer, i, 0, 0))
    prev, prev_specs, aliases = _alias_inputs(prev, 1)
    return pl.pallas_call(
        _skip_refs(functools.partial(_attn_sample_kernel, bs=bs, t_new=t_new, layer=layer), len(prev)),
        grid=(bsz // bs,),
        in_specs=prev_specs + [
            pl.BlockSpec((rows, ATTN_WIDTH), row),
            pl.BlockSpec((rows, 2 * KV_WIDTH), lambda i: (i, EVEN_KV_BLOCK)),
            cache, cache,
            _const_spec((rows, LANES)), _const_spec((rows, LANES)), _const_spec((rows, LANES)),
            _layer_spec((1, LANES), layer), _layer_spec((1, LANES), layer),
            pl.BlockSpec(memory_space=pltpu.SMEM)],
        out_specs=[pl.BlockSpec((rows, ATTN_WIDTH), row), cache, cache],
        out_shape=[jax.ShapeDtypeStruct((n, ATTN_WIDTH), BF16),
                   jax.ShapeDtypeStruct(cache_k.shape, F32), jax.ShapeDtypeStruct(cache_v.shape, F32)],
        input_output_aliases=aliases,
        scratch_shapes=[pltpu.VMEM((rows, ATTN_WIDTH), F32)],
        compiler_params=_cparams("parallel"),
        name="attn_sample",
    )(*prev, proj, proj, cache_k, cache_v, *tables, gq, gk, sinks)


S5_UCHUNKS = S5_WIDTH // LANES
S5_SUB = S5_FLAT // S5_UCHUNKS
S5_SCHUNKS = S5_FLAT // LANES


def _s5_tail(y, wglu_ref, bglu_ref):
    g = 0.5 * y * (1.0 + lax.erf(y * (2.0 ** -0.5)))
    z = jnp.dot(g.astype(BF16), wglu_ref[...], preferred_element_type=F32) + bglu_ref[...]
    return g * _sigmoid(z)


S5_PARTS = 2


def _s5_prompt_kernel(u_ref, wb_ref, wc_ref, lam_ref, d_ref, wglu_ref, bglu_ref,
                      o_ref, sr_ref, si_ref, xs, hst, *, nbatch, tc):
    rows = nbatch * tc
    prow, ptok = rows // S5_PARTS, tc // S5_PARTS

    @pl.when(pl.program_id(1) == 0)
    def _():
        hst[...] = jnp.zeros_like(hst)

    u = jnp.swapaxes(u_ref[...], 0, 1).reshape(rows, S5_WIDTH)
    ub = u.astype(BF16)

    def in_proj(p, cc):
        rs = slice(p * prow, (p + 1) * prow)
        res = jnp.dot(ub[rs, cc * LANES:(cc + 1) * LANES], wb_ref[cc], preferred_element_type=F32)
        for j in range(S5_SUB // LANES):
            xs[cc * 4 + j, rs, :] = res[:, j * LANES:(j + 1) * LANES]
            xs[S5_SCHUNKS + cc * 4 + j, rs, :] = res[:, S5_SUB + j * LANES:S5_SUB + (j + 1) * LANES]

    ys = {}

    def out_proj(p, cc):
        rs = slice(p * prow, (p + 1) * prow)
        s = jnp.concatenate([xs[cc * 4 + j, rs, :] for j in range(4)]
                            + [xs[S5_SCHUNKS + cc * 4 + j, rs, :] for j in range(4)], axis=1).astype(BF16)
        cols = slice(cc * LANES, (cc + 1) * LANES)
        ys[p, cc] = jnp.dot(s, wc_ref[cc], preferred_element_type=F32) + d_ref[:, cols] * u[rs, cols]

    def tail(p):
        out = _s5_tail(jnp.concatenate([ys.pop((p, cc)) for cc in range(S5_UCHUNKS)], axis=1), wglu_ref, bglu_ref)
        o_ref[:, p * ptok:(p + 1) * ptok, :] = jnp.swapaxes(out.reshape(ptok, nbatch, S5_WIDTH), 0, 1).astype(BF16)

    def scan_step(t, h):
        idx = slice(t * nbatch, (t + 1) * nbatch)
        new = list(h)
        for k in range(S5_SCHUNKS):
            hr, hi = h[k], h[S5_SCHUNKS + k]
            lr, li = lam_ref[k], lam_ref[S5_SCHUNKS + k]
            nr = lr * hr - li * hi + xs[k, idx, :]
            ni = lr * hi + li * hr + xs[S5_SCHUNKS + k, idx, :]
            xs[k, idx, :] = nr
            xs[S5_SCHUNKS + k, idx, :] = ni
            new[k], new[S5_SCHUNKS + k] = nr, ni
        return new

    for cc in range(S5_UCHUNKS):
        in_proj(0, cc)
    h = [hst[k] for k in range(2 * S5_SCHUNKS)]
    for p in range(S5_PARTS):
        work = []
        if p + 1 < S5_PARTS:
            work += [functools.partial(in_proj, p + 1, cc) for cc in range(S5_UCHUNKS)]
        if p >= 1:
            work += [functools.partial(out_proj, p - 1, cc) for cc in range(S5_UCHUNKS)]
            work.append(functools.partial(tail, p - 1))
        every = max(1, ptok // max(1, len(work)))
        for i in range(ptok):
            h = scan_step(p * ptok + i, h)
            if work and (i + 1) % every == 0:
                work.pop(0)()
        for w in work:
            w()
    for cc in range(S5_UCHUNKS):
        out_proj(S5_PARTS - 1, cc)
    tail(S5_PARTS - 1)
    for k in range(2 * S5_SCHUNKS):
        hst[k] = h[k]
    sr_ref[...] = jnp.concatenate(h[:S5_SCHUNKS], axis=1)
    si_ref[...] = jnp.concatenate(h[S5_SCHUNKS:], axis=1)


def _s5_prompt(proj, prm, layer, prev):
    bsz, seq, _ = proj.shape
    nbatch, tc = SUBLANES, S5_CHUNK
    n_layers = prm["wb"].shape[0]
    st = pl.BlockSpec((None, nbatch, S5_FLAT), lambda b, c: (layer, b, 0))
    st_shape = jax.ShapeDtypeStruct((n_layers, bsz, S5_FLAT), F32)
    prev, prev_specs, aliases = _alias_inputs(prev, 1)
    names = ("wb", "wc", "lam8", "d", "wglu", "bglu")
    return pl.pallas_call(
        _skip_refs(functools.partial(_s5_prompt_kernel, nbatch=nbatch, tc=tc), len(prev)),
        grid=(bsz // nbatch, seq // tc),
        in_specs=prev_specs + [pl.BlockSpec((nbatch, tc, S5_WIDTH), lambda b, c: (b, c, EVEN_U_BLOCK))]
        + [_layer_spec(prm[k].shape[1:], layer) for k in names],
        out_specs=[pl.BlockSpec((nbatch, tc, S5_WIDTH), lambda b, c: (b, c, 0)), st, st],
        out_shape=[jax.ShapeDtypeStruct((bsz, seq, S5_WIDTH), BF16), st_shape, st_shape],
        input_output_aliases=aliases,
        scratch_shapes=[pltpu.VMEM((2 * S5_SCHUNKS, nbatch * tc, LANES), F32),
                        pltpu.VMEM((2 * S5_SCHUNKS, nbatch, LANES), F32)],
        compiler_params=_cparams("parallel", "arbitrary"),
        name="s5_prompt",
    )(*prev, proj, *[prm[k] for k in names])


def _s5_sample_kernel(u_ref, wb_ref, wc_ref, lr_ref, li_ref, d_ref, wglu_ref, bglu_ref, h0r_ref, h0i_ref,
                      o_ref, sr_ref, si_ref, xr, xi, *, nseq, t_new):
    nt, st = nseq // SAMPLE_TILE, SAMPLE_TILE
    n = nseq * t_new
    u = u_ref[...]
    ub = u.astype(BF16)
    for cc in range(S5_UCHUNKS):
        res = jnp.dot(ub[:, cc * LANES:(cc + 1) * LANES], wb_ref[cc], preferred_element_type=F32)
        sc = slice(cc * S5_SUB, (cc + 1) * S5_SUB)
        xr[:, :, :, sc] = res[:, :S5_SUB].reshape(nt, t_new, st, S5_SUB)
        xi[:, :, :, sc] = res[:, S5_SUB:].reshape(nt, t_new, st, S5_SUB)
    lr, li = lr_ref[...], li_ref[...]
    hr, hi = h0r_ref[...], h0i_ref[...]
    for t in range(t_new):
        nr = lr * hr - li * hi + xr[:, t].reshape(nseq, S5_FLAT)
        ni = lr * hi + li * hr + xi[:, t].reshape(nseq, S5_FLAT)
        xr[:, t] = nr.reshape(nt, st, S5_FLAT)
        xi[:, t] = ni.reshape(nt, st, S5_FLAT)
        hr, hi = nr, ni
    sr_ref[...] = hr
    si_ref[...] = hi
    ys = []
    for cc in range(S5_UCHUNKS):
        sc = slice(cc * S5_SUB, (cc + 1) * S5_SUB)
        s = jnp.concatenate([xr[:, :, :, sc].reshape(n, S5_SUB), xi[:, :, :, sc].reshape(n, S5_SUB)],
                            axis=1).astype(BF16)
        cols = slice(cc * LANES, (cc + 1) * LANES)
        ys.append(jnp.dot(s, wc_ref[cc], preferred_element_type=F32) + d_ref[:, cols] * u[:, cols])
    o_ref[...] = _s5_tail(jnp.concatenate(ys, axis=1), wglu_ref, bglu_ref).astype(BF16)


def _s5_sample(proj, h0r, h0i, prm, t_new, layer, prev):
    n = proj.shape[0]
    nseq = n // t_new
    names = ("wb", "wc", "lr", "li", "d", "wglu", "bglu")
    st = pl.BlockSpec((None, nseq, S5_FLAT), lambda i: (layer, 0, 0))
    prev, prev_specs, aliases = _alias_inputs(prev, 1)
    scratch = pltpu.VMEM((nseq // SAMPLE_TILE, t_new, SAMPLE_TILE, S5_FLAT), F32)
    return pl.pallas_call(
        _skip_refs(functools.partial(_s5_sample_kernel, nseq=nseq, t_new=t_new), len(prev)),
        grid=(1,),
        in_specs=prev_specs + [pl.BlockSpec((n, S5_WIDTH), lambda i: (0, EVEN_U_BLOCK))]
        + [_layer_spec(prm[k].shape[1:], layer) for k in names]
        + [_layer_spec((nseq, S5_FLAT), layer), _layer_spec((nseq, S5_FLAT), layer)],
        out_specs=[pl.BlockSpec((n, S5_WIDTH), lambda i: (0, 0)), st, st],
        out_shape=[jax.ShapeDtypeStruct((n, S5_WIDTH), BF16),
                   jax.ShapeDtypeStruct(h0r.shape, F32), jax.ShapeDtypeStruct(h0i.shape, F32)],
        input_output_aliases=aliases,
        scratch_shapes=[scratch, scratch],
        compiler_params=_cparams("arbitrary"),
        name="s5_sample",
    )(*prev, proj, *[prm[k] for k in names], h0r, h0i)


def _s5_params(a_re, a_im, log_dt, b_re, b_im, c_re, c_im, d_skip, w_glu, b_glu):
    nl = a_re.shape[0]
    dt = jnp.exp(log_dt)
    mag = jnp.exp(a_re * dt)
    lr, li = mag * jnp.cos(a_im * dt), mag * jnp.sin(a_im * dt)
    den = a_re * a_re + a_im * a_im
    cr = ((lr - 1.0) * a_re + li * a_im) / den
    ci = (li * a_re - (lr - 1.0) * a_im) / den
    bbr = cr[..., None] * b_re - ci[..., None] * b_im
    bbi = cr[..., None] * b_im + ci[..., None] * b_re
    gpc = LANES // S5_GROUP
    eye = jnp.eye(gpc, dtype=F32)

    def in_blocks(bb):
        bb = bb.reshape(nl, S5_UCHUNKS, gpc, S5_STATE, S5_GROUP)
        return jnp.einsum("lcgph,gk->lcghkp", bb, eye).reshape(nl, S5_UCHUNKS, LANES, S5_SUB)

    def out_blocks(cm):
        cm = cm.reshape(nl, S5_UCHUNKS, gpc, S5_GROUP, S5_STATE)
        return jnp.einsum("lcghp,gk->lcgpkh", cm, eye).reshape(nl, S5_UCHUNKS, S5_SUB, LANES)

    wb = jnp.concatenate([in_blocks(bbr), in_blocks(bbi)], axis=3).astype(BF16)
    wc = jnp.concatenate([out_blocks(c_re), -out_blocks(c_im)], axis=2).astype(BF16)
    lr_f, li_f = lr.reshape(nl, 1, S5_FLAT), li.reshape(nl, 1, S5_FLAT)
    lam = jnp.concatenate([lr_f.reshape(nl, S5_SCHUNKS, 1, LANES), li_f.reshape(nl, S5_SCHUNKS, 1, LANES)], axis=1)
    lam8 = jnp.broadcast_to(lam, (nl, 2 * S5_SCHUNKS, SUBLANES, LANES))
    return dict(wb=wb, wc=wc, lam8=lam8, lr=lr_f, li=li_f, d=d_skip.reshape(nl, 1, S5_WIDTH),
                wglu=w_glu.astype(BF16), bglu=b_glu.reshape(nl, 1, S5_WIDTH))


ML_AUG = 2 * ML_DV


EXP_CLAMP = 88.0


def _den_floor(m_row):
    return jnp.exp(jnp.minimum(-m_row, EXP_CLAMP))


def _head_out(h, o, gout):
    hn = h * lax.rsqrt(jnp.mean(h * h, axis=-1, keepdims=True) + EPS) * gout
    return (hn * _sigmoid(o)).astype(BF16)


ODDP_V_BLOCK = 0
ODDP_O_BLOCK = 1
ODDP_Q_BLOCK = (2 * ML_WIDTH) // ML_QK
ODDP_G_BLOCK = (2 * ML_WIDTH + ML_QK) // LANES
ML_SPLIT = 3
ML_PIECE_LANES = 2 * ML_HEADS
ML_SEQS = 4
ML_STAGE_LAG = 2


def _norm_matmul_kt_kernel(x_ref, g_ref, w_ref, wg_ref, o_ref, kt_ref):
    h = _rms(x_ref[...], g_ref[...]).astype(BF16)
    k0, v0, g0 = ML_QK, 2 * ML_QK, 2 * ML_QK + 2 * ML_WIDTH
    o_ref[:, :g0 - v0] = jnp.dot(h, w_ref[:, v0:g0].astype(BF16), preferred_element_type=F32)
    o_ref[:, g0 - v0:g0 - v0 + k0] = jnp.dot(h, w_ref[:, :k0].astype(BF16), preferred_element_type=F32)
    o_ref[:, g0 - v0 + k0:] = jnp.dot(h, wg_ref[...], preferred_element_type=F32)
    kt = lax.dot_general(w_ref[:, k0:v0].astype(BF16), h, (((0,), (1,)), ((), ())), preferred_element_type=F32)
    kt_ref[...] = kt * (ML_DK ** -0.5)


def _norm_matmul_kt(xs, g, layer, w, wg, widx):
    d = w.shape[1]
    m_out = 2 * ML_WIDTH + ML_QK + wg.shape[2]
    return _row_groups_call(
        _norm_matmul_kt_kernel, [[x] for x in xs], [g, w, wg],
        [_layer_spec((1, d), layer), _layer_spec(w.shape[1:], widx), _layer_spec(wg.shape[1:], widx)],
        [(m_out, F32, True), (ML_QK, F32, False)], [], "norm_matmul_kt")


def _cummax_rows(x):
    n = x.shape[0]
    row = lax.broadcasted_iota(jnp.int32, x.shape, 0)
    shift = 1
    while shift < n:
        x = jnp.maximum(x, jnp.where(row >= shift, pltpu.roll(x, shift, 0), NEG_INF))
        shift *= 2
    return x


def _pieces(x):
    lane = lax.broadcasted_iota(jnp.int32, x.shape, 1)
    xx = x + pltpu.roll(x, ML_PIECE_LANES, 1) + pltpu.roll(x, 2 * ML_PIECE_LANES, 1)
    a1, a2, a3 = _split3(xx)
    return jnp.where(lane < ML_PIECE_LANES, a1, jnp.where(lane < 2 * ML_PIECE_LANES, a2, a3))


def _ml_select_constants():
    mask = np.zeros((ML_HEADS, LANES), np.float32)
    sel = np.zeros((ML_HEADS, LANES, 2 * ML_DV), np.float32)
    for h in range(ML_HEADS):
        for k in range(ML_SPLIT):
            lo, hi = k * ML_PIECE_LANES + h, k * ML_PIECE_LANES + ML_HEADS + h
            mask[h, lo] = mask[h, hi] = 1.0
            sel[h, lo, :ML_DV] = 1.0
            sel[h, hi, ML_DV:] = 1.0
    return jnp.asarray(mask), jnp.asarray(sel, dtype=BF16)


def _mlstm_prompt_kernel(*refs, tc, nchunks, nseq):
    v_ref, o_ref, q_ref, g_ref = refs[:4]
    kt_refs = refs[4:4 + nseq]
    bias_ref, gout_ref, mask_ref, sel_ref, h_ref, c_ref, n_ref, m_ref, caug, mst = refs[4 + nseq:]
    ci = pl.program_id(1)

    @pl.when(ci == 0)
    def _():
        caug[...] = jnp.zeros_like(caug)
        mst[...] = jnp.zeros_like(mst)

    nh = ML_HEADS
    lane = lax.broadcasted_iota(jnp.int32, (tc, LANES), 1)
    lo, hi = lane < nh, (lane >= nh) & (lane < 2 * nh)
    rt = lax.broadcasted_iota(jnp.int32, (tc, tc), 0)
    cs = lax.broadcasted_iota(jnp.int32, (tc, tc), 1)
    causal = cs <= rt
    tril = jnp.where(causal, 1.0, 0.0).astype(BF16)
    ones = jnp.ones((tc, ML_DV), F32)

    def gates(sq):
        g = g_ref[sq] + bias_ref[...]
        lf = jnp.where(hi, _log_sigmoid(g), 0.0)
        b = sum(jnp.dot(tril, p, preferred_element_type=F32) for p in _split3(lf))
        c = jnp.where(hi, pltpu.roll(g, nh, 1) - b, 0.0)
        m_prev = mst[sq]
        mx = jnp.maximum(_cummax_rows(c), m_prev)
        m_row = b + mx
        mx_lo = pltpu.roll(mx, LANES - nh, 1)
        w_inter = jnp.exp(pltpu.roll(m_prev, LANES - nh, 1) - mx_lo)
        mst[sq] = m_row[tc - 1:tc, :]
        return dict(xc=_pieces(jnp.where(lo, w_inter, jnp.where(hi, _den_floor(m_row), 0.0))),
                    lc=_pieces(jnp.where(lo, -mx_lo, jnp.where(hi, 1.0, 0.0))),
                    rc=_pieces(jnp.where(lo, 1.0, jnp.where(hi, c, 0.0))))

    gt = [gates(sq) for sq in range(nseq)]
    units = [(sq, hd) for hd in range(nh) for sq in range(nseq)]
    st = [dict() for _ in units]

    def stage1(u):
        sq, hd = units[u]
        d = st[u]
        rh = gt[sq]["rc"] * mask_ref[hd:hd + 1, :].astype(BF16)
        d["dmat"] = lax.dot_general(gt[sq]["lc"], rh, (((1,), (1,)), ((), ())), preferred_element_type=F32)
        d["wb"] = jnp.dot(gt[sq]["xc"], sel_ref[hd], preferred_element_type=F32)
        d["qh"] = q_ref[sq, :, hd * ML_DK:(hd + 1) * ML_DK]
        d["kt"] = kt_refs[sq][hd * ML_DK:(hd + 1) * ML_DK, :]
        d["qk"] = jnp.dot(d["qh"].astype(BF16), d["kt"].astype(BF16), preferred_element_type=F32)

    def stage2(u):
        sq, hd = units[u]
        d = st[u]
        cols = slice(hd * ML_DV, (hd + 1) * ML_DV)
        d["w"] = jnp.exp(jnp.where(causal, d["dmat"], NEG_INF))
        d["vaug"] = jnp.concatenate([v_ref[sq, :, cols], ones], axis=1).astype(BF16)
        d["cm"] = caug[sq, hd]
        lhs = jnp.concatenate([(d["qk"] * d["w"]).astype(BF16), (d["wb"][:, :ML_DK] * d["qh"]).astype(BF16)], axis=1)
        rhs = jnp.concatenate([d["vaug"], d["cm"].astype(BF16)], axis=0)
        d["both"] = jnp.dot(lhs, rhs, preferred_element_type=F32)
        kw = (d["kt"] * d["w"][tc - 1:tc, :]).astype(BF16)
        d["upd"] = jnp.dot(kw, d["vaug"], preferred_element_type=F32)

    def stage3(u):
        sq, hd = units[u]
        d = st[u]
        cols = slice(hd * ML_DV, (hd + 1) * ML_DV)
        both, wb = d["both"], d["wb"]
        h = both[:, :ML_DV] / jnp.maximum(jnp.abs(both[:, ML_DV:]), wb[:, ML_DV:])
        h_ref[sq, :, cols] = _head_out(h, o_ref[sq, :, cols], gout_ref[:, cols])
        decay = wb[tc - 1:tc, :ML_DV]
        caug[sq, hd] = jnp.concatenate([decay, decay], axis=1) * d["cm"] + d["upd"]
        d.clear()

    for step in range(len(units) + 2 * ML_STAGE_LAG):
        if step < len(units):
            stage1(step)
        if 0 <= step - ML_STAGE_LAG < len(units):
            stage2(step - ML_STAGE_LAG)
        if 0 <= step - 2 * ML_STAGE_LAG < len(units):
            stage3(step - 2 * ML_STAGE_LAG)

    @pl.when(ci == nchunks - 1)
    def _():
        c_ref[...] = caug[:, :, :, :ML_DV]
        n_ref[...] = caug[:, :, :, ML_DV:]
        m_ref[...] = mst[...]


def _kt_index(b, c, *, sq, nseq, nchunks):
    return 0, (b * nseq + sq) * nchunks + c


def _mlstm_prompt(proj, kt, bias, gout, consts, layer, prev):
    bsz, seq, _ = proj.shape
    tc, nseq = ML_CHUNK, ML_SEQS
    nchunks = seq // tc
    n_layers = bias.shape[0]
    mask, sel = consts
    blk = lambda w, j: pl.BlockSpec((nseq, tc, w), lambda b, c: (b, c, j))
    st = lambda shape: pl.BlockSpec((None, nseq) + shape, lambda b, c: (layer, b) + (0,) * len(shape))
    st_shape = lambda shape: jax.ShapeDtypeStruct((n_layers, bsz) + shape, F32)
    prev, prev_specs, aliases = _alias_inputs(prev, 1)
    return pl.pallas_call(
        _skip_refs(functools.partial(_mlstm_prompt_kernel, tc=tc, nchunks=nchunks, nseq=nseq), len(prev)),
        grid=(bsz // nseq, nchunks),
        in_specs=prev_specs + [
            blk(ML_WIDTH, ODDP_V_BLOCK), blk(ML_WIDTH, ODDP_O_BLOCK), blk(ML_QK, ODDP_Q_BLOCK),
            blk(LANES, ODDP_G_BLOCK)]
        + [pl.BlockSpec((ML_QK, tc), functools.partial(_kt_index, sq=sq, nseq=nseq, nchunks=nchunks))
           for sq in range(nseq)] + [
            _layer_spec((1, LANES), layer), _layer_spec((1, ML_WIDTH), layer),
            _const_spec(mask.shape), _const_spec(sel.shape)],
        out_specs=[blk(ML_WIDTH, 0), st((ML_HEADS, ML_DK, ML_DV)), st((ML_HEADS, ML_DK, ML_DV)), st((1, LANES))],
        out_shape=[jax.ShapeDtypeStruct((bsz, seq, ML_WIDTH), BF16),
                   st_shape((ML_HEADS, ML_DK, ML_DV)), st_shape((ML_HEADS, ML_DK, ML_DV)), st_shape((1, LANES))],
        input_output_aliases=aliases,
        scratch_shapes=[pltpu.VMEM((nseq, ML_HEADS, ML_DK, ML_AUG), F32), pltpu.VMEM((nseq, 1, LANES), F32)],
        compiler_params=_cparams("parallel", "arbitrary"),
        name="mlstm_prompt",
    )(*prev, proj, proj, proj, proj, *([kt] * nseq), bias, gout, mask, sel)


MLS_SEQS = 32


def _mlstm_sample_kernel(v_ref, o_ref, q_ref, g_ref, kt_ref, bias_ref, gout_ref, mask_ref, sel_ref,
                         c0_ref, n0_ref, m0_ref, h_ref, c_ref, n_ref, m_ref, *, nseq, t_new):
    nh, nt, st = ML_HEADS, nseq // SAMPLE_TILE, SAMPLE_TILE
    rows = nseq * t_new
    lane = lax.broadcasted_iota(jnp.int32, (rows, LANES), 1)
    lo, hi = lane < nh, (lane >= nh) & (lane < 2 * nh)
    tiles = lambda a: a.reshape(nt, t_new, st, a.shape[-1])
    flat = lambda a: a.reshape(rows, a.shape[-1])
    per_seq = lambda a: a.reshape(nseq, a.shape[-1])

    g = g_ref[...] + bias_ref[...]
    lf = tiles(jnp.where(hi, _log_sigmoid(g), 0.0))
    ig = tiles(jnp.where(hi, pltpu.roll(g, nh, 1), 0.0))
    m_prev = m0_ref[...].reshape(nt, st, LANES)
    bs, cs, ms = [], [], []
    b_run, m_run = None, m_prev
    for t in range(t_new):
        b_run = lf[:, t] if b_run is None else b_run + lf[:, t]
        c_t = ig[:, t] - b_run
        m_run = jnp.maximum(m_run, c_t)
        bs.append(b_run)
        cs.append(c_t)
        ms.append(m_run)
    stack = lambda xs: flat(jnp.stack(xs, axis=1))
    b, c, mx = stack(bs), stack(cs), stack(ms)
    m_prev_rows = stack([m_prev] * t_new)
    m_row = b + mx
    m_ref[...] = per_seq(bs[-1] + ms[-1])
    mx_lo = pltpu.roll(mx, LANES - nh, 1)
    w_inter = jnp.exp(pltpu.roll(m_prev_rows, LANES - nh, 1) - mx_lo)
    xc = _pieces(jnp.where(lo, w_inter, jnp.where(hi, _den_floor(m_row), 0.0)))
    lc = _pieces(jnp.where(lo, -mx_lo, jnp.where(hi, 1.0, 0.0)))
    rc = _pieces(jnp.where(lo, 1.0, jnp.where(hi, c, 0.0)))

    def seq_of(idx):
        return (idx // (t_new * st)) * st + idx % st, (idx % (t_new * st)) // st

    rt = lax.broadcasted_iota(jnp.int32, (rows, rows), 0)
    ct = lax.broadcasted_iota(jnp.int32, (rows, rows), 1)
    (rs, rtok), (cseq, ctok) = seq_of(rt), seq_of(ct)
    valid = (rs == cseq) & (ctok <= rtok)
    rq = lax.broadcasted_iota(jnp.int32, (rows, nseq * ML_DK), 0)
    cq = lax.broadcasted_iota(jnp.int32, (rows, nseq * ML_DK), 1)
    own_q = seq_of(rq)[0] == cq // ML_DK
    rk = lax.broadcasted_iota(jnp.int32, (nseq * ML_DK, rows), 0)
    ck = lax.broadcasted_iota(jnp.int32, (nseq * ML_DK, rows), 1)
    own_k = rk // ML_DK == seq_of(ck)[0]
    ones = jnp.ones((rows, ML_DV), F32)
    last = lambda a: per_seq(tiles(a)[:, t_new - 1])

    for hd in range(nh):
        cols = slice(hd * ML_DV, (hd + 1) * ML_DV)
        rh = rc * mask_ref[hd:hd + 1, :].astype(BF16)
        dmat = lax.dot_general(lc, rh, (((1,), (1,)), ((), ())), preferred_element_type=F32)
        wb = jnp.dot(xc, sel_ref[hd], preferred_element_type=F32)
        qh = q_ref[:, hd * ML_DK:(hd + 1) * ML_DK]
        kt = kt_ref[hd * ML_DK:(hd + 1) * ML_DK, :]
        ktb = kt.astype(BF16)
        w = jnp.exp(jnp.where(valid, dmat, NEG_INF))
        qk = jnp.dot(qh.astype(BF16), ktb, preferred_element_type=F32) * w
        vaug = jnp.concatenate([v_ref[:, cols], ones], axis=1).astype(BF16)
        po = jnp.dot(qk.astype(BF16), vaug, preferred_element_type=F32)
        wq = wb[:, :ML_DK] * qh
        wq2 = jnp.concatenate([wq, wq], axis=1)
        wq_bd = jnp.where(own_q, jnp.concatenate([wq2] * (nseq * ML_DK // LANES), axis=1), 0.0).astype(BF16)
        cstack = c0_ref[:, hd].reshape(nseq * ML_DK, ML_DV)
        num = po[:, :ML_DV] + jnp.dot(wq_bd, cstack.astype(BF16), preferred_element_type=F32)
        n0 = n0_ref[hd]
        n_rows = stack([n0.reshape(nt, st, ML_DK)] * t_new)
        den = po[:, ML_DV:] + jnp.sum(wq * n_rows, axis=-1, keepdims=True)
        h = num / jnp.maximum(jnp.abs(den), wb[:, ML_DV:])
        h_ref[:, cols] = _head_out(h, o_ref[:, cols], gout_ref[:, cols])
        w_last = last(w)
        decay = last(wb[:, :ML_DV])
        n_upd = lax.dot_general(w_last.astype(BF16), ktb, (((1,), (1,)), ((), ())), preferred_element_type=F32)
        n_ref[hd] = decay[:, :ML_DK] * n0 + n_upd
        wk = jnp.sum(w_last, axis=0, keepdims=True)
        kw_bd = jnp.where(own_k, jnp.concatenate([kt * wk] * nseq, axis=0), 0.0).astype(BF16)
        upd = jnp.dot(kw_bd, v_ref[:, cols].astype(BF16), preferred_element_type=F32)
        decay_rows = jnp.broadcast_to(decay[:, None, :], (nseq, ML_DK, ML_DV)).reshape(nseq * ML_DK, ML_DV)
        c_ref[:, hd] = (decay_rows * cstack + upd).reshape(nseq, ML_DK, ML_DV)


def _mlstm_sample(proj, kt, bias, gout, consts, c0, n0h, m0, t_new, layer, prev):
    n = proj.shape[0]
    bsz = n // t_new
    nseq = MLS_SEQS
    rows = nseq * t_new
    mask, sel = consts
    blk = lambda w, j: pl.BlockSpec((rows, w), lambda i: (i, j))
    cst = pl.BlockSpec((None, nseq, ML_HEADS, ML_DK, ML_DV), lambda i: (layer, i, 0, 0, 0))
    nst = pl.BlockSpec((None, ML_HEADS, nseq, ML_DK), lambda i: (layer, 0, i, 0))
    mst = pl.BlockSpec((None, nseq, LANES), lambda i: (layer, i, 0))
    prev, prev_specs, aliases = _alias_inputs(prev, 1)
    return pl.pallas_call(
        _skip_refs(functools.partial(_mlstm_sample_kernel, nseq=nseq, t_new=t_new), len(prev)),
        grid=(bsz // nseq,),
        in_specs=prev_specs + [
            blk(ML_WIDTH, ODDP_V_BLOCK), blk(ML_WIDTH, ODDP_O_BLOCK), blk(ML_QK, ODDP_Q_BLOCK),
            blk(LANES, ODDP_G_BLOCK), pl.BlockSpec((ML_QK, rows), lambda i: (0, i)),
            _layer_spec((1, LANES), layer), _layer_spec((1, ML_WIDTH), layer),
            _const_spec(mask.shape), _const_spec(sel.shape), cst, nst, mst],
        out_specs=[blk(ML_WIDTH, 0), cst, nst, mst],
        out_shape=[jax.ShapeDtypeStruct((n, ML_WIDTH), BF16),
                   jax.ShapeDtypeStruct(c0.shape, F32), jax.ShapeDtypeStruct(n0h.shape, F32),
                   jax.ShapeDtypeStruct(m0.shape, F32)],
        input_output_aliases=aliases,
        compiler_params=_cparams("parallel"),
        name="mlstm_sample",
    )(*prev, proj, proj, proj, proj, kt, bias, gout, mask, sel, c0, n0h, m0)


def _pad_lanes(x):
    return jnp.pad(x, [(0, 0)] * (x.ndim - 1) + [(0, LANES - x.shape[-1])])


def kernel(x_prompt, x_sample, cache_k, cache_v, state_ssm_re, state_ssm_im, state_mlstm_c, state_mlstm_n, state_mlstm_m, norm_mix, norm_ffn, w_in_even, q_norm, k_norm, attn_sinks, s5_a_re, s5_a_im, s5_log_dt, s5_b_re, s5_b_im, s5_c_re, s5_c_im, s5_d, s5_w_glu, s5_b_glu, w_out_even, w_in_odd, ml_b_i, ml_b_f, ml_out_norm, w_out_odd, w_gate, w_up, w_down):
    bp, lp, _ = x_prompt.shape
    bsm, ls, _ = x_sample.shape
    yp = x_prompt.reshape(bp * lp, D_MODEL)
    ys = x_sample.reshape(bsm // SAMPLE_TILE, SAMPLE_TILE, ls, D_MODEL).transpose(0, 2, 1, 3).reshape(bsm * ls, D_MODEL)
    tab_p = _rope_tables(jnp.arange(lp))
    tab_s = tuple(jnp.repeat(t, SAMPLE_TILE, axis=0) for t in _rope_tables(PAST_LEN + jnp.arange(ls)))
    n_even, n_odd = w_in_even.shape[0], w_in_odd.shape[0]

    g_mix = norm_mix.reshape(DEPTH, 1, D_MODEL)
    g_ffn = norm_ffn.reshape(DEPTH, 1, D_MODEL)
    wg, wu, wd = w_gate.astype(BF16), w_up.astype(BF16), w_down.astype(BF16)
    kv0, u0 = ATTN_WIDTH, ATTN_WIDTH + 2 * KV_WIDTH
    order = jnp.asarray(ATTN_HEAD_ORDER)
    wq = w_in_even[..., :kv0].reshape(n_even, D_MODEL, ATTN_HEADS, HEAD_DIM)[:, :, order].reshape(n_even, D_MODEL, kv0)
    w_in_e = jnp.concatenate([wq, w_in_even[..., u0:], w_in_even[..., kv0:u0]], axis=-1).astype(BF16)
    wo_attn = w_out_even[:, :kv0].reshape(n_even, ATTN_HEADS, HEAD_DIM, D_MODEL)[:, order].reshape(n_even, kv0, D_MODEL)
    w_out_e = jnp.concatenate([wo_attn, w_out_even[:, kv0:]], axis=1).astype(BF16)
    gq = jnp.tile(q_norm, (1, LANES // HEAD_DIM)).reshape(n_even, 1, LANES)
    gk = jnp.tile(k_norm, (1, LANES // HEAD_DIM)).reshape(n_even, 1, LANES)
    prm = _s5_params(s5_a_re, s5_a_im, s5_log_dt, s5_b_re, s5_b_im, s5_c_re, s5_c_im, s5_d, s5_w_glu, s5_b_glu)
    w_gates_o = _pad_lanes(w_in_odd[..., 2 * ML_QK + 2 * ML_WIDTH:]).astype(BF16)
    ml_consts = _ml_select_constants()
    w_out_o = w_out_odd.astype(BF16)
    ml_bias = _pad_lanes(jnp.concatenate([ml_b_i, ml_b_f], axis=-1)).reshape(n_odd, 1, LANES)
    ml_gout = ml_out_norm.reshape(n_odd, 1, ML_WIDTH)
    ck = cache_k.reshape(n_even, bsm, WINDOW, KV_WIDTH)
    cv = cache_v.reshape(n_even, bsm, WINDOW, KV_WIDTH)
    h0r = state_ssm_re.reshape(n_even, bsm, S5_FLAT)
    h0i = state_ssm_im.reshape(n_even, bsm, S5_FLAT)
    n0h = jnp.swapaxes(state_mlstm_n, 1, 2)
    m0 = jnp.pad(state_mlstm_m, ((0, 0), (0, 0), (ML_HEADS, LANES - 2 * ML_HEADS)))

    p_attn = p_ssm = p_ml = s_attn = s_ssm = s_ml = None
    for layer in range(DEPTH):
        ffn = (layer, g_ffn, wg, wu, wd)
        if layer % 2 == 0:
            e = layer // 2
            proj_p, proj_s = _norm_matmul([yp, ys], g_mix, layer, w_in_e, e)
            proj3 = proj_p.reshape(bp, lp, -1)
            attn_p, *p_attn = _attn_prompt(proj3, tab_p, gq, gk, attn_sinks, e, p_attn)
            ssm_p, *p_ssm = _s5_prompt(proj3, prm, e, p_ssm)
            attn_s, *s_attn = _attn_sample(proj_s, ck, cv, tab_s, gq, gk, attn_sinks, ls, e, s_attn)
            ssm_s, *s_ssm = _s5_sample(proj_s, h0r, h0i, prm, ls, e, s_ssm)
            yp, ys = _mix_ffn([[yp, attn_p.reshape(bp * lp, -1), ssm_p.reshape(bp * lp, -1)], [ys, attn_s, ssm_s]],
                              w_out_e, e, *ffn)
        else:
            o = layer // 2
            (proj_p, kt_p), (proj_s, kt_s) = _norm_matmul_kt([yp, ys], g_mix, layer, w_in_odd, w_gates_o, o)
            hh_p, *p_ml = _mlstm_prompt(proj_p.reshape(bp, lp, -1), kt_p, ml_bias, ml_gout, ml_consts, o, p_ml)
            hh_s, *s_ml = _mlstm_sample(proj_s, kt_s, ml_bias, ml_gout, ml_consts, state_mlstm_c, n0h, m0, ls, o,
                                        s_ml)
            yp, ys = _mix_ffn([[yp, hh_p.reshape(bp * lp, -1)], [ys, hh_s]], w_out_o, o, *ffn)
    heads = lambda a: a.reshape(a.shape[:3] + (KV_HEADS, HEAD_DIM))
    groups = lambda a: a.reshape(a.shape[:2] + (S5_GROUPS, S5_STATE))
    ys = ys.reshape(bsm // SAMPLE_TILE, ls, SAMPLE_TILE, D_MODEL).transpose(0, 2, 1, 3).reshape(bsm, ls, D_MODEL)
    return (yp.reshape(bp, lp, D_MODEL), ys,
            heads(p_attn[0]), heads(p_attn[1]), groups(p_ssm[0]), groups(p_ssm[1]),
            p_ml[0], p_ml[1][..., 0], p_ml[2][:, :, 0, ML_HEADS:2 * ML_HEADS],
            heads(s_attn[0]), heads(s_attn[1]), groups(s_ssm[0]), groups(s_ssm[1]),
            s_ml[0], jnp.swapaxes(s_ml[1], 1, 2), s_ml[2][..., ML_HEADS:2 * ML_HEADS])
```

```python
import functools

import numpy as np

import jax
import jax.numpy as jnp
from jax import lax
from jax.experimental import pallas as pl
from jax.experimental.pallas import tpu as pltpu

F32 = jnp.float32
BF16 = jnp.bfloat16

D_MODEL = 1024
DEPTH = 4
PAST_LEN = 8192
WINDOW = 128
ATTN_HEADS = 8
KV_HEADS = 2
HEAD_DIM = 64
ATTN_WIDTH = ATTN_HEADS * HEAD_DIM
KV_WIDTH = KV_HEADS * HEAD_DIM
ROT_DIM = HEAD_DIM // 4
ROPE_THETA = 500000.0
S5_GROUP = 16
S5_WIDTH = D_MODEL // 2
S5_GROUPS = S5_WIDTH // S5_GROUP
S5_STATE = 64
S5_FLAT = S5_GROUPS * S5_STATE
ML_HEADS = 8
ML_DV = D_MODEL // ML_HEADS
ML_DK = ML_DV // 2
ML_QK = ML_HEADS * ML_DK
ML_WIDTH = ML_HEADS * ML_DV
D_FF = 2816
EPS = 1e-6

LANES = 128
SUBLANES = 8
ROW_TILE = 512
FF_TILE = 256
S5_CHUNK = 64
ML_CHUNK = 128
SAMPLE_TILE = SUBLANES
VMEM_LIMIT = 56 * 1024 * 1024

NEG_INF = float("-inf")


def _cparams(*sem):
    return pltpu.CompilerParams(dimension_semantics=sem, vmem_limit_bytes=VMEM_LIMIT)


def _const_spec(shape):
    zeros = (0,) * len(shape)
    return pl.BlockSpec(shape, lambda *_: zeros, pipeline_mode=pl.Buffered(1))


def _layer_spec(shape, layer):
    zeros = (0,) * len(shape)
    return pl.BlockSpec((None,) + tuple(shape), lambda *_: (layer,) + zeros, pipeline_mode=pl.Buffered(1))


def _skip_refs(body, n_skip):
    if n_skip == 0:
        return body

    def wrapped(*refs):
        return body(*refs[n_skip:])

    return wrapped


def _alias_inputs(prev, first_state_out):
    prev = () if prev is None else tuple(prev)
    specs = [pl.BlockSpec(memory_space=pl.ANY) for _ in prev]
    aliases = {i: first_state_out + i for i in range(len(prev))}
    return prev, specs, aliases


def _rms(x, g):
    ms = jnp.mean(x * x, axis=-1, keepdims=True)
    return x * lax.rsqrt(ms + EPS) * g


def _split3(a):
    a1 = a.astype(BF16)
    r1 = a - a1.astype(F32)
    a2 = r1.astype(BF16)
    a3 = (r1 - a2.astype(F32)).astype(BF16)
    return a1, a2, a3


def _log_sigmoid(x):
    return jnp.minimum(x, 0.0) - jnp.log(1.0 + jnp.exp(-jnp.abs(x)))


def _sigmoid(x):
    return 1.0 / (1.0 + jnp.exp(-x))


def _norm_matmul_kernel(x_ref, g_ref, w_ref, o_ref):
    h = _rms(x_ref[...], g_ref[...]).astype(BF16)
    o_ref[...] = jnp.dot(h, w_ref[...], preferred_element_type=F32)


def _row_groups_call(body, groups, consts, const_specs, out_defs, scratch_shapes, name):
    steps, tiles = [], []
    for arrays in groups:
        n = arrays[0].shape[0]
        tm = min(ROW_TILE, n)
        tiles.append(tm)
        steps.append(n // tm)
    offs = [sum(steps[:k]) for k in range(len(groups))]

    def local(k):
        return lambda i: jnp.clip(i - offs[k], 0, steps[k] - 1)

    in_specs, out_specs, out_shape, args = [], [], [], []
    for k, arrays in enumerate(groups):
        for a in arrays:
            in_specs.append(pl.BlockSpec((tiles[k], a.shape[1]), lambda i, f=local(k): (f(i), 0)))
            args.append(a)
    for k, arrays in enumerate(groups):
        n = arrays[0].shape[0]
        for width, dtype, by_rows in out_defs:
            if by_rows:
                out_specs.append(pl.BlockSpec((tiles[k], width), lambda i, f=local(k): (f(i), 0)))
                out_shape.append(jax.ShapeDtypeStruct((n, width), dtype))
            else:
                out_specs.append(pl.BlockSpec((width, tiles[k]), lambda i, f=local(k): (0, f(i))))
                out_shape.append(jax.ShapeDtypeStruct((width, n), dtype))
    n_in = [len(arrays) for arrays in groups]
    n_out = len(out_defs)

    def kern(*refs):
        i = pl.program_id(0)
        pos = 0
        ins = []
        for cnt in n_in:
            ins.append(refs[pos:pos + cnt])
            pos += cnt
        crefs = refs[pos:pos + len(consts)]
        pos += len(consts)
        outs = [refs[pos + k * n_out:pos + (k + 1) * n_out] for k in range(len(groups))]
        scratch = refs[pos + len(groups) * n_out:]
        for k in range(len(groups)):
            @pl.when((i >= offs[k]) & (i < offs[k] + steps[k]))
            def _(k=k):
                body(*ins[k], *crefs, *outs[k], *scratch)

    res = pl.pallas_call(
        kern,
        grid=(sum(steps),),
        in_specs=in_specs + list(const_specs),
        out_specs=out_specs,
        out_shape=out_shape,
        scratch_shapes=scratch_shapes,
        compiler_params=_cparams("arbitrary"),
        name=name,
    )(*args, *consts)
    return [res[k * n_out:(k + 1) * n_out] for k in range(len(groups))]


def _norm_matmul(xs, g, layer, w, widx):
    d, m = w.shape[1], w.shape[2]
    res = _row_groups_call(_norm_matmul_kernel, [[x] for x in xs], [g, w],
                           [_layer_spec((1, d), layer), _layer_spec((d, m), widx)],
                           [(m, F32, True)], [], "norm_matmul")
    return [r[0] for r in res]


def _mix_ffn_kernel(*refs, n_mix):
    x_ref = refs[0]
    a_refs = refs[1:1 + n_mix]
    wo_ref, g_ref, wg_ref, wu_ref, wd_ref, o_ref, act_ref = refs[1 + n_mix:]
    y = x_ref[...]
    off = 0
    for a_ref in a_refs:
        ka = a_ref.shape[1]
        y = y + jnp.dot(a_ref[...], wo_ref[off:off + ka, :], preferred_element_type=F32)
        off += ka
    h = _rms(y, g_ref[...]).astype(BF16)
    for f in range(D_FF // FF_TILE):
        cols = slice(f * FF_TILE, (f + 1) * FF_TILE)
        gate = jnp.dot(h, wg_ref[:, cols], preferred_element_type=F32)
        up = jnp.dot(h, wu_ref[:, cols], preferred_element_type=F32)
        act_ref[:, cols] = (gate * _sigmoid(gate) * up).astype(BF16)
    o_ref[...] = y + jnp.dot(act_ref[...], wd_ref[...], preferred_element_type=F32)


def _mix_ffn(groups, w_out, oidx, layer, g_ffn, wg, wu, wd):
    d = w_out.shape[2]
    n_mix = len(groups[0]) - 1
    tm = min(ROW_TILE, max(g[0].shape[0] for g in groups))
    res = _row_groups_call(
        functools.partial(_mix_ffn_kernel, n_mix=n_mix), groups, [w_out, g_ffn, wg, wu, wd],
        [_layer_spec(w_out.shape[1:], oidx), _layer_spec((1, d), layer), _layer_spec(wg.shape[1:], layer),
         _layer_spec(wu.shape[1:], layer), _layer_spec(wd.shape[1:], layer)],
        [(d, F32, True)], [pltpu.VMEM((tm, D_FF), BF16)], "mix_ffn")
    return [r[0] for r in res]


def _head_ones():
    r = lax.broadcasted_iota(jnp.int32, (LANES, LANES), 0) // HEAD_DIM
    c = lax.broadcasted_iota(jnp.int32, (LANES, LANES), 1) // HEAD_DIM
    return jnp.where(r == c, 1.0, 0.0).astype(BF16)


def _qk_prep(x, g, ones, ct, sa, sb):
    x2 = x * x
    hi = x2.astype(BF16)
    lo = (x2 - hi.astype(F32)).astype(BF16)
    ss = jnp.dot(hi, ones, preferred_element_type=F32) + jnp.dot(lo, ones, preferred_element_type=F32)
    xn = x * lax.rsqrt(ss * (1.0 / HEAD_DIM) + EPS) * g
    return xn * ct + pltpu.roll(xn, LANES - ROT_DIM // 2, 1) * sa + pltpu.roll(xn, ROT_DIM // 2, 1) * sb


def _rope_tables(pos):
    half = ROT_DIM // 2
    inv = jnp.power(jnp.float32(ROPE_THETA), -jnp.arange(half, dtype=F32) / half)
    ang = pos.astype(F32)[:, None] * inv[None, :]
    cos, sin = jnp.cos(ang), jnp.sin(ang)
    n = pos.shape[0]
    one = jnp.ones((n, HEAD_DIM - ROT_DIM), F32)
    zero = jnp.zeros((n, HEAD_DIM - ROT_DIM), F32)
    z8 = jnp.zeros((n, half), F32)
    ct = jnp.concatenate([cos, cos, one], axis=1)
    sa = jnp.concatenate([-sin, z8, zero], axis=1)
    sb = jnp.concatenate([z8, sin, zero], axis=1)
    tile = lambda t: jnp.concatenate([t, t], axis=1)
    return tile(ct), tile(sa), tile(sb)


ATTN_SEQS = 4
ATTN_QCHUNKS = ATTN_WIDTH // LANES
ATTN_HEAD_ORDER = tuple(h * ATTN_QCHUNKS + j for j in range(ATTN_QCHUNKS) for h in range(KV_HEADS))


def _attn_prompt_kernel(q_ref, kv_ref, ct_ref, sa_ref, sb_ref, gq_ref, gk_ref, sink_ref,
                        o_ref, pk_ref, pv_ref, kprev, vprev, *, nb, layer, nseq):
    i = pl.program_id(1)

    @pl.when(i == 0)
    def _():
        kprev[...] = jnp.zeros_like(kprev)
        vprev[...] = jnp.zeros_like(vprev)

    ones = _head_ones()
    ct, sa, sb = ct_ref[...], sa_ref[...], sb_ref[...]
    r = lax.broadcasted_iota(jnp.int32, (WINDOW, 2 * WINDOW), 0)
    c = lax.broadcasted_iota(jnp.int32, (WINDOW, 2 * WINDOW), 1)
    rel = r + WINDOW - c
    mask = (rel >= 0) & (rel <= WINDOW) & ((c >= WINDOW) | (i > 0))
    lane = lax.broadcasted_iota(jnp.int32, (WINDOW, LANES), 1)
    group0 = lane < HEAD_DIM
    v_ones = jnp.ones((2 * WINDOW, LANES), BF16)
    nq = ATTN_QCHUNKS
    st = [dict() for _ in range(nseq)]

    def prep(sq):
        d = st[sq]
        kv = kv_ref[sq]
        d["kn"] = _qk_prep(kv[:, :KV_WIDTH], gk_ref[...], ones, ct, sa, sb)
        d["v"] = kv[:, KV_WIDTH:]
        d["qn"] = [_qk_prep(q_ref[sq, :, j * LANES:(j + 1) * LANES], gq_ref[...], ones, ct, sa, sb)
                   * (HEAD_DIM ** -0.5) for j in range(nq)]
        d["kcat"] = jnp.concatenate([kprev[sq], d["kn"]], axis=0).astype(BF16)
        d["vaug"] = jnp.concatenate([jnp.concatenate([vprev[sq], d["v"]], axis=0).astype(BF16), v_ones], axis=1)
        kprev[sq] = d["kn"]
        vprev[sq] = d["v"]

    def scores(sq, h):
        d = st[sq]
        keep = group0 if h == 0 else jnp.logical_not(group0)
        qs = jnp.concatenate([jnp.where(keep, qj, 0.0) for qj in d["qn"]], axis=0).astype(BF16)
        d["s", h] = lax.dot_general(qs, d["kcat"], (((1,), (1,)), ((), ())), preferred_element_type=F32)

    def softmax_pv(sq, h):
        d = st[sq]
        s = d.pop(("s", h))
        ps, corr = [], []
        for j in range(nq):
            sg = jnp.where(mask, s[j * WINDOW:(j + 1) * WINDOW], NEG_INF)
            sink = sink_ref[layer, h * nq + j]
            m = jnp.maximum(jnp.max(sg, axis=-1, keepdims=True), sink)
            ps.append(jnp.exp(sg - m).astype(BF16))
            corr.append(jnp.exp(sink - m))
        o = jnp.dot(jnp.concatenate(ps, axis=0), d["vaug"], preferred_element_type=F32)
        d["o", h] = [o[j * WINDOW:(j + 1) * WINDOW, :LANES] / (o[j * WINDOW:(j + 1) * WINDOW, LANES:] + corr[j])
                     for j in range(nq)]

    def finish(sq):
        d = st[sq]
        o_ref[sq] = jnp.concatenate([jnp.where(group0, d["o", 0][j], d["o", 1][j]) for j in range(nq)],
                                    axis=1).astype(BF16)

    for sq in range(nseq):
        prep(sq)
    for sq in range(nseq):
        scores(sq, 0)
        scores(sq, 1)
    for sq in range(nseq):
        softmax_pv(sq, 0)
        softmax_pv(sq, 1)
        finish(sq)

    @pl.when(i == nb - 1)
    def _():
        for sq in range(nseq):
            pk_ref[sq] = st[sq]["kn"]
            pv_ref[sq] = st[sq]["v"]


def _attn_prompt(proj, tables, gq, gk, sinks, layer, prev):
    bsz, seq, _ = proj.shape
    nb = seq // WINDOW
    nseq = ATTN_SEQS
    n_layers = gq.shape[0]
    tab = pl.BlockSpec((WINDOW, LANES), lambda b, i: (i, 0))
    prev, prev_specs, aliases = _alias_inputs(prev, 1)
    win = pl.BlockSpec((None, nseq, WINDOW, KV_WIDTH), lambda b, i: (layer, b, 0, 0))
    win_shape = jax.ShapeDtypeStruct((n_layers, bsz, WINDOW, KV_WIDTH), F32)
    return pl.pallas_call(
        _skip_refs(functools.partial(_attn_prompt_kernel, nb=nb, layer=layer, nseq=nseq), len(prev)),
        grid=(bsz // nseq, nb),
        in_specs=prev_specs + [
            pl.BlockSpec((nseq, WINDOW, ATTN_WIDTH), lambda b, i: (b, i, 0)),
            pl.BlockSpec((nseq, WINDOW, 2 * KV_WIDTH), lambda b, i: (b, i, EVEN_KV_BLOCK)),
            tab, tab, tab, _layer_spec((1, LANES), layer), _layer_spec((1, LANES), layer),
            pl.BlockSpec(memory_space=pltpu.SMEM)],
        out_specs=[pl.BlockSpec((nseq, WINDOW, ATTN_WIDTH), lambda b, i: (b, i, 0)), win, win],
        out_shape=[jax.ShapeDtypeStruct((bsz, seq, ATTN_WIDTH), BF16), win_shape, win_shape],
        input_output_aliases=aliases,
        scratch_shapes=[pltpu.VMEM((nseq, WINDOW, KV_WIDTH), F32), pltpu.VMEM((nseq, WINDOW, KV_WIDTH), F32)],
        compiler_params=_cparams("parallel", "arbitrary"),
        name="attn_prompt",
    )(*prev, proj, proj, *tables, gq, gk, sinks)


EVEN_U_BLOCK = ATTN_WIDTH // S5_WIDTH
EVEN_KV_BLOCK = (ATTN_WIDTH + S5_WIDTH) // (2 * KV_WIDTH)
KALL_ROWS = WINDOW + SUBLANES


def _attn_sample_kernel(q_ref, kv_ref, ck_ref, cv_ref, ct_ref, sa_ref, sb_ref, gq_ref, gk_ref, sink_ref,
                        o_ref, nk_ref, nv_ref, o_seq, *, bs, t_new, layer):
    ones = _head_ones()
    ct, sa, sb = ct_ref[...], sa_ref[...], sb_ref[...]
    kv = kv_ref[...]
    kn = _qk_prep(kv[:, :KV_WIDTH], gk_ref[...], ones, ct, sa, sb)
    v = kv[:, KV_WIDTH:]
    nq = ATTN_QCHUNKS
    qn = [_qk_prep(q_ref[:, j * LANES:(j + 1) * LANES], gq_ref[...], ones, ct, sa, sb) * (HEAD_DIM ** -0.5)
          for j in range(nq)]
    rows = nq * t_new
    r = lax.broadcasted_iota(jnp.int32, (rows, KALL_ROWS), 0)
    c = lax.broadcasted_iota(jnp.int32, (rows, KALL_ROWS), 1)
    t = r % t_new
    mask = (c >= t) & (c <= t + WINDOW)
    rj = lax.broadcasted_iota(jnp.int32, (rows, 1), 0) // t_new
    lane = lax.broadcasted_iota(jnp.int32, (t_new, LANES), 1)
    group0 = lane < HEAD_DIM
    pad = jnp.zeros((KALL_ROWS - WINDOW - t_new, KV_WIDTH), F32)
    ones_c = jnp.ones((LANES, WINDOW), BF16)
    ones_n = jnp.ones((KALL_ROWS - WINDOW, LANES), BF16)
    klane = lax.broadcasted_iota(jnp.int32, (KV_WIDTH, WINDOW), 1)
    zcols = jnp.zeros((KV_WIDTH, WINDOW - (KALL_ROWS - WINDOW)), F32)
    nt = (((1,), (1,)), ((), ()))

    def shifted(cache_t, new_rows):
        new_t = jnp.concatenate([jnp.concatenate([new_rows, pad], axis=0).T, zcols], axis=1)
        return jnp.where(klane >= WINDOW - t_new, pltpu.roll(new_t, WINDOW - t_new, 1),
                         pltpu.roll(cache_t, WINDOW - t_new, 1))

    def seq_rows(a, b):
        return jnp.concatenate([a[tt * bs + b:tt * bs + b + 1] for tt in range(t_new)], axis=0)

    sinks = []
    for h in range(KV_HEADS):
        sk = jnp.zeros((rows, 1), F32)
        for j in range(nq):
            sk = jnp.where(rj == j, sink_ref[layer, h * nq + j], sk)
        sinks.append(sk)

    st = [dict() for _ in range(bs)]
    for b in range(bs):
        d = st[b]
        ck, cv = ck_ref[b], cv_ref[b]
        kn_b, v_b = seq_rows(kn, b), seq_rows(v, b)
        nk_ref[b] = shifted(ck, kn_b)
        nv_ref[b] = shifted(cv, v_b)
        ckb = ck.astype(BF16)
        knb = jnp.concatenate([kn_b, pad], axis=0).astype(BF16)
        d["vc"] = jnp.concatenate([cv.astype(BF16), ones_c], axis=0)
        d["vn"] = jnp.concatenate([jnp.concatenate([v_b, pad], axis=0).astype(BF16), ones_n], axis=1)
        qb = [seq_rows(qj, b) for qj in qn]
        for h in range(KV_HEADS):
            keep = group0 if h == 0 else jnp.logical_not(group0)
            qs = jnp.concatenate([jnp.where(keep, q, 0.0) for q in qb], axis=0).astype(BF16)
            d["s", h] = jnp.concatenate([jnp.dot(qs, ckb, preferred_element_type=F32),
                                         lax.dot_general(qs, knb, nt, preferred_element_type=F32)], axis=1)
    for b in range(bs):
        d = st[b]
        for h in range(KV_HEADS):
            s = jnp.where(mask, d.pop(("s", h)), NEG_INF)
            m = jnp.maximum(jnp.max(s, axis=-1, keepdims=True), sinks[h])
            p = jnp.exp(s - m).astype(BF16)
            o = (lax.dot_general(p[:, :WINDOW], d["vc"], nt, preferred_element_type=F32)
                 + jnp.dot(p[:, WINDOW:], d["vn"], preferred_element_type=F32))
            d["o", h] = o[:, :LANES] / (o[:, LANES:] + jnp.exp(sinks[h] - m))
    for b in range(bs):
        d = st[b]
        o_b = jnp.concatenate([jnp.where(group0, d["o", 0][j * t_new:(j + 1) * t_new],
                                         d["o", 1][j * t_new:(j + 1) * t_new]) for j in range(nq)], axis=1)
        for tt in range(t_new):
            o_seq[tt * bs + b:tt * bs + b + 1, :] = o_b[tt:tt + 1]
    o_ref[...] = o_seq[...].astype(BF16)


def _attn_sample(proj, cache_k, cache_v, tables, gq, gk, sinks, t_new, layer, prev):
    n = proj.shape[0]
    bsz = n // t_new
    bs = SAMPLE_TILE
    rows = bs * t_new
    row = lambda i: (i, 0)
    cache = pl.BlockSpec((None, bs, KV_WIDTH, WINDOW), lambda i: (layer, i, 0, 0))
    prev, prev_specs, aliases = _alias_inputs(prev, 1)
    return pl.pallas_call(
        _skip_refs(functools.partial(_attn_sample_kernel, bs=bs, t_new=t_new, layer=layer), len(prev)),
        grid=(bsz // bs,),
        in_specs=prev_specs + [
            pl.BlockSpec((rows, ATTN_WIDTH), row),
            pl.BlockSpec((rows, 2 * KV_WIDTH), lambda i: (i, EVEN_KV_BLOCK)),
            cache, cache,
            _const_spec((rows, LANES)), _const_spec((rows, LANES)), _const_spec((rows, LANES)),
            _layer_spec((1, LANES), layer), _layer_spec((1, LANES), layer),
            pl.BlockSpec(memory_space=pltpu.SMEM)],
        out_specs=[pl.BlockSpec((rows, ATTN_WIDTH), row), cache, cache],
        out_shape=[jax.ShapeDtypeStruct((n, ATTN_WIDTH), BF16),
                   jax.ShapeDtypeStruct(cache_k.shape, F32), jax.ShapeDtypeStruct(cache_v.shape, F32)],
        input_output_aliases=aliases,
        scratch_shapes=[pltpu.VMEM((rows, ATTN_WIDTH), F32)],
        compiler_params=_cparams("parallel"),
        name="attn_sample",
    )(*prev, proj, proj, cache_k, cache_v, *tables, gq, gk, sinks)


S5_UCHUNKS = S5_WIDTH // LANES
S5_SUB = S5_FLAT // S5_UCHUNKS
S5_SCHUNKS = S5_FLAT // LANES


def _s5_tail(y, wglu_ref, bglu_ref):
    g = 0.5 * y * (1.0 + lax.erf(y * (2.0 ** -0.5)))
    z = jnp.dot(g.astype(BF16), wglu_ref[...], preferred_element_type=F32) + bglu_ref[...]
    return g * _sigmoid(z)


S5_PARTS = 2


def _s5_prompt_kernel(u_ref, wb_ref, wc_ref, lam_ref, d_ref, wglu_ref, bglu_ref,
                      o_ref, sr_ref, si_ref, xs, hst, *, nbatch, tc):
    rows = nbatch * tc
    prow, ptok = rows // S5_PARTS, tc // S5_PARTS

    @pl.when(pl.program_id(1) == 0)
    def _():
        hst[...] = jnp.zeros_like(hst)

    u = jnp.swapaxes(u_ref[...], 0, 1).reshape(rows, S5_WIDTH)
    ub = u.astype(BF16)

    def in_proj(p, cc):
        rs = slice(p * prow, (p + 1) * prow)
        res = jnp.dot(ub[rs, cc * LANES:(cc + 1) * LANES], wb_ref[cc], preferred_element_type=F32)
        for j in range(S5_SUB // LANES):
            xs[cc * 4 + j, rs, :] = res[:, j * LANES:(j + 1) * LANES]
            xs[S5_SCHUNKS + cc * 4 + j, rs, :] = res[:, S5_SUB + j * LANES:S5_SUB + (j + 1) * LANES]

    ys = {}

    def out_proj(p, cc):
        rs = slice(p * prow, (p + 1) * prow)
        s = jnp.concatenate([xs[cc * 4 + j, rs, :] for j in range(4)]
                            + [xs[S5_SCHUNKS + cc * 4 + j, rs, :] for j in range(4)], axis=1).astype(BF16)
        cols = slice(cc * LANES, (cc + 1) * LANES)
        ys[p, cc] = jnp.dot(s, wc_ref[cc], preferred_element_type=F32) + d_ref[:, cols] * u[rs, cols]

    def tail(p):
        out = _s5_tail(jnp.concatenate([ys.pop((p, cc)) for cc in range(S5_UCHUNKS)], axis=1), wglu_ref, bglu_ref)
        o_ref[:, p * ptok:(p + 1) * ptok, :] = jnp.swapaxes(out.reshape(ptok, nbatch, S5_WIDTH), 0, 1).astype(BF16)

    def scan_step(t, h):
        idx = slice(t * nbatch, (t + 1) * nbatch)
        new = list(h)
        for k in range(S5_SCHUNKS):
            hr, hi = h[k], h[S5_SCHUNKS + k]
            lr, li = lam_ref[k], lam_ref[S5_SCHUNKS + k]
            nr = lr * hr - li * hi + xs[k, idx, :]
            ni = lr * hi + li * hr + xs[S5_SCHUNKS + k, idx, :]
            xs[k, idx, :] = nr
            xs[S5_SCHUNKS + k, idx, :] = ni
            new[k], new[S5_SCHUNKS + k] = nr, ni
        return new

    for cc in range(S5_UCHUNKS):
        in_proj(0, cc)
    h = [hst[k] for k in range(2 * S5_SCHUNKS)]
    for p in range(S5_PARTS):
        work = []
        if p + 1 < S5_PARTS:
            work += [functools.partial(in_proj, p + 1, cc) for cc in range(S5_UCHUNKS)]
        if p >= 1:
            work += [functools.partial(out_proj, p - 1, cc) for cc in range(S5_UCHUNKS)]
            work.append(functools.partial(tail, p - 1))
        every = max(1, ptok // max(1, len(work)))
        for i in range(ptok):
            h = scan_step(p * ptok + i, h)
            if work and (i + 1) % every == 0:
                work.pop(0)()
        for w in work:
            w()
    for cc in range(S5_UCHUNKS):
        out_proj(S5_PARTS - 1, cc)
    tail(S5_PARTS - 1)
    for k in range(2 * S5_SCHUNKS):
        hst[k] = h[k]
    sr_ref[...] = jnp.concatenate(h[:S5_SCHUNKS], axis=1)
    si_ref[...] = jnp.concatenate(h[S5_SCHUNKS:], axis=1)


def _s5_prompt(proj, prm, layer, prev):
    bsz, seq, _ = proj.shape
    nbatch, tc = SUBLANES, S5_CHUNK
    n_layers = prm["wb"].shape[0]
    st = pl.BlockSpec((None, nbatch, S5_FLAT), lambda b, c: (layer, b, 0))
    st_shape = jax.ShapeDtypeStruct((n_layers, bsz, S5_FLAT), F32)
    prev, prev_specs, aliases = _alias_inputs(prev, 1)
    names = ("wb", "wc", "lam8", "d", "wglu", "bglu")
    return pl.pallas_call(
        _skip_refs(functools.partial(_s5_prompt_kernel, nbatch=nbatch, tc=tc), len(prev)),
        grid=(bsz // nbatch, seq // tc),
        in_specs=prev_specs + [pl.BlockSpec((nbatch, tc, S5_WIDTH), lambda b, c: (b, c, EVEN_U_BLOCK))]
        + [_layer_spec(prm[k].shape[1:], layer) for k in names],
        out_specs=[pl.BlockSpec((nbatch, tc, S5_WIDTH), lambda b, c: (b, c, 0)), st, st],
        out_shape=[jax.ShapeDtypeStruct((bsz, seq, S5_WIDTH), BF16), st_shape, st_shape],
        input_output_aliases=aliases,
        scratch_shapes=[pltpu.VMEM((2 * S5_SCHUNKS, nbatch * tc, LANES), F32),
                        pltpu.VMEM((2 * S5_SCHUNKS, nbatch, LANES), F32)],
        compiler_params=_cparams("parallel", "arbitrary"),
        name="s5_prompt",
    )(*prev, proj, *[prm[k] for k in names])


def _s5_sample_kernel(u_ref, wb_ref, wc_ref, lr_ref, li_ref, d_ref, wglu_ref, bglu_ref, h0r_ref, h0i_ref,
                      o_ref, sr_ref, si_ref, xr, xi, *, nseq, t_new):
    nt, st = nseq // SAMPLE_TILE, SAMPLE_TILE
    n = nseq * t_new
    u = u_ref[...]
    ub = u.astype(BF16)
    for cc in range(S5_UCHUNKS):
        res = jnp.dot(ub[:, cc * LANES:(cc + 1) * LANES], wb_ref[cc], preferred_element_type=F32)
        sc = slice(cc * S5_SUB, (cc + 1) * S5_SUB)
        xr[:, :, :, sc] = res[:, :S5_SUB].reshape(nt, t_new, st, S5_SUB)
        xi[:, :, :, sc] = res[:, S5_SUB:].reshape(nt, t_new, st, S5_SUB)
    lr, li = lr_ref[...], li_ref[...]
    hr, hi = h0r_ref[...], h0i_ref[...]
    for t in range(t_new):
        nr = lr * hr - li * hi + xr[:, t].reshape(nseq, S5_FLAT)
        ni = lr * hi + li * hr + xi[:, t].reshape(nseq, S5_FLAT)
        xr[:, t] = nr.reshape(nt, st, S5_FLAT)
        xi[:, t] = ni.reshape(nt, st, S5_FLAT)
        hr, hi = nr, ni
    sr_ref[...] = hr
    si_ref[...] = hi
    ys = []
    for cc in range(S5_UCHUNKS):
        sc = slice(cc * S5_SUB, (cc + 1) * S5_SUB)
        s = jnp.concatenate([xr[:, :, :, sc].reshape(n, S5_SUB), xi[:, :, :, sc].reshape(n, S5_SUB)],
                            axis=1).astype(BF16)
        cols = slice(cc * LANES, (cc + 1) * LANES)
        ys.append(jnp.dot(s, wc_ref[cc], preferred_element_type=F32) + d_ref[:, cols] * u[:, cols])
    o_ref[...] = _s5_tail(jnp.concatenate(ys, axis=1), wglu_ref, bglu_ref).astype(BF16)


def _s5_sample(proj, h0r, h0i, prm, t_new, layer, prev):
    n = proj.shape[0]
    nseq = n // t_new
    names = ("wb", "wc", "lr", "li", "d", "wglu", "bglu")
    st = pl.BlockSpec((None, nseq, S5_FLAT), lambda i: (layer, 0, 0))
    prev, prev_specs, aliases = _alias_inputs(prev, 1)
    scratch = pltpu.VMEM((nseq // SAMPLE_TILE, t_new, SAMPLE_TILE, S5_FLAT), F32)
    return pl.pallas_call(
        _skip_refs(functools.partial(_s5_sample_kernel, nseq=nseq, t_new=t_new), len(prev)),
        grid=(1,),
        in_specs=prev_specs + [pl.BlockSpec((n, S5_WIDTH), lambda i: (0, EVEN_U_BLOCK))]
        + [_layer_spec(prm[k].shape[1:], layer) for k in names]
        + [_layer_spec((nseq, S5_FLAT), layer), _layer_spec((nseq, S5_FLAT), layer)],
        out_specs=[pl.BlockSpec((n, S5_WIDTH), lambda i: (0, 0)), st, st],
        out_shape=[jax.ShapeDtypeStruct((n, S5_WIDTH), BF16),
                   jax.ShapeDtypeStruct(h0r.shape, F32), jax.ShapeDtypeStruct(h0i.shape, F32)],
        input_output_aliases=aliases,
        scratch_shapes=[scratch, scratch],
        compiler_params=_cparams("arbitrary"),
        name="s5_sample",
    )(*prev, proj, *[prm[k] for k in names], h0r, h0i)


def _s5_params(a_re, a_im, log_dt, b_re, b_im, c_re, c_im, d_skip, w_glu, b_glu):
    nl = a_re.shape[0]
    dt = jnp.exp(log_dt)
    mag = jnp.exp(a_re * dt)
    lr, li = mag * jnp.cos(a_im * dt), mag * jnp.sin(a_im * dt)
    den = a_re * a_re + a_im * a_im
    cr = ((lr - 1.0) * a_re + li * a_im) / den
    ci = (li * a_re - (lr - 1.0) * a_im) / den
    bbr = cr[..., None] * b_re - ci[..., None] * b_im
    bbi = cr[..., None] * b_im + ci[..., None] * b_re
    gpc = LANES // S5_GROUP
    eye = jnp.eye(gpc, dtype=F32)

    def in_blocks(bb):
        bb = bb.reshape(nl, S5_UCHUNKS, gpc, S5_STATE, S5_GROUP)
        return jnp.einsum("lcgph,gk->lcghkp", bb, eye).reshape(nl, S5_UCHUNKS, LANES, S5_SUB)

    def out_blocks(cm):
        cm = cm.reshape(nl, S5_UCHUNKS, gpc, S5_GROUP, S5_STATE)
        return jnp.einsum("lcghp,gk->lcgpkh", cm, eye).reshape(nl, S5_UCHUNKS, S5_SUB, LANES)

    wb = jnp.concatenate([in_blocks(bbr), in_blocks(bbi)], axis=3).astype(BF16)
    wc = jnp.concatenate([out_blocks(c_re), -out_blocks(c_im)], axis=2).astype(BF16)
    lr_f, li_f = lr.reshape(nl, 1, S5_FLAT), li.reshape(nl, 1, S5_FLAT)
    lam = jnp.concatenate([lr_f.reshape(nl, S5_SCHUNKS, 1, LANES), li_f.reshape(nl, S5_SCHUNKS, 1, LANES)], axis=1)
    lam8 = jnp.broadcast_to(lam, (nl, 2 * S5_SCHUNKS, SUBLANES, LANES))
    return dict(wb=wb, wc=wc, lam8=lam8, lr=lr_f, li=li_f, d=d_skip.reshape(nl, 1, S5_WIDTH),
                wglu=w_glu.astype(BF16), bglu=b_glu.reshape(nl, 1, S5_WIDTH))


ML_AUG = 2 * ML_DV


EXP_CLAMP = 88.0


def _den_floor(m_row):
    return jnp.exp(jnp.minimum(-m_row, EXP_CLAMP))


def _head_out(h, o, gout):
    hn = h * lax.rsqrt(jnp.mean(h * h, axis=-1, keepdims=True) + EPS) * gout
    return (hn * _sigmoid(o)).astype(BF16)


ODDP_V_BLOCK = 0
ODDP_O_BLOCK = 1
ODDP_Q_BLOCK = (2 * ML_WIDTH) // ML_QK
ODDP_G_BLOCK = (2 * ML_WIDTH + ML_QK) // LANES
ML_SPLIT = 3
ML_PIECE_LANES = 2 * ML_HEADS
ML_SEQS = 4
ML_STAGE_LAG = 2


def _norm_matmul_kt_kernel(x_ref, g_ref, wt_ref, wg_ref, o_ref, kt_ref):
    h = _rms(x_ref[...], g_ref[...]).astype(BF16)
    k0, v0, g0 = ML_QK, 2 * ML_QK, 2 * ML_QK + 2 * ML_WIDTH
    nt = (((1,), (1,)), ((), ()))
    o_ref[:, :g0 - v0] = lax.dot_general(h, wt_ref[v0:g0, :].astype(BF16), nt, preferred_element_type=F32)
    o_ref[:, g0 - v0:g0 - v0 + k0] = lax.dot_general(h, wt_ref[:k0, :].astype(BF16), nt, preferred_element_type=F32)
    o_ref[:, g0 - v0 + k0:] = jnp.dot(h, wg_ref[...], preferred_element_type=F32)
    kt = lax.dot_general(wt_ref[k0:v0, :].astype(BF16), h, nt, preferred_element_type=F32)
    kt_ref[...] = kt * (ML_DK ** -0.5)


def _norm_matmul_kt(xs, g, layer, wt, wg, widx):
    d = wt.shape[2]
    m_out = 2 * ML_WIDTH + ML_QK + wg.shape[2]
    return _row_groups_call(
        _norm_matmul_kt_kernel, [[x] for x in xs], [g, wt, wg],
        [_layer_spec((1, d), layer), _layer_spec(wt.shape[1:], widx), _layer_spec(wg.shape[1:], widx)],
        [(m_out, F32, True), (ML_QK, F32, False)], [], "norm_matmul_kt")


def _cummax_rows(x):
    n = x.shape[0]
    row = lax.broadcasted_iota(jnp.int32, x.shape, 0)
    shift = 1
    while shift < n:
        x = jnp.maximum(x, jnp.where(row >= shift, pltpu.roll(x, shift, 0), NEG_INF))
        shift *= 2
    return x


def _pieces(x):
    lane = lax.broadcasted_iota(jnp.int32, x.shape, 1)
    xx = x + pltpu.roll(x, ML_PIECE_LANES, 1) + pltpu.roll(x, 2 * ML_PIECE_LANES, 1)
    a1, a2, a3 = _split3(xx)
    return jnp.where(lane < ML_PIECE_LANES, a1, jnp.where(lane < 2 * ML_PIECE_LANES, a2, a3))


def _ml_select_constants():
    mask = np.zeros((ML_HEADS, LANES), np.float32)
    sel = np.zeros((ML_HEADS, LANES, 2 * ML_DV), np.float32)
    for h in range(ML_HEADS):
        for k in range(ML_SPLIT):
            lo, hi = k * ML_PIECE_LANES + h, k * ML_PIECE_LANES + ML_HEADS + h
            mask[h, lo] = mask[h, hi] = 1.0
            sel[h, lo, :ML_DV] = 1.0
            sel[h, hi, ML_DV:] = 1.0
    return jnp.asarray(mask), jnp.asarray(sel, dtype=BF16)


def _mlstm_prompt_kernel(*refs, tc, nchunks, nseq):
    v_ref, o_ref, q_ref, g_ref = refs[:4]
    kt_refs = refs[4:4 + nseq]
    bias_ref, gout_ref, mask_ref, sel_ref, h_ref, c_ref, n_ref, m_ref, caug, mst = refs[4 + nseq:]
    ci = pl.program_id(1)

    @pl.when(ci == 0)
    def _():
        caug[...] = jnp.zeros_like(caug)
        mst[...] = jnp.zeros_like(mst)

    nh = ML_HEADS
    lane = lax.broadcasted_iota(jnp.int32, (tc, LANES), 1)
    lo, hi = lane < nh, (lane >= nh) & (lane < 2 * nh)
    rt = lax.broadcasted_iota(jnp.int32, (tc, tc), 0)
    cs = lax.broadcasted_iota(jnp.int32, (tc, tc), 1)
    causal = cs <= rt
    tril = jnp.where(causal, 1.0, 0.0).astype(BF16)
    ones = jnp.ones((tc, ML_DV), F32)

    def gates(sq):
        g = g_ref[sq] + bias_ref[...]
        lf = jnp.where(hi, _log_sigmoid(g), 0.0)
        b = sum(jnp.dot(tril, p, preferred_element_type=F32) for p in _split3(lf))
        c = jnp.where(hi, pltpu.roll(g, nh, 1) - b, 0.0)
        m_prev = mst[sq]
        mx = jnp.maximum(_cummax_rows(c), m_prev)
        m_row = b + mx
        mx_lo = pltpu.roll(mx, LANES - nh, 1)
        w_inter = jnp.exp(pltpu.roll(m_prev, LANES - nh, 1) - mx_lo)
        mst[sq] = m_row[tc - 1:tc, :]
        return dict(xc=_pieces(jnp.where(lo, w_inter, jnp.where(hi, _den_floor(m_row), 0.0))),
                    lc=_pieces(jnp.where(lo, -mx_lo, jnp.where(hi, 1.0, 0.0))),
                    rc=_pieces(jnp.where(lo, 1.0, jnp.where(hi, c, 0.0))))

    gt = [gates(sq) for sq in range(nseq)]
    units = [(sq, hd) for hd in range(nh) for sq in range(nseq)]
    st = [dict() for _ in units]

    def stage1(u):
        sq, hd = units[u]
        d = st[u]
        rh = gt[sq]["rc"] * mask_ref[hd:hd + 1, :].astype(BF16)
        d["dmat"] = lax.dot_general(gt[sq]["lc"], rh, (((1,), (1,)), ((), ())), preferred_element_type=F32)
        d["wb"] = jnp.dot(gt[sq]["xc"], sel_ref[hd], preferred_element_type=F32)
        d["qh"] = q_ref[sq, :, hd * ML_DK:(hd + 1) * ML_DK]
        d["kt"] = kt_refs[sq][hd * ML_DK:(hd + 1) * ML_DK, :]
        d["qk"] = jnp.dot(d["qh"].astype(BF16), d["kt"].astype(BF16), preferred_element_type=F32)

    def stage2(u):
        sq, hd = units[u]
        d = st[u]
        cols = slice(hd * ML_DV, (hd + 1) * ML_DV)
        d["w"] = jnp.exp(jnp.where(causal, d["dmat"], NEG_INF))
        d["vaug"] = jnp.concatenate([v_ref[sq, :, cols], ones], axis=1).astype(BF16)
        d["cm"] = caug[sq, hd]
        lhs = jnp.concatenate([(d["qk"] * d["w"]).astype(BF16), (d["wb"][:, :ML_DK] * d["qh"]).astype(BF16)], axis=1)
        rhs = jnp.concatenate([d["vaug"], d["cm"].astype(BF16)], axis=0)
        d["both"] = jnp.dot(lhs, rhs, preferred_element_type=F32)
        kw = (d["kt"] * d["w"][tc - 1:tc, :]).astype(BF16)
        d["upd"] = jnp.dot(kw, d["vaug"], preferred_element_type=F32)

    def stage3(u):
        sq, hd = units[u]
        d = st[u]
        cols = slice(hd * ML_DV, (hd + 1) * ML_DV)
        both, wb = d["both"], d["wb"]
        h = both[:, :ML_DV] / jnp.maximum(jnp.abs(both[:, ML_DV:]), wb[:, ML_DV:])
        h_ref[sq, :, cols] = _head_out(h, o_ref[sq, :, cols], gout_ref[:, cols])
        decay = wb[tc - 1:tc, :ML_DV]
        caug[sq, hd] = jnp.concatenate([decay, decay], axis=1) * d["cm"] + d["upd"]
        d.clear()

    for step in range(len(units) + 2 * ML_STAGE_LAG):
        if step < len(units):
            stage1(step)
        if 0 <= step - ML_STAGE_LAG < len(units):
            stage2(step - ML_STAGE_LAG)
        if 0 <= step - 2 * ML_STAGE_LAG < len(units):
            stage3(step - 2 * ML_STAGE_LAG)

    @pl.when(ci == nchunks - 1)
    def _():
        c_ref[...] = caug[:, :, :, :ML_DV]
        n_ref[...] = caug[:, :, :, ML_DV:]
        m_ref[...] = mst[...]


def _kt_index(b, c, *, sq, nseq, nchunks):
    return 0, (b * nseq + sq) * nchunks + c


def _mlstm_prompt(proj, kt, bias, gout, consts, layer, prev):
    bsz, seq, _ = proj.shape
    tc, nseq = ML_CHUNK, ML_SEQS
    nchunks = seq // tc
    n_layers = bias.shape[0]
    mask, sel = consts
    blk = lambda w, j: pl.BlockSpec((nseq, tc, w), lambda b, c: (b, c, j))
    st = lambda shape: pl.BlockSpec((None, nseq) + shape, lambda b, c: (layer, b) + (0,) * len(shape))
    st_shape = lambda shape: jax.ShapeDtypeStruct((n_layers, bsz) + shape, F32)
    prev, prev_specs, aliases = _alias_inputs(prev, 1)
    return pl.pallas_call(
        _skip_refs(functools.partial(_mlstm_prompt_kernel, tc=tc, nchunks=nchunks, nseq=nseq), len(prev)),
        grid=(bsz // nseq, nchunks),
        in_specs=prev_specs + [
            blk(ML_WIDTH, ODDP_V_BLOCK), blk(ML_WIDTH, ODDP_O_BLOCK), blk(ML_QK, ODDP_Q_BLOCK),
            blk(LANES, ODDP_G_BLOCK)]
        + [pl.BlockSpec((ML_QK, tc), functools.partial(_kt_index, sq=sq, nseq=nseq, nchunks=nchunks))
           for sq in range(nseq)] + [
            _layer_spec((1, LANES), layer), _layer_spec((1, ML_WIDTH), layer),
            _const_spec(mask.shape), _const_spec(sel.shape)],
        out_specs=[blk(ML_WIDTH, 0), st((ML_HEADS, ML_DK, ML_DV)), st((ML_HEADS, ML_DK, ML_DV)), st((1, LANES))],
        out_shape=[jax.ShapeDtypeStruct((bsz, seq, ML_WIDTH), BF16),
                   st_shape((ML_HEADS, ML_DK, ML_DV)), st_shape((ML_HEADS, ML_DK, ML_DV)), st_shape((1, LANES))],
        input_output_aliases=aliases,
        scratch_shapes=[pltpu.VMEM((nseq, ML_HEADS, ML_DK, ML_AUG), F32), pltpu.VMEM((nseq, 1, LANES), F32)],
        compiler_params=_cparams("parallel", "arbitrary"),
        name="mlstm_prompt",
    )(*prev, proj, proj, proj, proj, *([kt] * nseq), bias, gout, mask, sel)


MLS_SEQS = 32


def _mlstm_sample_kernel(v_ref, o_ref, q_ref, g_ref, kt_ref, bias_ref, gout_ref, mask_ref, sel_ref,
                         c0_ref, n0_ref, m0_ref, h_ref, c_ref, n_ref, m_ref, *, nseq, t_new):
    nh, nt, st = ML_HEADS, nseq // SAMPLE_TILE, SAMPLE_TILE
    rows = nseq * t_new
    lane = lax.broadcasted_iota(jnp.int32, (rows, LANES), 1)
    lo, hi = lane < nh, (lane >= nh) & (lane < 2 * nh)
    tiles = lambda a: a.reshape(nt, t_new, st, a.shape[-1])
    flat = lambda a: a.reshape(rows, a.shape[-1])
    per_seq = lambda a: a.reshape(nseq, a.shape[-1])

    g = g_ref[...] + bias_ref[...]
    lf = tiles(jnp.where(hi, _log_sigmoid(g), 0.0))
    ig = tiles(jnp.where(hi, pltpu.roll(g, nh, 1), 0.0))
    m_prev = m0_ref[...].reshape(nt, st, LANES)
    bs, cs, ms = [], [], []
    b_run, m_run = None, m_prev
    for t in range(t_new):
        b_run = lf[:, t] if b_run is None else b_run + lf[:, t]
        c_t = ig[:, t] - b_run
        m_run = jnp.maximum(m_run, c_t)
        bs.append(b_run)
        cs.append(c_t)
        ms.append(m_run)
    stack = lambda xs: flat(jnp.stack(xs, axis=1))
    b, c, mx = stack(bs), stack(cs), stack(ms)
    m_prev_rows = stack([m_prev] * t_new)
    m_row = b + mx
    m_ref[...] = per_seq(bs[-1] + ms[-1])
    mx_lo = pltpu.roll(mx, LANES - nh, 1)
    w_inter = jnp.exp(pltpu.roll(m_prev_rows, LANES - nh, 1) - mx_lo)
    xc = _pieces(jnp.where(lo, w_inter, jnp.where(hi, _den_floor(m_row), 0.0)))
    lc = _pieces(jnp.where(lo, -mx_lo, jnp.where(hi, 1.0, 0.0)))
    rc = _pieces(jnp.where(lo, 1.0, jnp.where(hi, c, 0.0)))

    def seq_of(idx):
        return (idx // (t_new * st)) * st + idx % st, (idx % (t_new * st)) // st

    rt = lax.broadcasted_iota(jnp.int32, (rows, rows), 0)
    ct = lax.broadcasted_iota(jnp.int32, (rows, rows), 1)
    (rs, rtok), (cseq, ctok) = seq_of(rt), seq_of(ct)
    valid = (rs == cseq) & (ctok <= rtok)
    rq = lax.broadcasted_iota(jnp.int32, (rows, nseq * ML_DK), 0)
    cq = lax.broadcasted_iota(jnp.int32, (rows, nseq * ML_DK), 1)
    own_q = seq_of(rq)[0] == cq // ML_DK
    rk = lax.broadcasted_iota(jnp.int32, (nseq * ML_DK, rows), 0)
    ck = lax.broadcasted_iota(jnp.int32, (nseq * ML_DK, rows), 1)
    own_k = rk // ML_DK == seq_of(ck)[0]
    ones = jnp.ones((rows, ML_DV), F32)
    last = lambda a: per_seq(tiles(a)[:, t_new - 1])

    for hd in range(nh):
        cols = slice(hd * ML_DV, (hd + 1) * ML_DV)
        rh = rc * mask_ref[hd:hd + 1, :].astype(BF16)
        dmat = lax.dot_general(lc, rh, (((1,), (1,)), ((), ())), preferred_element_type=F32)
        wb = jnp.dot(xc, sel_ref[hd], preferred_element_type=F32)
        qh = q_ref[:, hd * ML_DK:(hd + 1) * ML_DK]
        kt = kt_ref[hd * ML_DK:(hd + 1) * ML_DK, :]
        ktb = kt.astype(BF16)
        w = jnp.exp(jnp.where(valid, dmat, NEG_INF))
        qk = jnp.dot(qh.astype(BF16), ktb, preferred_element_type=F32) * w
        vaug = jnp.concatenate([v_ref[:, cols], ones], axis=1).astype(BF16)
        po = jnp.dot(qk.astype(BF16), vaug, preferred_element_type=F32)
        wq = wb[:, :ML_DK] * qh
        wq2 = jnp.concatenate([wq, wq], axis=1)
        wq_bd = jnp.where(own_q, jnp.concatenate([wq2] * (nseq * ML_DK // LANES), axis=1), 0.0).astype(BF16)
        cstack = c0_ref[:, hd].reshape(nseq * ML_DK, ML_DV)
        num = po[:, :ML_DV] + jnp.dot(wq_bd, cstack.astype(BF16), preferred_element_type=F32)
        n0 = n0_ref[hd]
        n_rows = stack([n0.reshape(nt, st, ML_DK)] * t_new)
        den = po[:, ML_DV:] + jnp.sum(wq * n_rows, axis=-1, keepdims=True)
        h = num / jnp.maximum(jnp.abs(den), wb[:, ML_DV:])
        h_ref[:, cols] = _head_out(h, o_ref[:, cols], gout_ref[:, cols])
        w_last = last(w)
        decay = last(wb[:, :ML_DV])
        n_upd = lax.dot_general(w_last.astype(BF16), ktb, (((1,), (1,)), ((), ())), preferred_element_type=F32)
        n_ref[hd] = decay[:, :ML_DK] * n0 + n_upd
        wk = jnp.sum(w_last, axis=0, keepdims=True)
        kw_bd = jnp.where(own_k, jnp.concatenate([kt * wk] * nseq, axis=0), 0.0).astype(BF16)
        upd = jnp.dot(kw_bd, v_ref[:, cols].astype(BF16), preferred_element_type=F32)
        decay_rows = jnp.broadcast_to(decay[:, None, :], (nseq, ML_DK, ML_DV)).reshape(nseq * ML_DK, ML_DV)
        c_ref[:, hd] = (decay_rows * cstack + upd).reshape(nseq, ML_DK, ML_DV)


def _mlstm_sample(proj, kt, bias, gout, consts, c0, n0h, m0, t_new, layer, prev):
    n = proj.shape[0]
    bsz = n // t_new
    nseq = MLS_SEQS
    rows = nseq * t_new
    mask, sel = consts
    blk = lambda w, j: pl.BlockSpec((rows, w), lambda i: (i, j))
    cst = pl.BlockSpec((None, nseq, ML_HEADS, ML_DK, ML_DV), lambda i: (layer, i, 0, 0, 0))
    nst = pl.BlockSpec((None, ML_HEADS, nseq, ML_DK), lambda i: (layer, 0, i, 0))
    mst = pl.BlockSpec((None, nseq, LANES), lambda i: (layer, i, 0))
    prev, prev_specs, aliases = _alias_inputs(prev, 1)
    return pl.pallas_call(
        _skip_refs(functools.partial(_mlstm_sample_kernel, nseq=nseq, t_new=t_new), len(prev)),
        grid=(bsz // nseq,),
        in_specs=prev_specs + [
            blk(ML_WIDTH, ODDP_V_BLOCK), blk(ML_WIDTH, ODDP_O_BLOCK), blk(ML_QK, ODDP_Q_BLOCK),
            blk(LANES, ODDP_G_BLOCK), pl.BlockSpec((ML_QK, rows), lambda i: (0, i)),
            _layer_spec((1, LANES), layer), _layer_spec((1, ML_WIDTH), layer),
            _const_spec(mask.shape), _const_spec(sel.shape), cst, nst, mst],
        out_specs=[blk(ML_WIDTH, 0), cst, nst, mst],
        out_shape=[jax.ShapeDtypeStruct((n, ML_WIDTH), BF16),
                   jax.ShapeDtypeStruct(c0.shape, F32), jax.ShapeDtypeStruct(n0h.shape, F32),
                   jax.ShapeDtypeStruct(m0.shape, F32)],
        input_output_aliases=aliases,
        compiler_params=_cparams("parallel"),
        name="mlstm_sample",
    )(*prev, proj, proj, proj, proj, kt, bias, gout, mask, sel, c0, n0h, m0)


def _pad_lanes(x):
    return jnp.pad(x, [(0, 0)] * (x.ndim - 1) + [(0, LANES - x.shape[-1])])


def kernel(x_prompt, x_sample, cache_k, cache_v, state_ssm_re, state_ssm_im, state_mlstm_c, state_mlstm_n, state_mlstm_m, norm_mix, norm_ffn, w_in_even, q_norm, k_norm, attn_sinks, s5_a_re, s5_a_im, s5_log_dt, s5_b_re, s5_b_im, s5_c_re, s5_c_im, s5_d, s5_w_glu, s5_b_glu, w_out_even, w_in_odd, ml_b_i, ml_b_f, ml_out_norm, w_out_odd, w_gate, w_up, w_down):
    bp, lp, _ = x_prompt.shape
    bsm, ls, _ = x_sample.shape
    yp = x_prompt.reshape(bp * lp, D_MODEL)
    ys = x_sample.reshape(bsm // SAMPLE_TILE, SAMPLE_TILE, ls, D_MODEL).transpose(0, 2, 1, 3).reshape(bsm * ls, D_MODEL)
    tab_p = _rope_tables(jnp.arange(lp))
    tab_s = tuple(jnp.repeat(t, SAMPLE_TILE, axis=0) for t in _rope_tables(PAST_LEN + jnp.arange(ls)))
    n_even, n_odd = w_in_even.shape[0], w_in_odd.shape[0]

    g_mix = norm_mix.reshape(DEPTH, 1, D_MODEL)
    g_ffn = norm_ffn.reshape(DEPTH, 1, D_MODEL)
    wg, wu, wd = w_gate.astype(BF16), w_up.astype(BF16), w_down.astype(BF16)
    kv0, u0 = ATTN_WIDTH, ATTN_WIDTH + 2 * KV_WIDTH
    order = jnp.asarray(ATTN_HEAD_ORDER)
    wq = w_in_even[..., :kv0].reshape(n_even, D_MODEL, ATTN_HEADS, HEAD_DIM)[:, :, order].reshape(n_even, D_MODEL, kv0)
    w_in_e = jnp.concatenate([wq, w_in_even[..., u0:], w_in_even[..., kv0:u0]], axis=-1).astype(BF16)
    wo_attn = w_out_even[:, :kv0].reshape(n_even, ATTN_HEADS, HEAD_DIM, D_MODEL)[:, order].reshape(n_even, kv0, D_MODEL)
    w_out_e = jnp.concatenate([wo_attn, w_out_even[:, kv0:]], axis=1).astype(BF16)
    gq = jnp.tile(q_norm, (1, LANES // HEAD_DIM)).reshape(n_even, 1, LANES)
    gk = jnp.tile(k_norm, (1, LANES // HEAD_DIM)).reshape(n_even, 1, LANES)
    prm = _s5_params(s5_a_re, s5_a_im, s5_log_dt, s5_b_re, s5_b_im, s5_c_re, s5_c_im, s5_d, s5_w_glu, s5_b_glu)
    w_gates_o = _pad_lanes(w_in_odd[..., 2 * ML_QK + 2 * ML_WIDTH:]).astype(BF16)
    w_in_ot = jnp.swapaxes(w_in_odd, 1, 2)
    ml_consts = _ml_select_constants()
    w_out_o = w_out_odd.astype(BF16)
    ml_bias = _pad_lanes(jnp.concatenate([ml_b_i, ml_b_f], axis=-1)).reshape(n_odd, 1, LANES)
    ml_gout = ml_out_norm.reshape(n_odd, 1, ML_WIDTH)
    keys_last = lambda a: a.transpose(0, 1, 3, 4, 2).reshape(n_even, bsm, KV_WIDTH, WINDOW)
    ck, cv = keys_last(cache_k), keys_last(cache_v)
    h0r = state_ssm_re.reshape(n_even, bsm, S5_FLAT)
    h0i = state_ssm_im.reshape(n_even, bsm, S5_FLAT)
    n0h = jnp.swapaxes(state_mlstm_n, 1, 2)
    m0 = jnp.pad(state_mlstm_m, ((0, 0), (0, 0), (ML_HEADS, LANES - 2 * ML_HEADS)))

    p_attn = p_ssm = p_ml = s_attn = s_ssm = s_ml = None
    for layer in range(DEPTH):
        ffn = (layer, g_ffn, wg, wu, wd)
        if layer % 2 == 0:
            e = layer // 2
            proj_p, proj_s = _norm_matmul([yp, ys], g_mix, layer, w_in_e, e)
            proj3 = proj_p.reshape(bp, lp, -1)
            attn_p, *p_attn = _attn_prompt(proj3, tab_p, gq, gk, attn_sinks, e, p_attn)
            ssm_p, *p_ssm = _s5_prompt(proj3, prm, e, p_ssm)
            attn_s, *s_attn = _attn_sample(proj_s, ck, cv, tab_s, gq, gk, attn_sinks, ls, e, s_attn)
            ssm_s, *s_ssm = _s5_sample(proj_s, h0r, h0i, prm, ls, e, s_ssm)
            yp, ys = _mix_ffn([[yp, attn_p.reshape(bp * lp, -1), ssm_p.reshape(bp * lp, -1)], [ys, attn_s, ssm_s]],
                              w_out_e, e, *ffn)
        else:
            o = layer // 2
            (proj_p, kt_p), (proj_s, kt_s) = _norm_matmul_kt([yp, ys], g_mix, layer, w_in_ot, w_gates_o, o)
            hh_p, *p_ml = _mlstm_prompt(proj_p.reshape(bp, lp, -1), kt_p, ml_bias, ml_gout, ml_consts, o, p_ml)
            hh_s, *s_ml = _mlstm_sample(proj_s, kt_s, ml_bias, ml_gout, ml_consts, state_mlstm_c, n0h, m0, ls, o,
                                        s_ml)
            yp, ys = _mix_ffn([[yp, hh_p.reshape(bp * lp, -1)], [ys, hh_s]], w_out_o, o, *ffn)
    heads = lambda a: a.reshape(a.shape[:3] + (KV_HEADS, HEAD_DIM))
    groups = lambda a: a.reshape(a.shape[:2] + (S5_GROUPS, S5_STATE))
    keys_first = lambda a: a.reshape(a.shape[:2] + (KV_HEADS, HEAD_DIM, WINDOW)).transpose(0, 1, 4, 2, 3)
    ys = ys.reshape(bsm // SAMPLE_TILE, ls, SAMPLE_TILE, D_MODEL).transpose(0, 2, 1, 3).reshape(bsm, ls, D_MODEL)
    return (yp.reshape(bp, lp, D_MODEL), ys,
            heads(p_attn[0]), heads(p_attn[1]), groups(p_ssm[0]), groups(p_ssm[1]),
            p_ml[0], p_ml[1][..., 0], p_ml[2][:, :, 0, ML_HEADS:2 * ML_HEADS],
            keys_first(s_attn[0]), keys_first(s_attn[1]), groups(s_ssm[0]), groups(s_ssm[1]),
            s_ml[0], jnp.swapaxes(s_ml[1], 1, 2), s_ml[2][..., ML_HEADS:2 * ML_HEADS])
```

```python
import functools

import numpy as np

import jax
import jax.numpy as jnp
from jax import lax
from jax.experimental import pallas as pl
from jax.experimental.pallas import tpu as pltpu

F32 = jnp.float32
BF16 = jnp.bfloat16

D_MODEL = 1024
DEPTH = 4
PAST_LEN = 8192
WINDOW = 128
ATTN_HEADS = 8
KV_HEADS = 2
HEAD_DIM = 64
ATTN_WIDTH = ATTN_HEADS * HEAD_DIM
KV_WIDTH = KV_HEADS * HEAD_DIM
ROT_DIM = HEAD_DIM // 4
ROPE_THETA = 500000.0
S5_GROUP = 16
S5_WIDTH = D_MODEL // 2
S5_GROUPS = S5_WIDTH // S5_GROUP
S5_STATE = 64
S5_FLAT = S5_GROUPS * S5_STATE
ML_HEADS = 8
ML_DV = D_MODEL // ML_HEADS
ML_DK = ML_DV // 2
ML_QK = ML_HEADS * ML_DK
ML_WIDTH = ML_HEADS * ML_DV
D_FF = 2816
EPS = 1e-6

LANES = 128
SUBLANES = 8
ROW_TILE = 512
FF_TILE = 256
S5_CHUNK = 128
ML_CHUNK = 128
SAMPLE_TILE = SUBLANES
VMEM_LIMIT = 56 * 1024 * 1024

NEG_INF = float("-inf")


def _cparams(*sem):
    return pltpu.CompilerParams(dimension_semantics=sem, vmem_limit_bytes=VMEM_LIMIT)


def _const_spec(shape):
    zeros = (0,) * len(shape)
    return pl.BlockSpec(shape, lambda *_: zeros, pipeline_mode=pl.Buffered(1))


def _layer_spec(shape, layer):
    zeros = (0,) * len(shape)
    return pl.BlockSpec((None,) + tuple(shape), lambda *_: (layer,) + zeros, pipeline_mode=pl.Buffered(1))


def _skip_refs(body, n_skip):
    if n_skip == 0:
        return body

    def wrapped(*refs):
        return body(*refs[n_skip:])

    return wrapped


def _alias_inputs(prev, first_state_out):
    prev = () if prev is None else tuple(prev)
    specs = [pl.BlockSpec(memory_space=pl.ANY) for _ in prev]
    aliases = {i: first_state_out + i for i in range(len(prev))}
    return prev, specs, aliases


def _rms(x, g):
    ms = jnp.mean(x * x, axis=-1, keepdims=True)
    return x * lax.rsqrt(ms + EPS) * g


def _split3(a):
    a1 = a.astype(BF16)
    r1 = a - a1.astype(F32)
    a2 = r1.astype(BF16)
    a3 = (r1 - a2.astype(F32)).astype(BF16)
    return a1, a2, a3


def _log_sigmoid(x):
    return jnp.minimum(x, 0.0) - jnp.log(1.0 + jnp.exp(-jnp.abs(x)))


def _sigmoid(x):
    return 1.0 / (1.0 + jnp.exp(-x))


def _norm_matmul_kernel(x_ref, g_ref, w_ref, o_ref):
    h = _rms(x_ref[...], g_ref[...]).astype(BF16)
    o_ref[...] = jnp.dot(h, w_ref[...], preferred_element_type=F32)


def _row_groups_call(body, groups, consts, const_specs, out_defs, scratch_shapes, name):
    steps, tiles = [], []
    for arrays in groups:
        n = arrays[0].shape[0]
        tm = min(ROW_TILE, n)
        tiles.append(tm)
        steps.append(n // tm)
    offs = [sum(steps[:k]) for k in range(len(groups))]

    def local(k):
        return lambda i: jnp.clip(i - offs[k], 0, steps[k] - 1)

    in_specs, out_specs, out_shape, args = [], [], [], []
    for k, arrays in enumerate(groups):
        for a in arrays:
            in_specs.append(pl.BlockSpec((tiles[k], a.shape[1]), lambda i, f=local(k): (f(i), 0)))
            args.append(a)
    for k, arrays in enumerate(groups):
        n = arrays[0].shape[0]
        for width, dtype, by_rows in out_defs:
            if by_rows:
                out_specs.append(pl.BlockSpec((tiles[k], width), lambda i, f=local(k): (f(i), 0)))
                out_shape.append(jax.ShapeDtypeStruct((n, width), dtype))
            else:
                out_specs.append(pl.BlockSpec((width, tiles[k]), lambda i, f=local(k): (0, f(i))))
                out_shape.append(jax.ShapeDtypeStruct((width, n), dtype))
    n_in = [len(arrays) for arrays in groups]
    n_out = len(out_defs)

    def kern(*refs):
        i = pl.program_id(0)
        pos = 0
        ins = []
        for cnt in n_in:
            ins.append(refs[pos:pos + cnt])
            pos += cnt
        crefs = refs[pos:pos + len(consts)]
        pos += len(consts)
        outs = [refs[pos + k * n_out:pos + (k + 1) * n_out] for k in range(len(groups))]
        scratch = refs[pos + len(groups) * n_out:]
        for k in range(len(groups)):
            @pl.when((i >= offs[k]) & (i < offs[k] + steps[k]))
            def _(k=k):
                body(*ins[k], *crefs, *outs[k], *scratch)

    res = pl.pallas_call(
        kern,
        grid=(sum(steps),),
        in_specs=in_specs + list(const_specs),
        out_specs=out_specs,
        out_shape=out_shape,
        scratch_shapes=scratch_shapes,
        compiler_params=_cparams("arbitrary"),
        name=name,
    )(*args, *consts)
    return [res[k * n_out:(k + 1) * n_out] for k in range(len(groups))]


def _norm_matmul(xs, g, layer, w, widx):
    d, m = w.shape[1], w.shape[2]
    res = _row_groups_call(_norm_matmul_kernel, [[x] for x in xs], [g, w],
                           [_layer_spec((1, d), layer), _layer_spec((d, m), widx)],
                           [(m, F32, True)], [], "norm_matmul")
    return [r[0] for r in res]


def _mix_ffn_kernel(*refs, n_mix):
    x_ref = refs[0]
    a_refs = refs[1:1 + n_mix]
    wo_ref, g_ref, wg_ref, wu_ref, wd_ref, o_ref, act_ref = refs[1 + n_mix:]
    y = x_ref[...]
    off = 0
    for a_ref in a_refs:
        ka = a_ref.shape[1]
        y = y + jnp.dot(a_ref[...], wo_ref[off:off + ka, :], preferred_element_type=F32)
        off += ka
    h = _rms(y, g_ref[...]).astype(BF16)
    for f in range(D_FF // FF_TILE):
        cols = slice(f * FF_TILE, (f + 1) * FF_TILE)
        gate = jnp.dot(h, wg_ref[:, cols], preferred_element_type=F32)
        up = jnp.dot(h, wu_ref[:, cols], preferred_element_type=F32)
        act_ref[:, cols] = (gate * _sigmoid(gate) * up).astype(BF16)
    o_ref[...] = y + jnp.dot(act_ref[...], wd_ref[...], preferred_element_type=F32)


def _mix_ffn(groups, w_out, oidx, layer, g_ffn, wg, wu, wd):
    d = w_out.shape[2]
    n_mix = len(groups[0]) - 1
    tm = min(ROW_TILE, max(g[0].shape[0] for g in groups))
    res = _row_groups_call(
        functools.partial(_mix_ffn_kernel, n_mix=n_mix), groups, [w_out, g_ffn, wg, wu, wd],
        [_layer_spec(w_out.shape[1:], oidx), _layer_spec((1, d), layer), _layer_spec(wg.shape[1:], layer),
         _layer_spec(wu.shape[1:], layer), _layer_spec(wd.shape[1:], layer)],
        [(d, F32, True)], [pltpu.VMEM((tm, D_FF), BF16)], "mix_ffn")
    return [r[0] for r in res]


def _head_ones():
    r = lax.broadcasted_iota(jnp.int32, (LANES, LANES), 0) // HEAD_DIM
    c = lax.broadcasted_iota(jnp.int32, (LANES, LANES), 1) // HEAD_DIM
    return jnp.where(r == c, 1.0, 0.0).astype(BF16)


def _qk_prep(x, g, ones, ct, sa, sb):
    x2 = x * x
    hi = x2.astype(BF16)
    lo = (x2 - hi.astype(F32)).astype(BF16)
    ss = jnp.dot(hi, ones, preferred_element_type=F32) + jnp.dot(lo, ones, preferred_element_type=F32)
    xn = x * lax.rsqrt(ss * (1.0 / HEAD_DIM) + EPS) * g
    return xn * ct + pltpu.roll(xn, LANES - ROT_DIM // 2, 1) * sa + pltpu.roll(xn, ROT_DIM // 2, 1) * sb


def _rope_tables(pos):
    half = ROT_DIM // 2
    inv = jnp.power(jnp.float32(ROPE_THETA), -jnp.arange(half, dtype=F32) / half)
    ang = pos.astype(F32)[:, None] * inv[None, :]
    cos, sin = jnp.cos(ang), jnp.sin(ang)
    n = pos.shape[0]
    one = jnp.ones((n, HEAD_DIM - ROT_DIM), F32)
    zero = jnp.zeros((n, HEAD_DIM - ROT_DIM), F32)
    z8 = jnp.zeros((n, half), F32)
    ct = jnp.concatenate([cos, cos, one], axis=1)
    sa = jnp.concatenate([-sin, z8, zero], axis=1)
    sb = jnp.concatenate([z8, sin, zero], axis=1)
    tile = lambda t: jnp.concatenate([t, t], axis=1)
    return tile(ct), tile(sa), tile(sb)


ATTN_SEQS = 4
ATTN_QCHUNKS = ATTN_WIDTH // LANES
ATTN_HEAD_ORDER = tuple(h * ATTN_QCHUNKS + j for j in range(ATTN_QCHUNKS) for h in range(KV_HEADS))


def _attn_prompt_kernel(q_ref, kv_ref, ct_ref, sa_ref, sb_ref, gq_ref, gk_ref, sink_ref,
                        o_ref, pk_ref, pv_ref, kprev, vprev, *, nb, layer, nseq):
    i = pl.program_id(1)

    @pl.when(i == 0)
    def _():
        kprev[...] = jnp.zeros_like(kprev)
        vprev[...] = jnp.zeros_like(vprev)

    ones = _head_ones()
    ct, sa, sb = ct_ref[...], sa_ref[...], sb_ref[...]
    r = lax.broadcasted_iota(jnp.int32, (WINDOW, 2 * WINDOW), 0)
    c = lax.broadcasted_iota(jnp.int32, (WINDOW, 2 * WINDOW), 1)
    rel = r + WINDOW - c
    mask = (rel >= 0) & (rel <= WINDOW) & ((c >= WINDOW) | (i > 0))
    lane = lax.broadcasted_iota(jnp.int32, (WINDOW, LANES), 1)
    group0 = lane < HEAD_DIM
    v_ones = jnp.ones((2 * WINDOW, LANES), BF16)
    nq = ATTN_QCHUNKS
    st = [dict() for _ in range(nseq)]

    def prep(sq):
        d = st[sq]
        kv = kv_ref[sq]
        d["kn"] = _qk_prep(kv[:, :KV_WIDTH], gk_ref[...], ones, ct, sa, sb)
        d["v"] = kv[:, KV_WIDTH:]
        d["qn"] = [_qk_prep(q_ref[sq, :, j * LANES:(j + 1) * LANES], gq_ref[...], ones, ct, sa, sb)
                   * (HEAD_DIM ** -0.5) for j in range(nq)]
        d["kcat"] = jnp.concatenate([kprev[sq], d["kn"]], axis=0).astype(BF16)
        d["vaug"] = jnp.concatenate([jnp.concatenate([vprev[sq], d["v"]], axis=0).astype(BF16), v_ones], axis=1)
        kprev[sq] = d["kn"]
        vprev[sq] = d["v"]

    def scores(sq, h):
        d = st[sq]
        keep = group0 if h == 0 else jnp.logical_not(group0)
        qs = jnp.concatenate([jnp.where(keep, qj, 0.0) for qj in d["qn"]], axis=0).astype(BF16)
        d["s", h] = lax.dot_general(qs, d["kcat"], (((1,), (1,)), ((), ())), preferred_element_type=F32)

    def softmax_pv(sq, h):
        d = st[sq]
        s = d.pop(("s", h))
        ps, corr = [], []
        for j in range(nq):
            sg = jnp.where(mask, s[j * WINDOW:(j + 1) * WINDOW], NEG_INF)
            sink = sink_ref[layer, h * nq + j]
            m = jnp.maximum(jnp.max(sg, axis=-1, keepdims=True), sink)
            ps.append(jnp.exp(sg - m).astype(BF16))
            corr.append(jnp.exp(sink - m))
        o = jnp.dot(jnp.concatenate(ps, axis=0), d["vaug"], preferred_element_type=F32)
        d["o", h] = [o[j * WINDOW:(j + 1) * WINDOW, :LANES] / (o[j * WINDOW:(j + 1) * WINDOW, LANES:] + corr[j])
                     for j in range(nq)]

    def finish(sq):
        d = st[sq]
        o_ref[sq] = jnp.concatenate([jnp.where(group0, d["o", 0][j], d["o", 1][j]) for j in range(nq)],
                                    axis=1).astype(BF16)

    for sq in range(nseq):
        prep(sq)
    for sq in range(nseq):
        scores(sq, 0)
        scores(sq, 1)
    for sq in range(nseq):
        softmax_pv(sq, 0)
        softmax_pv(sq, 1)
        finish(sq)

    @pl.when(i == nb - 1)
    def _():
        for sq in range(nseq):
            pk_ref[sq] = st[sq]["kn"]
            pv_ref[sq] = st[sq]["v"]


def _attn_prompt(proj, tables, gq, gk, sinks, layer, prev):
    bsz, seq, _ = proj.shape
    nb = seq // WINDOW
    nseq = ATTN_SEQS
    n_layers = gq.shape[0]
    tab = pl.BlockSpec((WINDOW, LANES), lambda b, i: (i, 0))
    prev, prev_specs, aliases = _alias_inputs(prev, 1)
    win = pl.BlockSpec((None, nseq, WINDOW, KV_WIDTH), lambda b, i: (layer, b, 0, 0))
    win_shape = jax.ShapeDtypeStruct((n_layers, bsz, WINDOW, KV_WIDTH), F32)
    return pl.pallas_call(
        _skip_refs(functools.partial(_attn_prompt_kernel, nb=nb, layer=layer, nseq=nseq), len(prev)),
        grid=(bsz // nseq, nb),
        in_specs=prev_specs + [
            pl.BlockSpec((nseq, WINDOW, ATTN_WIDTH), lambda b, i: (b, i, 0)),
            pl.BlockSpec((nseq, WINDOW, 2 * KV_WIDTH), lambda b, i: (b, i, EVEN_KV_BLOCK)),
            tab, tab, tab, _layer_spec((1, LANES), layer), _layer_spec((1, LANES), layer),
            pl.BlockSpec(memory_space=pltpu.SMEM)],
        out_specs=[pl.BlockSpec((nseq, WINDOW, ATTN_WIDTH), lambda b, i: (b, i, 0)), win, win],
        out_shape=[jax.ShapeDtypeStruct((bsz, seq, ATTN_WIDTH), BF16), win_shape, win_shape],
        input_output_aliases=aliases,
        scratch_shapes=[pltpu.VMEM((nseq, WINDOW, KV_WIDTH), F32), pltpu.VMEM((nseq, WINDOW, KV_WIDTH), F32)],
        compiler_params=_cparams("parallel", "arbitrary"),
        name="attn_prompt",
    )(*prev, proj, proj, *tables, gq, gk, sinks)


EVEN_U_BLOCK = ATTN_WIDTH // S5_WIDTH
EVEN_KV_BLOCK = (ATTN_WIDTH + S5_WIDTH) // (2 * KV_WIDTH)
KALL_ROWS = WINDOW + SUBLANES


def _attn_sample_kernel(q_ref, kv_ref, ck_ref, cv_ref, ct_ref, sa_ref, sb_ref, gq_ref, gk_ref, sink_ref,
                        o_ref, nk_ref, nv_ref, o_seq, *, bs, t_new, layer):
    ones = _head_ones()
    ct, sa, sb = ct_ref[...], sa_ref[...], sb_ref[...]
    kv = kv_ref[...]
    kn = _qk_prep(kv[:, :KV_WIDTH], gk_ref[...], ones, ct, sa, sb)
    v = kv[:, KV_WIDTH:]
    nq = ATTN_QCHUNKS
    qn = [_qk_prep(q_ref[:, j * LANES:(j + 1) * LANES], gq_ref[...], ones, ct, sa, sb) * (HEAD_DIM ** -0.5)
          for j in range(nq)]
    rows = nq * t_new
    r = lax.broadcasted_iota(jnp.int32, (rows, KALL_ROWS), 0)
    c = lax.broadcasted_iota(jnp.int32, (rows, KALL_ROWS), 1)
    t = r % t_new
    mask = (c >= t) & (c <= t + WINDOW)
    rj = lax.broadcasted_iota(jnp.int32, (rows, 1), 0) // t_new
    lane = lax.broadcasted_iota(jnp.int32, (t_new, LANES), 1)
    group0 = lane < HEAD_DIM
    pad = jnp.zeros((KALL_ROWS - WINDOW - t_new, KV_WIDTH), F32)
    ones_c = jnp.ones((LANES, WINDOW), BF16)
    ones_n = jnp.ones((KALL_ROWS - WINDOW, LANES), BF16)
    klane = lax.broadcasted_iota(jnp.int32, (KV_WIDTH, WINDOW), 1)
    zcols = jnp.zeros((KV_WIDTH, WINDOW - (KALL_ROWS - WINDOW)), F32)
    nt = (((1,), (1,)), ((), ()))

    def shifted(cache_t, new_rows):
        new_t = jnp.concatenate([jnp.concatenate([new_rows, pad], axis=0).T, zcols], axis=1)
        return jnp.where(klane >= WINDOW - t_new, pltpu.roll(new_t, WINDOW - t_new, 1),
                         pltpu.roll(cache_t, WINDOW - t_new, 1))

    def seq_rows(a, b):
        return jnp.concatenate([a[tt * bs + b:tt * bs + b + 1] for tt in range(t_new)], axis=0)

    sinks = []
    for h in range(KV_HEADS):
        sk = jnp.zeros((rows, 1), F32)
        for j in range(nq):
            sk = jnp.where(rj == j, sink_ref[layer, h * nq + j], sk)
        sinks.append(sk)

    st = [dict() for _ in range(bs)]
    for b in range(bs):
        d = st[b]
        ck, cv = ck_ref[b], cv_ref[b]
        kn_b, v_b = seq_rows(kn, b), seq_rows(v, b)
        nk_ref[b] = shifted(ck, kn_b)
        nv_ref[b] = shifted(cv, v_b)
        ckb = ck.astype(BF16)
        knb = jnp.concatenate([kn_b, pad], axis=0).astype(BF16)
        d["vc"] = jnp.concatenate([cv.astype(BF16), ones_c], axis=0)
        d["vn"] = jnp.concatenate([jnp.concatenate([v_b, pad], axis=0).astype(BF16), ones_n], axis=1)
        qb = [seq_rows(qj, b) for qj in qn]
        for h in range(KV_HEADS):
            keep = group0 if h == 0 else jnp.logical_not(group0)
            qs = jnp.concatenate([jnp.where(keep, q, 0.0) for q in qb], axis=0).astype(BF16)
            d["s", h] = jnp.concatenate([jnp.dot(qs, ckb, preferred_element_type=F32),
                                         lax.dot_general(qs, knb, nt, preferred_element_type=F32)], axis=1)
    for b in range(bs):
        d = st[b]
        for h in range(KV_HEADS):
            s = jnp.where(mask, d.pop(("s", h)), NEG_INF)
            m = jnp.maximum(jnp.max(s, axis=-1, keepdims=True), sinks[h])
            p = jnp.exp(s - m).astype(BF16)
            o = (lax.dot_general(p[:, :WINDOW], d["vc"], nt, preferred_element_type=F32)
                 + jnp.dot(p[:, WINDOW:], d["vn"], preferred_element_type=F32))
            d["o", h] = o[:, :LANES] / (o[:, LANES:] + jnp.exp(sinks[h] - m))
    for b in range(bs):
        d = st[b]
        o_b = jnp.concatenate([jnp.where(group0, d["o", 0][j * t_new:(j + 1) * t_new],
                                         d["o", 1][j * t_new:(j + 1) * t_new]) for j in range(nq)], axis=1)
        for tt in range(t_new):
            o_seq[tt * bs + b:tt * bs + b + 1, :] = o_b[tt:tt + 1]
    o_ref[...] = o_seq[...].astype(BF16)


def _attn_sample(proj, cache_k, cache_v, tables, gq, gk, sinks, t_new, layer, prev):
    n = proj.shape[0]
    bsz = n // t_new
    bs = SAMPLE_TILE
    rows = bs * t_new
    row = lambda i: (i, 0)
    cache = pl.BlockSpec((None, bs, KV_WIDTH, WINDOW), lambda i: (layer, i, 0, 0))
    prev, prev_specs, aliases = _alias_inputs(prev, 1)
    return pl.pallas_call(
        _skip_refs(functools.partial(_attn_sample_kernel, bs=bs, t_new=t_new, layer=layer), len(prev)),
        grid=(bsz // bs,),
        in_specs=prev_specs + [
            pl.BlockSpec((rows, ATTN_WIDTH), row),
            pl.BlockSpec((rows, 2 * KV_WIDTH), lambda i: (i, EVEN_KV_BLOCK)),
            cache, cache,
            _const_spec((rows, LANES)), _const_spec((rows, LANES)), _const_spec((rows, LANES)),
            _layer_spec((1, LANES), layer), _layer_spec((1, LANES), layer),
            pl.BlockSpec(memory_space=pltpu.SMEM)],
        out_specs=[pl.BlockSpec((rows, ATTN_WIDTH), row), cache, cache],
        out_shape=[jax.ShapeDtypeStruct((n, ATTN_WIDTH), BF16),
                   jax.ShapeDtypeStruct(cache_k.shape, F32), jax.ShapeDtypeStruct(cache_v.shape, F32)],
        input_output_aliases=aliases,
        scratch_shapes=[pltpu.VMEM((rows, ATTN_WIDTH), F32)],
        compiler_params=_cparams("parallel"),
        name="attn_sample",
    )(*prev, proj, proj, cache_k, cache_v, *tables, gq, gk, sinks)


S5_UCHUNKS = S5_WIDTH // LANES
S5_SUB = S5_FLAT // S5_UCHUNKS
S5_SCHUNKS = S5_FLAT // LANES


def _s5_tail(y, wglu_ref, bglu_ref):
    g = 0.5 * y * (1.0 + lax.erf(y * (2.0 ** -0.5)))
    z = jnp.dot(g.astype(BF16), wglu_ref[...], preferred_element_type=F32) + bglu_ref[...]
    return g * _sigmoid(z)


S5_PARTS = 4


def _s5_prompt_kernel(u_ref, wb_ref, wc_ref, lam_ref, d_ref, wglu_ref, bglu_ref,
                      o_ref, sr_ref, si_ref, xs, hst, *, nbatch, tc):
    rows = nbatch * tc
    prow, ptok = rows // S5_PARTS, tc // S5_PARTS

    @pl.when(pl.program_id(1) == 0)
    def _():
        hst[...] = jnp.zeros_like(hst)

    u = jnp.swapaxes(u_ref[...], 0, 1).reshape(rows, S5_WIDTH)
    ub = u.astype(BF16)

    def in_proj(p, cc):
        rs = slice(p * prow, (p + 1) * prow)
        res = jnp.dot(ub[rs, cc * LANES:(cc + 1) * LANES], wb_ref[cc], preferred_element_type=F32)
        for j in range(S5_SUB // LANES):
            xs[cc * 4 + j, rs, :] = res[:, j * LANES:(j + 1) * LANES]
            xs[S5_SCHUNKS + cc * 4 + j, rs, :] = res[:, S5_SUB + j * LANES:S5_SUB + (j + 1) * LANES]

    ys = {}

    def out_proj(p, cc):
        rs = slice(p * prow, (p + 1) * prow)
        s = jnp.concatenate([xs[cc * 4 + j, rs, :] for j in range(4)]
                            + [xs[S5_SCHUNKS + cc * 4 + j, rs, :] for j in range(4)], axis=1).astype(BF16)
        cols = slice(cc * LANES, (cc + 1) * LANES)
        ys[p, cc] = jnp.dot(s, wc_ref[cc], preferred_element_type=F32) + d_ref[:, cols] * u[rs, cols]

    def tail(p):
        out = _s5_tail(jnp.concatenate([ys.pop((p, cc)) for cc in range(S5_UCHUNKS)], axis=1), wglu_ref, bglu_ref)
        o_ref[:, p * ptok:(p + 1) * ptok, :] = jnp.swapaxes(out.reshape(ptok, nbatch, S5_WIDTH), 0, 1).astype(BF16)

    def scan_step(t, h):
        idx = slice(t * nbatch, (t + 1) * nbatch)
        new = list(h)
        for k in range(S5_SCHUNKS):
            hr, hi = h[k], h[S5_SCHUNKS + k]
            lr, li = lam_ref[k], lam_ref[S5_SCHUNKS + k]
            nr = lr * hr - li * hi + xs[k, idx, :]
            ni = lr * hi + li * hr + xs[S5_SCHUNKS + k, idx, :]
            xs[k, idx, :] = nr
            xs[S5_SCHUNKS + k, idx, :] = ni
            new[k], new[S5_SCHUNKS + k] = nr, ni
        return new

    for cc in range(S5_UCHUNKS):
        in_proj(0, cc)
    h = [hst[k] for k in range(2 * S5_SCHUNKS)]
    for p in range(S5_PARTS):
        work = []
        if p + 1 < S5_PARTS:
            work += [functools.partial(in_proj, p + 1, cc) for cc in range(S5_UCHUNKS)]
        if p >= 1:
            work += [functools.partial(out_proj, p - 1, cc) for cc in range(S5_UCHUNKS)]
            work.append(functools.partial(tail, p - 1))
        every = max(1, ptok // max(1, len(work)))
        for i in range(ptok):
            h = scan_step(p * ptok + i, h)
            if work and (i + 1) % every == 0:
                work.pop(0)()
        for w in work:
            w()
    for cc in range(S5_UCHUNKS):
        out_proj(S5_PARTS - 1, cc)
    tail(S5_PARTS - 1)
    for k in range(2 * S5_SCHUNKS):
        hst[k] = h[k]
    sr_ref[...] = jnp.concatenate(h[:S5_SCHUNKS], axis=1)
    si_ref[...] = jnp.concatenate(h[S5_SCHUNKS:], axis=1)


def _s5_prompt(proj, prm, layer, prev):
    bsz, seq, _ = proj.shape
    nbatch, tc = SUBLANES, S5_CHUNK
    n_layers = prm["wb"].shape[0]
    st = pl.BlockSpec((None, nbatch, S5_FLAT), lambda b, c: (layer, b, 0))
    st_shape = jax.ShapeDtypeStruct((n_layers, bsz, S5_FLAT), F32)
    prev, prev_specs, aliases = _alias_inputs(prev, 1)
    names = ("wb", "wc", "lam8", "d", "wglu", "bglu")
    return pl.pallas_call(
        _skip_refs(functools.partial(_s5_prompt_kernel, nbatch=nbatch, tc=tc), len(prev)),
        grid=(bsz // nbatch, seq // tc),
        in_specs=prev_specs + [pl.BlockSpec((nbatch, tc, S5_WIDTH), lambda b, c: (b, c, EVEN_U_BLOCK))]
        + [_layer_spec(prm[k].shape[1:], layer) for k in names],
        out_specs=[pl.BlockSpec((nbatch, tc, S5_WIDTH), lambda b, c: (b, c, 0)), st, st],
        out_shape=[jax.ShapeDtypeStruct((bsz, seq, S5_WIDTH), BF16), st_shape, st_shape],
        input_output_aliases=aliases,
        scratch_shapes=[pltpu.VMEM((2 * S5_SCHUNKS, nbatch * tc, LANES), F32),
                        pltpu.VMEM((2 * S5_SCHUNKS, nbatch, LANES), F32)],
        compiler_params=_cparams("parallel", "arbitrary"),
        name="s5_prompt",
    )(*prev, proj, *[prm[k] for k in names])


def _s5_sample_kernel(u_ref, wb_ref, wc_ref, lr_ref, li_ref, d_ref, wglu_ref, bglu_ref, h0r_ref, h0i_ref,
                      o_ref, sr_ref, si_ref, xr, xi, *, nseq, t_new):
    nt, st = nseq // SAMPLE_TILE, SAMPLE_TILE
    n = nseq * t_new
    u = u_ref[...]
    ub = u.astype(BF16)
    for cc in range(S5_UCHUNKS):
        res = jnp.dot(ub[:, cc * LANES:(cc + 1) * LANES], wb_ref[cc], preferred_element_type=F32)
        sc = slice(cc * S5_SUB, (cc + 1) * S5_SUB)
        xr[:, :, :, sc] = res[:, :S5_SUB].reshape(nt, t_new, st, S5_SUB)
        xi[:, :, :, sc] = res[:, S5_SUB:].reshape(nt, t_new, st, S5_SUB)
    lr, li = lr_ref[...], li_ref[...]
    hr, hi = h0r_ref[...], h0i_ref[...]
    for t in range(t_new):
        nr = lr * hr - li * hi + xr[:, t].reshape(nseq, S5_FLAT)
        ni = lr * hi + li * hr + xi[:, t].reshape(nseq, S5_FLAT)
        xr[:, t] = nr.reshape(nt, st, S5_FLAT)
        xi[:, t] = ni.reshape(nt, st, S5_FLAT)
        hr, hi = nr, ni
    sr_ref[...] = hr
    si_ref[...] = hi
    ys = []
    for cc in range(S5_UCHUNKS):
        sc = slice(cc * S5_SUB, (cc + 1) * S5_SUB)
        s = jnp.concatenate([xr[:, :, :, sc].reshape(n, S5_SUB), xi[:, :, :, sc].reshape(n, S5_SUB)],
                            axis=1).astype(BF16)
        cols = slice(cc * LANES, (cc + 1) * LANES)
        ys.append(jnp.dot(s, wc_ref[cc], preferred_element_type=F32) + d_ref[:, cols] * u[:, cols])
    o_ref[...] = _s5_tail(jnp.concatenate(ys, axis=1), wglu_ref, bglu_ref).astype(BF16)


def _s5_sample(proj, h0r, h0i, prm, t_new, layer, prev):
    n = proj.shape[0]
    nseq = n // t_new
    names = ("wb", "wc", "lr", "li", "d", "wglu", "bglu")
    st = pl.BlockSpec((None, nseq, S5_FLAT), lambda i: (layer, 0, 0))
    prev, prev_specs, aliases = _alias_inputs(prev, 1)
    scratch = pltpu.VMEM((nseq // SAMPLE_TILE, t_new, SAMPLE_TILE, S5_FLAT), F32)
    return pl.pallas_call(
        _skip_refs(functools.partial(_s5_sample_kernel, nseq=nseq, t_new=t_new), len(prev)),
        grid=(1,),
        in_specs=prev_specs + [pl.BlockSpec((n, S5_WIDTH), lambda i: (0, EVEN_U_BLOCK))]
        + [_layer_spec(prm[k].shape[1:], layer) for k in names]
        + [_layer_spec((nseq, S5_FLAT), layer), _layer_spec((nseq, S5_FLAT), layer)],
        out_specs=[pl.BlockSpec((n, S5_WIDTH), lambda i: (0, 0)), st, st],
        out_shape=[jax.ShapeDtypeStruct((n, S5_WIDTH), BF16),
                   jax.ShapeDtypeStruct(h0r.shape, F32), jax.ShapeDtypeStruct(h0i.shape, F32)],
        input_output_aliases=aliases,
        scratch_shapes=[scratch, scratch],
        compiler_params=_cparams("arbitrary"),
        name="s5_sample",
    )(*prev, proj, *[prm[k] for k in names], h0r, h0i)


def _s5_params(a_re, a_im, log_dt, b_re, b_im, c_re, c_im, d_skip, w_glu, b_glu):
    nl = a_re.shape[0]
    dt = jnp.exp(log_dt)
    mag = jnp.exp(a_re * dt)
    lr, li = mag * jnp.cos(a_im * dt), mag * jnp.sin(a_im * dt)
    den = a_re * a_re + a_im * a_im
    cr = ((lr - 1.0) * a_re + li * a_im) / den
    ci = (li * a_re - (lr - 1.0) * a_im) / den
    bbr = cr[..., None] * b_re - ci[..., None] * b_im
    bbi = cr[..., None] * b_im + ci[..., None] * b_re
    gpc = LANES // S5_GROUP
    eye = jnp.eye(gpc, dtype=F32)

    def in_blocks(bb):
        bb = bb.reshape(nl, S5_UCHUNKS, gpc, S5_STATE, S5_GROUP)
        return jnp.einsum("lcgph,gk->lcghkp", bb, eye).reshape(nl, S5_UCHUNKS, LANES, S5_SUB)

    def out_blocks(cm):
        cm = cm.reshape(nl, S5_UCHUNKS, gpc, S5_GROUP, S5_STATE)
        return jnp.einsum("lcghp,gk->lcgpkh", cm, eye).reshape(nl, S5_UCHUNKS, S5_SUB, LANES)

    wb = jnp.concatenate([in_blocks(bbr), in_blocks(bbi)], axis=3).astype(BF16)
    wc = jnp.concatenate([out_blocks(c_re), -out_blocks(c_im)], axis=2).astype(BF16)
    lr_f, li_f = lr.reshape(nl, 1, S5_FLAT), li.reshape(nl, 1, S5_FLAT)
    lam = jnp.concatenate([lr_f.reshape(nl, S5_SCHUNKS, 1, LANES), li_f.reshape(nl, S5_SCHUNKS, 1, LANES)], axis=1)
    lam8 = jnp.broadcast_to(lam, (nl, 2 * S5_SCHUNKS, SUBLANES, LANES))
    return dict(wb=wb, wc=wc, lam8=lam8, lr=lr_f, li=li_f, d=d_skip.reshape(nl, 1, S5_WIDTH),
                wglu=w_glu.astype(BF16), bglu=b_glu.reshape(nl, 1, S5_WIDTH))


ML_AUG = 2 * ML_DV


EXP_CLAMP = 88.0


def _den_floor(m_row):
    return jnp.exp(jnp.minimum(-m_row, EXP_CLAMP))


def _head_out(h, o, gout):
    hn = h * lax.rsqrt(jnp.mean(h * h, axis=-1, keepdims=True) + EPS) * gout
    return (hn * _sigmoid(o)).astype(BF16)


ODDP_V_BLOCK = 0
ODDP_O_BLOCK = 1
ODDP_Q_BLOCK = (2 * ML_WIDTH) // ML_QK
ODDP_G_BLOCK = (2 * ML_WIDTH + ML_QK) // LANES
ML_SPLIT = 3
ML_PIECE_LANES = 2 * ML_HEADS
ML_SEQS = 8
ML_STAGE_LAG = 2


def _norm_matmul_kt_kernel(x_ref, g_ref, wt_ref, wg_ref, o_ref, kt_ref):
    h = _rms(x_ref[...], g_ref[...]).astype(BF16)
    k0, v0, g0 = ML_QK, 2 * ML_QK, 2 * ML_QK + 2 * ML_WIDTH
    nt = (((1,), (1,)), ((), ()))
    o_ref[:, :g0 - v0] = lax.dot_general(h, wt_ref[v0:g0, :].astype(BF16), nt, preferred_element_type=F32)
    o_ref[:, g0 - v0:g0 - v0 + k0] = lax.dot_general(h, wt_ref[:k0, :].astype(BF16), nt, preferred_element_type=F32)
    o_ref[:, g0 - v0 + k0:] = jnp.dot(h, wg_ref[...], preferred_element_type=F32)
    kt = lax.dot_general(wt_ref[k0:v0, :].astype(BF16), h, nt, preferred_element_type=F32)
    kt_ref[...] = kt * (ML_DK ** -0.5)


def _norm_matmul_kt(xs, g, layer, wt, wg, widx):
    d = wt.shape[2]
    m_out = 2 * ML_WIDTH + ML_QK + wg.shape[2]
    return _row_groups_call(
        _norm_matmul_kt_kernel, [[x] for x in xs], [g, wt, wg],
        [_layer_spec((1, d), layer), _layer_spec(wt.shape[1:], widx), _layer_spec(wg.shape[1:], widx)],
        [(m_out, F32, True), (ML_QK, F32, False)], [], "norm_matmul_kt")


def _cummax_rows(x):
    n = x.shape[0]
    row = lax.broadcasted_iota(jnp.int32, x.shape, 0)
    shift = 1
    while shift < n:
        x = jnp.maximum(x, jnp.where(row >= shift, pltpu.roll(x, shift, 0), NEG_INF))
        shift *= 2
    return x


def _pieces(x):
    lane = lax.broadcasted_iota(jnp.int32, x.shape, 1)
    xx = x + pltpu.roll(x, ML_PIECE_LANES, 1) + pltpu.roll(x, 2 * ML_PIECE_LANES, 1)
    a1, a2, a3 = _split3(xx)
    return jnp.where(lane < ML_PIECE_LANES, a1, jnp.where(lane < 2 * ML_PIECE_LANES, a2, a3))


def _ml_select_constants():
    mask = np.zeros((ML_HEADS, LANES), np.float32)
    sel = np.zeros((ML_HEADS, LANES, 2 * ML_DV), np.float32)
    for h in range(ML_HEADS):
        for k in range(ML_SPLIT):
            lo, hi = k * ML_PIECE_LANES + h, k * ML_PIECE_LANES + ML_HEADS + h
            mask[h, lo] = mask[h, hi] = 1.0
            sel[h, lo, :ML_DV] = 1.0
            sel[h, hi, ML_DV:] = 1.0
    return jnp.asarray(mask), jnp.asarray(sel, dtype=BF16)


def _mlstm_prompt_kernel(*refs, tc, nchunks, nseq):
    v_ref, o_ref, q_ref, g_ref = refs[:4]
    kt_refs = refs[4:4 + nseq]
    bias_ref, gout_ref, mask_ref, sel_ref, h_ref, c_ref, n_ref, m_ref, caug, mst = refs[4 + nseq:]
    ci = pl.program_id(1)

    @pl.when(ci == 0)
    def _():
        caug[...] = jnp.zeros_like(caug)
        mst[...] = jnp.zeros_like(mst)

    nh = ML_HEADS
    lane = lax.broadcasted_iota(jnp.int32, (tc, LANES), 1)
    lo, hi = lane < nh, (lane >= nh) & (lane < 2 * nh)
    rt = lax.broadcasted_iota(jnp.int32, (tc, tc), 0)
    cs = lax.broadcasted_iota(jnp.int32, (tc, tc), 1)
    causal = cs <= rt
    tril = jnp.where(causal, 1.0, 0.0).astype(BF16)
    ones = jnp.ones((tc, ML_DV), F32)

    def gates(sq):
        g = g_ref[sq] + bias_ref[...]
        lf = jnp.where(hi, _log_sigmoid(g), 0.0)
        b = sum(jnp.dot(tril, p, preferred_element_type=F32) for p in _split3(lf))
        c = jnp.where(hi, pltpu.roll(g, nh, 1) - b, 0.0)
        m_prev = mst[sq]
        mx = jnp.maximum(_cummax_rows(c), m_prev)
        m_row = b + mx
        mx_lo = pltpu.roll(mx, LANES - nh, 1)
        w_inter = jnp.exp(pltpu.roll(m_prev, LANES - nh, 1) - mx_lo)
        mst[sq] = m_row[tc - 1:tc, :]
        return dict(xc=_pieces(jnp.where(lo, w_inter, jnp.where(hi, _den_floor(m_row), 0.0))),
                    lc=_pieces(jnp.where(lo, -mx_lo, jnp.where(hi, 1.0, 0.0))),
                    rc=_pieces(jnp.where(lo, 1.0, jnp.where(hi, c, 0.0))))

    gt = [gates(sq) for sq in range(nseq)]
    units = [(sq, hd) for hd in range(nh) for sq in range(nseq)]
    st = [dict() for _ in units]

    def stage1(u):
        sq, hd = units[u]
        d = st[u]
        rh = gt[sq]["rc"] * mask_ref[hd:hd + 1, :].astype(BF16)
        d["dmat"] = lax.dot_general(gt[sq]["lc"], rh, (((1,), (1,)), ((), ())), preferred_element_type=F32)
        d["wb"] = jnp.dot(gt[sq]["xc"], sel_ref[hd], preferred_element_type=F32)
        d["qh"] = q_ref[sq, :, hd * ML_DK:(hd + 1) * ML_DK]
        d["kt"] = kt_refs[sq][hd * ML_DK:(hd + 1) * ML_DK, :]
        d["qk"] = jnp.dot(d["qh"].astype(BF16), d["kt"].astype(BF16), preferred_element_type=F32)

    def stage2(u):
        sq, hd = units[u]
        d = st[u]
        cols = slice(hd * ML_DV, (hd + 1) * ML_DV)
        d["w"] = jnp.exp(jnp.where(causal, d["dmat"], NEG_INF))
        d["vaug"] = jnp.concatenate([v_ref[sq, :, cols], ones], axis=1).astype(BF16)
        d["cm"] = caug[sq, hd]
        lhs = jnp.concatenate([(d["qk"] * d["w"]).astype(BF16), (d["wb"][:, :ML_DK] * d["qh"]).astype(BF16)], axis=1)
        rhs = jnp.concatenate([d["vaug"], d["cm"].astype(BF16)], axis=0)
        d["both"] = jnp.dot(lhs, rhs, preferred_element_type=F32)
        kw = (d["kt"] * d["w"][tc - 1:tc, :]).astype(BF16)
        d["upd"] = jnp.dot(kw, d["vaug"], preferred_element_type=F32)

    def stage3(u):
        sq, hd = units[u]
        d = st[u]
        cols = slice(hd * ML_DV, (hd + 1) * ML_DV)
        both, wb = d["both"], d["wb"]
        h = both[:, :ML_DV] / jnp.maximum(jnp.abs(both[:, ML_DV:]), wb[:, ML_DV:])
        h_ref[sq, :, cols] = _head_out(h, o_ref[sq, :, cols], gout_ref[:, cols])
        decay = wb[tc - 1:tc, :ML_DV]
        caug[sq, hd] = jnp.concatenate([decay, decay], axis=1) * d["cm"] + d["upd"]
        d.clear()

    for step in range(len(units) + 2 * ML_STAGE_LAG):
        if step < len(units):
            stage1(step)
        if 0 <= step - ML_STAGE_LAG < len(units):
            stage2(step - ML_STAGE_LAG)
        if 0 <= step - 2 * ML_STAGE_LAG < len(units):
            stage3(step - 2 * ML_STAGE_LAG)

    @pl.when(ci == nchunks - 1)
    def _():
        c_ref[...] = caug[:, :, :, :ML_DV]
        n_ref[...] = caug[:, :, :, ML_DV:]
        m_ref[...] = mst[...]


def _kt_index(b, c, *, sq, nseq, nchunks):
    return 0, (b * nseq + sq) * nchunks + c


def _mlstm_prompt(proj, kt, bias, gout, consts, layer, prev):
    bsz, seq, _ = proj.shape
    tc, nseq = ML_CHUNK, ML_SEQS
    nchunks = seq // tc
    n_layers = bias.shape[0]
    mask, sel = consts
    blk = lambda w, j: pl.BlockSpec((nseq, tc, w), lambda b, c: (b, c, j))
    st = lambda shape: pl.BlockSpec((None, nseq) + shape, lambda b, c: (layer, b) + (0,) * len(shape))
    st_shape = lambda shape: jax.ShapeDtypeStruct((n_layers, bsz) + shape, F32)
    prev, prev_specs, aliases = _alias_inputs(prev, 1)
    return pl.pallas_call(
        _skip_refs(functools.partial(_mlstm_prompt_kernel, tc=tc, nchunks=nchunks, nseq=nseq), len(prev)),
        grid=(bsz // nseq, nchunks),
        in_specs=prev_specs + [
            blk(ML_WIDTH, ODDP_V_BLOCK), blk(ML_WIDTH, ODDP_O_BLOCK), blk(ML_QK, ODDP_Q_BLOCK),
            blk(LANES, ODDP_G_BLOCK)]
        + [pl.BlockSpec((ML_QK, tc), functools.partial(_kt_index, sq=sq, nseq=nseq, nchunks=nchunks))
           for sq in range(nseq)] + [
            _layer_spec((1, LANES), layer), _layer_spec((1, ML_WIDTH), layer),
            _const_spec(mask.shape), _const_spec(sel.shape)],
        out_specs=[blk(ML_WIDTH, 0), st((ML_HEADS, ML_DK, ML_DV)), st((ML_HEADS, ML_DK, ML_DV)), st((1, LANES))],
        out_shape=[jax.ShapeDtypeStruct((bsz, seq, ML_WIDTH), BF16),
                   st_shape((ML_HEADS, ML_DK, ML_DV)), st_shape((ML_HEADS, ML_DK, ML_DV)), st_shape((1, LANES))],
        input_output_aliases=aliases,
        scratch_shapes=[pltpu.VMEM((nseq, ML_HEADS, ML_DK, ML_AUG), F32), pltpu.VMEM((nseq, 1, LANES), F32)],
        compiler_params=_cparams("parallel", "arbitrary"),
        name="mlstm_prompt",
    )(*prev, proj, proj, proj, proj, *([kt] * nseq), bias, gout, mask, sel)


MLS_SEQS = 32


def _mlstm_sample_kernel(v_ref, o_ref, q_ref, g_ref, kt_ref, bias_ref, gout_ref, mask_ref, sel_ref,
                         c0_ref, n0_ref, m0_ref, h_ref, c_ref, n_ref, m_ref, *, nseq, t_new):
    nh, nt, st = ML_HEADS, nseq // SAMPLE_TILE, SAMPLE_TILE
    rows = nseq * t_new
    lane = lax.broadcasted_iota(jnp.int32, (rows, LANES), 1)
    lo, hi = lane < nh, (lane >= nh) & (lane < 2 * nh)
    tiles = lambda a: a.reshape(nt, t_new, st, a.shape[-1])
    flat = lambda a: a.reshape(rows, a.shape[-1])
    per_seq = lambda a: a.reshape(nseq, a.shape[-1])

    g = g_ref[...] + bias_ref[...]
    lf = tiles(jnp.where(hi, _log_sigmoid(g), 0.0))
    ig = tiles(jnp.where(hi, pltpu.roll(g, nh, 1), 0.0))
    m_prev = m0_ref[...].reshape(nt, st, LANES)
    bs, cs, ms = [], [], []
    b_run, m_run = None, m_prev
    for t in range(t_new):
        b_run = lf[:, t] if b_run is None else b_run + lf[:, t]
        c_t = ig[:, t] - b_run
        m_run = jnp.maximum(m_run, c_t)
        bs.append(b_run)
        cs.append(c_t)
        ms.append(m_run)
    stack = lambda xs: flat(jnp.stack(xs, axis=1))
    b, c, mx = stack(bs), stack(cs), stack(ms)
    m_prev_rows = stack([m_prev] * t_new)
    m_row = b + mx
    m_ref[...] = per_seq(bs[-1] + ms[-1])
    mx_lo = pltpu.roll(mx, LANES - nh, 1)
    w_inter = jnp.exp(pltpu.roll(m_prev_rows, LANES - nh, 1) - mx_lo)
    xc = _pieces(jnp.where(lo, w_inter, jnp.where(hi, _den_floor(m_row), 0.0)))
    lc = _pieces(jnp.where(lo, -mx_lo, jnp.where(hi, 1.0, 0.0)))
    rc = _pieces(jnp.where(lo, 1.0, jnp.where(hi, c, 0.0)))

    def seq_of(idx):
        return (idx // (t_new * st)) * st + idx % st, (idx % (t_new * st)) // st

    rt = lax.broadcasted_iota(jnp.int32, (rows, rows), 0)
    ct = lax.broadcasted_iota(jnp.int32, (rows, rows), 1)
    (rs, rtok), (cseq, ctok) = seq_of(rt), seq_of(ct)
    valid = (rs == cseq) & (ctok <= rtok)
    rq = lax.broadcasted_iota(jnp.int32, (rows, nseq * ML_DK), 0)
    cq = lax.broadcasted_iota(jnp.int32, (rows, nseq * ML_DK), 1)
    own_q = seq_of(rq)[0] == cq // ML_DK
    rk = lax.broadcasted_iota(jnp.int32, (nseq * ML_DK, rows), 0)
    ck = lax.broadcasted_iota(jnp.int32, (nseq * ML_DK, rows), 1)
    own_k = rk // ML_DK == seq_of(ck)[0]
    ones = jnp.ones((rows, ML_DV), F32)
    last = lambda a: per_seq(tiles(a)[:, t_new - 1])

    for hd in range(nh):
        cols = slice(hd * ML_DV, (hd + 1) * ML_DV)
        rh = rc * mask_ref[hd:hd + 1, :].astype(BF16)
        dmat = lax.dot_general(lc, rh, (((1,), (1,)), ((), ())), preferred_element_type=F32)
        wb = jnp.dot(xc, sel_ref[hd], preferred_element_type=F32)
        qh = q_ref[:, hd * ML_DK:(hd + 1) * ML_DK]
        kt = kt_ref[hd * ML_DK:(hd + 1) * ML_DK, :]
        ktb = kt.astype(BF16)
        w = jnp.exp(jnp.where(valid, dmat, NEG_INF))
        qk = jnp.dot(qh.astype(BF16), ktb, preferred_element_type=F32) * w
        vaug = jnp.concatenate([v_ref[:, cols], ones], axis=1).astype(BF16)
        po = jnp.dot(qk.astype(BF16), vaug, preferred_element_type=F32)
        wq = wb[:, :ML_DK] * qh
        wq2 = jnp.concatenate([wq, wq], axis=1)
        wq_bd = jnp.where(own_q, jnp.concatenate([wq2] * (nseq * ML_DK // LANES), axis=1), 0.0).astype(BF16)
        cstack = c0_ref[:, hd].reshape(nseq * ML_DK, ML_DV)
        num = po[:, :ML_DV] + jnp.dot(wq_bd, cstack.astype(BF16), preferred_element_type=F32)
        n0 = n0_ref[hd]
        n_rows = stack([n0.reshape(nt, st, ML_DK)] * t_new)
        den = po[:, ML_DV:] + jnp.sum(wq * n_rows, axis=-1, keepdims=True)
        h = num / jnp.maximum(jnp.abs(den), wb[:, ML_DV:])
        h_ref[:, cols] = _head_out(h, o_ref[:, cols], gout_ref[:, cols])
        w_last = last(w)
        decay = last(wb[:, :ML_DV])
        n_upd = lax.dot_general(w_last.astype(BF16), ktb, (((1,), (1,)), ((), ())), preferred_element_type=F32)
        n_ref[hd] = decay[:, :ML_DK] * n0 + n_upd
        wk = jnp.sum(w_last, axis=0, keepdims=True)
        kw_bd = jnp.where(own_k, jnp.concatenate([kt * wk] * nseq, axis=0), 0.0).astype(BF16)
        upd = jnp.dot(kw_bd, v_ref[:, cols].astype(BF16), preferred_element_type=F32)
        decay_rows = jnp.broadcast_to(decay[:, None, :], (nseq, ML_DK, ML_DV)).reshape(nseq * ML_DK, ML_DV)
        c_ref[:, hd] = (decay_rows * cstack + upd).reshape(nseq, ML_DK, ML_DV)


def _mlstm_sample(proj, kt, bias, gout, consts, c0, n0h, m0, t_new, layer, prev):
    n = proj.shape[0]
    bsz = n // t_new
    nseq = MLS_SEQS
    rows = nseq * t_new
    mask, sel = consts
    blk = lambda w, j: pl.BlockSpec((rows, w), lambda i: (i, j))
    cst = pl.BlockSpec((None, nseq, ML_HEADS, ML_DK, ML_DV), lambda i: (layer, i, 0, 0, 0))
    nst = pl.BlockSpec((None, ML_HEADS, nseq, ML_DK), lambda i: (layer, 0, i, 0))
    mst = pl.BlockSpec((None, nseq, LANES), lambda i: (layer, i, 0))
    prev, prev_specs, aliases = _alias_inputs(prev, 1)
    return pl.pallas_call(
        _skip_refs(functools.partial(_mlstm_sample_kernel, nseq=nseq, t_new=t_new), len(prev)),
        grid=(bsz // nseq,),
        in_specs=prev_specs + [
            blk(ML_WIDTH, ODDP_V_BLOCK), blk(ML_WIDTH, ODDP_O_BLOCK), blk(ML_QK, ODDP_Q_BLOCK),
            blk(LANES, ODDP_G_BLOCK), pl.BlockSpec((ML_QK, rows), lambda i: (0, i)),
            _layer_spec((1, LANES), layer), _layer_spec((1, ML_WIDTH), layer),
            _const_spec(mask.shape), _const_spec(sel.shape), cst, nst, mst],
        out_specs=[blk(ML_WIDTH, 0), cst, nst, mst],
        out_shape=[jax.ShapeDtypeStruct((n, ML_WIDTH), BF16),
                   jax.ShapeDtypeStruct(c0.shape, F32), jax.ShapeDtypeStruct(n0h.shape, F32),
                   jax.ShapeDtypeStruct(m0.shape, F32)],
        input_output_aliases=aliases,
        compiler_params=_cparams("parallel"),
        name="mlstm_sample",
    )(*prev, proj, proj, proj, proj, kt, bias, gout, mask, sel, c0, n0h, m0)


def _pad_lanes(x):
    return jnp.pad(x, [(0, 0)] * (x.ndim - 1) + [(0, LANES - x.shape[-1])])


def kernel(x_prompt, x_sample, cache_k, cache_v, state_ssm_re, state_ssm_im, state_mlstm_c, state_mlstm_n, state_mlstm_m, norm_mix, norm_ffn, w_in_even, q_norm, k_norm, attn_sinks, s5_a_re, s5_a_im, s5_log_dt, s5_b_re, s5_b_im, s5_c_re, s5_c_im, s5_d, s5_w_glu, s5_b_glu, w_out_even, w_in_odd, ml_b_i, ml_b_f, ml_out_norm, w_out_odd, w_gate, w_up, w_down):
    bp, lp, _ = x_prompt.shape
    bsm, ls, _ = x_sample.shape
    yp = x_prompt.reshape(bp * lp, D_MODEL)
    ys = x_sample.reshape(bsm // SAMPLE_TILE, SAMPLE_TILE, ls, D_MODEL).transpose(0, 2, 1, 3).reshape(bsm * ls, D_MODEL)
    tab_p = _rope_tables(jnp.arange(lp))
    tab_s = tuple(jnp.repeat(t, SAMPLE_TILE, axis=0) for t in _rope_tables(PAST_LEN + jnp.arange(ls)))
    n_even, n_odd = w_in_even.shape[0], w_in_odd.shape[0]

    g_mix = norm_mix.reshape(DEPTH, 1, D_MODEL)
    g_ffn = norm_ffn.reshape(DEPTH, 1, D_MODEL)
    wg, wu, wd = w_gate.astype(BF16), w_up.astype(BF16), w_down.astype(BF16)
    kv0, u0 = ATTN_WIDTH, ATTN_WIDTH + 2 * KV_WIDTH
    order = jnp.asarray(ATTN_HEAD_ORDER)
    wq = w_in_even[..., :kv0].reshape(n_even, D_MODEL, ATTN_HEADS, HEAD_DIM)[:, :, order].reshape(n_even, D_MODEL, kv0)
    w_in_e = jnp.concatenate([wq, w_in_even[..., u0:], w_in_even[..., kv0:u0]], axis=-1).astype(BF16)
    wo_attn = w_out_even[:, :kv0].reshape(n_even, ATTN_HEADS, HEAD_DIM, D_MODEL)[:, order].reshape(n_even, kv0, D_MODEL)
    w_out_e = jnp.concatenate([wo_attn, w_out_even[:, kv0:]], axis=1).astype(BF16)
    gq = jnp.tile(q_norm, (1, LANES // HEAD_DIM)).reshape(n_even, 1, LANES)
    gk = jnp.tile(k_norm, (1, LANES // HEAD_DIM)).reshape(n_even, 1, LANES)
    prm = _s5_params(s5_a_re, s5_a_im, s5_log_dt, s5_b_re, s5_b_im, s5_c_re, s5_c_im, s5_d, s5_w_glu, s5_b_glu)
    w_gates_o = _pad_lanes(w_in_odd[..., 2 * ML_QK + 2 * ML_WIDTH:]).astype(BF16)
    w_in_ot = jnp.swapaxes(w_in_odd, 1, 2)
    ml_consts = _ml_select_constants()
    w_out_o = w_out_odd.astype(BF16)
    ml_bias = _pad_lanes(jnp.concatenate([ml_b_i, ml_b_f], axis=-1)).reshape(n_odd, 1, LANES)
    ml_gout = ml_out_norm.reshape(n_odd, 1, ML_WIDTH)
    keys_last = lambda a: a.transpose(0, 1, 3, 4, 2).reshape(n_even, bsm, KV_WIDTH, WINDOW)
    ck, cv = keys_last(cache_k), keys_last(cache_v)
    h0r = state_ssm_re.reshape(n_even, bsm, S5_FLAT)
    h0i = state_ssm_im.reshape(n_even, bsm, S5_FLAT)
    n0h = jnp.swapaxes(state_mlstm_n, 1, 2)
    m0 = jnp.pad(state_mlstm_m, ((0, 0), (0, 0), (ML_HEADS, LANES - 2 * ML_HEADS)))

    p_attn = p_ssm = p_ml = s_attn = s_ssm = s_ml = None
    for layer in range(DEPTH):
        ffn = (layer, g_ffn, wg, wu, wd)
        if layer % 2 == 0:
            e = layer // 2
            proj_p, proj_s = _norm_matmul([yp, ys], g_mix, layer, w_in_e, e)
            proj3 = proj_p.reshape(bp, lp, -1)
            attn_p, *p_attn = _attn_prompt(proj3, tab_p, gq, gk, attn_sinks, e, p_attn)
            ssm_p, *p_ssm = _s5_prompt(proj3, prm, e, p_ssm)
            attn_s, *s_attn = _attn_sample(proj_s, ck, cv, tab_s, gq, gk, attn_sinks, ls, e, s_attn)
            ssm_s, *s_ssm = _s5_sample(proj_s, h0r, h0i, prm, ls, e, s_ssm)
            yp, ys = _mix_ffn([[yp, attn_p.reshape(bp * lp, -1), ssm_p.reshape(bp * lp, -1)], [ys, attn_s, ssm_s]],
                              w_out_e, e, *ffn)
        else:
            o = layer // 2
            (proj_p, kt_p), (proj_s, kt_s) = _norm_matmul_kt([yp, ys], g_mix, layer, w_in_ot, w_gates_o, o)
            hh_p, *p_ml = _mlstm_prompt(proj_p.reshape(bp, lp, -1), kt_p, ml_bias, ml_gout, ml_consts, o, p_ml)
            hh_s, *s_ml = _mlstm_sample(proj_s, kt_s, ml_bias, ml_gout, ml_consts, state_mlstm_c, n0h, m0, ls, o,
                                        s_ml)
            yp, ys = _mix_ffn([[yp, hh_p.reshape(bp * lp, -1)], [ys, hh_s]], w_out_o, o, *ffn)
    heads = lambda a: a.reshape(a.shape[:3] + (KV_HEADS, HEAD_DIM))
    groups = lambda a: a.reshape(a.shape[:2] + (S5_GROUPS, S5_STATE))
    keys_first = lambda a: a.reshape(a.shape[:2] + (KV_HEADS, HEAD_DIM, WINDOW)).transpose(0, 1, 4, 2, 3)
    ys = ys.reshape(bsm // SAMPLE_TILE, ls, SAMPLE_TILE, D_MODEL).transpose(0, 2, 1, 3).reshape(bsm, ls, D_MODEL)
    return (yp.reshape(bp, lp, D_MODEL), ys,
            heads(p_attn[0]), heads(p_attn[1]), groups(p_ssm[0]), groups(p_ssm[1]),
            p_ml[0], p_ml[1][..., 0], p_ml[2][:, :, 0, ML_HEADS:2 * ML_HEADS],
            keys_first(s_attn[0]), keys_first(s_attn[1]), groups(s_ssm[0]), groups(s_ssm[1]),
            s_ml[0], jnp.swapaxes(s_ml[1], 1, 2), s_ml[2][..., ML_HEADS:2 * ML_HEADS])
```

```python
import functools

import numpy as np

import jax
import jax.numpy as jnp
from jax import lax
from jax.experimental import pallas as pl
from jax.experimental.pallas import tpu as pltpu

F32 = jnp.float32
BF16 = jnp.bfloat16

D_MODEL = 1024
DEPTH = 4
PAST_LEN = 8192
WINDOW = 128
ATTN_HEADS = 8
KV_HEADS = 2
HEAD_DIM = 64
ATTN_WIDTH = ATTN_HEADS * HEAD_DIM
KV_WIDTH = KV_HEADS * HEAD_DIM
ROT_DIM = HEAD_DIM // 4
ROPE_THETA = 500000.0
S5_GROUP = 16
S5_WIDTH = D_MODEL // 2
S5_GROUPS = S5_WIDTH // S5_GROUP
S5_STATE = 64
S5_FLAT = S5_GROUPS * S5_STATE
ML_HEADS = 8
ML_DV = D_MODEL // ML_HEADS
ML_DK = ML_DV // 2
ML_QK = ML_HEADS * ML_DK
ML_WIDTH = ML_HEADS * ML_DV
D_FF = 2816
EPS = 1e-6

LANES = 128
SUBLANES = 8
ROW_TILE = 512
FF_TILE = 256
FFN_PARTS = 2
S5_CHUNK = 128
ML_CHUNK = 128
SAMPLE_TILE = SUBLANES
VMEM_LIMIT = 56 * 1024 * 1024

NEG_INF = float("-inf")


def _cparams(*sem):
    return pltpu.CompilerParams(dimension_semantics=sem, vmem_limit_bytes=VMEM_LIMIT)


def _const_spec(shape):
    zeros = (0,) * len(shape)
    return pl.BlockSpec(shape, lambda *_: zeros, pipeline_mode=pl.Buffered(1))


def _layer_spec(shape, layer):
    zeros = (0,) * len(shape)
    return pl.BlockSpec((None,) + tuple(shape), lambda *_: (layer,) + zeros, pipeline_mode=pl.Buffered(1))


def _skip_refs(body, n_skip):
    if n_skip == 0:
        return body

    def wrapped(*refs):
        return body(*refs[n_skip:])

    return wrapped


def _alias_inputs(prev, first_state_out):
    prev = () if prev is None else tuple(prev)
    specs = [pl.BlockSpec(memory_space=pl.ANY) for _ in prev]
    aliases = {i: first_state_out + i for i in range(len(prev))}
    return prev, specs, aliases


def _rms(x, g):
    ms = jnp.mean(x * x, axis=-1, keepdims=True)
    return x * lax.rsqrt(ms + EPS) * g


def _split3(a):
    a1 = a.astype(BF16)
    r1 = a - a1.astype(F32)
    a2 = r1.astype(BF16)
    a3 = (r1 - a2.astype(F32)).astype(BF16)
    return a1, a2, a3


def _log_sigmoid(x):
    return jnp.minimum(x, 0.0) - jnp.log(1.0 + jnp.exp(-jnp.abs(x)))


def _sigmoid(x):
    return 1.0 / (1.0 + jnp.exp(-x))


def _norm_matmul_kernel(x_ref, g_ref, w_ref, o_ref):
    h = _rms(x_ref[...], g_ref[...]).astype(BF16)
    o_ref[...] = jnp.dot(h, w_ref[...], preferred_element_type=F32)


def _row_groups_call(body, groups, consts, const_specs, out_defs, scratch_shapes, name):
    steps, tiles = [], []
    for arrays in groups:
        n = arrays[0].shape[0]
        tm = min(ROW_TILE, n)
        tiles.append(tm)
        steps.append(n // tm)
    offs = [sum(steps[:k]) for k in range(len(groups))]

    def local(k):
        return lambda i: jnp.clip(i - offs[k], 0, steps[k] - 1)

    in_specs, out_specs, out_shape, args = [], [], [], []
    for k, arrays in enumerate(groups):
        for a in arrays:
            in_specs.append(pl.BlockSpec((tiles[k], a.shape[1]), lambda i, f=local(k): (f(i), 0)))
            args.append(a)
    for k, arrays in enumerate(groups):
        n = arrays[0].shape[0]
        for width, dtype, by_rows in out_defs:
            if by_rows:
                out_specs.append(pl.BlockSpec((tiles[k], width), lambda i, f=local(k): (f(i), 0)))
                out_shape.append(jax.ShapeDtypeStruct((n, width), dtype))
            else:
                out_specs.append(pl.BlockSpec((width, tiles[k]), lambda i, f=local(k): (0, f(i))))
                out_shape.append(jax.ShapeDtypeStruct((width, n), dtype))
    n_in = [len(arrays) for arrays in groups]
    n_out = len(out_defs)

    def kern(*refs):
        i = pl.program_id(0)
        pos = 0
        ins = []
        for cnt in n_in:
            ins.append(refs[pos:pos + cnt])
            pos += cnt
        crefs = refs[pos:pos + len(consts)]
        pos += len(consts)
        outs = [refs[pos + k * n_out:pos + (k + 1) * n_out] for k in range(len(groups))]
        scratch = refs[pos + len(groups) * n_out:]
        for k in range(len(groups)):
            @pl.when((i >= offs[k]) & (i < offs[k] + steps[k]))
            def _(k=k):
                body(*ins[k], *crefs, *outs[k], *scratch)

    res = pl.pallas_call(
        kern,
        grid=(sum(steps),),
        in_specs=in_specs + list(const_specs),
        out_specs=out_specs,
        out_shape=out_shape,
        scratch_shapes=scratch_shapes,
        compiler_params=_cparams("arbitrary"),
        name=name,
    )(*args, *consts)
    return [res[k * n_out:(k + 1) * n_out] for k in range(len(groups))]


def _norm_matmul(xs, g, layer, w, widx):
    d, m = w.shape[1], w.shape[2]
    res = _row_groups_call(_norm_matmul_kernel, [[x] for x in xs], [g, w],
                           [_layer_spec((1, d), layer), _layer_spec((d, m), widx)],
                           [(m, F32, True)], [], "norm_matmul")
    return [r[0] for r in res]


def _mix_ffn_kernel(*refs, n_mix):
    x_ref = refs[0]
    a_refs = refs[1:1 + n_mix]
    wo_ref, g_ref, wg_ref, wu_ref, wd_ref, o_ref, act_ref = refs[1 + n_mix:]
    rows = x_ref.shape[0]
    parts = [slice(p * rows // FFN_PARTS, (p + 1) * rows // FFN_PARTS) for p in range(FFN_PARTS)]
    ys = []
    for rs in parts:
        y = x_ref[rs, :]
        off = 0
        for a_ref in a_refs:
            ka = a_ref.shape[1]
            y = y + jnp.dot(a_ref[rs, :], wo_ref[off:off + ka, :], preferred_element_type=F32)
            off += ka
        ys.append(y)
    hs = [_rms(y, g_ref[...]).astype(BF16) for y in ys]
    for f in range(D_FF // FF_TILE):
        cols = slice(f * FF_TILE, (f + 1) * FF_TILE)
        for rs, h in zip(parts, hs):
            gate = jnp.dot(h, wg_ref[:, cols], preferred_element_type=F32)
            up = jnp.dot(h, wu_ref[:, cols], preferred_element_type=F32)
            act_ref[rs, cols] = (gate * _sigmoid(gate) * up).astype(BF16)
    down = jnp.dot(act_ref[...], wd_ref[...], preferred_element_type=F32)
    for rs, y in zip(parts, ys):
        o_ref[rs, :] = y + down[rs]


def _mix_ffn(groups, w_out, oidx, layer, g_ffn, wg, wu, wd):
    d = w_out.shape[2]
    n_mix = len(groups[0]) - 1
    tm = min(ROW_TILE, max(g[0].shape[0] for g in groups))
    res = _row_groups_call(
        functools.partial(_mix_ffn_kernel, n_mix=n_mix), groups, [w_out, g_ffn, wg, wu, wd],
        [_layer_spec(w_out.shape[1:], oidx), _layer_spec((1, d), layer), _layer_spec(wg.shape[1:], layer),
         _layer_spec(wu.shape[1:], layer), _layer_spec(wd.shape[1:], layer)],
        [(d, F32, True)], [pltpu.VMEM((tm, D_FF), BF16)], "mix_ffn")
    return [r[0] for r in res]


def _head_ones():
    r = lax.broadcasted_iota(jnp.int32, (LANES, LANES), 0) // HEAD_DIM
    c = lax.broadcasted_iota(jnp.int32, (LANES, LANES), 1) // HEAD_DIM
    return jnp.where(r == c, 1.0, 0.0).astype(BF16)


def _qk_prep(x, g, ones, ct, sa, sb):
    x2 = x * x
    hi = x2.astype(BF16)
    lo = (x2 - hi.astype(F32)).astype(BF16)
    ss = jnp.dot(hi, ones, preferred_element_type=F32) + jnp.dot(lo, ones, preferred_element_type=F32)
    xn = x * lax.rsqrt(ss * (1.0 / HEAD_DIM) + EPS) * g
    return xn * ct + pltpu.roll(xn, LANES - ROT_DIM // 2, 1) * sa + pltpu.roll(xn, ROT_DIM // 2, 1) * sb


def _rope_tables(pos):
    half = ROT_DIM // 2
    inv = jnp.power(jnp.float32(ROPE_THETA), -jnp.arange(half, dtype=F32) / half)
    ang = pos.astype(F32)[:, None] * inv[None, :]
    cos, sin = jnp.cos(ang), jnp.sin(ang)
    n = pos.shape[0]
    one = jnp.ones((n, HEAD_DIM - ROT_DIM), F32)
    zero = jnp.zeros((n, HEAD_DIM - ROT_DIM), F32)
    z8 = jnp.zeros((n, half), F32)
    ct = jnp.concatenate([cos, cos, one], axis=1)
    sa = jnp.concatenate([-sin, z8, zero], axis=1)
    sb = jnp.concatenate([z8, sin, zero], axis=1)
    tile = lambda t: jnp.concatenate([t, t], axis=1)
    return tile(ct), tile(sa), tile(sb)


ATTN_SEQS = 4
ATTN_QCHUNKS = ATTN_WIDTH // LANES
ATTN_HEAD_ORDER = tuple(h * ATTN_QCHUNKS + j for j in range(ATTN_QCHUNKS) for h in range(KV_HEADS))


def _attn_prompt_kernel(q_ref, kv_ref, ct_ref, sa_ref, sb_ref, gq_ref, gk_ref, sink_ref,
                        o_ref, pk_ref, pv_ref, kprev, vprev, *, nb, layer, nseq):
    i = pl.program_id(1)

    @pl.when(i == 0)
    def _():
        kprev[...] = jnp.zeros_like(kprev)
        vprev[...] = jnp.zeros_like(vprev)

    ones = _head_ones()
    ct, sa, sb = ct_ref[...], sa_ref[...], sb_ref[...]
    r = lax.broadcasted_iota(jnp.int32, (WINDOW, 2 * WINDOW), 0)
    c = lax.broadcasted_iota(jnp.int32, (WINDOW, 2 * WINDOW), 1)
    rel = r + WINDOW - c
    mask = (rel >= 0) & (rel <= WINDOW) & ((c >= WINDOW) | (i > 0))
    lane = lax.broadcasted_iota(jnp.int32, (WINDOW, LANES), 1)
    group0 = lane < HEAD_DIM
    v_ones = jnp.ones((2 * WINDOW, LANES), BF16)
    nq = ATTN_QCHUNKS
    st = [dict() for _ in range(nseq)]

    def prep(sq):
        d = st[sq]
        kv = kv_ref[sq]
        d["kn"] = _qk_prep(kv[:, :KV_WIDTH], gk_ref[...], ones, ct, sa, sb)
        d["v"] = kv[:, KV_WIDTH:]
        d["qn"] = [_qk_prep(q_ref[sq, :, j * LANES:(j + 1) * LANES], gq_ref[...], ones, ct, sa, sb)
                   * (HEAD_DIM ** -0.5) for j in range(nq)]
        d["kcat"] = jnp.concatenate([kprev[sq], d["kn"]], axis=0).astype(BF16)
        d["vaug"] = jnp.concatenate([jnp.concatenate([vprev[sq], d["v"]], axis=0).astype(BF16), v_ones], axis=1)
        kprev[sq] = d["kn"]
        vprev[sq] = d["v"]

    def scores(sq, h):
        d = st[sq]
        keep = group0 if h == 0 else jnp.logical_not(group0)
        qs = jnp.concatenate([jnp.where(keep, qj, 0.0) for qj in d["qn"]], axis=0).astype(BF16)
        d["s", h] = lax.dot_general(qs, d["kcat"], (((1,), (1,)), ((), ())), preferred_element_type=F32)

    def softmax_pv(sq, h):
        d = st[sq]
        s = d.pop(("s", h))
        ps, corr = [], []
        for j in range(nq):
            sg = jnp.where(mask, s[j * WINDOW:(j + 1) * WINDOW], NEG_INF)
            sink = sink_ref[layer, h * nq + j]
            m = jnp.maximum(jnp.max(sg, axis=-1, keepdims=True), sink)
            ps.append(jnp.exp(sg - m).astype(BF16))
            corr.append(jnp.exp(sink - m))
        o = jnp.dot(jnp.concatenate(ps, axis=0), d["vaug"], preferred_element_type=F32)
        d["o", h] = [o[j * WINDOW:(j + 1) * WINDOW, :LANES] / (o[j * WINDOW:(j + 1) * WINDOW, LANES:] + corr[j])
                     for j in range(nq)]

    def finish(sq):
        d = st[sq]
        o_ref[sq] = jnp.concatenate([jnp.where(group0, d["o", 0][j], d["o", 1][j]) for j in range(nq)],
                                    axis=1).astype(BF16)

    for sq in range(nseq):
        prep(sq)
    for sq in range(nseq):
        scores(sq, 0)
        scores(sq, 1)
    for sq in range(nseq):
        softmax_pv(sq, 0)
        softmax_pv(sq, 1)
        finish(sq)

    @pl.when(i == nb - 1)
    def _():
        for sq in range(nseq):
            pk_ref[sq] = st[sq]["kn"]
            pv_ref[sq] = st[sq]["v"]


def _attn_prompt(proj, tables, gq, gk, sinks, layer, prev):
    bsz, seq, _ = proj.shape
    nb = seq // WINDOW
    nseq = ATTN_SEQS
    n_layers = gq.shape[0]
    tab = pl.BlockSpec((WINDOW, LANES), lambda b, i: (i, 0))
    prev, prev_specs, aliases = _alias_inputs(prev, 1)
    win = pl.BlockSpec((None, nseq, WINDOW, KV_WIDTH), lambda b, i: (layer, b, 0, 0))
    win_shape = jax.ShapeDtypeStruct((n_layers, bsz, WINDOW, KV_WIDTH), F32)
    return pl.pallas_call(
        _skip_refs(functools.partial(_attn_prompt_kernel, nb=nb, layer=layer, nseq=nseq), len(prev)),
        grid=(bsz // nseq, nb),
        in_specs=prev_specs + [
            pl.BlockSpec((nseq, WINDOW, ATTN_WIDTH), lambda b, i: (b, i, 0)),
            pl.BlockSpec((nseq, WINDOW, 2 * KV_WIDTH), lambda b, i: (b, i, EVEN_KV_BLOCK)),
            tab, tab, tab, _layer_spec((1, LANES), layer), _layer_spec((1, LANES), layer),
            pl.BlockSpec(memory_space=pltpu.SMEM)],
        out_specs=[pl.BlockSpec((nseq, WINDOW, ATTN_WIDTH), lambda b, i: (b, i, 0)), win, win],
        out_shape=[jax.ShapeDtypeStruct((bsz, seq, ATTN_WIDTH), BF16), win_shape, win_shape],
        input_output_aliases=aliases,
        scratch_shapes=[pltpu.VMEM((nseq, WINDOW, KV_WIDTH), F32), pltpu.VMEM((nseq, WINDOW, KV_WIDTH), F32)],
        compiler_params=_cparams("parallel", "arbitrary"),
        name="attn_prompt",
    )(*prev, proj, proj, *tables, gq, gk, sinks)


EVEN_U_BLOCK = ATTN_WIDTH // S5_WIDTH
EVEN_KV_BLOCK = (ATTN_WIDTH + S5_WIDTH) // (2 * KV_WIDTH)
KALL_ROWS = WINDOW + SUBLANES


def _attn_sample_kernel(q_ref, kv_ref, ck_ref, cv_ref, ct_ref, sa_ref, sb_ref, gq_ref, gk_ref, sink_ref,
                        o_ref, nk_ref, nv_ref, o_seq, *, bs, t_new, layer):
    ones = _head_ones()
    ct, sa, sb = ct_ref[...], sa_ref[...], sb_ref[...]
    kv = kv_ref[...]
    kn = _qk_prep(kv[:, :KV_WIDTH], gk_ref[...], ones, ct, sa, sb)
    v = kv[:, KV_WIDTH:]
    nq = ATTN_QCHUNKS
    qn = [_qk_prep(q_ref[:, j * LANES:(j + 1) * LANES], gq_ref[...], ones, ct, sa, sb) * (HEAD_DIM ** -0.5)
          for j in range(nq)]
    rows = nq * t_new
    r = lax.broadcasted_iota(jnp.int32, (rows, KALL_ROWS), 0)
    c = lax.broadcasted_iota(jnp.int32, (rows, KALL_ROWS), 1)
    t = r % t_new
    mask = (c >= t) & (c <= t + WINDOW)
    rj = lax.broadcasted_iota(jnp.int32, (rows, 1), 0) // t_new
    lane = lax.broadcasted_iota(jnp.int32, (t_new, LANES), 1)
    group0 = lane < HEAD_DIM
    pad = jnp.zeros((KALL_ROWS - WINDOW - t_new, KV_WIDTH), F32)
    ones_c = jnp.ones((LANES, WINDOW), BF16)
    ones_n = jnp.ones((KALL_ROWS - WINDOW, LANES), BF16)
    klane = lax.broadcasted_iota(jnp.int32, (KV_WIDTH, WINDOW), 1)
    zcols = jnp.zeros((KV_WIDTH, WINDOW - (KALL_ROWS - WINDOW)), F32)
    nt = (((1,), (1,)), ((), ()))

    def shifted(cache_t, new_rows):
        new_t = jnp.concatenate([jnp.concatenate([new_rows, pad], axis=0).T, zcols], axis=1)
        return jnp.where(klane >= WINDOW - t_new, pltpu.roll(new_t, WINDOW - t_new, 1),
                         pltpu.roll(cache_t, WINDOW - t_new, 1))

    def seq_rows(a, b):
        return jnp.concatenate([a[tt * bs + b:tt * bs + b + 1] for tt in range(t_new)], axis=0)

    sinks = []
    for h in range(KV_HEADS):
        sk = jnp.zeros((rows, 1), F32)
        for j in range(nq):
            sk = jnp.where(rj == j, sink_ref[layer, h * nq + j], sk)
        sinks.append(sk)

    st = [dict() for _ in range(bs)]
    for b in range(bs):
        d = st[b]
        ck, cv = ck_ref[b], cv_ref[b]
        kn_b, v_b = seq_rows(kn, b), seq_rows(v, b)
        nk_ref[b] = shifted(ck, kn_b)
        nv_ref[b] = shifted(cv, v_b)
        ckb = ck.astype(BF16)
        knb = jnp.concatenate([kn_b, pad], axis=0).astype(BF16)
        d["vc"] = jnp.concatenate([cv.astype(BF16), ones_c], axis=0)
        d["vn"] = jnp.concatenate([jnp.concatenate([v_b, pad], axis=0).astype(BF16), ones_n], axis=1)
        qb = [seq_rows(qj, b) for qj in qn]
        for h in range(KV_HEADS):
            keep = group0 if h == 0 else jnp.logical_not(group0)
            qs = jnp.concatenate([jnp.where(keep, q, 0.0) for q in qb], axis=0).astype(BF16)
            d["s", h] = jnp.concatenate([jnp.dot(qs, ckb, preferred_element_type=F32),
                                         lax.dot_general(qs, knb, nt, preferred_element_type=F32)], axis=1)
    for b in range(bs):
        d = st[b]
        for h in range(KV_HEADS):
            s = jnp.where(mask, d.pop(("s", h)), NEG_INF)
            m = jnp.maximum(jnp.max(s, axis=-1, keepdims=True), sinks[h])
            p = jnp.exp(s - m).astype(BF16)
            o = (lax.dot_general(p[:, :WINDOW], d["vc"], nt, preferred_element_type=F32)
                 + jnp.dot(p[:, WINDOW:], d["vn"], preferred_element_type=F32))
            d["o", h] = o[:, :LANES] / (o[:, LANES:] + jnp.exp(sinks[h] - m))
    for b in range(bs):
        d = st[b]
        o_b = jnp.concatenate([jnp.where(group0, d["o", 0][j * t_new:(j + 1) * t_new],
                                         d["o", 1][j * t_new:(j + 1) * t_new]) for j in range(nq)], axis=1)
        for tt in range(t_new):
            o_seq[tt * bs + b:tt * bs + b + 1, :] = o_b[tt:tt + 1]
    o_ref[...] = o_seq[...].astype(BF16)


def _attn_sample(proj, cache_k, cache_v, tables, gq, gk, sinks, t_new, layer, prev):
    n = proj.shape[0]
    bsz = n // t_new
    bs = SAMPLE_TILE
    rows = bs * t_new
    row = lambda i: (i, 0)
    cache = pl.BlockSpec((None, bs, KV_WIDTH, WINDOW), lambda i: (layer, i, 0, 0))
    prev, prev_specs, aliases = _alias_inputs(prev, 1)
    return pl.pallas_call(
        _skip_refs(functools.partial(_attn_sample_kernel, bs=bs, t_new=t_new, layer=layer), len(prev)),
        grid=(bsz // bs,),
        in_specs=prev_specs + [
            pl.BlockSpec((rows, ATTN_WIDTH), row),
            pl.BlockSpec((rows, 2 * KV_WIDTH), lambda i: (i, EVEN_KV_BLOCK)),
            cache, cache,
            _const_spec((rows, LANES)), _const_spec((rows, LANES)), _const_spec((rows, LANES)),
            _layer_spec((1, LANES), layer), _layer_spec((1, LANES), layer),
            pl.BlockSpec(memory_space=pltpu.SMEM)],
        out_specs=[pl.BlockSpec((rows, ATTN_WIDTH), row), cache, cache],
        out_shape=[jax.ShapeDtypeStruct((n, ATTN_WIDTH), BF16),
                   jax.ShapeDtypeStruct(cache_k.shape, F32), jax.ShapeDtypeStruct(cache_v.shape, F32)],
        input_output_aliases=aliases,
        scratch_shapes=[pltpu.VMEM((rows, ATTN_WIDTH), F32)],
        compiler_params=_cparams("parallel"),
        name="attn_sample",
    )(*prev, proj, proj, cache_k, cache_v, *tables, gq, gk, sinks)


S5_UCHUNKS = S5_WIDTH // LANES
S5_SUB = S5_FLAT // S5_UCHUNKS
S5_SCHUNKS = S5_FLAT // LANES


def _s5_tail(y, wglu_ref, bglu_ref):
    g = 0.5 * y * (1.0 + lax.erf(y * (2.0 ** -0.5)))
    z = jnp.dot(g.astype(BF16), wglu_ref[...], preferred_element_type=F32) + bglu_ref[...]
    return g * _sigmoid(z)


S5_PARTS = 4


def _s5_prompt_kernel(u_ref, wb_ref, wc_ref, lam_ref, d_ref, wglu_ref, bglu_ref,
                      o_ref, sr_ref, si_ref, xs, hst, *, nbatch, tc):
    rows = nbatch * tc
    prow, ptok = rows // S5_PARTS, tc // S5_PARTS

    @pl.when(pl.program_id(1) == 0)
    def _():
        hst[...] = jnp.zeros_like(hst)

    u = jnp.swapaxes(u_ref[...], 0, 1).reshape(rows, S5_WIDTH)
    ub = u.astype(BF16)

    def in_proj(p, cc):
        rs = slice(p * prow, (p + 1) * prow)
        res = jnp.dot(ub[rs, cc * LANES:(cc + 1) * LANES], wb_ref[cc], preferred_element_type=F32)
        for j in range(S5_SUB // LANES):
            xs[cc * 4 + j, rs, :] = res[:, j * LANES:(j + 1) * LANES]
            xs[S5_SCHUNKS + cc * 4 + j, rs, :] = res[:, S5_SUB + j * LANES:S5_SUB + (j + 1) * LANES]

    ys = {}

    def out_proj(p, cc):
        rs = slice(p * prow, (p + 1) * prow)
        s = jnp.concatenate([xs[cc * 4 + j, rs, :] for j in range(4)]
                            + [xs[S5_SCHUNKS + cc * 4 + j, rs, :] for j in range(4)], axis=1).astype(BF16)
        cols = slice(cc * LANES, (cc + 1) * LANES)
        ys[p, cc] = jnp.dot(s, wc_ref[cc], preferred_element_type=F32) + d_ref[:, cols] * u[rs, cols]

    def tail(p):
        out = _s5_tail(jnp.concatenate([ys.pop((p, cc)) for cc in range(S5_UCHUNKS)], axis=1), wglu_ref, bglu_ref)
        o_ref[:, p * ptok:(p + 1) * ptok, :] = jnp.swapaxes(out.reshape(ptok, nbatch, S5_WIDTH), 0, 1).astype(BF16)

    def scan_step(t, h):
        idx = slice(t * nbatch, (t + 1) * nbatch)
        new = list(h)
        for k in range(S5_SCHUNKS):
            hr, hi = h[k], h[S5_SCHUNKS + k]
            lr, li = lam_ref[k], lam_ref[S5_SCHUNKS + k]
            nr = lr * hr - li * hi + xs[k, idx, :]
            ni = lr * hi + li * hr + xs[S5_SCHUNKS + k, idx, :]
            xs[k, idx, :] = nr
            xs[S5_SCHUNKS + k, idx, :] = ni
            new[k], new[S5_SCHUNKS + k] = nr, ni
        return new

    for cc in range(S5_UCHUNKS):
        in_proj(0, cc)
    h = [hst[k] for k in range(2 * S5_SCHUNKS)]
    for p in range(S5_PARTS):
        work = []
        if p + 1 < S5_PARTS:
            work += [functools.partial(in_proj, p + 1, cc) for cc in range(S5_UCHUNKS)]
        if p >= 1:
            work += [functools.partial(out_proj, p - 1, cc) for cc in range(S5_UCHUNKS)]
            work.append(functools.partial(tail, p - 1))
        every = max(1, ptok // max(1, len(work)))
        for i in range(ptok):
            h = scan_step(p * ptok + i, h)
            if work and (i + 1) % every == 0:
                work.pop(0)()
        for w in work:
            w()
    for cc in range(S5_UCHUNKS):
        out_proj(S5_PARTS - 1, cc)
    tail(S5_PARTS - 1)
    for k in range(2 * S5_SCHUNKS):
        hst[k] = h[k]
    sr_ref[...] = jnp.concatenate(h[:S5_SCHUNKS], axis=1)
    si_ref[...] = jnp.concatenate(h[S5_SCHUNKS:], axis=1)


def _s5_prompt(proj, prm, layer, prev):
    bsz, seq, _ = proj.shape
    nbatch, tc = SUBLANES, S5_CHUNK
    n_layers = prm["wb"].shape[0]
    st = pl.BlockSpec((None, nbatch, S5_FLAT), lambda b, c: (layer, b, 0))
    st_shape = jax.ShapeDtypeStruct((n_layers, bsz, S5_FLAT), F32)
    prev, prev_specs, aliases = _alias_inputs(prev, 1)
    names = ("wb", "wc", "lam8", "d", "wglu", "bglu")
    return pl.pallas_call(
        _skip_refs(functools.partial(_s5_prompt_kernel, nbatch=nbatch, tc=tc), len(prev)),
        grid=(bsz // nbatch, seq // tc),
        in_specs=prev_specs + [pl.BlockSpec((nbatch, tc, S5_WIDTH), lambda b, c: (b, c, EVEN_U_BLOCK))]
        + [_layer_spec(prm[k].shape[1:], layer) for k in names],
        out_specs=[pl.BlockSpec((nbatch, tc, S5_WIDTH), lambda b, c: (b, c, 0)), st, st],
        out_shape=[jax.ShapeDtypeStruct((bsz, seq, S5_WIDTH), BF16), st_shape, st_shape],
        input_output_aliases=aliases,
        scratch_shapes=[pltpu.VMEM((2 * S5_SCHUNKS, nbatch * tc, LANES), F32),
                        pltpu.VMEM((2 * S5_SCHUNKS, nbatch, LANES), F32)],
        compiler_params=_cparams("parallel", "arbitrary"),
        name="s5_prompt",
    )(*prev, proj, *[prm[k] for k in names])


def _s5_sample_kernel(u_ref, wb_ref, wc_ref, lr_ref, li_ref, d_ref, wglu_ref, bglu_ref, h0r_ref, h0i_ref,
                      o_ref, sr_ref, si_ref, xr, xi, *, nseq, t_new):
    nt, st = nseq // SAMPLE_TILE, SAMPLE_TILE
    n = nseq * t_new
    u = u_ref[...]
    ub = u.astype(BF16)
    for cc in range(S5_UCHUNKS):
        res = jnp.dot(ub[:, cc * LANES:(cc + 1) * LANES], wb_ref[cc], preferred_element_type=F32)
        sc = slice(cc * S5_SUB, (cc + 1) * S5_SUB)
        xr[:, :, :, sc] = res[:, :S5_SUB].reshape(nt, t_new, st, S5_SUB)
        xi[:, :, :, sc] = res[:, S5_SUB:].reshape(nt, t_new, st, S5_SUB)
    lr, li = lr_ref[...], li_ref[...]
    hr, hi = h0r_ref[...], h0i_ref[...]
    for t in range(t_new):
        nr = lr * hr - li * hi + xr[:, t].reshape(nseq, S5_FLAT)
        ni = lr * hi + li * hr + xi[:, t].reshape(nseq, S5_FLAT)
        xr[:, t] = nr.reshape(nt, st, S5_FLAT)
        xi[:, t] = ni.reshape(nt, st, S5_FLAT)
        hr, hi = nr, ni
    sr_ref[...] = hr
    si_ref[...] = hi
    ys = []
    for cc in range(S5_UCHUNKS):
        sc = slice(cc * S5_SUB, (cc + 1) * S5_SUB)
        s = jnp.concatenate([xr[:, :, :, sc].reshape(n, S5_SUB), xi[:, :, :, sc].reshape(n, S5_SUB)],
                            axis=1).astype(BF16)
        cols = slice(cc * LANES, (cc + 1) * LANES)
        ys.append(jnp.dot(s, wc_ref[cc], preferred_element_type=F32) + d_ref[:, cols] * u[:, cols])
    o_ref[...] = _s5_tail(jnp.concatenate(ys, axis=1), wglu_ref, bglu_ref).astype(BF16)


def _s5_sample(proj, h0r, h0i, prm, t_new, layer, prev):
    n = proj.shape[0]
    nseq = n // t_new
    names = ("wb", "wc", "lr", "li", "d", "wglu", "bglu")
    st = pl.BlockSpec((None, nseq, S5_FLAT), lambda i: (layer, 0, 0))
    prev, prev_specs, aliases = _alias_inputs(prev, 1)
    scratch = pltpu.VMEM((nseq // SAMPLE_TILE, t_new, SAMPLE_TILE, S5_FLAT), F32)
    return pl.pallas_call(
        _skip_refs(functools.partial(_s5_sample_kernel, nseq=nseq, t_new=t_new), len(prev)),
        grid=(1,),
        in_specs=prev_specs + [pl.BlockSpec((n, S5_WIDTH), lambda i: (0, EVEN_U_BLOCK))]
        + [_layer_spec(prm[k].shape[1:], layer) for k in names]
        + [_layer_spec((nseq, S5_FLAT), layer), _layer_spec((nseq, S5_FLAT), layer)],
        out_specs=[pl.BlockSpec((n, S5_WIDTH), lambda i: (0, 0)), st, st],
        out_shape=[jax.ShapeDtypeStruct((n, S5_WIDTH), BF16),
                   jax.ShapeDtypeStruct(h0r.shape, F32), jax.ShapeDtypeStruct(h0i.shape, F32)],
        input_output_aliases=aliases,
        scratch_shapes=[scratch, scratch],
        compiler_params=_cparams("arbitrary"),
        name="s5_sample",
    )(*prev, proj, *[prm[k] for k in names], h0r, h0i)


def _s5_params(a_re, a_im, log_dt, b_re, b_im, c_re, c_im, d_skip, w_glu, b_glu):
    nl = a_re.shape[0]
    dt = jnp.exp(log_dt)
    mag = jnp.exp(a_re * dt)
    lr, li = mag * jnp.cos(a_im * dt), mag * jnp.sin(a_im * dt)
    den = a_re * a_re + a_im * a_im
    cr = ((lr - 1.0) * a_re + li * a_im) / den
    ci = (li * a_re - (lr - 1.0) * a_im) / den
    bbr = cr[..., None] * b_re - ci[..., None] * b_im
    bbi = cr[..., None] * b_im + ci[..., None] * b_re
    gpc = LANES // S5_GROUP
    eye = jnp.eye(gpc, dtype=F32)

    def in_blocks(bb):
        bb = bb.reshape(nl, S5_UCHUNKS, gpc, S5_STATE, S5_GROUP)
        return jnp.einsum("lcgph,gk->lcghkp", bb, eye).reshape(nl, S5_UCHUNKS, LANES, S5_SUB)

    def out_blocks(cm):
        cm = cm.reshape(nl, S5_UCHUNKS, gpc, S5_GROUP, S5_STATE)
        return jnp.einsum("lcghp,gk->lcgpkh", cm, eye).reshape(nl, S5_UCHUNKS, S5_SUB, LANES)

    wb = jnp.concatenate([in_blocks(bbr), in_blocks(bbi)], axis=3).astype(BF16)
    wc = jnp.concatenate([out_blocks(c_re), -out_blocks(c_im)], axis=2).astype(BF16)
    lr_f, li_f = lr.reshape(nl, 1, S5_FLAT), li.reshape(nl, 1, S5_FLAT)
    lam = jnp.concatenate([lr_f.reshape(nl, S5_SCHUNKS, 1, LANES), li_f.reshape(nl, S5_SCHUNKS, 1, LANES)], axis=1)
    lam8 = jnp.broadcast_to(lam, (nl, 2 * S5_SCHUNKS, SUBLANES, LANES))
    return dict(wb=wb, wc=wc, lam8=lam8, lr=lr_f, li=li_f, d=d_skip.reshape(nl, 1, S5_WIDTH),
                wglu=w_glu.astype(BF16), bglu=b_glu.reshape(nl, 1, S5_WIDTH))


ML_AUG = 2 * ML_DV


EXP_CLAMP = 88.0


def _den_floor(m_row):
    return jnp.exp(jnp.minimum(-m_row, EXP_CLAMP))


def _head_out(h, o, gout):
    hn = h * lax.rsqrt(jnp.mean(h * h, axis=-1, keepdims=True) + EPS) * gout
    return (hn * _sigmoid(o)).astype(BF16)


ODDP_V_BLOCK = 0
ODDP_O_BLOCK = 1
ODDP_Q_BLOCK = (2 * ML_WIDTH) // ML_QK
ODDP_G_BLOCK = (2 * ML_WIDTH + ML_QK) // LANES
ML_SPLIT = 3
ML_PIECE_LANES = 2 * ML_HEADS
ML_SEQS = 8
ML_STAGE_LAG = 2


def _norm_matmul_kt_kernel(x_ref, g_ref, wt_ref, wg_ref, o_ref, kt_ref):
    h = _rms(x_ref[...], g_ref[...]).astype(BF16)
    k0, v0, g0 = ML_QK, 2 * ML_QK, 2 * ML_QK + 2 * ML_WIDTH
    nt = (((1,), (1,)), ((), ()))
    o_ref[:, :g0 - v0] = lax.dot_general(h, wt_ref[v0:g0, :].astype(BF16), nt, preferred_element_type=F32)
    o_ref[:, g0 - v0:g0 - v0 + k0] = lax.dot_general(h, wt_ref[:k0, :].astype(BF16), nt, preferred_element_type=F32)
    o_ref[:, g0 - v0 + k0:] = jnp.dot(h, wg_ref[...], preferred_element_type=F32)
    kt = lax.dot_general(wt_ref[k0:v0, :].astype(BF16), h, nt, preferred_element_type=F32)
    kt_ref[...] = kt * (ML_DK ** -0.5)


def _norm_matmul_kt(xs, g, layer, wt, wg, widx):
    d = wt.shape[2]
    m_out = 2 * ML_WIDTH + ML_QK + wg.shape[2]
    return _row_groups_call(
        _norm_matmul_kt_kernel, [[x] for x in xs], [g, wt, wg],
        [_layer_spec((1, d), layer), _layer_spec(wt.shape[1:], widx), _layer_spec(wg.shape[1:], widx)],
        [(m_out, F32, True), (ML_QK, F32, False)], [], "norm_matmul_kt")


def _cummax_rows(x):
    n = x.shape[0]
    row = lax.broadcasted_iota(jnp.int32, x.shape, 0)
    shift = 1
    while shift < n:
        x = jnp.maximum(x, jnp.where(row >= shift, pltpu.roll(x, shift, 0), NEG_INF))
        shift *= 2
    return x


def _pieces(x):
    lane = lax.broadcasted_iota(jnp.int32, x.shape, 1)
    xx = x + pltpu.roll(x, ML_PIECE_LANES, 1) + pltpu.roll(x, 2 * ML_PIECE_LANES, 1)
    a1, a2, a3 = _split3(xx)
    return jnp.where(lane < ML_PIECE_LANES, a1, jnp.where(lane < 2 * ML_PIECE_LANES, a2, a3))


def _ml_select_constants():
    mask = np.zeros((ML_HEADS, LANES), np.float32)
    sel = np.zeros((ML_HEADS, LANES, 2 * ML_DV), np.float32)
    for h in range(ML_HEADS):
        for k in range(ML_SPLIT):
            lo, hi = k * ML_PIECE_LANES + h, k * ML_PIECE_LANES + ML_HEADS + h
            mask[h, lo] = mask[h, hi] = 1.0
            sel[h, lo, :ML_DV] = 1.0
            sel[h, hi, ML_DV:] = 1.0
    return jnp.asarray(mask), jnp.asarray(sel, dtype=BF16)


def _mlstm_prompt_kernel(*refs, tc, nchunks, nseq):
    v_ref, o_ref, q_ref, g_ref = refs[:4]
    kt_refs = refs[4:4 + nseq]
    bias_ref, gout_ref, mask_ref, sel_ref, h_ref, c_ref, n_ref, m_ref, caug, mst = refs[4 + nseq:]
    ci = pl.program_id(1)

    @pl.when(ci == 0)
    def _():
        caug[...] = jnp.zeros_like(caug)
        mst[...] = jnp.zeros_like(mst)

    nh = ML_HEADS
    lane = lax.broadcasted_iota(jnp.int32, (tc, LANES), 1)
    lo, hi = lane < nh, (lane >= nh) & (lane < 2 * nh)
    rt = lax.broadcasted_iota(jnp.int32, (tc, tc), 0)
    cs = lax.broadcasted_iota(jnp.int32, (tc, tc), 1)
    causal = cs <= rt
    tril = jnp.where(causal, 1.0, 0.0).astype(BF16)
    ones = jnp.ones((tc, ML_DV), F32)

    def gates(sq):
        g = g_ref[sq] + bias_ref[...]
        lf = jnp.where(hi, _log_sigmoid(g), 0.0)
        b = sum(jnp.dot(tril, p, preferred_element_type=F32) for p in _split3(lf))
        c = jnp.where(hi, pltpu.roll(g, nh, 1) - b, 0.0)
        m_prev = mst[sq]
        mx = jnp.maximum(_cummax_rows(c), m_prev)
        m_row = b + mx
        mx_lo = pltpu.roll(mx, LANES - nh, 1)
        w_inter = jnp.exp(pltpu.roll(m_prev, LANES - nh, 1) - mx_lo)
        mst[sq] = m_row[tc - 1:tc, :]
        return dict(xc=_pieces(jnp.where(lo, w_inter, jnp.where(hi, _den_floor(m_row), 0.0))),
                    lc=_pieces(jnp.where(lo, -mx_lo, jnp.where(hi, 1.0, 0.0))),
                    rc=_pieces(jnp.where(lo, 1.0, jnp.where(hi, c, 0.0))))

    gt = [gates(sq) for sq in range(nseq)]
    units = [(sq, hd) for hd in range(nh) for sq in range(nseq)]
    st = [dict() for _ in units]

    def stage1(u):
        sq, hd = units[u]
        d = st[u]
        rh = gt[sq]["rc"] * mask_ref[hd:hd + 1, :].astype(BF16)
        d["dmat"] = lax.dot_general(gt[sq]["lc"], rh, (((1,), (1,)), ((), ())), preferred_element_type=F32)
        d["wb"] = jnp.dot(gt[sq]["xc"], sel_ref[hd], preferred_element_type=F32)
        d["qh"] = q_ref[sq, :, hd * ML_DK:(hd + 1) * ML_DK]
        d["kt"] = kt_refs[sq][hd * ML_DK:(hd + 1) * ML_DK, :]
        d["qk"] = jnp.dot(d["qh"].astype(BF16), d["kt"].astype(BF16), preferred_element_type=F32)

    def stage2(u):
        sq, hd = units[u]
        d = st[u]
        cols = slice(hd * ML_DV, (hd + 1) * ML_DV)
        d["w"] = jnp.exp(jnp.where(causal, d["dmat"], NEG_INF))
        d["vaug"] = jnp.concatenate([v_ref[sq, :, cols], ones], axis=1).astype(BF16)
        d["cm"] = caug[sq, hd]
        lhs = jnp.concatenate([(d["qk"] * d["w"]).astype(BF16), (d["wb"][:, :ML_DK] * d["qh"]).astype(BF16)], axis=1)
        rhs = jnp.concatenate([d["vaug"], d["cm"].astype(BF16)], axis=0)
        d["both"] = jnp.dot(lhs, rhs, preferred_element_type=F32)
        kw = (d["kt"] * d["w"][tc - 1:tc, :]).astype(BF16)
        d["upd"] = jnp.dot(kw, d["vaug"], preferred_element_type=F32)

    def stage3(u):
        sq, hd = units[u]
        d = st[u]
        cols = slice(hd * ML_DV, (hd + 1) * ML_DV)
        both, wb = d["both"], d["wb"]
        h = both[:, :ML_DV] / jnp.maximum(jnp.abs(both[:, ML_DV:]), wb[:, ML_DV:])
        h_ref[sq, :, cols] = _head_out(h, o_ref[sq, :, cols], gout_ref[:, cols])
        decay = wb[tc - 1:tc, :ML_DV]
        caug[sq, hd] = jnp.concatenate([decay, decay], axis=1) * d["cm"] + d["upd"]
        d.clear()

    for step in range(len(units) + 2 * ML_STAGE_LAG):
        if step < len(units):
            stage1(step)
        if 0 <= step - ML_STAGE_LAG < len(units):
            stage2(step - ML_STAGE_LAG)
        if 0 <= step - 2 * ML_STAGE_LAG < len(units):
            stage3(step - 2 * ML_STAGE_LAG)

    @pl.when(ci == nchunks - 1)
    def _():
        c_ref[...] = caug[:, :, :, :ML_DV]
        n_ref[...] = caug[:, :, :, ML_DV:]
        m_ref[...] = mst[...]


def _kt_index(b, c, *, sq, nseq, nchunks):
    return 0, (b * nseq + sq) * nchunks + c


def _mlstm_prompt(proj, kt, bias, gout, consts, layer, prev):
    bsz, seq, _ = proj.shape
    tc, nseq = ML_CHUNK, ML_SEQS
    nchunks = seq // tc
    n_layers = bias.shape[0]
    mask, sel = consts
    blk = lambda w, j: pl.BlockSpec((nseq, tc, w), lambda b, c: (b, c, j))
    st = lambda shape: pl.BlockSpec((None, nseq) + shape, lambda b, c: (layer, b) + (0,) * len(shape))
    st_shape = lambda shape: jax.ShapeDtypeStruct((n_layers, bsz) + shape, F32)
    prev, prev_specs, aliases = _alias_inputs(prev, 1)
    return pl.pallas_call(
        _skip_refs(functools.partial(_mlstm_prompt_kernel, tc=tc, nchunks=nchunks, nseq=nseq), len(prev)),
        grid=(bsz // nseq, nchunks),
        in_specs=prev_specs + [
            blk(ML_WIDTH, ODDP_V_BLOCK), blk(ML_WIDTH, ODDP_O_BLOCK), blk(ML_QK, ODDP_Q_BLOCK),
            blk(LANES, ODDP_G_BLOCK)]
        + [pl.BlockSpec((ML_QK, tc), functools.partial(_kt_index, sq=sq, nseq=nseq, nchunks=nchunks))
           for sq in range(nseq)] + [
            _layer_spec((1, LANES), layer), _layer_spec((1, ML_WIDTH), layer),
            _const_spec(mask.shape), _const_spec(sel.shape)],
        out_specs=[blk(ML_WIDTH, 0), st((ML_HEADS, ML_DK, ML_DV)), st((ML_HEADS, ML_DK, ML_DV)), st((1, LANES))],
        out_shape=[jax.ShapeDtypeStruct((bsz, seq, ML_WIDTH), BF16),
                   st_shape((ML_HEADS, ML_DK, ML_DV)), st_shape((ML_HEADS, ML_DK, ML_DV)), st_shape((1, LANES))],
        input_output_aliases=aliases,
        scratch_shapes=[pltpu.VMEM((nseq, ML_HEADS, ML_DK, ML_AUG), F32), pltpu.VMEM((nseq, 1, LANES), F32)],
        compiler_params=_cparams("parallel", "arbitrary"),
        name="mlstm_prompt",
    )(*prev, proj, proj, proj, proj, *([kt] * nseq), bias, gout, mask, sel)


MLS_SEQS = 32


def _mlstm_sample_kernel(v_ref, o_ref, q_ref, g_ref, kt_ref, bias_ref, gout_ref, mask_ref, sel_ref,
                         c0_ref, n0_ref, m0_ref, h_ref, c_ref, n_ref, m_ref, *, nseq, t_new):
    nh, nt, st = ML_HEADS, nseq // SAMPLE_TILE, SAMPLE_TILE
    rows = nseq * t_new
    lane = lax.broadcasted_iota(jnp.int32, (rows, LANES), 1)
    lo, hi = lane < nh, (lane >= nh) & (lane < 2 * nh)
    tiles = lambda a: a.reshape(nt, t_new, st, a.shape[-1])
    flat = lambda a: a.reshape(rows, a.shape[-1])
    per_seq = lambda a: a.reshape(nseq, a.shape[-1])

    g = g_ref[...] + bias_ref[...]
    lf = tiles(jnp.where(hi, _log_sigmoid(g), 0.0))
    ig = tiles(jnp.where(hi, pltpu.roll(g, nh, 1), 0.0))
    m_prev = m0_ref[...].reshape(nt, st, LANES)
    bs, cs, ms = [], [], []
    b_run, m_run = None, m_prev
    for t in range(t_new):
        b_run = lf[:, t] if b_run is None else b_run + lf[:, t]
        c_t = ig[:, t] - b_run
        m_run = jnp.maximum(m_run, c_t)
        bs.append(b_run)
        cs.append(c_t)
        ms.append(m_run)
    stack = lambda xs: flat(jnp.stack(xs, axis=1))
    b, c, mx = stack(bs), stack(cs), stack(ms)
    m_prev_rows = stack([m_prev] * t_new)
    m_row = b + mx
    m_ref[...] = per_seq(bs[-1] + ms[-1])
    mx_lo = pltpu.roll(mx, LANES - nh, 1)
    w_inter = jnp.exp(pltpu.roll(m_prev_rows, LANES - nh, 1) - mx_lo)
    xc = _pieces(jnp.where(lo, w_inter, jnp.where(hi, _den_floor(m_row), 0.0)))
    lc = _pieces(jnp.where(lo, -mx_lo, jnp.where(hi, 1.0, 0.0)))
    rc = _pieces(jnp.where(lo, 1.0, jnp.where(hi, c, 0.0)))

    def seq_of(idx):
        return (idx // (t_new * st)) * st + idx % st, (idx % (t_new * st)) // st

    rt = lax.broadcasted_iota(jnp.int32, (rows, rows), 0)
    ct = lax.broadcasted_iota(jnp.int32, (rows, rows), 1)
    (rs, rtok), (cseq, ctok) = seq_of(rt), seq_of(ct)
    valid = (rs == cseq) & (ctok <= rtok)
    rq = lax.broadcasted_iota(jnp.int32, (rows, nseq * ML_DK), 0)
    cq = lax.broadcasted_iota(jnp.int32, (rows, nseq * ML_DK), 1)
    own_q = seq_of(rq)[0] == cq // ML_DK
    rk = lax.broadcasted_iota(jnp.int32, (nseq * ML_DK, rows), 0)
    ck = lax.broadcasted_iota(jnp.int32, (nseq * ML_DK, rows), 1)
    own_k = rk // ML_DK == seq_of(ck)[0]
    ones = jnp.ones((rows, ML_DV), F32)
    last = lambda a: per_seq(tiles(a)[:, t_new - 1])

    for hd in range(nh):
        cols = slice(hd * ML_DV, (hd + 1) * ML_DV)
        rh = rc * mask_ref[hd:hd + 1, :].astype(BF16)
        dmat = lax.dot_general(lc, rh, (((1,), (1,)), ((), ())), preferred_element_type=F32)
        wb = jnp.dot(xc, sel_ref[hd], preferred_element_type=F32)
        qh = q_ref[:, hd * ML_DK:(hd + 1) * ML_DK]
        kt = kt_ref[hd * ML_DK:(hd + 1) * ML_DK, :]
        ktb = kt.astype(BF16)
        w = jnp.exp(jnp.where(valid, dmat, NEG_INF))
        qk = jnp.dot(qh.astype(BF16), ktb, preferred_element_type=F32) * w
        vaug = jnp.concatenate([v_ref[:, cols], ones], axis=1).astype(BF16)
        po = jnp.dot(qk.astype(BF16), vaug, preferred_element_type=F32)
        wq = wb[:, :ML_DK] * qh
        wq2 = jnp.concatenate([wq, wq], axis=1)
        wq_bd = jnp.where(own_q, jnp.concatenate([wq2] * (nseq * ML_DK // LANES), axis=1), 0.0).astype(BF16)
        cstack = c0_ref[:, hd].reshape(nseq * ML_DK, ML_DV)
        num = po[:, :ML_DV] + jnp.dot(wq_bd, cstack.astype(BF16), preferred_element_type=F32)
        n0 = n0_ref[hd]
        n_rows = stack([n0.reshape(nt, st, ML_DK)] * t_new)
        den = po[:, ML_DV:] + jnp.sum(wq * n_rows, axis=-1, keepdims=True)
        h = num / jnp.maximum(jnp.abs(den), wb[:, ML_DV:])
        h_ref[:, cols] = _head_out(h, o_ref[:, cols], gout_ref[:, cols])
        w_last = last(w)
        decay = last(wb[:, :ML_DV])
        n_upd = lax.dot_general(w_last.astype(BF16), ktb, (((1,), (1,)), ((), ())), preferred_element_type=F32)
        n_ref[hd] = decay[:, :ML_DK] * n0 + n_upd
        wk = jnp.sum(w_last, axis=0, keepdims=True)
        kw_bd = jnp.where(own_k, jnp.concatenate([kt * wk] * nseq, axis=0), 0.0).astype(BF16)
        upd = jnp.dot(kw_bd, v_ref[:, cols].astype(BF16), preferred_element_type=F32)
        decay_rows = jnp.broadcast_to(decay[:, None, :], (nseq, ML_DK, ML_DV)).reshape(nseq * ML_DK, ML_DV)
        c_ref[:, hd] = (decay_rows * cstack + upd).reshape(nseq, ML_DK, ML_DV)


def _mlstm_sample(proj, kt, bias, gout, consts, c0, n0h, m0, t_new, layer, prev):
    n = proj.shape[0]
    bsz = n // t_new
    nseq = MLS_SEQS
    rows = nseq * t_new
    mask, sel = consts
    blk = lambda w, j: pl.BlockSpec((rows, w), lambda i: (i, j))
    cst = pl.BlockSpec((None, nseq, ML_HEADS, ML_DK, ML_DV), lambda i: (layer, i, 0, 0, 0))
    nst = pl.BlockSpec((None, ML_HEADS, nseq, ML_DK), lambda i: (layer, 0, i, 0))
    mst = pl.BlockSpec((None, nseq, LANES), lambda i: (layer, i, 0))
    prev, prev_specs, aliases = _alias_inputs(prev, 1)
    return pl.pallas_call(
        _skip_refs(functools.partial(_mlstm_sample_kernel, nseq=nseq, t_new=t_new), len(prev)),
        grid=(bsz // nseq,),
        in_specs=prev_specs + [
            blk(ML_WIDTH, ODDP_V_BLOCK), blk(ML_WIDTH, ODDP_O_BLOCK), blk(ML_QK, ODDP_Q_BLOCK),
            blk(LANES, ODDP_G_BLOCK), pl.BlockSpec((ML_QK, rows), lambda i: (0, i)),
            _layer_spec((1, LANES), layer), _layer_spec((1, ML_WIDTH), layer),
            _const_spec(mask.shape), _const_spec(sel.shape), cst, nst, mst],
        out_specs=[blk(ML_WIDTH, 0), cst, nst, mst],
        out_shape=[jax.ShapeDtypeStruct((n, ML_WIDTH), BF16),
                   jax.ShapeDtypeStruct(c0.shape, F32), jax.ShapeDtypeStruct(n0h.shape, F32),
                   jax.ShapeDtypeStruct(m0.shape, F32)],
        input_output_aliases=aliases,
        compiler_params=_cparams("parallel"),
        name="mlstm_sample",
    )(*prev, proj, proj, proj, proj, kt, bias, gout, mask, sel, c0, n0h, m0)


def _pad_lanes(x):
    return jnp.pad(x, [(0, 0)] * (x.ndim - 1) + [(0, LANES - x.shape[-1])])


def kernel(x_prompt, x_sample, cache_k, cache_v, state_ssm_re, state_ssm_im, state_mlstm_c, state_mlstm_n, state_mlstm_m, norm_mix, norm_ffn, w_in_even, q_norm, k_norm, attn_sinks, s5_a_re, s5_a_im, s5_log_dt, s5_b_re, s5_b_im, s5_c_re, s5_c_im, s5_d, s5_w_glu, s5_b_glu, w_out_even, w_in_odd, ml_b_i, ml_b_f, ml_out_norm, w_out_odd, w_gate, w_up, w_down):
    bp, lp, _ = x_prompt.shape
    bsm, ls, _ = x_sample.shape
    yp = x_prompt.reshape(bp * lp, D_MODEL)
    ys = x_sample.reshape(bsm // SAMPLE_TILE, SAMPLE_TILE, ls, D_MODEL).transpose(0, 2, 1, 3).reshape(bsm * ls, D_MODEL)
    tab_p = _rope_tables(jnp.arange(lp))
    tab_s = tuple(jnp.repeat(t, SAMPLE_TILE, axis=0) for t in _rope_tables(PAST_LEN + jnp.arange(ls)))
    n_even, n_odd = w_in_even.shape[0], w_in_odd.shape[0]

    g_mix = norm_mix.reshape(DEPTH, 1, D_MODEL)
    g_ffn = norm_ffn.reshape(DEPTH, 1, D_MODEL)
    wg, wu, wd = w_gate.astype(BF16), w_up.astype(BF16), w_down.astype(BF16)
    kv0, u0 = ATTN_WIDTH, ATTN_WIDTH + 2 * KV_WIDTH
    order = jnp.asarray(ATTN_HEAD_ORDER)
    wq = w_in_even[..., :kv0].reshape(n_even, D_MODEL, ATTN_HEADS, HEAD_DIM)[:, :, order].reshape(n_even, D_MODEL, kv0)
    w_in_e = jnp.concatenate([wq, w_in_even[..., u0:], w_in_even[..., kv0:u0]], axis=-1).astype(BF16)
    wo_attn = w_out_even[:, :kv0].reshape(n_even, ATTN_HEADS, HEAD_DIM, D_MODEL)[:, order].reshape(n_even, kv0, D_MODEL)
    w_out_e = jnp.concatenate([wo_attn, w_out_even[:, kv0:]], axis=1).astype(BF16)
    gq = jnp.tile(q_norm, (1, LANES // HEAD_DIM)).reshape(n_even, 1, LANES)
    gk = jnp.tile(k_norm, (1, LANES // HEAD_DIM)).reshape(n_even, 1, LANES)
    prm = _s5_params(s5_a_re, s5_a_im, s5_log_dt, s5_b_re, s5_b_im, s5_c_re, s5_c_im, s5_d, s5_w_glu, s5_b_glu)
    w_gates_o = _pad_lanes(w_in_odd[..., 2 * ML_QK + 2 * ML_WIDTH:]).astype(BF16)
    w_in_ot = jnp.swapaxes(w_in_odd, 1, 2)
    ml_consts = _ml_select_constants()
    w_out_o = w_out_odd.astype(BF16)
    ml_bias = _pad_lanes(jnp.concatenate([ml_b_i, ml_b_f], axis=-1)).reshape(n_odd, 1, LANES)
    ml_gout = ml_out_norm.reshape(n_odd, 1, ML_WIDTH)
    keys_last = lambda a: a.transpose(0, 1, 3, 4, 2).reshape(n_even, bsm, KV_WIDTH, WINDOW)
    ck, cv = keys_last(cache_k), keys_last(cache_v)
    h0r = state_ssm_re.reshape(n_even, bsm, S5_FLAT)
    h0i = state_ssm_im.reshape(n_even, bsm, S5_FLAT)
    n0h = jnp.swapaxes(state_mlstm_n, 1, 2)
    m0 = jnp.pad(state_mlstm_m, ((0, 0), (0, 0), (ML_HEADS, LANES - 2 * ML_HEADS)))

    p_attn = p_ssm = p_ml = s_attn = s_ssm = s_ml = None
    for layer in range(DEPTH):
        ffn = (layer, g_ffn, wg, wu, wd)
        if layer % 2 == 0:
            e = layer // 2
            proj_p, proj_s = _norm_matmul([yp, ys], g_mix, layer, w_in_e, e)
            proj3 = proj_p.reshape(bp, lp, -1)
            attn_p, *p_attn = _attn_prompt(proj3, tab_p, gq, gk, attn_sinks, e, p_attn)
            ssm_p, *p_ssm = _s5_prompt(proj3, prm, e, p_ssm)
            attn_s, *s_attn = _attn_sample(proj_s, ck, cv, tab_s, gq, gk, attn_sinks, ls, e, s_attn)
            ssm_s, *s_ssm = _s5_sample(proj_s, h0r, h0i, prm, ls, e, s_ssm)
            yp, ys = _mix_ffn([[yp, attn_p.reshape(bp * lp, -1), ssm_p.reshape(bp * lp, -1)], [ys, attn_s, ssm_s]],
                              w_out_e, e, *ffn)
        else:
            o = layer // 2
            (proj_p, kt_p), (proj_s, kt_s) = _norm_matmul_kt([yp, ys], g_mix, layer, w_in_ot, w_gates_o, o)
            hh_p, *p_ml = _mlstm_prompt(proj_p.reshape(bp, lp, -1), kt_p, ml_bias, ml_gout, ml_consts, o, p_ml)
            hh_s, *s_ml = _mlstm_sample(proj_s, kt_s, ml_bias, ml_gout, ml_consts, state_mlstm_c, n0h, m0, ls, o,
                                        s_ml)
            yp, ys = _mix_ffn([[yp, hh_p.reshape(bp * lp, -1)], [ys, hh_s]], w_out_o, o, *ffn)
    heads = lambda a: a.reshape(a.shape[:3] + (KV_HEADS, HEAD_DIM))
    groups = lambda a: a.reshape(a.shape[:2] + (S5_GROUPS, S5_STATE))
    keys_first = lambda a: a.reshape(a.shape[:2] + (KV_HEADS, HEAD_DIM, WINDOW)).transpose(0, 1, 4, 2, 3)
    ys = ys.reshape(bsm // SAMPLE_TILE, ls, SAMPLE_TILE, D_MODEL).transpose(0, 2, 1, 3).reshape(bsm, ls, D_MODEL)
    return (yp.reshape(bp, lp, D_MODEL), ys,
            heads(p_attn[0]), heads(p_attn[1]), groups(p_ssm[0]), groups(p_ssm[1]),
            p_ml[0], p_ml[1][..., 0], p_ml[2][:, :, 0, ML_HEADS:2 * ML_HEADS],
            keys_first(s_attn[0]), keys_first(s_attn[1]), groups(s_ssm[0]), groups(s_ssm[1]),
            s_ml[0], jnp.swapaxes(s_ml[1], 1, 2), s_ml[2][..., ML_HEADS:2 * ML_HEADS])
```

```python
import functools

import numpy as np

import jax
import jax.numpy as jnp
from jax import lax
from jax.experimental import pallas as pl
from jax.experimental.pallas import tpu as pltpu

F32 = jnp.float32
BF16 = jnp.bfloat16

D_MODEL = 1024
DEPTH = 4
PAST_LEN = 8192
WINDOW = 128
ATTN_HEADS = 8
KV_HEADS = 2
HEAD_DIM = 64
ATTN_WIDTH = ATTN_HEADS * HEAD_DIM
KV_WIDTH = KV_HEADS * HEAD_DIM
ROT_DIM = HEAD_DIM // 4
ROPE_THETA = 500000.0
S5_GROUP = 16
S5_WIDTH = D_MODEL // 2
S5_GROUPS = S5_WIDTH // S5_GROUP
S5_STATE = 64
S5_FLAT = S5_GROUPS * S5_STATE
ML_HEADS = 8
ML_DV = D_MODEL // ML_HEADS
ML_DK = ML_DV // 2
ML_QK = ML_HEADS * ML_DK
ML_WIDTH = ML_HEADS * ML_DV
D_FF = 2816
EPS = 1e-6

LANES = 128
SUBLANES = 8
ROW_TILE = 512
FF_TILE = 256
S5_CHUNK = 128
ML_CHUNK = 128
SAMPLE_TILE = SUBLANES
VMEM_LIMIT = 56 * 1024 * 1024

NEG_INF = float("-inf")


def _cparams(*sem):
    return pltpu.CompilerParams(dimension_semantics=sem, vmem_limit_bytes=VMEM_LIMIT)


def _const_spec(shape):
    zeros = (0,) * len(shape)
    return pl.BlockSpec(shape, lambda *_: zeros, pipeline_mode=pl.Buffered(1))


def _layer_spec(shape, layer):
    zeros = (0,) * len(shape)
    return pl.BlockSpec((None,) + tuple(shape), lambda *_: (layer,) + zeros, pipeline_mode=pl.Buffered(1))


def _skip_refs(body, n_skip):
    if n_skip == 0:
        return body

    def wrapped(*refs):
        return body(*refs[n_skip:])

    return wrapped


def _alias_inputs(prev, first_state_out):
    prev = () if prev is None else tuple(prev)
    specs = [pl.BlockSpec(memory_space=pl.ANY) for _ in prev]
    aliases = {i: first_state_out + i for i in range(len(prev))}
    return prev, specs, aliases


def _rms(x, g):
    ms = jnp.mean(x * x, axis=-1, keepdims=True)
    return x * lax.rsqrt(ms + EPS) * g


def _split3(a):
    a1 = a.astype(BF16)
    r1 = a - a1.astype(F32)
    a2 = r1.astype(BF16)
    a3 = (r1 - a2.astype(F32)).astype(BF16)
    return a1, a2, a3


def _log_sigmoid(x):
    return jnp.minimum(x, 0.0) - jnp.log(1.0 + jnp.exp(-jnp.abs(x)))


def _sigmoid(x):
    return 1.0 / (1.0 + jnp.exp(-x))


def _norm_matmul_kernel(x_ref, g_ref, w_ref, o_ref):
    h = _rms(x_ref[...], g_ref[...]).astype(BF16)
    o_ref[...] = jnp.dot(h, w_ref[...], preferred_element_type=F32)


def _row_groups_call(body, groups, consts, const_specs, out_defs, scratch_shapes, name):
    steps, tiles = [], []
    for arrays in groups:
        n = arrays[0].shape[0]
        tm = min(ROW_TILE, n)
        tiles.append(tm)
        steps.append(n // tm)
    offs = [sum(steps[:k]) for k in range(len(groups))]

    def local(k):
        return lambda i: jnp.clip(i - offs[k], 0, steps[k] - 1)

    in_specs, out_specs, out_shape, args = [], [], [], []
    for k, arrays in enumerate(groups):
        for a in arrays:
            in_specs.append(pl.BlockSpec((tiles[k], a.shape[1]), lambda i, f=local(k): (f(i), 0)))
            args.append(a)
    for k, arrays in enumerate(groups):
        n = arrays[0].shape[0]
        for width, dtype, by_rows in out_defs:
            if by_rows:
                out_specs.append(pl.BlockSpec((tiles[k], width), lambda i, f=local(k): (f(i), 0)))
                out_shape.append(jax.ShapeDtypeStruct((n, width), dtype))
            else:
                out_specs.append(pl.BlockSpec((width, tiles[k]), lambda i, f=local(k): (0, f(i))))
                out_shape.append(jax.ShapeDtypeStruct((width, n), dtype))
    n_in = [len(arrays) for arrays in groups]
    n_out = len(out_defs)

    def kern(*refs):
        i = pl.program_id(0)
        pos = 0
        ins = []
        for cnt in n_in:
            ins.append(refs[pos:pos + cnt])
            pos += cnt
        crefs = refs[pos:pos + len(consts)]
        pos += len(consts)
        outs = [refs[pos + k * n_out:pos + (k + 1) * n_out] for k in range(len(groups))]
        scratch = refs[pos + len(groups) * n_out:]
        for k in range(len(groups)):
            @pl.when((i >= offs[k]) & (i < offs[k] + steps[k]))
            def _(k=k):
                body(*ins[k], *crefs, *outs[k], *scratch)

    res = pl.pallas_call(
        kern,
        grid=(sum(steps),),
        in_specs=in_specs + list(const_specs),
        out_specs=out_specs,
        out_shape=out_shape,
        scratch_shapes=scratch_shapes,
        compiler_params=_cparams("arbitrary"),
        name=name,
    )(*args, *consts)
    return [res[k * n_out:(k + 1) * n_out] for k in range(len(groups))]


RING_SLOTS = 3


def _ring_proj_call(body, x_main, x_tail, consts, const_specs, out_defs, name):
    n, d = x_main.shape
    tm = ROW_TILE
    steps = n // tm
    n_tail = x_tail.shape[0]
    assert n % tm == 0 and steps >= RING_SLOTS - 1 and n_tail <= tm

    def main_idx(i):
        return jnp.minimum(i, steps - 1)

    out_specs, out_shape = [], []
    for rows, idx in ((n, main_idx), (n_tail, lambda i: 0)):
        tile = min(tm, rows)
        for width, dtype, by_rows in out_defs:
            if by_rows:
                out_specs.append(pl.BlockSpec((tile, width), lambda i, f=idx: (f(i), 0)))
                out_shape.append(jax.ShapeDtypeStruct((rows, width), dtype))
            else:
                out_specs.append(pl.BlockSpec((width, tile), lambda i, f=idx: (0, f(i))))
                out_shape.append(jax.ShapeDtypeStruct((width, rows), dtype))
    n_out = len(out_defs)

    def kern(x_hbm, xt_ref, *refs):
        crefs = refs[:len(consts)]
        outs_main = refs[len(consts):len(consts) + n_out]
        outs_tail = refs[len(consts) + n_out:len(consts) + 2 * n_out]
        xbuf, sem = refs[len(consts) + 2 * n_out:]
        i = pl.program_id(0)

        def tile_copy(j, slot):
            rows = pl.ds(pl.multiple_of(j * tm, tm), tm)
            return pltpu.make_async_copy(x_hbm.at[rows, :], xbuf.at[slot], sem.at[slot])

        @pl.when(i == 0)
        def _():
            for j in range(RING_SLOTS - 1):
                tile_copy(j, j).start()

        @pl.when(i + RING_SLOTS - 1 < steps)
        def _():
            nxt = i + RING_SLOTS - 1
            tile_copy(nxt, nxt % RING_SLOTS).start()

        @pl.when(i < steps)
        def _():
            slot = i % RING_SLOTS
            tile_copy(i, slot).wait()
            body(xbuf.at[slot], *crefs, *outs_main)

        @pl.when(i >= steps)
        def _():
            body(xt_ref, *crefs, *outs_tail)

    res = pl.pallas_call(
        kern,
        grid=(steps + 1,),
        in_specs=[pl.BlockSpec(memory_space=pl.ANY), _const_spec((n_tail, d))] + list(const_specs),
        out_specs=out_specs,
        out_shape=out_shape,
        scratch_shapes=[pltpu.VMEM((RING_SLOTS, tm, d), F32), pltpu.SemaphoreType.DMA((RING_SLOTS,))],
        compiler_params=_cparams("arbitrary"),
        name=name,
    )(x_main, x_tail, *consts)
    return [res[:n_out], res[n_out:]]


def _norm_matmul(xs, g, layer, w, widx):
    d, m = w.shape[1], w.shape[2]
    res = _ring_proj_call(_norm_matmul_kernel, xs[0], xs[1], [g, w],
                          [_layer_spec((1, d), layer), _layer_spec((d, m), widx)],
                          [(m, F32, True)], "norm_matmul")
    return [r[0] for r in res]


def _mix_ffn_kernel(*refs, n_mix):
    x_ref = refs[0]
    a_refs = refs[1:1 + n_mix]
    wo_ref, g_ref, wg_ref, wu_ref, wd_ref, o_ref, act_ref = refs[1 + n_mix:]
    y = x_ref[...]
    off = 0
    for a_ref in a_refs:
        ka = a_ref.shape[1]
        y = y + jnp.dot(a_ref[...], wo_ref[off:off + ka, :], preferred_element_type=F32)
        off += ka
    h = _rms(y, g_ref[...]).astype(BF16)
    for f in range(D_FF // FF_TILE):
        cols = slice(f * FF_TILE, (f + 1) * FF_TILE)
        gate = jnp.dot(h, wg_ref[:, cols], preferred_element_type=F32)
        up = jnp.dot(h, wu_ref[:, cols], preferred_element_type=F32)
        act_ref[:, cols] = (gate * _sigmoid(gate) * up).astype(BF16)
    o_ref[...] = y + jnp.dot(act_ref[...], wd_ref[...], preferred_element_type=F32)


def _mix_ffn(groups, w_out, oidx, layer, g_ffn, wg, wu, wd):
    d = w_out.shape[2]
    n_mix = len(groups[0]) - 1
    tm = min(ROW_TILE, max(g[0].shape[0] for g in groups))
    res = _row_groups_call(
        functools.partial(_mix_ffn_kernel, n_mix=n_mix), groups, [w_out, g_ffn, wg, wu, wd],
        [_layer_spec(w_out.shape[1:], oidx), _layer_spec((1, d), layer), _layer_spec(wg.shape[1:], layer),
         _layer_spec(wu.shape[1:], layer), _layer_spec(wd.shape[1:], layer)],
        [(d, F32, True)], [pltpu.VMEM((tm, D_FF), BF16)], "mix_ffn")
    return [r[0] for r in res]


def _head_ones():
    r = lax.broadcasted_iota(jnp.int32, (LANES, LANES), 0) // HEAD_DIM
    c = lax.broadcasted_iota(jnp.int32, (LANES, LANES), 1) // HEAD_DIM
    return jnp.where(r == c, 1.0, 0.0).astype(BF16)


def _qk_prep(x, g, ones, ct, sa, sb):
    x2 = x * x
    hi = x2.astype(BF16)
    lo = (x2 - hi.astype(F32)).astype(BF16)
    ss = jnp.dot(hi, ones, preferred_element_type=F32) + jnp.dot(lo, ones, preferred_element_type=F32)
    xn = x * lax.rsqrt(ss * (1.0 / HEAD_DIM) + EPS) * g
    return xn * ct + pltpu.roll(xn, LANES - ROT_DIM // 2, 1) * sa + pltpu.roll(xn, ROT_DIM // 2, 1) * sb


def _rope_tables(pos):
    half = ROT_DIM // 2
    inv = jnp.power(jnp.float32(ROPE_THETA), -jnp.arange(half, dtype=F32) / half)
    ang = pos.astype(F32)[:, None] * inv[None, :]
    cos, sin = jnp.cos(ang), jnp.sin(ang)
    n = pos.shape[0]
    one = jnp.ones((n, HEAD_DIM - ROT_DIM), F32)
    zero = jnp.zeros((n, HEAD_DIM - ROT_DIM), F32)
    z8 = jnp.zeros((n, half), F32)
    ct = jnp.concatenate([cos, cos, one], axis=1)
    sa = jnp.concatenate([-sin, z8, zero], axis=1)
    sb = jnp.concatenate([z8, sin, zero], axis=1)
    tile = lambda t: jnp.concatenate([t, t], axis=1)
    return tile(ct), tile(sa), tile(sb)


ATTN_SEQS = 4
ATTN_QCHUNKS = ATTN_WIDTH // LANES
ATTN_HEAD_ORDER = tuple(h * ATTN_QCHUNKS + j for j in range(ATTN_QCHUNKS) for h in range(KV_HEADS))


def _attn_prompt_kernel(q_ref, kv_ref, ct_ref, sa_ref, sb_ref, gq_ref, gk_ref, sink_ref,
                        o_ref, pk_ref, pv_ref, kprev, vprev, *, nb, layer, nseq):
    i = pl.program_id(1)

    @pl.when(i == 0)
    def _():
        kprev[...] = jnp.zeros_like(kprev)
        vprev[...] = jnp.zeros_like(vprev)

    ones = _head_ones()
    ct, sa, sb = ct_ref[...], sa_ref[...], sb_ref[...]
    r = lax.broadcasted_iota(jnp.int32, (WINDOW, 2 * WINDOW), 0)
    c = lax.broadcasted_iota(jnp.int32, (WINDOW, 2 * WINDOW), 1)
    rel = r + WINDOW - c
    mask = (rel >= 0) & (rel <= WINDOW) & ((c >= WINDOW) | (i > 0))
    lane = lax.broadcasted_iota(jnp.int32, (WINDOW, LANES), 1)
    group0 = lane < HEAD_DIM
    v_ones = jnp.ones((2 * WINDOW, LANES), BF16)
    nq = ATTN_QCHUNKS
    st = [dict() for _ in range(nseq)]

    def prep(sq):
        d = st[sq]
        kv = kv_ref[sq]
        d["kn"] = _qk_prep(kv[:, :KV_WIDTH], gk_ref[...], ones, ct, sa, sb)
        d["v"] = kv[:, KV_WIDTH:]
        d["qn"] = [_qk_prep(q_ref[sq, :, j * LANES:(j + 1) * LANES], gq_ref[...], ones, ct, sa, sb)
                   * (HEAD_DIM ** -0.5) for j in range(nq)]
        d["kcat"] = jnp.concatenate([kprev[sq], d["kn"]], axis=0).astype(BF16)
        d["vaug"] = jnp.concatenate([jnp.concatenate([vprev[sq], d["v"]], axis=0).astype(BF16), v_ones], axis=1)
        kprev[sq] = d["kn"]
        vprev[sq] = d["v"]

    def scores(sq, h):
        d = st[sq]
        keep = group0 if h == 0 else jnp.logical_not(group0)
        qs = jnp.concatenate([jnp.where(keep, qj, 0.0) for qj in d["qn"]], axis=0).astype(BF16)
        d["s", h] = lax.dot_general(qs, d["kcat"], (((1,), (1,)), ((), ())), preferred_element_type=F32)

    def softmax_pv(sq, h):
        d = st[sq]
        s = d.pop(("s", h))
        ps, corr = [], []
        for j in range(nq):
            sg = jnp.where(mask, s[j * WINDOW:(j + 1) * WINDOW], NEG_INF)
            sink = sink_ref[layer, h * nq + j]
            m = jnp.maximum(jnp.max(sg, axis=-1, keepdims=True), sink)
            ps.append(jnp.exp(sg - m).astype(BF16))
            corr.append(jnp.exp(sink - m))
        o = jnp.dot(jnp.concatenate(ps, axis=0), d["vaug"], preferred_element_type=F32)
        d["o", h] = [o[j * WINDOW:(j + 1) * WINDOW, :LANES] / (o[j * WINDOW:(j + 1) * WINDOW, LANES:] + corr[j])
                     for j in range(nq)]

    def finish(sq):
        d = st[sq]
        o_ref[sq] = jnp.concatenate([jnp.where(group0, d["o", 0][j], d["o", 1][j]) for j in range(nq)],
                                    axis=1).astype(BF16)

    for sq in range(nseq):
        prep(sq)
    for sq in range(nseq):
        scores(sq, 0)
        scores(sq, 1)
    for sq in range(nseq):
        softmax_pv(sq, 0)
        softmax_pv(sq, 1)
        finish(sq)

    @pl.when(i == nb - 1)
    def _():
        for sq in range(nseq):
            pk_ref[sq] = st[sq]["kn"]
            pv_ref[sq] = st[sq]["v"]


def _attn_prompt(proj, tables, gq, gk, sinks, layer, prev):
    bsz, seq, _ = proj.shape
    nb = seq // WINDOW
    nseq = ATTN_SEQS
    n_layers = gq.shape[0]
    tab = pl.BlockSpec((WINDOW, LANES), lambda b, i: (i, 0))
    prev, prev_specs, aliases = _alias_inputs(prev, 1)
    win = pl.BlockSpec((None, nseq, WINDOW, KV_WIDTH), lambda b, i: (layer, b, 0, 0))
    win_shape = jax.ShapeDtypeStruct((n_layers, bsz, WINDOW, KV_WIDTH), F32)
    return pl.pallas_call(
        _skip_refs(functools.partial(_attn_prompt_kernel, nb=nb, layer=layer, nseq=nseq), len(prev)),
        grid=(bsz // nseq, nb),
        in_specs=prev_specs + [
            pl.BlockSpec((nseq, WINDOW, ATTN_WIDTH), lambda b, i: (b, i, 0)),
            pl.BlockSpec((nseq, WINDOW, 2 * KV_WIDTH), lambda b, i: (b, i, EVEN_KV_BLOCK)),
            tab, tab, tab, _layer_spec((1, LANES), layer), _layer_spec((1, LANES), layer),
            pl.BlockSpec(memory_space=pltpu.SMEM)],
        out_specs=[pl.BlockSpec((nseq, WINDOW, ATTN_WIDTH), lambda b, i: (b, i, 0)), win, win],
        out_shape=[jax.ShapeDtypeStruct((bsz, seq, ATTN_WIDTH), BF16), win_shape, win_shape],
        input_output_aliases=aliases,
        scratch_shapes=[pltpu.VMEM((nseq, WINDOW, KV_WIDTH), F32), pltpu.VMEM((nseq, WINDOW, KV_WIDTH), F32)],
        compiler_params=_cparams("parallel", "arbitrary"),
        name="attn_prompt",
    )(*prev, proj, proj, *tables, gq, gk, sinks)


EVEN_U_BLOCK = ATTN_WIDTH // S5_WIDTH
EVEN_KV_BLOCK = (ATTN_WIDTH + S5_WIDTH) // (2 * KV_WIDTH)
KALL_ROWS = WINDOW + SUBLANES


def _attn_sample_kernel(q_ref, kv_ref, ck_ref, cv_ref, ct_ref, sa_ref, sb_ref, gq_ref, gk_ref, sink_ref,
                        o_ref, nk_ref, nv_ref, o_seq, *, bs, t_new, layer):
    ones = _head_ones()
    ct, sa, sb = ct_ref[...], sa_ref[...], sb_ref[...]
    kv = kv_ref[...]
    kn = _qk_prep(kv[:, :KV_WIDTH], gk_ref[...], ones, ct, sa, sb)
    v = kv[:, KV_WIDTH:]
    nq = ATTN_QCHUNKS
    qn = [_qk_prep(q_ref[:, j * LANES:(j + 1) * LANES], gq_ref[...], ones, ct, sa, sb) * (HEAD_DIM ** -0.5)
          for j in range(nq)]
    rows = nq * t_new
    r = lax.broadcasted_iota(jnp.int32, (rows, KALL_ROWS), 0)
    c = lax.broadcasted_iota(jnp.int32, (rows, KALL_ROWS), 1)
    t = r % t_new
    mask = (c >= t) & (c <= t + WINDOW)
    rj = lax.broadcasted_iota(jnp.int32, (rows, 1), 0) // t_new
    lane = lax.broadcasted_iota(jnp.int32, (t_new, LANES), 1)
    group0 = lane < HEAD_DIM
    pad = jnp.zeros((KALL_ROWS - WINDOW - t_new, KV_WIDTH), F32)
    ones_c = jnp.ones((LANES, WINDOW), BF16)
    ones_n = jnp.ones((KALL_ROWS - WINDOW, LANES), BF16)
    klane = lax.broadcasted_iota(jnp.int32, (KV_WIDTH, WINDOW), 1)
    zcols = jnp.zeros((KV_WIDTH, WINDOW - (KALL_ROWS - WINDOW)), F32)
    nt = (((1,), (1,)), ((), ()))

    def shifted(cache_t, new_rows):
        new_t = jnp.concatenate([jnp.concatenate([new_rows, pad], axis=0).T, zcols], axis=1)
        return jnp.where(klane >= WINDOW - t_new, pltpu.roll(new_t, WINDOW - t_new, 1),
                         pltpu.roll(cache_t, WINDOW - t_new, 1))

    def seq_rows(a, b):
        return jnp.concatenate([a[tt * bs + b:tt * bs + b + 1] for tt in range(t_new)], axis=0)

    sinks = []
    for h in range(KV_HEADS):
        sk = jnp.zeros((rows, 1), F32)
        for j in range(nq):
            sk = jnp.where(rj == j, sink_ref[layer, h * nq + j], sk)
        sinks.append(sk)

    st = [dict() for _ in range(bs)]
    for b in range(bs):
        d = st[b]
        ck, cv = ck_ref[b], cv_ref[b]
        kn_b, v_b = seq_rows(kn, b), seq_rows(v, b)
        nk_ref[b] = shifted(ck, kn_b)
        nv_ref[b] = shifted(cv, v_b)
        ckb = ck.astype(BF16)
        knb = jnp.concatenate([kn_b, pad], axis=0).astype(BF16)
        d["vc"] = jnp.concatenate([cv.astype(BF16), ones_c], axis=0)
        d["vn"] = jnp.concatenate([jnp.concatenate([v_b, pad], axis=0).astype(BF16), ones_n], axis=1)
        qb = [seq_rows(qj, b) for qj in qn]
        for h in range(KV_HEADS):
            keep = group0 if h == 0 else jnp.logical_not(group0)
            qs = jnp.concatenate([jnp.where(keep, q, 0.0) for q in qb], axis=0).astype(BF16)
            d["s", h] = jnp.concatenate([jnp.dot(qs, ckb, preferred_element_type=F32),
                                         lax.dot_general(qs, knb, nt, preferred_element_type=F32)], axis=1)
    for b in range(bs):
        d = st[b]
        for h in range(KV_HEADS):
            s = jnp.where(mask, d.pop(("s", h)), NEG_INF)
            m = jnp.maximum(jnp.max(s, axis=-1, keepdims=True), sinks[h])
            p = jnp.exp(s - m).astype(BF16)
            o = (lax.dot_general(p[:, :WINDOW], d["vc"], nt, preferred_element_type=F32)
                 + jnp.dot(p[:, WINDOW:], d["vn"], preferred_element_type=F32))
            d["o", h] = o[:, :LANES] / (o[:, LANES:] + jnp.exp(sinks[h] - m))
    for b in range(bs):
        d = st[b]
        o_b = jnp.concatenate([jnp.where(group0, d["o", 0][j * t_new:(j + 1) * t_new],
                                         d["o", 1][j * t_new:(j + 1) * t_new]) for j in range(nq)], axis=1)
        for tt in range(t_new):
            o_seq[tt * bs + b:tt * bs + b + 1, :] = o_b[tt:tt + 1]
    o_ref[...] = o_seq[...].astype(BF16)


def _attn_sample(proj, cache_k, cache_v, tables, gq, gk, sinks, t_new, layer, prev):
    n = proj.shape[0]
    bsz = n // t_new
    bs = SAMPLE_TILE
    rows = bs * t_new
    row = lambda i: (i, 0)
    cache = pl.BlockSpec((None, bs, KV_WIDTH, WINDOW), lambda i: (layer, i, 0, 0))
    prev, prev_specs, aliases = _alias_inputs(prev, 1)
    return pl.pallas_call(
        _skip_refs(functools.partial(_attn_sample_kernel, bs=bs, t_new=t_new, layer=layer), len(prev)),
        grid=(bsz // bs,),
        in_specs=prev_specs + [
            pl.BlockSpec((rows, ATTN_WIDTH), row),
            pl.BlockSpec((rows, 2 * KV_WIDTH), lambda i: (i, EVEN_KV_BLOCK)),
            cache, cache,
            _const_spec((rows, LANES)), _const_spec((rows, LANES)), _const_spec((rows, LANES)),
            _layer_spec((1, LANES), layer), _layer_spec((1, LANES), layer),
            pl.BlockSpec(memory_space=pltpu.SMEM)],
        out_specs=[pl.BlockSpec((rows, ATTN_WIDTH), row), cache, cache],
        out_shape=[jax.ShapeDtypeStruct((n, ATTN_WIDTH), BF16),
                   jax.ShapeDtypeStruct(cache_k.shape, F32), jax.ShapeDtypeStruct(cache_v.shape, F32)],
        input_output_aliases=aliases,
        scratch_shapes=[pltpu.VMEM((rows, ATTN_WIDTH), F32)],
        compiler_params=_cparams("parallel"),
        name="attn_sample",
    )(*prev, proj, proj, cache_k, cache_v, *tables, gq, gk, sinks)


S5_UCHUNKS = S5_WIDTH // LANES
S5_SUB = S5_FLAT // S5_UCHUNKS
S5_SCHUNKS = S5_FLAT // LANES


def _s5_tail(y, wglu_ref, bglu_ref):
    g = 0.5 * y * (1.0 + lax.erf(y * (2.0 ** -0.5)))
    z = jnp.dot(g.astype(BF16), wglu_ref[...], preferred_element_type=F32) + bglu_ref[...]
    return g * _sigmoid(z)


S5_PARTS = 4


def _s5_prompt_kernel(u_ref, wb_ref, wc_ref, lam_ref, d_ref, wglu_ref, bglu_ref,
                      o_ref, sr_ref, si_ref, xs, hst, *, nbatch, tc):
    rows = nbatch * tc
    prow, ptok = rows // S5_PARTS, tc // S5_PARTS

    @pl.when(pl.program_id(1) == 0)
    def _():
        hst[...] = jnp.zeros_like(hst)

    u = jnp.swapaxes(u_ref[...], 0, 1).reshape(rows, S5_WIDTH)
    ub = u.astype(BF16)

    def in_proj(p, cc):
        rs = slice(p * prow, (p + 1) * prow)
        res = jnp.dot(ub[rs, cc * LANES:(cc + 1) * LANES], wb_ref[cc], preferred_element_type=F32)
        for j in range(S5_SUB // LANES):
            xs[cc * 4 + j, rs, :] = res[:, j * LANES:(j + 1) * LANES]
            xs[S5_SCHUNKS + cc * 4 + j, rs, :] = res[:, S5_SUB + j * LANES:S5_SUB + (j + 1) * LANES]

    ys = {}

    def out_proj(p, cc):
        rs = slice(p * prow, (p + 1) * prow)
        s = jnp.concatenate([xs[cc * 4 + j, rs, :] for j in range(4)]
                            + [xs[S5_SCHUNKS + cc * 4 + j, rs, :] for j in range(4)], axis=1).astype(BF16)
        cols = slice(cc * LANES, (cc + 1) * LANES)
        ys[p, cc] = jnp.dot(s, wc_ref[cc], preferred_element_type=F32) + d_ref[:, cols] * u[rs, cols]

    def tail(p):
        out = _s5_tail(jnp.concatenate([ys.pop((p, cc)) for cc in range(S5_UCHUNKS)], axis=1), wglu_ref, bglu_ref)
        o_ref[:, p * ptok:(p + 1) * ptok, :] = jnp.swapaxes(out.reshape(ptok, nbatch, S5_WIDTH), 0, 1).astype(BF16)

    def scan_step(t, h):
        idx = slice(t * nbatch, (t + 1) * nbatch)
        new = list(h)
        for k in range(S5_SCHUNKS):
            hr, hi = h[k], h[S5_SCHUNKS + k]
            lr, li = lam_ref[k], lam_ref[S5_SCHUNKS + k]
            nr = lr * hr - li * hi + xs[k, idx, :]
            ni = lr * hi + li * hr + xs[S5_SCHUNKS + k, idx, :]
            xs[k, idx, :] = nr
            xs[S5_SCHUNKS + k, idx, :] = ni
            new[k], new[S5_SCHUNKS + k] = nr, ni
        return new

    for cc in range(S5_UCHUNKS):
        in_proj(0, cc)
    h = [hst[k] for k in range(2 * S5_SCHUNKS)]
    for p in range(S5_PARTS):
        work = []
        if p + 1 < S5_PARTS:
            work += [functools.partial(in_proj, p + 1, cc) for cc in range(S5_UCHUNKS)]
        if p >= 1:
            work += [functools.partial(out_proj, p - 1, cc) for cc in range(S5_UCHUNKS)]
            work.append(functools.partial(tail, p - 1))
        every = max(1, ptok // max(1, len(work)))
        for i in range(ptok):
            h = scan_step(p * ptok + i, h)
            if work and (i + 1) % every == 0:
                work.pop(0)()
        for w in work:
            w()
    for cc in range(S5_UCHUNKS):
        out_proj(S5_PARTS - 1, cc)
    tail(S5_PARTS - 1)
    for k in range(2 * S5_SCHUNKS):
        hst[k] = h[k]
    sr_ref[...] = jnp.concatenate(h[:S5_SCHUNKS], axis=1)
    si_ref[...] = jnp.concatenate(h[S5_SCHUNKS:], axis=1)


def _s5_prompt(proj, prm, layer, prev):
    bsz, seq, _ = proj.shape
    nbatch, tc = SUBLANES, S5_CHUNK
    n_layers = prm["wb"].shape[0]
    st = pl.BlockSpec((None, nbatch, S5_FLAT), lambda b, c: (layer, b, 0))
    st_shape = jax.ShapeDtypeStruct((n_layers, bsz, S5_FLAT), F32)
    prev, prev_specs, aliases = _alias_inputs(prev, 1)
    names = ("wb", "wc", "lam8", "d", "wglu", "bglu")
    return pl.pallas_call(
        _skip_refs(functools.partial(_s5_prompt_kernel, nbatch=nbatch, tc=tc), len(prev)),
        grid=(bsz // nbatch, seq // tc),
        in_specs=prev_specs + [pl.BlockSpec((nbatch, tc, S5_WIDTH), lambda b, c: (b, c, EVEN_U_BLOCK))]
        + [_layer_spec(prm[k].shape[1:], layer) for k in names],
        out_specs=[pl.BlockSpec((nbatch, tc, S5_WIDTH), lambda b, c: (b, c, 0)), st, st],
        out_shape=[jax.ShapeDtypeStruct((bsz, seq, S5_WIDTH), BF16), st_shape, st_shape],
        input_output_aliases=aliases,
        scratch_shapes=[pltpu.VMEM((2 * S5_SCHUNKS, nbatch * tc, LANES), F32),
                        pltpu.VMEM((2 * S5_SCHUNKS, nbatch, LANES), F32)],
        compiler_params=_cparams("parallel", "arbitrary"),
        name="s5_prompt",
    )(*prev, proj, *[prm[k] for k in names])


def _s5_sample_kernel(u_ref, wb_ref, wc_ref, lr_ref, li_ref, d_ref, wglu_ref, bglu_ref, h0r_ref, h0i_ref,
                      o_ref, sr_ref, si_ref, xr, xi, *, nseq, t_new):
    nt, st = nseq // SAMPLE_TILE, SAMPLE_TILE
    n = nseq * t_new
    u = u_ref[...]
    ub = u.astype(BF16)
    for cc in range(S5_UCHUNKS):
        res = jnp.dot(ub[:, cc * LANES:(cc + 1) * LANES], wb_ref[cc], preferred_element_type=F32)
        sc = slice(cc * S5_SUB, (cc + 1) * S5_SUB)
        xr[:, :, :, sc] = res[:, :S5_SUB].reshape(nt, t_new, st, S5_SUB)
        xi[:, :, :, sc] = res[:, S5_SUB:].reshape(nt, t_new, st, S5_SUB)
    lr, li = lr_ref[...], li_ref[...]
    hr, hi = h0r_ref[...], h0i_ref[...]
    for t in range(t_new):
        nr = lr * hr - li * hi + xr[:, t].reshape(nseq, S5_FLAT)
        ni = lr * hi + li * hr + xi[:, t].reshape(nseq, S5_FLAT)
        xr[:, t] = nr.reshape(nt, st, S5_FLAT)
        xi[:, t] = ni.reshape(nt, st, S5_FLAT)
        hr, hi = nr, ni
    sr_ref[...] = hr
    si_ref[...] = hi
    ys = []
    for cc in range(S5_UCHUNKS):
        sc = slice(cc * S5_SUB, (cc + 1) * S5_SUB)
        s = jnp.concatenate([xr[:, :, :, sc].reshape(n, S5_SUB), xi[:, :, :, sc].reshape(n, S5_SUB)],
                            axis=1).astype(BF16)
        cols = slice(cc * LANES, (cc + 1) * LANES)
        ys.append(jnp.dot(s, wc_ref[cc], preferred_element_type=F32) + d_ref[:, cols] * u[:, cols])
    o_ref[...] = _s5_tail(jnp.concatenate(ys, axis=1), wglu_ref, bglu_ref).astype(BF16)


def _s5_sample(proj, h0r, h0i, prm, t_new, layer, prev):
    n = proj.shape[0]
    nseq = n // t_new
    names = ("wb", "wc", "lr", "li", "d", "wglu", "bglu")
    st = pl.BlockSpec((None, nseq, S5_FLAT), lambda i: (layer, 0, 0))
    prev, prev_specs, aliases = _alias_inputs(prev, 1)
    scratch = pltpu.VMEM((nseq // SAMPLE_TILE, t_new, SAMPLE_TILE, S5_FLAT), F32)
    return pl.pallas_call(
        _skip_refs(functools.partial(_s5_sample_kernel, nseq=nseq, t_new=t_new), len(prev)),
        grid=(1,),
        in_specs=prev_specs + [pl.BlockSpec((n, S5_WIDTH), lambda i: (0, EVEN_U_BLOCK))]
        + [_layer_spec(prm[k].shape[1:], layer) for k in names]
        + [_layer_spec((nseq, S5_FLAT), layer), _layer_spec((nseq, S5_FLAT), layer)],
        out_specs=[pl.BlockSpec((n, S5_WIDTH), lambda i: (0, 0)), st, st],
        out_shape=[jax.ShapeDtypeStruct((n, S5_WIDTH), BF16),
                   jax.ShapeDtypeStruct(h0r.shape, F32), jax.ShapeDtypeStruct(h0i.shape, F32)],
        input_output_aliases=aliases,
        scratch_shapes=[scratch, scratch],
        compiler_params=_cparams("arbitrary"),
        name="s5_sample",
    )(*prev, proj, *[prm[k] for k in names], h0r, h0i)


def _s5_params(a_re, a_im, log_dt, b_re, b_im, c_re, c_im, d_skip, w_glu, b_glu):
    nl = a_re.shape[0]
    dt = jnp.exp(log_dt)
    mag = jnp.exp(a_re * dt)
    lr, li = mag * jnp.cos(a_im * dt), mag * jnp.sin(a_im * dt)
    den = a_re * a_re + a_im * a_im
    cr = ((lr - 1.0) * a_re + li * a_im) / den
    ci = (li * a_re - (lr - 1.0) * a_im) / den
    bbr = cr[..., None] * b_re - ci[..., None] * b_im
    bbi = cr[..., None] * b_im + ci[..., None] * b_re
    gpc = LANES // S5_GROUP
    eye = jnp.eye(gpc, dtype=F32)

    def in_blocks(bb):
        bb = bb.reshape(nl, S5_UCHUNKS, gpc, S5_STATE, S5_GROUP)
        return jnp.einsum("lcgph,gk->lcghkp", bb, eye).reshape(nl, S5_UCHUNKS, LANES, S5_SUB)

    def out_blocks(cm):
        cm = cm.reshape(nl, S5_UCHUNKS, gpc, S5_GROUP, S5_STATE)
        return jnp.einsum("lcghp,gk->lcgpkh", cm, eye).reshape(nl, S5_UCHUNKS, S5_SUB, LANES)

    wb = jnp.concatenate([in_blocks(bbr), in_blocks(bbi)], axis=3).astype(BF16)
    wc = jnp.concatenate([out_blocks(c_re), -out_blocks(c_im)], axis=2).astype(BF16)
    lr_f, li_f = lr.reshape(nl, 1, S5_FLAT), li.reshape(nl, 1, S5_FLAT)
    lam = jnp.concatenate([lr_f.reshape(nl, S5_SCHUNKS, 1, LANES), li_f.reshape(nl, S5_SCHUNKS, 1, LANES)], axis=1)
    lam8 = jnp.broadcast_to(lam, (nl, 2 * S5_SCHUNKS, SUBLANES, LANES))
    return dict(wb=wb, wc=wc, lam8=lam8, lr=lr_f, li=li_f, d=d_skip.reshape(nl, 1, S5_WIDTH),
                wglu=w_glu.astype(BF16), bglu=b_glu.reshape(nl, 1, S5_WIDTH))


ML_AUG = 2 * ML_DV


EXP_CLAMP = 88.0


def _den_floor(m_row):
    return jnp.exp(jnp.minimum(-m_row, EXP_CLAMP))


def _head_out(h, o, gout):
    hn = h * lax.rsqrt(jnp.mean(h * h, axis=-1, keepdims=True) + EPS) * gout
    return (hn * _sigmoid(o)).astype(BF16)


ODDP_V_BLOCK = 0
ODDP_O_BLOCK = 1
ODDP_Q_BLOCK = (2 * ML_WIDTH) // ML_QK
ODDP_G_BLOCK = (2 * ML_WIDTH + ML_QK) // LANES
ML_SPLIT = 3
ML_PIECE_LANES = 2 * ML_HEADS
ML_SEQS = 8
ML_STAGE_LAG = 2


def _norm_matmul_kt_kernel(x_ref, g_ref, wt_ref, wg_ref, o_ref, kt_ref):
    h = _rms(x_ref[...], g_ref[...]).astype(BF16)
    k0, v0, g0 = ML_QK, 2 * ML_QK, 2 * ML_QK + 2 * ML_WIDTH
    nt = (((1,), (1,)), ((), ()))
    o_ref[:, :g0 - v0] = lax.dot_general(h, wt_ref[v0:g0, :].astype(BF16), nt, preferred_element_type=F32)
    o_ref[:, g0 - v0:g0 - v0 + k0] = lax.dot_general(h, wt_ref[:k0, :].astype(BF16), nt, preferred_element_type=F32)
    o_ref[:, g0 - v0 + k0:] = jnp.dot(h, wg_ref[...], preferred_element_type=F32)
    kt = lax.dot_general(wt_ref[k0:v0, :].astype(BF16), h, nt, preferred_element_type=F32)
    kt_ref[...] = kt * (ML_DK ** -0.5)


def _norm_matmul_kt(xs, g, layer, wt, wg, widx):
    d = wt.shape[2]
    m_out = 2 * ML_WIDTH + ML_QK + wg.shape[2]
    return _ring_proj_call(
        _norm_matmul_kt_kernel, xs[0], xs[1], [g, wt, wg],
        [_layer_spec((1, d), layer), _layer_spec(wt.shape[1:], widx), _layer_spec(wg.shape[1:], widx)],
        [(m_out, F32, True), (ML_QK, F32, False)], "norm_matmul_kt")


def _cummax_rows(x):
    n = x.shape[0]
    row = lax.broadcasted_iota(jnp.int32, x.shape, 0)
    shift = 1
    while shift < n:
        x = jnp.maximum(x, jnp.where(row >= shift, pltpu.roll(x, shift, 0), NEG_INF))
        shift *= 2
    return x


def _pieces(x):
    lane = lax.broadcasted_iota(jnp.int32, x.shape, 1)
    xx = x + pltpu.roll(x, ML_PIECE_LANES, 1) + pltpu.roll(x, 2 * ML_PIECE_LANES, 1)
    a1, a2, a3 = _split3(xx)
    return jnp.where(lane < ML_PIECE_LANES, a1, jnp.where(lane < 2 * ML_PIECE_LANES, a2, a3))


def _ml_select_constants():
    mask = np.zeros((ML_HEADS, LANES), np.float32)
    sel = np.zeros((ML_HEADS, LANES, 2 * ML_DV), np.float32)
    for h in range(ML_HEADS):
        for k in range(ML_SPLIT):
            lo, hi = k * ML_PIECE_LANES + h, k * ML_PIECE_LANES + ML_HEADS + h
            mask[h, lo] = mask[h, hi] = 1.0
            sel[h, lo, :ML_DV] = 1.0
            sel[h, hi, ML_DV:] = 1.0
    return jnp.asarray(mask), jnp.asarray(sel, dtype=BF16)


def _mlstm_prompt_kernel(*refs, tc, nchunks, nseq):
    v_ref, o_ref, q_ref, g_ref = refs[:4]
    kt_refs = refs[4:4 + nseq]
    bias_ref, gout_ref, mask_ref, sel_ref, h_ref, c_ref, n_ref, m_ref, caug, mst = refs[4 + nseq:]
    ci = pl.program_id(1)

    @pl.when(ci == 0)
    def _():
        caug[...] = jnp.zeros_like(caug)
        mst[...] = jnp.zeros_like(mst)

    nh = ML_HEADS
    lane = lax.broadcasted_iota(jnp.int32, (tc, LANES), 1)
    lo, hi = lane < nh, (lane >= nh) & (lane < 2 * nh)
    rt = lax.broadcasted_iota(jnp.int32, (tc, tc), 0)
    cs = lax.broadcasted_iota(jnp.int32, (tc, tc), 1)
    causal = cs <= rt
    tril = jnp.where(causal, 1.0, 0.0).astype(BF16)
    ones = jnp.ones((tc, ML_DV), F32)

    def gates(sq):
        g = g_ref[sq] + bias_ref[...]
        lf = jnp.where(hi, _log_sigmoid(g), 0.0)
        b = sum(jnp.dot(tril, p, preferred_element_type=F32) for p in _split3(lf))
        c = jnp.where(hi, pltpu.roll(g, nh, 1) - b, 0.0)
        m_prev = mst[sq]
        mx = jnp.maximum(_cummax_rows(c), m_prev)
        m_row = b + mx
        mx_lo = pltpu.roll(mx, LANES - nh, 1)
        w_inter = jnp.exp(pltpu.roll(m_prev, LANES - nh, 1) - mx_lo)
        mst[sq] = m_row[tc - 1:tc, :]
        return dict(xc=_pieces(jnp.where(lo, w_inter, jnp.where(hi, _den_floor(m_row), 0.0))),
                    lc=_pieces(jnp.where(lo, -mx_lo, jnp.where(hi, 1.0, 0.0))),
                    rc=_pieces(jnp.where(lo, 1.0, jnp.where(hi, c, 0.0))))

    gt = [gates(sq) for sq in range(nseq)]
    units = [(sq, hd) for hd in range(nh) for sq in range(nseq)]
    st = [dict() for _ in units]

    def stage1(u):
        sq, hd = units[u]
        d = st[u]
        rh = gt[sq]["rc"] * mask_ref[hd:hd + 1, :].astype(BF16)
        d["dmat"] = lax.dot_general(gt[sq]["lc"], rh, (((1,), (1,)), ((), ())), preferred_element_type=F32)
        d["wb"] = jnp.dot(gt[sq]["xc"], sel_ref[hd], preferred_element_type=F32)
        d["qh"] = q_ref[sq, :, hd * ML_DK:(hd + 1) * ML_DK]
        d["kt"] = kt_refs[sq][hd * ML_DK:(hd + 1) * ML_DK, :]
        d["qk"] = jnp.dot(d["qh"].astype(BF16), d["kt"].astype(BF16), preferred_element_type=F32)

    def stage2(u):
        sq, hd = units[u]
        d = st[u]
        cols = slice(hd * ML_DV, (hd + 1) * ML_DV)
        d["w"] = jnp.exp(jnp.where(causal, d["dmat"], NEG_INF))
        d["vaug"] = jnp.concatenate([v_ref[sq, :, cols], ones], axis=1).astype(BF16)
        d["cm"] = caug[sq, hd]
        lhs = jnp.concatenate([(d["qk"] * d["w"]).astype(BF16), (d["wb"][:, :ML_DK] * d["qh"]).astype(BF16)], axis=1)
        rhs = jnp.concatenate([d["vaug"], d["cm"].astype(BF16)], axis=0)
        d["both"] = jnp.dot(lhs, rhs, preferred_element_type=F32)
        kw = (d["kt"] * d["w"][tc - 1:tc, :]).astype(BF16)
        d["upd"] = jnp.dot(kw, d["vaug"], preferred_element_type=F32)

    def stage3(u):
        sq, hd = units[u]
        d = st[u]
        cols = slice(hd * ML_DV, (hd + 1) * ML_DV)
        both, wb = d["both"], d["wb"]
        h = both[:, :ML_DV] / jnp.maximum(jnp.abs(both[:, ML_DV:]), wb[:, ML_DV:])
        h_ref[sq, :, cols] = _head_out(h, o_ref[sq, :, cols], gout_ref[:, cols])
        decay = wb[tc - 1:tc, :ML_DV]
        caug[sq, hd] = jnp.concatenate([decay, decay], axis=1) * d["cm"] + d["upd"]
        d.clear()

    for step in range(len(units) + 2 * ML_STAGE_LAG):
        if step < len(units):
            stage1(step)
        if 0 <= step - ML_STAGE_LAG < len(units):
            stage2(step - ML_STAGE_LAG)
        if 0 <= step - 2 * ML_STAGE_LAG < len(units):
            stage3(step - 2 * ML_STAGE_LAG)

    @pl.when(ci == nchunks - 1)
    def _():
        c_ref[...] = caug[:, :, :, :ML_DV]
        n_ref[...] = caug[:, :, :, ML_DV:]
        m_ref[...] = mst[...]


def _kt_index(b, c, *, sq, nseq, nchunks):
    return 0, (b * nseq + sq) * nchunks + c


def _mlstm_prompt(proj, kt, bias, gout, consts, layer, prev):
    bsz, seq, _ = proj.shape
    tc, nseq = ML_CHUNK, ML_SEQS
    nchunks = seq // tc
    n_layers = bias.shape[0]
    mask, sel = consts
    blk = lambda w, j: pl.BlockSpec((nseq, tc, w), lambda b, c: (b, c, j))
    st = lambda shape: pl.BlockSpec((None, nseq) + shape, lambda b, c: (layer, b) + (0,) * len(shape))
    st_shape = lambda shape: jax.ShapeDtypeStruct((n_layers, bsz) + shape, F32)
    prev, prev_specs, aliases = _alias_inputs(prev, 1)
    return pl.pallas_call(
        _skip_refs(functools.partial(_mlstm_prompt_kernel, tc=tc, nchunks=nchunks, nseq=nseq), len(prev)),
        grid=(bsz // nseq, nchunks),
        in_specs=prev_specs + [
            blk(ML_WIDTH, ODDP_V_BLOCK), blk(ML_WIDTH, ODDP_O_BLOCK), blk(ML_QK, ODDP_Q_BLOCK),
            blk(LANES, ODDP_G_BLOCK)]
        + [pl.BlockSpec((ML_QK, tc), functools.partial(_kt_index, sq=sq, nseq=nseq, nchunks=nchunks))
           for sq in range(nseq)] + [
            _layer_spec((1, LANES), layer), _layer_spec((1, ML_WIDTH), layer),
            _const_spec(mask.shape), _const_spec(sel.shape)],
        out_specs=[blk(ML_WIDTH, 0), st((ML_HEADS, ML_DK, ML_DV)), st((ML_HEADS, ML_DK, ML_DV)), st((1, LANES))],
        out_shape=[jax.ShapeDtypeStruct((bsz, seq, ML_WIDTH), BF16),
                   st_shape((ML_HEADS, ML_DK, ML_DV)), st_shape((ML_HEADS, ML_DK, ML_DV)), st_shape((1, LANES))],
        input_output_aliases=aliases,
        scratch_shapes=[pltpu.VMEM((nseq, ML_HEADS, ML_DK, ML_AUG), F32), pltpu.VMEM((nseq, 1, LANES), F32)],
        compiler_params=_cparams("parallel", "arbitrary"),
        name="mlstm_prompt",
    )(*prev, proj, proj, proj, proj, *([kt] * nseq), bias, gout, mask, sel)


MLS_SEQS = 32


def _mlstm_sample_kernel(v_ref, o_ref, q_ref, g_ref, kt_ref, bias_ref, gout_ref, mask_ref, sel_ref,
                         c0_ref, n0_ref, m0_ref, h_ref, c_ref, n_ref, m_ref, *, nseq, t_new):
    nh, nt, st = ML_HEADS, nseq // SAMPLE_TILE, SAMPLE_TILE
    rows = nseq * t_new
    lane = lax.broadcasted_iota(jnp.int32, (rows, LANES), 1)
    lo, hi = lane < nh, (lane >= nh) & (lane < 2 * nh)
    tiles = lambda a: a.reshape(nt, t_new, st, a.shape[-1])
    flat = lambda a: a.reshape(rows, a.shape[-1])
    per_seq = lambda a: a.reshape(nseq, a.shape[-1])

    g = g_ref[...] + bias_ref[...]
    lf = tiles(jnp.where(hi, _log_sigmoid(g), 0.0))
    ig = tiles(jnp.where(hi, pltpu.roll(g, nh, 1), 0.0))
    m_prev = m0_ref[...].reshape(nt, st, LANES)
    bs, cs, ms = [], [], []
    b_run, m_run = None, m_prev
    for t in range(t_new):
        b_run = lf[:, t] if b_run is None else b_run + lf[:, t]
        c_t = ig[:, t] - b_run
        m_run = jnp.maximum(m_run, c_t)
        bs.append(b_run)
        cs.append(c_t)
        ms.append(m_run)
    stack = lambda xs: flat(jnp.stack(xs, axis=1))
    b, c, mx = stack(bs), stack(cs), stack(ms)
    m_prev_rows = stack([m_prev] * t_new)
    m_row = b + mx
    m_ref[...] = per_seq(bs[-1] + ms[-1])
    mx_lo = pltpu.roll(mx, LANES - nh, 1)
    w_inter = jnp.exp(pltpu.roll(m_prev_rows, LANES - nh, 1) - mx_lo)
    xc = _pieces(jnp.where(lo, w_inter, jnp.where(hi, _den_floor(m_row), 0.0)))
    lc = _pieces(jnp.where(lo, -mx_lo, jnp.where(hi, 1.0, 0.0)))
    rc = _pieces(jnp.where(lo, 1.0, jnp.where(hi, c, 0.0)))

    def seq_of(idx):
        return (idx // (t_new * st)) * st + idx % st, (idx % (t_new * st)) // st

    rt = lax.broadcasted_iota(jnp.int32, (rows, rows), 0)
    ct = lax.broadcasted_iota(jnp.int32, (rows, rows), 1)
    (rs, rtok), (cseq, ctok) = seq_of(rt), seq_of(ct)
    valid = (rs == cseq) & (ctok <= rtok)
    rq = lax.broadcasted_iota(jnp.int32, (rows, nseq * ML_DK), 0)
    cq = lax.broadcasted_iota(jnp.int32, (rows, nseq * ML_DK), 1)
    own_q = seq_of(rq)[0] == cq // ML_DK
    rk = lax.broadcasted_iota(jnp.int32, (nseq * ML_DK, rows), 0)
    ck = lax.broadcasted_iota(jnp.int32, (nseq * ML_DK, rows), 1)
    own_k = rk // ML_DK == seq_of(ck)[0]
    ones = jnp.ones((rows, ML_DV), F32)
    last = lambda a: per_seq(tiles(a)[:, t_new - 1])

    for hd in range(nh):
        cols = slice(hd * ML_DV, (hd + 1) * ML_DV)
        rh = rc * mask_ref[hd:hd + 1, :].astype(BF16)
        dmat = lax.dot_general(lc, rh, (((1,), (1,)), ((), ())), preferred_element_type=F32)
        wb = jnp.dot(xc, sel_ref[hd], preferred_element_type=F32)
        qh = q_ref[:, hd * ML_DK:(hd + 1) * ML_DK]
        kt = kt_ref[hd * ML_DK:(hd + 1) * ML_DK, :]
        ktb = kt.astype(BF16)
        w = jnp.exp(jnp.where(valid, dmat, NEG_INF))
        qk = jnp.dot(qh.astype(BF16), ktb, preferred_element_type=F32) * w
        vaug = jnp.concatenate([v_ref[:, cols], ones], axis=1).astype(BF16)
        po = jnp.dot(qk.astype(BF16), vaug, preferred_element_type=F32)
        wq = wb[:, :ML_DK] * qh
        wq2 = jnp.concatenate([wq, wq], axis=1)
        wq_bd = jnp.where(own_q, jnp.concatenate([wq2] * (nseq * ML_DK // LANES), axis=1), 0.0).astype(BF16)
        cstack = c0_ref[:, hd].reshape(nseq * ML_DK, ML_DV)
        num = po[:, :ML_DV] + jnp.dot(wq_bd, cstack.astype(BF16), preferred_element_type=F32)
        n0 = n0_ref[hd]
        n_rows = stack([n0.reshape(nt, st, ML_DK)] * t_new)
        den = po[:, ML_DV:] + jnp.sum(wq * n_rows, axis=-1, keepdims=True)
        h = num / jnp.maximum(jnp.abs(den), wb[:, ML_DV:])
        h_ref[:, cols] = _head_out(h, o_ref[:, cols], gout_ref[:, cols])
        w_last = last(w)
        decay = last(wb[:, :ML_DV])
        n_upd = lax.dot_general(w_last.astype(BF16), ktb, (((1,), (1,)), ((), ())), preferred_element_type=F32)
        n_ref[hd] = decay[:, :ML_DK] * n0 + n_upd
        wk = jnp.sum(w_last, axis=0, keepdims=True)
        kw_bd = jnp.where(own_k, jnp.concatenate([kt * wk] * nseq, axis=0), 0.0).astype(BF16)
        upd = jnp.dot(kw_bd, v_ref[:, cols].astype(BF16), preferred_element_type=F32)
        decay_rows = jnp.broadcast_to(decay[:, None, :], (nseq, ML_DK, ML_DV)).reshape(nseq * ML_DK, ML_DV)
        c_ref[:, hd] = (decay_rows * cstack + upd).reshape(nseq, ML_DK, ML_DV)


def _mlstm_sample(proj, kt, bias, gout, consts, c0, n0h, m0, t_new, layer, prev):
    n = proj.shape[0]
    bsz = n // t_new
    nseq = MLS_SEQS
    rows = nseq * t_new
    mask, sel = consts
    blk = lambda w, j: pl.BlockSpec((rows, w), lambda i: (i, j))
    cst = pl.BlockSpec((None, nseq, ML_HEADS, ML_DK, ML_DV), lambda i: (layer, i, 0, 0, 0))
    nst = pl.BlockSpec((None, ML_HEADS, nseq, ML_DK), lambda i: (layer, 0, i, 0))
    mst = pl.BlockSpec((None, nseq, LANES), lambda i: (layer, i, 0))
    prev, prev_specs, aliases = _alias_inputs(prev, 1)
    return pl.pallas_call(
        _skip_refs(functools.partial(_mlstm_sample_kernel, nseq=nseq, t_new=t_new), len(prev)),
        grid=(bsz // nseq,),
        in_specs=prev_specs + [
            blk(ML_WIDTH, ODDP_V_BLOCK), blk(ML_WIDTH, ODDP_O_BLOCK), blk(ML_QK, ODDP_Q_BLOCK),
            blk(LANES, ODDP_G_BLOCK), pl.BlockSpec((ML_QK, rows), lambda i: (0, i)),
            _layer_spec((1, LANES), layer), _layer_spec((1, ML_WIDTH), layer),
            _const_spec(mask.shape), _const_spec(sel.shape), cst, nst, mst],
        out_specs=[blk(ML_WIDTH, 0), cst, nst, mst],
        out_shape=[jax.ShapeDtypeStruct((n, ML_WIDTH), BF16),
                   jax.ShapeDtypeStruct(c0.shape, F32), jax.ShapeDtypeStruct(n0h.shape, F32),
                   jax.ShapeDtypeStruct(m0.shape, F32)],
        input_output_aliases=aliases,
        compiler_params=_cparams("parallel"),
        name="mlstm_sample",
    )(*prev, proj, proj, proj, proj, kt, bias, gout, mask, sel, c0, n0h, m0)


def _pad_lanes(x):
    return jnp.pad(x, [(0, 0)] * (x.ndim - 1) + [(0, LANES - x.shape[-1])])


def kernel(x_prompt, x_sample, cache_k, cache_v, state_ssm_re, state_ssm_im, state_mlstm_c, state_mlstm_n, state_mlstm_m, norm_mix, norm_ffn, w_in_even, q_norm, k_norm, attn_sinks, s5_a_re, s5_a_im, s5_log_dt, s5_b_re, s5_b_im, s5_c_re, s5_c_im, s5_d, s5_w_glu, s5_b_glu, w_out_even, w_in_odd, ml_b_i, ml_b_f, ml_out_norm, w_out_odd, w_gate, w_up, w_down):
    bp, lp, _ = x_prompt.shape
    bsm, ls, _ = x_sample.shape
    yp = x_prompt.reshape(bp * lp, D_MODEL)
    ys = x_sample.reshape(bsm // SAMPLE_TILE, SAMPLE_TILE, ls, D_MODEL).transpose(0, 2, 1, 3).reshape(bsm * ls, D_MODEL)
    tab_p = _rope_tables(jnp.arange(lp))
    tab_s = tuple(jnp.repeat(t, SAMPLE_TILE, axis=0) for t in _rope_tables(PAST_LEN + jnp.arange(ls)))
    n_even, n_odd = w_in_even.shape[0], w_in_odd.shape[0]

    g_mix = norm_mix.reshape(DEPTH, 1, D_MODEL)
    g_ffn = norm_ffn.reshape(DEPTH, 1, D_MODEL)
    wg, wu, wd = w_gate.astype(BF16), w_up.astype(BF16), w_down.astype(BF16)
    kv0, u0 = ATTN_WIDTH, ATTN_WIDTH + 2 * KV_WIDTH
    order = jnp.asarray(ATTN_HEAD_ORDER)
    wq = w_in_even[..., :kv0].reshape(n_even, D_MODEL, ATTN_HEADS, HEAD_DIM)[:, :, order].reshape(n_even, D_MODEL, kv0)
    w_in_e = jnp.concatenate([wq, w_in_even[..., u0:], w_in_even[..., kv0:u0]], axis=-1).astype(BF16)
    wo_attn = w_out_even[:, :kv0].reshape(n_even, ATTN_HEADS, HEAD_DIM, D_MODEL)[:, order].reshape(n_even, kv0, D_MODEL)
    w_out_e = jnp.concatenate([wo_attn, w_out_even[:, kv0:]], axis=1).astype(BF16)
    gq = jnp.tile(q_norm, (1, LANES // HEAD_DIM)).reshape(n_even, 1, LANES)
    gk = jnp.tile(k_norm, (1, LANES // HEAD_DIM)).reshape(n_even, 1, LANES)
    prm = _s5_params(s5_a_re, s5_a_im, s5_log_dt, s5_b_re, s5_b_im, s5_c_re, s5_c_im, s5_d, s5_w_glu, s5_b_glu)
    w_gates_o = _pad_lanes(w_in_odd[..., 2 * ML_QK + 2 * ML_WIDTH:]).astype(BF16)
    w_in_ot = jnp.swapaxes(w_in_odd, 1, 2)
    ml_consts = _ml_select_constants()
    w_out_o = w_out_odd.astype(BF16)
    ml_bias = _pad_lanes(jnp.concatenate([ml_b_i, ml_b_f], axis=-1)).reshape(n_odd, 1, LANES)
    ml_gout = ml_out_norm.reshape(n_odd, 1, ML_WIDTH)
    keys_last = lambda a: a.transpose(0, 1, 3, 4, 2).reshape(n_even, bsm, KV_WIDTH, WINDOW)
    ck, cv = keys_last(cache_k), keys_last(cache_v)
    h0r = state_ssm_re.reshape(n_even, bsm, S5_FLAT)
    h0i = state_ssm_im.reshape(n_even, bsm, S5_FLAT)
    n0h = jnp.swapaxes(state_mlstm_n, 1, 2)
    m0 = jnp.pad(state_mlstm_m, ((0, 0), (0, 0), (ML_HEADS, LANES - 2 * ML_HEADS)))

    p_attn = p_ssm = p_ml = s_attn = s_ssm = s_ml = None
    for layer in range(DEPTH):
        ffn = (layer, g_ffn, wg, wu, wd)
        if layer % 2 == 0:
            e = layer // 2
            proj_p, proj_s = _norm_matmul([yp, ys], g_mix, layer, w_in_e, e)
            proj3 = proj_p.reshape(bp, lp, -1)
            attn_p, *p_attn = _attn_prompt(proj3, tab_p, gq, gk, attn_sinks, e, p_attn)
            ssm_p, *p_ssm = _s5_prompt(proj3, prm, e, p_ssm)
            attn_s, *s_attn = _attn_sample(proj_s, ck, cv, tab_s, gq, gk, attn_sinks, ls, e, s_attn)
            ssm_s, *s_ssm = _s5_sample(proj_s, h0r, h0i, prm, ls, e, s_ssm)
            yp, ys = _mix_ffn([[yp, attn_p.reshape(bp * lp, -1), ssm_p.reshape(bp * lp, -1)], [ys, attn_s, ssm_s]],
                              w_out_e, e, *ffn)
        else:
            o = layer // 2
            (proj_p, kt_p), (proj_s, kt_s) = _norm_matmul_kt([yp, ys], g_mix, layer, w_in_ot, w_gates_o, o)
            hh_p, *p_ml = _mlstm_prompt(proj_p.reshape(bp, lp, -1), kt_p, ml_bias, ml_gout, ml_consts, o, p_ml)
            hh_s, *s_ml = _mlstm_sample(proj_s, kt_s, ml_bias, ml_gout, ml_consts, state_mlstm_c, n0h, m0, ls, o,
                                        s_ml)
            yp, ys = _mix_ffn([[yp, hh_p.reshape(bp * lp, -1)], [ys, hh_s]], w_out_o, o, *ffn)
    heads = lambda a: a.reshape(a.shape[:3] + (KV_HEADS, HEAD_DIM))
    groups = lambda a: a.reshape(a.shape[:2] + (S5_GROUPS, S5_STATE))
    keys_first = lambda a: a.reshape(a.shape[:2] + (KV_HEADS, HEAD_DIM, WINDOW)).transpose(0, 1, 4, 2, 3)
    ys = ys.reshape(bsm // SAMPLE_TILE, ls, SAMPLE_TILE, D_MODEL).transpose(0, 2, 1, 3).reshape(bsm, ls, D_MODEL)
    return (yp.reshape(bp, lp, D_MODEL), ys,
            heads(p_attn[0]), heads(p_attn[1]), groups(p_ssm[0]), groups(p_ssm[1]),
            p_ml[0], p_ml[1][..., 0], p_ml[2][:, :, 0, ML_HEADS:2 * ML_HEADS],
            keys_first(s_attn[0]), keys_first(s_attn[1]), groups(s_ssm[0]), groups(s_ssm[1]),
            s_ml[0], jnp.swapaxes(s_ml[1], 1, 2), s_ml[2][..., ML_HEADS:2 * ML_HEADS])
```

```python
import functools

import numpy as np

import jax
import jax.numpy as jnp
from jax import lax
from jax.experimental import pallas as pl
from jax.experimental.pallas import tpu as pltpu

F32 = jnp.float32
BF16 = jnp.bfloat16

D_MODEL = 1024
DEPTH = 4
PAST_LEN = 8192
WINDOW = 128
ATTN_HEADS = 8
KV_HEADS = 2
HEAD_DIM = 64
ATTN_WIDTH = ATTN_HEADS * HEAD_DIM
KV_WIDTH = KV_HEADS * HEAD_DIM
ROT_DIM = HEAD_DIM // 4
ROPE_THETA = 500000.0
S5_GROUP = 16
S5_WIDTH = D_MODEL // 2
S5_GROUPS = S5_WIDTH // S5_GROUP
S5_STATE = 64
S5_FLAT = S5_GROUPS * S5_STATE
ML_HEADS = 8
ML_DV = D_MODEL // ML_HEADS
ML_DK = ML_DV // 2
ML_QK = ML_HEADS * ML_DK
ML_WIDTH = ML_HEADS * ML_DV
D_FF = 2816
EPS = 1e-6

LANES = 128
SUBLANES = 8
ROW_TILE = 512
FF_TILE = 256
S5_CHUNK = 128
ML_CHUNK = 128
SAMPLE_TILE = SUBLANES
VMEM_LIMIT = 56 * 1024 * 1024

NEG_INF = float("-inf")


def _cparams(*sem):
    return pltpu.CompilerParams(dimension_semantics=sem, vmem_limit_bytes=VMEM_LIMIT)


def _const_spec(shape):
    zeros = (0,) * len(shape)
    return pl.BlockSpec(shape, lambda *_: zeros, pipeline_mode=pl.Buffered(1))


def _layer_spec(shape, layer):
    zeros = (0,) * len(shape)
    return pl.BlockSpec((None,) + tuple(shape), lambda *_: (layer,) + zeros, pipeline_mode=pl.Buffered(1))


def _skip_refs(body, n_skip):
    if n_skip == 0:
        return body

    def wrapped(*refs):
        return body(*refs[n_skip:])

    return wrapped


def _alias_inputs(prev, first_state_out):
    prev = () if prev is None else tuple(prev)
    specs = [pl.BlockSpec(memory_space=pl.ANY) for _ in prev]
    aliases = {i: first_state_out + i for i in range(len(prev))}
    return prev, specs, aliases


def _rms(x, g):
    ms = jnp.mean(x * x, axis=-1, keepdims=True)
    return x * lax.rsqrt(ms + EPS) * g


def _split3(a):
    a1 = a.astype(BF16)
    r1 = a - a1.astype(F32)
    a2 = r1.astype(BF16)
    a3 = (r1 - a2.astype(F32)).astype(BF16)
    return a1, a2, a3


def _log_sigmoid(x):
    return jnp.minimum(x, 0.0) - jnp.log(1.0 + jnp.exp(-jnp.abs(x)))


def _sigmoid(x):
    return 1.0 / (1.0 + jnp.exp(-x))


def _norm_matmul_kernel(x_ref, g_ref, w_ref, o_ref):
    h = _rms(x_ref[...], g_ref[...]).astype(BF16)
    o_ref[...] = jnp.dot(h, w_ref[...], preferred_element_type=F32)


def _row_groups_call(body, groups, consts, const_specs, out_defs, scratch_shapes, name, side=None):
    steps, tiles = [], []
    for arrays in groups:
        n = arrays[0].shape[0]
        tm = min(ROW_TILE, n)
        tiles.append(tm)
        steps.append(n // tm)
    offs = [sum(steps[:k]) for k in range(len(groups))]

    def local(k):
        return lambda i: jnp.clip(i - offs[k], 0, steps[k] - 1)

    in_specs, out_specs, out_shape, args = [], [], [], []
    for k, arrays in enumerate(groups):
        for a in arrays:
            in_specs.append(pl.BlockSpec((tiles[k], a.shape[1]), lambda i, f=local(k): (f(i), 0)))
            args.append(a)
    for k, arrays in enumerate(groups):
        n = arrays[0].shape[0]
        for width, dtype, by_rows in out_defs:
            if by_rows:
                out_specs.append(pl.BlockSpec((tiles[k], width), lambda i, f=local(k): (f(i), 0)))
                out_shape.append(jax.ShapeDtypeStruct((n, width), dtype))
            else:
                out_specs.append(pl.BlockSpec((width, tiles[k]), lambda i, f=local(k): (0, f(i))))
                out_shape.append(jax.ShapeDtypeStruct((width, n), dtype))
    n_in = [len(arrays) for arrays in groups]
    n_out = len(out_defs)
    side_in, side_out, side_scratch, before, after = side if side is not None else ([], [], [], None, None)
    hbm = pl.BlockSpec(memory_space=pl.ANY)
    n_steps = sum(steps)

    def kern(*refs):
        i = pl.program_id(0)
        pos = 0
        ins = []
        for cnt in n_in:
            ins.append(refs[pos:pos + cnt])
            pos += cnt
        crefs = refs[pos:pos + len(consts)]
        pos += len(consts)
        s_in = refs[pos:pos + len(side_in)]
        pos += len(side_in)
        outs = [refs[pos + k * n_out:pos + (k + 1) * n_out] for k in range(len(groups))]
        pos += len(groups) * n_out
        s_out = refs[pos:pos + len(side_out)]
        pos += len(side_out)
        scratch = refs[pos:pos + len(scratch_shapes)]
        s_scratch = refs[pos + len(scratch_shapes):]
        if before is not None:
            before(i, n_steps, s_in, s_out, s_scratch)
        for k in range(len(groups)):
            @pl.when((i >= offs[k]) & (i < offs[k] + steps[k]))
            def _(k=k):
                body(*ins[k], *crefs, *outs[k], *scratch)
        if after is not None:
            after(i, n_steps, s_in, s_out, s_scratch)

    res = pl.pallas_call(
        kern,
        grid=(n_steps,),
        in_specs=in_specs + list(const_specs) + [hbm] * len(side_in),
        out_specs=out_specs + [hbm] * len(side_out),
        out_shape=out_shape + list(side_out),
        scratch_shapes=list(scratch_shapes) + list(side_scratch),
        compiler_params=_cparams("arbitrary"),
        name=name,
    )(*args, *consts, *side_in)
    grouped = [res[k * n_out:(k + 1) * n_out] for k in range(len(groups))]
    return grouped + [list(res[len(groups) * n_out:])] if side is not None else grouped


RING_SLOTS = 3


def _ring_proj_call(body, x_main, x_tail, consts, const_specs, out_defs, name):
    n, d = x_main.shape
    tm = ROW_TILE
    steps = n // tm
    n_tail = x_tail.shape[0]
    assert n % tm == 0 and steps >= RING_SLOTS - 1 and n_tail <= tm

    def main_idx(i):
        return jnp.minimum(i, steps - 1)

    out_specs, out_shape = [], []
    for rows, idx in ((n, main_idx), (n_tail, lambda i: 0)):
        tile = min(tm, rows)
        for width, dtype, by_rows in out_defs:
            if by_rows:
                out_specs.append(pl.BlockSpec((tile, width), lambda i, f=idx: (f(i), 0)))
                out_shape.append(jax.ShapeDtypeStruct((rows, width), dtype))
            else:
                out_specs.append(pl.BlockSpec((width, tile), lambda i, f=idx: (0, f(i))))
                out_shape.append(jax.ShapeDtypeStruct((width, rows), dtype))
    n_out = len(out_defs)

    def kern(x_hbm, xt_ref, *refs):
        crefs = refs[:len(consts)]
        outs_main = refs[len(consts):len(consts) + n_out]
        outs_tail = refs[len(consts) + n_out:len(consts) + 2 * n_out]
        xbuf, sem = refs[len(consts) + 2 * n_out:]
        i = pl.program_id(0)

        def tile_copy(j, slot):
            rows = pl.ds(pl.multiple_of(j * tm, tm), tm)
            return pltpu.make_async_copy(x_hbm.at[rows, :], xbuf.at[slot], sem.at[slot])

        @pl.when(i == 0)
        def _():
            for j in range(RING_SLOTS - 1):
                tile_copy(j, j).start()

        @pl.when(i + RING_SLOTS - 1 < steps)
        def _():
            nxt = i + RING_SLOTS - 1
            tile_copy(nxt, nxt % RING_SLOTS).start()

        @pl.when(i < steps)
        def _():
            slot = i % RING_SLOTS
            tile_copy(i, slot).wait()
            body(xbuf.at[slot], *crefs, *outs_main)

        @pl.when(i >= steps)
        def _():
            body(xt_ref, *crefs, *outs_tail)

    res = pl.pallas_call(
        kern,
        grid=(steps + 1,),
        in_specs=[pl.BlockSpec(memory_space=pl.ANY), _const_spec((n_tail, d))] + list(const_specs),
        out_specs=out_specs,
        out_shape=out_shape,
        scratch_shapes=[pltpu.VMEM((RING_SLOTS, tm, d), F32), pltpu.SemaphoreType.DMA((RING_SLOTS,))],
        compiler_params=_cparams("arbitrary"),
        name=name,
    )(x_main, x_tail, *consts)
    return [res[:n_out], res[n_out:]]


def _norm_matmul(xs, g, layer, w, widx):
    d, m = w.shape[1], w.shape[2]
    res = _ring_proj_call(_norm_matmul_kernel, xs[0], xs[1], [g, w],
                          [_layer_spec((1, d), layer), _layer_spec((d, m), widx)],
                          [(m, F32, True)], "norm_matmul")
    return [r[0] for r in res]


def _mix_ffn_kernel(*refs, n_mix):
    x_ref = refs[0]
    a_refs = refs[1:1 + n_mix]
    wo_ref, g_ref, wg_ref, wu_ref, wd_ref, o_ref, act_ref = refs[1 + n_mix:]
    y = x_ref[...]
    off = 0
    for a_ref in a_refs:
        ka = a_ref.shape[1]
        y = y + jnp.dot(a_ref[...], wo_ref[off:off + ka, :], preferred_element_type=F32)
        off += ka
    h = _rms(y, g_ref[...]).astype(BF16)
    for f in range(D_FF // FF_TILE):
        cols = slice(f * FF_TILE, (f + 1) * FF_TILE)
        gate = jnp.dot(h, wg_ref[:, cols], preferred_element_type=F32)
        up = jnp.dot(h, wu_ref[:, cols], preferred_element_type=F32)
        act_ref[:, cols] = (gate * _sigmoid(gate) * up).astype(BF16)
    o_ref[...] = y + jnp.dot(act_ref[...], wd_ref[...], preferred_element_type=F32)


CAST_STEPS_IN = 32
CAST_ROWS_DOWN = 128


def _ffn_cast_side(w_gate, w_up, w_down, nxt):
    rows_in = D_MODEL // CAST_STEPS_IN
    steps_down = D_FF // CAST_ROWS_DOWN
    srcs = [w_gate, w_up, w_down]
    outs = [jax.ShapeDtypeStruct((1,) + w.shape[1:], BF16) for w in srcs]
    scratch = [pltpu.VMEM((2, rows_in, D_FF), F32), pltpu.VMEM((CAST_ROWS_DOWN, D_MODEL), F32),
               pltpu.VMEM((2, rows_in, D_FF), BF16), pltpu.VMEM((CAST_ROWS_DOWN, D_MODEL), BF16),
               pltpu.SemaphoreType.DMA((6,))]

    def copy_in(j, w, in_refs, scr):
        ibuf, dbuf, _, _, sem = scr
        nrows = CAST_ROWS_DOWN if w == 2 else rows_in
        rows = pl.ds(pl.multiple_of(j * nrows, nrows), nrows)
        return pltpu.make_async_copy(in_refs[w].at[nxt, rows, :], dbuf if w == 2 else ibuf.at[w], sem.at[w])

    def copy_out(j, w, out_refs, scr):
        _, _, obuf, odbuf, sem = scr
        nrows = CAST_ROWS_DOWN if w == 2 else rows_in
        rows = pl.ds(pl.multiple_of(j * nrows, nrows), nrows)
        return pltpu.make_async_copy(odbuf if w == 2 else obuf.at[w], out_refs[w].at[0, rows, :], sem.at[3 + w])

    def before(i, n_steps, in_refs, out_refs, scr):
        assert n_steps > max(CAST_STEPS_IN, steps_down)

        @pl.when((i >= 1) & (i <= CAST_STEPS_IN))
        def _():
            copy_out(i - 1, 0, out_refs, scr).wait()
            copy_out(i - 1, 1, out_refs, scr).wait()

        @pl.when((i >= 1) & (i <= steps_down))
        def _():
            copy_out(i - 1, 2, out_refs, scr).wait()

        @pl.when(i < CAST_STEPS_IN)
        def _():
            copy_in(i, 0, in_refs, scr).start()
            copy_in(i, 1, in_refs, scr).start()

        @pl.when(i < steps_down)
        def _():
            copy_in(i, 2, in_refs, scr).start()

    def after(i, n_steps, in_refs, out_refs, scr):
        ibuf, dbuf, obuf, odbuf, _ = scr

        @pl.when(i < CAST_STEPS_IN)
        def _():
            for w in range(2):
                copy_in(i, w, in_refs, scr).wait()
                obuf[w] = ibuf[w].astype(BF16)
                copy_out(i, w, out_refs, scr).start()

        @pl.when(i < steps_down)
        def _():
            copy_in(i, 2, in_refs, scr).wait()
            odbuf[...] = dbuf[...].astype(BF16)
            copy_out(i, 2, out_refs, scr).start()

    return srcs, outs, scratch, before, after


def _mix_ffn(groups, w_out, oidx, layer, g_ffn, wg, wu, wd, cast_next=None):
    d = w_out.shape[2]
    n_mix = len(groups[0]) - 1
    tm = min(ROW_TILE, max(g[0].shape[0] for g in groups))
    res = _row_groups_call(
        functools.partial(_mix_ffn_kernel, n_mix=n_mix), groups, [w_out, g_ffn, wg, wu, wd],
        [_layer_spec(w_out.shape[1:], oidx), _layer_spec((1, d), layer), _layer_spec(wg.shape[1:], 0),
         _layer_spec(wu.shape[1:], 0), _layer_spec(wd.shape[1:], 0)],
        [(d, F32, True)], [pltpu.VMEM((tm, D_FF), BF16)], "mix_ffn",
        side=None if cast_next is None else _ffn_cast_side(*cast_next))
    ys = [r[0] for r in res[:len(groups)]]
    return ys + [tuple(res[len(groups)])] if cast_next is not None else ys + [None]


def _head_ones():
    r = lax.broadcasted_iota(jnp.int32, (LANES, LANES), 0) // HEAD_DIM
    c = lax.broadcasted_iota(jnp.int32, (LANES, LANES), 1) // HEAD_DIM
    return jnp.where(r == c, 1.0, 0.0).astype(BF16)


def _qk_prep(x, g, ones, ct, sa, sb):
    x2 = x * x
    hi = x2.astype(BF16)
    lo = (x2 - hi.astype(F32)).astype(BF16)
    ss = jnp.dot(hi, ones, preferred_element_type=F32) + jnp.dot(lo, ones, preferred_element_type=F32)
    xn = x * lax.rsqrt(ss * (1.0 / HEAD_DIM) + EPS) * g
    return xn * ct + pltpu.roll(xn, LANES - ROT_DIM // 2, 1) * sa + pltpu.roll(xn, ROT_DIM // 2, 1) * sb


def _rope_tables(pos):
    half = ROT_DIM // 2
    inv = jnp.power(jnp.float32(ROPE_THETA), -jnp.arange(half, dtype=F32) / half)
    ang = pos.astype(F32)[:, None] * inv[None, :]
    cos, sin = jnp.cos(ang), jnp.sin(ang)
    n = pos.shape[0]
    one = jnp.ones((n, HEAD_DIM - ROT_DIM), F32)
    zero = jnp.zeros((n, HEAD_DIM - ROT_DIM), F32)
    z8 = jnp.zeros((n, half), F32)
    ct = jnp.concatenate([cos, cos, one], axis=1)
    sa = jnp.concatenate([-sin, z8, zero], axis=1)
    sb = jnp.concatenate([z8, sin, zero], axis=1)
    tile = lambda t: jnp.concatenate([t, t], axis=1)
    return tile(ct), tile(sa), tile(sb)


ATTN_SEQS = 4
ATTN_QCHUNKS = ATTN_WIDTH // LANES
ATTN_HEAD_ORDER = tuple(h * ATTN_QCHUNKS + j for j in range(ATTN_QCHUNKS) for h in range(KV_HEADS))


def _attn_prompt_kernel(q_ref, kv_ref, ct_ref, sa_ref, sb_ref, gq_ref, gk_ref, sink_ref,
                        o_ref, pk_ref, pv_ref, kprev, vprev, *, nb, layer, nseq):
    i = pl.program_id(1)

    @pl.when(i == 0)
    def _():
        kprev[...] = jnp.zeros_like(kprev)
        vprev[...] = jnp.zeros_like(vprev)

    ones = _head_ones()
    ct, sa, sb = ct_ref[...], sa_ref[...], sb_ref[...]
    r = lax.broadcasted_iota(jnp.int32, (WINDOW, 2 * WINDOW), 0)
    c = lax.broadcasted_iota(jnp.int32, (WINDOW, 2 * WINDOW), 1)
    rel = r + WINDOW - c
    mask = (rel >= 0) & (rel <= WINDOW) & ((c >= WINDOW) | (i > 0))
    lane = lax.broadcasted_iota(jnp.int32, (WINDOW, LANES), 1)
    group0 = lane < HEAD_DIM
    v_ones = jnp.ones((2 * WINDOW, LANES), BF16)
    nq = ATTN_QCHUNKS
    st = [dict() for _ in range(nseq)]

    def prep(sq):
        d = st[sq]
        kv = kv_ref[sq]
        d["kn"] = _qk_prep(kv[:, :KV_WIDTH], gk_ref[...], ones, ct, sa, sb)
        d["v"] = kv[:, KV_WIDTH:]
        d["qn"] = [_qk_prep(q_ref[sq, :, j * LANES:(j + 1) * LANES], gq_ref[...], ones, ct, sa, sb)
                   * (HEAD_DIM ** -0.5) for j in range(nq)]
        d["kcat"] = jnp.concatenate([kprev[sq], d["kn"]], axis=0).astype(BF16)
        d["vaug"] = jnp.concatenate([jnp.concatenate([vprev[sq], d["v"]], axis=0).astype(BF16), v_ones], axis=1)
        kprev[sq] = d["kn"]
        vprev[sq] = d["v"]

    def scores(sq, h):
        d = st[sq]
        keep = group0 if h == 0 else jnp.logical_not(group0)
        qs = jnp.concatenate([jnp.where(keep, qj, 0.0) for qj in d["qn"]], axis=0).astype(BF16)
        d["s", h] = lax.dot_general(qs, d["kcat"], (((1,), (1,)), ((), ())), preferred_element_type=F32)

    def softmax_pv(sq, h):
        d = st[sq]
        s = d.pop(("s", h))
        ps, corr = [], []
        for j in range(nq):
            sg = jnp.where(mask, s[j * WINDOW:(j + 1) * WINDOW], NEG_INF)
            sink = sink_ref[layer, h * nq + j]
            m = jnp.maximum(jnp.max(sg, axis=-1, keepdims=True), sink)
            ps.append(jnp.exp(sg - m).astype(BF16))
            corr.append(jnp.exp(sink - m))
        o = jnp.dot(jnp.concatenate(ps, axis=0), d["vaug"], preferred_element_type=F32)
        d["o", h] = [o[j * WINDOW:(j + 1) * WINDOW, :LANES] / (o[j * WINDOW:(j + 1) * WINDOW, LANES:] + corr[j])
                     for j in range(nq)]

    def finish(sq):
        d = st[sq]
        o_ref[sq] = jnp.concatenate([jnp.where(group0, d["o", 0][j], d["o", 1][j]) for j in range(nq)],
                                    axis=1).astype(BF16)

    for sq in range(nseq):
        prep(sq)
    for sq in range(nseq):
        scores(sq, 0)
        scores(sq, 1)
    for sq in range(nseq):
        softmax_pv(sq, 0)
        softmax_pv(sq, 1)
        finish(sq)

    @pl.when(i == nb - 1)
    def _():
        for sq in range(nseq):
            pk_ref[sq] = st[sq]["kn"]
            pv_ref[sq] = st[sq]["v"]


def _attn_prompt(proj, tables, gq, gk, sinks, layer, prev):
    bsz, seq, _ = proj.shape
    nb = seq // WINDOW
    nseq = ATTN_SEQS
    n_layers = gq.shape[0]
    tab = pl.BlockSpec((WINDOW, LANES), lambda b, i: (i, 0))
    prev, prev_specs, aliases = _alias_inputs(prev, 1)
    win = pl.BlockSpec((None, nseq, WINDOW, KV_WIDTH), lambda b, i: (layer, b, 0, 0))
    win_shape = jax.ShapeDtypeStruct((n_layers, bsz, WINDOW, KV_WIDTH), F32)
    return pl.pallas_call(
        _skip_refs(functools.partial(_attn_prompt_kernel, nb=nb, layer=layer, nseq=nseq), len(prev)),
        grid=(bsz // nseq, nb),
        in_specs=prev_specs + [
            pl.BlockSpec((nseq, WINDOW, ATTN_WIDTH), lambda b, i: (b, i, 0)),
            pl.BlockSpec((nseq, WINDOW, 2 * KV_WIDTH), lambda b, i: (b, i, EVEN_KV_BLOCK)),
            tab, tab, tab, _layer_spec((1, LANES), layer), _layer_spec((1, LANES), layer),
            pl.BlockSpec(memory_space=pltpu.SMEM)],
        out_specs=[pl.BlockSpec((nseq, WINDOW, ATTN_WIDTH), lambda b, i: (b, i, 0)), win, win],
        out_shape=[jax.ShapeDtypeStruct((bsz, seq, ATTN_WIDTH), BF16), win_shape, win_shape],
        input_output_aliases=aliases,
        scratch_shapes=[pltpu.VMEM((nseq, WINDOW, KV_WIDTH), F32), pltpu.VMEM((nseq, WINDOW, KV_WIDTH), F32)],
        compiler_params=_cparams("parallel", "arbitrary"),
        name="attn_prompt",
    )(*prev, proj, proj, *tables, gq, gk, sinks)


EVEN_U_BLOCK = ATTN_WIDTH // S5_WIDTH
EVEN_KV_BLOCK = (ATTN_WIDTH + S5_WIDTH) // (2 * KV_WIDTH)
KALL_ROWS = WINDOW + SUBLANES


def _attn_sample_kernel(q_ref, kv_ref, ck_ref, cv_ref, ct_ref, sa_ref, sb_ref, gq_ref, gk_ref, sink_ref,
                        o_ref, nk_ref, nv_ref, o_seq, *, bs, t_new, layer):
    ones = _head_ones()
    ct, sa, sb = ct_ref[...], sa_ref[...], sb_ref[...]
    kv = kv_ref[...]
    kn = _qk_prep(kv[:, :KV_WIDTH], gk_ref[...], ones, ct, sa, sb)
    v = kv[:, KV_WIDTH:]
    nq = ATTN_QCHUNKS
    qn = [_qk_prep(q_ref[:, j * LANES:(j + 1) * LANES], gq_ref[...], ones, ct, sa, sb) * (HEAD_DIM ** -0.5)
          for j in range(nq)]
    rows = nq * t_new
    r = lax.broadcasted_iota(jnp.int32, (rows, KALL_ROWS), 0)
    c = lax.broadcasted_iota(jnp.int32, (rows, KALL_ROWS), 1)
    t = r % t_new
    mask = (c >= t) & (c <= t + WINDOW)
    rj = lax.broadcasted_iota(jnp.int32, (rows, 1), 0) // t_new
    lane = lax.broadcasted_iota(jnp.int32, (t_new, LANES), 1)
    group0 = lane < HEAD_DIM
    pad = jnp.zeros((KALL_ROWS - WINDOW - t_new, KV_WIDTH), F32)
    ones_c = jnp.ones((LANES, WINDOW), BF16)
    ones_n = jnp.ones((KALL_ROWS - WINDOW, LANES), BF16)
    klane = lax.broadcasted_iota(jnp.int32, (KV_WIDTH, WINDOW), 1)
    zcols = jnp.zeros((KV_WIDTH, WINDOW - (KALL_ROWS - WINDOW)), F32)
    nt = (((1,), (1,)), ((), ()))

    def shifted(cache_t, new_rows):
        new_t = jnp.concatenate([jnp.concatenate([new_rows, pad], axis=0).T, zcols], axis=1)
        return jnp.where(klane >= WINDOW - t_new, pltpu.roll(new_t, WINDOW - t_new, 1),
                         pltpu.roll(cache_t, WINDOW - t_new, 1))

    def seq_rows(a, b):
        return jnp.concatenate([a[tt * bs + b:tt * bs + b + 1] for tt in range(t_new)], axis=0)

    sinks = []
    for h in range(KV_HEADS):
        sk = jnp.zeros((rows, 1), F32)
        for j in range(nq):
            sk = jnp.where(rj == j, sink_ref[layer, h * nq + j], sk)
        sinks.append(sk)

    st = [dict() for _ in range(bs)]
    for b in range(bs):
        d = st[b]
        ck, cv = ck_ref[b], cv_ref[b]
        kn_b, v_b = seq_rows(kn, b), seq_rows(v, b)
        nk_ref[b] = shifted(ck, kn_b)
        nv_ref[b] = shifted(cv, v_b)
        ckb = ck.astype(BF16)
        knb = jnp.concatenate([kn_b, pad], axis=0).astype(BF16)
        d["vc"] = jnp.concatenate([cv.astype(BF16), ones_c], axis=0)
        d["vn"] = jnp.concatenate([jnp.concatenate([v_b, pad], axis=0).astype(BF16), ones_n], axis=1)
        qb = [seq_rows(qj, b) for qj in qn]
        for h in range(KV_HEADS):
            keep = group0 if h == 0 else jnp.logical_not(group0)
            qs = jnp.concatenate([jnp.where(keep, q, 0.0) for q in qb], axis=0).astype(BF16)
            d["s", h] = jnp.concatenate([jnp.dot(qs, ckb, preferred_element_type=F32),
                                         lax.dot_general(qs, knb, nt, preferred_element_type=F32)], axis=1)
    for b in range(bs):
        d = st[b]
        for h in range(KV_HEADS):
            s = jnp.where(mask, d.pop(("s", h)), NEG_INF)
            m = jnp.maximum(jnp.max(s, axis=-1, keepdims=True), sinks[h])
            p = jnp.exp(s - m).astype(BF16)
            o = (lax.dot_general(p[:, :WINDOW], d["vc"], nt, preferred_element_type=F32)
                 + jnp.dot(p[:, WINDOW:], d["vn"], preferred_element_type=F32))
            d["o", h] = o[:, :LANES] / (o[:, LANES:] + jnp.exp(sinks[h] - m))
    for b in range(bs):
        d = st[b]
        o_b = jnp.concatenate([jnp.where(group0, d["o", 0][j * t_new:(j + 1) * t_new],
                                         d["o", 1][j * t_new:(j + 1) * t_new]) for j in range(nq)], axis=1)
        for tt in range(t_new):
            o_seq[tt * bs + b:tt * bs + b + 1, :] = o_b[tt:tt + 1]
    o_ref[...] = o_seq[...].astype(BF16)


def _attn_sample(proj, cache_k, cache_v, tables, gq, gk, sinks, t_new, layer, prev):
    n = proj.shape[0]
    bsz = n // t_new
    bs = SAMPLE_TILE
    rows = bs * t_new
    row = lambda i: (i, 0)
    cache = pl.BlockSpec((None, bs, KV_WIDTH, WINDOW), lambda i: (layer, i, 0, 0))
    prev, prev_specs, aliases = _alias_inputs(prev, 1)
    return pl.pallas_call(
        _skip_refs(functools.partial(_attn_sample_kernel, bs=bs, t_new=t_new, layer=layer), len(prev)),
        grid=(bsz // bs,),
        in_specs=prev_specs + [
            pl.BlockSpec((rows, ATTN_WIDTH), row),
            pl.BlockSpec((rows, 2 * KV_WIDTH), lambda i: (i, EVEN_KV_BLOCK)),
            cache, cache,
            _const_spec((rows, LANES)), _const_spec((rows, LANES)), _const_spec((rows, LANES)),
            _layer_spec((1, LANES), layer), _layer_spec((1, LANES), layer),
            pl.BlockSpec(memory_space=pltpu.SMEM)],
        out_specs=[pl.BlockSpec((rows, ATTN_WIDTH), row), cache, cache],
        out_shape=[jax.ShapeDtypeStruct((n, ATTN_WIDTH), BF16),
                   jax.ShapeDtypeStruct(cache_k.shape, F32), jax.ShapeDtypeStruct(cache_v.shape, F32)],
        input_output_aliases=aliases,
        scratch_shapes=[pltpu.VMEM((rows, ATTN_WIDTH), F32)],
        compiler_params=_cparams("parallel"),
        name="attn_sample",
    )(*prev, proj, proj, cache_k, cache_v, *tables, gq, gk, sinks)


S5_UCHUNKS = S5_WIDTH // LANES
S5_SUB = S5_FLAT // S5_UCHUNKS
S5_SCHUNKS = S5_FLAT // LANES


def _s5_tail(y, wglu_ref, bglu_ref):
    g = 0.5 * y * (1.0 + lax.erf(y * (2.0 ** -0.5)))
    z = jnp.dot(g.astype(BF16), wglu_ref[...], preferred_element_type=F32) + bglu_ref[...]
    return g * _sigmoid(z)


S5_PARTS = 4


def _s5_prompt_kernel(u_ref, wb_ref, wc_ref, lam_ref, d_ref, wglu_ref, bglu_ref,
                      o_ref, sr_ref, si_ref, xs, hst, *, nbatch, tc):
    rows = nbatch * tc
    prow, ptok = rows // S5_PARTS, tc // S5_PARTS

    @pl.when(pl.program_id(1) == 0)
    def _():
        hst[...] = jnp.zeros_like(hst)

    u = jnp.swapaxes(u_ref[...], 0, 1).reshape(rows, S5_WIDTH)
    ub = u.astype(BF16)

    def in_proj(p, cc):
        rs = slice(p * prow, (p + 1) * prow)
        res = jnp.dot(ub[rs, cc * LANES:(cc + 1) * LANES], wb_ref[cc], preferred_element_type=F32)
        for j in range(S5_SUB // LANES):
            xs[cc * 4 + j, rs, :] = res[:, j * LANES:(j + 1) * LANES]
            xs[S5_SCHUNKS + cc * 4 + j, rs, :] = res[:, S5_SUB + j * LANES:S5_SUB + (j + 1) * LANES]

    ys = {}

    def out_proj(p, cc):
        rs = slice(p * prow, (p + 1) * prow)
        s = jnp.concatenate([xs[cc * 4 + j, rs, :] for j in range(4)]
                            + [xs[S5_SCHUNKS + cc * 4 + j, rs, :] for j in range(4)], axis=1).astype(BF16)
        cols = slice(cc * LANES, (cc + 1) * LANES)
        ys[p, cc] = jnp.dot(s, wc_ref[cc], preferred_element_type=F32) + d_ref[:, cols] * u[rs, cols]

    def tail(p):
        out = _s5_tail(jnp.concatenate([ys.pop((p, cc)) for cc in range(S5_UCHUNKS)], axis=1), wglu_ref, bglu_ref)
        o_ref[:, p * ptok:(p + 1) * ptok, :] = jnp.swapaxes(out.reshape(ptok, nbatch, S5_WIDTH), 0, 1).astype(BF16)

    def scan_step(t, h):
        idx = slice(t * nbatch, (t + 1) * nbatch)
        new = list(h)
        for k in range(S5_SCHUNKS):
            hr, hi = h[k], h[S5_SCHUNKS + k]
            lr, li = lam_ref[k], lam_ref[S5_SCHUNKS + k]
            nr = lr * hr - li * hi + xs[k, idx, :]
            ni = lr * hi + li * hr + xs[S5_SCHUNKS + k, idx, :]
            xs[k, idx, :] = nr
            xs[S5_SCHUNKS + k, idx, :] = ni
            new[k], new[S5_SCHUNKS + k] = nr, ni
        return new

    for cc in range(S5_UCHUNKS):
        in_proj(0, cc)
    h = [hst[k] for k in range(2 * S5_SCHUNKS)]
    for p in range(S5_PARTS):
        work = []
        if p + 1 < S5_PARTS:
            work += [functools.partial(in_proj, p + 1, cc) for cc in range(S5_UCHUNKS)]
        if p >= 1:
            work += [functools.partial(out_proj, p - 1, cc) for cc in range(S5_UCHUNKS)]
            work.append(functools.partial(tail, p - 1))
        every = max(1, ptok // max(1, len(work)))
        for i in range(ptok):
            h = scan_step(p * ptok + i, h)
            if work and (i + 1) % every == 0:
                work.pop(0)()
        for w in work:
            w()
    for cc in range(S5_UCHUNKS):
        out_proj(S5_PARTS - 1, cc)
    tail(S5_PARTS - 1)
    for k in range(2 * S5_SCHUNKS):
        hst[k] = h[k]
    sr_ref[...] = jnp.concatenate(h[:S5_SCHUNKS], axis=1)
    si_ref[...] = jnp.concatenate(h[S5_SCHUNKS:], axis=1)


def _s5_prompt(proj, prm, layer, prev):
    bsz, seq, _ = proj.shape
    nbatch, tc = SUBLANES, S5_CHUNK
    n_layers = prm["wb"].shape[0]
    st = pl.BlockSpec((None, nbatch, S5_FLAT), lambda b, c: (layer, b, 0))
    st_shape = jax.ShapeDtypeStruct((n_layers, bsz, S5_FLAT), F32)
    prev, prev_specs, aliases = _alias_inputs(prev, 1)
    names = ("wb", "wc", "lam8", "d", "wglu", "bglu")
    return pl.pallas_call(
        _skip_refs(functools.partial(_s5_prompt_kernel, nbatch=nbatch, tc=tc), len(prev)),
        grid=(bsz // nbatch, seq // tc),
        in_specs=prev_specs + [pl.BlockSpec((nbatch, tc, S5_WIDTH), lambda b, c: (b, c, EVEN_U_BLOCK))]
        + [_layer_spec(prm[k].shape[1:], layer) for k in names],
        out_specs=[pl.BlockSpec((nbatch, tc, S5_WIDTH), lambda b, c: (b, c, 0)), st, st],
        out_shape=[jax.ShapeDtypeStruct((bsz, seq, S5_WIDTH), BF16), st_shape, st_shape],
        input_output_aliases=aliases,
        scratch_shapes=[pltpu.VMEM((2 * S5_SCHUNKS, nbatch * tc, LANES), F32),
                        pltpu.VMEM((2 * S5_SCHUNKS, nbatch, LANES), F32)],
        compiler_params=_cparams("parallel", "arbitrary"),
        name="s5_prompt",
    )(*prev, proj, *[prm[k] for k in names])


def _s5_sample_kernel(u_ref, wb_ref, wc_ref, lr_ref, li_ref, d_ref, wglu_ref, bglu_ref, h0r_ref, h0i_ref,
                      o_ref, sr_ref, si_ref, xr, xi, *, nseq, t_new):
    nt, st = nseq // SAMPLE_TILE, SAMPLE_TILE
    n = nseq * t_new
    u = u_ref[...]
    ub = u.astype(BF16)
    for cc in range(S5_UCHUNKS):
        res = jnp.dot(ub[:, cc * LANES:(cc + 1) * LANES], wb_ref[cc], preferred_element_type=F32)
        sc = slice(cc * S5_SUB, (cc + 1) * S5_SUB)
        xr[:, :, :, sc] = res[:, :S5_SUB].reshape(nt, t_new, st, S5_SUB)
        xi[:, :, :, sc] = res[:, S5_SUB:].reshape(nt, t_new, st, S5_SUB)
    lr, li = lr_ref[...], li_ref[...]
    hr, hi = h0r_ref[...], h0i_ref[...]
    for t in range(t_new):
        nr = lr * hr - li * hi + xr[:, t].reshape(nseq, S5_FLAT)
        ni = lr * hi + li * hr + xi[:, t].reshape(nseq, S5_FLAT)
        xr[:, t] = nr.reshape(nt, st, S5_FLAT)
        xi[:, t] = ni.reshape(nt, st, S5_FLAT)
        hr, hi = nr, ni
    sr_ref[...] = hr
    si_ref[...] = hi
    ys = []
    for cc in range(S5_UCHUNKS):
        sc = slice(cc * S5_SUB, (cc + 1) * S5_SUB)
        s = jnp.concatenate([xr[:, :, :, sc].reshape(n, S5_SUB), xi[:, :, :, sc].reshape(n, S5_SUB)],
                            axis=1).astype(BF16)
        cols = slice(cc * LANES, (cc + 1) * LANES)
        ys.append(jnp.dot(s, wc_ref[cc], preferred_element_type=F32) + d_ref[:, cols] * u[:, cols])
    o_ref[...] = _s5_tail(jnp.concatenate(ys, axis=1), wglu_ref, bglu_ref).astype(BF16)


def _s5_sample(proj, h0r, h0i, prm, t_new, layer, prev):
    n = proj.shape[0]
    nseq = n // t_new
    names = ("wb", "wc", "lr", "li", "d", "wglu", "bglu")
    st = pl.BlockSpec((None, nseq, S5_FLAT), lambda i: (layer, 0, 0))
    prev, prev_specs, aliases = _alias_inputs(prev, 1)
    scratch = pltpu.VMEM((nseq // SAMPLE_TILE, t_new, SAMPLE_TILE, S5_FLAT), F32)
    return pl.pallas_call(
        _skip_refs(functools.partial(_s5_sample_kernel, nseq=nseq, t_new=t_new), len(prev)),
        grid=(1,),
        in_specs=prev_specs + [pl.BlockSpec((n, S5_WIDTH), lambda i: (0, EVEN_U_BLOCK))]
        + [_layer_spec(prm[k].shape[1:], layer) for k in names]
        + [_layer_spec((nseq, S5_FLAT), layer), _layer_spec((nseq, S5_FLAT), layer)],
        out_specs=[pl.BlockSpec((n, S5_WIDTH), lambda i: (0, 0)), st, st],
        out_shape=[jax.ShapeDtypeStruct((n, S5_WIDTH), BF16),
                   jax.ShapeDtypeStruct(h0r.shape, F32), jax.ShapeDtypeStruct(h0i.shape, F32)],
        input_output_aliases=aliases,
        scratch_shapes=[scratch, scratch],
        compiler_params=_cparams("arbitrary"),
        name="s5_sample",
    )(*prev, proj, *[prm[k] for k in names], h0r, h0i)


def _s5_params(a_re, a_im, log_dt, b_re, b_im, c_re, c_im, d_skip, w_glu, b_glu):
    nl = a_re.shape[0]
    dt = jnp.exp(log_dt)
    mag = jnp.exp(a_re * dt)
    lr, li = mag * jnp.cos(a_im * dt), mag * jnp.sin(a_im * dt)
    den = a_re * a_re + a_im * a_im
    cr = ((lr - 1.0) * a_re + li * a_im) / den
    ci = (li * a_re - (lr - 1.0) * a_im) / den
    bbr = cr[..., None] * b_re - ci[..., None] * b_im
    bbi = cr[..., None] * b_im + ci[..., None] * b_re
    gpc = LANES // S5_GROUP
    eye = jnp.eye(gpc, dtype=F32)

    def in_blocks(bb):
        bb = bb.reshape(nl, S5_UCHUNKS, gpc, S5_STATE, S5_GROUP)
        return jnp.einsum("lcgph,gk->lcghkp", bb, eye).reshape(nl, S5_UCHUNKS, LANES, S5_SUB)

    def out_blocks(cm):
        cm = cm.reshape(nl, S5_UCHUNKS, gpc, S5_GROUP, S5_STATE)
        return jnp.einsum("lcghp,gk->lcgpkh", cm, eye).reshape(nl, S5_UCHUNKS, S5_SUB, LANES)

    wb = jnp.concatenate([in_blocks(bbr), in_blocks(bbi)], axis=3).astype(BF16)
    wc = jnp.concatenate([out_blocks(c_re), -out_blocks(c_im)], axis=2).astype(BF16)
    lr_f, li_f = lr.reshape(nl, 1, S5_FLAT), li.reshape(nl, 1, S5_FLAT)
    lam = jnp.concatenate([lr_f.reshape(nl, S5_SCHUNKS, 1, LANES), li_f.reshape(nl, S5_SCHUNKS, 1, LANES)], axis=1)
    lam8 = jnp.broadcast_to(lam, (nl, 2 * S5_SCHUNKS, SUBLANES, LANES))
    return dict(wb=wb, wc=wc, lam8=lam8, lr=lr_f, li=li_f, d=d_skip.reshape(nl, 1, S5_WIDTH),
                wglu=w_glu.astype(BF16), bglu=b_glu.reshape(nl, 1, S5_WIDTH))


ML_AUG = 2 * ML_DV


EXP_CLAMP = 88.0


def _den_floor(m_row):
    return jnp.exp(jnp.minimum(-m_row, EXP_CLAMP))


def _head_out(h, o, gout):
    hn = h * lax.rsqrt(jnp.mean(h * h, axis=-1, keepdims=True) + EPS) * gout
    return (hn * _sigmoid(o)).astype(BF16)


ODDP_V_BLOCK = 0
ODDP_O_BLOCK = 1
ODDP_Q_BLOCK = (2 * ML_WIDTH) // ML_QK
ODDP_G_BLOCK = (2 * ML_WIDTH + ML_QK) // LANES
ML_SPLIT = 3
ML_PIECE_LANES = 2 * ML_HEADS
ML_SEQS = 8
ML_STAGE_LAG = 2


def _norm_matmul_kt_kernel(x_ref, g_ref, wt_ref, wg_ref, o_ref, kt_ref):
    h = _rms(x_ref[...], g_ref[...]).astype(BF16)
    k0, v0, g0 = ML_QK, 2 * ML_QK, 2 * ML_QK + 2 * ML_WIDTH
    nt = (((1,), (1,)), ((), ()))
    o_ref[:, :g0 - v0] = lax.dot_general(h, wt_ref[v0:g0, :].astype(BF16), nt, preferred_element_type=F32)
    o_ref[:, g0 - v0:g0 - v0 + k0] = lax.dot_general(h, wt_ref[:k0, :].astype(BF16), nt, preferred_element_type=F32)
    o_ref[:, g0 - v0 + k0:] = jnp.dot(h, wg_ref[...], preferred_element_type=F32)
    kt = lax.dot_general(wt_ref[k0:v0, :].astype(BF16), h, nt, preferred_element_type=F32)
    kt_ref[...] = kt * (ML_DK ** -0.5)


def _norm_matmul_kt(xs, g, layer, wt, wg, widx):
    d = wt.shape[2]
    m_out = 2 * ML_WIDTH + ML_QK + wg.shape[2]
    return _ring_proj_call(
        _norm_matmul_kt_kernel, xs[0], xs[1], [g, wt, wg],
        [_layer_spec((1, d), layer), _layer_spec(wt.shape[1:], widx), _layer_spec(wg.shape[1:], widx)],
        [(m_out, F32, True), (ML_QK, F32, False)], "norm_matmul_kt")


def _cummax_rows(x):
    n = x.shape[0]
    row = lax.broadcasted_iota(jnp.int32, x.shape, 0)
    shift = 1
    while shift < n:
        x = jnp.maximum(x, jnp.where(row >= shift, pltpu.roll(x, shift, 0), NEG_INF))
        shift *= 2
    return x


def _pieces(x):
    lane = lax.broadcasted_iota(jnp.int32, x.shape, 1)
    xx = x + pltpu.roll(x, ML_PIECE_LANES, 1) + pltpu.roll(x, 2 * ML_PIECE_LANES, 1)
    a1, a2, a3 = _split3(xx)
    return jnp.where(lane < ML_PIECE_LANES, a1, jnp.where(lane < 2 * ML_PIECE_LANES, a2, a3))


def _ml_select_constants():
    mask = np.zeros((ML_HEADS, LANES), np.float32)
    sel = np.zeros((ML_HEADS, LANES, 2 * ML_DV), np.float32)
    for h in range(ML_HEADS):
        for k in range(ML_SPLIT):
            lo, hi = k * ML_PIECE_LANES + h, k * ML_PIECE_LANES + ML_HEADS + h
            mask[h, lo] = mask[h, hi] = 1.0
            sel[h, lo, :ML_DV] = 1.0
            sel[h, hi, ML_DV:] = 1.0
    return jnp.asarray(mask), jnp.asarray(sel, dtype=BF16)


def _mlstm_prompt_kernel(*refs, tc, nchunks, nseq):
    v_ref, o_ref, q_ref, g_ref = refs[:4]
    kt_refs = refs[4:4 + nseq]
    bias_ref, gout_ref, mask_ref, sel_ref, h_ref, c_ref, n_ref, m_ref, caug, mst = refs[4 + nseq:]
    ci = pl.program_id(1)

    @pl.when(ci == 0)
    def _():
        caug[...] = jnp.zeros_like(caug)
        mst[...] = jnp.zeros_like(mst)

    nh = ML_HEADS
    lane = lax.broadcasted_iota(jnp.int32, (tc, LANES), 1)
    lo, hi = lane < nh, (lane >= nh) & (lane < 2 * nh)
    rt = lax.broadcasted_iota(jnp.int32, (tc, tc), 0)
    cs = lax.broadcasted_iota(jnp.int32, (tc, tc), 1)
    causal = cs <= rt
    tril = jnp.where(causal, 1.0, 0.0).astype(BF16)
    ones = jnp.ones((tc, ML_DV), F32)

    def gates(sq):
        g = g_ref[sq] + bias_ref[...]
        lf = jnp.where(hi, _log_sigmoid(g), 0.0)
        b = sum(jnp.dot(tril, p, preferred_element_type=F32) for p in _split3(lf))
        c = jnp.where(hi, pltpu.roll(g, nh, 1) - b, 0.0)
        m_prev = mst[sq]
        mx = jnp.maximum(_cummax_rows(c), m_prev)
        m_row = b + mx
        mx_lo = pltpu.roll(mx, LANES - nh, 1)
        w_inter = jnp.exp(pltpu.roll(m_prev, LANES - nh, 1) - mx_lo)
        mst[sq] = m_row[tc - 1:tc, :]
        return dict(xc=_pieces(jnp.where(lo, w_inter, jnp.where(hi, _den_floor(m_row), 0.0))),
                    lc=_pieces(jnp.where(lo, -mx_lo, jnp.where(hi, 1.0, 0.0))),
                    rc=_pieces(jnp.where(lo, 1.0, jnp.where(hi, c, 0.0))))

    gt = [gates(sq) for sq in range(nseq)]
    units = [(sq, hd) for hd in range(nh) for sq in range(nseq)]
    st = [dict() for _ in units]

    def stage1(u):
        sq, hd = units[u]
        d = st[u]
        rh = gt[sq]["rc"] * mask_ref[hd:hd + 1, :].astype(BF16)
        d["dmat"] = lax.dot_general(gt[sq]["lc"], rh, (((1,), (1,)), ((), ())), preferred_element_type=F32)
        d["wb"] = jnp.dot(gt[sq]["xc"], sel_ref[hd], preferred_element_type=F32)
        d["qh"] = q_ref[sq, :, hd * ML_DK:(hd + 1) * ML_DK]
        d["kt"] = kt_refs[sq][hd * ML_DK:(hd + 1) * ML_DK, :]
        d["qk"] = jnp.dot(d["qh"].astype(BF16), d["kt"].astype(BF16), preferred_element_type=F32)

    def stage2(u):
        sq, hd = units[u]
        d = st[u]
        cols = slice(hd * ML_DV, (hd + 1) * ML_DV)
        d["w"] = jnp.exp(jnp.where(causal, d["dmat"], NEG_INF))
        d["vaug"] = jnp.concatenate([v_ref[sq, :, cols], ones], axis=1).astype(BF16)
        d["cm"] = caug[sq, hd]
        lhs = jnp.concatenate([(d["qk"] * d["w"]).astype(BF16), (d["wb"][:, :ML_DK] * d["qh"]).astype(BF16)], axis=1)
        rhs = jnp.concatenate([d["vaug"], d["cm"].astype(BF16)], axis=0)
        d["both"] = jnp.dot(lhs, rhs, preferred_element_type=F32)
        kw = (d["kt"] * d["w"][tc - 1:tc, :]).astype(BF16)
        d["upd"] = jnp.dot(kw, d["vaug"], preferred_element_type=F32)

    def stage3(u):
        sq, hd = units[u]
        d = st[u]
        cols = slice(hd * ML_DV, (hd + 1) * ML_DV)
        both, wb = d["both"], d["wb"]
        h = both[:, :ML_DV] / jnp.maximum(jnp.abs(both[:, ML_DV:]), wb[:, ML_DV:])
        h_ref[sq, :, cols] = _head_out(h, o_ref[sq, :, cols], gout_ref[:, cols])
        decay = wb[tc - 1:tc, :ML_DV]
        caug[sq, hd] = jnp.concatenate([decay, decay], axis=1) * d["cm"] + d["upd"]
        d.clear()

    for step in range(len(units) + 2 * ML_STAGE_LAG):
        if step < len(units):
            stage1(step)
        if 0 <= step - ML_STAGE_LAG < len(units):
            stage2(step - ML_STAGE_LAG)
        if 0 <= step - 2 * ML_STAGE_LAG < len(units):
            stage3(step - 2 * ML_STAGE_LAG)

    @pl.when(ci == nchunks - 1)
    def _():
        c_ref[...] = caug[:, :, :, :ML_DV]
        n_ref[...] = caug[:, :, :, ML_DV:]
        m_ref[...] = mst[...]


def _kt_index(b, c, *, sq, nseq, nchunks):
    return 0, (b * nseq + sq) * nchunks + c


def _mlstm_prompt(proj, kt, bias, gout, consts, layer, prev):
    bsz, seq, _ = proj.shape
    tc, nseq = ML_CHUNK, ML_SEQS
    nchunks = seq // tc
    n_layers = bias.shape[0]
    mask, sel = consts
    blk = lambda w, j: pl.BlockSpec((nseq, tc, w), lambda b, c: (b, c, j))
    st = lambda shape: pl.BlockSpec((None, nseq) + shape, lambda b, c: (layer, b) + (0,) * len(shape))
    st_shape = lambda shape: jax.ShapeDtypeStruct((n_layers, bsz) + shape, F32)
    prev, prev_specs, aliases = _alias_inputs(prev, 1)
    return pl.pallas_call(
        _skip_refs(functools.partial(_mlstm_prompt_kernel, tc=tc, nchunks=nchunks, nseq=nseq), len(prev)),
        grid=(bsz // nseq, nchunks),
        in_specs=prev_specs + [
            blk(ML_WIDTH, ODDP_V_BLOCK), blk(ML_WIDTH, ODDP_O_BLOCK), blk(ML_QK, ODDP_Q_BLOCK),
            blk(LANES, ODDP_G_BLOCK)]
        + [pl.BlockSpec((ML_QK, tc), functools.partial(_kt_index, sq=sq, nseq=nseq, nchunks=nchunks))
           for sq in range(nseq)] + [
            _layer_spec((1, LANES), layer), _layer_spec((1, ML_WIDTH), layer),
            _const_spec(mask.shape), _const_spec(sel.shape)],
        out_specs=[blk(ML_WIDTH, 0), st((ML_HEADS, ML_DK, ML_DV)), st((ML_HEADS, ML_DK, ML_DV)), st((1, LANES))],
        out_shape=[jax.ShapeDtypeStruct((bsz, seq, ML_WIDTH), BF16),
                   st_shape((ML_HEADS, ML_DK, ML_DV)), st_shape((ML_HEADS, ML_DK, ML_DV)), st_shape((1, LANES))],
        input_output_aliases=aliases,
        scratch_shapes=[pltpu.VMEM((nseq, ML_HEADS, ML_DK, ML_AUG), F32), pltpu.VMEM((nseq, 1, LANES), F32)],
        compiler_params=_cparams("parallel", "arbitrary"),
        name="mlstm_prompt",
    )(*prev, proj, proj, proj, proj, *([kt] * nseq), bias, gout, mask, sel)


MLS_SEQS = 32


def _mlstm_sample_kernel(v_ref, o_ref, q_ref, g_ref, kt_ref, bias_ref, gout_ref, mask_ref, sel_ref,
                         c0_ref, n0_ref, m0_ref, h_ref, c_ref, n_ref, m_ref, *, nseq, t_new):
    nh, nt, st = ML_HEADS, nseq // SAMPLE_TILE, SAMPLE_TILE
    rows = nseq * t_new
    lane = lax.broadcasted_iota(jnp.int32, (rows, LANES), 1)
    lo, hi = lane < nh, (lane >= nh) & (lane < 2 * nh)
    tiles = lambda a: a.reshape(nt, t_new, st, a.shape[-1])
    flat = lambda a: a.reshape(rows, a.shape[-1])
    per_seq = lambda a: a.reshape(nseq, a.shape[-1])

    g = g_ref[...] + bias_ref[...]
    lf = tiles(jnp.where(hi, _log_sigmoid(g), 0.0))
    ig = tiles(jnp.where(hi, pltpu.roll(g, nh, 1), 0.0))
    m_prev = m0_ref[...].reshape(nt, st, LANES)
    bs, cs, ms = [], [], []
    b_run, m_run = None, m_prev
    for t in range(t_new):
        b_run = lf[:, t] if b_run is None else b_run + lf[:, t]
        c_t = ig[:, t] - b_run
        m_run = jnp.maximum(m_run, c_t)
        bs.append(b_run)
        cs.append(c_t)
        ms.append(m_run)
    stack = lambda xs: flat(jnp.stack(xs, axis=1))
    b, c, mx = stack(bs), stack(cs), stack(ms)
    m_prev_rows = stack([m_prev] * t_new)
    m_row = b + mx
    m_ref[...] = per_seq(bs[-1] + ms[-1])
    mx_lo = pltpu.roll(mx, LANES - nh, 1)
    w_inter = jnp.exp(pltpu.roll(m_prev_rows, LANES - nh, 1) - mx_lo)
    xc = _pieces(jnp.where(lo, w_inter, jnp.where(hi, _den_floor(m_row), 0.0)))
    lc = _pieces(jnp.where(lo, -mx_lo, jnp.where(hi, 1.0, 0.0)))
    rc = _pieces(jnp.where(lo, 1.0, jnp.where(hi, c, 0.0)))

    def seq_of(idx):
        return (idx // (t_new * st)) * st + idx % st, (idx % (t_new * st)) // st

    rt = lax.broadcasted_iota(jnp.int32, (rows, rows), 0)
    ct = lax.broadcasted_iota(jnp.int32, (rows, rows), 1)
    (rs, rtok), (cseq, ctok) = seq_of(rt), seq_of(ct)
    valid = (rs == cseq) & (ctok <= rtok)
    rq = lax.broadcasted_iota(jnp.int32, (rows, nseq * ML_DK), 0)
    cq = lax.broadcasted_iota(jnp.int32, (rows, nseq * ML_DK), 1)
    own_q = seq_of(rq)[0] == cq // ML_DK
    rk = lax.broadcasted_iota(jnp.int32, (nseq * ML_DK, rows), 0)
    ck = lax.broadcasted_iota(jnp.int32, (nseq * ML_DK, rows), 1)
    own_k = rk // ML_DK == seq_of(ck)[0]
    ones = jnp.ones((rows, ML_DV), F32)
    last = lambda a: per_seq(tiles(a)[:, t_new - 1])

    for hd in range(nh):
        cols = slice(hd * ML_DV, (hd + 1) * ML_DV)
        rh = rc * mask_ref[hd:hd + 1, :].astype(BF16)
        dmat = lax.dot_general(lc, rh, (((1,), (1,)), ((), ())), preferred_element_type=F32)
        wb = jnp.dot(xc, sel_ref[hd], preferred_element_type=F32)
        qh = q_ref[:, hd * ML_DK:(hd + 1) * ML_DK]
        kt = kt_ref[hd * ML_DK:(hd + 1) * ML_DK, :]
        ktb = kt.astype(BF16)
        w = jnp.exp(jnp.where(valid, dmat, NEG_INF))
        qk = jnp.dot(qh.astype(BF16), ktb, preferred_element_type=F32) * w
        vaug = jnp.concatenate([v_ref[:, cols], ones], axis=1).astype(BF16)
        po = jnp.dot(qk.astype(BF16), vaug, preferred_element_type=F32)
        wq = wb[:, :ML_DK] * qh
        wq2 = jnp.concatenate([wq, wq], axis=1)
        wq_bd = jnp.where(own_q, jnp.concatenate([wq2] * (nseq * ML_DK // LANES), axis=1), 0.0).astype(BF16)
        cstack = c0_ref[:, hd].reshape(nseq * ML_DK, ML_DV)
        num = po[:, :ML_DV] + jnp.dot(wq_bd, cstack.astype(BF16), preferred_element_type=F32)
        n0 = n0_ref[hd]
        n_rows = stack([n0.reshape(nt, st, ML_DK)] * t_new)
        den = po[:, ML_DV:] + jnp.sum(wq * n_rows, axis=-1, keepdims=True)
        h = num / jnp.maximum(jnp.abs(den), wb[:, ML_DV:])
        h_ref[:, cols] = _head_out(h, o_ref[:, cols], gout_ref[:, cols])
        w_last = last(w)
        decay = last(wb[:, :ML_DV])
        n_upd = lax.dot_general(w_last.astype(BF16), ktb, (((1,), (1,)), ((), ())), preferred_element_type=F32)
        n_ref[hd] = decay[:, :ML_DK] * n0 + n_upd
        wk = jnp.sum(w_last, axis=0, keepdims=True)
        kw_bd = jnp.where(own_k, jnp.concatenate([kt * wk] * nseq, axis=0), 0.0).astype(BF16)
        upd = jnp.dot(kw_bd, v_ref[:, cols].astype(BF16), preferred_element_type=F32)
        decay_rows = jnp.broadcast_to(decay[:, None, :], (nseq, ML_DK, ML_DV)).reshape(nseq * ML_DK, ML_DV)
        c_ref[:, hd] = (decay_rows * cstack + upd).reshape(nseq, ML_DK, ML_DV)


def _mlstm_sample(proj, kt, bias, gout, consts, c0, n0h, m0, t_new, layer, prev):
    n = proj.shape[0]
    bsz = n // t_new
    nseq = MLS_SEQS
    rows = nseq * t_new
    mask, sel = consts
    blk = lambda w, j: pl.BlockSpec((rows, w), lambda i: (i, j))
    cst = pl.BlockSpec((None, nseq, ML_HEADS, ML_DK, ML_DV), lambda i: (layer, i, 0, 0, 0))
    nst = pl.BlockSpec((None, ML_HEADS, nseq, ML_DK), lambda i: (layer, 0, i, 0))
    mst = pl.BlockSpec((None, nseq, LANES), lambda i: (layer, i, 0))
    prev, prev_specs, aliases = _alias_inputs(prev, 1)
    return pl.pallas_call(
        _skip_refs(functools.partial(_mlstm_sample_kernel, nseq=nseq, t_new=t_new), len(prev)),
        grid=(bsz // nseq,),
        in_specs=prev_specs + [
            blk(ML_WIDTH, ODDP_V_BLOCK), blk(ML_WIDTH, ODDP_O_BLOCK), blk(ML_QK, ODDP_Q_BLOCK),
            blk(LANES, ODDP_G_BLOCK), pl.BlockSpec((ML_QK, rows), lambda i: (0, i)),
            _layer_spec((1, LANES), layer), _layer_spec((1, ML_WIDTH), layer),
            _const_spec(mask.shape), _const_spec(sel.shape), cst, nst, mst],
        out_specs=[blk(ML_WIDTH, 0), cst, nst, mst],
        out_shape=[jax.ShapeDtypeStruct((n, ML_WIDTH), BF16),
                   jax.ShapeDtypeStruct(c0.shape, F32), jax.ShapeDtypeStruct(n0h.shape, F32),
                   jax.ShapeDtypeStruct(m0.shape, F32)],
        input_output_aliases=aliases,
        compiler_params=_cparams("parallel"),
        name="mlstm_sample",
    )(*prev, proj, proj, proj, proj, kt, bias, gout, mask, sel, c0, n0h, m0)


def _pad_lanes(x):
    return jnp.pad(x, [(0, 0)] * (x.ndim - 1) + [(0, LANES - x.shape[-1])])


def kernel(x_prompt, x_sample, cache_k, cache_v, state_ssm_re, state_ssm_im, state_mlstm_c, state_mlstm_n, state_mlstm_m, norm_mix, norm_ffn, w_in_even, q_norm, k_norm, attn_sinks, s5_a_re, s5_a_im, s5_log_dt, s5_b_re, s5_b_im, s5_c_re, s5_c_im, s5_d, s5_w_glu, s5_b_glu, w_out_even, w_in_odd, ml_b_i, ml_b_f, ml_out_norm, w_out_odd, w_gate, w_up, w_down):
    bp, lp, _ = x_prompt.shape
    bsm, ls, _ = x_sample.shape
    yp = x_prompt.reshape(bp * lp, D_MODEL)
    ys = x_sample.reshape(bsm // SAMPLE_TILE, SAMPLE_TILE, ls, D_MODEL).transpose(0, 2, 1, 3).reshape(bsm * ls, D_MODEL)
    tab_p = _rope_tables(jnp.arange(lp))
    tab_s = tuple(jnp.repeat(t, SAMPLE_TILE, axis=0) for t in _rope_tables(PAST_LEN + jnp.arange(ls)))
    n_even, n_odd = w_in_even.shape[0], w_in_odd.shape[0]

    g_mix = norm_mix.reshape(DEPTH, 1, D_MODEL)
    g_ffn = norm_ffn.reshape(DEPTH, 1, D_MODEL)
    ffn_w = (w_gate[:1].astype(BF16), w_up[:1].astype(BF16), w_down[:1].astype(BF16))
    kv0, u0 = ATTN_WIDTH, ATTN_WIDTH + 2 * KV_WIDTH
    order = jnp.asarray(ATTN_HEAD_ORDER)
    wq = w_in_even[..., :kv0].reshape(n_even, D_MODEL, ATTN_HEADS, HEAD_DIM)[:, :, order].reshape(n_even, D_MODEL, kv0)
    w_in_e = jnp.concatenate([wq, w_in_even[..., u0:], w_in_even[..., kv0:u0]], axis=-1).astype(BF16)
    wo_attn = w_out_even[:, :kv0].reshape(n_even, ATTN_HEADS, HEAD_DIM, D_MODEL)[:, order].reshape(n_even, kv0, D_MODEL)
    w_out_e = jnp.concatenate([wo_attn, w_out_even[:, kv0:]], axis=1).astype(BF16)
    gq = jnp.tile(q_norm, (1, LANES // HEAD_DIM)).reshape(n_even, 1, LANES)
    gk = jnp.tile(k_norm, (1, LANES // HEAD_DIM)).reshape(n_even, 1, LANES)
    prm = _s5_params(s5_a_re, s5_a_im, s5_log_dt, s5_b_re, s5_b_im, s5_c_re, s5_c_im, s5_d, s5_w_glu, s5_b_glu)
    w_gates_o = _pad_lanes(w_in_odd[..., 2 * ML_QK + 2 * ML_WIDTH:]).astype(BF16)
    w_in_ot = jnp.swapaxes(w_in_odd, 1, 2)
    ml_consts = _ml_select_constants()
    w_out_o = w_out_odd.astype(BF16)
    ml_bias = _pad_lanes(jnp.concatenate([ml_b_i, ml_b_f], axis=-1)).reshape(n_odd, 1, LANES)
    ml_gout = ml_out_norm.reshape(n_odd, 1, ML_WIDTH)
    keys_last = lambda a: a.transpose(0, 1, 3, 4, 2).reshape(n_even, bsm, KV_WIDTH, WINDOW)
    ck, cv = keys_last(cache_k), keys_last(cache_v)
    h0r = state_ssm_re.reshape(n_even, bsm, S5_FLAT)
    h0i = state_ssm_im.reshape(n_even, bsm, S5_FLAT)
    n0h = jnp.swapaxes(state_mlstm_n, 1, 2)
    m0 = jnp.pad(state_mlstm_m, ((0, 0), (0, 0), (ML_HEADS, LANES - 2 * ML_HEADS)))

    p_attn = p_ssm = p_ml = s_attn = s_ssm = s_ml = None
    for layer in range(DEPTH):
        ffn = (layer, g_ffn) + ffn_w
        cast_next = (w_gate, w_up, w_down, layer + 1) if layer + 1 < DEPTH else None
        if layer % 2 == 0:
            e = layer // 2
            proj_p, proj_s = _norm_matmul([yp, ys], g_mix, layer, w_in_e, e)
            proj3 = proj_p.reshape(bp, lp, -1)
            attn_p, *p_attn = _attn_prompt(proj3, tab_p, gq, gk, attn_sinks, e, p_attn)
            ssm_p, *p_ssm = _s5_prompt(proj3, prm, e, p_ssm)
            attn_s, *s_attn = _attn_sample(proj_s, ck, cv, tab_s, gq, gk, attn_sinks, ls, e, s_attn)
            ssm_s, *s_ssm = _s5_sample(proj_s, h0r, h0i, prm, ls, e, s_ssm)
            yp, ys, ffn_w = _mix_ffn([[yp, attn_p.reshape(bp * lp, -1), ssm_p.reshape(bp * lp, -1)],
                                      [ys, attn_s, ssm_s]], w_out_e, e, *ffn, cast_next=cast_next)
        else:
            o = layer // 2
            (proj_p, kt_p), (proj_s, kt_s) = _norm_matmul_kt([yp, ys], g_mix, layer, w_in_ot, w_gates_o, o)
            hh_p, *p_ml = _mlstm_prompt(proj_p.reshape(bp, lp, -1), kt_p, ml_bias, ml_gout, ml_consts, o, p_ml)
            hh_s, *s_ml = _mlstm_sample(proj_s, kt_s, ml_bias, ml_gout, ml_consts, state_mlstm_c, n0h, m0, ls, o,
                                        s_ml)
            yp, ys, ffn_w = _mix_ffn([[yp, hh_p.reshape(bp * lp, -1)], [ys, hh_s]], w_out_o, o, *ffn,
                                     cast_next=cast_next)
    heads = lambda a: a.reshape(a.shape[:3] + (KV_HEADS, HEAD_DIM))
    groups = lambda a: a.reshape(a.shape[:2] + (S5_GROUPS, S5_STATE))
    keys_first = lambda a: a.reshape(a.shape[:2] + (KV_HEADS, HEAD_DIM, WINDOW)).transpose(0, 1, 4, 2, 3)
    ys = ys.reshape(bsm // SAMPLE_TILE, ls, SAMPLE_TILE, D_MODEL).transpose(0, 2, 1, 3).reshape(bsm, ls, D_MODEL)
    return (yp.reshape(bp, lp, D_MODEL), ys,
            heads(p_attn[0]), heads(p_attn[1]), groups(p_ssm[0]), groups(p_ssm[1]),
            p_ml[0], p_ml[1][..., 0], p_ml[2][:, :, 0, ML_HEADS:2 * ML_HEADS],
            keys_first(s_attn[0]), keys_first(s_attn[1]), groups(s_ssm[0]), groups(s_ssm[1]),
            s_ml[0], jnp.swapaxes(s_ml[1], 1, 2), s_ml[2][..., ML_HEADS:2 * ML_HEADS])
```

```python
import functools

import numpy as np

import jax
import jax.numpy as jnp
from jax import lax
from jax.experimental import pallas as pl
from jax.experimental.pallas import tpu as pltpu

F32 = jnp.float32
BF16 = jnp.bfloat16

D_MODEL = 1024
DEPTH = 4
PAST_LEN = 8192
WINDOW = 128
ATTN_HEADS = 8
KV_HEADS = 2
HEAD_DIM = 64
ATTN_WIDTH = ATTN_HEADS * HEAD_DIM
KV_WIDTH = KV_HEADS * HEAD_DIM
ROT_DIM = HEAD_DIM // 4
ROPE_THETA = 500000.0
S5_GROUP = 16
S5_WIDTH = D_MODEL // 2
S5_GROUPS = S5_WIDTH // S5_GROUP
S5_STATE = 64
S5_FLAT = S5_GROUPS * S5_STATE
ML_HEADS = 8
ML_DV = D_MODEL // ML_HEADS
ML_DK = ML_DV // 2
ML_QK = ML_HEADS * ML_DK
ML_WIDTH = ML_HEADS * ML_DV
D_FF = 2816
EPS = 1e-6

LANES = 128
SUBLANES = 8
ROW_TILE = 512
FF_TILE = 256
S5_CHUNK = 128
ML_CHUNK = 128
SAMPLE_TILE = SUBLANES
VMEM_LIMIT = 56 * 1024 * 1024

NEG_INF = float("-inf")


def _cparams(*sem):
    return pltpu.CompilerParams(dimension_semantics=sem, vmem_limit_bytes=VMEM_LIMIT)


def _const_spec(shape):
    zeros = (0,) * len(shape)
    return pl.BlockSpec(shape, lambda *_: zeros, pipeline_mode=pl.Buffered(1))


def _layer_spec(shape, layer):
    zeros = (0,) * len(shape)
    return pl.BlockSpec((None,) + tuple(shape), lambda *_: (layer,) + zeros, pipeline_mode=pl.Buffered(1))


def _skip_refs(body, n_skip):
    if n_skip == 0:
        return body

    def wrapped(*refs):
        return body(*refs[n_skip:])

    return wrapped


def _alias_inputs(prev, first_state_out):
    prev = () if prev is None else tuple(prev)
    specs = [pl.BlockSpec(memory_space=pl.ANY) for _ in prev]
    aliases = {i: first_state_out + i for i in range(len(prev))}
    return prev, specs, aliases


def _rms(x, g):
    ms = jnp.mean(x * x, axis=-1, keepdims=True)
    return x * lax.rsqrt(ms + EPS) * g


def _split3(a):
    a1 = a.astype(BF16)
    r1 = a - a1.astype(F32)
    a2 = r1.astype(BF16)
    a3 = (r1 - a2.astype(F32)).astype(BF16)
    return a1, a2, a3


def _log_sigmoid(x):
    return jnp.minimum(x, 0.0) - jnp.log(1.0 + jnp.exp(-jnp.abs(x)))


def _sigmoid(x):
    return 1.0 / (1.0 + jnp.exp(-x))


def _norm_matmul_kernel(x_ref, g_ref, w_ref, o_ref):
    h = _rms(x_ref[...], g_ref[...]).astype(BF16)
    o_ref[...] = jnp.dot(h, w_ref[...], preferred_element_type=F32)


def _row_groups_call(body, groups, consts, const_specs, out_defs, scratch_shapes, name):
    steps, tiles = [], []
    for arrays in groups:
        n = arrays[0].shape[0]
        tm = min(ROW_TILE, n)
        tiles.append(tm)
        steps.append(n // tm)
    offs = [sum(steps[:k]) for k in range(len(groups))]

    def local(k):
        return lambda i: jnp.clip(i - offs[k], 0, steps[k] - 1)

    in_specs, out_specs, out_shape, args = [], [], [], []
    for k, arrays in enumerate(groups):
        for a in arrays:
            in_specs.append(pl.BlockSpec((tiles[k], a.shape[1]), lambda i, f=local(k): (f(i), 0)))
            args.append(a)
    for k, arrays in enumerate(groups):
        n = arrays[0].shape[0]
        for width, dtype, by_rows in out_defs:
            if by_rows:
                out_specs.append(pl.BlockSpec((tiles[k], width), lambda i, f=local(k): (f(i), 0)))
                out_shape.append(jax.ShapeDtypeStruct((n, width), dtype))
            else:
                out_specs.append(pl.BlockSpec((width, tiles[k]), lambda i, f=local(k): (0, f(i))))
                out_shape.append(jax.ShapeDtypeStruct((width, n), dtype))
    n_in = [len(arrays) for arrays in groups]
    n_out = len(out_defs)

    def kern(*refs):
        i = pl.program_id(0)
        pos = 0
        ins = []
        for cnt in n_in:
            ins.append(refs[pos:pos + cnt])
            pos += cnt
        crefs = refs[pos:pos + len(consts)]
        pos += len(consts)
        outs = [refs[pos + k * n_out:pos + (k + 1) * n_out] for k in range(len(groups))]
        scratch = refs[pos + len(groups) * n_out:]
        for k in range(len(groups)):
            @pl.when((i >= offs[k]) & (i < offs[k] + steps[k]))
            def _(k=k):
                body(*ins[k], *crefs, *outs[k], *scratch)

    res = pl.pallas_call(
        kern,
        grid=(sum(steps),),
        in_specs=in_specs + list(const_specs),
        out_specs=out_specs,
        out_shape=out_shape,
        scratch_shapes=scratch_shapes,
        compiler_params=_cparams("arbitrary"),
        name=name,
    )(*args, *consts)
    return [res[k * n_out:(k + 1) * n_out] for k in range(len(groups))]


RING_SLOTS = 3


def _ring_proj_call(body, x_main, x_tail, consts, const_specs, out_defs, name):
    n, d = x_main.shape
    tm = ROW_TILE
    steps = n // tm
    n_tail = x_tail.shape[0]
    assert n % tm == 0 and steps >= RING_SLOTS - 1 and n_tail <= tm

    def main_idx(i):
        return jnp.minimum(i, steps - 1)

    out_specs, out_shape = [], []
    for rows, idx in ((n, main_idx), (n_tail, lambda i: 0)):
        tile = min(tm, rows)
        for width, dtype, by_rows in out_defs:
            if by_rows:
                out_specs.append(pl.BlockSpec((tile, width), lambda i, f=idx: (f(i), 0)))
                out_shape.append(jax.ShapeDtypeStruct((rows, width), dtype))
            else:
                out_specs.append(pl.BlockSpec((width, tile), lambda i, f=idx: (0, f(i))))
                out_shape.append(jax.ShapeDtypeStruct((width, rows), dtype))
    n_out = len(out_defs)

    def kern(x_hbm, xt_ref, *refs):
        crefs = refs[:len(consts)]
        outs_main = refs[len(consts):len(consts) + n_out]
        outs_tail = refs[len(consts) + n_out:len(consts) + 2 * n_out]
        xbuf, sem = refs[len(consts) + 2 * n_out:]
        i = pl.program_id(0)

        def tile_copy(j, slot):
            rows = pl.ds(pl.multiple_of(j * tm, tm), tm)
            return pltpu.make_async_copy(x_hbm.at[rows, :], xbuf.at[slot], sem.at[slot])

        @pl.when(i == 0)
        def _():
            for j in range(RING_SLOTS - 1):
                tile_copy(j, j).start()

        @pl.when(i + RING_SLOTS - 1 < steps)
        def _():
            nxt = i + RING_SLOTS - 1
            tile_copy(nxt, nxt % RING_SLOTS).start()

        @pl.when(i < steps)
        def _():
            slot = i % RING_SLOTS
            tile_copy(i, slot).wait()
            body(xbuf.at[slot], *crefs, *outs_main)

        @pl.when(i >= steps)
        def _():
            body(xt_ref, *crefs, *outs_tail)

    res = pl.pallas_call(
        kern,
        grid=(steps + 1,),
        in_specs=[pl.BlockSpec(memory_space=pl.ANY), _const_spec((n_tail, d))] + list(const_specs),
        out_specs=out_specs,
        out_shape=out_shape,
        scratch_shapes=[pltpu.VMEM((RING_SLOTS, tm, d), F32), pltpu.SemaphoreType.DMA((RING_SLOTS,))],
        compiler_params=_cparams("arbitrary"),
        name=name,
    )(x_main, x_tail, *consts)
    return [res[:n_out], res[n_out:]]


def _norm_matmul(xs, g, layer, w, widx):
    d, m = w.shape[1], w.shape[2]
    res = _ring_proj_call(_norm_matmul_kernel, xs[0], xs[1], [g, w],
                          [_layer_spec((1, d), layer), _layer_spec((d, m), widx)],
                          [(m, F32, True)], "norm_matmul")
    return [r[0] for r in res]


def _mix_ffn_kernel(*refs, n_mix):
    x_ref = refs[0]
    a_refs = refs[1:1 + n_mix]
    wo_ref, g_ref, wg_ref, wu_ref, wd_ref, o_ref, act_ref = refs[1 + n_mix:]
    y = x_ref[...]
    off = 0
    for a_ref in a_refs:
        ka = a_ref.shape[1]
        y = y + jnp.dot(a_ref[...], wo_ref[off:off + ka, :], preferred_element_type=F32)
        off += ka
    h = _rms(y, g_ref[...]).astype(BF16)
    for f in range(D_FF // FF_TILE):
        cols = slice(f * FF_TILE, (f + 1) * FF_TILE)
        gate = jnp.dot(h, wg_ref[:, cols], preferred_element_type=F32)
        up = jnp.dot(h, wu_ref[:, cols], preferred_element_type=F32)
        act_ref[:, cols] = (gate * _sigmoid(gate) * up).astype(BF16)
    o_ref[...] = y + jnp.dot(act_ref[...], wd_ref[...], preferred_element_type=F32)


def _mix_ffn(groups, w_out, oidx, layer, g_ffn, wg, wu, wd):
    d = w_out.shape[2]
    n_mix = len(groups[0]) - 1
    tm = min(ROW_TILE, max(g[0].shape[0] for g in groups))
    res = _row_groups_call(
        functools.partial(_mix_ffn_kernel, n_mix=n_mix), groups, [w_out, g_ffn, wg, wu, wd],
        [_layer_spec(w_out.shape[1:], oidx), _layer_spec((1, d), layer), _layer_spec(wg.shape[1:], layer),
         _layer_spec(wu.shape[1:], layer), _layer_spec(wd.shape[1:], layer)],
        [(d, F32, True)], [pltpu.VMEM((tm, D_FF), BF16)], "mix_ffn")
    return [r[0] for r in res]


def _head_ones():
    r = lax.broadcasted_iota(jnp.int32, (LANES, LANES), 0) // HEAD_DIM
    c = lax.broadcasted_iota(jnp.int32, (LANES, LANES), 1) // HEAD_DIM
    return jnp.where(r == c, 1.0, 0.0).astype(BF16)


def _qk_prep(x, g, ones, ct, sa, sb):
    x2 = x * x
    hi = x2.astype(BF16)
    lo = (x2 - hi.astype(F32)).astype(BF16)
    ss = jnp.dot(hi, ones, preferred_element_type=F32) + jnp.dot(lo, ones, preferred_element_type=F32)
    xn = x * lax.rsqrt(ss * (1.0 / HEAD_DIM) + EPS) * g
    return xn * ct + pltpu.roll(xn, LANES - ROT_DIM // 2, 1) * sa + pltpu.roll(xn, ROT_DIM // 2, 1) * sb


def _rope_tables(pos):
    half = ROT_DIM // 2
    inv = jnp.power(jnp.float32(ROPE_THETA), -jnp.arange(half, dtype=F32) / half)
    ang = pos.astype(F32)[:, None] * inv[None, :]
    cos, sin = jnp.cos(ang), jnp.sin(ang)
    n = pos.shape[0]
    one = jnp.ones((n, HEAD_DIM - ROT_DIM), F32)
    zero = jnp.zeros((n, HEAD_DIM - ROT_DIM), F32)
    z8 = jnp.zeros((n, half), F32)
    ct = jnp.concatenate([cos, cos, one], axis=1)
    sa = jnp.concatenate([-sin, z8, zero], axis=1)
    sb = jnp.concatenate([z8, sin, zero], axis=1)
    tile = lambda t: jnp.concatenate([t, t], axis=1)
    return tile(ct), tile(sa), tile(sb)


ATTN_SEQS = 4
ATTN_QCHUNKS = ATTN_WIDTH // LANES
ATTN_HEAD_ORDER = tuple(h * ATTN_QCHUNKS + j for j in range(ATTN_QCHUNKS) for h in range(KV_HEADS))


def _attn_prompt_kernel(q_ref, kv_ref, ct_ref, sa_ref, sb_ref, gq_ref, gk_ref, sink_ref,
                        o_ref, pk_ref, pv_ref, kprev, vprev, *, nb, layer, nseq):
    i = pl.program_id(1)

    @pl.when(i == 0)
    def _():
        kprev[...] = jnp.zeros_like(kprev)
        vprev[...] = jnp.zeros_like(vprev)

    ones = _head_ones()
    ct, sa, sb = ct_ref[...], sa_ref[...], sb_ref[...]
    r = lax.broadcasted_iota(jnp.int32, (WINDOW, 2 * WINDOW), 0)
    c = lax.broadcasted_iota(jnp.int32, (WINDOW, 2 * WINDOW), 1)
    rel = r + WINDOW - c
    mask = (rel >= 0) & (rel <= WINDOW) & ((c >= WINDOW) | (i > 0))
    lane = lax.broadcasted_iota(jnp.int32, (WINDOW, LANES), 1)
    group0 = lane < HEAD_DIM
    v_ones = jnp.ones((2 * WINDOW, LANES), BF16)
    nq = ATTN_QCHUNKS
    st = [dict() for _ in range(nseq)]

    def prep(sq):
        d = st[sq]
        kv = kv_ref[sq]
        d["kn"] = _qk_prep(kv[:, :KV_WIDTH], gk_ref[...], ones, ct, sa, sb)
        d["v"] = kv[:, KV_WIDTH:]
        d["qn"] = [_qk_prep(q_ref[sq, :, j * LANES:(j + 1) * LANES], gq_ref[...], ones, ct, sa, sb)
                   * (HEAD_DIM ** -0.5) for j in range(nq)]
        d["kcat"] = jnp.concatenate([kprev[sq], d["kn"]], axis=0).astype(BF16)
        d["vaug"] = jnp.concatenate([jnp.concatenate([vprev[sq], d["v"]], axis=0).astype(BF16), v_ones], axis=1)
        kprev[sq] = d["kn"]
        vprev[sq] = d["v"]

    def scores(sq, h):
        d = st[sq]
        keep = group0 if h == 0 else jnp.logical_not(group0)
        qs = jnp.concatenate([jnp.where(keep, qj, 0.0) for qj in d["qn"]], axis=0).astype(BF16)
        d["s", h] = lax.dot_general(qs, d["kcat"], (((1,), (1,)), ((), ())), preferred_element_type=F32)

    def softmax_pv(sq, h):
        d = st[sq]
        s = d.pop(("s", h))
        ps, corr = [], []
        for j in range(nq):
            sg = jnp.where(mask, s[j * WINDOW:(j + 1) * WINDOW], NEG_INF)
            sink = sink_ref[layer, h * nq + j]
            m = jnp.maximum(jnp.max(sg, axis=-1, keepdims=True), sink)
            ps.append(jnp.exp((sg - m).astype(BF16)))
            corr.append(jnp.exp(sink - m))
        o = jnp.dot(jnp.concatenate(ps, axis=0), d["vaug"], preferred_element_type=F32)
        d["o", h] = [o[j * WINDOW:(j + 1) * WINDOW, :LANES] / (o[j * WINDOW:(j + 1) * WINDOW, LANES:] + corr[j])
                     for j in range(nq)]

    def finish(sq):
        d = st[sq]
        o_ref[sq] = jnp.concatenate([jnp.where(group0, d["o", 0][j], d["o", 1][j]) for j in range(nq)],
                                    axis=1).astype(BF16)

    for sq in range(nseq):
        prep(sq)
    for sq in range(nseq):
        scores(sq, 0)
        scores(sq, 1)
    for sq in range(nseq):
        softmax_pv(sq, 0)
        softmax_pv(sq, 1)
        finish(sq)

    @pl.when(i == nb - 1)
    def _():
        for sq in range(nseq):
            pk_ref[sq] = st[sq]["kn"]
            pv_ref[sq] = st[sq]["v"]


def _attn_prompt(proj, tables, gq, gk, sinks, layer, prev):
    bsz, seq, _ = proj.shape
    nb = seq // WINDOW
    nseq = ATTN_SEQS
    n_layers = gq.shape[0]
    tab = pl.BlockSpec((WINDOW, LANES), lambda b, i: (i, 0))
    prev, prev_specs, aliases = _alias_inputs(prev, 1)
    win = pl.BlockSpec((None, nseq, WINDOW, KV_WIDTH), lambda b, i: (layer, b, 0, 0))
    win_shape = jax.ShapeDtypeStruct((n_layers, bsz, WINDOW, KV_WIDTH), F32)
    return pl.pallas_call(
        _skip_refs(functools.partial(_attn_prompt_kernel, nb=nb, layer=layer, nseq=nseq), len(prev)),
        grid=(bsz // nseq, nb),
        in_specs=prev_specs + [
            pl.BlockSpec((nseq, WINDOW, ATTN_WIDTH), lambda b, i: (b, i, 0)),
            pl.BlockSpec((nseq, WINDOW, 2 * KV_WIDTH), lambda b, i: (b, i, EVEN_KV_BLOCK)),
            tab, tab, tab, _layer_spec((1, LANES), layer), _layer_spec((1, LANES), layer),
            pl.BlockSpec(memory_space=pltpu.SMEM)],
        out_specs=[pl.BlockSpec((nseq, WINDOW, ATTN_WIDTH), lambda b, i: (b, i, 0)), win, win],
        out_shape=[jax.ShapeDtypeStruct((bsz, seq, ATTN_WIDTH), BF16), win_shape, win_shape],
        input_output_aliases=aliases,
        scratch_shapes=[pltpu.VMEM((nseq, WINDOW, KV_WIDTH), F32), pltpu.VMEM((nseq, WINDOW, KV_WIDTH), F32)],
        compiler_params=_cparams("parallel", "arbitrary"),
        name="attn_prompt",
    )(*prev, proj, proj, *tables, gq, gk, sinks)


EVEN_U_BLOCK = ATTN_WIDTH // S5_WIDTH
EVEN_KV_BLOCK = (ATTN_WIDTH + S5_WIDTH) // (2 * KV_WIDTH)
KALL_ROWS = WINDOW + SUBLANES


def _attn_sample_kernel(q_ref, kv_ref, ck_ref, cv_ref, ct_ref, sa_ref, sb_ref, gq_ref, gk_ref, sink_ref,
                        o_ref, nk_ref, nv_ref, o_seq, *, bs, t_new, layer):
    ones = _head_ones()
    ct, sa, sb = ct_ref[...], sa_ref[...], sb_ref[...]
    kv = kv_ref[...]
    kn = _qk_prep(kv[:, :KV_WIDTH], gk_ref[...], ones, ct, sa, sb)
    v = kv[:, KV_WIDTH:]
    nq = ATTN_QCHUNKS
    qn = [_qk_prep(q_ref[:, j * LANES:(j + 1) * LANES], gq_ref[...], ones, ct, sa, sb) * (HEAD_DIM ** -0.5)
          for j in range(nq)]
    rows = nq * t_new
    r = lax.broadcasted_iota(jnp.int32, (rows, KALL_ROWS), 0)
    c = lax.broadcasted_iota(jnp.int32, (rows, KALL_ROWS), 1)
    t = r % t_new
    mask = (c >= t) & (c <= t + WINDOW)
    rj = lax.broadcasted_iota(jnp.int32, (rows, 1), 0) // t_new
    lane = lax.broadcasted_iota(jnp.int32, (t_new, LANES), 1)
    group0 = lane < HEAD_DIM
    pad = jnp.zeros((KALL_ROWS - WINDOW - t_new, KV_WIDTH), F32)
    ones_c = jnp.ones((LANES, WINDOW), BF16)
    ones_n = jnp.ones((KALL_ROWS - WINDOW, LANES), BF16)
    klane = lax.broadcasted_iota(jnp.int32, (KV_WIDTH, WINDOW), 1)
    zcols = jnp.zeros((KV_WIDTH, WINDOW - (KALL_ROWS - WINDOW)), F32)
    nt = (((1,), (1,)), ((), ()))

    def shifted(cache_t, new_rows):
        new_t = jnp.concatenate([jnp.concatenate([new_rows, pad], axis=0).T, zcols], axis=1)
        return jnp.where(klane >= WINDOW - t_new, pltpu.roll(new_t, WINDOW - t_new, 1),
                         pltpu.roll(cache_t, WINDOW - t_new, 1))

    def seq_rows(a, b):
        return jnp.concatenate([a[tt * bs + b:tt * bs + b + 1] for tt in range(t_new)], axis=0)

    sinks = []
    for h in range(KV_HEADS):
        sk = jnp.zeros((rows, 1), F32)
        for j in range(nq):
            sk = jnp.where(rj == j, sink_ref[layer, h * nq + j], sk)
        sinks.append(sk)

    st = [dict() for _ in range(bs)]
    for b in range(bs):
        d = st[b]
        ck, cv = ck_ref[b], cv_ref[b]
        kn_b, v_b = seq_rows(kn, b), seq_rows(v, b)
        nk_ref[b] = shifted(ck, kn_b)
        nv_ref[b] = shifted(cv, v_b)
        ckb = ck.astype(BF16)
        knb = jnp.concatenate([kn_b, pad], axis=0).astype(BF16)
        d["vc"] = jnp.concatenate([cv.astype(BF16), ones_c], axis=0)
        d["vn"] = jnp.concatenate([jnp.concatenate([v_b, pad], axis=0).astype(BF16), ones_n], axis=1)
        qb = [seq_rows(qj, b) for qj in qn]
        for h in range(KV_HEADS):
            keep = group0 if h == 0 else jnp.logical_not(group0)
            qs = jnp.concatenate([jnp.where(keep, q, 0.0) for q in qb], axis=0).astype(BF16)
            d["s", h] = jnp.concatenate([jnp.dot(qs, ckb, preferred_element_type=F32),
                                         lax.dot_general(qs, knb, nt, preferred_element_type=F32)], axis=1)
    for b in range(bs):
        d = st[b]
        for h in range(KV_HEADS):
            s = jnp.where(mask, d.pop(("s", h)), NEG_INF)
            m = jnp.maximum(jnp.max(s, axis=-1, keepdims=True), sinks[h])
            p = jnp.exp(s - m).astype(BF16)
            o = (lax.dot_general(p[:, :WINDOW], d["vc"], nt, preferred_element_type=F32)
                 + jnp.dot(p[:, WINDOW:], d["vn"], preferred_element_type=F32))
            d["o", h] = o[:, :LANES] / (o[:, LANES:] + jnp.exp(sinks[h] - m))
    for b in range(bs):
        d = st[b]
        o_b = jnp.concatenate([jnp.where(group0, d["o", 0][j * t_new:(j + 1) * t_new],
                                         d["o", 1][j * t_new:(j + 1) * t_new]) for j in range(nq)], axis=1)
        for tt in range(t_new):
            o_seq[tt * bs + b:tt * bs + b + 1, :] = o_b[tt:tt + 1]
    o_ref[...] = o_seq[...].astype(BF16)


def _attn_sample(proj, cache_k, cache_v, tables, gq, gk, sinks, t_new, layer, prev):
    n = proj.shape[0]
    bsz = n // t_new
    bs = SAMPLE_TILE
    rows = bs * t_new
    row = lambda i: (i, 0)
    cache = pl.BlockSpec((None, bs, KV_WIDTH, WINDOW), lambda i: (layer, i, 0, 0))
    prev, prev_specs, aliases = _alias_inputs(prev, 1)
    return pl.pallas_call(
        _skip_refs(functools.partial(_attn_sample_kernel, bs=bs, t_new=t_new, layer=layer), len(prev)),
        grid=(bsz // bs,),
        in_specs=prev_specs + [
            pl.BlockSpec((rows, ATTN_WIDTH), row),
            pl.BlockSpec((rows, 2 * KV_WIDTH), lambda i: (i, EVEN_KV_BLOCK)),
            cache, cache,
            _const_spec((rows, LANES)), _const_spec((rows, LANES)), _const_spec((rows, LANES)),
            _layer_spec((1, LANES), layer), _layer_spec((1, LANES), layer),
            pl.BlockSpec(memory_space=pltpu.SMEM)],
        out_specs=[pl.BlockSpec((rows, ATTN_WIDTH), row), cache, cache],
        out_shape=[jax.ShapeDtypeStruct((n, ATTN_WIDTH), BF16),
                   jax.ShapeDtypeStruct(cache_k.shape, F32), jax.ShapeDtypeStruct(cache_v.shape, F32)],
        input_output_aliases=aliases,
        scratch_shapes=[pltpu.VMEM((rows, ATTN_WIDTH), F32)],
        compiler_params=_cparams("parallel"),
        name="attn_sample",
    )(*prev, proj, proj, cache_k, cache_v, *tables, gq, gk, sinks)


S5_UCHUNKS = S5_WIDTH // LANES
S5_SUB = S5_FLAT // S5_UCHUNKS
S5_SCHUNKS = S5_FLAT // LANES


def _s5_tail(y, wglu_ref, bglu_ref):
    g = 0.5 * y * (1.0 + lax.erf(y * (2.0 ** -0.5)))
    z = jnp.dot(g.astype(BF16), wglu_ref[...], preferred_element_type=F32) + bglu_ref[...]
    return g * _sigmoid(z)


S5_PARTS = 4


def _s5_prompt_kernel(u_ref, wb_ref, wc_ref, lam_ref, d_ref, wglu_ref, bglu_ref,
                      o_ref, sr_ref, si_ref, xs, hst, *, nbatch, tc):
    rows = nbatch * tc
    prow, ptok = rows // S5_PARTS, tc // S5_PARTS

    @pl.when(pl.program_id(1) == 0)
    def _():
        hst[...] = jnp.zeros_like(hst)

    u = jnp.swapaxes(u_ref[...], 0, 1).reshape(rows, S5_WIDTH)
    ub = u.astype(BF16)

    def in_proj(p, cc):
        rs = slice(p * prow, (p + 1) * prow)
        res = jnp.dot(ub[rs, cc * LANES:(cc + 1) * LANES], wb_ref[cc], preferred_element_type=F32)
        for j in range(S5_SUB // LANES):
            xs[cc * 4 + j, rs, :] = res[:, j * LANES:(j + 1) * LANES]
            xs[S5_SCHUNKS + cc * 4 + j, rs, :] = res[:, S5_SUB + j * LANES:S5_SUB + (j + 1) * LANES]

    ys = {}

    def out_proj(p, cc):
        rs = slice(p * prow, (p + 1) * prow)
        s = jnp.concatenate([xs[cc * 4 + j, rs, :] for j in range(4)]
                            + [xs[S5_SCHUNKS + cc * 4 + j, rs, :] for j in range(4)], axis=1).astype(BF16)
        cols = slice(cc * LANES, (cc + 1) * LANES)
        ys[p, cc] = jnp.dot(s, wc_ref[cc], preferred_element_type=F32) + d_ref[:, cols] * u[rs, cols]

    def tail(p):
        out = _s5_tail(jnp.concatenate([ys.pop((p, cc)) for cc in range(S5_UCHUNKS)], axis=1), wglu_ref, bglu_ref)
        o_ref[:, p * ptok:(p + 1) * ptok, :] = jnp.swapaxes(out.reshape(ptok, nbatch, S5_WIDTH), 0, 1).astype(BF16)

    def scan_step(t, h):
        idx = slice(t * nbatch, (t + 1) * nbatch)
        new = list(h)
        for k in range(S5_SCHUNKS):
            hr, hi = h[k], h[S5_SCHUNKS + k]
            lr, li = lam_ref[k], lam_ref[S5_SCHUNKS + k]
            nr = lr * hr - li * hi + xs[k, idx, :]
            ni = lr * hi + li * hr + xs[S5_SCHUNKS + k, idx, :]
            xs[k, idx, :] = nr
            xs[S5_SCHUNKS + k, idx, :] = ni
            new[k], new[S5_SCHUNKS + k] = nr, ni
        return new

    for cc in range(S5_UCHUNKS):
        in_proj(0, cc)
    h = [hst[k] for k in range(2 * S5_SCHUNKS)]
    for p in range(S5_PARTS):
        work = []
        if p + 1 < S5_PARTS:
            work += [functools.partial(in_proj, p + 1, cc) for cc in range(S5_UCHUNKS)]
        if p >= 1:
            work += [functools.partial(out_proj, p - 1, cc) for cc in range(S5_UCHUNKS)]
            work.append(functools.partial(tail, p - 1))
        every = max(1, ptok // max(1, len(work)))
        for i in range(ptok):
            h = scan_step(p * ptok + i, h)
            if work and (i + 1) % every == 0:
                work.pop(0)()
        for w in work:
            w()
    for cc in range(S5_UCHUNKS):
        out_proj(S5_PARTS - 1, cc)
    tail(S5_PARTS - 1)
    for k in range(2 * S5_SCHUNKS):
        hst[k] = h[k]
    sr_ref[...] = jnp.concatenate(h[:S5_SCHUNKS], axis=1)
    si_ref[...] = jnp.concatenate(h[S5_SCHUNKS:], axis=1)


def _s5_prompt(proj, prm, layer, prev):
    bsz, seq, _ = proj.shape
    nbatch, tc = SUBLANES, S5_CHUNK
    n_layers = prm["wb"].shape[0]
    st = pl.BlockSpec((None, nbatch, S5_FLAT), lambda b, c: (layer, b, 0))
    st_shape = jax.ShapeDtypeStruct((n_layers, bsz, S5_FLAT), F32)
    prev, prev_specs, aliases = _alias_inputs(prev, 1)
    names = ("wb", "wc", "lam8", "d", "wglu", "bglu")
    return pl.pallas_call(
        _skip_refs(functools.partial(_s5_prompt_kernel, nbatch=nbatch, tc=tc), len(prev)),
        grid=(bsz // nbatch, seq // tc),
        in_specs=prev_specs + [pl.BlockSpec((nbatch, tc, S5_WIDTH), lambda b, c: (b, c, EVEN_U_BLOCK))]
        + [_layer_spec(prm[k].shape[1:], layer) for k in names],
        out_specs=[pl.BlockSpec((nbatch, tc, S5_WIDTH), lambda b, c: (b, c, 0)), st, st],
        out_shape=[jax.ShapeDtypeStruct((bsz, seq, S5_WIDTH), BF16), st_shape, st_shape],
        input_output_aliases=aliases,
        scratch_shapes=[pltpu.VMEM((2 * S5_SCHUNKS, nbatch * tc, LANES), F32),
                        pltpu.VMEM((2 * S5_SCHUNKS, nbatch, LANES), F32)],
        compiler_params=_cparams("parallel", "arbitrary"),
        name="s5_prompt",
    )(*prev, proj, *[prm[k] for k in names])


def _s5_sample_kernel(u_ref, wb_ref, wc_ref, lr_ref, li_ref, d_ref, wglu_ref, bglu_ref, h0r_ref, h0i_ref,
                      o_ref, sr_ref, si_ref, xr, xi, *, nseq, t_new):
    nt, st = nseq // SAMPLE_TILE, SAMPLE_TILE
    n = nseq * t_new
    u = u_ref[...]
    ub = u.astype(BF16)
    for cc in range(S5_UCHUNKS):
        res = jnp.dot(ub[:, cc * LANES:(cc + 1) * LANES], wb_ref[cc], preferred_element_type=F32)
        sc = slice(cc * S5_SUB, (cc + 1) * S5_SUB)
        xr[:, :, :, sc] = res[:, :S5_SUB].reshape(nt, t_new, st, S5_SUB)
        xi[:, :, :, sc] = res[:, S5_SUB:].reshape(nt, t_new, st, S5_SUB)
    lr, li = lr_ref[...], li_ref[...]
    hr, hi = h0r_ref[...], h0i_ref[...]
    for t in range(t_new):
        nr = lr * hr - li * hi + xr[:, t].reshape(nseq, S5_FLAT)
        ni = lr * hi + li * hr + xi[:, t].reshape(nseq, S5_FLAT)
        xr[:, t] = nr.reshape(nt, st, S5_FLAT)
        xi[:, t] = ni.reshape(nt, st, S5_FLAT)
        hr, hi = nr, ni
    sr_ref[...] = hr
    si_ref[...] = hi
    ys = []
    for cc in range(S5_UCHUNKS):
        sc = slice(cc * S5_SUB, (cc + 1) * S5_SUB)
        s = jnp.concatenate([xr[:, :, :, sc].reshape(n, S5_SUB), xi[:, :, :, sc].reshape(n, S5_SUB)],
                            axis=1).astype(BF16)
        cols = slice(cc * LANES, (cc + 1) * LANES)
        ys.append(jnp.dot(s, wc_ref[cc], preferred_element_type=F32) + d_ref[:, cols] * u[:, cols])
    o_ref[...] = _s5_tail(jnp.concatenate(ys, axis=1), wglu_ref, bglu_ref).astype(BF16)


def _s5_sample(proj, h0r, h0i, prm, t_new, layer, prev):
    n = proj.shape[0]
    nseq = n // t_new
    names = ("wb", "wc", "lr", "li", "d", "wglu", "bglu")
    st = pl.BlockSpec((None, nseq, S5_FLAT), lambda i: (layer, 0, 0))
    prev, prev_specs, aliases = _alias_inputs(prev, 1)
    scratch = pltpu.VMEM((nseq // SAMPLE_TILE, t_new, SAMPLE_TILE, S5_FLAT), F32)
    return pl.pallas_call(
        _skip_refs(functools.partial(_s5_sample_kernel, nseq=nseq, t_new=t_new), len(prev)),
        grid=(1,),
        in_specs=prev_specs + [pl.BlockSpec((n, S5_WIDTH), lambda i: (0, EVEN_U_BLOCK))]
        + [_layer_spec(prm[k].shape[1:], layer) for k in names]
        + [_layer_spec((nseq, S5_FLAT), layer), _layer_spec((nseq, S5_FLAT), layer)],
        out_specs=[pl.BlockSpec((n, S5_WIDTH), lambda i: (0, 0)), st, st],
        out_shape=[jax.ShapeDtypeStruct((n, S5_WIDTH), BF16),
                   jax.ShapeDtypeStruct(h0r.shape, F32), jax.ShapeDtypeStruct(h0i.shape, F32)],
        input_output_aliases=aliases,
        scratch_shapes=[scratch, scratch],
        compiler_params=_cparams("arbitrary"),
        name="s5_sample",
    )(*prev, proj, *[prm[k] for k in names], h0r, h0i)


def _s5_params(a_re, a_im, log_dt, b_re, b_im, c_re, c_im, d_skip, w_glu, b_glu):
    nl = a_re.shape[0]
    dt = jnp.exp(log_dt)
    mag = jnp.exp(a_re * dt)
    lr, li = mag * jnp.cos(a_im * dt), mag * jnp.sin(a_im * dt)
    den = a_re * a_re + a_im * a_im
    cr = ((lr - 1.0) * a_re + li * a_im) / den
    ci = (li * a_re - (lr - 1.0) * a_im) / den
    bbr = cr[..., None] * b_re - ci[..., None] * b_im
    bbi = cr[..., None] * b_im + ci[..., None] * b_re
    gpc = LANES // S5_GROUP
    eye = jnp.eye(gpc, dtype=F32)

    def in_blocks(bb):
        bb = bb.reshape(nl, S5_UCHUNKS, gpc, S5_STATE, S5_GROUP)
        return jnp.einsum("lcgph,gk->lcghkp", bb, eye).reshape(nl, S5_UCHUNKS, LANES, S5_SUB)

    def out_blocks(cm):
        cm = cm.reshape(nl, S5_UCHUNKS, gpc, S5_GROUP, S5_STATE)
        return jnp.einsum("lcghp,gk->lcgpkh", cm, eye).reshape(nl, S5_UCHUNKS, S5_SUB, LANES)

    wb = jnp.concatenate([in_blocks(bbr), in_blocks(bbi)], axis=3).astype(BF16)
    wc = jnp.concatenate([out_blocks(c_re), -out_blocks(c_im)], axis=2).astype(BF16)
    lr_f, li_f = lr.reshape(nl, 1, S5_FLAT), li.reshape(nl, 1, S5_FLAT)
    lam = jnp.concatenate([lr_f.reshape(nl, S5_SCHUNKS, 1, LANES), li_f.reshape(nl, S5_SCHUNKS, 1, LANES)], axis=1)
    lam8 = jnp.broadcast_to(lam, (nl, 2 * S5_SCHUNKS, SUBLANES, LANES))
    return dict(wb=wb, wc=wc, lam8=lam8, lr=lr_f, li=li_f, d=d_skip.reshape(nl, 1, S5_WIDTH),
                wglu=w_glu.astype(BF16), bglu=b_glu.reshape(nl, 1, S5_WIDTH))


ML_AUG = 2 * ML_DV


EXP_CLAMP = 88.0


def _den_floor(m_row):
    return jnp.exp(jnp.minimum(-m_row, EXP_CLAMP))


def _head_out(h, o, gout):
    hn = h * lax.rsqrt(jnp.mean(h * h, axis=-1, keepdims=True) + EPS) * gout
    return (hn * _sigmoid(o)).astype(BF16)


ODDP_V_BLOCK = 0
ODDP_O_BLOCK = 1
ODDP_Q_BLOCK = (2 * ML_WIDTH) // ML_QK
ODDP_G_BLOCK = (2 * ML_WIDTH + ML_QK) // LANES
ML_SPLIT = 3
ML_PIECE_LANES = 2 * ML_HEADS
ML_SEQS = 8
ML_STAGE_LAG = 2


def _norm_matmul_kt_kernel(x_ref, g_ref, wt_ref, wg_ref, o_ref, kt_ref):
    h = _rms(x_ref[...], g_ref[...]).astype(BF16)
    k0, v0, g0 = ML_QK, 2 * ML_QK, 2 * ML_QK + 2 * ML_WIDTH
    nt = (((1,), (1,)), ((), ()))
    o_ref[:, :g0 - v0] = lax.dot_general(h, wt_ref[v0:g0, :].astype(BF16), nt, preferred_element_type=F32)
    o_ref[:, g0 - v0:g0 - v0 + k0] = lax.dot_general(h, wt_ref[:k0, :].astype(BF16), nt, preferred_element_type=F32)
    o_ref[:, g0 - v0 + k0:] = jnp.dot(h, wg_ref[...], preferred_element_type=F32)
    kt = lax.dot_general(wt_ref[k0:v0, :].astype(BF16), h, nt, preferred_element_type=F32)
    kt_ref[...] = kt * (ML_DK ** -0.5)


def _norm_matmul_kt(xs, g, layer, wt, wg, widx):
    d = wt.shape[2]
    m_out = 2 * ML_WIDTH + ML_QK + wg.shape[2]
    return _ring_proj_call(
        _norm_matmul_kt_kernel, xs[0], xs[1], [g, wt, wg],
        [_layer_spec((1, d), layer), _layer_spec(wt.shape[1:], widx), _layer_spec(wg.shape[1:], widx)],
        [(m_out, F32, True), (ML_QK, F32, False)], "norm_matmul_kt")


def _cummax_rows(x):
    n = x.shape[0]
    row = lax.broadcasted_iota(jnp.int32, x.shape, 0)
    shift = 1
    while shift < n:
        x = jnp.maximum(x, jnp.where(row >= shift, pltpu.roll(x, shift, 0), NEG_INF))
        shift *= 2
    return x


def _pieces(x):
    lane = lax.broadcasted_iota(jnp.int32, x.shape, 1)
    xx = x + pltpu.roll(x, ML_PIECE_LANES, 1) + pltpu.roll(x, 2 * ML_PIECE_LANES, 1)
    a1, a2, a3 = _split3(xx)
    return jnp.where(lane < ML_PIECE_LANES, a1, jnp.where(lane < 2 * ML_PIECE_LANES, a2, a3))


def _ml_select_constants():
    mask = np.zeros((ML_HEADS, LANES), np.float32)
    sel = np.zeros((ML_HEADS, LANES, 2 * ML_DV), np.float32)
    for h in range(ML_HEADS):
        for k in range(ML_SPLIT):
            lo, hi = k * ML_PIECE_LANES + h, k * ML_PIECE_LANES + ML_HEADS + h
            mask[h, lo] = mask[h, hi] = 1.0
            sel[h, lo, :ML_DV] = 1.0
            sel[h, hi, ML_DV:] = 1.0
    return jnp.asarray(mask), jnp.asarray(sel, dtype=BF16)


def _mlstm_prompt_kernel(*refs, tc, nchunks, nseq):
    v_ref, o_ref, q_ref, g_ref = refs[:4]
    kt_refs = refs[4:4 + nseq]
    bias_ref, gout_ref, mask_ref, sel_ref, h_ref, c_ref, n_ref, m_ref, caug, mst = refs[4 + nseq:]
    ci = pl.program_id(1)

    @pl.when(ci == 0)
    def _():
        caug[...] = jnp.zeros_like(caug)
        mst[...] = jnp.zeros_like(mst)

    nh = ML_HEADS
    lane = lax.broadcasted_iota(jnp.int32, (tc, LANES), 1)
    lo, hi = lane < nh, (lane >= nh) & (lane < 2 * nh)
    rt = lax.broadcasted_iota(jnp.int32, (tc, tc), 0)
    cs = lax.broadcasted_iota(jnp.int32, (tc, tc), 1)
    causal = cs <= rt
    tril = jnp.where(causal, 1.0, 0.0).astype(BF16)
    ones = jnp.ones((tc, ML_DV), F32)

    def gates(sq):
        g = g_ref[sq] + bias_ref[...]
        lf = jnp.where(hi, _log_sigmoid(g), 0.0)
        b = sum(jnp.dot(tril, p, preferred_element_type=F32) for p in _split3(lf))
        c = jnp.where(hi, pltpu.roll(g, nh, 1) - b, 0.0)
        m_prev = mst[sq]
        mx = jnp.maximum(_cummax_rows(c), m_prev)
        m_row = b + mx
        mx_lo = pltpu.roll(mx, LANES - nh, 1)
        w_inter = jnp.exp(pltpu.roll(m_prev, LANES - nh, 1) - mx_lo)
        mst[sq] = m_row[tc - 1:tc, :]
        return dict(xc=_pieces(jnp.where(lo, w_inter, jnp.where(hi, _den_floor(m_row), 0.0))),
                    lc=_pieces(jnp.where(lo, -mx_lo, jnp.where(hi, 1.0, 0.0))),
                    rc=_pieces(jnp.where(lo, 1.0, jnp.where(hi, c, 0.0))))

    gt = [gates(sq) for sq in range(nseq)]
    units = [(sq, hd) for hd in range(nh) for sq in range(nseq)]
    st = [dict() for _ in units]

    def stage1(u):
        sq, hd = units[u]
        d = st[u]
        rh = gt[sq]["rc"] * mask_ref[hd:hd + 1, :].astype(BF16)
        d["dmat"] = lax.dot_general(gt[sq]["lc"], rh, (((1,), (1,)), ((), ())), preferred_element_type=F32)
        d["wb"] = jnp.dot(gt[sq]["xc"], sel_ref[hd], preferred_element_type=F32)
        d["qh"] = q_ref[sq, :, hd * ML_DK:(hd + 1) * ML_DK]
        d["kt"] = kt_refs[sq][hd * ML_DK:(hd + 1) * ML_DK, :]
        d["qk"] = jnp.dot(d["qh"].astype(BF16), d["kt"].astype(BF16), preferred_element_type=F32)

    def stage2(u):
        sq, hd = units[u]
        d = st[u]
        cols = slice(hd * ML_DV, (hd + 1) * ML_DV)
        d["w"] = jnp.exp(jnp.where(causal, d["dmat"], NEG_INF))
        d["vaug"] = jnp.concatenate([v_ref[sq, :, cols], ones], axis=1).astype(BF16)
        d["cm"] = caug[sq, hd]
        lhs = jnp.concatenate([(d["qk"] * d["w"]).astype(BF16), (d["wb"][:, :ML_DK] * d["qh"]).astype(BF16)], axis=1)
        rhs = jnp.concatenate([d["vaug"], d["cm"].astype(BF16)], axis=0)
        d["both"] = jnp.dot(lhs, rhs, preferred_element_type=F32)
        kw = (d["kt"] * d["w"][tc - 1:tc, :]).astype(BF16)
        d["upd"] = jnp.dot(kw, d["vaug"], preferred_element_type=F32)

    def stage3(u):
        sq, hd = units[u]
        d = st[u]
        cols = slice(hd * ML_DV, (hd + 1) * ML_DV)
        both, wb = d["both"], d["wb"]
        h = both[:, :ML_DV] / jnp.maximum(jnp.abs(both[:, ML_DV:]), wb[:, ML_DV:])
        h_ref[sq, :, cols] = _head_out(h, o_ref[sq, :, cols], gout_ref[:, cols])
        decay = wb[tc - 1:tc, :ML_DV]
        caug[sq, hd] = jnp.concatenate([decay, decay], axis=1) * d["cm"] + d["upd"]
        d.clear()

    for step in range(len(units) + 2 * ML_STAGE_LAG):
        if step < len(units):
            stage1(step)
        if 0 <= step - ML_STAGE_LAG < len(units):
            stage2(step - ML_STAGE_LAG)
        if 0 <= step - 2 * ML_STAGE_LAG < len(units):
            stage3(step - 2 * ML_STAGE_LAG)

    @pl.when(ci == nchunks - 1)
    def _():
        c_ref[...] = caug[:, :, :, :ML_DV]
        n_ref[...] = caug[:, :, :, ML_DV:]
        m_ref[...] = mst[...]


def _kt_index(b, c, *, sq, nseq, nchunks):
    return 0, (b * nseq + sq) * nchunks + c


def _mlstm_prompt(proj, kt, bias, gout, consts, layer, prev):
    bsz, seq, _ = proj.shape
    tc, nseq = ML_CHUNK, ML_SEQS
    nchunks = seq // tc
    n_layers = bias.shape[0]
    mask, sel = consts
    blk = lambda w, j: pl.BlockSpec((nseq, tc, w), lambda b, c: (b, c, j))
    st = lambda shape: pl.BlockSpec((None, nseq) + shape, lambda b, c: (layer, b) + (0,) * len(shape))
    st_shape = lambda shape: jax.ShapeDtypeStruct((n_layers, bsz) + shape, F32)
    prev, prev_specs, aliases = _alias_inputs(prev, 1)
    return pl.pallas_call(
        _skip_refs(functools.partial(_mlstm_prompt_kernel, tc=tc, nchunks=nchunks, nseq=nseq), len(prev)),
        grid=(bsz // nseq, nchunks),
        in_specs=prev_specs + [
            blk(ML_WIDTH, ODDP_V_BLOCK), blk(ML_WIDTH, ODDP_O_BLOCK), blk(ML_QK, ODDP_Q_BLOCK),
            blk(LANES, ODDP_G_BLOCK)]
        + [pl.BlockSpec((ML_QK, tc), functools.partial(_kt_index, sq=sq, nseq=nseq, nchunks=nchunks))
           for sq in range(nseq)] + [
            _layer_spec((1, LANES), layer), _layer_spec((1, ML_WIDTH), layer),
            _const_spec(mask.shape), _const_spec(sel.shape)],
        out_specs=[blk(ML_WIDTH, 0), st((ML_HEADS, ML_DK, ML_DV)), st((ML_HEADS, ML_DK, ML_DV)), st((1, LANES))],
        out_shape=[jax.ShapeDtypeStruct((bsz, seq, ML_WIDTH), BF16),
                   st_shape((ML_HEADS, ML_DK, ML_DV)), st_shape((ML_HEADS, ML_DK, ML_DV)), st_shape((1, LANES))],
        input_output_aliases=aliases,
        scratch_shapes=[pltpu.VMEM((nseq, ML_HEADS, ML_DK, ML_AUG), F32), pltpu.VMEM((nseq, 1, LANES), F32)],
        compiler_params=_cparams("parallel", "arbitrary"),
        name="mlstm_prompt",
    )(*prev, proj, proj, proj, proj, *([kt] * nseq), bias, gout, mask, sel)


MLS_SEQS = 32


def _mlstm_sample_kernel(v_ref, o_ref, q_ref, g_ref, kt_ref, bias_ref, gout_ref, mask_ref, sel_ref,
                         c0_ref, n0_ref, m0_ref, h_ref, c_ref, n_ref, m_ref, *, nseq, t_new):
    nh, nt, st = ML_HEADS, nseq // SAMPLE_TILE, SAMPLE_TILE
    rows = nseq * t_new
    lane = lax.broadcasted_iota(jnp.int32, (rows, LANES), 1)
    lo, hi = lane < nh, (lane >= nh) & (lane < 2 * nh)
    tiles = lambda a: a.reshape(nt, t_new, st, a.shape[-1])
    flat = lambda a: a.reshape(rows, a.shape[-1])
    per_seq = lambda a: a.reshape(nseq, a.shape[-1])

    g = g_ref[...] + bias_ref[...]
    lf = tiles(jnp.where(hi, _log_sigmoid(g), 0.0))
    ig = tiles(jnp.where(hi, pltpu.roll(g, nh, 1), 0.0))
    m_prev = m0_ref[...].reshape(nt, st, LANES)
    bs, cs, ms = [], [], []
    b_run, m_run = None, m_prev
    for t in range(t_new):
        b_run = lf[:, t] if b_run is None else b_run + lf[:, t]
        c_t = ig[:, t] - b_run
        m_run = jnp.maximum(m_run, c_t)
        bs.append(b_run)
        cs.append(c_t)
        ms.append(m_run)
    stack = lambda xs: flat(jnp.stack(xs, axis=1))
    b, c, mx = stack(bs), stack(cs), stack(ms)
    m_prev_rows = stack([m_prev] * t_new)
    m_row = b + mx
    m_ref[...] = per_seq(bs[-1] + ms[-1])
    mx_lo = pltpu.roll(mx, LANES - nh, 1)
    w_inter = jnp.exp(pltpu.roll(m_prev_rows, LANES - nh, 1) - mx_lo)
    xc = _pieces(jnp.where(lo, w_inter, jnp.where(hi, _den_floor(m_row), 0.0)))
    lc = _pieces(jnp.where(lo, -mx_lo, jnp.where(hi, 1.0, 0.0)))
    rc = _pieces(jnp.where(lo, 1.0, jnp.where(hi, c, 0.0)))

    def seq_of(idx):
        return (idx // (t_new * st)) * st + idx % st, (idx % (t_new * st)) // st

    rt = lax.broadcasted_iota(jnp.int32, (rows, rows), 0)
    ct = lax.broadcasted_iota(jnp.int32, (rows, rows), 1)
    (rs, rtok), (cseq, ctok) = seq_of(rt), seq_of(ct)
    valid = (rs == cseq) & (ctok <= rtok)
    rq = lax.broadcasted_iota(jnp.int32, (rows, nseq * ML_DK), 0)
    cq = lax.broadcasted_iota(jnp.int32, (rows, nseq * ML_DK), 1)
    own_q = seq_of(rq)[0] == cq // ML_DK
    rk = lax.broadcasted_iota(jnp.int32, (nseq * ML_DK, rows), 0)
    ck = lax.broadcasted_iota(jnp.int32, (nseq * ML_DK, rows), 1)
    own_k = rk // ML_DK == seq_of(ck)[0]
    ones = jnp.ones((rows, ML_DV), F32)
    last = lambda a: per_seq(tiles(a)[:, t_new - 1])

    for hd in range(nh):
        cols = slice(hd * ML_DV, (hd + 1) * ML_DV)
        rh = rc * mask_ref[hd:hd + 1, :].astype(BF16)
        dmat = lax.dot_general(lc, rh, (((1,), (1,)), ((), ())), preferred_element_type=F32)
        wb = jnp.dot(xc, sel_ref[hd], preferred_element_type=F32)
        qh = q_ref[:, hd * ML_DK:(hd + 1) * ML_DK]
        kt = kt_ref[hd * ML_DK:(hd + 1) * ML_DK, :]
        ktb = kt.astype(BF16)
        w = jnp.exp(jnp.where(valid, dmat, NEG_INF))
        qk = jnp.dot(qh.astype(BF16), ktb, preferred_element_type=F32) * w
        vaug = jnp.concatenate([v_ref[:, cols], ones], axis=1).astype(BF16)
        po = jnp.dot(qk.astype(BF16), vaug, preferred_element_type=F32)
        wq = wb[:, :ML_DK] * qh
        wq2 = jnp.concatenate([wq, wq], axis=1)
        wq_bd = jnp.where(own_q, jnp.concatenate([wq2] * (nseq * ML_DK // LANES), axis=1), 0.0).astype(BF16)
        cstack = c0_ref[:, hd].reshape(nseq * ML_DK, ML_DV)
        num = po[:, :ML_DV] + jnp.dot(wq_bd, cstack.astype(BF16), preferred_element_type=F32)
        n0 = n0_ref[hd]
        n_rows = stack([n0.reshape(nt, st, ML_DK)] * t_new)
        den = po[:, ML_DV:] + jnp.sum(wq * n_rows, axis=-1, keepdims=True)
        h = num / jnp.maximum(jnp.abs(den), wb[:, ML_DV:])
        h_ref[:, cols] = _head_out(h, o_ref[:, cols], gout_ref[:, cols])
        w_last = last(w)
        decay = last(wb[:, :ML_DV])
        n_upd = lax.dot_general(w_last.astype(BF16), ktb, (((1,), (1,)), ((), ())), preferred_element_type=F32)
        n_ref[hd] = decay[:, :ML_DK] * n0 + n_upd
        wk = jnp.sum(w_last, axis=0, keepdims=True)
        kw_bd = jnp.where(own_k, jnp.concatenate([kt * wk] * nseq, axis=0), 0.0).astype(BF16)
        upd = jnp.dot(kw_bd, v_ref[:, cols].astype(BF16), preferred_element_type=F32)
        decay_rows = jnp.broadcast_to(decay[:, None, :], (nseq, ML_DK, ML_DV)).reshape(nseq * ML_DK, ML_DV)
        c_ref[:, hd] = (decay_rows * cstack + upd).reshape(nseq, ML_DK, ML_DV)


def _mlstm_sample(proj, kt, bias, gout, consts, c0, n0h, m0, t_new, layer, prev):
    n = proj.shape[0]
    bsz = n // t_new
    nseq = MLS_SEQS
    rows = nseq * t_new
    mask, sel = consts
    blk = lambda w, j: pl.BlockSpec((rows, w), lambda i: (i, j))
    cst = pl.BlockSpec((None, nseq, ML_HEADS, ML_DK, ML_DV), lambda i: (layer, i, 0, 0, 0))
    nst = pl.BlockSpec((None, ML_HEADS, nseq, ML_DK), lambda i: (layer, 0, i, 0))
    mst = pl.BlockSpec((None, nseq, LANES), lambda i: (layer, i, 0))
    prev, prev_specs, aliases = _alias_inputs(prev, 1)
    return pl.pallas_call(
        _skip_refs(functools.partial(_mlstm_sample_kernel, nseq=nseq, t_new=t_new), len(prev)),
        grid=(bsz // nseq,),
        in_specs=prev_specs + [
            blk(ML_WIDTH, ODDP_V_BLOCK), blk(ML_WIDTH, ODDP_O_BLOCK), blk(ML_QK, ODDP_Q_BLOCK),
            blk(LANES, ODDP_G_BLOCK), pl.BlockSpec((ML_QK, rows), lambda i: (0, i)),
            _layer_spec((1, LANES), layer), _layer_spec((1, ML_WIDTH), layer),
            _const_spec(mask.shape), _const_spec(sel.shape), cst, nst, mst],
        out_specs=[blk(ML_WIDTH, 0), cst, nst, mst],
        out_shape=[jax.ShapeDtypeStruct((n, ML_WIDTH), BF16),
                   jax.ShapeDtypeStruct(c0.shape, F32), jax.ShapeDtypeStruct(n0h.shape, F32),
                   jax.ShapeDtypeStruct(m0.shape, F32)],
        input_output_aliases=aliases,
        compiler_params=_cparams("parallel"),
        name="mlstm_sample",
    )(*prev, proj, proj, proj, proj, kt, bias, gout, mask, sel, c0, n0h, m0)


def _pad_lanes(x):
    return jnp.pad(x, [(0, 0)] * (x.ndim - 1) + [(0, LANES - x.shape[-1])])


def kernel(x_prompt, x_sample, cache_k, cache_v, state_ssm_re, state_ssm_im, state_mlstm_c, state_mlstm_n, state_mlstm_m, norm_mix, norm_ffn, w_in_even, q_norm, k_norm, attn_sinks, s5_a_re, s5_a_im, s5_log_dt, s5_b_re, s5_b_im, s5_c_re, s5_c_im, s5_d, s5_w_glu, s5_b_glu, w_out_even, w_in_odd, ml_b_i, ml_b_f, ml_out_norm, w_out_odd, w_gate, w_up, w_down):
    bp, lp, _ = x_prompt.shape
    bsm, ls, _ = x_sample.shape
    yp = x_prompt.reshape(bp * lp, D_MODEL)
    ys = x_sample.reshape(bsm // SAMPLE_TILE, SAMPLE_TILE, ls, D_MODEL).transpose(0, 2, 1, 3).reshape(bsm * ls, D_MODEL)
    tab_p = _rope_tables(jnp.arange(lp))
    tab_s = tuple(jnp.repeat(t, SAMPLE_TILE, axis=0) for t in _rope_tables(PAST_LEN + jnp.arange(ls)))
    n_even, n_odd = w_in_even.shape[0], w_in_odd.shape[0]

    g_mix = norm_mix.reshape(DEPTH, 1, D_MODEL)
    g_ffn = norm_ffn.reshape(DEPTH, 1, D_MODEL)
    wg, wu, wd = w_gate.astype(BF16), w_up.astype(BF16), w_down.astype(BF16)
    kv0, u0 = ATTN_WIDTH, ATTN_WIDTH + 2 * KV_WIDTH
    order = jnp.asarray(ATTN_HEAD_ORDER)
    wq = w_in_even[..., :kv0].reshape(n_even, D_MODEL, ATTN_HEADS, HEAD_DIM)[:, :, order].reshape(n_even, D_MODEL, kv0)
    w_in_e = jnp.concatenate([wq, w_in_even[..., u0:], w_in_even[..., kv0:u0]], axis=-1).astype(BF16)
    wo_attn = w_out_even[:, :kv0].reshape(n_even, ATTN_HEADS, HEAD_DIM, D_MODEL)[:, order].reshape(n_even, kv0, D_MODEL)
    w_out_e = jnp.concatenate([wo_attn, w_out_even[:, kv0:]], axis=1).astype(BF16)
    gq = jnp.tile(q_norm, (1, LANES // HEAD_DIM)).reshape(n_even, 1, LANES)
    gk = jnp.tile(k_norm, (1, LANES // HEAD_DIM)).reshape(n_even, 1, LANES)
    prm = _s5_params(s5_a_re, s5_a_im, s5_log_dt, s5_b_re, s5_b_im, s5_c_re, s5_c_im, s5_d, s5_w_glu, s5_b_glu)
    w_gates_o = _pad_lanes(w_in_odd[..., 2 * ML_QK + 2 * ML_WIDTH:]).astype(BF16)
    w_in_ot = jnp.swapaxes(w_in_odd, 1, 2)
    ml_consts = _ml_select_constants()
    w_out_o = w_out_odd.astype(BF16)
    ml_bias = _pad_lanes(jnp.concatenate([ml_b_i, ml_b_f], axis=-1)).reshape(n_odd, 1, LANES)
    ml_gout = ml_out_norm.reshape(n_odd, 1, ML_WIDTH)
    keys_last = lambda a: a.transpose(0, 1, 3, 4, 2).reshape(n_even, bsm, KV_WIDTH, WINDOW)
    ck, cv = keys_last(cache_k), keys_last(cache_v)
    h0r = state_ssm_re.reshape(n_even, bsm, S5_FLAT)
    h0i = state_ssm_im.reshape(n_even, bsm, S5_FLAT)
    n0h = jnp.swapaxes(state_mlstm_n, 1, 2)
    m0 = jnp.pad(state_mlstm_m, ((0, 0), (0, 0), (ML_HEADS, LANES - 2 * ML_HEADS)))

    p_attn = p_ssm = p_ml = s_attn = s_ssm = s_ml = None
    for layer in range(DEPTH):
        ffn = (layer, g_ffn, wg, wu, wd)
        if layer % 2 == 0:
            e = layer // 2
            proj_p, proj_s = _norm_matmul([yp, ys], g_mix, layer, w_in_e, e)
            proj3 = proj_p.reshape(bp, lp, -1)
            attn_p, *p_attn = _attn_prompt(proj3, tab_p, gq, gk, attn_sinks, e, p_attn)
            ssm_p, *p_ssm = _s5_prompt(proj3, prm, e, p_ssm)
            attn_s, *s_attn = _attn_sample(proj_s, ck, cv, tab_s, gq, gk, attn_sinks, ls, e, s_attn)
            ssm_s, *s_ssm = _s5_sample(proj_s, h0r, h0i, prm, ls, e, s_ssm)
            yp, ys = _mix_ffn([[yp, attn_p.reshape(bp * lp, -1), ssm_p.reshape(bp * lp, -1)], [ys, attn_s, ssm_s]],
                              w_out_e, e, *ffn)
        else:
            o = layer // 2
            (proj_p, kt_p), (proj_s, kt_s) = _norm_matmul_kt([yp, ys], g_mix, layer, w_in_ot, w_gates_o, o)
            hh_p, *p_ml = _mlstm_prompt(proj_p.reshape(bp, lp, -1), kt_p, ml_bias, ml_gout, ml_consts, o, p_ml)
            hh_s, *s_ml = _mlstm_sample(proj_s, kt_s, ml_bias, ml_gout, ml_consts, state_mlstm_c, n0h, m0, ls, o,
                                        s_ml)
            yp, ys = _mix_ffn([[yp, hh_p.reshape(bp * lp, -1)], [ys, hh_s]], w_out_o, o, *ffn)
    heads = lambda a: a.reshape(a.shape[:3] + (KV_HEADS, HEAD_DIM))
    groups = lambda a: a.reshape(a.shape[:2] + (S5_GROUPS, S5_STATE))
    keys_first = lambda a: a.reshape(a.shape[:2] + (KV_HEADS, HEAD_DIM, WINDOW)).transpose(0, 1, 4, 2, 3)
    ys = ys.reshape(bsm // SAMPLE_TILE, ls, SAMPLE_TILE, D_MODEL).transpose(0, 2, 1, 3).reshape(bsm, ls, D_MODEL)
    return (yp.reshape(bp, lp, D_MODEL), ys,
            heads(p_attn[0]), heads(p_attn[1]), groups(p_ssm[0]), groups(p_ssm[1]),
            p_ml[0], p_ml[1][..., 0], p_ml[2][:, :, 0, ML_HEADS:2 * ML_HEADS],
            keys_first(s_attn[0]), keys_first(s_attn[1]), groups(s_ssm[0]), groups(s_ssm[1]),
            s_ml[0], jnp.swapaxes(s_ml[1], 1, 2), s_ml[2][..., ML_HEADS:2 * ML_HEADS])
```

```python
import functools

import numpy as np

import jax
import jax.numpy as jnp
from jax import lax
from jax.experimental import pallas as pl
from jax.experimental.pallas import tpu as pltpu

F32 = jnp.float32
BF16 = jnp.bfloat16

D_MODEL = 1024
DEPTH = 4
PAST_LEN = 8192
WINDOW = 128
ATTN_HEADS = 8
KV_HEADS = 2
HEAD_DIM = 64
ATTN_WIDTH = ATTN_HEADS * HEAD_DIM
KV_WIDTH = KV_HEADS * HEAD_DIM
ROT_DIM = HEAD_DIM // 4
ROPE_THETA = 500000.0
S5_GROUP = 16
S5_WIDTH = D_MODEL // 2
S5_GROUPS = S5_WIDTH // S5_GROUP
S5_STATE = 64
S5_FLAT = S5_GROUPS * S5_STATE
ML_HEADS = 8
ML_DV = D_MODEL // ML_HEADS
ML_DK = ML_DV // 2
ML_QK = ML_HEADS * ML_DK
ML_WIDTH = ML_HEADS * ML_DV
D_FF = 2816
EPS = 1e-6

LANES = 128
SUBLANES = 8
ROW_TILE = 512
FF_TILE = 256
S5_CHUNK = 128
ML_CHUNK = 128
SAMPLE_TILE = SUBLANES
VMEM_LIMIT = 56 * 1024 * 1024

NEG_INF = float("-inf")


def _cparams(*sem):
    return pltpu.CompilerParams(dimension_semantics=sem, vmem_limit_bytes=VMEM_LIMIT)


def _const_spec(shape):
    zeros = (0,) * len(shape)
    return pl.BlockSpec(shape, lambda *_: zeros, pipeline_mode=pl.Buffered(1))


def _layer_spec(shape, layer):
    zeros = (0,) * len(shape)
    return pl.BlockSpec((None,) + tuple(shape), lambda *_: (layer,) + zeros, pipeline_mode=pl.Buffered(1))


def _skip_refs(body, n_skip):
    if n_skip == 0:
        return body

    def wrapped(*refs):
        return body(*refs[n_skip:])

    return wrapped


def _alias_inputs(prev, first_state_out):
    prev = () if prev is None else tuple(prev)
    specs = [pl.BlockSpec(memory_space=pl.ANY) for _ in prev]
    aliases = {i: first_state_out + i for i in range(len(prev))}
    return prev, specs, aliases


def _rms(x, g):
    ms = jnp.mean(x * x, axis=-1, keepdims=True)
    return x * lax.rsqrt(ms + EPS) * g


def _split3(a):
    a1 = a.astype(BF16)
    r1 = a - a1.astype(F32)
    a2 = r1.astype(BF16)
    a3 = (r1 - a2.astype(F32)).astype(BF16)
    return a1, a2, a3


def _log_sigmoid(x):
    return jnp.minimum(x, 0.0) - jnp.log(1.0 + jnp.exp(-jnp.abs(x)))


def _sigmoid(x):
    return 1.0 / (1.0 + jnp.exp(-x))


def _norm_matmul_kernel(x_ref, g_ref, w_ref, o_ref):
    h = _rms(x_ref[...], g_ref[...]).astype(BF16)
    o_ref[...] = jnp.dot(h, w_ref[...], preferred_element_type=F32)


def _row_groups_call(body, groups, consts, const_specs, out_defs, scratch_shapes, name):
    steps, tiles = [], []
    for arrays in groups:
        n = arrays[0].shape[0]
        tm = min(ROW_TILE, n)
        tiles.append(tm)
        steps.append(n // tm)
    offs = [sum(steps[:k]) for k in range(len(groups))]

    def local(k):
        return lambda i: jnp.clip(i - offs[k], 0, steps[k] - 1)

    in_specs, out_specs, out_shape, args = [], [], [], []
    for k, arrays in enumerate(groups):
        for a in arrays:
            in_specs.append(pl.BlockSpec((tiles[k], a.shape[1]), lambda i, f=local(k): (f(i), 0)))
            args.append(a)
    for k, arrays in enumerate(groups):
        n = arrays[0].shape[0]
        for width, dtype, by_rows in out_defs:
            if by_rows:
                out_specs.append(pl.BlockSpec((tiles[k], width), lambda i, f=local(k): (f(i), 0)))
                out_shape.append(jax.ShapeDtypeStruct((n, width), dtype))
            else:
                out_specs.append(pl.BlockSpec((width, tiles[k]), lambda i, f=local(k): (0, f(i))))
                out_shape.append(jax.ShapeDtypeStruct((width, n), dtype))
    n_in = [len(arrays) for arrays in groups]
    n_out = len(out_defs)

    def kern(*refs):
        i = pl.program_id(0)
        pos = 0
        ins = []
        for cnt in n_in:
            ins.append(refs[pos:pos + cnt])
            pos += cnt
        crefs = refs[pos:pos + len(consts)]
        pos += len(consts)
        outs = [refs[pos + k * n_out:pos + (k + 1) * n_out] for k in range(len(groups))]
        scratch = refs[pos + len(groups) * n_out:]
        for k in range(len(groups)):
            @pl.when((i >= offs[k]) & (i < offs[k] + steps[k]))
            def _(k=k):
                body(*ins[k], *crefs, *outs[k], *scratch)

    res = pl.pallas_call(
        kern,
        grid=(sum(steps),),
        in_specs=in_specs + list(const_specs),
        out_specs=out_specs,
        out_shape=out_shape,
        scratch_shapes=scratch_shapes,
        compiler_params=_cparams("arbitrary"),
        name=name,
    )(*args, *consts)
    return [res[k * n_out:(k + 1) * n_out] for k in range(len(groups))]


RING_SLOTS = 3


def _ring_proj_call(body, x_main, x_tail, consts, const_specs, out_defs, name):
    n, d = x_main.shape
    tm = ROW_TILE
    steps = n // tm
    n_tail = x_tail.shape[0]
    assert n % tm == 0 and steps >= RING_SLOTS - 1 and n_tail <= tm

    def main_idx(i):
        return jnp.minimum(i, steps - 1)

    out_specs, out_shape = [], []
    for rows, idx in ((n, main_idx), (n_tail, lambda i: 0)):
        tile = min(tm, rows)
        for width, dtype, by_rows in out_defs:
            if by_rows:
                out_specs.append(pl.BlockSpec((tile, width), lambda i, f=idx: (f(i), 0)))
                out_shape.append(jax.ShapeDtypeStruct((rows, width), dtype))
            else:
                out_specs.append(pl.BlockSpec((width, tile), lambda i, f=idx: (0, f(i))))
                out_shape.append(jax.ShapeDtypeStruct((width, rows), dtype))
    n_out = len(out_defs)

    def kern(x_hbm, xt_ref, *refs):
        crefs = refs[:len(consts)]
        outs_main = refs[len(consts):len(consts) + n_out]
        outs_tail = refs[len(consts) + n_out:len(consts) + 2 * n_out]
        xbuf, sem = refs[len(consts) + 2 * n_out:]
        i = pl.program_id(0)

        def tile_copy(j, slot):
            rows = pl.ds(pl.multiple_of(j * tm, tm), tm)
            return pltpu.make_async_copy(x_hbm.at[rows, :], xbuf.at[slot], sem.at[slot])

        @pl.when(i == 0)
        def _():
            for j in range(RING_SLOTS - 1):
                tile_copy(j, j).start()

        @pl.when(i + RING_SLOTS - 1 < steps)
        def _():
            nxt = i + RING_SLOTS - 1
            tile_copy(nxt, nxt % RING_SLOTS).start()

        @pl.when(i < steps)
        def _():
            slot = i % RING_SLOTS
            tile_copy(i, slot).wait()
            body(xbuf.at[slot], *crefs, *outs_main)

        @pl.when(i >= steps)
        def _():
            body(xt_ref, *crefs, *outs_tail)

    res = pl.pallas_call(
        kern,
        grid=(steps + 1,),
        in_specs=[pl.BlockSpec(memory_space=pl.ANY), _const_spec((n_tail, d))] + list(const_specs),
        out_specs=out_specs,
        out_shape=out_shape,
        scratch_shapes=[pltpu.VMEM((RING_SLOTS, tm, d), F32), pltpu.SemaphoreType.DMA((RING_SLOTS,))],
        compiler_params=_cparams("arbitrary"),
        name=name,
    )(x_main, x_tail, *consts)
    return [res[:n_out], res[n_out:]]


def _norm_matmul(xs, g, layer, w, widx):
    d, m = w.shape[1], w.shape[2]
    res = _ring_proj_call(_norm_matmul_kernel, xs[0], xs[1], [g, w],
                          [_layer_spec((1, d), layer), _layer_spec((d, m), widx)],
                          [(m, F32, True)], "norm_matmul")
    return [r[0] for r in res]


def _mix_ffn_kernel(*refs, n_mix):
    x_ref = refs[0]
    a_refs = refs[1:1 + n_mix]
    wo_ref, g_ref, wg_ref, wu_ref, wd_ref, o_ref, act_ref = refs[1 + n_mix:]
    y = x_ref[...]
    off = 0
    for a_ref in a_refs:
        ka = a_ref.shape[1]
        y = y + jnp.dot(a_ref[...], wo_ref[off:off + ka, :], preferred_element_type=F32)
        off += ka
    h = _rms(y, g_ref[...]).astype(BF16)
    for f in range(D_FF // FF_TILE):
        cols = slice(f * FF_TILE, (f + 1) * FF_TILE)
        gate = jnp.dot(h, wg_ref[:, cols], preferred_element_type=F32)
        up = jnp.dot(h, wu_ref[:, cols], preferred_element_type=F32)
        act_ref[:, cols] = (gate * _sigmoid(gate) * up).astype(BF16)
    o_ref[...] = y + jnp.dot(act_ref[...], wd_ref[...], preferred_element_type=F32)


def _mix_ffn(groups, w_out, oidx, layer, g_ffn, wg, wu, wd):
    d = w_out.shape[2]
    n_mix = len(groups[0]) - 1
    tm = min(ROW_TILE, max(g[0].shape[0] for g in groups))
    res = _row_groups_call(
        functools.partial(_mix_ffn_kernel, n_mix=n_mix), groups, [w_out, g_ffn, wg, wu, wd],
        [_layer_spec(w_out.shape[1:], oidx), _layer_spec((1, d), layer), _layer_spec(wg.shape[1:], layer),
         _layer_spec(wu.shape[1:], layer), _layer_spec(wd.shape[1:], layer)],
        [(d, F32, True)], [pltpu.VMEM((tm, D_FF), BF16)], "mix_ffn")
    return [r[0] for r in res]


def _head_ones():
    r = lax.broadcasted_iota(jnp.int32, (LANES, LANES), 0) // HEAD_DIM
    c = lax.broadcasted_iota(jnp.int32, (LANES, LANES), 1) // HEAD_DIM
    return jnp.where(r == c, 1.0, 0.0).astype(BF16)


def _qk_prep(x, g, ones, ct, sa, sb):
    x2 = x * x
    hi = x2.astype(BF16)
    lo = (x2 - hi.astype(F32)).astype(BF16)
    ss = jnp.dot(hi, ones, preferred_element_type=F32) + jnp.dot(lo, ones, preferred_element_type=F32)
    xn = x * lax.rsqrt(ss * (1.0 / HEAD_DIM) + EPS) * g
    return xn * ct + pltpu.roll(xn, LANES - ROT_DIM // 2, 1) * sa + pltpu.roll(xn, ROT_DIM // 2, 1) * sb


def _rope_tables(pos):
    half = ROT_DIM // 2
    inv = jnp.power(jnp.float32(ROPE_THETA), -jnp.arange(half, dtype=F32) / half)
    ang = pos.astype(F32)[:, None] * inv[None, :]
    cos, sin = jnp.cos(ang), jnp.sin(ang)
    n = pos.shape[0]
    one = jnp.ones((n, HEAD_DIM - ROT_DIM), F32)
    zero = jnp.zeros((n, HEAD_DIM - ROT_DIM), F32)
    z8 = jnp.zeros((n, half), F32)
    ct = jnp.concatenate([cos, cos, one], axis=1)
    sa = jnp.concatenate([-sin, z8, zero], axis=1)
    sb = jnp.concatenate([z8, sin, zero], axis=1)
    tile = lambda t: jnp.concatenate([t, t], axis=1)
    return tile(ct), tile(sa), tile(sb)


ATTN_SEQS = 4
ATTN_QCHUNKS = ATTN_WIDTH // LANES
ATTN_HEAD_ORDER = tuple(h * ATTN_QCHUNKS + j for j in range(ATTN_QCHUNKS) for h in range(KV_HEADS))


def _attn_prompt_kernel(q_ref, kv_ref, ct_ref, sa_ref, sb_ref, gq_ref, gk_ref, sink_ref,
                        o_ref, pk_ref, pv_ref, kprev, vprev, *, nb, layer, nseq):
    i = pl.program_id(1)

    @pl.when(i == 0)
    def _():
        kprev[...] = jnp.zeros_like(kprev)
        vprev[...] = jnp.zeros_like(vprev)

    ones = _head_ones()
    ct, sa, sb = ct_ref[...], sa_ref[...], sb_ref[...]
    r = lax.broadcasted_iota(jnp.int32, (WINDOW, 2 * WINDOW), 0)
    c = lax.broadcasted_iota(jnp.int32, (WINDOW, 2 * WINDOW), 1)
    rel = r + WINDOW - c
    mask = (rel >= 0) & (rel <= WINDOW) & ((c >= WINDOW) | (i > 0))
    lane = lax.broadcasted_iota(jnp.int32, (WINDOW, LANES), 1)
    group0 = lane < HEAD_DIM
    v_ones = jnp.ones((2 * WINDOW, LANES), BF16)
    nq = ATTN_QCHUNKS
    st = [dict() for _ in range(nseq)]

    def prep(sq):
        d = st[sq]
        kv = kv_ref[sq]
        d["kn"] = _qk_prep(kv[:, :KV_WIDTH], gk_ref[...], ones, ct, sa, sb)
        d["v"] = kv[:, KV_WIDTH:]
        d["qn"] = [_qk_prep(q_ref[sq, :, j * LANES:(j + 1) * LANES], gq_ref[...], ones, ct, sa, sb)
                   * (HEAD_DIM ** -0.5) for j in range(nq)]
        d["kcat"] = jnp.concatenate([kprev[sq], d["kn"]], axis=0).astype(BF16)
        d["vaug"] = jnp.concatenate([jnp.concatenate([vprev[sq], d["v"]], axis=0).astype(BF16), v_ones], axis=1)
        kprev[sq] = d["kn"]
        vprev[sq] = d["v"]

    def scores(sq, h):
        d = st[sq]
        keep = group0 if h == 0 else jnp.logical_not(group0)
        qs = jnp.concatenate([jnp.where(keep, qj, 0.0) for qj in d["qn"]], axis=0).astype(BF16)
        d["s", h] = lax.dot_general(qs, d["kcat"], (((1,), (1,)), ((), ())), preferred_element_type=F32)

    def softmax_pv(sq, h):
        d = st[sq]
        s = d.pop(("s", h))
        ps, corr = [], []
        for j in range(nq):
            sg = jnp.where(mask, s[j * WINDOW:(j + 1) * WINDOW], NEG_INF)
            sink = sink_ref[layer, h * nq + j]
            m = jnp.maximum(jnp.max(sg, axis=-1, keepdims=True), sink)
            ps.append(jnp.exp(sg - m).astype(BF16))
            corr.append(jnp.exp(sink - m))
        o = jnp.dot(jnp.concatenate(ps, axis=0), d["vaug"], preferred_element_type=F32)
        d["o", h] = [o[j * WINDOW:(j + 1) * WINDOW, :LANES] / (o[j * WINDOW:(j + 1) * WINDOW, LANES:] + corr[j])
                     for j in range(nq)]

    def finish(sq):
        d = st[sq]
        o_ref[sq] = jnp.concatenate([jnp.where(group0, d["o", 0][j], d["o", 1][j]) for j in range(nq)],
                                    axis=1).astype(BF16)

    for sq in range(nseq):
        prep(sq)
    for sq in range(nseq):
        scores(sq, 0)
        scores(sq, 1)
    for sq in range(nseq):
        softmax_pv(sq, 0)
        softmax_pv(sq, 1)
        finish(sq)

    @pl.when(i == nb - 1)
    def _():
        for sq in range(nseq):
            pk_ref[sq] = st[sq]["kn"]
            pv_ref[sq] = st[sq]["v"]


def _attn_prompt(proj, tables, gq, gk, sinks, layer, prev):
    bsz, seq, _ = proj.shape
    nb = seq // WINDOW
    nseq = ATTN_SEQS
    n_layers = gq.shape[0]
    tab = pl.BlockSpec((WINDOW, LANES), lambda b, i: (i, 0))
    prev, prev_specs, aliases = _alias_inputs(prev, 1)
    win = pl.BlockSpec((None, nseq, WINDOW, KV_WIDTH), lambda b, i: (layer, b, 0, 0))
    win_shape = jax.ShapeDtypeStruct((n_layers, bsz, WINDOW, KV_WIDTH), F32)
    return pl.pallas_call(
        _skip_refs(functools.partial(_attn_prompt_kernel, nb=nb, layer=layer, nseq=nseq), len(prev)),
        grid=(bsz // nseq, nb),
        in_specs=prev_specs + [
            pl.BlockSpec((nseq, WINDOW, ATTN_WIDTH), lambda b, i: (b, i, 0)),
            pl.BlockSpec((nseq, WINDOW, 2 * KV_WIDTH), lambda b, i: (b, i, EVEN_KV_BLOCK)),
            tab, tab, tab, _layer_spec((1, LANES), layer), _layer_spec((1, LANES), layer),
            pl.BlockSpec(memory_space=pltpu.SMEM)],
        out_specs=[pl.BlockSpec((nseq, WINDOW, ATTN_WIDTH), lambda b, i: (b, i, 0)), win, win],
        out_shape=[jax.ShapeDtypeStruct((bsz, seq, ATTN_WIDTH), BF16), win_shape, win_shape],
        input_output_aliases=aliases,
        scratch_shapes=[pltpu.VMEM((nseq, WINDOW, KV_WIDTH), F32), pltpu.VMEM((nseq, WINDOW, KV_WIDTH), F32)],
        compiler_params=_cparams("parallel", "arbitrary"),
        name="attn_prompt",
    )(*prev, proj, proj, *tables, gq, gk, sinks)


EVEN_U_BLOCK = ATTN_WIDTH // S5_WIDTH
EVEN_KV_BLOCK = (ATTN_WIDTH + S5_WIDTH) // (2 * KV_WIDTH)
KALL_ROWS = WINDOW + SUBLANES


def _attn_sample_kernel(q_ref, kv_ref, ck_ref, cv_ref, ct_ref, sa_ref, sb_ref, gq_ref, gk_ref, sink_ref,
                        o_ref, nk_ref, nv_ref, o_seq, *, bs, t_new, layer):
    ones = _head_ones()
    ct, sa, sb = ct_ref[...], sa_ref[...], sb_ref[...]
    kv = kv_ref[...]
    kn = _qk_prep(kv[:, :KV_WIDTH], gk_ref[...], ones, ct, sa, sb)
    v = kv[:, KV_WIDTH:]
    nq = ATTN_QCHUNKS
    qn = [_qk_prep(q_ref[:, j * LANES:(j + 1) * LANES], gq_ref[...], ones, ct, sa, sb) * (HEAD_DIM ** -0.5)
          for j in range(nq)]
    rows = nq * t_new
    r = lax.broadcasted_iota(jnp.int32, (rows, KALL_ROWS), 0)
    c = lax.broadcasted_iota(jnp.int32, (rows, KALL_ROWS), 1)
    t = r % t_new
    mask = (c >= t) & (c <= t + WINDOW)
    rj = lax.broadcasted_iota(jnp.int32, (rows, 1), 0) // t_new
    lane = lax.broadcasted_iota(jnp.int32, (t_new, LANES), 1)
    group0 = lane < HEAD_DIM
    pad = jnp.zeros((KALL_ROWS - WINDOW - t_new, KV_WIDTH), F32)
    ones_c = jnp.ones((LANES, WINDOW), BF16)
    ones_n = jnp.ones((KALL_ROWS - WINDOW, LANES), BF16)
    klane = lax.broadcasted_iota(jnp.int32, (KV_WIDTH, WINDOW), 1)
    zcols = jnp.zeros((KV_WIDTH, WINDOW - (KALL_ROWS - WINDOW)), F32)
    nt = (((1,), (1,)), ((), ()))

    def shifted(cache_t, new_rows):
        new_t = jnp.concatenate([jnp.concatenate([new_rows, pad], axis=0).T, zcols], axis=1)
        return jnp.where(klane >= WINDOW - t_new, pltpu.roll(new_t, WINDOW - t_new, 1),
                         pltpu.roll(cache_t, WINDOW - t_new, 1))

    def seq_rows(a, b):
        return jnp.concatenate([a[tt * bs + b:tt * bs + b + 1] for tt in range(t_new)], axis=0)

    sinks = []
    for h in range(KV_HEADS):
        sk = jnp.zeros((rows, 1), F32)
        for j in range(nq):
            sk = jnp.where(rj == j, sink_ref[layer, h * nq + j], sk)
        sinks.append(sk)

    st = [dict() for _ in range(bs)]
    for b in range(bs):
        d = st[b]
        ck, cv = ck_ref[b], cv_ref[b]
        kn_b, v_b = seq_rows(kn, b), seq_rows(v, b)
        nk_ref[b] = shifted(ck, kn_b)
        nv_ref[b] = shifted(cv, v_b)
        ckb = ck.astype(BF16)
        knb = jnp.concatenate([kn_b, pad], axis=0).astype(BF16)
        d["vc"] = jnp.concatenate([cv.astype(BF16), ones_c], axis=0)
        d["vn"] = jnp.concatenate([jnp.concatenate([v_b, pad], axis=0).astype(BF16), ones_n], axis=1)
        qb = [seq_rows(qj, b) for qj in qn]
        for h in range(KV_HEADS):
            keep = group0 if h == 0 else jnp.logical_not(group0)
            qs = jnp.concatenate([jnp.where(keep, q, 0.0) for q in qb], axis=0).astype(BF16)
            d["s", h] = jnp.concatenate([jnp.dot(qs, ckb, preferred_element_type=F32),
                                         lax.dot_general(qs, knb, nt, preferred_element_type=F32)], axis=1)
    for b in range(bs):
        d = st[b]
        for h in range(KV_HEADS):
            s = jnp.where(mask, d.pop(("s", h)), NEG_INF)
            m = jnp.maximum(jnp.max(s, axis=-1, keepdims=True), sinks[h])
            p = jnp.exp(s - m).astype(BF16)
            o = (lax.dot_general(p[:, :WINDOW], d["vc"], nt, preferred_element_type=F32)
                 + jnp.dot(p[:, WINDOW:], d["vn"], preferred_element_type=F32))
            d["o", h] = o[:, :LANES] / (o[:, LANES:] + jnp.exp(sinks[h] - m))
    for b in range(bs):
        d = st[b]
        o_b = jnp.concatenate([jnp.where(group0, d["o", 0][j * t_new:(j + 1) * t_new],
                                         d["o", 1][j * t_new:(j + 1) * t_new]) for j in range(nq)], axis=1)
        for tt in range(t_new):
            o_seq[tt * bs + b:tt * bs + b + 1, :] = o_b[tt:tt + 1]
    o_ref[...] = o_seq[...].astype(BF16)


def _attn_sample(proj, cache_k, cache_v, tables, gq, gk, sinks, t_new, layer, prev):
    n = proj.shape[0]
    bsz = n // t_new
    bs = SAMPLE_TILE
    rows = bs * t_new
    row = lambda i: (i, 0)
    cache = pl.BlockSpec((None, bs, KV_WIDTH, WINDOW), lambda i: (layer, i, 0, 0))
    prev, prev_specs, aliases = _alias_inputs(prev, 1)
    return pl.pallas_call(
        _skip_refs(functools.partial(_attn_sample_kernel, bs=bs, t_new=t_new, layer=layer), len(prev)),
        grid=(bsz // bs,),
        in_specs=prev_specs + [
            pl.BlockSpec((rows, ATTN_WIDTH), row),
            pl.BlockSpec((rows, 2 * KV_WIDTH), lambda i: (i, EVEN_KV_BLOCK)),
            cache, cache,
            _const_spec((rows, LANES)), _const_spec((rows, LANES)), _const_spec((rows, LANES)),
            _layer_spec((1, LANES), layer), _layer_spec((1, LANES), layer),
            pl.BlockSpec(memory_space=pltpu.SMEM)],
        out_specs=[pl.BlockSpec((rows, ATTN_WIDTH), row), cache, cache],
        out_shape=[jax.ShapeDtypeStruct((n, ATTN_WIDTH), BF16),
                   jax.ShapeDtypeStruct(cache_k.shape, F32), jax.ShapeDtypeStruct(cache_v.shape, F32)],
        input_output_aliases=aliases,
        scratch_shapes=[pltpu.VMEM((rows, ATTN_WIDTH), F32)],
        compiler_params=_cparams("parallel"),
        name="attn_sample",
    )(*prev, proj, proj, cache_k, cache_v, *tables, gq, gk, sinks)


S5_UCHUNKS = S5_WIDTH // LANES
S5_SUB = S5_FLAT // S5_UCHUNKS
S5_SCHUNKS = S5_FLAT // LANES


def _s5_tail(y, wglu_ref, bglu_ref):
    g = 0.5 * y * (1.0 + lax.erf(y * (2.0 ** -0.5)))
    z = jnp.dot(g.astype(BF16), wglu_ref[...], preferred_element_type=F32) + bglu_ref[...]
    return g * _sigmoid(z)


S5_PARTS = 4


def _s5_prompt_kernel(u_ref, wb_ref, wc_ref, lam_ref, d_ref, wglu_ref, bglu_ref,
                      o_ref, sr_ref, si_ref, xs, hst, *, nbatch, tc):
    rows = nbatch * tc
    prow, ptok = rows // S5_PARTS, tc // S5_PARTS

    @pl.when(pl.program_id(1) == 0)
    def _():
        hst[...] = jnp.zeros_like(hst)

    u = jnp.swapaxes(u_ref[...], 0, 1).reshape(rows, S5_WIDTH)
    ub = u.astype(BF16)

    def in_proj(p, cc):
        rs = slice(p * prow, (p + 1) * prow)
        res = jnp.dot(ub[rs, cc * LANES:(cc + 1) * LANES], wb_ref[cc], preferred_element_type=F32)
        for j in range(S5_SUB // LANES):
            xs[cc * 4 + j, rs, :] = res[:, j * LANES:(j + 1) * LANES]
            xs[S5_SCHUNKS + cc * 4 + j, rs, :] = res[:, S5_SUB + j * LANES:S5_SUB + (j + 1) * LANES]

    ys = {}

    def out_proj(p, cc):
        rs = slice(p * prow, (p + 1) * prow)
        s = jnp.concatenate([xs[cc * 4 + j, rs, :] for j in range(4)]
                            + [xs[S5_SCHUNKS + cc * 4 + j, rs, :] for j in range(4)], axis=1).astype(BF16)
        cols = slice(cc * LANES, (cc + 1) * LANES)
        ys[p, cc] = jnp.dot(s, wc_ref[cc], preferred_element_type=F32) + d_ref[:, cols] * u[rs, cols]

    def tail(p):
        out = _s5_tail(jnp.concatenate([ys.pop((p, cc)) for cc in range(S5_UCHUNKS)], axis=1), wglu_ref, bglu_ref)
        o_ref[:, p * ptok:(p + 1) * ptok, :] = jnp.swapaxes(out.reshape(ptok, nbatch, S5_WIDTH), 0, 1).astype(BF16)

    def scan_step(t, h):
        idx = slice(t * nbatch, (t + 1) * nbatch)
        new = list(h)
        for k in range(S5_SCHUNKS):
            hr, hi = h[k], h[S5_SCHUNKS + k]
            lr, li = lam_ref[k], lam_ref[S5_SCHUNKS + k]
            nr = lr * hr - li * hi + xs[k, idx, :]
            ni = lr * hi + li * hr + xs[S5_SCHUNKS + k, idx, :]
            xs[k, idx, :] = nr
            xs[S5_SCHUNKS + k, idx, :] = ni
            new[k], new[S5_SCHUNKS + k] = nr, ni
        return new

    for cc in range(S5_UCHUNKS):
        in_proj(0, cc)
    h = [hst[k] for k in range(2 * S5_SCHUNKS)]
    for p in range(S5_PARTS):
        work = []
        if p + 1 < S5_PARTS:
            work += [functools.partial(in_proj, p + 1, cc) for cc in range(S5_UCHUNKS)]
        if p >= 1:
            work += [functools.partial(out_proj, p - 1, cc) for cc in range(S5_UCHUNKS)]
            work.append(functools.partial(tail, p - 1))
        every = max(1, ptok // max(1, len(work)))
        for i in range(ptok):
            h = scan_step(p * ptok + i, h)
            if work and (i + 1) % every == 0:
                work.pop(0)()
        for w in work:
            w()
    for cc in range(S5_UCHUNKS):
        out_proj(S5_PARTS - 1, cc)
    tail(S5_PARTS - 1)
    for k in range(2 * S5_SCHUNKS):
        hst[k] = h[k]
    sr_ref[...] = jnp.concatenate(h[:S5_SCHUNKS], axis=1)
    si_ref[...] = jnp.concatenate(h[S5_SCHUNKS:], axis=1)


def _s5_prompt(proj, prm, layer, prev):
    bsz, seq, _ = proj.shape
    nbatch, tc = SUBLANES, S5_CHUNK
    n_layers = prm["wb"].shape[0]
    st = pl.BlockSpec((None, nbatch, S5_FLAT), lambda b, c: (layer, b, 0))
    st_shape = jax.ShapeDtypeStruct((n_layers, bsz, S5_FLAT), F32)
    prev, prev_specs, aliases = _alias_inputs(prev, 1)
    names = ("wb", "wc", "lam8", "d", "wglu", "bglu")
    return pl.pallas_call(
        _skip_refs(functools.partial(_s5_prompt_kernel, nbatch=nbatch, tc=tc), len(prev)),
        grid=(bsz // nbatch, seq // tc),
        in_specs=prev_specs + [pl.BlockSpec((nbatch, tc, S5_WIDTH), lambda b, c: (b, c, EVEN_U_BLOCK))]
        + [_layer_spec(prm[k].shape[1:], layer) for k in names],
        out_specs=[pl.BlockSpec((nbatch, tc, S5_WIDTH), lambda b, c: (b, c, 0)), st, st],
        out_shape=[jax.ShapeDtypeStruct((bsz, seq, S5_WIDTH), BF16), st_shape, st_shape],
        input_output_aliases=aliases,
        scratch_shapes=[pltpu.VMEM((2 * S5_SCHUNKS, nbatch * tc, LANES), F32),
                        pltpu.VMEM((2 * S5_SCHUNKS, nbatch, LANES), F32)],
        compiler_params=_cparams("parallel", "arbitrary"),
        name="s5_prompt",
    )(*prev, proj, *[prm[k] for k in names])


def _s5_sample_kernel(u_ref, wb_ref, wc_ref, lr_ref, li_ref, d_ref, wglu_ref, bglu_ref, h0r_ref, h0i_ref,
                      o_ref, sr_ref, si_ref, xr, xi, *, nseq, t_new):
    nt, st = nseq // SAMPLE_TILE, SAMPLE_TILE
    n = nseq * t_new
    u = u_ref[...]
    ub = u.astype(BF16)
    for cc in range(S5_UCHUNKS):
        res = jnp.dot(ub[:, cc * LANES:(cc + 1) * LANES], wb_ref[cc], preferred_element_type=F32)
        sc = slice(cc * S5_SUB, (cc + 1) * S5_SUB)
        xr[:, :, :, sc] = res[:, :S5_SUB].reshape(nt, t_new, st, S5_SUB)
        xi[:, :, :, sc] = res[:, S5_SUB:].reshape(nt, t_new, st, S5_SUB)
    lr, li = lr_ref[...], li_ref[...]
    hr, hi = h0r_ref[...], h0i_ref[...]
    for t in range(t_new):
        nr = lr * hr - li * hi + xr[:, t].reshape(nseq, S5_FLAT)
        ni = lr * hi + li * hr + xi[:, t].reshape(nseq, S5_FLAT)
        xr[:, t] = nr.reshape(nt, st, S5_FLAT)
        xi[:, t] = ni.reshape(nt, st, S5_FLAT)
        hr, hi = nr, ni
    sr_ref[...] = hr
    si_ref[...] = hi
    ys = []
    for cc in range(S5_UCHUNKS):
        sc = slice(cc * S5_SUB, (cc + 1) * S5_SUB)
        s = jnp.concatenate([xr[:, :, :, sc].reshape(n, S5_SUB), xi[:, :, :, sc].reshape(n, S5_SUB)],
                            axis=1).astype(BF16)
        cols = slice(cc * LANES, (cc + 1) * LANES)
        ys.append(jnp.dot(s, wc_ref[cc], preferred_element_type=F32) + d_ref[:, cols] * u[:, cols])
    o_ref[...] = _s5_tail(jnp.concatenate(ys, axis=1), wglu_ref, bglu_ref).astype(BF16)


def _s5_sample(proj, h0r, h0i, prm, t_new, layer, prev):
    n = proj.shape[0]
    nseq = n // t_new
    names = ("wb", "wc", "lr", "li", "d", "wglu", "bglu")
    st = pl.BlockSpec((None, nseq, S5_FLAT), lambda i: (layer, 0, 0))
    prev, prev_specs, aliases = _alias_inputs(prev, 1)
    scratch = pltpu.VMEM((nseq // SAMPLE_TILE, t_new, SAMPLE_TILE, S5_FLAT), F32)
    return pl.pallas_call(
        _skip_refs(functools.partial(_s5_sample_kernel, nseq=nseq, t_new=t_new), len(prev)),
        grid=(1,),
        in_specs=prev_specs + [pl.BlockSpec((n, S5_WIDTH), lambda i: (0, EVEN_U_BLOCK))]
        + [_layer_spec(prm[k].shape[1:], layer) for k in names]
        + [_layer_spec((nseq, S5_FLAT), layer), _layer_spec((nseq, S5_FLAT), layer)],
        out_specs=[pl.BlockSpec((n, S5_WIDTH), lambda i: (0, 0)), st, st],
        out_shape=[jax.ShapeDtypeStruct((n, S5_WIDTH), BF16),
                   jax.ShapeDtypeStruct(h0r.shape, F32), jax.ShapeDtypeStruct(h0i.shape, F32)],
        input_output_aliases=aliases,
        scratch_shapes=[scratch, scratch],
        compiler_params=_cparams("arbitrary"),
        name="s5_sample",
    )(*prev, proj, *[prm[k] for k in names], h0r, h0i)


def _s5_params(a_re, a_im, log_dt, b_re, b_im, c_re, c_im, d_skip, w_glu, b_glu):
    nl = a_re.shape[0]
    dt = jnp.exp(log_dt)
    mag = jnp.exp(a_re * dt)
    lr, li = mag * jnp.cos(a_im * dt), mag * jnp.sin(a_im * dt)
    den = a_re * a_re + a_im * a_im
    cr = ((lr - 1.0) * a_re + li * a_im) / den
    ci = (li * a_re - (lr - 1.0) * a_im) / den
    bbr = cr[..., None] * b_re - ci[..., None] * b_im
    bbi = cr[..., None] * b_im + ci[..., None] * b_re
    gpc = LANES // S5_GROUP
    eye = jnp.eye(gpc, dtype=F32)

    def in_blocks(bb):
        bb = bb.reshape(nl, S5_UCHUNKS, gpc, S5_STATE, S5_GROUP)
        return jnp.einsum("lcgph,gk->lcghkp", bb, eye).reshape(nl, S5_UCHUNKS, LANES, S5_SUB)

    def out_blocks(cm):
        cm = cm.reshape(nl, S5_UCHUNKS, gpc, S5_GROUP, S5_STATE)
        return jnp.einsum("lcghp,gk->lcgpkh", cm, eye).reshape(nl, S5_UCHUNKS, S5_SUB, LANES)

    wb = jnp.concatenate([in_blocks(bbr), in_blocks(bbi)], axis=3).astype(BF16)
    wc = jnp.concatenate([out_blocks(c_re), -out_blocks(c_im)], axis=2).astype(BF16)
    lr_f, li_f = lr.reshape(nl, 1, S5_FLAT), li.reshape(nl, 1, S5_FLAT)
    lam = jnp.concatenate([lr_f.reshape(nl, S5_SCHUNKS, 1, LANES), li_f.reshape(nl, S5_SCHUNKS, 1, LANES)], axis=1)
    lam8 = jnp.broadcast_to(lam, (nl, 2 * S5_SCHUNKS, SUBLANES, LANES))
    return dict(wb=wb, wc=wc, lam8=lam8, lr=lr_f, li=li_f, d=d_skip.reshape(nl, 1, S5_WIDTH),
                wglu=w_glu.astype(BF16), bglu=b_glu.reshape(nl, 1, S5_WIDTH))


ML_AUG = 2 * ML_DV


EXP_CLAMP = 88.0


def _den_floor(m_row):
    return jnp.exp(jnp.minimum(-m_row, EXP_CLAMP))


def _head_out(h, o, gout):
    hn = h * lax.rsqrt(jnp.mean(h * h, axis=-1, keepdims=True) + EPS) * gout
    return (hn * _sigmoid(o)).astype(BF16)


ODDP_V_BLOCK = 0
ODDP_O_BLOCK = 1
ODDP_Q_BLOCK = (2 * ML_WIDTH) // ML_QK
ODDP_G_BLOCK = (2 * ML_WIDTH + ML_QK) // LANES
ML_SPLIT = 3
ML_PIECE_LANES = 2 * ML_HEADS
ML_SEQS = 8
ML_STAGE_LAG = 2


def _norm_matmul_kt_kernel(x_ref, g_ref, wt_ref, wg_ref, o_ref, kt_ref):
    h = _rms(x_ref[...], g_ref[...]).astype(BF16)
    k0, v0, g0 = ML_QK, 2 * ML_QK, 2 * ML_QK + 2 * ML_WIDTH
    nt = (((1,), (1,)), ((), ()))
    o_ref[:, :g0 - v0] = lax.dot_general(h, wt_ref[v0:g0, :].astype(BF16), nt, preferred_element_type=F32)
    o_ref[:, g0 - v0:g0 - v0 + k0] = lax.dot_general(h, wt_ref[:k0, :].astype(BF16), nt, preferred_element_type=F32)
    o_ref[:, g0 - v0 + k0:] = jnp.dot(h, wg_ref[...], preferred_element_type=F32)
    kt = lax.dot_general(wt_ref[k0:v0, :].astype(BF16), h, nt, preferred_element_type=F32)
    kt_ref[...] = kt * (ML_DK ** -0.5)


def _norm_matmul_kt(xs, g, layer, wt, wg, widx):
    d = wt.shape[2]
    m_out = 2 * ML_WIDTH + ML_QK + wg.shape[2]
    return _row_groups_call(
        _norm_matmul_kt_kernel, [[x] for x in xs], [g, wt, wg],
        [_layer_spec((1, d), layer), _layer_spec(wt.shape[1:], widx), _layer_spec(wg.shape[1:], widx)],
        [(m_out, F32, True), (ML_QK, F32, False)], [], "norm_matmul_kt")


def _cummax_rows(x):
    n = x.shape[0]
    row = lax.broadcasted_iota(jnp.int32, x.shape, 0)
    shift = 1
    while shift < n:
        x = jnp.maximum(x, jnp.where(row >= shift, pltpu.roll(x, shift, 0), NEG_INF))
        shift *= 2
    return x


def _pieces(x):
    lane = lax.broadcasted_iota(jnp.int32, x.shape, 1)
    xx = x + pltpu.roll(x, ML_PIECE_LANES, 1) + pltpu.roll(x, 2 * ML_PIECE_LANES, 1)
    a1, a2, a3 = _split3(xx)
    return jnp.where(lane < ML_PIECE_LANES, a1, jnp.where(lane < 2 * ML_PIECE_LANES, a2, a3))


def _ml_select_constants():
    mask = np.zeros((ML_HEADS, LANES), np.float32)
    sel = np.zeros((ML_HEADS, LANES, 2 * ML_DV), np.float32)
    for h in range(ML_HEADS):
        for k in range(ML_SPLIT):
            lo, hi = k * ML_PIECE_LANES + h, k * ML_PIECE_LANES + ML_HEADS + h
            mask[h, lo] = mask[h, hi] = 1.0
            sel[h, lo, :ML_DV] = 1.0
            sel[h, hi, ML_DV:] = 1.0
    return jnp.asarray(mask), jnp.asarray(sel, dtype=BF16)


def _mlstm_prompt_kernel(*refs, tc, nchunks, nseq):
    v_ref, o_ref, q_ref, g_ref = refs[:4]
    kt_refs = refs[4:4 + nseq]
    bias_ref, gout_ref, mask_ref, sel_ref, h_ref, c_ref, n_ref, m_ref, caug, mst = refs[4 + nseq:]
    ci = pl.program_id(1)

    @pl.when(ci == 0)
    def _():
        caug[...] = jnp.zeros_like(caug)
        mst[...] = jnp.zeros_like(mst)

    nh = ML_HEADS
    lane = lax.broadcasted_iota(jnp.int32, (tc, LANES), 1)
    lo, hi = lane < nh, (lane >= nh) & (lane < 2 * nh)
    rt = lax.broadcasted_iota(jnp.int32, (tc, tc), 0)
    cs = lax.broadcasted_iota(jnp.int32, (tc, tc), 1)
    causal = cs <= rt
    tril = jnp.where(causal, 1.0, 0.0).astype(BF16)
    ones = jnp.ones((tc, ML_DV), F32)

    def gates(sq):
        g = g_ref[sq] + bias_ref[...]
        lf = jnp.where(hi, _log_sigmoid(g), 0.0)
        b = sum(jnp.dot(tril, p, preferred_element_type=F32) for p in _split3(lf))
        c = jnp.where(hi, pltpu.roll(g, nh, 1) - b, 0.0)
        m_prev = mst[sq]
        mx = jnp.maximum(_cummax_rows(c), m_prev)
        m_row = b + mx
        mx_lo = pltpu.roll(mx, LANES - nh, 1)
        w_inter = jnp.exp(pltpu.roll(m_prev, LANES - nh, 1) - mx_lo)
        mst[sq] = m_row[tc - 1:tc, :]
        return dict(xc=_pieces(jnp.where(lo, w_inter, jnp.where(hi, _den_floor(m_row), 0.0))),
                    lc=_pieces(jnp.where(lo, -mx_lo, jnp.where(hi, 1.0, 0.0))),
                    rc=_pieces(jnp.where(lo, 1.0, jnp.where(hi, c, 0.0))))

    gt = [gates(sq) for sq in range(nseq)]
    units = [(sq, hd) for hd in range(nh) for sq in range(nseq)]
    st = [dict() for _ in units]

    def stage1(u):
        sq, hd = units[u]
        d = st[u]
        rh = gt[sq]["rc"] * mask_ref[hd:hd + 1, :].astype(BF16)
        d["dmat"] = lax.dot_general(gt[sq]["lc"], rh, (((1,), (1,)), ((), ())), preferred_element_type=F32)
        d["wb"] = jnp.dot(gt[sq]["xc"], sel_ref[hd], preferred_element_type=F32)
        d["qh"] = q_ref[sq, :, hd * ML_DK:(hd + 1) * ML_DK]
        d["kt"] = kt_refs[sq][hd * ML_DK:(hd + 1) * ML_DK, :]
        d["qk"] = jnp.dot(d["qh"].astype(BF16), d["kt"].astype(BF16), preferred_element_type=F32)

    def stage2(u):
        sq, hd = units[u]
        d = st[u]
        cols = slice(hd * ML_DV, (hd + 1) * ML_DV)
        d["w"] = jnp.exp(jnp.where(causal, d["dmat"], NEG_INF))
        d["vaug"] = jnp.concatenate([v_ref[sq, :, cols], ones], axis=1).astype(BF16)
        d["cm"] = caug[sq, hd]
        lhs = jnp.concatenate([(d["qk"] * d["w"]).astype(BF16), (d["wb"][:, :ML_DK] * d["qh"]).astype(BF16)], axis=1)
        rhs = jnp.concatenate([d["vaug"], d["cm"].astype(BF16)], axis=0)
        d["both"] = jnp.dot(lhs, rhs, preferred_element_type=F32)
        kw = (d["kt"] * d["w"][tc - 1:tc, :]).astype(BF16)
        d["upd"] = jnp.dot(kw, d["vaug"], preferred_element_type=F32)

    def stage3(u):
        sq, hd = units[u]
        d = st[u]
        cols = slice(hd * ML_DV, (hd + 1) * ML_DV)
        both, wb = d["both"], d["wb"]
        h = both[:, :ML_DV] / jnp.maximum(jnp.abs(both[:, ML_DV:]), wb[:, ML_DV:])
        h_ref[sq, :, cols] = _head_out(h, o_ref[sq, :, cols], gout_ref[:, cols])
        decay = wb[tc - 1:tc, :ML_DV]
        caug[sq, hd] = jnp.concatenate([decay, decay], axis=1) * d["cm"] + d["upd"]
        d.clear()

    for step in range(len(units) + 2 * ML_STAGE_LAG):
        if step < len(units):
            stage1(step)
        if 0 <= step - ML_STAGE_LAG < len(units):
            stage2(step - ML_STAGE_LAG)
        if 0 <= step - 2 * ML_STAGE_LAG < len(units):
            stage3(step - 2 * ML_STAGE_LAG)

    @pl.when(ci == nchunks - 1)
    def _():
        c_ref[...] = caug[:, :, :, :ML_DV]
        n_ref[...] = caug[:, :, :, ML_DV:]
        m_ref[...] = mst[...]


def _kt_index(b, c, *, sq, nseq, nchunks):
    return 0, (b * nseq + sq) * nchunks + c


def _mlstm_prompt(proj, kt, bias, gout, consts, layer, prev):
    bsz, seq, _ = proj.shape
    tc, nseq = ML_CHUNK, ML_SEQS
    nchunks = seq // tc
    n_layers = bias.shape[0]
    mask, sel = consts
    blk = lambda w, j: pl.BlockSpec((nseq, tc, w), lambda b, c: (b, c, j))
    st = lambda shape: pl.BlockSpec((None, nseq) + shape, lambda b, c: (layer, b) + (0,) * len(shape))
    st_shape = lambda shape: jax.ShapeDtypeStruct((n_layers, bsz) + shape, F32)
    prev, prev_specs, aliases = _alias_inputs(prev, 1)
    return pl.pallas_call(
        _skip_refs(functools.partial(_mlstm_prompt_kernel, tc=tc, nchunks=nchunks, nseq=nseq), len(prev)),
        grid=(bsz // nseq, nchunks),
        in_specs=prev_specs + [
            blk(ML_WIDTH, ODDP_V_BLOCK), blk(ML_WIDTH, ODDP_O_BLOCK), blk(ML_QK, ODDP_Q_BLOCK),
            blk(LANES, ODDP_G_BLOCK)]
        + [pl.BlockSpec((ML_QK, tc), functools.partial(_kt_index, sq=sq, nseq=nseq, nchunks=nchunks))
           for sq in range(nseq)] + [
            _layer_spec((1, LANES), layer), _layer_spec((1, ML_WIDTH), layer),
            _const_spec(mask.shape), _const_spec(sel.shape)],
        out_specs=[blk(ML_WIDTH, 0), st((ML_HEADS, ML_DK, ML_DV)), st((ML_HEADS, ML_DK, ML_DV)), st((1, LANES))],
        out_shape=[jax.ShapeDtypeStruct((bsz, seq, ML_WIDTH), BF16),
                   st_shape((ML_HEADS, ML_DK, ML_DV)), st_shape((ML_HEADS, ML_DK, ML_DV)), st_shape((1, LANES))],
        input_output_aliases=aliases,
        scratch_shapes=[pltpu.VMEM((nseq, ML_HEADS, ML_DK, ML_AUG), F32), pltpu.VMEM((nseq, 1, LANES), F32)],
        compiler_params=_cparams("parallel", "arbitrary"),
        name="mlstm_prompt",
    )(*prev, proj, proj, proj, proj, *([kt] * nseq), bias, gout, mask, sel)


MLS_SEQS = 32


def _mlstm_sample_kernel(v_ref, o_ref, q_ref, g_ref, kt_ref, bias_ref, gout_ref, mask_ref, sel_ref,
                         c0_ref, n0_ref, m0_ref, h_ref, c_ref, n_ref, m_ref, *, nseq, t_new):
    nh, nt, st = ML_HEADS, nseq // SAMPLE_TILE, SAMPLE_TILE
    rows = nseq * t_new
    lane = lax.broadcasted_iota(jnp.int32, (rows, LANES), 1)
    lo, hi = lane < nh, (lane >= nh) & (lane < 2 * nh)
    tiles = lambda a: a.reshape(nt, t_new, st, a.shape[-1])
    flat = lambda a: a.reshape(rows, a.shape[-1])
    per_seq = lambda a: a.reshape(nseq, a.shape[-1])

    g = g_ref[...] + bias_ref[...]
    lf = tiles(jnp.where(hi, _log_sigmoid(g), 0.0))
    ig = tiles(jnp.where(hi, pltpu.roll(g, nh, 1), 0.0))
    m_prev = m0_ref[...].reshape(nt, st, LANES)
    bs, cs, ms = [], [], []
    b_run, m_run = None, m_prev
    for t in range(t_new):
        b_run = lf[:, t] if b_run is None else b_run + lf[:, t]
        c_t = ig[:, t] - b_run
        m_run = jnp.maximum(m_run, c_t)
        bs.append(b_run)
        cs.append(c_t)
        ms.append(m_run)
    stack = lambda xs: flat(jnp.stack(xs, axis=1))
    b, c, mx = stack(bs), stack(cs), stack(ms)
    m_prev_rows = stack([m_prev] * t_new)
    m_row = b + mx
    m_ref[...] = per_seq(bs[-1] + ms[-1])
    mx_lo = pltpu.roll(mx, LANES - nh, 1)
    w_inter = jnp.exp(pltpu.roll(m_prev_rows, LANES - nh, 1) - mx_lo)
    xc = _pieces(jnp.where(lo, w_inter, jnp.where(hi, _den_floor(m_row), 0.0)))
    lc = _pieces(jnp.where(lo, -mx_lo, jnp.where(hi, 1.0, 0.0)))
    rc = _pieces(jnp.where(lo, 1.0, jnp.where(hi, c, 0.0)))

    def seq_of(idx):
        return (idx // (t_new * st)) * st + idx % st, (idx % (t_new * st)) // st

    rt = lax.broadcasted_iota(jnp.int32, (rows, rows), 0)
    ct = lax.broadcasted_iota(jnp.int32, (rows, rows), 1)
    (rs, rtok), (cseq, ctok) = seq_of(rt), seq_of(ct)
    valid = (rs == cseq) & (ctok <= rtok)
    rq = lax.broadcasted_iota(jnp.int32, (rows, nseq * ML_DK), 0)
    cq = lax.broadcasted_iota(jnp.int32, (rows, nseq * ML_DK), 1)
    own_q = seq_of(rq)[0] == cq // ML_DK
    rk = lax.broadcasted_iota(jnp.int32, (nseq * ML_DK, rows), 0)
    ck = lax.broadcasted_iota(jnp.int32, (nseq * ML_DK, rows), 1)
    own_k = rk // ML_DK == seq_of(ck)[0]
    ones = jnp.ones((rows, ML_DV), F32)
    last = lambda a: per_seq(tiles(a)[:, t_new - 1])

    for hd in range(nh):
        cols = slice(hd * ML_DV, (hd + 1) * ML_DV)
        rh = rc * mask_ref[hd:hd + 1, :].astype(BF16)
        dmat = lax.dot_general(lc, rh, (((1,), (1,)), ((), ())), preferred_element_type=F32)
        wb = jnp.dot(xc, sel_ref[hd], preferred_element_type=F32)
        qh = q_ref[:, hd * ML_DK:(hd + 1) * ML_DK]
        kt = kt_ref[hd * ML_DK:(hd + 1) * ML_DK, :]
        ktb = kt.astype(BF16)
        w = jnp.exp(jnp.where(valid, dmat, NEG_INF))
        qk = jnp.dot(qh.astype(BF16), ktb, preferred_element_type=F32) * w
        vaug = jnp.concatenate([v_ref[:, cols], ones], axis=1).astype(BF16)
        po = jnp.dot(qk.astype(BF16), vaug, preferred_element_type=F32)
        wq = wb[:, :ML_DK] * qh
        wq2 = jnp.concatenate([wq, wq], axis=1)
        wq_bd = jnp.where(own_q, jnp.concatenate([wq2] * (nseq * ML_DK // LANES), axis=1), 0.0).astype(BF16)
        cstack = c0_ref[:, hd].reshape(nseq * ML_DK, ML_DV)
        num = po[:, :ML_DV] + jnp.dot(wq_bd, cstack.astype(BF16), preferred_element_type=F32)
        n0 = n0_ref[hd]
        n_rows = stack([n0.reshape(nt, st, ML_DK)] * t_new)
        den = po[:, ML_DV:] + jnp.sum(wq * n_rows, axis=-1, keepdims=True)
        h = num / jnp.maximum(jnp.abs(den), wb[:, ML_DV:])
        h_ref[:, cols] = _head_out(h, o_ref[:, cols], gout_ref[:, cols])
        w_last = last(w)
        decay = last(wb[:, :ML_DV])
        n_upd = lax.dot_general(w_last.astype(BF16), ktb, (((1,), (1,)), ((), ())), preferred_element_type=F32)
        n_ref[hd] = decay[:, :ML_DK] * n0 + n_upd
        wk = jnp.sum(w_last, axis=0, keepdims=True)
        kw_bd = jnp.where(own_k, jnp.concatenate([kt * wk] * nseq, axis=0), 0.0).astype(BF16)
        upd = jnp.dot(kw_bd, v_ref[:, cols].astype(BF16), preferred_element_type=F32)
        decay_rows = jnp.broadcast_to(decay[:, None, :], (nseq, ML_DK, ML_DV)).reshape(nseq * ML_DK, ML_DV)
        c_ref[:, hd] = (decay_rows * cstack + upd).reshape(nseq, ML_DK, ML_DV)


def _mlstm_sample(proj, kt, bias, gout, consts, c0, n0h, m0, t_new, layer, prev):
    n = proj.shape[0]
    bsz = n // t_new
    nseq = MLS_SEQS
    rows = nseq * t_new
    mask, sel = consts
    blk = lambda w, j: pl.BlockSpec((rows, w), lambda i: (i, j))
    cst = pl.BlockSpec((None, nseq, ML_HEADS, ML_DK, ML_DV), lambda i: (layer, i, 0, 0, 0))
    nst = pl.BlockSpec((None, ML_HEADS, nseq, ML_DK), lambda i: (layer, 0, i, 0))
    mst = pl.BlockSpec((None, nseq, LANES), lambda i: (layer, i, 0))
    prev, prev_specs, aliases = _alias_inputs(prev, 1)
    return pl.pallas_call(
        _skip_refs(functools.partial(_mlstm_sample_kernel, nseq=nseq, t_new=t_new), len(prev)),
        grid=(bsz // nseq,),
        in_specs=prev_specs + [
            blk(ML_WIDTH, ODDP_V_BLOCK), blk(ML_WIDTH, ODDP_O_BLOCK), blk(ML_QK, ODDP_Q_BLOCK),
            blk(LANES, ODDP_G_BLOCK), pl.BlockSpec((ML_QK, rows), lambda i: (0, i)),
            _layer_spec((1, LANES), layer), _layer_spec((1, ML_WIDTH), layer),
            _const_spec(mask.shape), _const_spec(sel.shape), cst, nst, mst],
        out_specs=[blk(ML_WIDTH, 0), cst, nst, mst],
        out_shape=[jax.ShapeDtypeStruct((n, ML_WIDTH), BF16),
                   jax.ShapeDtypeStruct(c0.shape, F32), jax.ShapeDtypeStruct(n0h.shape, F32),
                   jax.ShapeDtypeStruct(m0.shape, F32)],
        input_output_aliases=aliases,
        compiler_params=_cparams("parallel"),
        name="mlstm_sample",
    )(*prev, proj, proj, proj, proj, kt, bias, gout, mask, sel, c0, n0h, m0)


def _pad_lanes(x):
    return jnp.pad(x, [(0, 0)] * (x.ndim - 1) + [(0, LANES - x.shape[-1])])


def kernel(x_prompt, x_sample, cache_k, cache_v, state_ssm_re, state_ssm_im, state_mlstm_c, state_mlstm_n, state_mlstm_m, norm_mix, norm_ffn, w_in_even, q_norm, k_norm, attn_sinks, s5_a_re, s5_a_im, s5_log_dt, s5_b_re, s5_b_im, s5_c_re, s5_c_im, s5_d, s5_w_glu, s5_b_glu, w_out_even, w_in_odd, ml_b_i, ml_b_f, ml_out_norm, w_out_odd, w_gate, w_up, w_down):
    bp, lp, _ = x_prompt.shape
    bsm, ls, _ = x_sample.shape
    yp = x_prompt.reshape(bp * lp, D_MODEL)
    ys = x_sample.reshape(bsm // SAMPLE_TILE, SAMPLE_TILE, ls, D_MODEL).transpose(0, 2, 1, 3).reshape(bsm * ls, D_MODEL)
    tab_p = _rope_tables(jnp.arange(lp))
    tab_s = tuple(jnp.repeat(t, SAMPLE_TILE, axis=0) for t in _rope_tables(PAST_LEN + jnp.arange(ls)))
    n_even, n_odd = w_in_even.shape[0], w_in_odd.shape[0]

    g_mix = norm_mix.reshape(DEPTH, 1, D_MODEL)
    g_ffn = norm_ffn.reshape(DEPTH, 1, D_MODEL)
    wg, wu, wd = w_gate.astype(BF16), w_up.astype(BF16), w_down.astype(BF16)
    kv0, u0 = ATTN_WIDTH, ATTN_WIDTH + 2 * KV_WIDTH
    order = jnp.asarray(ATTN_HEAD_ORDER)
    wq = w_in_even[..., :kv0].reshape(n_even, D_MODEL, ATTN_HEADS, HEAD_DIM)[:, :, order].reshape(n_even, D_MODEL, kv0)
    w_in_e = jnp.concatenate([wq, w_in_even[..., u0:], w_in_even[..., kv0:u0]], axis=-1).astype(BF16)
    wo_attn = w_out_even[:, :kv0].reshape(n_even, ATTN_HEADS, HEAD_DIM, D_MODEL)[:, order].reshape(n_even, kv0, D_MODEL)
    w_out_e = jnp.concatenate([wo_attn, w_out_even[:, kv0:]], axis=1).astype(BF16)
    gq = jnp.tile(q_norm, (1, LANES // HEAD_DIM)).reshape(n_even, 1, LANES)
    gk = jnp.tile(k_norm, (1, LANES // HEAD_DIM)).reshape(n_even, 1, LANES)
    prm = _s5_params(s5_a_re, s5_a_im, s5_log_dt, s5_b_re, s5_b_im, s5_c_re, s5_c_im, s5_d, s5_w_glu, s5_b_glu)
    w_gates_o = _pad_lanes(w_in_odd[..., 2 * ML_QK + 2 * ML_WIDTH:]).astype(BF16)
    w_in_ot = jnp.swapaxes(w_in_odd, 1, 2)
    ml_consts = _ml_select_constants()
    w_out_o = w_out_odd.astype(BF16)
    ml_bias = _pad_lanes(jnp.concatenate([ml_b_i, ml_b_f], axis=-1)).reshape(n_odd, 1, LANES)
    ml_gout = ml_out_norm.reshape(n_odd, 1, ML_WIDTH)
    keys_last = lambda a: a.transpose(0, 1, 3, 4, 2).reshape(n_even, bsm, KV_WIDTH, WINDOW)
    ck, cv = keys_last(cache_k), keys_last(cache_v)
    h0r = state_ssm_re.reshape(n_even, bsm, S5_FLAT)
    h0i = state_ssm_im.reshape(n_even, bsm, S5_FLAT)
    n0h = jnp.swapaxes(state_mlstm_n, 1, 2)
    m0 = jnp.pad(state_mlstm_m, ((0, 0), (0, 0), (ML_HEADS, LANES - 2 * ML_HEADS)))

    p_attn = p_ssm = p_ml = s_attn = s_ssm = s_ml = None
    for layer in range(DEPTH):
        ffn = (layer, g_ffn, wg, wu, wd)
        if layer % 2 == 0:
            e = layer // 2
            proj_p, proj_s = _norm_matmul([yp, ys], g_mix, layer, w_in_e, e)
            proj3 = proj_p.reshape(bp, lp, -1)
            attn_p, *p_attn = _attn_prompt(proj3, tab_p, gq, gk, attn_sinks, e, p_attn)
            ssm_p, *p_ssm = _s5_prompt(proj3, prm, e, p_ssm)
            attn_s, *s_attn = _attn_sample(proj_s, ck, cv, tab_s, gq, gk, attn_sinks, ls, e, s_attn)
            ssm_s, *s_ssm = _s5_sample(proj_s, h0r, h0i, prm, ls, e, s_ssm)
            yp, ys = _mix_ffn([[yp, attn_p.reshape(bp * lp, -1), ssm_p.reshape(bp * lp, -1)], [ys, attn_s, ssm_s]],
                              w_out_e, e, *ffn)
        else:
            o = layer // 2
            (proj_p, kt_p), (proj_s, kt_s) = _norm_matmul_kt([yp, ys], g_mix, layer, w_in_ot, w_gates_o, o)
            hh_p, *p_ml = _mlstm_prompt(proj_p.reshape(bp, lp, -1), kt_p, ml_bias, ml_gout, ml_consts, o, p_ml)
            hh_s, *s_ml = _mlstm_sample(proj_s, kt_s, ml_bias, ml_gout, ml_consts, state_mlstm_c, n0h, m0, ls, o,
                                        s_ml)
            yp, ys = _mix_ffn([[yp, hh_p.reshape(bp * lp, -1)], [ys, hh_s]], w_out_o, o, *ffn)
    heads = lambda a: a.reshape(a.shape[:3] + (KV_HEADS, HEAD_DIM))
    groups = lambda a: a.reshape(a.shape[:2] + (S5_GROUPS, S5_STATE))
    keys_first = lambda a: a.reshape(a.shape[:2] + (KV_HEADS, HEAD_DIM, WINDOW)).transpose(0, 1, 4, 2, 3)
    ys = ys.reshape(bsm // SAMPLE_TILE, ls, SAMPLE_TILE, D_MODEL).transpose(0, 2, 1, 3).reshape(bsm, ls, D_MODEL)
    return (yp.reshape(bp, lp, D_MODEL), ys,
            heads(p_attn[0]), heads(p_attn[1]), groups(p_ssm[0]), groups(p_ssm[1]),
            p_ml[0], p_ml[1][..., 0], p_ml[2][:, :, 0, ML_HEADS:2 * ML_HEADS],
            keys_first(s_attn[0]), keys_first(s_attn[1]), groups(s_ssm[0]), groups(s_ssm[1]),
            s_ml[0], jnp.swapaxes(s_ml[1], 1, 2), s_ml[2][..., ML_HEADS:2 * ML_HEADS])
```
